```python
import math
import jax
import jax.numpy as jnp
from jax import lax
import numpy as np

D_MODEL = 2048
BATCH = 4
SEQ = 4096
DEPTH = 1
DEC_BATCH = 16
DEC_SEQ = 32
PAST_LEN = 1024

CHUNK = 64
HEAD_DIM = 64
N_Q_HEADS = 16
N_KV_HEADS = 4
Q_PER_KV = N_Q_HEADS // N_KV_HEADS
ATTN_DIM = N_Q_HEADS * HEAD_DIM
KV_DIM = N_KV_HEADS * HEAD_DIM
WINDOW = 128
WIN_CHUNKS = WINDOW // CHUNK
BAND = (WIN_CHUNKS + 1) * CHUNK
CONV_DIM = D_MODEL // 2
CONV_WIDTH = 31
NUM_BUCKETS = 32
MAX_DISTANCE = 128
N_GROUPS = 4
EXPERTS_PER_GROUP = 8
N_EXPERTS = N_GROUPS * EXPERTS_PER_GROUP
D_EXPERT = 512
TOP_K_INNER = 2
MOE_BLOCK = 128
RMS_EPS = 1e-6
LN_EPS = 1e-5
NEG_INF = -1e30

K_OFF = ATTN_DIM
V_OFF = K_OFF + KV_DIM
GLU_OFF = V_OFF + KV_DIM
GATE_OFF = GLU_OFF + 2 * CONV_DIM
IN_COLS = GATE_OFF + 2 * D_MODEL

kernel_name = 'chunk_streaming_hybrid_swa_conformer_hmoe'


def rms_norm(x, g):
    xf = x.astype(jnp.float32)
    y = xf * lax.rsqrt(jnp.mean(xf * xf, axis=-1, keepdims=True) + RMS_EPS)
    return (y * g.astype(jnp.float32)).astype(x.dtype)


def layer_norm(x, g, b):
    xf = x.astype(jnp.float32)
    mu = jnp.mean(xf, axis=-1, keepdims=True)
    var = jnp.mean(jnp.square(xf - mu), axis=-1, keepdims=True)
    y = (xf - mu) * lax.rsqrt(var + LN_EPS) * g.astype(jnp.float32) + b.astype(jnp.float32)
    return y.astype(x.dtype)


def ada(c, w_ada, b_ada):
    m = jax.nn.silu(c) @ w_ada + b_ada
    return [t[:, None, :] for t in jnp.split(m, 6, axis=-1)]


def t5_bucket(rel):
    nb = NUM_BUCKETS // 2
    n = -rel
    ret = jnp.where(n < 0, nb, 0)
    n = jnp.abs(n)
    max_exact = nb // 2
    nf = jnp.maximum(n, 1).astype(jnp.float32)
    large = max_exact + (jnp.log(nf / max_exact) / math.log(MAX_DISTANCE / max_exact) * (nb - max_exact)).astype(jnp.int32)
    large = jnp.minimum(large, nb - 1)
    return ret + jnp.where(n < max_exact, n, large)


def rel_bias(rel_table, rel):
    b = rel_table.astype(jnp.float32)[t5_bucket(rel)]
    return jnp.transpose(b, (2, 0, 1)).reshape(N_KV_HEADS, Q_PER_KV, rel.shape[0], rel.shape[1])


def sink_softmax(logits, sink):
    s = sink.astype(jnp.float32).reshape(N_KV_HEADS, Q_PER_KV, 1, 1)
    m = jnp.maximum(jnp.max(logits, axis=-1, keepdims=True), s)
    p = jnp.exp(logits - m)
    return p / (jnp.sum(p, axis=-1, keepdims=True) + jnp.exp(s - m))


def attn_prompt(q, k, v, rel_table, sink):
    B, S, _ = q.shape
    nc = S // CHUNK
    q = q.reshape(B, nc, CHUNK, N_KV_HEADS, Q_PER_KV, HEAD_DIM)

    def band(t):
        t = t.reshape(B, nc, CHUNK, N_KV_HEADS, HEAD_DIM)
        t = jnp.pad(t, ((0, 0), (WIN_CHUNKS, 0), (0, 0), (0, 0), (0, 0)))
        return jnp.concatenate([t[:, i:i + nc] for i in range(WIN_CHUNKS + 1)], axis=2)

    kb = band(k)
    vb = band(v)
    logits = jnp.einsum('bnqkgd,bnskd->bnkgqs', q, kb, preferred_element_type=jnp.float32) * (HEAD_DIM ** -0.5)
    qi = jnp.arange(CHUNK, dtype=jnp.int32)
    kj = jnp.arange(BAND, dtype=jnp.int32)
    rel = kj[None, :] - WIN_CHUNKS * CHUNK - qi[:, None]
    logits = logits + rel_bias(rel_table, rel)
    key_chunk = jnp.arange(nc, dtype=jnp.int32)[:, None] - WIN_CHUNKS + kj[None, :] // CHUNK
    valid = key_chunk >= 0
    logits = jnp.where(valid[None, :, None, None, None, :], logits, NEG_INF)
    p = sink_softmax(logits, sink)
    o = jnp.einsum('bnkgqs,bnskd->bnqkgd', p.astype(vb.dtype), vb)
    return o.reshape(B, S, ATTN_DIM)


def attn_sample(q, k_new, v_new, k_cache, v_cache, rel_table, sink):
    DB, T, _ = q.shape
    buf = k_cache.shape[1]
    q = q.reshape(DB, T, N_KV_HEADS, Q_PER_KV, HEAD_DIM)
    k = jnp.concatenate([k_cache, k_new.reshape(DB, T, N_KV_HEADS, HEAD_DIM)], axis=1)
    v = jnp.concatenate([v_cache, v_new.reshape(DB, T, N_KV_HEADS, HEAD_DIM)], axis=1)
    logits = jnp.einsum('btkgd,bskd->bkgts', q, k, preferred_element_type=jnp.float32) * (HEAD_DIM ** -0.5)
    kpos = jnp.concatenate([jnp.arange(buf, dtype=jnp.int32) - buf, jnp.arange(T, dtype=jnp.int32)])
    rel = kpos[None, :] - jnp.arange(T, dtype=jnp.int32)[:, None]
    logits = logits + rel_bias(rel_table, rel)
    p = sink_softmax(logits, sink)
    o = jnp.einsum('bkgts,bskd->btkgd', p.astype(v.dtype), v)
    return o.reshape(DB, T, ATTN_DIM), k[:, -buf:], v[:, -buf:]


def dwconv(u_ext, w, b):
    out = lax.conv_general_dilated(u_ext, w[:, None, :], (1,), 'VALID',
                                   dimension_numbers=('NWC', 'WIO', 'NWC'),
                                   feature_group_count=CONV_DIM)
    return out + b


def token_mixer(h, w_in, attn_sink, w_attn_o, conv_w, conv_b, conv_ln_g, conv_ln_b,
                w_pw2, b_pw2, w_out, rel_table, cache):
    z = h @ w_in
    q = z[..., :K_OFF]
    k = z[..., K_OFF:V_OFF]
    v = z[..., V_OFF:GLU_OFF]
    u = z[..., GLU_OFF:GLU_OFF + CONV_DIM] * jax.nn.sigmoid(z[..., GLU_OFF + CONV_DIM:GATE_OFF])
    g_attn = jax.nn.sigmoid(z[..., GATE_OFF:GATE_OFF + D_MODEL])
    g_conv = jax.nn.sigmoid(z[..., GATE_OFF + D_MODEL:])
    if cache is None:
        B, S = h.shape[0], h.shape[1]
        o = attn_prompt(q, k, v, rel_table, attn_sink)
        k_st = k.reshape(B, S, N_KV_HEADS, HEAD_DIM)[:, -WINDOW:]
        v_st = v.reshape(B, S, N_KV_HEADS, HEAD_DIM)[:, -WINDOW:]
        u_ext = jnp.pad(u, ((0, 0), (CONV_WIDTH - 1, 0), (0, 0)))
    else:
        k_cache, v_cache, conv_cache = cache
        o, k_st, v_st = attn_sample(q, k, v, k_cache, v_cache, rel_table, attn_sink)
        u_ext = jnp.concatenate([conv_cache, u], axis=1)
    conv_st = u_ext[:, -(CONV_WIDTH - 1):]
    d = dwconv(u_ext, conv_w, conv_b)
    conv_out = jax.nn.silu(layer_norm(d, conv_ln_g, conv_ln_b)) @ w_pw2 + b_pw2
    merged = g_attn * (o @ w_attn_o) + g_conv * conv_out
    return merged @ w_out, (k_st, v_st, conv_st)


def routed_experts(x, e_idx, w_tok, w_gate, w_up, w_down):
    N = x.shape[0]
    A = N * TOP_K_INNER
    flat_e = e_idx.reshape(-1).astype(jnp.int32)
    flat_t = jnp.repeat(jnp.arange(N, dtype=jnp.int32), TOP_K_INNER)
    flat_w = w_tok.reshape(-1)
    order = jnp.argsort(flat_e)
    se = flat_e[order]
    st = flat_t[order]
    sw = flat_w[order]
    counts = jnp.bincount(flat_e, length=N_EXPERTS).astype(jnp.int32)
    padded = (counts + MOE_BLOCK - 1) // MOE_BLOCK * MOE_BLOCK
    pad_end = jnp.cumsum(padded)
    pad_start = pad_end - padded
    seg_start = jnp.cumsum(counts) - counts
    dest = pad_start[se] + jnp.arange(A, dtype=jnp.int32) - seg_start[se]
    n_blocks = -(-(A + N_EXPERTS * (MOE_BLOCK - 1)) // MOE_BLOCK)
    n_rows = n_blocks * MOE_BLOCK
    row_tok = jnp.full((n_rows,), N, jnp.int32).at[dest].set(st)
    x_pad = jnp.concatenate([x, jnp.zeros((1, x.shape[1]), x.dtype)], axis=0)
    xb = x_pad[row_tok].reshape(n_blocks, MOE_BLOCK, x.shape[1])
    blk_e = jnp.minimum(jnp.searchsorted(pad_end, jnp.arange(n_blocks, dtype=jnp.int32) * MOE_BLOCK, side='right'),
                        N_EXPERTS - 1)

    def expert_block(args):
        xe, e = args
        return (jax.nn.silu(xe @ w_gate[e]) * (xe @ w_up[e])) @ w_down[e]

    yb = lax.map(expert_block, (xb, blk_e)).reshape(n_rows, x.shape[1])
    contrib = yb[dest] * sw[:, None]
    return jnp.zeros_like(x).at[st].add(contrib)


def hier_moe(h, w_group, b_group, w_router, b_router, w_gate, w_up, w_down):
    lead = h.shape[:-1]
    x = h.reshape(-1, D_MODEL)
    N = x.shape[0]
    rows = jnp.arange(N)
    gl = (x @ w_group).astype(jnp.float32) + b_group.astype(jnp.float32)
    p_grp = jax.nn.softmax(gl, axis=-1)
    g_sel = jnp.argmax(gl, axis=-1).astype(jnp.int32)
    el = ((x @ w_router).astype(jnp.float32) + b_router.astype(jnp.float32)).reshape(N, N_GROUPS, EXPERTS_PER_GROUP)
    p_in = jax.nn.softmax(el[rows, g_sel], axis=-1)
    top_p, top_i = lax.top_k(p_in, TOP_K_INNER)
    w_tok = p_grp[rows, g_sel][:, None] * top_p / jnp.sum(top_p, axis=-1, keepdims=True)
    e_idx = g_sel[:, None] * EXPERTS_PER_GROUP + top_i
    y = routed_experts(x, e_idx, w_tok.astype(x.dtype), w_gate, w_up, w_down)
    return y.reshape(*lead, D_MODEL)


def trunk_layer(x, c, lp, rel_table, cache):
    (n1, n2, wa, ba, w_in, sink, w_ao, cw, cb, lng, lnb, w_pw2, b_pw2, w_out,
     w_grp, b_grp, w_rt, b_rt, w_g, w_u, w_d) = lp
    shift1, scale1, gate1, shift2, scale2, gate2 = ada(c, wa, ba)
    h = rms_norm(x, n1) * (1 + scale1) + shift1
    mix, state = token_mixer(h, w_in, sink, w_ao, cw, cb, lng, lnb, w_pw2, b_pw2, w_out, rel_table, cache)
    x = x + gate1 * mix
    h = rms_norm(x, n2) * (1 + scale2) + shift2
    x = x + gate2 * hier_moe(h, w_grp, b_grp, w_rt, b_rt, w_g, w_u, w_d)
    return x, state


def setup_inputs(seed: int = 0) -> dict:
    key = jax.random.key(seed)
    ks = jax.random.split(key, 30)
    f32 = jnp.float32

    def nrm(k, shape, scale):
        return jax.random.normal(k, shape, f32) * scale

    kv_buf = min(WINDOW, PAST_LEN)
    L = DEPTH
    return {
        'x_prompt': nrm(ks[0], (BATCH, SEQ, D_MODEL), 1.0),
        'x_sample': nrm(ks[1], (DEC_BATCH, DEC_SEQ, D_MODEL), 1.0),
        'c_prompt': nrm(ks[2], (BATCH, D_MODEL), 1.0),
        'c_sample': nrm(ks[3], (DEC_BATCH, D_MODEL), 1.0),
        'cache_k': nrm(ks[4], (L, DEC_BATCH, kv_buf, N_KV_HEADS, HEAD_DIM), 1.0),
        'cache_v': nrm(ks[5], (L, DEC_BATCH, kv_buf, N_KV_HEADS, HEAD_DIM), 1.0),
        'state_conv': nrm(ks[6], (L, DEC_BATCH, CONV_WIDTH - 1, CONV_DIM), 0.5),
        'rel_bias_table': nrm(ks[7], (NUM_BUCKETS, N_Q_HEADS), 0.2),
        'norm1_g': 1.0 + nrm(ks[8], (L, D_MODEL), 0.02),
        'norm2_g': 1.0 + nrm(ks[9], (L, D_MODEL), 0.02),
        'w_ada': nrm(ks[10], (L, D_MODEL, 6 * D_MODEL), 0.5 * D_MODEL ** -0.5),
        'b_ada': nrm(ks[11], (L, 6 * D_MODEL), 0.02),
        'w_in': nrm(ks[12], (L, D_MODEL, IN_COLS), D_MODEL ** -0.5),
        'attn_sink': nrm(ks[13], (L, N_Q_HEADS), 0.5),
        'w_attn_o': nrm(ks[14], (L, ATTN_DIM, D_MODEL), ATTN_DIM ** -0.5),
        'conv_w': nrm(ks[15], (L, CONV_WIDTH, CONV_DIM), CONV_WIDTH ** -0.5),
        'conv_b': nrm(ks[16], (L, CONV_DIM), 0.02),
        'conv_ln_g': 1.0 + nrm(ks[17], (L, CONV_DIM), 0.02),
        'conv_ln_b': nrm(ks[18], (L, CONV_DIM), 0.02),
        'w_pw2': nrm(ks[19], (L, CONV_DIM, D_MODEL), CONV_DIM ** -0.5),
        'b_pw2': nrm(ks[20], (L, D_MODEL), 0.02),
        'w_out': nrm(ks[21], (L, D_MODEL, D_MODEL), D_MODEL ** -0.5),
        'w_group': nrm(ks[22], (L, D_MODEL, N_GROUPS), D_MODEL ** -0.5),
        'b_group': nrm(ks[23], (L, N_GROUPS), 0.01),
        'w_router': nrm(ks[24], (L, D_MODEL, N_EXPERTS), D_MODEL ** -0.5),
        'b_router': nrm(ks[25], (L, N_EXPERTS), 0.01),
        'w_gate': nrm(ks[26], (L, N_EXPERTS, D_MODEL, D_EXPERT), D_MODEL ** -0.5),
        'w_up': nrm(ks[27], (L, N_EXPERTS, D_MODEL, D_EXPERT), D_MODEL ** -0.5),
        'w_down': nrm(ks[28], (L, N_EXPERTS, D_EXPERT, D_MODEL), D_EXPERT ** -0.5),
        'final_g': 1.0 + nrm(ks[29], (D_MODEL,), 0.02),
    }


def reference(x_prompt, x_sample, c_prompt, c_sample, cache_k, cache_v, state_conv, rel_bias_table,
              norm1_g, norm2_g, w_ada, b_ada, w_in, attn_sink, w_attn_o, conv_w, conv_b, conv_ln_g, conv_ln_b,
              w_pw2, b_pw2, w_out, w_group, b_group, w_router, b_router, w_gate, w_up, w_down, final_g):
    xp = x_prompt
    xs = x_sample
    kp, vp, cp, ksm, vsm, csm = [], [], [], [], [], []
    for l in range(DEPTH):
        lp = (norm1_g[l], norm2_g[l], w_ada[l], b_ada[l], w_in[l], attn_sink[l], w_attn_o[l], conv_w[l],
              conv_b[l], conv_ln_g[l], conv_ln_b[l], w_pw2[l], b_pw2[l], w_out[l], w_group[l], b_group[l],
              w_router[l], b_router[l], w_gate[l], w_up[l], w_down[l])
        xp, sp = trunk_layer(xp, c_prompt, lp, rel_bias_table, None)
        xs, ss = trunk_layer(xs, c_sample, lp, rel_bias_table, (cache_k[l], cache_v[l], state_conv[l]))
        kp.append(sp[0]); vp.append(sp[1]); cp.append(sp[2])
        ksm.append(ss[0]); vsm.append(ss[1]); csm.append(ss[2])
    y_prompt = rms_norm(xp, final_g)
    y_sample = rms_norm(xs, final_g)
    new_k_prompt = jnp.stack(kp)
    new_v_prompt = jnp.stack(vp)
    new_conv_prompt = jnp.stack(cp)
    new_k_sample = jnp.stack(ksm)
    new_v_sample = jnp.stack(vsm)
    new_conv_sample = jnp.stack(csm)
    return (y_prompt, y_sample, new_k_prompt, new_v_prompt, new_conv_prompt, new_k_sample, new_v_sample, new_conv_sample)
```

```python
import functools
import math

import jax
import jax.numpy as jnp
from jax import lax
from jax.experimental import pallas as pl
from jax.experimental.pallas import tpu as pltpu

F32 = jnp.float32
BF16 = jnp.bfloat16

CHUNK = 64
HEAD_DIM = 64
N_Q_HEADS = 16
N_KV_HEADS = 4
Q_PER_KV = N_Q_HEADS // N_KV_HEADS
ATTN_DIM = N_Q_HEADS * HEAD_DIM
KV_DIM = N_KV_HEADS * HEAD_DIM
WINDOW = 128
WIN_CHUNKS = WINDOW // CHUNK
BAND = (WIN_CHUNKS + 1) * CHUNK
CONV_WIDTH = 31
NUM_BUCKETS = 32
MAX_DISTANCE = 128
N_GROUPS = 4
EXPERTS_PER_GROUP = 8
N_EXPERTS = N_GROUPS * EXPERTS_PER_GROUP
MOE_BLOCK = 128
RMS_EPS = 1e-6
LN_EPS = 1e-5
NEG_INF = -1e30

GROUP_ROWS = 32
LANES = 128
ROUTER_ROWS = 64
EXPERT_ROW0 = 8
V7X_VMEM_LIMIT = 56 * 1024 * 1024


def _cparams(*sem):
    return pltpu.CompilerParams(dimension_semantics=sem, vmem_limit_bytes=V7X_VMEM_LIMIT)


def _ada_kernel(c_ref, w_ref, b_ref, o_ref):
    c = c_ref[...]
    s = c * jax.nn.sigmoid(c)
    o_ref[...] = jnp.dot(s.astype(BF16), w_ref[...].astype(BF16),
                         preferred_element_type=F32) + b_ref[...]


def _ada(c_all, w_ada, b_ada):
    rows, d = c_all.shape
    ncol = w_ada.shape[1]
    tn = 1024
    return pl.pallas_call(
        _ada_kernel,
        grid=(ncol // tn,),
        in_specs=[pl.BlockSpec((rows, d), lambda j: (0, 0)),
                  pl.BlockSpec((d, tn), lambda j: (0, j)),
                  pl.BlockSpec((1, tn), lambda j: (0, j))],
        out_specs=pl.BlockSpec((rows, tn), lambda j: (0, j)),
        out_shape=jax.ShapeDtypeStruct((rows, ncol), F32),
        compiler_params=_cparams("arbitrary"),
        name="ada",
    )(c_all, w_ada, b_ada.reshape(1, ncol))


def _rms_mod(x, gain, scale, shift):
    ms = jnp.mean(x * x, axis=-1, keepdims=True)
    y = x * lax.rsqrt(ms + RMS_EPS) * gain
    return y * (1.0 + scale) + shift


def _norm1_kernel(x_ref, g_ref, sc_ref, sh_ref, h_ref, *, groups):
    def body(gi, carry):
        r = pl.multiple_of(gi * GROUP_ROWS, GROUP_ROWS)
        h = _rms_mod(x_ref[pl.ds(r, GROUP_ROWS), :], g_ref[...],
                     sc_ref[pl.ds(gi, 1), :], sh_ref[pl.ds(gi, 1), :])
        h_ref[pl.ds(r, GROUP_ROWS), :] = h.astype(h_ref.dtype)
        return carry

    lax.fori_loop(0, groups, body, 0)


def _norm1(x, gain, scale_g, shift_g, tm):
    n, d = x.shape
    groups = tm // GROUP_ROWS
    return pl.pallas_call(
        functools.partial(_norm1_kernel, groups=groups),
        grid=(n // tm,),
        in_specs=[pl.BlockSpec((tm, d), lambda i: (i, 0)),
                  pl.BlockSpec((1, d), lambda i: (0, 0)),
                  pl.BlockSpec((groups, d), lambda i: (i, 0)),
                  pl.BlockSpec((groups, d), lambda i: (i, 0))],
        out_specs=pl.BlockSpec((tm, d), lambda i: (i, 0)),
        out_shape=jax.ShapeDtypeStruct((n, d), BF16),
        compiler_params=_cparams("arbitrary"),
        name="norm1",
    )(x, gain.reshape(1, d), scale_g, shift_g)


def _proj_kernel(h_ref, w_ref, *o_refs, mode):
    acc = jnp.dot(h_ref[...], w_ref[...], preferred_element_type=F32)
    if mode == "q":
        o_refs[0][...] = acc.astype(BF16)
    elif mode == "kv":
        o_refs[0][...] = acc.astype(BF16)
        o_refs[1][...] = acc
    elif mode == "glu":
        half = acc.shape[1] // 2
        o_refs[0][...] = acc[:, :half] * jax.nn.sigmoid(acc[:, half:])
    else:
        o_refs[0][...] = jax.nn.sigmoid(acc).astype(BF16)


def _proj(h, w, mode, tm, tnw, out_dtypes):
    n, d = h.shape
    ncol = w.shape[1]
    tno = tnw // 2 if mode == "glu" else tnw
    nout = ncol // 2 if mode == "glu" else ncol
    outs = pl.pallas_call(
        functools.partial(_proj_kernel, mode=mode),
        grid=(n // tm, ncol // tnw),
        in_specs=[pl.BlockSpec((tm, d), lambda i, j: (i, 0)),
                  pl.BlockSpec((d, tnw), lambda i, j: (0, j))],
        out_specs=[pl.BlockSpec((tm, tno), lambda i, j: (i, j)) for _ in out_dtypes],
        out_shape=[jax.ShapeDtypeStruct((n, nout), dt) for dt in out_dtypes],
        compiler_params=_cparams("arbitrary", "arbitrary"),
        name="proj_" + mode,
    )(h, w)
    return outs


def _t5_bucket(rel):
    nb = NUM_BUCKETS // 2
    n = -rel
    ret = jnp.where(n < 0, nb, 0)
    n = jnp.abs(n)
    max_exact = nb // 2
    nf = jnp.maximum(n, 1).astype(F32)
    large = max_exact + (jnp.log(nf / max_exact) / math.log(MAX_DISTANCE / max_exact)
                         * (nb - max_exact)).astype(jnp.int32)
    large = jnp.minimum(large, nb - 1)
    return ret + jnp.where(n < max_exact, n, large)


def _bias_kernel(tbl_ref, bkt_ref, o_ref):
    bkt = bkt_ref[...]
    nq = bkt.shape[0]
    for k in range(N_KV_HEADS):
        for g in range(Q_PER_KV):
            acc = jnp.zeros(bkt.shape, F32)
            for b in range(NUM_BUCKETS):
                acc = jnp.where(bkt == b, tbl_ref[b, k * Q_PER_KV + g], acc)
            o_ref[k, g * nq:(g + 1) * nq, :] = acc


def _rel_bias(rel_table, rel):
    nq, nk = rel.shape
    return pl.pallas_call(
        _bias_kernel,
        in_specs=[pl.BlockSpec(memory_space=pltpu.SMEM),
                  pl.BlockSpec((nq, nk), lambda: (0, 0))],
        out_specs=pl.BlockSpec((N_KV_HEADS, Q_PER_KV * nq, nk), lambda: (0, 0, 0)),
        out_shape=jax.ShapeDtypeStruct((N_KV_HEADS, Q_PER_KV * nq, nk), F32),
        name="rel_bias",
    )(rel_table.astype(F32), _t5_bucket(rel))


def _attend(qc, kk_all, vv_all, bias_ref, sink_ref, mask_thr):
    nq = qc.shape[0]
    nk = kk_all.shape[0]
    pieces = []
    for k in range(N_KV_HEADS):
        qs = jnp.concatenate(
            [qc[:, (k * Q_PER_KV + g) * HEAD_DIM:(k * Q_PER_KV + g + 1) * HEAD_DIM]
             for g in range(Q_PER_KV)], axis=0)
        kk = kk_all[:, k * HEAD_DIM:(k + 1) * HEAD_DIM]
        vv = vv_all[:, k * HEAD_DIM:(k + 1) * HEAD_DIM]
        lg = lax.dot_general(qs, kk, (((1,), (1,)), ((), ())),
                             preferred_element_type=F32) * (HEAD_DIM ** -0.5) + bias_ref[k]
        if mask_thr is not None:
            col = lax.broadcasted_iota(jnp.int32, (Q_PER_KV * nq, nk), 1)
            lg = jnp.where(col < mask_thr, NEG_INF, lg)
        s = sink_ref[k]
        m = jnp.maximum(jnp.max(lg, axis=-1, keepdims=True), s)
        p = jnp.exp(lg - m)
        den = jnp.sum(p, axis=-1, keepdims=True) + jnp.exp(s - m)
        o = jnp.dot(p.astype(BF16), vv, preferred_element_type=F32) / den
        pieces.extend(o[g * nq:(g + 1) * nq, :] for g in range(Q_PER_KV))
    return jnp.concatenate(pieces, axis=1)


def _attn_prompt_kernel(q_ref, kv_ref, halo_ref, bias_ref, sink_ref, o_ref, *, cpt):
    t = pl.program_id(1)
    kvcat = jnp.concatenate([halo_ref[...], kv_ref[...]], axis=0)
    for c in range(cpt):
        band = kvcat[c * CHUNK:c * CHUNK + BAND, :]
        thr = None
        if c < WIN_CHUNKS:
            thr = jnp.where(t == 0, (WIN_CHUNKS - c) * CHUNK, 0)
        o = _attend(q_ref[c * CHUNK:(c + 1) * CHUNK, :], band[:, :KV_DIM], band[:, KV_DIM:],
                    bias_ref, sink_ref, thr)
        o_ref[c * CHUNK:(c + 1) * CHUNK, :] = o.astype(o_ref.dtype)


def _attn_prompt(q, kv, bias, sink_rows, n_prompt, seq, cpt):
    rows = cpt * CHUNK
    tiles = seq // rows
    halo_per_tile = rows // WINDOW
    return pl.pallas_call(
        functools.partial(_attn_prompt_kernel, cpt=cpt),
        grid=(n_prompt // seq, tiles),
        in_specs=[pl.BlockSpec((rows, ATTN_DIM), lambda b, t: (b * tiles + t, 0)),
                  pl.BlockSpec((rows, 2 * KV_DIM), lambda b, t: (b * tiles + t, 0)),
                  pl.BlockSpec((WINDOW, 2 * KV_DIM),
                               lambda b, t: (jnp.maximum((b * tiles + t) * halo_per_tile - 1, 0), 0)),
                  pl.BlockSpec(bias.shape, lambda b, t: (0, 0, 0)),
                  pl.BlockSpec(sink_rows.shape, lambda b, t: (0, 0, 0))],
        out_specs=pl.BlockSpec((rows, ATTN_DIM), lambda b, t: (b * tiles + t, 0)),
        out_shape=jax.ShapeDtypeStruct((n_prompt, ATTN_DIM), BF16),
        compiler_params=_cparams("arbitrary", "arbitrary"),
        name="attn_prompt",
    )(q, kv, kv, bias, sink_rows)


def _attn_sample_kernel(q_ref, kv_ref, ck_ref, cv_ref, bias_ref, sink_ref, o_ref):
    kv = kv_ref[...]
    kk = jnp.concatenate([ck_ref[0].astype(BF16), kv[:, :KV_DIM]], axis=0)
    vv = jnp.concatenate([cv_ref[0].astype(BF16), kv[:, KV_DIM:]], axis=0)
    o = _attend(q_ref[...], kk, vv, bias_ref, sink_ref, None)
    o_ref[...] = o.astype(o_ref.dtype)


def _attn_sample(q, kv, cache_k, cache_v, bias, sink_rows, n_prompt, t_new):
    dec_batch, buf, _ = cache_k.shape
    first = n_prompt // t_new
    return pl.pallas_call(
        _attn_sample_kernel,
        grid=(dec_batch,),
        in_specs=[pl.BlockSpec((t_new, ATTN_DIM), lambda s: (first + s, 0)),
                  pl.BlockSpec((t_new, 2 * KV_DIM), lambda s: (first + s, 0)),
                  pl.BlockSpec((1, buf, KV_DIM), lambda s: (s, 0, 0)),
                  pl.BlockSpec((1, buf, KV_DIM), lambda s: (s, 0, 0)),
                  pl.BlockSpec(bias.shape, lambda s: (0, 0, 0)),
                  pl.BlockSpec(sink_rows.shape, lambda s: (0, 0, 0))],
        out_specs=pl.BlockSpec((t_new, ATTN_DIM), lambda s: (s, 0)),
        out_shape=jax.ShapeDtypeStruct((dec_batch * t_new, ATTN_DIM), BF16),
        compiler_params=_cparams("arbitrary"),
        name="attn_sample",
    )(q, kv, cache_k, cache_v, bias, sink_rows)


def _route(lt, eidx_ref, wtok_ref):
    gl = [lt[r:r + 1, :] for r in range(N_GROUPS)]
    gmax = gl[0]
    gsel = jnp.zeros(gl[0].shape, jnp.int32)
    for r in range(1, N_GROUPS):
        better = gl[r] > gmax
        gsel = jnp.where(better, r, gsel)
        gmax = jnp.maximum(gmax, gl[r])
    gexp = [jnp.exp(v - gmax) for v in gl]
    gsum = gexp[0]
    for r in range(1, N_GROUPS):
        gsum = gsum + gexp[r]
    psel = jnp.zeros(gl[0].shape, F32)
    for r in range(N_GROUPS):
        psel = jnp.where(gsel == r, gexp[r] / gsum, psel)
    el = jnp.zeros((EXPERTS_PER_GROUP, lt.shape[1]), F32)
    for r in range(N_GROUPS):
        lo = EXPERT_ROW0 + r * EXPERTS_PER_GROUP
        el = jnp.where(gsel == r, lt[lo:lo + EXPERTS_PER_GROUP, :], el)
    emax = jnp.max(el, axis=0, keepdims=True)
    ee = jnp.exp(el - emax)
    pin = ee / jnp.sum(ee, axis=0, keepdims=True)
    idx = lax.broadcasted_iota(jnp.int32, pin.shape, 0)
    p1 = jnp.max(pin, axis=0, keepdims=True)
    i1 = jnp.min(jnp.where(pin == p1, idx, EXPERTS_PER_GROUP), axis=0, keepdims=True)
    rest = jnp.where(idx == i1, -1.0, pin)
    p2 = jnp.max(rest, axis=0, keepdims=True)
    i2 = jnp.min(jnp.where(rest == p2, idx, EXPERTS_PER_GROUP), axis=0, keepdims=True)
    tot = p1 + p2
    eidx_ref[0:1, :] = gsel * EXPERTS_PER_GROUP + i1
    eidx_ref[1:2, :] = gsel * EXPERTS_PER_GROUP + i2
    wtok_ref[0:1, :] = psel * p1 / tot
    wtok_ref[1:2, :] = psel * p2 / tot


def _post_kernel(x_ref, u_ref, uhalo_ref, sconv_ref, op_ref, os_ref, gates_ref,
                 gate1_ref, shift2_ref, scale2_ref,
                 cw_ref, cb_ref, lng_ref, lnb_ref, wpw2_ref, bpw2_ref, wao_ref, wout_ref,
                 n2_ref, wrt_ref, brt_ref,
                 x1_ref, h2_ref, eidx_ref, wtok_ref,
                 uext, s_scr, o_scr, mix_scr, h2b_scr, *, groups, prompt_tiles, tiles_per_seq):
    i = pl.program_id(0)
    d_model = x_ref.shape[1]
    hist = uext.shape[1] - GROUP_ROWS
    lead = hist - (CONV_WIDTH - 1)

    @pl.when(i < prompt_tiles)
    def _():
        first = (i % tiles_per_seq) == 0
        uext[0, 0:hist, :] = jnp.where(first, 0.0, uhalo_ref[...])
        for g in range(1, groups):
            uext[g, 0:hist, :] = u_ref[(g - 1) * GROUP_ROWS:g * GROUP_ROWS, :]
        o_scr[...] = op_ref[...]

    @pl.when(i >= prompt_tiles)
    def _():
        for g in range(groups):
            uext[g, 0:hist, :] = sconv_ref[g]
        o_scr[...] = os_ref[...]

    for g in range(groups):
        uext[g, hist:hist + GROUP_ROWS, :] = u_ref[g * GROUP_ROWS:(g + 1) * GROUP_ROWS, :]

    def conv_group(g, carry):
        acc = uext[g, pl.ds(lead, GROUP_ROWS), :] * cw_ref[0:1, :]
        for j in range(1, CONV_WIDTH):
            acc = acc + uext[g, pl.ds(lead + j, GROUP_ROWS), :] * cw_ref[j:j + 1, :]
        dd = acc + cb_ref[...]
        mu = jnp.mean(dd, axis=-1, keepdims=True)
        var = jnp.mean(jnp.square(dd - mu), axis=-1, keepdims=True)
        y = (dd - mu) * lax.rsqrt(var + LN_EPS) * lng_ref[...] + lnb_ref[...]
        r = pl.multiple_of(g * GROUP_ROWS, GROUP_ROWS)
        s_scr[pl.ds(r, GROUP_ROWS), :] = (y * jax.nn.sigmoid(y)).astype(BF16)
        return carry

    lax.fori_loop(0, groups, conv_group, 0)

    conv_out = jnp.dot(s_scr[...], wpw2_ref[...], preferred_element_type=F32) + bpw2_ref[...]
    attn_out = jnp.dot(o_scr[...], wao_ref[...], preferred_element_type=F32)
    merged = (gates_ref[:, :d_model].astype(F32) * attn_out
              + gates_ref[:, d_model:].astype(F32) * conv_out)
    mix_scr[...] = jnp.dot(merged.astype(BF16), wout_ref[...], preferred_element_type=F32)

    sub = d_model // LANES

    def res_group(g, carry):
        r = pl.multiple_of(g * GROUP_ROWS, GROUP_ROWS)
        x1 = x_ref[pl.ds(r, GROUP_ROWS), :] + gate1_ref[pl.ds(g, 1), :] * mix_scr[pl.ds(r, GROUP_ROWS), :]
        x1_ref[pl.ds(r, GROUP_ROWS), :] = x1
        h = _rms_mod(x1, n2_ref[...], scale2_ref[pl.ds(g, 1), :], shift2_ref[pl.ds(g, 1), :])
        h2b_scr[pl.ds(r, GROUP_ROWS), :] = h.astype(BF16)
        for s in range(sub):
            h2_ref[pl.ds(r * sub + s, GROUP_ROWS, stride=sub), :] = h[:, s * LANES:(s + 1) * LANES]
        return carry

    lax.fori_loop(0, groups, res_group, 0)

    lt = lax.dot_general(wrt_ref[...], h2b_scr[...], (((1,), (1,)), ((), ())),
                         preferred_element_type=F32) + brt_ref[...]
    _route(lt, eidx_ref, wtok_ref)


def _post(x, u, sconv_pad, o_p, o_s, gates, gate1_g, shift2_g, scale2_g, p, n_prompt, seq, tm):
    n, d = x.shape
    cdim = u.shape[1]
    groups = tm // GROUP_ROWS
    sub = d // LANES
    prompt_tiles = n_prompt // tm
    const = lambda shape: pl.BlockSpec(shape, lambda i: (0,) * len(shape))
    row = lambda w: pl.BlockSpec((tm, w), lambda i: (i, 0))
    grp = pl.BlockSpec((groups, d), lambda i: (i, 0))
    return pl.pallas_call(
        functools.partial(_post_kernel, groups=groups, prompt_tiles=prompt_tiles,
                          tiles_per_seq=seq // tm),
        grid=(n // tm,),
        in_specs=[row(d), row(cdim),
                  pl.BlockSpec((GROUP_ROWS, cdim), lambda i: (jnp.maximum(i * groups - 1, 0), 0)),
                  pl.BlockSpec((groups, GROUP_ROWS, cdim),
                               lambda i: (jnp.maximum(i - prompt_tiles, 0), 0, 0)),
                  pl.BlockSpec((tm, ATTN_DIM), lambda i: (jnp.minimum(i, prompt_tiles - 1), 0)),
                  pl.BlockSpec((tm, ATTN_DIM), lambda i: (jnp.maximum(i - prompt_tiles, 0), 0)),
                  row(2 * d), grp, grp, grp,
                  const((CONV_WIDTH, cdim)), const((1, cdim)), const((1, cdim)), const((1, cdim)),
                  const((cdim, d)), const((1, d)), const((ATTN_DIM, d)), const((d, d)),
                  const((1, d)), const((ROUTER_ROWS, d)), const((ROUTER_ROWS, 1))],
        out_specs=[row(d),
                   pl.BlockSpec((tm * sub, LANES), lambda i: (i, 0)),
                   pl.BlockSpec((2, tm), lambda i: (0, i)),
                   pl.BlockSpec((2, tm), lambda i: (0, i))],
        out_shape=[jax.ShapeDtypeStruct((n, d), F32),
                   jax.ShapeDtypeStruct((n * sub, LANES), F32),
                   jax.ShapeDtypeStruct((2, n), jnp.int32),
                   jax.ShapeDtypeStruct((2, n), F32)],
        scratch_shapes=[pltpu.VMEM((groups, 2 * GROUP_ROWS, cdim), F32),
                        pltpu.VMEM((tm, cdim), BF16),
                        pltpu.VMEM((tm, ATTN_DIM), BF16),
                        pltpu.VMEM((tm, d), F32),
                        pltpu.VMEM((tm, d), BF16)],
        compiler_params=_cparams("arbitrary"),
        name="post",
    )(x, u, u, sconv_pad, o_p, o_s, gates, gate1_g, shift2_g, scale2_g,
      p["conv_w"], p["conv_b"], p["ln_g"], p["ln_b"], p["w_pw2"], p["b_pw2"], p["w_ao"], p["w_out"],
      p["n2"], p["w_rt"], p["b_rt"])


def _moe_kernel(blk_e_ref, nused_ref, src_ref, src_next_ref, dst_ref, roww_ref,
                h2_hbm, wg_ref, wu_ref, wd_ref, y_hbm,
                xbuf, obuf, wg_b, wu_b, wd_b, gsem, ssem, *, sub, n_tok, n_pad):
    i = pl.program_id(0)
    nused = nused_ref[0]
    slot = i % 2

    rows = MOE_BLOCK * sub

    def gather(idx_ref, sl):
        for r in range(MOE_BLOCK):
            off = pl.multiple_of(idx_ref[0, 0, r], sub)
            pltpu.make_async_copy(h2_hbm.at[pl.ds(off, sub), :],
                                  xbuf.at[sl, pl.ds(r * sub, sub), :], gsem.at[sl]).start()

    def gather_wait(sl):
        pltpu.make_async_copy(h2_hbm.at[pl.ds(0, rows), :], xbuf.at[sl], gsem.at[sl]).wait()

    def scatter_wait():
        pltpu.make_async_copy(obuf, y_hbm.at[pl.ds(0, rows), :], ssem.at[0]).wait()

    @pl.when(i == 0)
    def _():
        obuf[...] = jnp.zeros(obuf.shape, F32)
        fills = [pltpu.make_async_copy(obuf, y_hbm.at[pl.ds((half * n_pad + n_tok) * sub + c * rows, rows), :],
                                       ssem.at[0])
                 for half in range(2) for c in range((n_pad - n_tok) // MOE_BLOCK)]
        for f in fills:
            f.start()
        for f in fills:
            f.wait()

    @pl.when(i < nused)
    def _():
        @pl.when(i == 0)
        def _():
            gather(src_ref, 0)

        @pl.when(i + 1 < nused)
        def _():
            gather(src_next_ref, 1 - slot)

        changed = jnp.logical_or(i == 0, blk_e_ref[i] != blk_e_ref[jnp.maximum(i - 1, 0)])

        @pl.when(changed)
        def _():
            wg_b[...] = wg_ref[0].astype(BF16)
            wu_b[...] = wu_ref[0].astype(BF16)
            wd_b[...] = wd_ref[0].astype(BF16)

        gather_wait(slot)
        x = jnp.concatenate(
            [xbuf[slot, pl.ds(s, MOE_BLOCK, stride=sub), :] for s in range(sub)], axis=1).astype(BF16)
        hg = jnp.dot(x, wg_b[...], preferred_element_type=F32)
        hu = jnp.dot(x, wu_b[...], preferred_element_type=F32)
        hid = (hg * jax.nn.sigmoid(hg) * hu).astype(BF16)
        y = jnp.dot(hid, wd_b[...], preferred_element_type=F32) * roww_ref[...]

        @pl.when(i > 0)
        def _():
            scatter_wait()

        for s in range(sub):
            obuf[pl.ds(s, MOE_BLOCK, stride=sub), :] = y[:, s * LANES:(s + 1) * LANES]
        for r in range(MOE_BLOCK):
            off = pl.multiple_of(dst_ref[0, 0, r], sub)
            pltpu.make_async_copy(obuf.at[pl.ds(r * sub, sub), :],
                                  y_hbm.at[pl.ds(off, sub), :], ssem.at[0]).start()

        @pl.when(i == nused - 1)
        def _():
            scatter_wait()


def _moe(h2_2d, blk_e, nused, src, dst, roww, w_gate, w_up, w_down, n_tok, n_pad):
    n_blocks = blk_e.shape[0]
    lanes = h2_2d.shape[1]
    n_exp, d, de = w_gate.shape
    sub = d // lanes
    idx_spec = lambda f: pl.BlockSpec((1, 1, MOE_BLOCK), f, memory_space=pltpu.SMEM)
    last = n_blocks - 1
    grid_spec = pltpu.PrefetchScalarGridSpec(
        num_scalar_prefetch=2,
        grid=(n_blocks,),
        in_specs=[idx_spec(lambda i, be, nu: (i, 0, 0)),
                  idx_spec(lambda i, be, nu: (jnp.minimum(i + 1, last), 0, 0)),
                  idx_spec(lambda i, be, nu: (i, 0, 0)),
                  pl.BlockSpec((MOE_BLOCK, 1), lambda i, be, nu: (i, 0)),
                  pl.BlockSpec(memory_space=pl.ANY),
                  pl.BlockSpec((1, d, de), lambda i, be, nu: (be[i], 0, 0)),
                  pl.BlockSpec((1, d, de), lambda i, be, nu: (be[i], 0, 0)),
                  pl.BlockSpec((1, de, d), lambda i, be, nu: (be[i], 0, 0))],
        out_specs=pl.BlockSpec(memory_space=pl.ANY),
        scratch_shapes=[pltpu.VMEM((2, MOE_BLOCK * sub, lanes), F32),
                        pltpu.VMEM((MOE_BLOCK * sub, lanes), F32),
                        pltpu.VMEM((d, de), BF16), pltpu.VMEM((d, de), BF16), pltpu.VMEM((de, d), BF16),
                        pltpu.SemaphoreType.DMA((2,)), pltpu.SemaphoreType.DMA((1,))],
    )
    return pl.pallas_call(
        functools.partial(_moe_kernel, sub=sub, n_tok=n_tok, n_pad=n_pad),
        grid_spec=grid_spec,
        out_shape=jax.ShapeDtypeStruct((2 * n_pad * sub, lanes), F32),
        compiler_params=_cparams("arbitrary"),
        name="moe",
    )(blk_e, nused, src, src, dst, roww, h2_2d, w_gate, w_up, w_down)


def _dispatch(eidx, wtok, n, n_pad, sub):
    a_tot = 2 * n
    flat_e = eidx.T.reshape(-1)
    flat_w = wtok.T.reshape(-1)
    onehot = (flat_e[:, None] == jnp.arange(N_EXPERTS, dtype=jnp.int32)[None, :]).astype(jnp.int32)
    cs = jnp.cumsum(onehot, axis=0)
    rank = jnp.take_along_axis(cs, flat_e[:, None], axis=1)[:, 0] - 1
    counts = cs[-1]
    padded = (counts + MOE_BLOCK - 1) // MOE_BLOCK * MOE_BLOCK
    pad_end = jnp.cumsum(padded)
    pad_start = pad_end - padded
    dest = pad_start[flat_e] + rank
    n_blocks = -(-(a_tot + N_EXPERTS * (MOE_BLOCK - 1)) // MOE_BLOCK)
    n_rows = n_blocks * MOE_BLOCK
    row_a = jnp.full((n_rows,), -1, jnp.int32).at[dest].set(jnp.arange(a_tot, dtype=jnp.int32))
    valid = row_a >= 0
    a_safe = jnp.maximum(row_a, 0)
    tok = a_safe >> 1
    src = jnp.where(valid, tok, 0)
    trash = n + (jnp.arange(n_rows, dtype=jnp.int32) % MOE_BLOCK)
    dst = jnp.where(valid, (a_safe & 1) * n_pad + tok, trash)
    roww = jnp.where(valid, flat_w[a_safe], 0.0)
    blk_e = jnp.minimum(
        jnp.searchsorted(pad_end, jnp.arange(n_blocks, dtype=jnp.int32) * MOE_BLOCK, side="right"),
        N_EXPERTS - 1).astype(jnp.int32)
    nused = (pad_end[-1] // MOE_BLOCK).astype(jnp.int32).reshape(1)
    shp = (n_blocks, 1, MOE_BLOCK)
    return blk_e, nused, (src * sub).reshape(shp), (dst * sub).reshape(shp), roww.reshape(n_rows, 1)


def _final_kernel(x1_ref, y0_ref, y1_ref, gate2_ref, fg_ref, o_ref, *, groups, sub):
    def body(g, carry):
        r = pl.multiple_of(g * GROUP_ROWS, GROUP_ROWS)
        moe = jnp.concatenate(
            [y0_ref[pl.ds(r * sub + s, GROUP_ROWS, stride=sub), :]
             + y1_ref[pl.ds(r * sub + s, GROUP_ROWS, stride=sub), :] for s in range(sub)], axis=1)
        x2 = x1_ref[pl.ds(r, GROUP_ROWS), :] + gate2_ref[pl.ds(g, 1), :] * moe
        ms = jnp.mean(x2 * x2, axis=-1, keepdims=True)
        o_ref[pl.ds(r, GROUP_ROWS), :] = x2 * lax.rsqrt(ms + RMS_EPS) * fg_ref[...]
        return carry

    lax.fori_loop(0, groups, body, 0)


def _final(x1, y2d, gate2_g, final_g, n_pad, tm):
    n, d = x1.shape
    sub = d // LANES
    groups = tm // GROUP_ROWS
    off = n_pad // tm
    return pl.pallas_call(
        functools.partial(_final_kernel, groups=groups, sub=sub),
        grid=(n // tm,),
        in_specs=[pl.BlockSpec((tm, d), lambda i: (i, 0)),
                  pl.BlockSpec((tm * sub, LANES), lambda i: (i, 0)),
                  pl.BlockSpec((tm * sub, LANES), lambda i: (off + i, 0)),
                  pl.BlockSpec((groups, d), lambda i: (i, 0)),
                  pl.BlockSpec((1, d), lambda i: (0, 0))],
        out_specs=pl.BlockSpec((tm, d), lambda i: (i, 0)),
        out_shape=jax.ShapeDtypeStruct((n, d), F32),
        compiler_params=_cparams("arbitrary"),
        name="final",
    )(x1, y2d, y2d, gate2_g, final_g.reshape(1, d))


def _layer(x, c_all, cache_k, cache_v, sconv, rel_table, lp, n_prompt, seq, t_new):
    (n1, n2, w_ada, b_ada, w_in, sink, w_ao, conv_w, conv_b, ln_g, ln_b, w_pw2, b_pw2, w_out,
     w_grp, b_grp, w_rt, b_rt, w_gate, w_up, w_down) = lp
    n, d = x.shape
    batch = n_prompt // seq
    dec_batch = (n - n_prompt) // t_new
    cdim = conv_w.shape[1]
    tm = 256
    n_pad = n + tm

    c_rows = -(-c_all.shape[0] // 8) * 8
    c_pad = jnp.pad(c_all, ((0, c_rows - c_all.shape[0]), (0, 0)))
    mods = _ada(c_pad, w_ada, b_ada)

    def per_group(k):
        m = mods[:, k * d:(k + 1) * d]
        return jnp.concatenate([jnp.repeat(m[:batch], seq // GROUP_ROWS, axis=0),
                                m[batch:batch + dec_batch]], axis=0)

    shift1, scale1, gate1, shift2, scale2, gate2 = [per_group(k) for k in range(6)]

    h1 = _norm1(x, n1, scale1, shift1, 512)

    k_off = ATTN_DIM
    glu_off = k_off + 2 * KV_DIM
    gate_off = glu_off + 2 * cdim
    w_in_b = w_in.astype(BF16)
    tglu = 256
    w_glu = jnp.stack([w_in_b[:, glu_off:glu_off + cdim].reshape(d, cdim // tglu, tglu),
                       w_in_b[:, glu_off + cdim:gate_off].reshape(d, cdim // tglu, tglu)],
                      axis=2).reshape(d, 2 * cdim)
    tmm = 1536
    (q,) = _proj(h1, w_in_b[:, :k_off], "q", tmm, 512, [BF16])
    kv, kv32 = _proj(h1, w_in_b[:, k_off:glu_off], "kv", tmm, 512, [BF16, F32])
    (u,) = _proj(h1, w_glu, "glu", tmm, 2 * tglu, [F32])
    (gates,) = _proj(h1, w_in_b[:, gate_off:], "gate", tmm, 512, [BF16])

    qi = jnp.arange(CHUNK, dtype=jnp.int32)
    kj = jnp.arange(BAND, dtype=jnp.int32)
    bias_p = _rel_bias(rel_table, kj[None, :] - WIN_CHUNKS * CHUNK - qi[:, None])
    buf = cache_k.shape[1]
    kpos = jnp.concatenate([jnp.arange(buf, dtype=jnp.int32) - buf, jnp.arange(t_new, dtype=jnp.int32)])
    bias_s = _rel_bias(rel_table, kpos[None, :] - jnp.arange(t_new, dtype=jnp.int32)[:, None])
    sink_f = sink.astype(F32).reshape(N_KV_HEADS, Q_PER_KV, 1, 1)
    sink_p = jnp.broadcast_to(sink_f, (N_KV_HEADS, Q_PER_KV, CHUNK, 1)).reshape(N_KV_HEADS, Q_PER_KV * CHUNK, 1)
    sink_s = jnp.broadcast_to(sink_f, (N_KV_HEADS, Q_PER_KV, t_new, 1)).reshape(N_KV_HEADS, Q_PER_KV * t_new, 1)
    o_p = _attn_prompt(q, kv, bias_p, sink_p, n_prompt, seq, 4)
    o_s = _attn_sample(q, kv, cache_k.reshape(dec_batch, buf, KV_DIM), cache_v.reshape(dec_batch, buf, KV_DIM),
                       bias_s, sink_s, n_prompt, t_new)

    hist = GROUP_ROWS
    sconv_pad = jnp.pad(sconv, ((0, 0), (hist - sconv.shape[1], 0), (0, 0)))
    w_rt_t = jnp.zeros((ROUTER_ROWS, d), F32)
    w_rt_t = w_rt_t.at[:N_GROUPS].set(w_grp.T).at[EXPERT_ROW0:EXPERT_ROW0 + N_EXPERTS].set(w_rt.T)
    b_rt_t = jnp.zeros((ROUTER_ROWS, 1), F32)
    b_rt_t = b_rt_t.at[:N_GROUPS, 0].set(b_grp.astype(F32)).at[EXPERT_ROW0:EXPERT_ROW0 + N_EXPERTS, 0].set(
        b_rt.astype(F32))
    params = dict(conv_w=conv_w, conv_b=conv_b.reshape(1, cdim), ln_g=ln_g.reshape(1, cdim),
                  ln_b=ln_b.reshape(1, cdim), w_pw2=w_pw2.astype(BF16), b_pw2=b_pw2.reshape(1, d),
                  w_ao=w_ao.astype(BF16), w_out=w_out.astype(BF16), n2=n2.reshape(1, d),
                  w_rt=w_rt_t.astype(BF16), b_rt=b_rt_t)
    x1, h2_2d, eidx, wtok = _post(x, u, sconv_pad, o_p, o_s, gates, gate1, shift2, scale2, params,
                                  n_prompt, seq, tm)

    sub = d // LANES
    blk_e, nused, src, dst, roww = _dispatch(eidx, wtok, n, n_pad, sub)
    y = _moe(h2_2d, blk_e, nused, src, dst, roww, w_gate, w_up, w_down, n, n_pad)
    return x1, y, gate2, kv32, u


def kernel(x_prompt, x_sample, c_prompt, c_sample, cache_k, cache_v, state_conv, rel_bias_table, norm1_g, norm2_g, w_ada, b_ada, w_in, attn_sink, w_attn_o, conv_w, conv_b, conv_ln_g, conv_ln_b, w_pw2, b_pw2, w_out, w_group, b_group, w_router, b_router, w_gate, w_up, w_down, final_g):
    batch, seq, d = x_prompt.shape
    dec_batch, t_new, _ = x_sample.shape
    depth = norm1_g.shape[0]
    assert depth == 1, "single trunk layer"
    assert t_new == GROUP_ROWS and seq % GROUP_ROWS == 0
    n_prompt = batch * seq
    n = n_prompt + dec_batch * t_new
    tm = 256
    n_pad = n + tm
    sub = d // LANES

    x = jnp.concatenate([x_prompt.reshape(n_prompt, d), x_sample.reshape(dec_batch * t_new, d)], axis=0)
    c_all = jnp.concatenate([c_prompt, c_sample], axis=0)
    l = 0
    lp = (norm1_g[l], norm2_g[l], w_ada[l], b_ada[l], w_in[l], attn_sink[l], w_attn_o[l], conv_w[l],
          conv_b[l], conv_ln_g[l], conv_ln_b[l], w_pw2[l], b_pw2[l], w_out[l], w_group[l], b_group[l],
          w_router[l], b_router[l], w_gate[l], w_up[l], w_down[l])
    x1, y, gate2, kv32, u = _layer(x, c_all, cache_k[l], cache_v[l], state_conv[l], rel_bias_table, lp,
                                   n_prompt, seq, t_new)
    out = _final(x1, y, gate2, final_g, n_pad, tm)

    y_prompt = out[:n_prompt].reshape(batch, seq, d)
    y_sample = out[n_prompt:].reshape(dec_batch, t_new, d)
    cdim = u.shape[1]
    kvp = kv32[:n_prompt].reshape(batch, seq, 2 * KV_DIM)[:, -WINDOW:]
    new_k_prompt = kvp[..., :KV_DIM].reshape(1, batch, WINDOW, N_KV_HEADS, HEAD_DIM)
    new_v_prompt = kvp[..., KV_DIM:].reshape(1, batch, WINDOW, N_KV_HEADS, HEAD_DIM)
    new_conv_prompt = u[:n_prompt].reshape(batch, seq, cdim)[:, -(CONV_WIDTH - 1):][None]
    kvs = kv32[n_prompt:].reshape(dec_batch, t_new, 2 * KV_DIM)
    buf = cache_k.shape[2]
    k_new = kvs[..., :KV_DIM].reshape(dec_batch, t_new, N_KV_HEADS, HEAD_DIM)
    v_new = kvs[..., KV_DIM:].reshape(dec_batch, t_new, N_KV_HEADS, HEAD_DIM)
    new_k_sample = jnp.concatenate([cache_k[l], k_new], axis=1)[:, -buf:][None]
    new_v_sample = jnp.concatenate([cache_v[l], v_new], axis=1)[:, -buf:][None]
    us = u[n_prompt:].reshape(dec_batch, t_new, cdim)
    new_conv_sample = jnp.concatenate([state_conv[l], us], axis=1)[:, -(CONV_WIDTH - 1):][None]
    return (y_prompt, y_sample, new_k_prompt, new_v_prompt, new_conv_prompt,
            new_k_sample, new_v_sample, new_conv_sample)
```

```python
import functools
import math

import jax
import jax.numpy as jnp
from jax import lax
from jax.experimental import pallas as pl
from jax.experimental.pallas import tpu as pltpu

F32 = jnp.float32
BF16 = jnp.bfloat16

CHUNK = 64
HEAD_DIM = 64
N_Q_HEADS = 16
N_KV_HEADS = 4
Q_PER_KV = N_Q_HEADS // N_KV_HEADS
ATTN_DIM = N_Q_HEADS * HEAD_DIM
KV_DIM = N_KV_HEADS * HEAD_DIM
WINDOW = 128
WIN_CHUNKS = WINDOW // CHUNK
BAND = (WIN_CHUNKS + 1) * CHUNK
CONV_WIDTH = 31
NUM_BUCKETS = 32
MAX_DISTANCE = 128
N_GROUPS = 4
EXPERTS_PER_GROUP = 8
N_EXPERTS = N_GROUPS * EXPERTS_PER_GROUP
MOE_BLOCK = 128
RMS_EPS = 1e-6
LN_EPS = 1e-5
NEG_INF = -1e30

GROUP_ROWS = 32
LANES = 128
SUBLANES = 8
CONV_CHUNK = 256
ROUTER_ROWS = 64
EXPERT_ROW0 = 8
V7X_VMEM_LIMIT = 56 * 1024 * 1024


def _cparams(*sem):
    return pltpu.CompilerParams(dimension_semantics=sem, vmem_limit_bytes=V7X_VMEM_LIMIT)


def _ada_kernel(c_ref, w_ref, b_ref, o_ref):
    c = c_ref[...]
    s = c * jax.nn.sigmoid(c)
    o_ref[...] = jnp.dot(s.astype(BF16), w_ref[...].astype(BF16),
                         preferred_element_type=F32) + b_ref[...]


def _ada(c_all, w_ada, b_ada):
    rows, d = c_all.shape
    ncol = w_ada.shape[1]
    tn = 1024
    return pl.pallas_call(
        _ada_kernel,
        grid=(ncol // tn,),
        in_specs=[pl.BlockSpec((rows, d), lambda j: (0, 0)),
                  pl.BlockSpec((d, tn), lambda j: (0, j)),
                  pl.BlockSpec((1, tn), lambda j: (0, j))],
        out_specs=pl.BlockSpec((rows, tn), lambda j: (0, j)),
        out_shape=jax.ShapeDtypeStruct((rows, ncol), F32),
        compiler_params=_cparams("arbitrary"),
        name="ada",
    )(c_all, w_ada, b_ada.reshape(1, ncol))


def _rms_mod(x, gain, scale, shift):
    ms = jnp.mean(x * x, axis=-1, keepdims=True)
    y = x * lax.rsqrt(ms + RMS_EPS) * gain
    return y * (1.0 + scale) + shift


def _two_stream_specs(tm, width, prompt_tiles):
    return [pl.BlockSpec((tm, width), lambda i: (jnp.minimum(i, prompt_tiles - 1), 0)),
            pl.BlockSpec((tm, width), lambda i: (jnp.maximum(i - prompt_tiles, 0), 0))]


def _norm1_kernel(xp_ref, xs_ref, g_ref, sc_ref, sh_ref, h_ref, *, groups, prompt_tiles):
    def run(x_ref):
        def body(gi, carry):
            r = pl.multiple_of(gi * GROUP_ROWS, GROUP_ROWS)
            h = _rms_mod(x_ref[pl.ds(r, GROUP_ROWS), :], g_ref[...],
                         sc_ref[pl.ds(gi, 1), :], sh_ref[pl.ds(gi, 1), :])
            h_ref[pl.ds(r, GROUP_ROWS), :] = h.astype(h_ref.dtype)
            return carry

        lax.fori_loop(0, groups, body, 0)

    i = pl.program_id(0)
    pl.when(i < prompt_tiles)(lambda: run(xp_ref))
    pl.when(i >= prompt_tiles)(lambda: run(xs_ref))


def _norm1(x_p, x_s, gain, scale_g, shift_g, tm):
    d = x_p.shape[1]
    n = x_p.shape[0] + x_s.shape[0]
    groups = tm // GROUP_ROWS
    prompt_tiles = x_p.shape[0] // tm
    return pl.pallas_call(
        functools.partial(_norm1_kernel, groups=groups, prompt_tiles=prompt_tiles),
        grid=(n // tm,),
        in_specs=_two_stream_specs(tm, d, prompt_tiles) + [
            pl.BlockSpec((1, d), lambda i: (0, 0)),
            pl.BlockSpec((groups, d), lambda i: (i, 0)),
            pl.BlockSpec((groups, d), lambda i: (i, 0))],
        out_specs=pl.BlockSpec((tm, d), lambda i: (i, 0)),
        out_shape=jax.ShapeDtypeStruct((n, d), BF16),
        compiler_params=_cparams("arbitrary"),
        name="norm1",
    )(x_p, x_s, gain.reshape(1, d), scale_g, shift_g)


def _proj_kernel(h_ref, w_ref, *o_refs, mode):
    acc = jnp.dot(h_ref[...], w_ref[...], preferred_element_type=F32)
    if mode == "qkv":
        o_refs[0][...] = acc.astype(BF16)
    elif mode == "glu":
        half = acc.shape[1] // 2
        o_refs[0][...] = acc[:, :half] * jax.nn.sigmoid(acc[:, half:])
    else:
        o_refs[0][...] = jax.nn.sigmoid(acc).astype(BF16)


def _proj(h, w, mode, tm, tnw, out_dtypes):
    n, d = h.shape
    ncol = w.shape[1]
    tno = tnw // 2 if mode == "glu" else tnw
    nout = ncol // 2 if mode == "glu" else ncol
    outs = pl.pallas_call(
        functools.partial(_proj_kernel, mode=mode),
        grid=(n // tm, ncol // tnw),
        in_specs=[pl.BlockSpec((tm, d), lambda i, j: (i, 0)),
                  pl.BlockSpec((d, tnw), lambda i, j: (0, j))],
        out_specs=[pl.BlockSpec((tm, tno), lambda i, j: (i, j)) for _ in out_dtypes],
        out_shape=[jax.ShapeDtypeStruct((n, nout), dt) for dt in out_dtypes],
        compiler_params=_cparams("arbitrary", "arbitrary"),
        name="proj_" + mode,
    )(h, w)
    return outs


def _kv_state_kernel(h_ref, w_ref, o_ref):
    o_ref[...] = jnp.dot(h_ref[...], w_ref[...], preferred_element_type=F32)


def _kv_state(h, w_kv, n_prompt, seq):
    n, d = h.shape
    batch = n_prompt // seq
    per_seq = seq // WINDOW
    steps = batch + (n - n_prompt) // WINDOW

    def row_block(i):
        return jnp.where(i < batch, (i + 1) * per_seq - 1, n_prompt // WINDOW + i - batch)

    return pl.pallas_call(
        _kv_state_kernel,
        grid=(steps,),
        in_specs=[pl.BlockSpec((WINDOW, d), lambda i: (row_block(i), 0)),
                  pl.BlockSpec((d, 2 * KV_DIM), lambda i: (0, 0))],
        out_specs=pl.BlockSpec((WINDOW, 2 * KV_DIM), lambda i: (i, 0)),
        out_shape=jax.ShapeDtypeStruct((steps * WINDOW, 2 * KV_DIM), F32),
        compiler_params=_cparams("arbitrary"),
        name="kv_state",
    )(h, w_kv)


def _t5_bucket(rel):
    nb = NUM_BUCKETS // 2
    n = -rel
    ret = jnp.where(n < 0, nb, 0)
    n = jnp.abs(n)
    max_exact = nb // 2
    nf = jnp.maximum(n, 1).astype(F32)
    large = max_exact + (jnp.log(nf / max_exact) / math.log(MAX_DISTANCE / max_exact)
                         * (nb - max_exact)).astype(jnp.int32)
    large = jnp.minimum(large, nb - 1)
    return ret + jnp.where(n < max_exact, n, large)


def _bias_kernel(tbl_ref, bkt_ref, o_ref):
    bkt = bkt_ref[...]
    nq = bkt.shape[0]
    for k in range(N_KV_HEADS):
        for g in range(Q_PER_KV):
            acc = jnp.zeros(bkt.shape, F32)
            for b in range(NUM_BUCKETS):
                acc = jnp.where(bkt == b, tbl_ref[b, k * Q_PER_KV + g], acc)
            o_ref[k, g * nq:(g + 1) * nq, :] = acc


def _rel_bias(rel_table, rel):
    nq, nk = rel.shape
    return pl.pallas_call(
        _bias_kernel,
        in_specs=[pl.BlockSpec(memory_space=pltpu.SMEM),
                  pl.BlockSpec((nq, nk), lambda: (0, 0))],
        out_specs=pl.BlockSpec((N_KV_HEADS, Q_PER_KV * nq, nk), lambda: (0, 0, 0)),
        out_shape=jax.ShapeDtypeStruct((N_KV_HEADS, Q_PER_KV * nq, nk), F32),
        name="rel_bias",
    )(rel_table.astype(F32), _t5_bucket(rel))


def _attend(qc, kk_all, vv_all, bias_ref, sink_ref, mask_thr):
    nq = qc.shape[0]
    nk = kk_all.shape[0]
    pieces = []
    for k in range(N_KV_HEADS):
        qs = jnp.concatenate(
            [qc[:, (k * Q_PER_KV + g) * HEAD_DIM:(k * Q_PER_KV + g + 1) * HEAD_DIM]
             for g in range(Q_PER_KV)], axis=0)
        kk = kk_all[:, k * HEAD_DIM:(k + 1) * HEAD_DIM]
        vv = vv_all[:, k * HEAD_DIM:(k + 1) * HEAD_DIM]
        lg = lax.dot_general(qs, kk, (((1,), (1,)), ((), ())),
                             preferred_element_type=F32) * (HEAD_DIM ** -0.5) + bias_ref[k]
        if mask_thr is not None:
            col = lax.broadcasted_iota(jnp.int32, (Q_PER_KV * nq, nk), 1)
            lg = jnp.where(col < mask_thr, NEG_INF, lg)
        s = sink_ref[k]
        m = jnp.maximum(jnp.max(lg, axis=-1, keepdims=True), s)
        p = jnp.exp(lg - m)
        den = jnp.sum(p, axis=-1, keepdims=True) + jnp.exp(s - m)
        o = jnp.dot(p.astype(BF16), vv, preferred_element_type=F32) / den
        pieces.extend(o[g * nq:(g + 1) * nq, :] for g in range(Q_PER_KV))
    return jnp.concatenate(pieces, axis=1)


def _attn_prompt_kernel(q_ref, kv_ref, halo_ref, bias_ref, sink_ref, o_ref, *, cpt):
    t = pl.program_id(1)
    kvcat = jnp.concatenate([halo_ref[...], kv_ref[...]], axis=0)
    for c in range(cpt):
        band = kvcat[c * CHUNK:c * CHUNK + BAND, :]
        thr = None
        if c < WIN_CHUNKS:
            thr = jnp.where(t == 0, (WIN_CHUNKS - c) * CHUNK, 0)
        o = _attend(q_ref[c * CHUNK:(c + 1) * CHUNK, :], band[:, :KV_DIM], band[:, KV_DIM:],
                    bias_ref, sink_ref, thr)
        o_ref[c * CHUNK:(c + 1) * CHUNK, :] = o.astype(o_ref.dtype)


def _attn_prompt(q, kv, bias, sink_rows, n_prompt, seq, cpt):
    rows = cpt * CHUNK
    tiles = seq // rows
    halo_per_tile = rows // WINDOW
    return pl.pallas_call(
        functools.partial(_attn_prompt_kernel, cpt=cpt),
        grid=(n_prompt // seq, tiles),
        in_specs=[pl.BlockSpec((rows, ATTN_DIM), lambda b, t: (b * tiles + t, 0)),
                  pl.BlockSpec((rows, 2 * KV_DIM), lambda b, t: (b * tiles + t, 0)),
                  pl.BlockSpec((WINDOW, 2 * KV_DIM),
                               lambda b, t: (jnp.maximum((b * tiles + t) * halo_per_tile - 1, 0), 0)),
                  pl.BlockSpec(bias.shape, lambda b, t: (0, 0, 0)),
                  pl.BlockSpec(sink_rows.shape, lambda b, t: (0, 0, 0))],
        out_specs=pl.BlockSpec((rows, ATTN_DIM), lambda b, t: (b * tiles + t, 0)),
        out_shape=jax.ShapeDtypeStruct((n_prompt, ATTN_DIM), BF16),
        compiler_params=_cparams("arbitrary", "arbitrary"),
        name="attn_prompt",
    )(q, kv, kv, bias, sink_rows)


def _attn_sample_kernel(q_ref, kv_ref, ck_ref, cv_ref, bias_ref, sink_ref, o_ref):
    kv = kv_ref[...]
    kk = jnp.concatenate([ck_ref[0].astype(BF16), kv[:, :KV_DIM]], axis=0)
    vv = jnp.concatenate([cv_ref[0].astype(BF16), kv[:, KV_DIM:]], axis=0)
    o = _attend(q_ref[...], kk, vv, bias_ref, sink_ref, None)
    o_ref[...] = o.astype(o_ref.dtype)


def _attn_sample(q, kv, cache_k, cache_v, bias, sink_rows, n_prompt, t_new):
    dec_batch, buf, _ = cache_k.shape
    first = n_prompt // t_new
    return pl.pallas_call(
        _attn_sample_kernel,
        grid=(dec_batch,),
        in_specs=[pl.BlockSpec((t_new, ATTN_DIM), lambda s: (first + s, 0)),
                  pl.BlockSpec((t_new, 2 * KV_DIM), lambda s: (first + s, 0)),
                  pl.BlockSpec((1, buf, KV_DIM), lambda s: (s, 0, 0)),
                  pl.BlockSpec((1, buf, KV_DIM), lambda s: (s, 0, 0)),
                  pl.BlockSpec(bias.shape, lambda s: (0, 0, 0)),
                  pl.BlockSpec(sink_rows.shape, lambda s: (0, 0, 0))],
        out_specs=pl.BlockSpec((t_new, ATTN_DIM), lambda s: (s, 0)),
        out_shape=jax.ShapeDtypeStruct((dec_batch * t_new, ATTN_DIM), BF16),
        compiler_params=_cparams("arbitrary"),
        name="attn_sample",
    )(q, kv, cache_k, cache_v, bias, sink_rows)


def _route(lt, eidx_ref, wtok_ref, rank_ref, cnt_ref, cnt_scr):
    gl = [lt[r:r + 1, :] for r in range(N_GROUPS)]
    gmax = gl[0]
    gsel = jnp.zeros(gl[0].shape, jnp.int32)
    for r in range(1, N_GROUPS):
        better = gl[r] > gmax
        gsel = jnp.where(better, r, gsel)
        gmax = jnp.maximum(gmax, gl[r])
    gexp = [jnp.exp(v - gmax) for v in gl]
    gsum = gexp[0]
    for r in range(1, N_GROUPS):
        gsum = gsum + gexp[r]
    psel = jnp.zeros(gl[0].shape, F32)
    for r in range(N_GROUPS):
        psel = jnp.where(gsel == r, gexp[r] / gsum, psel)
    el = jnp.zeros((EXPERTS_PER_GROUP, lt.shape[1]), F32)
    for r in range(N_GROUPS):
        lo = EXPERT_ROW0 + r * EXPERTS_PER_GROUP
        el = jnp.where(gsel == r, lt[lo:lo + EXPERTS_PER_GROUP, :], el)
    emax = jnp.max(el, axis=0, keepdims=True)
    ee = jnp.exp(el - emax)
    pin = ee / jnp.sum(ee, axis=0, keepdims=True)
    idx = lax.broadcasted_iota(jnp.int32, pin.shape, 0)
    p1 = jnp.max(pin, axis=0, keepdims=True)
    i1 = jnp.min(jnp.where(pin == p1, idx, EXPERTS_PER_GROUP), axis=0, keepdims=True)
    rest = jnp.where(idx == i1, -1.0, pin)
    p2 = jnp.max(rest, axis=0, keepdims=True)
    i2 = jnp.min(jnp.where(rest == p2, idx, EXPERTS_PER_GROUP), axis=0, keepdims=True)
    tot = p1 + p2
    e1 = gsel * EXPERTS_PER_GROUP + i1
    e2 = gsel * EXPERTS_PER_GROUP + i2
    eidx_ref[0:1, :] = e1
    eidx_ref[1:2, :] = e2
    wtok_ref[0:1, :] = psel * p1 / tot
    wtok_ref[1:2, :] = psel * p2 / tot

    t = lt.shape[1]
    eiota = lax.broadcasted_iota(jnp.int32, (N_EXPERTS, t), 0)
    oh1 = (eiota == e1).astype(F32)
    oh2 = (eiota == e2).astype(F32)
    both = oh1 + oh2
    before = (lax.broadcasted_iota(jnp.int32, (t, t), 0)
              < lax.broadcasted_iota(jnp.int32, (t, t), 1)).astype(BF16)
    prior = jnp.dot(both.astype(BF16), before, preferred_element_type=F32) + cnt_scr[:, 0:1]
    rank_ref[0:1, :] = jnp.sum(oh1 * prior, axis=0, keepdims=True).astype(jnp.int32)
    rank_ref[1:2, :] = jnp.sum(oh2 * prior, axis=0, keepdims=True).astype(jnp.int32)
    cnt_scr[...] = cnt_scr[...] + jnp.sum(both, axis=1, keepdims=True)
    cnt_ref[...] = cnt_scr[...]


def _post_kernel(xp_ref, xs_ref, u_ref, uhalo_ref, sconv_ref, op_ref, os_ref, gates_ref,
                 gate1_ref, shift2_ref, scale2_ref,
                 cw_ref, cb_ref, lng_ref, lnb_ref, wpw2_ref, bpw2_ref, wao_ref, wout_ref,
                 n2_ref, wrt_ref, brt_ref,
                 x1_ref, h2_ref, eidx_ref, wtok_ref, rank_ref, cnt_ref,
                 uext, shift_scr, d_scr, s_scr, o_scr, mix_scr, h2b_scr, cnt_scr,
                 *, groups, prompt_tiles, tiles_per_seq):
    i = pl.program_id(0)
    d_model = xp_ref.shape[1]

    @pl.when(i == 0)
    def _():
        cnt_scr[...] = jnp.zeros(cnt_scr.shape, F32)
    nch = cw_ref.shape[0]
    cw = uext.shape[2]
    hist = uext.shape[1] - GROUP_ROWS
    lead = hist - (CONV_WIDTH - 1)

    def put_hist(g, rows):
        for c in range(nch):
            uext[g * nch + c, 0:hist, :] = rows[:, c * cw:(c + 1) * cw]

    @pl.when(i < prompt_tiles)
    def _():
        first = (i % tiles_per_seq) == 0
        put_hist(0, jnp.where(first, 0.0, uhalo_ref[...]))
        for g in range(1, groups):
            put_hist(g, u_ref[(g - 1) * GROUP_ROWS:g * GROUP_ROWS, :])
        o_scr[...] = op_ref[...]

    @pl.when(i >= prompt_tiles)
    def _():
        for g in range(groups):
            put_hist(g, sconv_ref[g])
        o_scr[...] = os_ref[...]

    for g in range(groups):
        for c in range(nch):
            uext[g * nch + c, hist:hist + GROUP_ROWS, :] = (
                u_ref[g * GROUP_ROWS:(g + 1) * GROUP_ROWS, c * cw:(c + 1) * cw])

    span = shift_scr.shape[1]

    def conv_chunk(k, carry):
        c = k % nch
        win_all = uext[k]
        for s in range(1, SUBLANES):
            shift_scr[s, :, :] = pltpu.roll(win_all, win_all.shape[0] - s, 0)[0:span, :]
        acc = None
        for j in range(CONV_WIDTH):
            base, s = divmod(lead + j, SUBLANES)
            rows = pl.ds(base * SUBLANES, GROUP_ROWS)
            win = uext[k, rows, :] if s == 0 else shift_scr[s, rows, :]
            term = win * cw_ref[c, j:j + 1, :]
            acc = term if acc is None else acc + term
        d_scr[k] = acc + cb_ref[c]
        return carry

    lax.fori_loop(0, groups * nch, conv_chunk, 0)

    for g in range(groups):
        dd = jnp.concatenate([d_scr[g * nch + c] for c in range(nch)], axis=1)
        mu = jnp.mean(dd, axis=-1, keepdims=True)
        var = jnp.mean(jnp.square(dd - mu), axis=-1, keepdims=True)
        y = (dd - mu) * lax.rsqrt(var + LN_EPS) * lng_ref[...] + lnb_ref[...]
        s_scr[g * GROUP_ROWS:(g + 1) * GROUP_ROWS, :] = (y * jax.nn.sigmoid(y)).astype(BF16)

    conv_out = jnp.dot(s_scr[...], wpw2_ref[...], preferred_element_type=F32) + bpw2_ref[...]
    attn_out = jnp.dot(o_scr[...], wao_ref[...], preferred_element_type=F32)
    merged = (gates_ref[:, :d_model].astype(F32) * attn_out
              + gates_ref[:, d_model:].astype(F32) * conv_out)
    mix_scr[...] = jnp.dot(merged.astype(BF16), wout_ref[...], preferred_element_type=F32)

    def residual(x_ref):
        def res_group(g, carry):
            r = pl.multiple_of(g * GROUP_ROWS, GROUP_ROWS)
            x1 = (x_ref[pl.ds(r, GROUP_ROWS), :]
                  + gate1_ref[pl.ds(g, 1), :] * mix_scr[pl.ds(r, GROUP_ROWS), :])
            x1_ref[pl.ds(r, GROUP_ROWS), :] = x1
            h = _rms_mod(x1, n2_ref[...], scale2_ref[pl.ds(g, 1), :], shift2_ref[pl.ds(g, 1), :])
            h2b_scr[pl.ds(r, GROUP_ROWS), :] = h.astype(BF16)
            h2_ref[pl.ds(r, GROUP_ROWS), :] = h
            return carry

        lax.fori_loop(0, groups, res_group, 0)

    pl.when(i < prompt_tiles)(lambda: residual(xp_ref))
    pl.when(i >= prompt_tiles)(lambda: residual(xs_ref))

    lt = lax.dot_general(wrt_ref[...], h2b_scr[...], (((1,), (1,)), ((), ())),
                         preferred_element_type=F32) + brt_ref[...]
    _route(lt, eidx_ref, wtok_ref, rank_ref, cnt_ref, cnt_scr)


def _post(x_p, x_s, u, sconv_pad, o_p, o_s, gates, gate1_g, shift2_g, scale2_g, p, n_prompt, seq, tm):
    d = x_p.shape[1]
    n = x_p.shape[0] + x_s.shape[0]
    cdim = u.shape[1]
    nch = cdim // CONV_CHUNK
    groups = tm // GROUP_ROWS
    prompt_tiles = n_prompt // tm
    const = lambda shape: pl.BlockSpec(shape, lambda i: (0,) * len(shape))
    row = lambda w: pl.BlockSpec((tm, w), lambda i: (i, 0))
    grp = pl.BlockSpec((groups, d), lambda i: (i, 0))
    return pl.pallas_call(
        functools.partial(_post_kernel, groups=groups, prompt_tiles=prompt_tiles,
                          tiles_per_seq=seq // tm),
        grid=(n // tm,),
        in_specs=_two_stream_specs(tm, d, prompt_tiles) + [
                  row(cdim),
                  pl.BlockSpec((GROUP_ROWS, cdim), lambda i: (jnp.maximum(i * groups - 1, 0), 0)),
                  pl.BlockSpec((groups, GROUP_ROWS, cdim),
                               lambda i: (jnp.maximum(i - prompt_tiles, 0), 0, 0))]
                 + _two_stream_specs(tm, ATTN_DIM, prompt_tiles) + [
                  row(2 * d), grp, grp, grp,
                  const((nch, CONV_WIDTH, CONV_CHUNK)), const((nch, 1, CONV_CHUNK)),
                  const((1, cdim)), const((1, cdim)),
                  const((cdim, d)), const((1, d)), const((ATTN_DIM, d)), const((d, d)),
                  const((1, d)), const((ROUTER_ROWS, d)), const((ROUTER_ROWS, 1))],
        out_specs=[row(d),
                   row(d),
                   pl.BlockSpec((2, tm), lambda i: (0, i)),
                   pl.BlockSpec((2, tm), lambda i: (0, i)),
                   pl.BlockSpec((2, tm), lambda i: (0, i)),
                   pl.BlockSpec((N_EXPERTS, LANES), lambda i: (0, 0))],
        out_shape=[jax.ShapeDtypeStruct((n, d), F32),
                   jax.ShapeDtypeStruct((n, d), F32),
                   jax.ShapeDtypeStruct((2, n), jnp.int32),
                   jax.ShapeDtypeStruct((2, n), F32),
                   jax.ShapeDtypeStruct((2, n), jnp.int32),
                   jax.ShapeDtypeStruct((N_EXPERTS, LANES), F32)],
        scratch_shapes=[pltpu.VMEM((groups * nch, 2 * GROUP_ROWS, CONV_CHUNK), F32),
                        pltpu.VMEM((SUBLANES, 2 * GROUP_ROWS - SUBLANES, CONV_CHUNK), F32),
                        pltpu.VMEM((groups * nch, GROUP_ROWS, CONV_CHUNK), F32),
                        pltpu.VMEM((tm, cdim), BF16),
                        pltpu.VMEM((tm, ATTN_DIM), BF16),
                        pltpu.VMEM((tm, d), F32),
                        pltpu.VMEM((tm, d), BF16),
                        pltpu.VMEM((N_EXPERTS, LANES), F32)],
        compiler_params=_cparams("arbitrary"),
        name="post",
    )(x_p, x_s, u, u, sconv_pad, o_p, o_s, gates, gate1_g, shift2_g, scale2_g,
      p["conv_w"], p["conv_b"], p["ln_g"], p["ln_b"], p["w_pw2"], p["b_pw2"], p["w_ao"], p["w_out"],
      p["n2"], p["w_rt"], p["b_rt"])


def _moe_kernel(blk_e_ref, nused_ref, src_ref, src_next_ref, dst_ref, roww_ref,
                h2_hbm, wg_ref, wu_ref, wd_ref, y_hbm,
                xbuf, obuf, wg_b, wu_b, wd_b, gsem, ssem, *, n_tok, n_pad):
    i = pl.program_id(0)
    nused = nused_ref[0]
    slot = i % 2

    def gather(idx_ref, sl):
        for r in range(MOE_BLOCK):
            pltpu.make_async_copy(h2_hbm.at[pl.ds(idx_ref[0, 0, r], 1), :],
                                  xbuf.at[sl, pl.ds(r, 1), :], gsem.at[sl]).start()

    def gather_wait(sl):
        pltpu.make_async_copy(h2_hbm.at[pl.ds(0, MOE_BLOCK), :], xbuf.at[sl], gsem.at[sl]).wait()

    def scatter_wait():
        pltpu.make_async_copy(obuf, y_hbm.at[pl.ds(0, MOE_BLOCK), :], ssem.at[0]).wait()

    @pl.when(i == 0)
    def _():
        obuf[...] = jnp.zeros(obuf.shape, F32)
        fills = [pltpu.make_async_copy(
            obuf, y_hbm.at[pl.ds(half * n_pad + n_tok + c * MOE_BLOCK, MOE_BLOCK), :], ssem.at[0])
                 for half in range(2) for c in range((n_pad - n_tok) // MOE_BLOCK)]
        for f in fills:
            f.start()
        for f in fills:
            f.wait()

    @pl.when(i < nused)
    def _():
        @pl.when(i == 0)
        def _():
            gather(src_ref, 0)

        @pl.when(i + 1 < nused)
        def _():
            gather(src_next_ref, 1 - slot)

        changed = jnp.logical_or(i == 0, blk_e_ref[i] != blk_e_ref[jnp.maximum(i - 1, 0)])

        @pl.when(changed)
        def _():
            wg_b[...] = wg_ref[0].astype(BF16)
            wu_b[...] = wu_ref[0].astype(BF16)
            wd_b[...] = wd_ref[0].astype(BF16)

        gather_wait(slot)
        x = xbuf[slot].astype(BF16)
        hg = jnp.dot(x, wg_b[...], preferred_element_type=F32)
        hu = jnp.dot(x, wu_b[...], preferred_element_type=F32)
        hid = (hg * jax.nn.sigmoid(hg) * hu).astype(BF16)
        y = jnp.dot(hid, wd_b[...], preferred_element_type=F32) * roww_ref[...]

        @pl.when(i > 0)
        def _():
            scatter_wait()

        obuf[...] = y
        for r in range(MOE_BLOCK):
            pltpu.make_async_copy(obuf.at[pl.ds(r, 1), :],
                                  y_hbm.at[pl.ds(dst_ref[0, 0, r], 1), :], ssem.at[0]).start()

        @pl.when(i == nused - 1)
        def _():
            scatter_wait()


def _moe(h2, blk_e, nused, src, dst, roww, w_gate, w_up, w_down, n_tok, n_pad):
    n_blocks = blk_e.shape[0]
    n_exp, d, de = w_gate.shape
    idx_spec = lambda f: pl.BlockSpec((1, 1, MOE_BLOCK), f, memory_space=pltpu.SMEM)
    last = n_blocks - 1
    grid_spec = pltpu.PrefetchScalarGridSpec(
        num_scalar_prefetch=2,
        grid=(n_blocks,),
        in_specs=[idx_spec(lambda i, be, nu: (i, 0, 0)),
                  idx_spec(lambda i, be, nu: (jnp.minimum(i + 1, last), 0, 0)),
                  idx_spec(lambda i, be, nu: (i, 0, 0)),
                  pl.BlockSpec((MOE_BLOCK, 1), lambda i, be, nu: (i, 0)),
                  pl.BlockSpec(memory_space=pl.ANY),
                  pl.BlockSpec((1, d, de), lambda i, be, nu: (be[i], 0, 0)),
                  pl.BlockSpec((1, d, de), lambda i, be, nu: (be[i], 0, 0)),
                  pl.BlockSpec((1, de, d), lambda i, be, nu: (be[i], 0, 0))],
        out_specs=pl.BlockSpec(memory_space=pl.ANY),
        scratch_shapes=[pltpu.VMEM((2, MOE_BLOCK, d), F32),
                        pltpu.VMEM((MOE_BLOCK, d), F32),
                        pltpu.VMEM((d, de), BF16), pltpu.VMEM((d, de), BF16), pltpu.VMEM((de, d), BF16),
                        pltpu.SemaphoreType.DMA((2,)), pltpu.SemaphoreType.DMA((1,))],
    )
    return pl.pallas_call(
        functools.partial(_moe_kernel, n_tok=n_tok, n_pad=n_pad),
        grid_spec=grid_spec,
        out_shape=jax.ShapeDtypeStruct((2 * n_pad, d), F32),
        compiler_params=_cparams("arbitrary"),
        name="moe",
    )(blk_e, nused, src, src, dst, roww, h2, w_gate, w_up, w_down)


def _dispatch(eidx, wtok, rank, counts, n, n_pad):
    a_tot = 2 * n
    experts = jnp.arange(N_EXPERTS, dtype=jnp.int32)
    padded = (counts + MOE_BLOCK - 1) // MOE_BLOCK * MOE_BLOCK
    pad_end = jnp.sum(jnp.where(experts[None, :] <= experts[:, None], padded[None, :], 0), axis=1)
    pad_start = pad_end - padded
    start_of = jnp.sum(jnp.where(eidx[:, :, None] == experts, pad_start, 0), axis=-1)
    dest = (start_of + rank).reshape(-1)
    n_blocks = -(-(a_tot + N_EXPERTS * (MOE_BLOCK - 1)) // MOE_BLOCK)
    n_rows = n_blocks * MOE_BLOCK
    tok = jnp.tile(jnp.arange(n, dtype=jnp.int32), 2)
    slot = jnp.repeat(jnp.arange(2, dtype=jnp.int32), n)
    upd = jnp.stack([tok, slot * n_pad + tok,
                     lax.bitcast_convert_type(wtok.reshape(-1), jnp.int32)], axis=1)
    trash = n + jnp.arange(n_rows, dtype=jnp.int32) % MOE_BLOCK
    init = jnp.stack([jnp.zeros((n_rows,), jnp.int32), trash, jnp.zeros((n_rows,), jnp.int32)], axis=1)
    rows = init.at[dest].set(upd, unique_indices=True)
    blk_start = jnp.arange(n_blocks, dtype=jnp.int32) * MOE_BLOCK
    blk_e = jnp.minimum(jnp.sum((pad_end[None, :] <= blk_start[:, None]).astype(jnp.int32), axis=1),
                        N_EXPERTS - 1)
    nused = (pad_end[-1] // MOE_BLOCK).reshape(1)
    shp = (n_blocks, 1, MOE_BLOCK)
    roww = lax.bitcast_convert_type(rows[:, 2], F32)
    return blk_e, nused, rows[:, 0].reshape(shp), rows[:, 1].reshape(shp), roww.reshape(n_rows, 1)


def _final_kernel(x1_ref, y0_ref, y1_ref, gate2_ref, fg_ref, op_ref, os_ref, *, groups, prompt_tiles):
    def run(o_ref):
        def body(g, carry):
            r = pl.multiple_of(g * GROUP_ROWS, GROUP_ROWS)
            moe = y0_ref[pl.ds(r, GROUP_ROWS), :] + y1_ref[pl.ds(r, GROUP_ROWS), :]
            x2 = x1_ref[pl.ds(r, GROUP_ROWS), :] + gate2_ref[pl.ds(g, 1), :] * moe
            ms = jnp.mean(x2 * x2, axis=-1, keepdims=True)
            o_ref[pl.ds(r, GROUP_ROWS), :] = x2 * lax.rsqrt(ms + RMS_EPS) * fg_ref[...]
            return carry

        lax.fori_loop(0, groups, body, 0)

    i = pl.program_id(0)
    pl.when(i < prompt_tiles)(lambda: run(op_ref))
    pl.when(i >= prompt_tiles)(lambda: run(os_ref))


def _final(x1, y2, gate2_g, final_g, n_prompt, n_pad, tm):
    n, d = x1.shape
    groups = tm // GROUP_ROWS
    off = n_pad // tm
    prompt_tiles = n_prompt // tm
    return pl.pallas_call(
        functools.partial(_final_kernel, groups=groups, prompt_tiles=prompt_tiles),
        grid=(n // tm,),
        in_specs=[pl.BlockSpec((tm, d), lambda i: (i, 0)),
                  pl.BlockSpec((tm, d), lambda i: (i, 0)),
                  pl.BlockSpec((tm, d), lambda i: (off + i, 0)),
                  pl.BlockSpec((groups, d), lambda i: (i, 0)),
                  pl.BlockSpec((1, d), lambda i: (0, 0))],
        out_specs=_two_stream_specs(tm, d, prompt_tiles),
        out_shape=[jax.ShapeDtypeStruct((n_prompt, d), F32),
                   jax.ShapeDtypeStruct((n - n_prompt, d), F32)],
        compiler_params=_cparams("arbitrary"),
        name="final",
    )(x1, y2, y2, gate2_g, final_g.reshape(1, d))


def _layer(x_p, x_s, c_all, cache_k, cache_v, sconv, rel_table, lp, seq, t_new):
    (n1, n2, w_ada, b_ada, w_in, sink, w_ao, conv_w, conv_b, ln_g, ln_b, w_pw2, b_pw2, w_out,
     w_grp, b_grp, w_rt, b_rt, w_gate, w_up, w_down) = lp
    n_prompt, d = x_p.shape
    n = n_prompt + x_s.shape[0]
    batch = n_prompt // seq
    dec_batch = (n - n_prompt) // t_new
    cdim = conv_w.shape[1]
    tm = 256
    n_pad = n + tm

    c_rows = -(-c_all.shape[0] // 8) * 8
    c_pad = jnp.pad(c_all, ((0, c_rows - c_all.shape[0]), (0, 0)))
    mods = _ada(c_pad, w_ada, b_ada)

    def per_group(k):
        m = mods[:, k * d:(k + 1) * d]
        mp = jnp.broadcast_to(m[:batch, None, :], (batch, seq // GROUP_ROWS, d))
        return jnp.concatenate([mp.reshape(batch * (seq // GROUP_ROWS), d), m[batch:batch + dec_batch]],
                               axis=0)

    shift1, scale1, gate1, shift2, scale2, gate2 = [per_group(k) for k in range(6)]

    h1 = _norm1(x_p, x_s, n1, scale1, shift1, 512)

    k_off = ATTN_DIM
    glu_off = k_off + 2 * KV_DIM
    gate_off = glu_off + 2 * cdim
    w_in_b = w_in.astype(BF16)
    tglu = 256
    w_glu = jnp.stack([w_in_b[:, glu_off:glu_off + cdim].reshape(d, cdim // tglu, tglu),
                       w_in_b[:, glu_off + cdim:gate_off].reshape(d, cdim // tglu, tglu)],
                      axis=2).reshape(d, 2 * cdim)
    tmm = 1536
    (q,) = _proj(h1, w_in_b[:, :k_off], "qkv", tmm, 512, [BF16])
    (kv,) = _proj(h1, w_in_b[:, k_off:glu_off], "qkv", tmm, 512, [BF16])
    kv_state = _kv_state(h1, w_in_b[:, k_off:glu_off], n_prompt, seq)
    (u,) = _proj(h1, w_glu, "glu", tmm, 2 * tglu, [F32])
    (gates,) = _proj(h1, w_in_b[:, gate_off:], "gate", tmm, 512, [BF16])

    qi = jnp.arange(CHUNK, dtype=jnp.int32)
    kj = jnp.arange(BAND, dtype=jnp.int32)
    bias_p = _rel_bias(rel_table, kj[None, :] - WIN_CHUNKS * CHUNK - qi[:, None])
    buf = cache_k.shape[1]
    kpos = jnp.concatenate([jnp.arange(buf, dtype=jnp.int32) - buf, jnp.arange(t_new, dtype=jnp.int32)])
    bias_s = _rel_bias(rel_table, kpos[None, :] - jnp.arange(t_new, dtype=jnp.int32)[:, None])
    sink_f = sink.astype(F32).reshape(N_KV_HEADS, Q_PER_KV, 1, 1)
    sink_p = jnp.broadcast_to(sink_f, (N_KV_HEADS, Q_PER_KV, CHUNK, 1)).reshape(N_KV_HEADS, Q_PER_KV * CHUNK, 1)
    sink_s = jnp.broadcast_to(sink_f, (N_KV_HEADS, Q_PER_KV, t_new, 1)).reshape(N_KV_HEADS, Q_PER_KV * t_new, 1)
    o_p = _attn_prompt(q, kv, bias_p, sink_p, n_prompt, seq, 4)
    o_s = _attn_sample(q, kv, cache_k.reshape(dec_batch, buf, KV_DIM), cache_v.reshape(dec_batch, buf, KV_DIM),
                       bias_s, sink_s, n_prompt, t_new)

    hist = GROUP_ROWS
    sconv_pad = jnp.pad(sconv, ((0, 0), (hist - sconv.shape[1], 0), (0, 0)))
    w_rt_t = jnp.zeros((ROUTER_ROWS, d), F32)
    w_rt_t = w_rt_t.at[:N_GROUPS].set(w_grp.T).at[EXPERT_ROW0:EXPERT_ROW0 + N_EXPERTS].set(w_rt.T)
    b_rt_t = jnp.zeros((ROUTER_ROWS, 1), F32)
    b_rt_t = b_rt_t.at[:N_GROUPS, 0].set(b_grp.astype(F32)).at[EXPERT_ROW0:EXPERT_ROW0 + N_EXPERTS, 0].set(
        b_rt.astype(F32))
    nch = cdim // CONV_CHUNK
    conv_w_c = conv_w.reshape(CONV_WIDTH, nch, CONV_CHUNK).transpose(1, 0, 2)
    params = dict(conv_w=conv_w_c, conv_b=conv_b.reshape(nch, 1, CONV_CHUNK), ln_g=ln_g.reshape(1, cdim),
                  ln_b=ln_b.reshape(1, cdim), w_pw2=w_pw2.astype(BF16), b_pw2=b_pw2.reshape(1, d),
                  w_ao=w_ao.astype(BF16), w_out=w_out.astype(BF16), n2=n2.reshape(1, d),
                  w_rt=w_rt_t.astype(BF16), b_rt=b_rt_t)
    x1, h2_2d, eidx, wtok, rank, cnt = _post(x_p, x_s, u, sconv_pad, o_p, o_s, gates, gate1, shift2, scale2,
                                             params, n_prompt, seq, tm)

    blk_e, nused, src, dst, roww = _dispatch(eidx, wtok, rank, cnt[:, 0].astype(jnp.int32), n, n_pad)
    y = _moe(h2_2d, blk_e, nused, src, dst, roww, w_gate, w_up, w_down, n, n_pad)
    return x1, y, gate2, kv_state, u


def kernel(x_prompt, x_sample, c_prompt, c_sample, cache_k, cache_v, state_conv, rel_bias_table, norm1_g, norm2_g, w_ada, b_ada, w_in, attn_sink, w_attn_o, conv_w, conv_b, conv_ln_g, conv_ln_b, w_pw2, b_pw2, w_out, w_group, b_group, w_router, b_router, w_gate, w_up, w_down, final_g):
    batch, seq, d = x_prompt.shape
    dec_batch, t_new, _ = x_sample.shape
    depth = norm1_g.shape[0]
    assert depth == 1, "single trunk layer"
    assert t_new == GROUP_ROWS and seq % GROUP_ROWS == 0
    n_prompt = batch * seq
    n = n_prompt + dec_batch * t_new
    tm = 256
    n_pad = n + tm

    x_p = x_prompt.reshape(n_prompt, d)
    x_s = x_sample.reshape(dec_batch * t_new, d)
    c_all = jnp.concatenate([c_prompt, c_sample], axis=0)
    l = 0
    lp = (norm1_g[l], norm2_g[l], w_ada[l], b_ada[l], w_in[l], attn_sink[l], w_attn_o[l], conv_w[l],
          conv_b[l], conv_ln_g[l], conv_ln_b[l], w_pw2[l], b_pw2[l], w_out[l], w_group[l], b_group[l],
          w_router[l], b_router[l], w_gate[l], w_up[l], w_down[l])
    x1, y, gate2, kv_state, u = _layer(x_p, x_s, c_all, cache_k[l], cache_v[l], state_conv[l], rel_bias_table,
                                       lp, seq, t_new)
    out_p, out_s = _final(x1, y, gate2, final_g, n_prompt, n_pad, tm)

    y_prompt = out_p.reshape(batch, seq, d)
    y_sample = out_s.reshape(dec_batch, t_new, d)
    cdim = u.shape[1]
    kvp = kv_state[:batch * WINDOW].reshape(batch, WINDOW, 2 * KV_DIM)
    new_k_prompt = kvp[..., :KV_DIM].reshape(1, batch, WINDOW, N_KV_HEADS, HEAD_DIM)
    new_v_prompt = kvp[..., KV_DIM:].reshape(1, batch, WINDOW, N_KV_HEADS, HEAD_DIM)
    new_conv_prompt = u[:n_prompt].reshape(batch, seq, cdim)[:, -(CONV_WIDTH - 1):][None]
    kvs = kv_state[batch * WINDOW:].reshape(dec_batch, t_new, 2 * KV_DIM)
    buf = cache_k.shape[2]
    k_new = kvs[..., :KV_DIM].reshape(dec_batch, t_new, N_KV_HEADS, HEAD_DIM)
    v_new = kvs[..., KV_DIM:].reshape(dec_batch, t_new, N_KV_HEADS, HEAD_DIM)
    new_k_sample = jnp.concatenate([cache_k[l], k_new], axis=1)[:, -buf:][None]
    new_v_sample = jnp.concatenate([cache_v[l], v_new], axis=1)[:, -buf:][None]
    us = u[n_prompt:].reshape(dec_batch, t_new, cdim)
    new_conv_sample = jnp.concatenate([state_conv[l], us], axis=1)[:, -(CONV_WIDTH - 1):][None]
    return (y_prompt, y_sample, new_k_prompt, new_v_prompt, new_conv_prompt,
            new_k_sample, new_v_sample, new_conv_sample)
```

```python
import functools
import math

import jax
import jax.numpy as jnp
from jax import lax
from jax.experimental import pallas as pl
from jax.experimental.pallas import tpu as pltpu

F32 = jnp.float32
BF16 = jnp.bfloat16

CHUNK = 64
HEAD_DIM = 64
N_Q_HEADS = 16
N_KV_HEADS = 4
Q_PER_KV = N_Q_HEADS // N_KV_HEADS
ATTN_DIM = N_Q_HEADS * HEAD_DIM
KV_DIM = N_KV_HEADS * HEAD_DIM
WINDOW = 128
WIN_CHUNKS = WINDOW // CHUNK
BAND = (WIN_CHUNKS + 1) * CHUNK
CONV_WIDTH = 31
NUM_BUCKETS = 32
MAX_DISTANCE = 128
N_GROUPS = 4
EXPERTS_PER_GROUP = 8
N_EXPERTS = N_GROUPS * EXPERTS_PER_GROUP
MOE_BLOCK = 128
RMS_EPS = 1e-6
LN_EPS = 1e-5
NEG_INF = -1e30

GROUP_ROWS = 32
LANES = 128
SUBLANES = 8
CONV_CHUNK = 256
ROUTER_ROWS = 64
EXPERT_ROW0 = 8
V7X_VMEM_LIMIT = 56 * 1024 * 1024


def _cparams(*sem):
    return pltpu.CompilerParams(dimension_semantics=sem, vmem_limit_bytes=V7X_VMEM_LIMIT)


def _ada_kernel(c_ref, w_ref, b_ref, o_ref):
    c = c_ref[...]
    s = c * jax.nn.sigmoid(c)
    o_ref[...] = jnp.dot(s.astype(BF16), w_ref[...].astype(BF16),
                         preferred_element_type=F32) + b_ref[...]


def _ada(c_all, w_ada, b_ada):
    rows, d = c_all.shape
    ncol = w_ada.shape[1]
    tn = 1024
    return pl.pallas_call(
        _ada_kernel,
        grid=(ncol // tn,),
        in_specs=[pl.BlockSpec((rows, d), lambda j: (0, 0)),
                  pl.BlockSpec((d, tn), lambda j: (0, j)),
                  pl.BlockSpec((1, tn), lambda j: (0, j))],
        out_specs=pl.BlockSpec((rows, tn), lambda j: (0, j)),
        out_shape=jax.ShapeDtypeStruct((rows, ncol), F32),
        compiler_params=_cparams("arbitrary"),
        name="ada",
    )(c_all, w_ada, b_ada.reshape(1, ncol))


def _rms_mod(x, gain, scale, shift):
    ms = jnp.mean(x * x, axis=-1, keepdims=True)
    y = x * lax.rsqrt(ms + RMS_EPS) * gain
    return y * (1.0 + scale) + shift


def _two_stream_specs(tm, width, prompt_tiles):
    return [pl.BlockSpec((tm, width), lambda i: (jnp.minimum(i, prompt_tiles - 1), 0)),
            pl.BlockSpec((tm, width), lambda i: (jnp.maximum(i - prompt_tiles, 0), 0))]


def _norm1_kernel(xp_ref, xs_ref, g_ref, sc_ref, sh_ref, h_ref, *, groups, prompt_tiles):
    def run(x_ref):
        def body(gi, carry):
            r = pl.multiple_of(gi * GROUP_ROWS, GROUP_ROWS)
            h = _rms_mod(x_ref[pl.ds(r, GROUP_ROWS), :], g_ref[...],
                         sc_ref[pl.ds(gi, 1), :], sh_ref[pl.ds(gi, 1), :])
            h_ref[pl.ds(r, GROUP_ROWS), :] = h.astype(h_ref.dtype)
            return carry

        lax.fori_loop(0, groups, body, 0)

    i = pl.program_id(0)
    pl.when(i < prompt_tiles)(lambda: run(xp_ref))
    pl.when(i >= prompt_tiles)(lambda: run(xs_ref))


def _norm1(x_p, x_s, gain, scale_g, shift_g, tm):
    d = x_p.shape[1]
    n = x_p.shape[0] + x_s.shape[0]
    groups = tm // GROUP_ROWS
    prompt_tiles = x_p.shape[0] // tm
    return pl.pallas_call(
        functools.partial(_norm1_kernel, groups=groups, prompt_tiles=prompt_tiles),
        grid=(n // tm,),
        in_specs=_two_stream_specs(tm, d, prompt_tiles) + [
            pl.BlockSpec((1, d), lambda i: (0, 0)),
            pl.BlockSpec((groups, d), lambda i: (i, 0)),
            pl.BlockSpec((groups, d), lambda i: (i, 0))],
        out_specs=pl.BlockSpec((tm, d), lambda i: (i, 0)),
        out_shape=jax.ShapeDtypeStruct((n, d), BF16),
        compiler_params=_cparams("arbitrary"),
        name="norm1",
    )(x_p, x_s, gain.reshape(1, d), scale_g, shift_g)


def _proj_kernel(h_ref, w_ref, *o_refs, mode):
    acc = jnp.dot(h_ref[...], w_ref[...], preferred_element_type=F32)
    if mode == "qkv":
        o_refs[0][...] = acc.astype(BF16)
    elif mode == "glu":
        half = acc.shape[1] // 2
        o_refs[0][...] = acc[:, :half] * jax.nn.sigmoid(acc[:, half:])
    else:
        o_refs[0][...] = jax.nn.sigmoid(acc).astype(BF16)


def _proj(h, w, mode, tm, tnw, out_dtypes):
    n, d = h.shape
    ncol = w.shape[1]
    tno = tnw // 2 if mode == "glu" else tnw
    nout = ncol // 2 if mode == "glu" else ncol
    outs = pl.pallas_call(
        functools.partial(_proj_kernel, mode=mode),
        grid=(n // tm, ncol // tnw),
        in_specs=[pl.BlockSpec((tm, d), lambda i, j: (i, 0)),
                  pl.BlockSpec((d, tnw), lambda i, j: (0, j))],
        out_specs=[pl.BlockSpec((tm, tno), lambda i, j: (i, j)) for _ in out_dtypes],
        out_shape=[jax.ShapeDtypeStruct((n, nout), dt) for dt in out_dtypes],
        compiler_params=_cparams("arbitrary", "arbitrary"),
        name="proj_" + mode,
    )(h, w)
    return outs


def _kv_state_kernel(h_ref, w_ref, o_ref):
    o_ref[...] = jnp.dot(h_ref[...], w_ref[...], preferred_element_type=F32)


def _kv_state(h, w_kv, n_prompt, seq):
    n, d = h.shape
    batch = n_prompt // seq
    per_seq = seq // WINDOW
    steps = batch + (n - n_prompt) // WINDOW

    def row_block(i):
        return jnp.where(i < batch, (i + 1) * per_seq - 1, n_prompt // WINDOW + i - batch)

    return pl.pallas_call(
        _kv_state_kernel,
        grid=(steps,),
        in_specs=[pl.BlockSpec((WINDOW, d), lambda i: (row_block(i), 0)),
                  pl.BlockSpec((d, 2 * KV_DIM), lambda i: (0, 0))],
        out_specs=pl.BlockSpec((WINDOW, 2 * KV_DIM), lambda i: (i, 0)),
        out_shape=jax.ShapeDtypeStruct((steps * WINDOW, 2 * KV_DIM), F32),
        compiler_params=_cparams("arbitrary"),
        name="kv_state",
    )(h, w_kv)


def _t5_bucket(rel):
    nb = NUM_BUCKETS // 2
    n = -rel
    ret = jnp.where(n < 0, nb, 0)
    n = jnp.abs(n)
    max_exact = nb // 2
    nf = jnp.maximum(n, 1).astype(F32)
    large = max_exact + (jnp.log(nf / max_exact) / math.log(MAX_DISTANCE / max_exact)
                         * (nb - max_exact)).astype(jnp.int32)
    large = jnp.minimum(large, nb - 1)
    return ret + jnp.where(n < max_exact, n, large)


def _bias_kernel(tbl_ref, bkt_ref, o_ref):
    bkt = bkt_ref[...]
    nq = bkt.shape[0]
    for k in range(N_KV_HEADS):
        for g in range(Q_PER_KV):
            acc = jnp.zeros(bkt.shape, F32)
            for b in range(NUM_BUCKETS):
                acc = jnp.where(bkt == b, tbl_ref[b, k * Q_PER_KV + g], acc)
            o_ref[k, g * nq:(g + 1) * nq, :] = acc


def _rel_bias(rel_table, rel):
    nq, nk = rel.shape
    return pl.pallas_call(
        _bias_kernel,
        in_specs=[pl.BlockSpec(memory_space=pltpu.SMEM),
                  pl.BlockSpec((nq, nk), lambda: (0, 0))],
        out_specs=pl.BlockSpec((N_KV_HEADS, Q_PER_KV * nq, nk), lambda: (0, 0, 0)),
        out_shape=jax.ShapeDtypeStruct((N_KV_HEADS, Q_PER_KV * nq, nk), F32),
        name="rel_bias",
    )(rel_table.astype(F32), _t5_bucket(rel))


PAIR_BAND = 2 * CHUNK + WINDOW
HEADS_PER_COL = LANES // HEAD_DIM
COLS_PER_KV = Q_PER_KV // HEADS_PER_COL


def _pair_bias_kernel(tbl_ref, bkt_ref, o_ref):
    for e in range(2):
        bkt = bkt_ref[e]
        for k in range(N_KV_HEADS):
            for col in range(COLS_PER_KV):
                for half in range(HEADS_PER_COL):
                    head = k * Q_PER_KV + col * HEADS_PER_COL + half
                    acc = jnp.full(bkt.shape, NEG_INF, F32)
                    for b in range(NUM_BUCKETS):
                        acc = jnp.where(bkt == b, tbl_ref[b, head], acc)
                    r0 = (e * COLS_PER_KV + col) * CHUNK
                    o_ref[k, r0:r0 + CHUNK, half * PAIR_BAND:(half + 1) * PAIR_BAND] = acc


def _pair_bias(rel_table):
    qi = jnp.arange(CHUNK, dtype=jnp.int32)[None, :, None]
    kj = jnp.arange(PAIR_BAND, dtype=jnp.int32)[None, None, :]
    e = jnp.arange(2, dtype=jnp.int32)[:, None, None]
    rel = (kj - WINDOW) - (e * CHUNK + qi)
    key_chunk = kj // CHUNK - e
    seen = (key_chunk >= 0) & (key_chunk <= WIN_CHUNKS)
    bkt = jnp.where(seen, _t5_bucket(rel), -1)
    rows = 2 * COLS_PER_KV * CHUNK
    return pl.pallas_call(
        _pair_bias_kernel,
        in_specs=[pl.BlockSpec(memory_space=pltpu.SMEM),
                  pl.BlockSpec((2, CHUNK, PAIR_BAND), lambda: (0, 0, 0))],
        out_specs=pl.BlockSpec((N_KV_HEADS, rows, HEADS_PER_COL * PAIR_BAND), lambda: (0, 0, 0)),
        out_shape=jax.ShapeDtypeStruct((N_KV_HEADS, rows, HEADS_PER_COL * PAIR_BAND), F32),
        name="pair_bias",
    )(rel_table.astype(F32), bkt)


def _attend(qc, kk_all, vv_all, bias_ref, sink_ref, mask_thr):
    nq = qc.shape[0]
    nk = kk_all.shape[0]
    pieces = []
    for k in range(N_KV_HEADS):
        qs = jnp.concatenate(
            [qc[:, (k * Q_PER_KV + g) * HEAD_DIM:(k * Q_PER_KV + g + 1) * HEAD_DIM]
             for g in range(Q_PER_KV)], axis=0)
        kk = kk_all[:, k * HEAD_DIM:(k + 1) * HEAD_DIM]
        vv = vv_all[:, k * HEAD_DIM:(k + 1) * HEAD_DIM]
        lg = lax.dot_general(qs, kk, (((1,), (1,)), ((), ())),
                             preferred_element_type=F32) * (HEAD_DIM ** -0.5) + bias_ref[k]
        if mask_thr is not None:
            col = lax.broadcasted_iota(jnp.int32, (Q_PER_KV * nq, nk), 1)
            lg = jnp.where(col < mask_thr, NEG_INF, lg)
        s = sink_ref[k]
        m = jnp.maximum(jnp.max(lg, axis=-1, keepdims=True), s)
        p = jnp.exp(lg - m)
        den = jnp.sum(p, axis=-1, keepdims=True) + jnp.exp(s - m)
        o = jnp.dot(p.astype(BF16), vv, preferred_element_type=F32) / den
        pieces.extend(o[g * nq:(g + 1) * nq, :] for g in range(Q_PER_KV))
    return jnp.concatenate(pieces, axis=1)


def _attn_prompt_kernel(q_ref, kv_ref, halo_ref, bias_ref, sink_ref, o_ref, ka, kb, va, vb, *, pairs):
    t = pl.program_id(1)
    kvcat = jnp.concatenate([halo_ref[...], kv_ref[...]], axis=0)
    low = lax.broadcasted_iota(jnp.int32, (kvcat.shape[0], LANES), 1) < HEAD_DIM
    zero = jnp.zeros((kvcat.shape[0], LANES), BF16)
    ones_low = jnp.where(low, 1.0, 0.0).astype(BF16)
    ones_high = jnp.where(low, 0.0, 1.0).astype(BF16)
    kv_cols = KV_DIM // LANES
    for col in range(2 * kv_cols):
        x = kvcat[:, col * LANES:(col + 1) * LANES]
        xs = jnp.concatenate([x[:, HEAD_DIM:], x[:, :HEAD_DIM]], axis=1)
        dst_a, dst_b = (ka, kb) if col < kv_cols else (va, vb)
        k0 = (col % kv_cols) * HEADS_PER_COL
        dst_a[k0, :, 0:LANES] = jnp.where(low, x, zero)
        dst_b[k0, :, 0:LANES] = jnp.where(low, zero, xs)
        dst_a[k0 + 1, :, 0:LANES] = jnp.where(low, xs, zero)
        dst_b[k0 + 1, :, 0:LANES] = jnp.where(low, zero, x)
    for k in range(N_KV_HEADS):
        va[k, :, LANES:2 * LANES] = ones_low
        vb[k, :, LANES:2 * LANES] = ones_high

    out_low = lax.broadcasted_iota(jnp.int32, (2 * COLS_PER_KV * CHUNK, LANES), 1) < HEAD_DIM
    for p in range(pairs):
        r0 = p * 2 * CHUNK
        for k in range(N_KV_HEADS):
            lhs = jnp.concatenate(
                [q_ref[r0 + e * CHUNK:r0 + (e + 1) * CHUNK, (k * COLS_PER_KV + c) * LANES:(k * COLS_PER_KV + c + 1) * LANES]
                 for e in range(2) for c in range(COLS_PER_KV)], axis=0)
            keys = jnp.concatenate([ka[k, r0:r0 + PAIR_BAND, :], kb[k, r0:r0 + PAIR_BAND, :]], axis=0)
            lg = lax.dot_general(lhs, keys, (((1,), (1,)), ((), ())),
                                 preferred_element_type=F32) * (HEAD_DIM ** -0.5) + bias_ref[k]
            probs, sink_terms = [], []
            for half in range(HEADS_PER_COL):
                seg = lg[:, half * PAIR_BAND:(half + 1) * PAIR_BAND]
                if p == 0:
                    kcol = lax.broadcasted_iota(jnp.int32, seg.shape, 1)
                    seg = jnp.where(kcol < jnp.where(t == 0, WINDOW, 0), NEG_INF, seg)
                s = sink_ref[k, half]
                folded = seg[:, :LANES]
                for j in range(1, PAIR_BAND // LANES):
                    folded = jnp.maximum(folded, seg[:, j * LANES:(j + 1) * LANES])
                m = jnp.maximum(jnp.max(folded, axis=-1, keepdims=True), s)
                pr = jnp.exp(seg - jnp.concatenate([m] * (PAIR_BAND // LANES), axis=1))
                sink_terms.append(jnp.exp(s - m))
                probs.append(pr.astype(BF16))
            vals = jnp.concatenate([va[k, r0:r0 + PAIR_BAND, :], vb[k, r0:r0 + PAIR_BAND, :]], axis=0)
            oe = jnp.dot(jnp.concatenate(probs, axis=1), vals, preferred_element_type=F32)
            o = oe[:, :LANES] / (oe[:, LANES:] + jnp.where(out_low, sink_terms[0], sink_terms[1]))
            for e in range(2):
                for c in range(COLS_PER_KV):
                    rr = (e * COLS_PER_KV + c) * CHUNK
                    o_ref[r0 + e * CHUNK:r0 + (e + 1) * CHUNK,
                          (k * COLS_PER_KV + c) * LANES:(k * COLS_PER_KV + c + 1) * LANES] = (
                        o[rr:rr + CHUNK, :].astype(o_ref.dtype))


def _attn_prompt(q, kv, bias, sink_rows, n_prompt, seq, pairs):
    rows = pairs * 2 * CHUNK
    tiles = seq // rows
    halo_per_tile = rows // WINDOW
    keys = WINDOW + rows
    return pl.pallas_call(
        functools.partial(_attn_prompt_kernel, pairs=pairs),
        grid=(n_prompt // seq, tiles),
        scratch_shapes=[pltpu.VMEM((N_KV_HEADS, keys, LANES), BF16) for _ in range(2)]
        + [pltpu.VMEM((N_KV_HEADS, keys, 2 * LANES), BF16) for _ in range(2)],
        in_specs=[pl.BlockSpec((rows, ATTN_DIM), lambda b, t: (b * tiles + t, 0)),
                  pl.BlockSpec((rows, 2 * KV_DIM), lambda b, t: (b * tiles + t, 0)),
                  pl.BlockSpec((WINDOW, 2 * KV_DIM),
                               lambda b, t: (jnp.maximum((b * tiles + t) * halo_per_tile - 1, 0), 0)),
                  pl.BlockSpec(bias.shape, lambda b, t: (0, 0, 0)),
                  pl.BlockSpec(sink_rows.shape, lambda b, t: (0, 0, 0, 0))],
        out_specs=pl.BlockSpec((rows, ATTN_DIM), lambda b, t: (b * tiles + t, 0)),
        out_shape=jax.ShapeDtypeStruct((n_prompt, ATTN_DIM), BF16),
        compiler_params=_cparams("arbitrary", "arbitrary"),
        name="attn_prompt",
    )(q, kv, kv, bias, sink_rows)


def _attn_sample_kernel(q_ref, kv_ref, ck_ref, cv_ref, bias_ref, sink_ref, o_ref):
    kv = kv_ref[...]
    kk = jnp.concatenate([ck_ref[0].astype(BF16), kv[:, :KV_DIM]], axis=0)
    vv = jnp.concatenate([cv_ref[0].astype(BF16), kv[:, KV_DIM:]], axis=0)
    o = _attend(q_ref[...], kk, vv, bias_ref, sink_ref, None)
    o_ref[...] = o.astype(o_ref.dtype)


def _attn_sample(q, kv, cache_k, cache_v, bias, sink_rows, n_prompt, t_new):
    dec_batch, buf, _ = cache_k.shape
    first = n_prompt // t_new
    return pl.pallas_call(
        _attn_sample_kernel,
        grid=(dec_batch,),
        in_specs=[pl.BlockSpec((t_new, ATTN_DIM), lambda s: (first + s, 0)),
                  pl.BlockSpec((t_new, 2 * KV_DIM), lambda s: (first + s, 0)),
                  pl.BlockSpec((1, buf, KV_DIM), lambda s: (s, 0, 0)),
                  pl.BlockSpec((1, buf, KV_DIM), lambda s: (s, 0, 0)),
                  pl.BlockSpec(bias.shape, lambda s: (0, 0, 0)),
                  pl.BlockSpec(sink_rows.shape, lambda s: (0, 0, 0))],
        out_specs=pl.BlockSpec((t_new, ATTN_DIM), lambda s: (s, 0)),
        out_shape=jax.ShapeDtypeStruct((dec_batch * t_new, ATTN_DIM), BF16),
        compiler_params=_cparams("arbitrary"),
        name="attn_sample",
    )(q, kv, cache_k, cache_v, bias, sink_rows)


def _route(lt, eidx_ref, wtok_ref, rank_ref, cnt_ref, cnt_scr):
    gl = [lt[r:r + 1, :] for r in range(N_GROUPS)]
    gmax = gl[0]
    gsel = jnp.zeros(gl[0].shape, jnp.int32)
    for r in range(1, N_GROUPS):
        better = gl[r] > gmax
        gsel = jnp.where(better, r, gsel)
        gmax = jnp.maximum(gmax, gl[r])
    gexp = [jnp.exp(v - gmax) for v in gl]
    gsum = gexp[0]
    for r in range(1, N_GROUPS):
        gsum = gsum + gexp[r]
    psel = jnp.zeros(gl[0].shape, F32)
    for r in range(N_GROUPS):
        psel = jnp.where(gsel == r, gexp[r] / gsum, psel)
    el = jnp.zeros((EXPERTS_PER_GROUP, lt.shape[1]), F32)
    for r in range(N_GROUPS):
        lo = EXPERT_ROW0 + r * EXPERTS_PER_GROUP
        el = jnp.where(gsel == r, lt[lo:lo + EXPERTS_PER_GROUP, :], el)
    emax = jnp.max(el, axis=0, keepdims=True)
    ee = jnp.exp(el - emax)
    pin = ee / jnp.sum(ee, axis=0, keepdims=True)
    idx = lax.broadcasted_iota(jnp.int32, pin.shape, 0)
    p1 = jnp.max(pin, axis=0, keepdims=True)
    i1 = jnp.min(jnp.where(pin == p1, idx, EXPERTS_PER_GROUP), axis=0, keepdims=True)
    rest = jnp.where(idx == i1, -1.0, pin)
    p2 = jnp.max(rest, axis=0, keepdims=True)
    i2 = jnp.min(jnp.where(rest == p2, idx, EXPERTS_PER_GROUP), axis=0, keepdims=True)
    tot = p1 + p2
    e1 = gsel * EXPERTS_PER_GROUP + i1
    e2 = gsel * EXPERTS_PER_GROUP + i2
    eidx_ref[0:1, :] = e1
    eidx_ref[1:2, :] = e2
    wtok_ref[0:1, :] = psel * p1 / tot
    wtok_ref[1:2, :] = psel * p2 / tot

    t = lt.shape[1]
    eiota = lax.broadcasted_iota(jnp.int32, (N_EXPERTS, t), 0)
    oh1 = (eiota == e1).astype(F32)
    oh2 = (eiota == e2).astype(F32)
    both = oh1 + oh2
    before = (lax.broadcasted_iota(jnp.int32, (t, t), 0)
              < lax.broadcasted_iota(jnp.int32, (t, t), 1)).astype(BF16)
    prior = jnp.dot(both.astype(BF16), before, preferred_element_type=F32) + cnt_scr[:, 0:1]
    rank_ref[0:1, :] = jnp.sum(oh1 * prior, axis=0, keepdims=True).astype(jnp.int32)
    rank_ref[1:2, :] = jnp.sum(oh2 * prior, axis=0, keepdims=True).astype(jnp.int32)
    cnt_scr[...] = cnt_scr[...] + jnp.sum(both, axis=1, keepdims=True)
    cnt_ref[...] = cnt_scr[...]


def _post_kernel(xp_ref, xs_ref, u_ref, uhalo_ref, sconv_ref, op_ref, os_ref, gates_ref,
                 gate1_ref, shift2_ref, scale2_ref,
                 cw_ref, cb_ref, lng_ref, lnb_ref, wpw2_ref, bpw2_ref, wao_ref, wout_ref,
                 n2_ref, wrt_ref, brt_ref,
                 x1_ref, h2_ref, eidx_ref, wtok_ref, rank_ref, cnt_ref,
                 uext, shift_scr, d_scr, s_scr, o_scr, mix_scr, h2b_scr, cnt_scr,
                 *, groups, prompt_tiles, tiles_per_seq):
    i = pl.program_id(0)
    d_model = xp_ref.shape[1]

    @pl.when(i == 0)
    def _():
        cnt_scr[...] = jnp.zeros(cnt_scr.shape, F32)
    nch = cw_ref.shape[0]
    cw = uext.shape[2]
    hist = uext.shape[1] - GROUP_ROWS
    lead = hist - (CONV_WIDTH - 1)

    def put_hist(g, rows):
        for c in range(nch):
            uext[g * nch + c, 0:hist, :] = rows[:, c * cw:(c + 1) * cw]

    @pl.when(i < prompt_tiles)
    def _():
        first = (i % tiles_per_seq) == 0
        put_hist(0, jnp.where(first, 0.0, uhalo_ref[...]))
        for g in range(1, groups):
            put_hist(g, u_ref[(g - 1) * GROUP_ROWS:g * GROUP_ROWS, :])
        o_scr[...] = op_ref[...]

    @pl.when(i >= prompt_tiles)
    def _():
        for g in range(groups):
            put_hist(g, sconv_ref[g])
        o_scr[...] = os_ref[...]

    for g in range(groups):
        for c in range(nch):
            uext[g * nch + c, hist:hist + GROUP_ROWS, :] = (
                u_ref[g * GROUP_ROWS:(g + 1) * GROUP_ROWS, c * cw:(c + 1) * cw])

    span = shift_scr.shape[1]

    def conv_chunk(k, carry):
        c = k % nch
        win_all = uext[k]
        for s in range(1, SUBLANES):
            shift_scr[s, :, :] = pltpu.roll(win_all, win_all.shape[0] - s, 0)[0:span, :]
        acc = None
        for j in range(CONV_WIDTH):
            base, s = divmod(lead + j, SUBLANES)
            rows = pl.ds(base * SUBLANES, GROUP_ROWS)
            win = uext[k, rows, :] if s == 0 else shift_scr[s, rows, :]
            term = win * cw_ref[c, j:j + 1, :]
            acc = term if acc is None else acc + term
        d_scr[k] = acc + cb_ref[c]
        return carry

    lax.fori_loop(0, groups * nch, conv_chunk, 0)

    for g in range(groups):
        dd = jnp.concatenate([d_scr[g * nch + c] for c in range(nch)], axis=1)
        mu = jnp.mean(dd, axis=-1, keepdims=True)
        var = jnp.mean(jnp.square(dd - mu), axis=-1, keepdims=True)
        y = (dd - mu) * lax.rsqrt(var + LN_EPS) * lng_ref[...] + lnb_ref[...]
        s_scr[g * GROUP_ROWS:(g + 1) * GROUP_ROWS, :] = (y * jax.nn.sigmoid(y)).astype(BF16)

    conv_out = jnp.dot(s_scr[...], wpw2_ref[...], preferred_element_type=F32) + bpw2_ref[...]
    attn_out = jnp.dot(o_scr[...], wao_ref[...], preferred_element_type=F32)
    merged = (gates_ref[:, :d_model].astype(F32) * attn_out
              + gates_ref[:, d_model:].astype(F32) * conv_out)
    mix_scr[...] = jnp.dot(merged.astype(BF16), wout_ref[...], preferred_element_type=F32)

    def residual(x_ref):
        def res_group(g, carry):
            r = pl.multiple_of(g * GROUP_ROWS, GROUP_ROWS)
            x1 = (x_ref[pl.ds(r, GROUP_ROWS), :]
                  + gate1_ref[pl.ds(g, 1), :] * mix_scr[pl.ds(r, GROUP_ROWS), :])
            x1_ref[pl.ds(r, GROUP_ROWS), :] = x1
            h = _rms_mod(x1, n2_ref[...], scale2_ref[pl.ds(g, 1), :], shift2_ref[pl.ds(g, 1), :])
            h2b_scr[pl.ds(r, GROUP_ROWS), :] = h.astype(BF16)
            h2_ref[pl.ds(r, GROUP_ROWS), :] = h
            return carry

        lax.fori_loop(0, groups, res_group, 0)

    pl.when(i < prompt_tiles)(lambda: residual(xp_ref))
    pl.when(i >= prompt_tiles)(lambda: residual(xs_ref))

    lt = lax.dot_general(wrt_ref[...], h2b_scr[...], (((1,), (1,)), ((), ())),
                         preferred_element_type=F32) + brt_ref[...]
    _route(lt, eidx_ref, wtok_ref, rank_ref, cnt_ref, cnt_scr)


def _post(x_p, x_s, u, sconv_pad, o_p, o_s, gates, gate1_g, shift2_g, scale2_g, p, n_prompt, seq, tm):
    d = x_p.shape[1]
    n = x_p.shape[0] + x_s.shape[0]
    cdim = u.shape[1]
    nch = cdim // CONV_CHUNK
    groups = tm // GROUP_ROWS
    prompt_tiles = n_prompt // tm
    const = lambda shape: pl.BlockSpec(shape, lambda i: (0,) * len(shape))
    row = lambda w: pl.BlockSpec((tm, w), lambda i: (i, 0))
    grp = pl.BlockSpec((groups, d), lambda i: (i, 0))
    return pl.pallas_call(
        functools.partial(_post_kernel, groups=groups, prompt_tiles=prompt_tiles,
                          tiles_per_seq=seq // tm),
        grid=(n // tm,),
        in_specs=_two_stream_specs(tm, d, prompt_tiles) + [
                  row(cdim),
                  pl.BlockSpec((GROUP_ROWS, cdim), lambda i: (jnp.maximum(i * groups - 1, 0), 0)),
                  pl.BlockSpec((groups, GROUP_ROWS, cdim),
                               lambda i: (jnp.maximum(i - prompt_tiles, 0), 0, 0))]
                 + _two_stream_specs(tm, ATTN_DIM, prompt_tiles) + [
                  row(2 * d), grp, grp, grp,
                  const((nch, CONV_WIDTH, CONV_CHUNK)), const((nch, 1, CONV_CHUNK)),
                  const((1, cdim)), const((1, cdim)),
                  const((cdim, d)), const((1, d)), const((ATTN_DIM, d)), const((d, d)),
                  const((1, d)), const((ROUTER_ROWS, d)), const((ROUTER_ROWS, 1))],
        out_specs=[row(d),
                   row(d),
                   pl.BlockSpec((2, tm), lambda i: (0, i)),
                   pl.BlockSpec((2, tm), lambda i: (0, i)),
                   pl.BlockSpec((2, tm), lambda i: (0, i)),
                   pl.BlockSpec((N_EXPERTS, LANES), lambda i: (0, 0))],
        out_shape=[jax.ShapeDtypeStruct((n, d), F32),
                   jax.ShapeDtypeStruct((n, d), F32),
                   jax.ShapeDtypeStruct((2, n), jnp.int32),
                   jax.ShapeDtypeStruct((2, n), F32),
                   jax.ShapeDtypeStruct((2, n), jnp.int32),
                   jax.ShapeDtypeStruct((N_EXPERTS, LANES), F32)],
        scratch_shapes=[pltpu.VMEM((groups * nch, 2 * GROUP_ROWS, CONV_CHUNK), F32),
                        pltpu.VMEM((SUBLANES, 2 * GROUP_ROWS - SUBLANES, CONV_CHUNK), F32),
                        pltpu.VMEM((groups * nch, GROUP_ROWS, CONV_CHUNK), F32),
                        pltpu.VMEM((tm, cdim), BF16),
                        pltpu.VMEM((tm, ATTN_DIM), BF16),
                        pltpu.VMEM((tm, d), F32),
                        pltpu.VMEM((tm, d), BF16),
                        pltpu.VMEM((N_EXPERTS, LANES), F32)],
        compiler_params=_cparams("arbitrary"),
        name="post",
    )(x_p, x_s, u, u, sconv_pad, o_p, o_s, gates, gate1_g, shift2_g, scale2_g,
      p["conv_w"], p["conv_b"], p["ln_g"], p["ln_b"], p["w_pw2"], p["b_pw2"], p["w_ao"], p["w_out"],
      p["n2"], p["w_rt"], p["b_rt"])


def _moe_kernel(blk_e_ref, nused_ref, src_ref, src_next_ref, dst_ref, roww_ref,
                h2_hbm, wg_ref, wu_ref, wd_ref, y_hbm,
                xbuf, obuf, wg_b, wu_b, wd_b, gsem, ssem, *, n_tok, n_pad):
    i = pl.program_id(0)
    nused = nused_ref[0]
    slot = i % 2

    def gather(idx_ref, sl):
        for r in range(MOE_BLOCK):
            pltpu.make_async_copy(h2_hbm.at[pl.ds(idx_ref[0, 0, r], 1), :],
                                  xbuf.at[sl, pl.ds(r, 1), :], gsem.at[sl]).start(priority=r % 2)

    def gather_wait(sl):
        pltpu.make_async_copy(h2_hbm.at[pl.ds(0, MOE_BLOCK), :], xbuf.at[sl], gsem.at[sl]).wait()

    def scatter_wait():
        pltpu.make_async_copy(obuf, y_hbm.at[pl.ds(0, MOE_BLOCK), :], ssem.at[0]).wait()

    @pl.when(i == 0)
    def _():
        obuf[...] = jnp.zeros(obuf.shape, F32)
        fills = [pltpu.make_async_copy(
            obuf, y_hbm.at[pl.ds(half * n_pad + n_tok + c * MOE_BLOCK, MOE_BLOCK), :], ssem.at[0])
                 for half in range(2) for c in range((n_pad - n_tok) // MOE_BLOCK)]
        for f in fills:
            f.start()
        for f in fills:
            f.wait()

    @pl.when(i < nused)
    def _():
        @pl.when(i == 0)
        def _():
            gather(src_ref, 0)

        @pl.when(i + 1 < nused)
        def _():
            gather(src_next_ref, 1 - slot)

        changed = jnp.logical_or(i == 0, blk_e_ref[i] != blk_e_ref[jnp.maximum(i - 1, 0)])

        @pl.when(changed)
        def _():
            wg_b[...] = wg_ref[0].astype(BF16)
            wu_b[...] = wu_ref[0].astype(BF16)
            wd_b[...] = wd_ref[0].astype(BF16)

        gather_wait(slot)
        x = xbuf[slot].astype(BF16)
        hg = jnp.dot(x, wg_b[...], preferred_element_type=F32)
        hu = jnp.dot(x, wu_b[...], preferred_element_type=F32)
        hid = (hg * jax.nn.sigmoid(hg) * hu).astype(BF16)
        y = jnp.dot(hid, wd_b[...], preferred_element_type=F32) * roww_ref[...]

        @pl.when(i > 0)
        def _():
            scatter_wait()

        obuf[...] = y
        for r in range(MOE_BLOCK):
            pltpu.make_async_copy(obuf.at[pl.ds(r, 1), :],
                                  y_hbm.at[pl.ds(dst_ref[0, 0, r], 1), :], ssem.at[0]).start(priority=r % 2)

        @pl.when(i == nused - 1)
        def _():
            scatter_wait()


def _moe(h2, blk_e, nused, src, dst, roww, w_gate, w_up, w_down, n_tok, n_pad):
    n_blocks = blk_e.shape[0]
    n_exp, d, de = w_gate.shape
    idx_spec = lambda f: pl.BlockSpec((1, 1, MOE_BLOCK), f, memory_space=pltpu.SMEM)
    last = n_blocks - 1
    grid_spec = pltpu.PrefetchScalarGridSpec(
        num_scalar_prefetch=2,
        grid=(n_blocks,),
        in_specs=[idx_spec(lambda i, be, nu: (i, 0, 0)),
                  idx_spec(lambda i, be, nu: (jnp.minimum(i + 1, last), 0, 0)),
                  idx_spec(lambda i, be, nu: (i, 0, 0)),
                  pl.BlockSpec((MOE_BLOCK, 1), lambda i, be, nu: (i, 0)),
                  pl.BlockSpec(memory_space=pl.ANY),
                  pl.BlockSpec((1, d, de), lambda i, be, nu: (be[i], 0, 0)),
                  pl.BlockSpec((1, d, de), lambda i, be, nu: (be[i], 0, 0)),
                  pl.BlockSpec((1, de, d), lambda i, be, nu: (be[i], 0, 0))],
        out_specs=pl.BlockSpec(memory_space=pl.ANY),
        scratch_shapes=[pltpu.VMEM((2, MOE_BLOCK, d), F32),
                        pltpu.VMEM((MOE_BLOCK, d), F32),
                        pltpu.VMEM((d, de), BF16), pltpu.VMEM((d, de), BF16), pltpu.VMEM((de, d), BF16),
                        pltpu.SemaphoreType.DMA((2,)), pltpu.SemaphoreType.DMA((1,))],
    )
    return pl.pallas_call(
        functools.partial(_moe_kernel, n_tok=n_tok, n_pad=n_pad),
        grid_spec=grid_spec,
        out_shape=jax.ShapeDtypeStruct((2 * n_pad, d), F32),
        compiler_params=_cparams("arbitrary"),
        name="moe",
    )(blk_e, nused, src, src, dst, roww, h2, w_gate, w_up, w_down)


def _dispatch(eidx, wtok, rank, counts, n, n_pad):
    a_tot = 2 * n
    experts = jnp.arange(N_EXPERTS, dtype=jnp.int32)
    padded = (counts + MOE_BLOCK - 1) // MOE_BLOCK * MOE_BLOCK
    pad_end = jnp.sum(jnp.where(experts[None, :] <= experts[:, None], padded[None, :], 0), axis=1)
    pad_start = pad_end - padded
    start_of = jnp.sum(jnp.where(eidx[:, :, None] == experts, pad_start, 0), axis=-1)
    dest = (start_of + rank).reshape(-1)
    n_blocks = -(-(a_tot + N_EXPERTS * (MOE_BLOCK - 1)) // MOE_BLOCK)
    n_rows = n_blocks * MOE_BLOCK
    tok = jnp.tile(jnp.arange(n, dtype=jnp.int32), 2)
    slot = jnp.repeat(jnp.arange(2, dtype=jnp.int32), n)
    upd = jnp.stack([tok, slot * n_pad + tok,
                     lax.bitcast_convert_type(wtok.reshape(-1), jnp.int32)], axis=1)
    trash = n + jnp.arange(n_rows, dtype=jnp.int32) % MOE_BLOCK
    init = jnp.stack([jnp.zeros((n_rows,), jnp.int32), trash, jnp.zeros((n_rows,), jnp.int32)], axis=1)
    rows = init.at[dest].set(upd, unique_indices=True)
    blk_start = jnp.arange(n_blocks, dtype=jnp.int32) * MOE_BLOCK
    blk_e = jnp.minimum(jnp.sum((pad_end[None, :] <= blk_start[:, None]).astype(jnp.int32), axis=1),
                        N_EXPERTS - 1)
    nused = (pad_end[-1] // MOE_BLOCK).reshape(1)
    shp = (n_blocks, 1, MOE_BLOCK)
    roww = lax.bitcast_convert_type(rows[:, 2], F32)
    return blk_e, nused, rows[:, 0].reshape(shp), rows[:, 1].reshape(shp), roww.reshape(n_rows, 1)


def _final_kernel(x1_ref, y0_ref, y1_ref, gate2_ref, fg_ref, op_ref, os_ref, *, groups, prompt_tiles):
    def run(o_ref):
        def body(g, carry):
            r = pl.multiple_of(g * GROUP_ROWS, GROUP_ROWS)
            moe = y0_ref[pl.ds(r, GROUP_ROWS), :] + y1_ref[pl.ds(r, GROUP_ROWS), :]
            x2 = x1_ref[pl.ds(r, GROUP_ROWS), :] + gate2_ref[pl.ds(g, 1), :] * moe
            ms = jnp.mean(x2 * x2, axis=-1, keepdims=True)
            o_ref[pl.ds(r, GROUP_ROWS), :] = x2 * lax.rsqrt(ms + RMS_EPS) * fg_ref[...]
            return carry

        lax.fori_loop(0, groups, body, 0)

    i = pl.program_id(0)
    pl.when(i < prompt_tiles)(lambda: run(op_ref))
    pl.when(i >= prompt_tiles)(lambda: run(os_ref))


def _final(x1, y2, gate2_g, final_g, n_prompt, n_pad, tm):
    n, d = x1.shape
    groups = tm // GROUP_ROWS
    off = n_pad // tm
    prompt_tiles = n_prompt // tm
    return pl.pallas_call(
        functools.partial(_final_kernel, groups=groups, prompt_tiles=prompt_tiles),
        grid=(n // tm,),
        in_specs=[pl.BlockSpec((tm, d), lambda i: (i, 0)),
                  pl.BlockSpec((tm, d), lambda i: (i, 0)),
                  pl.BlockSpec((tm, d), lambda i: (off + i, 0)),
                  pl.BlockSpec((groups, d), lambda i: (i, 0)),
                  pl.BlockSpec((1, d), lambda i: (0, 0))],
        out_specs=_two_stream_specs(tm, d, prompt_tiles),
        out_shape=[jax.ShapeDtypeStruct((n_prompt, d), F32),
                   jax.ShapeDtypeStruct((n - n_prompt, d), F32)],
        compiler_params=_cparams("arbitrary"),
        name="final",
    )(x1, y2, y2, gate2_g, final_g.reshape(1, d))


def _layer(x_p, x_s, c_all, cache_k, cache_v, sconv, rel_table, lp, seq, t_new):
    (n1, n2, w_ada, b_ada, w_in, sink, w_ao, conv_w, conv_b, ln_g, ln_b, w_pw2, b_pw2, w_out,
     w_grp, b_grp, w_rt, b_rt, w_gate, w_up, w_down) = lp
    n_prompt, d = x_p.shape
    n = n_prompt + x_s.shape[0]
    batch = n_prompt // seq
    dec_batch = (n - n_prompt) // t_new
    cdim = conv_w.shape[1]
    tm = 256
    n_pad = n + tm

    c_rows = -(-c_all.shape[0] // 8) * 8
    c_pad = jnp.pad(c_all, ((0, c_rows - c_all.shape[0]), (0, 0)))
    mods = _ada(c_pad, w_ada, b_ada)

    def per_group(k):
        m = mods[:, k * d:(k + 1) * d]
        mp = jnp.broadcast_to(m[:batch, None, :], (batch, seq // GROUP_ROWS, d))
        return jnp.concatenate([mp.reshape(batch * (seq // GROUP_ROWS), d), m[batch:batch + dec_batch]],
                               axis=0)

    shift1, scale1, gate1, shift2, scale2, gate2 = [per_group(k) for k in range(6)]

    h1 = _norm1(x_p, x_s, n1, scale1, shift1, 512)

    k_off = ATTN_DIM
    glu_off = k_off + 2 * KV_DIM
    gate_off = glu_off + 2 * cdim
    w_in_b = w_in.astype(BF16)
    tglu = 256
    w_glu = jnp.stack([w_in_b[:, glu_off:glu_off + cdim].reshape(d, cdim // tglu, tglu),
                       w_in_b[:, glu_off + cdim:gate_off].reshape(d, cdim // tglu, tglu)],
                      axis=2).reshape(d, 2 * cdim)
    tmm = 1536
    (q,) = _proj(h1, w_in_b[:, :k_off], "qkv", tmm, 512, [BF16])
    (kv,) = _proj(h1, w_in_b[:, k_off:glu_off], "qkv", tmm, 512, [BF16])
    kv_state = _kv_state(h1, w_in_b[:, k_off:glu_off], n_prompt, seq)
    (u,) = _proj(h1, w_glu, "glu", tmm, 2 * tglu, [F32])
    (gates,) = _proj(h1, w_in_b[:, gate_off:], "gate", tmm, 512, [BF16])

    bias_p = _pair_bias(rel_table)
    buf = cache_k.shape[1]
    kpos = jnp.concatenate([jnp.arange(buf, dtype=jnp.int32) - buf, jnp.arange(t_new, dtype=jnp.int32)])
    bias_s = _rel_bias(rel_table, kpos[None, :] - jnp.arange(t_new, dtype=jnp.int32)[:, None])
    sink_f = sink.astype(F32).reshape(N_KV_HEADS, Q_PER_KV, 1, 1)
    sink_kch = sink.astype(F32).reshape(N_KV_HEADS, 1, COLS_PER_KV, HEADS_PER_COL).transpose(0, 3, 1, 2)
    sink_p = jnp.broadcast_to(sink_kch[..., None, None],
                              (N_KV_HEADS, HEADS_PER_COL, 2, COLS_PER_KV, CHUNK, LANES)).reshape(
        N_KV_HEADS, HEADS_PER_COL, 2 * COLS_PER_KV * CHUNK, LANES)
    sink_s = jnp.broadcast_to(sink_f, (N_KV_HEADS, Q_PER_KV, t_new, 1)).reshape(N_KV_HEADS, Q_PER_KV * t_new, 1)
    o_p = _attn_prompt(q, kv, bias_p, sink_p, n_prompt, seq, 2)
    o_s = _attn_sample(q, kv, cache_k.reshape(dec_batch, buf, KV_DIM), cache_v.reshape(dec_batch, buf, KV_DIM),
                       bias_s, sink_s, n_prompt, t_new)

    hist = GROUP_ROWS
    sconv_pad = jnp.pad(sconv, ((0, 0), (hist - sconv.shape[1], 0), (0, 0)))
    w_rt_t = jnp.zeros((ROUTER_ROWS, d), F32)
    w_rt_t = w_rt_t.at[:N_GROUPS].set(w_grp.T).at[EXPERT_ROW0:EXPERT_ROW0 + N_EXPERTS].set(w_rt.T)
    b_rt_t = jnp.zeros((ROUTER_ROWS, 1), F32)
    b_rt_t = b_rt_t.at[:N_GROUPS, 0].set(b_grp.astype(F32)).at[EXPERT_ROW0:EXPERT_ROW0 + N_EXPERTS, 0].set(
        b_rt.astype(F32))
    nch = cdim // CONV_CHUNK
    conv_w_c = conv_w.reshape(CONV_WIDTH, nch, CONV_CHUNK).transpose(1, 0, 2)
    params = dict(conv_w=conv_w_c, conv_b=conv_b.reshape(nch, 1, CONV_CHUNK), ln_g=ln_g.reshape(1, cdim),
                  ln_b=ln_b.reshape(1, cdim), w_pw2=w_pw2.astype(BF16), b_pw2=b_pw2.reshape(1, d),
                  w_ao=w_ao.astype(BF16), w_out=w_out.astype(BF16), n2=n2.reshape(1, d),
                  w_rt=w_rt_t.astype(BF16), b_rt=b_rt_t)
    x1, h2_2d, eidx, wtok, rank, cnt = _post(x_p, x_s, u, sconv_pad, o_p, o_s, gates, gate1, shift2, scale2,
                                             params, n_prompt, seq, tm)

    blk_e, nused, src, dst, roww = _dispatch(eidx, wtok, rank, cnt[:, 0].astype(jnp.int32), n, n_pad)
    y = _moe(h2_2d, blk_e, nused, src, dst, roww, w_gate, w_up, w_down, n, n_pad)
    return x1, y, gate2, kv_state, u


def kernel(x_prompt, x_sample, c_prompt, c_sample, cache_k, cache_v, state_conv, rel_bias_table, norm1_g, norm2_g, w_ada, b_ada, w_in, attn_sink, w_attn_o, conv_w, conv_b, conv_ln_g, conv_ln_b, w_pw2, b_pw2, w_out, w_group, b_group, w_router, b_router, w_gate, w_up, w_down, final_g):
    batch, seq, d = x_prompt.shape
    dec_batch, t_new, _ = x_sample.shape
    depth = norm1_g.shape[0]
    assert depth == 1, "single trunk layer"
    assert t_new == GROUP_ROWS and seq % GROUP_ROWS == 0
    n_prompt = batch * seq
    n = n_prompt + dec_batch * t_new
    tm = 256
    n_pad = n + tm

    x_p = x_prompt.reshape(n_prompt, d)
    x_s = x_sample.reshape(dec_batch * t_new, d)
    c_all = jnp.concatenate([c_prompt, c_sample], axis=0)
    l = 0
    lp = (norm1_g[l], norm2_g[l], w_ada[l], b_ada[l], w_in[l], attn_sink[l], w_attn_o[l], conv_w[l],
          conv_b[l], conv_ln_g[l], conv_ln_b[l], w_pw2[l], b_pw2[l], w_out[l], w_group[l], b_group[l],
          w_router[l], b_router[l], w_gate[l], w_up[l], w_down[l])
    x1, y, gate2, kv_state, u = _layer(x_p, x_s, c_all, cache_k[l], cache_v[l], state_conv[l], rel_bias_table,
                                       lp, seq, t_new)
    out_p, out_s = _final(x1, y, gate2, final_g, n_prompt, n_pad, tm)

    y_prompt = out_p.reshape(batch, seq, d)
    y_sample = out_s.reshape(dec_batch, t_new, d)
    cdim = u.shape[1]
    kvp = kv_state[:batch * WINDOW].reshape(batch, WINDOW, 2 * KV_DIM)
    new_k_prompt = kvp[..., :KV_DIM].reshape(1, batch, WINDOW, N_KV_HEADS, HEAD_DIM)
    new_v_prompt = kvp[..., KV_DIM:].reshape(1, batch, WINDOW, N_KV_HEADS, HEAD_DIM)
    new_conv_prompt = u[:n_prompt].reshape(batch, seq, cdim)[:, -(CONV_WIDTH - 1):][None]
    kvs = kv_state[batch * WINDOW:].reshape(dec_batch, t_new, 2 * KV_DIM)
    buf = cache_k.shape[2]
    k_new = kvs[..., :KV_DIM].reshape(dec_batch, t_new, N_KV_HEADS, HEAD_DIM)
    v_new = kvs[..., KV_DIM:].reshape(dec_batch, t_new, N_KV_HEADS, HEAD_DIM)
    new_k_sample = jnp.concatenate([cache_k[l], k_new], axis=1)[:, -buf:][None]
    new_v_sample = jnp.concatenate([cache_v[l], v_new], axis=1)[:, -buf:][None]
    us = u[n_prompt:].reshape(dec_batch, t_new, cdim)
    new_conv_sample = jnp.concatenate([state_conv[l], us], axis=1)[:, -(CONV_WIDTH - 1):][None]
    return (y_prompt, y_sample, new_k_prompt, new_v_prompt, new_conv_prompt,
            new_k_sample, new_v_sample, new_conv_sample)
```

```python
import functools
import math

import jax
import jax.numpy as jnp
from jax import lax
from jax.experimental import pallas as pl
from jax.experimental.pallas import tpu as pltpu

F32 = jnp.float32
BF16 = jnp.bfloat16

CHUNK = 64
HEAD_DIM = 64
N_Q_HEADS = 16
N_KV_HEADS = 4
Q_PER_KV = N_Q_HEADS // N_KV_HEADS
ATTN_DIM = N_Q_HEADS * HEAD_DIM
KV_DIM = N_KV_HEADS * HEAD_DIM
WINDOW = 128
WIN_CHUNKS = WINDOW // CHUNK
BAND = (WIN_CHUNKS + 1) * CHUNK
CONV_WIDTH = 31
NUM_BUCKETS = 32
MAX_DISTANCE = 128
N_GROUPS = 4
EXPERTS_PER_GROUP = 8
N_EXPERTS = N_GROUPS * EXPERTS_PER_GROUP
MOE_BLOCK = 128
RMS_EPS = 1e-6
LN_EPS = 1e-5
NEG_INF = -1e30

GROUP_ROWS = 32
LANES = 128
SUBLANES = 8
CONV_CHUNK = 256
ROUTER_ROWS = 64
EXPERT_ROW0 = 8
V7X_VMEM_LIMIT = 56 * 1024 * 1024


def _cparams(*sem):
    return pltpu.CompilerParams(dimension_semantics=sem, vmem_limit_bytes=V7X_VMEM_LIMIT)


def _ada_kernel(c_ref, w_ref, b_ref, o_ref):
    c = c_ref[...]
    s = c * jax.nn.sigmoid(c)
    o_ref[...] = jnp.dot(s.astype(BF16), w_ref[...].astype(BF16),
                         preferred_element_type=F32) + b_ref[...]


def _ada(c_all, w_ada, b_ada):
    rows, d = c_all.shape
    ncol = w_ada.shape[1]
    tn = 1024
    return pl.pallas_call(
        _ada_kernel,
        grid=(ncol // tn,),
        in_specs=[pl.BlockSpec((rows, d), lambda j: (0, 0)),
                  pl.BlockSpec((d, tn), lambda j: (0, j)),
                  pl.BlockSpec((1, tn), lambda j: (0, j))],
        out_specs=pl.BlockSpec((rows, tn), lambda j: (0, j)),
        out_shape=jax.ShapeDtypeStruct((rows, ncol), F32),
        compiler_params=_cparams("arbitrary"),
        name="ada",
    )(c_all, w_ada, b_ada.reshape(1, ncol))


def _rms_mod(x, gain, scale, shift):
    ms = jnp.mean(x * x, axis=-1, keepdims=True)
    y = x * lax.rsqrt(ms + RMS_EPS) * gain
    return y * (1.0 + scale) + shift


def _two_stream_specs(tm, width, prompt_tiles):
    return [pl.BlockSpec((tm, width), lambda i: (jnp.minimum(i, prompt_tiles - 1), 0)),
            pl.BlockSpec((tm, width), lambda i: (jnp.maximum(i - prompt_tiles, 0), 0))]


def _norm1_kernel(xp_ref, xs_ref, g_ref, sc_ref, sh_ref, h_ref, *, groups, prompt_tiles):
    def run(x_ref):
        def body(gi, carry):
            r = pl.multiple_of(gi * GROUP_ROWS, GROUP_ROWS)
            h = _rms_mod(x_ref[pl.ds(r, GROUP_ROWS), :], g_ref[...],
                         sc_ref[pl.ds(gi, 1), :], sh_ref[pl.ds(gi, 1), :])
            h_ref[pl.ds(r, GROUP_ROWS), :] = h.astype(h_ref.dtype)
            return carry

        lax.fori_loop(0, groups, body, 0)

    i = pl.program_id(0)
    pl.when(i < prompt_tiles)(lambda: run(xp_ref))
    pl.when(i >= prompt_tiles)(lambda: run(xs_ref))


def _norm1(x_p, x_s, gain, scale_g, shift_g, tm):
    d = x_p.shape[1]
    n = x_p.shape[0] + x_s.shape[0]
    groups = tm // GROUP_ROWS
    prompt_tiles = x_p.shape[0] // tm
    return pl.pallas_call(
        functools.partial(_norm1_kernel, groups=groups, prompt_tiles=prompt_tiles),
        grid=(n // tm,),
        in_specs=_two_stream_specs(tm, d, prompt_tiles) + [
            pl.BlockSpec((1, d), lambda i: (0, 0)),
            pl.BlockSpec((groups, d), lambda i: (i, 0)),
            pl.BlockSpec((groups, d), lambda i: (i, 0))],
        out_specs=pl.BlockSpec((tm, d), lambda i: (i, 0)),
        out_shape=jax.ShapeDtypeStruct((n, d), BF16),
        compiler_params=_cparams("arbitrary"),
        name="norm1",
    )(x_p, x_s, gain.reshape(1, d), scale_g, shift_g)


def _proj_kernel(h_ref, w_ref, *o_refs, mode):
    acc = jnp.dot(h_ref[...], w_ref[...], preferred_element_type=F32)
    if mode == "qkv":
        o_refs[0][...] = acc.astype(BF16)
    elif mode == "glu":
        half = acc.shape[1] // 2
        o_refs[0][...] = acc[:, :half] * jax.nn.sigmoid(acc[:, half:])
    else:
        o_refs[0][...] = jax.nn.sigmoid(acc).astype(BF16)


def _proj(h, w, mode, tm, tnw, out_dtypes):
    n, d = h.shape
    ncol = w.shape[1]
    tno = tnw // 2 if mode == "glu" else tnw
    nout = ncol // 2 if mode == "glu" else ncol
    outs = pl.pallas_call(
        functools.partial(_proj_kernel, mode=mode),
        grid=(n // tm, ncol // tnw),
        in_specs=[pl.BlockSpec((tm, d), lambda i, j: (i, 0)),
                  pl.BlockSpec((d, tnw), lambda i, j: (0, j))],
        out_specs=[pl.BlockSpec((tm, tno), lambda i, j: (i, j)) for _ in out_dtypes],
        out_shape=[jax.ShapeDtypeStruct((n, nout), dt) for dt in out_dtypes],
        compiler_params=_cparams("arbitrary", "arbitrary"),
        name="proj_" + mode,
    )(h, w)
    return outs


def _kv_state_kernel(h_ref, w_ref, o_ref):
    o_ref[...] = jnp.dot(h_ref[...], w_ref[...], preferred_element_type=F32)


def _kv_state(h, w_kv, n_prompt, seq):
    n, d = h.shape
    batch = n_prompt // seq
    per_seq = seq // WINDOW
    steps = batch + (n - n_prompt) // WINDOW

    def row_block(i):
        return jnp.where(i < batch, (i + 1) * per_seq - 1, n_prompt // WINDOW + i - batch)

    return pl.pallas_call(
        _kv_state_kernel,
        grid=(steps,),
        in_specs=[pl.BlockSpec((WINDOW, d), lambda i: (row_block(i), 0)),
                  pl.BlockSpec((d, 2 * KV_DIM), lambda i: (0, 0))],
        out_specs=pl.BlockSpec((WINDOW, 2 * KV_DIM), lambda i: (i, 0)),
        out_shape=jax.ShapeDtypeStruct((steps * WINDOW, 2 * KV_DIM), F32),
        compiler_params=_cparams("arbitrary"),
        name="kv_state",
    )(h, w_kv)


def _t5_bucket(rel):
    nb = NUM_BUCKETS // 2
    n = -rel
    ret = jnp.where(n < 0, nb, 0)
    n = jnp.abs(n)
    max_exact = nb // 2
    nf = jnp.maximum(n, 1).astype(F32)
    large = max_exact + (jnp.log(nf / max_exact) / math.log(MAX_DISTANCE / max_exact)
                         * (nb - max_exact)).astype(jnp.int32)
    large = jnp.minimum(large, nb - 1)
    return ret + jnp.where(n < max_exact, n, large)


def _bias_kernel(tbl_ref, bkt_ref, o_ref):
    bkt = bkt_ref[...]
    nq = bkt.shape[0]
    for k in range(N_KV_HEADS):
        for g in range(Q_PER_KV):
            acc = jnp.zeros(bkt.shape, F32)
            for b in range(NUM_BUCKETS):
                acc = jnp.where(bkt == b, tbl_ref[b, k * Q_PER_KV + g], acc)
            o_ref[k, g * nq:(g + 1) * nq, :] = acc


def _rel_bias(rel_table, rel):
    nq, nk = rel.shape
    return pl.pallas_call(
        _bias_kernel,
        in_specs=[pl.BlockSpec(memory_space=pltpu.SMEM),
                  pl.BlockSpec((nq, nk), lambda: (0, 0))],
        out_specs=pl.BlockSpec((N_KV_HEADS, Q_PER_KV * nq, nk), lambda: (0, 0, 0)),
        out_shape=jax.ShapeDtypeStruct((N_KV_HEADS, Q_PER_KV * nq, nk), F32),
        name="rel_bias",
    )(rel_table.astype(F32), _t5_bucket(rel))


PAIR_BAND = 2 * CHUNK + WINDOW
HEADS_PER_COL = LANES // HEAD_DIM
COLS_PER_KV = Q_PER_KV // HEADS_PER_COL


def _pair_bias_kernel(tbl_ref, bkt_ref, o_ref):
    for e in range(2):
        bkt = bkt_ref[e]
        for k in range(N_KV_HEADS):
            for col in range(COLS_PER_KV):
                for half in range(HEADS_PER_COL):
                    head = k * Q_PER_KV + col * HEADS_PER_COL + half
                    acc = jnp.full(bkt.shape, NEG_INF, F32)
                    for b in range(NUM_BUCKETS):
                        acc = jnp.where(bkt == b, tbl_ref[b, head], acc)
                    r0 = (e * COLS_PER_KV + col) * CHUNK
                    o_ref[k, r0:r0 + CHUNK, half * PAIR_BAND:(half + 1) * PAIR_BAND] = acc


def _pair_bias(rel_table):
    qi = jnp.arange(CHUNK, dtype=jnp.int32)[None, :, None]
    kj = jnp.arange(PAIR_BAND, dtype=jnp.int32)[None, None, :]
    e = jnp.arange(2, dtype=jnp.int32)[:, None, None]
    rel = (kj - WINDOW) - (e * CHUNK + qi)
    key_chunk = kj // CHUNK - e
    seen = (key_chunk >= 0) & (key_chunk <= WIN_CHUNKS)
    bkt = jnp.where(seen, _t5_bucket(rel), -1)
    rows = 2 * COLS_PER_KV * CHUNK
    return pl.pallas_call(
        _pair_bias_kernel,
        in_specs=[pl.BlockSpec(memory_space=pltpu.SMEM),
                  pl.BlockSpec((2, CHUNK, PAIR_BAND), lambda: (0, 0, 0))],
        out_specs=pl.BlockSpec((N_KV_HEADS, rows, HEADS_PER_COL * PAIR_BAND), lambda: (0, 0, 0)),
        out_shape=jax.ShapeDtypeStruct((N_KV_HEADS, rows, HEADS_PER_COL * PAIR_BAND), F32),
        name="pair_bias",
    )(rel_table.astype(F32), bkt)


def _attend(qc, kk_all, vv_all, bias_ref, sink_ref, mask_thr):
    nq = qc.shape[0]
    nk = kk_all.shape[0]
    pieces = []
    for k in range(N_KV_HEADS):
        qs = jnp.concatenate(
            [qc[:, (k * Q_PER_KV + g) * HEAD_DIM:(k * Q_PER_KV + g + 1) * HEAD_DIM]
             for g in range(Q_PER_KV)], axis=0)
        kk = kk_all[:, k * HEAD_DIM:(k + 1) * HEAD_DIM]
        vv = vv_all[:, k * HEAD_DIM:(k + 1) * HEAD_DIM]
        lg = lax.dot_general(qs, kk, (((1,), (1,)), ((), ())),
                             preferred_element_type=F32) * (HEAD_DIM ** -0.5) + bias_ref[k]
        if mask_thr is not None:
            col = lax.broadcasted_iota(jnp.int32, (Q_PER_KV * nq, nk), 1)
            lg = jnp.where(col < mask_thr, NEG_INF, lg)
        s = sink_ref[k]
        m = jnp.maximum(jnp.max(lg, axis=-1, keepdims=True), s)
        p = jnp.exp(lg - m)
        den = jnp.sum(p, axis=-1, keepdims=True) + jnp.exp(s - m)
        o = jnp.dot(p.astype(BF16), vv, preferred_element_type=F32) / den
        pieces.extend(o[g * nq:(g + 1) * nq, :] for g in range(Q_PER_KV))
    return jnp.concatenate(pieces, axis=1)


def _attn_prompt_kernel(q_ref, kv_ref, halo_ref, bias_ref, sink_ref, o_ref, ka, kb, va, vb, *, pairs):
    t = pl.program_id(1)
    kvcat = jnp.concatenate([halo_ref[...], kv_ref[...]], axis=0)
    low = lax.broadcasted_iota(jnp.int32, (kvcat.shape[0], LANES), 1) < HEAD_DIM
    zero = jnp.zeros((kvcat.shape[0], LANES), BF16)
    ones_low = jnp.where(low, 1.0, 0.0).astype(BF16)
    ones_high = jnp.where(low, 0.0, 1.0).astype(BF16)
    kv_cols = KV_DIM // LANES
    for col in range(2 * kv_cols):
        x = kvcat[:, col * LANES:(col + 1) * LANES]
        xs = jnp.concatenate([x[:, HEAD_DIM:], x[:, :HEAD_DIM]], axis=1)
        dst_a, dst_b = (ka, kb) if col < kv_cols else (va, vb)
        k0 = (col % kv_cols) * HEADS_PER_COL
        dst_a[k0, :, 0:LANES] = jnp.where(low, x, zero)
        dst_b[k0, :, 0:LANES] = jnp.where(low, zero, xs)
        dst_a[k0 + 1, :, 0:LANES] = jnp.where(low, xs, zero)
        dst_b[k0 + 1, :, 0:LANES] = jnp.where(low, zero, x)
    for k in range(N_KV_HEADS):
        va[k, :, LANES:2 * LANES] = ones_low
        vb[k, :, LANES:2 * LANES] = ones_high

    out_low = lax.broadcasted_iota(jnp.int32, (2 * COLS_PER_KV * CHUNK, LANES), 1) < HEAD_DIM
    for p in range(pairs):
        r0 = p * 2 * CHUNK
        for k in range(N_KV_HEADS):
            lhs = jnp.concatenate(
                [q_ref[r0 + e * CHUNK:r0 + (e + 1) * CHUNK, (k * COLS_PER_KV + c) * LANES:(k * COLS_PER_KV + c + 1) * LANES]
                 for e in range(2) for c in range(COLS_PER_KV)], axis=0)
            keys = jnp.concatenate([ka[k, r0:r0 + PAIR_BAND, :], kb[k, r0:r0 + PAIR_BAND, :]], axis=0)
            lg = lax.dot_general(lhs, keys, (((1,), (1,)), ((), ())),
                                 preferred_element_type=F32) * (HEAD_DIM ** -0.5) + bias_ref[k]
            probs, sink_terms = [], []
            for half in range(HEADS_PER_COL):
                seg = lg[:, half * PAIR_BAND:(half + 1) * PAIR_BAND]
                if p == 0:
                    kcol = lax.broadcasted_iota(jnp.int32, seg.shape, 1)
                    seg = jnp.where(kcol < jnp.where(t == 0, WINDOW, 0), NEG_INF, seg)
                s = sink_ref[k, half]
                folded = seg[:, :LANES]
                for j in range(1, PAIR_BAND // LANES):
                    folded = jnp.maximum(folded, seg[:, j * LANES:(j + 1) * LANES])
                m = jnp.maximum(jnp.max(folded, axis=-1, keepdims=True), s)
                pr = jnp.exp(seg - jnp.concatenate([m] * (PAIR_BAND // LANES), axis=1))
                sink_terms.append(jnp.exp(s - m))
                probs.append(pr.astype(BF16))
            vals = jnp.concatenate([va[k, r0:r0 + PAIR_BAND, :], vb[k, r0:r0 + PAIR_BAND, :]], axis=0)
            oe = jnp.dot(jnp.concatenate(probs, axis=1), vals, preferred_element_type=F32)
            o = oe[:, :LANES] / (oe[:, LANES:] + jnp.where(out_low, sink_terms[0], sink_terms[1]))
            for e in range(2):
                for c in range(COLS_PER_KV):
                    rr = (e * COLS_PER_KV + c) * CHUNK
                    o_ref[r0 + e * CHUNK:r0 + (e + 1) * CHUNK,
                          (k * COLS_PER_KV + c) * LANES:(k * COLS_PER_KV + c + 1) * LANES] = (
                        o[rr:rr + CHUNK, :].astype(o_ref.dtype))


def _attn_prompt(q, kv, bias, sink_rows, n_prompt, seq, pairs):
    rows = pairs * 2 * CHUNK
    tiles = seq // rows
    halo_per_tile = rows // WINDOW
    keys = WINDOW + rows
    return pl.pallas_call(
        functools.partial(_attn_prompt_kernel, pairs=pairs),
        grid=(n_prompt // seq, tiles),
        scratch_shapes=[pltpu.VMEM((N_KV_HEADS, keys, LANES), BF16) for _ in range(2)]
        + [pltpu.VMEM((N_KV_HEADS, keys, 2 * LANES), BF16) for _ in range(2)],
        in_specs=[pl.BlockSpec((rows, ATTN_DIM), lambda b, t: (b * tiles + t, 0)),
                  pl.BlockSpec((rows, 2 * KV_DIM), lambda b, t: (b * tiles + t, 0)),
                  pl.BlockSpec((WINDOW, 2 * KV_DIM),
                               lambda b, t: (jnp.maximum((b * tiles + t) * halo_per_tile - 1, 0), 0)),
                  pl.BlockSpec(bias.shape, lambda b, t: (0, 0, 0)),
                  pl.BlockSpec(sink_rows.shape, lambda b, t: (0, 0, 0, 0))],
        out_specs=pl.BlockSpec((rows, ATTN_DIM), lambda b, t: (b * tiles + t, 0)),
        out_shape=jax.ShapeDtypeStruct((n_prompt, ATTN_DIM), BF16),
        compiler_params=_cparams("arbitrary", "arbitrary"),
        name="attn_prompt",
    )(q, kv, kv, bias, sink_rows)


def _attn_sample_kernel(q_ref, kv_ref, ck_ref, cv_ref, bias_ref, sink_ref, o_ref):
    kv = kv_ref[...]
    kk = jnp.concatenate([ck_ref[0].astype(BF16), kv[:, :KV_DIM]], axis=0)
    vv = jnp.concatenate([cv_ref[0].astype(BF16), kv[:, KV_DIM:]], axis=0)
    o = _attend(q_ref[...], kk, vv, bias_ref, sink_ref, None)
    o_ref[...] = o.astype(o_ref.dtype)


def _attn_sample(q, kv, cache_k, cache_v, bias, sink_rows, n_prompt, t_new):
    dec_batch, buf, _ = cache_k.shape
    first = n_prompt // t_new
    return pl.pallas_call(
        _attn_sample_kernel,
        grid=(dec_batch,),
        in_specs=[pl.BlockSpec((t_new, ATTN_DIM), lambda s: (first + s, 0)),
                  pl.BlockSpec((t_new, 2 * KV_DIM), lambda s: (first + s, 0)),
                  pl.BlockSpec((1, buf, KV_DIM), lambda s: (s, 0, 0)),
                  pl.BlockSpec((1, buf, KV_DIM), lambda s: (s, 0, 0)),
                  pl.BlockSpec(bias.shape, lambda s: (0, 0, 0)),
                  pl.BlockSpec(sink_rows.shape, lambda s: (0, 0, 0))],
        out_specs=pl.BlockSpec((t_new, ATTN_DIM), lambda s: (s, 0)),
        out_shape=jax.ShapeDtypeStruct((dec_batch * t_new, ATTN_DIM), BF16),
        compiler_params=_cparams("arbitrary"),
        name="attn_sample",
    )(q, kv, cache_k, cache_v, bias, sink_rows)


def _route(lt, eidx_ref, wtok_ref, rank_ref, cnt_ref, cnt_scr):
    gl = [lt[r:r + 1, :] for r in range(N_GROUPS)]
    gmax = gl[0]
    gsel = jnp.zeros(gl[0].shape, jnp.int32)
    for r in range(1, N_GROUPS):
        better = gl[r] > gmax
        gsel = jnp.where(better, r, gsel)
        gmax = jnp.maximum(gmax, gl[r])
    gexp = [jnp.exp(v - gmax) for v in gl]
    gsum = gexp[0]
    for r in range(1, N_GROUPS):
        gsum = gsum + gexp[r]
    psel = jnp.zeros(gl[0].shape, F32)
    for r in range(N_GROUPS):
        psel = jnp.where(gsel == r, gexp[r] / gsum, psel)
    el = jnp.zeros((EXPERTS_PER_GROUP, lt.shape[1]), F32)
    for r in range(N_GROUPS):
        lo = EXPERT_ROW0 + r * EXPERTS_PER_GROUP
        el = jnp.where(gsel == r, lt[lo:lo + EXPERTS_PER_GROUP, :], el)
    emax = jnp.max(el, axis=0, keepdims=True)
    ee = jnp.exp(el - emax)
    pin = ee / jnp.sum(ee, axis=0, keepdims=True)
    idx = lax.broadcasted_iota(jnp.int32, pin.shape, 0)
    p1 = jnp.max(pin, axis=0, keepdims=True)
    i1 = jnp.min(jnp.where(pin == p1, idx, EXPERTS_PER_GROUP), axis=0, keepdims=True)
    rest = jnp.where(idx == i1, -1.0, pin)
    p2 = jnp.max(rest, axis=0, keepdims=True)
    i2 = jnp.min(jnp.where(rest == p2, idx, EXPERTS_PER_GROUP), axis=0, keepdims=True)
    tot = p1 + p2
    e1 = gsel * EXPERTS_PER_GROUP + i1
    e2 = gsel * EXPERTS_PER_GROUP + i2
    eidx_ref[0:1, :] = e1
    eidx_ref[1:2, :] = e2
    wtok_ref[0:1, :] = psel * p1 / tot
    wtok_ref[1:2, :] = psel * p2 / tot

    t = lt.shape[1]
    eiota = lax.broadcasted_iota(jnp.int32, (N_EXPERTS, t), 0)
    oh1 = (eiota == e1).astype(F32)
    oh2 = (eiota == e2).astype(F32)
    both = oh1 + oh2
    before = (lax.broadcasted_iota(jnp.int32, (t, t), 0)
              < lax.broadcasted_iota(jnp.int32, (t, t), 1)).astype(BF16)
    prior = jnp.dot(both.astype(BF16), before, preferred_element_type=F32) + cnt_scr[:, 0:1]
    rank_ref[0:1, :] = jnp.sum(oh1 * prior, axis=0, keepdims=True).astype(jnp.int32)
    rank_ref[1:2, :] = jnp.sum(oh2 * prior, axis=0, keepdims=True).astype(jnp.int32)
    cnt_scr[...] = cnt_scr[...] + jnp.sum(both, axis=1, keepdims=True)
    cnt_ref[...] = cnt_scr[...]


def _post_kernel(xp_ref, xs_ref, u_ref, uhalo_ref, sconv_ref, op_ref, os_ref, gates_ref,
                 gate1_ref, shift2_ref, scale2_ref,
                 cw_ref, cb_ref, lng_ref, lnb_ref, wpw2_ref, bpw2_ref, wao_ref, wout_ref,
                 n2_ref, wrt_ref, brt_ref,
                 x1_ref, h2_ref, eidx_ref, wtok_ref, rank_ref, cnt_ref,
                 uext, shift_scr, d_scr, s_scr, o_scr, mix_scr, h2b_scr, cnt_scr,
                 *, groups, prompt_tiles, tiles_per_seq):
    i = pl.program_id(0)
    d_model = xp_ref.shape[1]

    @pl.when(i == 0)
    def _():
        cnt_scr[...] = jnp.zeros(cnt_scr.shape, F32)
    nch = cw_ref.shape[0]
    cw = uext.shape[2]
    hist = uext.shape[1] - GROUP_ROWS
    lead = hist - (CONV_WIDTH - 1)

    def put_hist(g, rows):
        for c in range(nch):
            uext[g * nch + c, 0:hist, :] = rows[:, c * cw:(c + 1) * cw]

    @pl.when(i < prompt_tiles)
    def _():
        first = (i % tiles_per_seq) == 0
        put_hist(0, jnp.where(first, 0.0, uhalo_ref[...]))
        for g in range(1, groups):
            put_hist(g, u_ref[(g - 1) * GROUP_ROWS:g * GROUP_ROWS, :])
        o_scr[...] = op_ref[...]

    @pl.when(i >= prompt_tiles)
    def _():
        for g in range(groups):
            put_hist(g, sconv_ref[g])
        o_scr[...] = os_ref[...]

    for g in range(groups):
        for c in range(nch):
            uext[g * nch + c, hist:hist + GROUP_ROWS, :] = (
                u_ref[g * GROUP_ROWS:(g + 1) * GROUP_ROWS, c * cw:(c + 1) * cw])

    span = shift_scr.shape[1]

    def conv_chunk(k, carry):
        c = k % nch
        win_all = uext[k]
        for s in range(1, SUBLANES):
            shift_scr[s, :, :] = pltpu.roll(win_all, win_all.shape[0] - s, 0)[0:span, :]
        acc = None
        for j in range(CONV_WIDTH):
            base, s = divmod(lead + j, SUBLANES)
            rows = pl.ds(base * SUBLANES, GROUP_ROWS)
            win = uext[k, rows, :] if s == 0 else shift_scr[s, rows, :]
            term = win * cw_ref[c, j:j + 1, :]
            acc = term if acc is None else acc + term
        d_scr[k] = acc + cb_ref[c]
        return carry

    lax.fori_loop(0, groups * nch, conv_chunk, 0)

    for g in range(groups):
        dd = jnp.concatenate([d_scr[g * nch + c] for c in range(nch)], axis=1)
        mu = jnp.mean(dd, axis=-1, keepdims=True)
        var = jnp.mean(jnp.square(dd - mu), axis=-1, keepdims=True)
        y = (dd - mu) * lax.rsqrt(var + LN_EPS) * lng_ref[...] + lnb_ref[...]
        s_scr[g * GROUP_ROWS:(g + 1) * GROUP_ROWS, :] = (y * jax.nn.sigmoid(y)).astype(BF16)

    conv_out = jnp.dot(s_scr[...], wpw2_ref[...], preferred_element_type=F32) + bpw2_ref[...]
    attn_out = jnp.dot(o_scr[...], wao_ref[...], preferred_element_type=F32)
    merged = (gates_ref[:, :d_model].astype(F32) * attn_out
              + gates_ref[:, d_model:].astype(F32) * conv_out)
    mix_scr[...] = jnp.dot(merged.astype(BF16), wout_ref[...], preferred_element_type=F32)

    def residual(x_ref):
        def res_group(g, carry):
            r = pl.multiple_of(g * GROUP_ROWS, GROUP_ROWS)
            x1 = (x_ref[pl.ds(r, GROUP_ROWS), :]
                  + gate1_ref[pl.ds(g, 1), :] * mix_scr[pl.ds(r, GROUP_ROWS), :])
            x1_ref[pl.ds(r, GROUP_ROWS), :] = x1
            h = _rms_mod(x1, n2_ref[...], scale2_ref[pl.ds(g, 1), :], shift2_ref[pl.ds(g, 1), :])
            h2b_scr[pl.ds(r, GROUP_ROWS), :] = h.astype(BF16)
            h2_ref[pl.ds(r, GROUP_ROWS), :] = h
            return carry

        lax.fori_loop(0, groups, res_group, 0)

    pl.when(i < prompt_tiles)(lambda: residual(xp_ref))
    pl.when(i >= prompt_tiles)(lambda: residual(xs_ref))

    lt = lax.dot_general(wrt_ref[...], h2b_scr[...], (((1,), (1,)), ((), ())),
                         preferred_element_type=F32) + brt_ref[...]
    _route(lt, eidx_ref, wtok_ref, rank_ref, cnt_ref, cnt_scr)


def _post(x_p, x_s, u, sconv_pad, o_p, o_s, gates, gate1_g, shift2_g, scale2_g, p, n_prompt, seq, tm):
    d = x_p.shape[1]
    n = x_p.shape[0] + x_s.shape[0]
    cdim = u.shape[1]
    nch = cdim // CONV_CHUNK
    groups = tm // GROUP_ROWS
    prompt_tiles = n_prompt // tm
    const = lambda shape: pl.BlockSpec(shape, lambda i: (0,) * len(shape))
    row = lambda w: pl.BlockSpec((tm, w), lambda i: (i, 0))
    grp = pl.BlockSpec((groups, d), lambda i: (i, 0))
    return pl.pallas_call(
        functools.partial(_post_kernel, groups=groups, prompt_tiles=prompt_tiles,
                          tiles_per_seq=seq // tm),
        grid=(n // tm,),
        in_specs=_two_stream_specs(tm, d, prompt_tiles) + [
                  row(cdim),
                  pl.BlockSpec((GROUP_ROWS, cdim), lambda i: (jnp.maximum(i * groups - 1, 0), 0)),
                  pl.BlockSpec((groups, GROUP_ROWS, cdim),
                               lambda i: (jnp.maximum(i - prompt_tiles, 0), 0, 0))]
                 + _two_stream_specs(tm, ATTN_DIM, prompt_tiles) + [
                  row(2 * d), grp, grp, grp,
                  const((nch, CONV_WIDTH, CONV_CHUNK)), const((nch, 1, CONV_CHUNK)),
                  const((1, cdim)), const((1, cdim)),
                  const((cdim, d)), const((1, d)), const((ATTN_DIM, d)), const((d, d)),
                  const((1, d)), const((ROUTER_ROWS, d)), const((ROUTER_ROWS, 1))],
        out_specs=[row(d),
                   row(d),
                   pl.BlockSpec((2, tm), lambda i: (0, i)),
                   pl.BlockSpec((2, tm), lambda i: (0, i)),
                   pl.BlockSpec((2, tm), lambda i: (0, i)),
                   pl.BlockSpec((N_EXPERTS, LANES), lambda i: (0, 0))],
        out_shape=[jax.ShapeDtypeStruct((n, d), F32),
                   jax.ShapeDtypeStruct((n, d), F32),
                   jax.ShapeDtypeStruct((2, n), jnp.int32),
                   jax.ShapeDtypeStruct((2, n), F32),
                   jax.ShapeDtypeStruct((2, n), jnp.int32),
                   jax.ShapeDtypeStruct((N_EXPERTS, LANES), F32)],
        scratch_shapes=[pltpu.VMEM((groups * nch, 2 * GROUP_ROWS, CONV_CHUNK), F32),
                        pltpu.VMEM((SUBLANES, 2 * GROUP_ROWS - SUBLANES, CONV_CHUNK), F32),
                        pltpu.VMEM((groups * nch, GROUP_ROWS, CONV_CHUNK), F32),
                        pltpu.VMEM((tm, cdim), BF16),
                        pltpu.VMEM((tm, ATTN_DIM), BF16),
                        pltpu.VMEM((tm, d), F32),
                        pltpu.VMEM((tm, d), BF16),
                        pltpu.VMEM((N_EXPERTS, LANES), F32)],
        compiler_params=_cparams("arbitrary"),
        name="post",
    )(x_p, x_s, u, u, sconv_pad, o_p, o_s, gates, gate1_g, shift2_g, scale2_g,
      p["conv_w"], p["conv_b"], p["ln_g"], p["ln_b"], p["w_pw2"], p["b_pw2"], p["w_ao"], p["w_out"],
      p["n2"], p["w_rt"], p["b_rt"])


def _moe_kernel(blk_e_ref, nused_ref, src_ref, src_next_ref, roww_ref,
                h2_hbm, wg_ref, wu_ref, wd_ref, y_ref,
                xbuf, x16, wg_b, wu_b, wd_b, gsem):
    i = pl.program_id(0)
    nused = nused_ref[0]
    slot = i % 2

    def gather(idx_ref, sl):
        for r in range(MOE_BLOCK):
            pltpu.make_async_copy(h2_hbm.at[pl.ds(idx_ref[0, 0, r], 1), :],
                                  xbuf.at[sl, pl.ds(r, 1), :], gsem.at[sl]).start(priority=r % 2)

    def gather_wait(sl):
        pltpu.make_async_copy(h2_hbm.at[pl.ds(0, MOE_BLOCK), :], xbuf.at[sl], gsem.at[sl]).wait()

    @pl.when(i == 0)
    def _():
        gather(src_ref, 0)

    @pl.when(i >= nused)
    def _():
        y_ref[...] = jnp.zeros(y_ref.shape, F32)

    @pl.when(i < nused)
    def _():
        changed = jnp.logical_or(i == 0, blk_e_ref[i] != blk_e_ref[jnp.maximum(i - 1, 0)])

        @pl.when(changed)
        def _():
            wg_b[...] = wg_ref[0].astype(BF16)
            wu_b[...] = wu_ref[0].astype(BF16)
            wd_b[...] = wd_ref[0].astype(BF16)

        gather_wait(slot)
        x16[...] = xbuf[slot].astype(BF16)
        gather(src_next_ref, 1 - slot)
        x = x16[...]
        hg = jnp.dot(x, wg_b[...], preferred_element_type=F32)
        hu = jnp.dot(x, wu_b[...], preferred_element_type=F32)
        hid = (hg * jax.nn.sigmoid(hg) * hu).astype(BF16)
        y_ref[...] = jnp.dot(hid, wd_b[...], preferred_element_type=F32) * roww_ref[...]

        @pl.when(i == nused - 1)
        def _():
            gather_wait(1 - slot)


def _moe(h2, blk_e, nused, src, roww, w_gate, w_up, w_down):
    n_blocks = blk_e.shape[0]
    n_exp, d, de = w_gate.shape
    idx_spec = lambda f: pl.BlockSpec((1, 1, MOE_BLOCK), f, memory_space=pltpu.SMEM)
    last = n_blocks - 1
    grid_spec = pltpu.PrefetchScalarGridSpec(
        num_scalar_prefetch=2,
        grid=(n_blocks,),
        in_specs=[idx_spec(lambda i, be, nu: (i, 0, 0)),
                  idx_spec(lambda i, be, nu: (jnp.minimum(i + 1, last), 0, 0)),
                  pl.BlockSpec((MOE_BLOCK, 1), lambda i, be, nu: (i, 0)),
                  pl.BlockSpec(memory_space=pl.ANY),
                  pl.BlockSpec((1, d, de), lambda i, be, nu: (be[i], 0, 0)),
                  pl.BlockSpec((1, d, de), lambda i, be, nu: (be[i], 0, 0)),
                  pl.BlockSpec((1, de, d), lambda i, be, nu: (be[i], 0, 0))],
        out_specs=pl.BlockSpec((MOE_BLOCK, d), lambda i, be, nu: (i, 0)),
        scratch_shapes=[pltpu.VMEM((2, MOE_BLOCK, d), F32),
                        pltpu.VMEM((MOE_BLOCK, d), BF16),
                        pltpu.VMEM((d, de), BF16), pltpu.VMEM((d, de), BF16), pltpu.VMEM((de, d), BF16),
                        pltpu.SemaphoreType.DMA((2,))],
    )
    return pl.pallas_call(
        _moe_kernel,
        grid_spec=grid_spec,
        out_shape=jax.ShapeDtypeStruct((n_blocks * MOE_BLOCK, d), F32),
        compiler_params=_cparams("arbitrary"),
        name="moe",
    )(blk_e, nused, src, src, roww, h2, w_gate, w_up, w_down)


def _dispatch(eidx, wtok, rank, counts, n):
    a_tot = 2 * n
    experts = jnp.arange(N_EXPERTS, dtype=jnp.int32)
    padded = (counts + MOE_BLOCK - 1) // MOE_BLOCK * MOE_BLOCK
    pad_end = jnp.sum(jnp.where(experts[None, :] <= experts[:, None], padded[None, :], 0), axis=1)
    pad_start = pad_end - padded
    dest = jnp.sum(jnp.where(eidx[:, :, None] == experts, pad_start, 0), axis=-1) + rank
    n_blocks = -(-(a_tot + N_EXPERTS * (MOE_BLOCK - 1)) // MOE_BLOCK)
    n_rows = n_blocks * MOE_BLOCK
    tok = jnp.tile(jnp.arange(n, dtype=jnp.int32), 2)
    upd = jnp.stack([tok, lax.bitcast_convert_type(wtok.reshape(-1), jnp.int32)], axis=1)
    rows = jnp.zeros((n_rows, 2), jnp.int32).at[dest.reshape(-1)].set(upd, unique_indices=True)
    blk_start = jnp.arange(n_blocks, dtype=jnp.int32) * MOE_BLOCK
    blk_e = jnp.minimum(jnp.sum((pad_end[None, :] <= blk_start[:, None]).astype(jnp.int32), axis=1),
                        N_EXPERTS - 1)
    nused = (pad_end[-1] // MOE_BLOCK).reshape(1)
    roww = lax.bitcast_convert_type(rows[:, 1], F32)
    return blk_e, nused, rows[:, 0].reshape(n_blocks, 1, MOE_BLOCK), roww.reshape(n_rows, 1), dest


def _final_kernel(d0_ref, d1_ref, d0_next_ref, d1_next_ref, x1_ref, gate2_ref, fg_ref, yb_hbm,
                  op_ref, os_ref, ybuf, sem, *, groups, prompt_tiles, tiles):
    i = pl.program_id(0)
    slot = i % 2
    tm = x1_ref.shape[0]

    def gather(a_ref, b_ref, sl):
        for r in range(tm):
            pltpu.make_async_copy(yb_hbm.at[pl.ds(a_ref[0, 0, r], 1), :],
                                  ybuf.at[sl, 0, pl.ds(r, 1), :], sem.at[sl]).start(priority=0)
            pltpu.make_async_copy(yb_hbm.at[pl.ds(b_ref[0, 0, r], 1), :],
                                  ybuf.at[sl, 1, pl.ds(r, 1), :], sem.at[sl]).start(priority=1)

    def gather_wait(sl):
        for k in range(2):
            pltpu.make_async_copy(yb_hbm.at[pl.ds(0, tm), :], ybuf.at[sl, k], sem.at[sl]).wait()

    @pl.when(i == 0)
    def _():
        gather(d0_ref, d1_ref, 0)

    gather_wait(slot)
    gather(d0_next_ref, d1_next_ref, 1 - slot)

    def run(o_ref):
        def body(g, carry):
            r = pl.multiple_of(g * GROUP_ROWS, GROUP_ROWS)
            moe = ybuf[slot, 0, pl.ds(r, GROUP_ROWS), :] + ybuf[slot, 1, pl.ds(r, GROUP_ROWS), :]
            x2 = x1_ref[pl.ds(r, GROUP_ROWS), :] + gate2_ref[pl.ds(g, 1), :] * moe
            ms = jnp.mean(x2 * x2, axis=-1, keepdims=True)
            o_ref[pl.ds(r, GROUP_ROWS), :] = x2 * lax.rsqrt(ms + RMS_EPS) * fg_ref[...]
            return carry

        lax.fori_loop(0, groups, body, 0)

    pl.when(i < prompt_tiles)(lambda: run(op_ref))
    pl.when(i >= prompt_tiles)(lambda: run(os_ref))

    @pl.when(i == tiles - 1)
    def _():
        gather_wait(1 - slot)


def _final(x1, yb, dest, gate2_g, final_g, n_prompt, tm):
    n, d = x1.shape
    groups = tm // GROUP_ROWS
    tiles = n // tm
    prompt_tiles = n_prompt // tm
    idx_spec = lambda f: pl.BlockSpec((1, 1, tm), f, memory_space=pltpu.SMEM)
    nxt = lambda i: (jnp.minimum(i + 1, tiles - 1), 0, 0)
    d0 = dest[0].reshape(tiles, 1, tm)
    d1 = dest[1].reshape(tiles, 1, tm)
    return pl.pallas_call(
        functools.partial(_final_kernel, groups=groups, prompt_tiles=prompt_tiles, tiles=tiles),
        grid=(tiles,),
        in_specs=[idx_spec(lambda i: (i, 0, 0)), idx_spec(lambda i: (i, 0, 0)), idx_spec(nxt), idx_spec(nxt),
                  pl.BlockSpec((tm, d), lambda i: (i, 0)),
                  pl.BlockSpec((groups, d), lambda i: (i, 0)),
                  pl.BlockSpec((1, d), lambda i: (0, 0)),
                  pl.BlockSpec(memory_space=pl.ANY)],
        out_specs=_two_stream_specs(tm, d, prompt_tiles),
        out_shape=[jax.ShapeDtypeStruct((n_prompt, d), F32),
                   jax.ShapeDtypeStruct((n - n_prompt, d), F32)],
        scratch_shapes=[pltpu.VMEM((2, 2, tm, d), F32), pltpu.SemaphoreType.DMA((2,))],
        compiler_params=_cparams("arbitrary"),
        name="final",
    )(d0, d1, d0, d1, x1, gate2_g, final_g.reshape(1, d), yb)


def _layer(x_p, x_s, c_all, cache_k, cache_v, sconv, rel_table, lp, seq, t_new):
    (n1, n2, w_ada, b_ada, w_in, sink, w_ao, conv_w, conv_b, ln_g, ln_b, w_pw2, b_pw2, w_out,
     w_grp, b_grp, w_rt, b_rt, w_gate, w_up, w_down) = lp
    n_prompt, d = x_p.shape
    n = n_prompt + x_s.shape[0]
    batch = n_prompt // seq
    dec_batch = (n - n_prompt) // t_new
    cdim = conv_w.shape[1]
    tm = 256

    c_rows = -(-c_all.shape[0] // 8) * 8
    c_pad = jnp.pad(c_all, ((0, c_rows - c_all.shape[0]), (0, 0)))
    mods = _ada(c_pad, w_ada, b_ada)

    def per_group(k):
        m = mods[:, k * d:(k + 1) * d]
        mp = jnp.broadcast_to(m[:batch, None, :], (batch, seq // GROUP_ROWS, d))
        return jnp.concatenate([mp.reshape(batch * (seq // GROUP_ROWS), d), m[batch:batch + dec_batch]],
                               axis=0)

    shift1, scale1, gate1, shift2, scale2, gate2 = [per_group(k) for k in range(6)]

    h1 = _norm1(x_p, x_s, n1, scale1, shift1, 512)

    k_off = ATTN_DIM
    glu_off = k_off + 2 * KV_DIM
    gate_off = glu_off + 2 * cdim
    w_in_b = w_in.astype(BF16)
    tglu = 256
    w_glu = jnp.stack([w_in_b[:, glu_off:glu_off + cdim].reshape(d, cdim // tglu, tglu),
                       w_in_b[:, glu_off + cdim:gate_off].reshape(d, cdim // tglu, tglu)],
                      axis=2).reshape(d, 2 * cdim)
    tmm = 1536
    (q,) = _proj(h1, w_in_b[:, :k_off], "qkv", tmm, 512, [BF16])
    (kv,) = _proj(h1, w_in_b[:, k_off:glu_off], "qkv", tmm, 512, [BF16])
    kv_state = _kv_state(h1, w_in_b[:, k_off:glu_off], n_prompt, seq)
    (u,) = _proj(h1, w_glu, "glu", tmm, 2 * tglu, [F32])
    (gates,) = _proj(h1, w_in_b[:, gate_off:], "gate", tmm, 512, [BF16])

    bias_p = _pair_bias(rel_table)
    buf = cache_k.shape[1]
    kpos = jnp.concatenate([jnp.arange(buf, dtype=jnp.int32) - buf, jnp.arange(t_new, dtype=jnp.int32)])
    bias_s = _rel_bias(rel_table, kpos[None, :] - jnp.arange(t_new, dtype=jnp.int32)[:, None])
    sink_f = sink.astype(F32).reshape(N_KV_HEADS, Q_PER_KV, 1, 1)
    sink_kch = sink.astype(F32).reshape(N_KV_HEADS, 1, COLS_PER_KV, HEADS_PER_COL).transpose(0, 3, 1, 2)
    sink_p = jnp.broadcast_to(sink_kch[..., None, None],
                              (N_KV_HEADS, HEADS_PER_COL, 2, COLS_PER_KV, CHUNK, LANES)).reshape(
        N_KV_HEADS, HEADS_PER_COL, 2 * COLS_PER_KV * CHUNK, LANES)
    sink_s = jnp.broadcast_to(sink_f, (N_KV_HEADS, Q_PER_KV, t_new, 1)).reshape(N_KV_HEADS, Q_PER_KV * t_new, 1)
    o_p = _attn_prompt(q, kv, bias_p, sink_p, n_prompt, seq, 2)
    o_s = _attn_sample(q, kv, cache_k.reshape(dec_batch, buf, KV_DIM), cache_v.reshape(dec_batch, buf, KV_DIM),
                       bias_s, sink_s, n_prompt, t_new)

    hist = GROUP_ROWS
    sconv_pad = jnp.pad(sconv, ((0, 0), (hist - sconv.shape[1], 0), (0, 0)))
    w_rt_t = jnp.zeros((ROUTER_ROWS, d), F32)
    w_rt_t = w_rt_t.at[:N_GROUPS].set(w_grp.T).at[EXPERT_ROW0:EXPERT_ROW0 + N_EXPERTS].set(w_rt.T)
    b_rt_t = jnp.zeros((ROUTER_ROWS, 1), F32)
    b_rt_t = b_rt_t.at[:N_GROUPS, 0].set(b_grp.astype(F32)).at[EXPERT_ROW0:EXPERT_ROW0 + N_EXPERTS, 0].set(
        b_rt.astype(F32))
    nch = cdim // CONV_CHUNK
    conv_w_c = conv_w.reshape(CONV_WIDTH, nch, CONV_CHUNK).transpose(1, 0, 2)
    params = dict(conv_w=conv_w_c, conv_b=conv_b.reshape(nch, 1, CONV_CHUNK), ln_g=ln_g.reshape(1, cdim),
                  ln_b=ln_b.reshape(1, cdim), w_pw2=w_pw2.astype(BF16), b_pw2=b_pw2.reshape(1, d),
                  w_ao=w_ao.astype(BF16), w_out=w_out.astype(BF16), n2=n2.reshape(1, d),
                  w_rt=w_rt_t.astype(BF16), b_rt=b_rt_t)
    x1, h2, eidx, wtok, rank, cnt = _post(x_p, x_s, u, sconv_pad, o_p, o_s, gates, gate1, shift2, scale2,
                                             params, n_prompt, seq, tm)

    blk_e, nused, src, roww, dest = _dispatch(eidx, wtok, rank, cnt[:, 0].astype(jnp.int32), n)
    yb = _moe(h2, blk_e, nused, src, roww, w_gate, w_up, w_down)
    return x1, yb, dest, gate2, kv_state, u


def kernel(x_prompt, x_sample, c_prompt, c_sample, cache_k, cache_v, state_conv, rel_bias_table, norm1_g, norm2_g, w_ada, b_ada, w_in, attn_sink, w_attn_o, conv_w, conv_b, conv_ln_g, conv_ln_b, w_pw2, b_pw2, w_out, w_group, b_group, w_router, b_router, w_gate, w_up, w_down, final_g):
    batch, seq, d = x_prompt.shape
    dec_batch, t_new, _ = x_sample.shape
    depth = norm1_g.shape[0]
    assert depth == 1, "single trunk layer"
    assert t_new == GROUP_ROWS and seq % GROUP_ROWS == 0
    n_prompt = batch * seq
    n = n_prompt + dec_batch * t_new
    x_p = x_prompt.reshape(n_prompt, d)
    x_s = x_sample.reshape(dec_batch * t_new, d)
    c_all = jnp.concatenate([c_prompt, c_sample], axis=0)
    l = 0
    lp = (norm1_g[l], norm2_g[l], w_ada[l], b_ada[l], w_in[l], attn_sink[l], w_attn_o[l], conv_w[l],
          conv_b[l], conv_ln_g[l], conv_ln_b[l], w_pw2[l], b_pw2[l], w_out[l], w_group[l], b_group[l],
          w_router[l], b_router[l], w_gate[l], w_up[l], w_down[l])
    x1, yb, dest, gate2, kv_state, u = _layer(x_p, x_s, c_all, cache_k[l], cache_v[l], state_conv[l],
                                              rel_bias_table, lp, seq, t_new)
    out_p, out_s = _final(x1, yb, dest, gate2, final_g, n_prompt, 256)

    y_prompt = out_p.reshape(batch, seq, d)
    y_sample = out_s.reshape(dec_batch, t_new, d)
    cdim = u.shape[1]
    kvp = kv_state[:batch * WINDOW].reshape(batch, WINDOW, 2 * KV_DIM)
    new_k_prompt = kvp[..., :KV_DIM].reshape(1, batch, WINDOW, N_KV_HEADS, HEAD_DIM)
    new_v_prompt = kvp[..., KV_DIM:].reshape(1, batch, WINDOW, N_KV_HEADS, HEAD_DIM)
    new_conv_prompt = u[:n_prompt].reshape(batch, seq, cdim)[:, -(CONV_WIDTH - 1):][None]
    kvs = kv_state[batch * WINDOW:].reshape(dec_batch, t_new, 2 * KV_DIM)
    buf = cache_k.shape[2]
    k_new = kvs[..., :KV_DIM].reshape(dec_batch, t_new, N_KV_HEADS, HEAD_DIM)
    v_new = kvs[..., KV_DIM:].reshape(dec_batch, t_new, N_KV_HEADS, HEAD_DIM)
    new_k_sample = jnp.concatenate([cache_k[l], k_new], axis=1)[:, -buf:][None]
    new_v_sample = jnp.concatenate([cache_v[l], v_new], axis=1)[:, -buf:][None]
    us = u[n_prompt:].reshape(dec_batch, t_new, cdim)
    new_conv_sample = jnp.concatenate([state_conv[l], us], axis=1)[:, -(CONV_WIDTH - 1):][None]
    return (y_prompt, y_sample, new_k_prompt, new_v_prompt, new_conv_prompt,
            new_k_sample, new_v_sample, new_conv_sample)
```

```python
import functools
import math

import jax
import jax.numpy as jnp
from jax import lax
from jax.experimental import pallas as pl
from jax.experimental.pallas import tpu as pltpu

F32 = jnp.float32
BF16 = jnp.bfloat16

CHUNK = 64
HEAD_DIM = 64
N_Q_HEADS = 16
N_KV_HEADS = 4
Q_PER_KV = N_Q_HEADS // N_KV_HEADS
ATTN_DIM = N_Q_HEADS * HEAD_DIM
KV_DIM = N_KV_HEADS * HEAD_DIM
WINDOW = 128
WIN_CHUNKS = WINDOW // CHUNK
BAND = (WIN_CHUNKS + 1) * CHUNK
CONV_WIDTH = 31
NUM_BUCKETS = 32
MAX_DISTANCE = 128
N_GROUPS = 4
EXPERTS_PER_GROUP = 8
N_EXPERTS = N_GROUPS * EXPERTS_PER_GROUP
MOE_BLOCK = 128
RMS_EPS = 1e-6
LN_EPS = 1e-5
NEG_INF = -1e30

GROUP_ROWS = 32
LANES = 128
SUBLANES = 8
CONV_CHUNK = 256
ROUTER_ROWS = 64
EXPERT_ROW0 = 8
V7X_VMEM_LIMIT = 56 * 1024 * 1024


def _cparams(*sem):
    return pltpu.CompilerParams(dimension_semantics=sem, vmem_limit_bytes=V7X_VMEM_LIMIT)


def _ada_kernel(c_ref, w_ref, b_ref, o_ref):
    c = c_ref[...]
    s = c * jax.nn.sigmoid(c)
    o_ref[...] = jnp.dot(s.astype(BF16), w_ref[...].astype(BF16),
                         preferred_element_type=F32) + b_ref[...]


def _ada(c_all, w_ada, b_ada):
    rows, d = c_all.shape
    ncol = w_ada.shape[1]
    tn = 1024
    return pl.pallas_call(
        _ada_kernel,
        grid=(ncol // tn,),
        in_specs=[pl.BlockSpec((rows, d), lambda j: (0, 0)),
                  pl.BlockSpec((d, tn), lambda j: (0, j)),
                  pl.BlockSpec((1, tn), lambda j: (0, j))],
        out_specs=pl.BlockSpec((rows, tn), lambda j: (0, j)),
        out_shape=jax.ShapeDtypeStruct((rows, ncol), F32),
        compiler_params=_cparams("arbitrary"),
        name="ada",
    )(c_all, w_ada, b_ada.reshape(1, ncol))


def _rms_mod(x, gain, scale, shift):
    ms = jnp.mean(x * x, axis=-1, keepdims=True)
    y = x * lax.rsqrt(ms + RMS_EPS) * gain
    return y * (1.0 + scale) + shift


def _two_stream_specs(tm, width, prompt_tiles):
    return [pl.BlockSpec((tm, width), lambda i: (jnp.minimum(i, prompt_tiles - 1), 0)),
            pl.BlockSpec((tm, width), lambda i: (jnp.maximum(i - prompt_tiles, 0), 0))]


def _norm1_kernel(xp_ref, xs_ref, g_ref, sc_ref, sh_ref, h_ref, *, groups, prompt_tiles):
    def run(x_ref):
        def body(gi, carry):
            r = pl.multiple_of(gi * GROUP_ROWS, GROUP_ROWS)
            h = _rms_mod(x_ref[pl.ds(r, GROUP_ROWS), :], g_ref[...],
                         sc_ref[pl.ds(gi, 1), :], sh_ref[pl.ds(gi, 1), :])
            h_ref[pl.ds(r, GROUP_ROWS), :] = h.astype(h_ref.dtype)
            return carry

        lax.fori_loop(0, groups, body, 0)

    i = pl.program_id(0)
    pl.when(i < prompt_tiles)(lambda: run(xp_ref))
    pl.when(i >= prompt_tiles)(lambda: run(xs_ref))


def _norm1(x_p, x_s, gain, scale_g, shift_g, tm):
    d = x_p.shape[1]
    n = x_p.shape[0] + x_s.shape[0]
    groups = tm // GROUP_ROWS
    prompt_tiles = x_p.shape[0] // tm
    return pl.pallas_call(
        functools.partial(_norm1_kernel, groups=groups, prompt_tiles=prompt_tiles),
        grid=(n // tm,),
        in_specs=_two_stream_specs(tm, d, prompt_tiles) + [
            pl.BlockSpec((1, d), lambda i: (0, 0)),
            pl.BlockSpec((groups, d), lambda i: (i, 0)),
            pl.BlockSpec((groups, d), lambda i: (i, 0))],
        out_specs=pl.BlockSpec((tm, d), lambda i: (i, 0)),
        out_shape=jax.ShapeDtypeStruct((n, d), BF16),
        compiler_params=_cparams("arbitrary"),
        name="norm1",
    )(x_p, x_s, gain.reshape(1, d), scale_g, shift_g)


def _proj_kernel(h_ref, w_ref, *o_refs, mode):
    acc = jnp.dot(h_ref[...], w_ref[...], preferred_element_type=F32)
    if mode == "qkv":
        o_refs[0][...] = acc.astype(BF16)
    elif mode == "glu":
        half = acc.shape[1] // 2
        o_refs[0][...] = acc[:, :half] * jax.nn.sigmoid(acc[:, half:])
    else:
        o_refs[0][...] = jax.nn.sigmoid(acc).astype(BF16)


def _proj(h, w, mode, tm, tnw, out_dtypes):
    n, d = h.shape
    ncol = w.shape[1]
    tno = tnw // 2 if mode == "glu" else tnw
    nout = ncol // 2 if mode == "glu" else ncol
    outs = pl.pallas_call(
        functools.partial(_proj_kernel, mode=mode),
        grid=(n // tm, ncol // tnw),
        in_specs=[pl.BlockSpec((tm, d), lambda i, j: (i, 0)),
                  pl.BlockSpec((d, tnw), lambda i, j: (0, j))],
        out_specs=[pl.BlockSpec((tm, tno), lambda i, j: (i, j)) for _ in out_dtypes],
        out_shape=[jax.ShapeDtypeStruct((n, nout), dt) for dt in out_dtypes],
        compiler_params=_cparams("arbitrary", "arbitrary"),
        name="proj_" + mode,
    )(h, w)
    return outs


def _kv_state_kernel(h_ref, w_ref, o_ref):
    o_ref[...] = jnp.dot(h_ref[...], w_ref[...], preferred_element_type=F32)


def _kv_state(h, w_kv, n_prompt, seq):
    n, d = h.shape
    batch = n_prompt // seq
    per_seq = seq // WINDOW
    steps = batch + (n - n_prompt) // WINDOW

    def row_block(i):
        return jnp.where(i < batch, (i + 1) * per_seq - 1, n_prompt // WINDOW + i - batch)

    return pl.pallas_call(
        _kv_state_kernel,
        grid=(steps,),
        in_specs=[pl.BlockSpec((WINDOW, d), lambda i: (row_block(i), 0)),
                  pl.BlockSpec((d, 2 * KV_DIM), lambda i: (0, 0))],
        out_specs=pl.BlockSpec((WINDOW, 2 * KV_DIM), lambda i: (i, 0)),
        out_shape=jax.ShapeDtypeStruct((steps * WINDOW, 2 * KV_DIM), F32),
        compiler_params=_cparams("arbitrary"),
        name="kv_state",
    )(h, w_kv)


def _t5_bucket(rel):
    nb = NUM_BUCKETS // 2
    n = -rel
    ret = jnp.where(n < 0, nb, 0)
    n = jnp.abs(n)
    max_exact = nb // 2
    nf = jnp.maximum(n, 1).astype(F32)
    large = max_exact + (jnp.log(nf / max_exact) / math.log(MAX_DISTANCE / max_exact)
                         * (nb - max_exact)).astype(jnp.int32)
    large = jnp.minimum(large, nb - 1)
    return ret + jnp.where(n < max_exact, n, large)


def _bias_kernel(tbl_ref, bkt_ref, o_ref):
    bkt = bkt_ref[...]
    nq = bkt.shape[0]
    for k in range(N_KV_HEADS):
        for g in range(Q_PER_KV):
            acc = jnp.zeros(bkt.shape, F32)
            for b in range(NUM_BUCKETS):
                acc = jnp.where(bkt == b, tbl_ref[b, k * Q_PER_KV + g], acc)
            o_ref[k, g * nq:(g + 1) * nq, :] = acc


def _rel_bias(rel_table, rel):
    nq, nk = rel.shape
    return pl.pallas_call(
        _bias_kernel,
        in_specs=[pl.BlockSpec(memory_space=pltpu.SMEM),
                  pl.BlockSpec((nq, nk), lambda: (0, 0))],
        out_specs=pl.BlockSpec((N_KV_HEADS, Q_PER_KV * nq, nk), lambda: (0, 0, 0)),
        out_shape=jax.ShapeDtypeStruct((N_KV_HEADS, Q_PER_KV * nq, nk), F32),
        name="rel_bias",
    )(rel_table.astype(F32), _t5_bucket(rel))


PAIR_BAND = 2 * CHUNK + WINDOW
HEADS_PER_COL = LANES // HEAD_DIM
COLS_PER_KV = Q_PER_KV // HEADS_PER_COL


def _pair_bias_kernel(tbl_ref, bkt_ref, o_ref):
    for e in range(2):
        bkt = bkt_ref[e]
        for k in range(N_KV_HEADS):
            for col in range(COLS_PER_KV):
                for half in range(HEADS_PER_COL):
                    head = k * Q_PER_KV + col * HEADS_PER_COL + half
                    acc = jnp.full(bkt.shape, NEG_INF, F32)
                    for b in range(NUM_BUCKETS):
                        acc = jnp.where(bkt == b, tbl_ref[b, head], acc)
                    r0 = (e * COLS_PER_KV + col) * CHUNK
                    o_ref[k, r0:r0 + CHUNK, half * PAIR_BAND:(half + 1) * PAIR_BAND] = acc


def _pair_bias(rel_table):
    qi = jnp.arange(CHUNK, dtype=jnp.int32)[None, :, None]
    kj = jnp.arange(PAIR_BAND, dtype=jnp.int32)[None, None, :]
    e = jnp.arange(2, dtype=jnp.int32)[:, None, None]
    rel = (kj - WINDOW) - (e * CHUNK + qi)
    key_chunk = kj // CHUNK - e
    seen = (key_chunk >= 0) & (key_chunk <= WIN_CHUNKS)
    bkt = jnp.where(seen, _t5_bucket(rel), -1)
    rows = 2 * COLS_PER_KV * CHUNK
    return pl.pallas_call(
        _pair_bias_kernel,
        in_specs=[pl.BlockSpec(memory_space=pltpu.SMEM),
                  pl.BlockSpec((2, CHUNK, PAIR_BAND), lambda: (0, 0, 0))],
        out_specs=pl.BlockSpec((N_KV_HEADS, rows, HEADS_PER_COL * PAIR_BAND), lambda: (0, 0, 0)),
        out_shape=jax.ShapeDtypeStruct((N_KV_HEADS, rows, HEADS_PER_COL * PAIR_BAND), F32),
        name="pair_bias",
    )(rel_table.astype(F32), bkt)


def _attend(qc, kk_all, vv_all, bias_ref, sink_ref, mask_thr):
    nq = qc.shape[0]
    nk = kk_all.shape[0]
    pieces = []
    for k in range(N_KV_HEADS):
        qs = jnp.concatenate(
            [qc[:, (k * Q_PER_KV + g) * HEAD_DIM:(k * Q_PER_KV + g + 1) * HEAD_DIM]
             for g in range(Q_PER_KV)], axis=0)
        kk = kk_all[:, k * HEAD_DIM:(k + 1) * HEAD_DIM]
        vv = vv_all[:, k * HEAD_DIM:(k + 1) * HEAD_DIM]
        lg = lax.dot_general(qs, kk, (((1,), (1,)), ((), ())),
                             preferred_element_type=F32) * (HEAD_DIM ** -0.5) + bias_ref[k]
        if mask_thr is not None:
            col = lax.broadcasted_iota(jnp.int32, (Q_PER_KV * nq, nk), 1)
            lg = jnp.where(col < mask_thr, NEG_INF, lg)
        s = sink_ref[k]
        m = jnp.maximum(jnp.max(lg, axis=-1, keepdims=True), s)
        p = jnp.exp(lg - m)
        den = jnp.sum(p, axis=-1, keepdims=True) + jnp.exp(s - m)
        o = jnp.dot(p.astype(BF16), vv, preferred_element_type=F32) / den
        pieces.extend(o[g * nq:(g + 1) * nq, :] for g in range(Q_PER_KV))
    return jnp.concatenate(pieces, axis=1)


def _attn_prompt_kernel(q_ref, kv_ref, halo_ref, bias_ref, sink_ref, o_ref, ka, kb, va, vb, *, pairs):
    t = pl.program_id(1)
    kvcat = jnp.concatenate([halo_ref[...], kv_ref[...]], axis=0)
    low = lax.broadcasted_iota(jnp.int32, (kvcat.shape[0], LANES), 1) < HEAD_DIM
    zero = jnp.zeros((kvcat.shape[0], LANES), BF16)
    ones_low = jnp.where(low, 1.0, 0.0).astype(BF16)
    ones_high = jnp.where(low, 0.0, 1.0).astype(BF16)
    kv_cols = KV_DIM // LANES
    for col in range(2 * kv_cols):
        x = kvcat[:, col * LANES:(col + 1) * LANES]
        xs = jnp.concatenate([x[:, HEAD_DIM:], x[:, :HEAD_DIM]], axis=1)
        dst_a, dst_b = (ka, kb) if col < kv_cols else (va, vb)
        k0 = (col % kv_cols) * HEADS_PER_COL
        dst_a[k0, :, 0:LANES] = jnp.where(low, x, zero)
        dst_b[k0, :, 0:LANES] = jnp.where(low, zero, xs)
        dst_a[k0 + 1, :, 0:LANES] = jnp.where(low, xs, zero)
        dst_b[k0 + 1, :, 0:LANES] = jnp.where(low, zero, x)
    for k in range(N_KV_HEADS):
        va[k, :, LANES:2 * LANES] = ones_low
        vb[k, :, LANES:2 * LANES] = ones_high

    out_low = lax.broadcasted_iota(jnp.int32, (2 * COLS_PER_KV * CHUNK, LANES), 1) < HEAD_DIM
    for p in range(pairs):
        r0 = p * 2 * CHUNK
        for k in range(N_KV_HEADS):
            lhs = jnp.concatenate(
                [q_ref[r0 + e * CHUNK:r0 + (e + 1) * CHUNK, (k * COLS_PER_KV + c) * LANES:(k * COLS_PER_KV + c + 1) * LANES]
                 for e in range(2) for c in range(COLS_PER_KV)], axis=0)
            keys = jnp.concatenate([ka[k, r0:r0 + PAIR_BAND, :], kb[k, r0:r0 + PAIR_BAND, :]], axis=0)
            lg = lax.dot_general(lhs, keys, (((1,), (1,)), ((), ())),
                                 preferred_element_type=F32) * (HEAD_DIM ** -0.5) + bias_ref[k]
            probs, sink_terms = [], []
            for half in range(HEADS_PER_COL):
                seg = lg[:, half * PAIR_BAND:(half + 1) * PAIR_BAND]
                if p == 0:
                    kcol = lax.broadcasted_iota(jnp.int32, seg.shape, 1)
                    seg = jnp.where(kcol < jnp.where(t == 0, WINDOW, 0), NEG_INF, seg)
                s = sink_ref[k, half]
                folded = seg[:, :LANES]
                for j in range(1, PAIR_BAND // LANES):
                    folded = jnp.maximum(folded, seg[:, j * LANES:(j + 1) * LANES])
                m = jnp.maximum(jnp.max(folded, axis=-1, keepdims=True), s)
                pr = jnp.exp(seg - jnp.concatenate([m] * (PAIR_BAND // LANES), axis=1))
                sink_terms.append(jnp.exp(s - m))
                probs.append(pr.astype(BF16))
            vals = jnp.concatenate([va[k, r0:r0 + PAIR_BAND, :], vb[k, r0:r0 + PAIR_BAND, :]], axis=0)
            oe = jnp.dot(jnp.concatenate(probs, axis=1), vals, preferred_element_type=F32)
            o = oe[:, :LANES] / (oe[:, LANES:] + jnp.where(out_low, sink_terms[0], sink_terms[1]))
            for e in range(2):
                for c in range(COLS_PER_KV):
                    rr = (e * COLS_PER_KV + c) * CHUNK
                    o_ref[r0 + e * CHUNK:r0 + (e + 1) * CHUNK,
                          (k * COLS_PER_KV + c) * LANES:(k * COLS_PER_KV + c + 1) * LANES] = (
                        o[rr:rr + CHUNK, :].astype(o_ref.dtype))


def _attn_prompt(q, kv, bias, sink_rows, n_prompt, seq, pairs):
    rows = pairs * 2 * CHUNK
    tiles = seq // rows
    halo_per_tile = rows // WINDOW
    keys = WINDOW + rows
    return pl.pallas_call(
        functools.partial(_attn_prompt_kernel, pairs=pairs),
        grid=(n_prompt // seq, tiles),
        scratch_shapes=[pltpu.VMEM((N_KV_HEADS, keys, LANES), BF16) for _ in range(2)]
        + [pltpu.VMEM((N_KV_HEADS, keys, 2 * LANES), BF16) for _ in range(2)],
        in_specs=[pl.BlockSpec((rows, ATTN_DIM), lambda b, t: (b * tiles + t, 0)),
                  pl.BlockSpec((rows, 2 * KV_DIM), lambda b, t: (b * tiles + t, 0)),
                  pl.BlockSpec((WINDOW, 2 * KV_DIM),
                               lambda b, t: (jnp.maximum((b * tiles + t) * halo_per_tile - 1, 0), 0)),
                  pl.BlockSpec(bias.shape, lambda b, t: (0, 0, 0)),
                  pl.BlockSpec(sink_rows.shape, lambda b, t: (0, 0, 0, 0))],
        out_specs=pl.BlockSpec((rows, ATTN_DIM), lambda b, t: (b * tiles + t, 0)),
        out_shape=jax.ShapeDtypeStruct((n_prompt, ATTN_DIM), BF16),
        compiler_params=_cparams("arbitrary", "arbitrary"),
        name="attn_prompt",
    )(q, kv, kv, bias, sink_rows)


def _attn_sample_kernel(q_ref, kv_ref, ck_ref, cv_ref, bias_ref, sink_ref, o_ref):
    kv = kv_ref[...]
    kk = jnp.concatenate([ck_ref[0].astype(BF16), kv[:, :KV_DIM]], axis=0)
    vv = jnp.concatenate([cv_ref[0].astype(BF16), kv[:, KV_DIM:]], axis=0)
    o = _attend(q_ref[...], kk, vv, bias_ref, sink_ref, None)
    o_ref[...] = o.astype(o_ref.dtype)


def _attn_sample(q, kv, cache_k, cache_v, bias, sink_rows, n_prompt, t_new):
    dec_batch, buf, _ = cache_k.shape
    first = n_prompt // t_new
    return pl.pallas_call(
        _attn_sample_kernel,
        grid=(dec_batch,),
        in_specs=[pl.BlockSpec((t_new, ATTN_DIM), lambda s: (first + s, 0)),
                  pl.BlockSpec((t_new, 2 * KV_DIM), lambda s: (first + s, 0)),
                  pl.BlockSpec((1, buf, KV_DIM), lambda s: (s, 0, 0)),
                  pl.BlockSpec((1, buf, KV_DIM), lambda s: (s, 0, 0)),
                  pl.BlockSpec(bias.shape, lambda s: (0, 0, 0)),
                  pl.BlockSpec(sink_rows.shape, lambda s: (0, 0, 0))],
        out_specs=pl.BlockSpec((t_new, ATTN_DIM), lambda s: (s, 0)),
        out_shape=jax.ShapeDtypeStruct((dec_batch * t_new, ATTN_DIM), BF16),
        compiler_params=_cparams("arbitrary"),
        name="attn_sample",
    )(q, kv, cache_k, cache_v, bias, sink_rows)


def _route(lt, eidx_ref, wtok_ref, rank_ref, cnt_ref, cnt_scr):
    gl = [lt[r:r + 1, :] for r in range(N_GROUPS)]
    gmax = gl[0]
    gsel = jnp.zeros(gl[0].shape, jnp.int32)
    for r in range(1, N_GROUPS):
        better = gl[r] > gmax
        gsel = jnp.where(better, r, gsel)
        gmax = jnp.maximum(gmax, gl[r])
    gexp = [jnp.exp(v - gmax) for v in gl]
    gsum = gexp[0]
    for r in range(1, N_GROUPS):
        gsum = gsum + gexp[r]
    psel = jnp.zeros(gl[0].shape, F32)
    for r in range(N_GROUPS):
        psel = jnp.where(gsel == r, gexp[r] / gsum, psel)
    el = jnp.zeros((EXPERTS_PER_GROUP, lt.shape[1]), F32)
    for r in range(N_GROUPS):
        lo = EXPERT_ROW0 + r * EXPERTS_PER_GROUP
        el = jnp.where(gsel == r, lt[lo:lo + EXPERTS_PER_GROUP, :], el)
    emax = jnp.max(el, axis=0, keepdims=True)
    ee = jnp.exp(el - emax)
    pin = ee / jnp.sum(ee, axis=0, keepdims=True)
    idx = lax.broadcasted_iota(jnp.int32, pin.shape, 0)
    p1 = jnp.max(pin, axis=0, keepdims=True)
    i1 = jnp.min(jnp.where(pin == p1, idx, EXPERTS_PER_GROUP), axis=0, keepdims=True)
    rest = jnp.where(idx == i1, -1.0, pin)
    p2 = jnp.max(rest, axis=0, keepdims=True)
    i2 = jnp.min(jnp.where(rest == p2, idx, EXPERTS_PER_GROUP), axis=0, keepdims=True)
    tot = p1 + p2
    e1 = gsel * EXPERTS_PER_GROUP + i1
    e2 = gsel * EXPERTS_PER_GROUP + i2
    eidx_ref[0:1, :] = e1
    eidx_ref[1:2, :] = e2
    wtok_ref[0:1, :] = psel * p1 / tot
    wtok_ref[1:2, :] = psel * p2 / tot

    t = lt.shape[1]
    eiota = lax.broadcasted_iota(jnp.int32, (N_EXPERTS, t), 0)
    oh1 = (eiota == e1).astype(F32)
    oh2 = (eiota == e2).astype(F32)
    both = oh1 + oh2
    before = (lax.broadcasted_iota(jnp.int32, (t, t), 0)
              < lax.broadcasted_iota(jnp.int32, (t, t), 1)).astype(BF16)
    prior = jnp.dot(both.astype(BF16), before, preferred_element_type=F32) + cnt_scr[:, 0:1]
    rank_ref[0:1, :] = jnp.sum(oh1 * prior, axis=0, keepdims=True).astype(jnp.int32)
    rank_ref[1:2, :] = jnp.sum(oh2 * prior, axis=0, keepdims=True).astype(jnp.int32)
    cnt_scr[...] = cnt_scr[...] + jnp.sum(both, axis=1, keepdims=True)
    cnt_ref[...] = cnt_scr[...]


def _post_kernel(xp_ref, xs_ref, u_ref, uhalo_ref, sconv_ref, op_ref, os_ref, gates_ref,
                 gate1_ref, shift2_ref, scale2_ref,
                 cw_ref, cb_ref, lng_ref, lnb_ref, wpw2_ref, bpw2_ref, wao_ref, wout_ref,
                 n2_ref, wrt_ref, brt_ref,
                 x1_ref, h2_ref, eidx_ref, wtok_ref, rank_ref, cnt_ref,
                 uext, shift_scr, d_scr, s_scr, o_scr, mix_scr, h2b_scr, cnt_scr,
                 *, groups, prompt_tiles, tiles_per_seq):
    i = pl.program_id(0)
    d_model = xp_ref.shape[1]

    @pl.when(i == 0)
    def _():
        cnt_scr[...] = jnp.zeros(cnt_scr.shape, F32)
    nch = cw_ref.shape[0]
    cw = uext.shape[2]
    hist = uext.shape[1] - GROUP_ROWS
    lead = hist - (CONV_WIDTH - 1)

    def put_hist(g, rows):
        for c in range(nch):
            uext[g * nch + c, 0:hist, :] = rows[:, c * cw:(c + 1) * cw]

    @pl.when(i < prompt_tiles)
    def _():
        first = (i % tiles_per_seq) == 0
        put_hist(0, jnp.where(first, 0.0, uhalo_ref[...]))
        for g in range(1, groups):
            put_hist(g, u_ref[(g - 1) * GROUP_ROWS:g * GROUP_ROWS, :])
        o_scr[...] = op_ref[...]

    @pl.when(i >= prompt_tiles)
    def _():
        for g in range(groups):
            put_hist(g, sconv_ref[g])
        o_scr[...] = os_ref[...]

    for g in range(groups):
        for c in range(nch):
            uext[g * nch + c, hist:hist + GROUP_ROWS, :] = (
                u_ref[g * GROUP_ROWS:(g + 1) * GROUP_ROWS, c * cw:(c + 1) * cw])

    span = shift_scr.shape[1]

    def conv_chunk(k, carry):
        c = k % nch
        win_all = uext[k]
        for s in range(1, SUBLANES):
            shift_scr[s, :, :] = pltpu.roll(win_all, win_all.shape[0] - s, 0)[0:span, :]
        acc = None
        for j in range(CONV_WIDTH):
            base, s = divmod(lead + j, SUBLANES)
            rows = pl.ds(base * SUBLANES, GROUP_ROWS)
            win = uext[k, rows, :] if s == 0 else shift_scr[s, rows, :]
            term = win * cw_ref[c, j:j + 1, :]
            acc = term if acc is None else acc + term
        d_scr[k] = acc + cb_ref[c]
        return carry

    lax.fori_loop(0, groups * nch, conv_chunk, 0)

    for g in range(groups):
        dd = jnp.concatenate([d_scr[g * nch + c] for c in range(nch)], axis=1)
        mu = jnp.mean(dd, axis=-1, keepdims=True)
        var = jnp.mean(jnp.square(dd - mu), axis=-1, keepdims=True)
        y = (dd - mu) * lax.rsqrt(var + LN_EPS) * lng_ref[...] + lnb_ref[...]
        s_scr[g * GROUP_ROWS:(g + 1) * GROUP_ROWS, :] = (y * jax.nn.sigmoid(y)).astype(BF16)

    conv_out = jnp.dot(s_scr[...], wpw2_ref[...], preferred_element_type=F32) + bpw2_ref[...]
    attn_out = jnp.dot(o_scr[...], wao_ref[...], preferred_element_type=F32)
    merged = (gates_ref[:, :d_model].astype(F32) * attn_out
              + gates_ref[:, d_model:].astype(F32) * conv_out)
    mix_scr[...] = jnp.dot(merged.astype(BF16), wout_ref[...], preferred_element_type=F32)

    def residual(x_ref):
        def res_group(g, carry):
            r = pl.multiple_of(g * GROUP_ROWS, GROUP_ROWS)
            x1 = (x_ref[pl.ds(r, GROUP_ROWS), :]
                  + gate1_ref[pl.ds(g, 1), :] * mix_scr[pl.ds(r, GROUP_ROWS), :])
            x1_ref[pl.ds(r, GROUP_ROWS), :] = x1
            h = _rms_mod(x1, n2_ref[...], scale2_ref[pl.ds(g, 1), :], shift2_ref[pl.ds(g, 1), :])
            h2b_scr[pl.ds(r, GROUP_ROWS), :] = h.astype(BF16)
            h2_ref[pl.ds(r, GROUP_ROWS), :] = h
            return carry

        lax.fori_loop(0, groups, res_group, 0)

    pl.when(i < prompt_tiles)(lambda: residual(xp_ref))
    pl.when(i >= prompt_tiles)(lambda: residual(xs_ref))

    lt = lax.dot_general(wrt_ref[...], h2b_scr[...], (((1,), (1,)), ((), ())),
                         preferred_element_type=F32) + brt_ref[...]
    _route(lt, eidx_ref, wtok_ref, rank_ref, cnt_ref, cnt_scr)


def _post(x_p, x_s, u, sconv_pad, o_p, o_s, gates, gate1_g, shift2_g, scale2_g, p, n_prompt, seq, tm):
    d = x_p.shape[1]
    n = x_p.shape[0] + x_s.shape[0]
    cdim = u.shape[1]
    nch = cdim // CONV_CHUNK
    groups = tm // GROUP_ROWS
    prompt_tiles = n_prompt // tm
    const = lambda shape: pl.BlockSpec(shape, lambda i: (0,) * len(shape))
    row = lambda w: pl.BlockSpec((tm, w), lambda i: (i, 0))
    grp = pl.BlockSpec((groups, d), lambda i: (i, 0))
    return pl.pallas_call(
        functools.partial(_post_kernel, groups=groups, prompt_tiles=prompt_tiles,
                          tiles_per_seq=seq // tm),
        grid=(n // tm,),
        in_specs=_two_stream_specs(tm, d, prompt_tiles) + [
                  row(cdim),
                  pl.BlockSpec((GROUP_ROWS, cdim), lambda i: (jnp.maximum(i * groups - 1, 0), 0)),
                  pl.BlockSpec((groups, GROUP_ROWS, cdim),
                               lambda i: (jnp.maximum(i - prompt_tiles, 0), 0, 0))]
                 + _two_stream_specs(tm, ATTN_DIM, prompt_tiles) + [
                  row(2 * d), grp, grp, grp,
                  const((nch, CONV_WIDTH, CONV_CHUNK)), const((nch, 1, CONV_CHUNK)),
                  const((1, cdim)), const((1, cdim)),
                  const((cdim, d)), const((1, d)), const((ATTN_DIM, d)), const((d, d)),
                  const((1, d)), const((ROUTER_ROWS, d)), const((ROUTER_ROWS, 1))],
        out_specs=[row(d),
                   row(d),
                   pl.BlockSpec((2, tm), lambda i: (0, i)),
                   pl.BlockSpec((2, tm), lambda i: (0, i)),
                   pl.BlockSpec((2, tm), lambda i: (0, i)),
                   pl.BlockSpec((N_EXPERTS, LANES), lambda i: (0, 0))],
        out_shape=[jax.ShapeDtypeStruct((n, d), F32),
                   jax.ShapeDtypeStruct((n, d), F32),
                   jax.ShapeDtypeStruct((2, n), jnp.int32),
                   jax.ShapeDtypeStruct((2, n), F32),
                   jax.ShapeDtypeStruct((2, n), jnp.int32),
                   jax.ShapeDtypeStruct((N_EXPERTS, LANES), F32)],
        scratch_shapes=[pltpu.VMEM((groups * nch, 2 * GROUP_ROWS, CONV_CHUNK), F32),
                        pltpu.VMEM((SUBLANES, 2 * GROUP_ROWS - SUBLANES, CONV_CHUNK), F32),
                        pltpu.VMEM((groups * nch, GROUP_ROWS, CONV_CHUNK), F32),
                        pltpu.VMEM((tm, cdim), BF16),
                        pltpu.VMEM((tm, ATTN_DIM), BF16),
                        pltpu.VMEM((tm, d), F32),
                        pltpu.VMEM((tm, d), BF16),
                        pltpu.VMEM((N_EXPERTS, LANES), F32)],
        compiler_params=_cparams("arbitrary"),
        name="post",
    )(x_p, x_s, u, u, sconv_pad, o_p, o_s, gates, gate1_g, shift2_g, scale2_g,
      p["conv_w"], p["conv_b"], p["ln_g"], p["ln_b"], p["w_pw2"], p["b_pw2"], p["w_ao"], p["w_out"],
      p["n2"], p["w_rt"], p["b_rt"])


TABLE_GROUP = 8


def _grouped_tables(tab, group=TABLE_GROUP):
    steps, w = tab.shape
    assert steps % group == 0
    nxt = jnp.concatenate([tab[group::group], tab[-1:]], axis=0)
    return jnp.concatenate([tab.reshape(steps // group, group * w), nxt], axis=1).reshape(
        steps // group, 1, (group + 1) * w)

def _moe_kernel(blk_e_ref, nused_ref, src_ref, roww_ref,
                h2_hbm, wg_ref, wu_ref, wd_ref, y_ref,
                xbuf, x16, wg_b, wu_b, wd_b, gsem):
    i = pl.program_id(0)
    nused = nused_ref[0]
    slot = i % 2
    base = (i % TABLE_GROUP) * MOE_BLOCK

    def gather(first, sl):
        for r in range(MOE_BLOCK):
            pltpu.make_async_copy(h2_hbm.at[pl.ds(src_ref[0, 0, first + r], 1), :],
                                  xbuf.at[sl, pl.ds(r, 1), :], gsem.at[sl]).start(priority=r % 2)

    def gather_wait(sl):
        pltpu.make_async_copy(h2_hbm.at[pl.ds(0, MOE_BLOCK), :], xbuf.at[sl], gsem.at[sl]).wait()

    @pl.when(i == 0)
    def _():
        gather(base, 0)

    @pl.when(i >= nused)
    def _():
        y_ref[...] = jnp.zeros(y_ref.shape, F32)

    @pl.when(i < nused)
    def _():
        changed = jnp.logical_or(i == 0, blk_e_ref[i] != blk_e_ref[jnp.maximum(i - 1, 0)])

        @pl.when(changed)
        def _():
            wg_b[...] = wg_ref[0].astype(BF16)
            wu_b[...] = wu_ref[0].astype(BF16)
            wd_b[...] = wd_ref[0].astype(BF16)

        gather_wait(slot)
        x16[...] = xbuf[slot].astype(BF16)
        gather(base + MOE_BLOCK, 1 - slot)
        x = x16[...]
        hg = jnp.dot(x, wg_b[...], preferred_element_type=F32)
        hu = jnp.dot(x, wu_b[...], preferred_element_type=F32)
        hid = (hg * jax.nn.sigmoid(hg) * hu).astype(BF16)
        y_ref[...] = jnp.dot(hid, wd_b[...], preferred_element_type=F32) * roww_ref[...]

        @pl.when(i == nused - 1)
        def _():
            gather_wait(1 - slot)


def _moe(h2, blk_e, nused, src, roww, w_gate, w_up, w_down):
    n_blocks = blk_e.shape[0]
    n_exp, d, de = w_gate.shape
    src_tables = _grouped_tables(src.reshape(n_blocks, MOE_BLOCK))
    grid_spec = pltpu.PrefetchScalarGridSpec(
        num_scalar_prefetch=2,
        grid=(n_blocks,),
        in_specs=[pl.BlockSpec((1, 1, src_tables.shape[2]), lambda i, be, nu: (i // TABLE_GROUP, 0, 0),
                               memory_space=pltpu.SMEM),
                  pl.BlockSpec((MOE_BLOCK, 1), lambda i, be, nu: (i, 0)),
                  pl.BlockSpec(memory_space=pl.ANY),
                  pl.BlockSpec((1, d, de), lambda i, be, nu: (be[i], 0, 0)),
                  pl.BlockSpec((1, d, de), lambda i, be, nu: (be[i], 0, 0)),
                  pl.BlockSpec((1, de, d), lambda i, be, nu: (be[i], 0, 0))],
        out_specs=pl.BlockSpec((MOE_BLOCK, d), lambda i, be, nu: (i, 0)),
        scratch_shapes=[pltpu.VMEM((2, MOE_BLOCK, d), F32),
                        pltpu.VMEM((MOE_BLOCK, d), BF16),
                        pltpu.VMEM((d, de), BF16), pltpu.VMEM((d, de), BF16), pltpu.VMEM((de, d), BF16),
                        pltpu.SemaphoreType.DMA((2,))],
    )
    return pl.pallas_call(
        _moe_kernel,
        grid_spec=grid_spec,
        out_shape=jax.ShapeDtypeStruct((n_blocks * MOE_BLOCK, d), F32),
        compiler_params=_cparams("arbitrary"),
        name="moe",
    )(blk_e, nused, src_tables, roww, h2, w_gate, w_up, w_down)


def _dispatch(eidx, wtok, rank, counts, n):
    a_tot = 2 * n
    experts = jnp.arange(N_EXPERTS, dtype=jnp.int32)
    padded = (counts + MOE_BLOCK - 1) // MOE_BLOCK * MOE_BLOCK
    pad_end = jnp.sum(jnp.where(experts[None, :] <= experts[:, None], padded[None, :], 0), axis=1)
    pad_start = pad_end - padded
    dest = jnp.sum(jnp.where(eidx[:, :, None] == experts, pad_start, 0), axis=-1) + rank
    n_blocks = -(-(a_tot + N_EXPERTS * (MOE_BLOCK - 1)) // MOE_BLOCK)
    n_rows = n_blocks * MOE_BLOCK
    tok = jnp.tile(jnp.arange(n, dtype=jnp.int32), 2)
    upd = jnp.stack([tok, lax.bitcast_convert_type(wtok.reshape(-1), jnp.int32)], axis=1)
    rows = jnp.zeros((n_rows, 2), jnp.int32).at[dest.reshape(-1)].set(upd, unique_indices=True)
    blk_start = jnp.arange(n_blocks, dtype=jnp.int32) * MOE_BLOCK
    blk_e = jnp.minimum(jnp.sum((pad_end[None, :] <= blk_start[:, None]).astype(jnp.int32), axis=1),
                        N_EXPERTS - 1)
    nused = (pad_end[-1] // MOE_BLOCK).reshape(1)
    roww = lax.bitcast_convert_type(rows[:, 1], F32)
    return blk_e, nused, rows[:, 0].reshape(n_blocks, 1, MOE_BLOCK), roww.reshape(n_rows, 1), dest


def _final_kernel(d0_ref, d1_ref, x1_ref, gate2_ref, fg_ref, yb_hbm,
                  op_ref, os_ref, ybuf, sem, *, groups, prompt_tiles, tiles, table_group):
    i = pl.program_id(0)
    slot = i % 2
    tm = x1_ref.shape[0]
    base = (i % table_group) * tm

    def gather(first, sl):
        for r in range(tm):
            pltpu.make_async_copy(yb_hbm.at[pl.ds(d0_ref[0, 0, first + r], 1), :],
                                  ybuf.at[sl, 0, pl.ds(r, 1), :], sem.at[sl]).start(priority=0)
            pltpu.make_async_copy(yb_hbm.at[pl.ds(d1_ref[0, 0, first + r], 1), :],
                                  ybuf.at[sl, 1, pl.ds(r, 1), :], sem.at[sl]).start(priority=1)

    def gather_wait(sl):
        for k in range(2):
            pltpu.make_async_copy(yb_hbm.at[pl.ds(0, tm), :], ybuf.at[sl, k], sem.at[sl]).wait()

    @pl.when(i == 0)
    def _():
        gather(base, 0)

    gather_wait(slot)
    gather(base + tm, 1 - slot)

    def run(o_ref):
        def body(g, carry):
            r = pl.multiple_of(g * GROUP_ROWS, GROUP_ROWS)
            moe = ybuf[slot, 0, pl.ds(r, GROUP_ROWS), :] + ybuf[slot, 1, pl.ds(r, GROUP_ROWS), :]
            x2 = x1_ref[pl.ds(r, GROUP_ROWS), :] + gate2_ref[pl.ds(g, 1), :] * moe
            ms = jnp.mean(x2 * x2, axis=-1, keepdims=True)
            o_ref[pl.ds(r, GROUP_ROWS), :] = x2 * lax.rsqrt(ms + RMS_EPS) * fg_ref[...]
            return carry

        lax.fori_loop(0, groups, body, 0)

    pl.when(i < prompt_tiles)(lambda: run(op_ref))
    pl.when(i >= prompt_tiles)(lambda: run(os_ref))

    @pl.when(i == tiles - 1)
    def _():
        gather_wait(1 - slot)


def _final(x1, yb, dest, gate2_g, final_g, n_prompt, tm):
    n, d = x1.shape
    groups = tm // GROUP_ROWS
    tiles = n // tm
    prompt_tiles = n_prompt // tm
    table_group = max(g for g in range(1, TABLE_GROUP + 1) if tiles % g == 0)
    tables = [_grouped_tables(dest[k].reshape(tiles, tm), table_group) for k in range(2)]
    idx_spec = pl.BlockSpec((1, 1, tables[0].shape[2]), lambda i: (i // table_group, 0, 0),
                            memory_space=pltpu.SMEM)
    return pl.pallas_call(
        functools.partial(_final_kernel, groups=groups, prompt_tiles=prompt_tiles, tiles=tiles,
                          table_group=table_group),
        grid=(tiles,),
        in_specs=[idx_spec, idx_spec,
                  pl.BlockSpec((tm, d), lambda i: (i, 0)),
                  pl.BlockSpec((groups, d), lambda i: (i, 0)),
                  pl.BlockSpec((1, d), lambda i: (0, 0)),
                  pl.BlockSpec(memory_space=pl.ANY)],
        out_specs=_two_stream_specs(tm, d, prompt_tiles),
        out_shape=[jax.ShapeDtypeStruct((n_prompt, d), F32),
                   jax.ShapeDtypeStruct((n - n_prompt, d), F32)],
        scratch_shapes=[pltpu.VMEM((2, 2, tm, d), F32), pltpu.SemaphoreType.DMA((2,))],
        compiler_params=_cparams("arbitrary"),
        name="final",
    )(tables[0], tables[1], x1, gate2_g, final_g.reshape(1, d), yb)


def _layer(x_p, x_s, c_all, cache_k, cache_v, sconv, rel_table, lp, seq, t_new):
    (n1, n2, w_ada, b_ada, w_in, sink, w_ao, conv_w, conv_b, ln_g, ln_b, w_pw2, b_pw2, w_out,
     w_grp, b_grp, w_rt, b_rt, w_gate, w_up, w_down) = lp
    n_prompt, d = x_p.shape
    n = n_prompt + x_s.shape[0]
    batch = n_prompt // seq
    dec_batch = (n - n_prompt) // t_new
    cdim = conv_w.shape[1]
    tm = 256

    c_rows = -(-c_all.shape[0] // 8) * 8
    c_pad = jnp.pad(c_all, ((0, c_rows - c_all.shape[0]), (0, 0)))
    mods = _ada(c_pad, w_ada, b_ada)

    def per_group(k):
        m = mods[:, k * d:(k + 1) * d]
        mp = jnp.broadcast_to(m[:batch, None, :], (batch, seq // GROUP_ROWS, d))
        return jnp.concatenate([mp.reshape(batch * (seq // GROUP_ROWS), d), m[batch:batch + dec_batch]],
                               axis=0)

    shift1, scale1, gate1, shift2, scale2, gate2 = [per_group(k) for k in range(6)]

    h1 = _norm1(x_p, x_s, n1, scale1, shift1, 512)

    k_off = ATTN_DIM
    glu_off = k_off + 2 * KV_DIM
    gate_off = glu_off + 2 * cdim
    w_in_b = w_in.astype(BF16)
    tglu = 256
    w_glu = jnp.stack([w_in_b[:, glu_off:glu_off + cdim].reshape(d, cdim // tglu, tglu),
                       w_in_b[:, glu_off + cdim:gate_off].reshape(d, cdim // tglu, tglu)],
                      axis=2).reshape(d, 2 * cdim)
    tmm = 1536
    (q,) = _proj(h1, w_in_b[:, :k_off], "qkv", tmm, 512, [BF16])
    (kv,) = _proj(h1, w_in_b[:, k_off:glu_off], "qkv", tmm, 512, [BF16])
    kv_state = _kv_state(h1, w_in_b[:, k_off:glu_off], n_prompt, seq)
    (u,) = _proj(h1, w_glu, "glu", tmm, 2 * tglu, [F32])
    (gates,) = _proj(h1, w_in_b[:, gate_off:], "gate", tmm, 512, [BF16])

    bias_p = _pair_bias(rel_table)
    buf = cache_k.shape[1]
    kpos = jnp.concatenate([jnp.arange(buf, dtype=jnp.int32) - buf, jnp.arange(t_new, dtype=jnp.int32)])
    bias_s = _rel_bias(rel_table, kpos[None, :] - jnp.arange(t_new, dtype=jnp.int32)[:, None])
    sink_f = sink.astype(F32).reshape(N_KV_HEADS, Q_PER_KV, 1, 1)
    sink_kch = sink.astype(F32).reshape(N_KV_HEADS, 1, COLS_PER_KV, HEADS_PER_COL).transpose(0, 3, 1, 2)
    sink_p = jnp.broadcast_to(sink_kch[..., None, None],
                              (N_KV_HEADS, HEADS_PER_COL, 2, COLS_PER_KV, CHUNK, LANES)).reshape(
        N_KV_HEADS, HEADS_PER_COL, 2 * COLS_PER_KV * CHUNK, LANES)
    sink_s = jnp.broadcast_to(sink_f, (N_KV_HEADS, Q_PER_KV, t_new, 1)).reshape(N_KV_HEADS, Q_PER_KV * t_new, 1)
    o_p = _attn_prompt(q, kv, bias_p, sink_p, n_prompt, seq, 2)
    o_s = _attn_sample(q, kv, cache_k.reshape(dec_batch, buf, KV_DIM), cache_v.reshape(dec_batch, buf, KV_DIM),
                       bias_s, sink_s, n_prompt, t_new)

    hist = GROUP_ROWS
    sconv_pad = jnp.pad(sconv, ((0, 0), (hist - sconv.shape[1], 0), (0, 0)))
    w_rt_t = jnp.zeros((ROUTER_ROWS, d), F32)
    w_rt_t = w_rt_t.at[:N_GROUPS].set(w_grp.T).at[EXPERT_ROW0:EXPERT_ROW0 + N_EXPERTS].set(w_rt.T)
    b_rt_t = jnp.zeros((ROUTER_ROWS, 1), F32)
    b_rt_t = b_rt_t.at[:N_GROUPS, 0].set(b_grp.astype(F32)).at[EXPERT_ROW0:EXPERT_ROW0 + N_EXPERTS, 0].set(
        b_rt.astype(F32))
    nch = cdim // CONV_CHUNK
    conv_w_c = conv_w.reshape(CONV_WIDTH, nch, CONV_CHUNK).transpose(1, 0, 2)
    params = dict(conv_w=conv_w_c, conv_b=conv_b.reshape(nch, 1, CONV_CHUNK), ln_g=ln_g.reshape(1, cdim),
                  ln_b=ln_b.reshape(1, cdim), w_pw2=w_pw2.astype(BF16), b_pw2=b_pw2.reshape(1, d),
                  w_ao=w_ao.astype(BF16), w_out=w_out.astype(BF16), n2=n2.reshape(1, d),
                  w_rt=w_rt_t.astype(BF16), b_rt=b_rt_t)
    x1, h2, eidx, wtok, rank, cnt = _post(x_p, x_s, u, sconv_pad, o_p, o_s, gates, gate1, shift2, scale2,
                                             params, n_prompt, seq, tm)

    blk_e, nused, src, roww, dest = _dispatch(eidx, wtok, rank, cnt[:, 0].astype(jnp.int32), n)
    yb = _moe(h2, blk_e, nused, src, roww, w_gate, w_up, w_down)
    return x1, yb, dest, gate2, kv_state, u


def kernel(x_prompt, x_sample, c_prompt, c_sample, cache_k, cache_v, state_conv, rel_bias_table, norm1_g, norm2_g, w_ada, b_ada, w_in, attn_sink, w_attn_o, conv_w, conv_b, conv_ln_g, conv_ln_b, w_pw2, b_pw2, w_out, w_group, b_group, w_router, b_router, w_gate, w_up, w_down, final_g):
    batch, seq, d = x_prompt.shape
    dec_batch, t_new, _ = x_sample.shape
    depth = norm1_g.shape[0]
    assert depth == 1, "single trunk layer"
    assert t_new == GROUP_ROWS and seq % GROUP_ROWS == 0
    n_prompt = batch * seq
    n = n_prompt + dec_batch * t_new
    x_p = x_prompt.reshape(n_prompt, d)
    x_s = x_sample.reshape(dec_batch * t_new, d)
    c_all = jnp.concatenate([c_prompt, c_sample], axis=0)
    l = 0
    lp = (norm1_g[l], norm2_g[l], w_ada[l], b_ada[l], w_in[l], attn_sink[l], w_attn_o[l], conv_w[l],
          conv_b[l], conv_ln_g[l], conv_ln_b[l], w_pw2[l], b_pw2[l], w_out[l], w_group[l], b_group[l],
          w_router[l], b_router[l], w_gate[l], w_up[l], w_down[l])
    x1, yb, dest, gate2, kv_state, u = _layer(x_p, x_s, c_all, cache_k[l], cache_v[l], state_conv[l],
                                              rel_bias_table, lp, seq, t_new)
    out_p, out_s = _final(x1, yb, dest, gate2, final_g, n_prompt, 256)

    y_prompt = out_p.reshape(batch, seq, d)
    y_sample = out_s.reshape(dec_batch, t_new, d)
    cdim = u.shape[1]
    kvp = kv_state[:batch * WINDOW].reshape(batch, WINDOW, 2 * KV_DIM)
    new_k_prompt = kvp[..., :KV_DIM].reshape(1, batch, WINDOW, N_KV_HEADS, HEAD_DIM)
    new_v_prompt = kvp[..., KV_DIM:].reshape(1, batch, WINDOW, N_KV_HEADS, HEAD_DIM)
    new_conv_prompt = u[:n_prompt].reshape(batch, seq, cdim)[:, -(CONV_WIDTH - 1):][None]
    kvs = kv_state[batch * WINDOW:].reshape(dec_batch, t_new, 2 * KV_DIM)
    buf = cache_k.shape[2]
    k_new = kvs[..., :KV_DIM].reshape(dec_batch, t_new, N_KV_HEADS, HEAD_DIM)
    v_new = kvs[..., KV_DIM:].reshape(dec_batch, t_new, N_KV_HEADS, HEAD_DIM)
    new_k_sample = jnp.concatenate([cache_k[l], k_new], axis=1)[:, -buf:][None]
    new_v_sample = jnp.concatenate([cache_v[l], v_new], axis=1)[:, -buf:][None]
    us = u[n_prompt:].reshape(dec_batch, t_new, cdim)
    new_conv_sample = jnp.concatenate([state_conv[l], us], axis=1)[:, -(CONV_WIDTH - 1):][None]
    return (y_prompt, y_sample, new_k_prompt, new_v_prompt, new_conv_prompt,
            new_k_sample, new_v_sample, new_conv_sample)
```

```python
import functools
import math

import jax
import jax.numpy as jnp
from jax import lax
from jax.experimental import pallas as pl
from jax.experimental.pallas import tpu as pltpu

F32 = jnp.float32
BF16 = jnp.bfloat16

CHUNK = 64
HEAD_DIM = 64
N_Q_HEADS = 16
N_KV_HEADS = 4
Q_PER_KV = N_Q_HEADS // N_KV_HEADS
ATTN_DIM = N_Q_HEADS * HEAD_DIM
KV_DIM = N_KV_HEADS * HEAD_DIM
WINDOW = 128
WIN_CHUNKS = WINDOW // CHUNK
BAND = (WIN_CHUNKS + 1) * CHUNK
CONV_WIDTH = 31
NUM_BUCKETS = 32
MAX_DISTANCE = 128
N_GROUPS = 4
EXPERTS_PER_GROUP = 8
N_EXPERTS = N_GROUPS * EXPERTS_PER_GROUP
MOE_BLOCK = 256
RMS_EPS = 1e-6
LN_EPS = 1e-5
NEG_INF = -1e30

GROUP_ROWS = 32
LANES = 128
SUBLANES = 8
CONV_CHUNK = 256
ROUTER_ROWS = 64
EXPERT_ROW0 = 8
V7X_VMEM_LIMIT = 56 * 1024 * 1024


def _cparams(*sem):
    return pltpu.CompilerParams(dimension_semantics=sem, vmem_limit_bytes=V7X_VMEM_LIMIT)


def _ada_kernel(c_ref, w_ref, b_ref, o_ref):
    c = c_ref[...]
    s = c * jax.nn.sigmoid(c)
    o_ref[...] = jnp.dot(s.astype(BF16), w_ref[...].astype(BF16),
                         preferred_element_type=F32) + b_ref[...]


def _ada(c_all, w_ada, b_ada):
    rows, d = c_all.shape
    ncol = w_ada.shape[1]
    tn = 1024
    return pl.pallas_call(
        _ada_kernel,
        grid=(ncol // tn,),
        in_specs=[pl.BlockSpec((rows, d), lambda j: (0, 0)),
                  pl.BlockSpec((d, tn), lambda j: (0, j)),
                  pl.BlockSpec((1, tn), lambda j: (0, j))],
        out_specs=pl.BlockSpec((rows, tn), lambda j: (0, j)),
        out_shape=jax.ShapeDtypeStruct((rows, ncol), F32),
        compiler_params=_cparams("arbitrary"),
        name="ada",
    )(c_all, w_ada, b_ada.reshape(1, ncol))


def _rms_mod(x, gain, scale, shift):
    ms = jnp.mean(x * x, axis=-1, keepdims=True)
    y = x * lax.rsqrt(ms + RMS_EPS) * gain
    return y * (1.0 + scale) + shift


def _two_stream_specs(tm, width, prompt_tiles):
    return [pl.BlockSpec((tm, width), lambda i: (jnp.minimum(i, prompt_tiles - 1), 0)),
            pl.BlockSpec((tm, width), lambda i: (jnp.maximum(i - prompt_tiles, 0), 0))]


def _norm1_kernel(xp_ref, xs_ref, g_ref, sc_ref, sh_ref, h_ref, *, groups, prompt_tiles):
    def run(x_ref):
        def body(gi, carry):
            r = pl.multiple_of(gi * GROUP_ROWS, GROUP_ROWS)
            h = _rms_mod(x_ref[pl.ds(r, GROUP_ROWS), :], g_ref[...],
                         sc_ref[pl.ds(gi, 1), :], sh_ref[pl.ds(gi, 1), :])
            h_ref[pl.ds(r, GROUP_ROWS), :] = h.astype(h_ref.dtype)
            return carry

        lax.fori_loop(0, groups, body, 0)

    i = pl.program_id(0)
    pl.when(i < prompt_tiles)(lambda: run(xp_ref))
    pl.when(i >= prompt_tiles)(lambda: run(xs_ref))


def _norm1(x_p, x_s, gain, scale_g, shift_g, tm):
    d = x_p.shape[1]
    n = x_p.shape[0] + x_s.shape[0]
    groups = tm // GROUP_ROWS
    prompt_tiles = x_p.shape[0] // tm
    return pl.pallas_call(
        functools.partial(_norm1_kernel, groups=groups, prompt_tiles=prompt_tiles),
        grid=(n // tm,),
        in_specs=_two_stream_specs(tm, d, prompt_tiles) + [
            pl.BlockSpec((1, d), lambda i: (0, 0)),
            pl.BlockSpec((groups, d), lambda i: (i, 0)),
            pl.BlockSpec((groups, d), lambda i: (i, 0))],
        out_specs=pl.BlockSpec((tm, d), lambda i: (i, 0)),
        out_shape=jax.ShapeDtypeStruct((n, d), BF16),
        compiler_params=_cparams("arbitrary"),
        name="norm1",
    )(x_p, x_s, gain.reshape(1, d), scale_g, shift_g)


def _proj_kernel(h_ref, w_ref, *o_refs, mode):
    acc = jnp.dot(h_ref[...], w_ref[...], preferred_element_type=F32)
    if mode == "qkv":
        o_refs[0][...] = acc.astype(BF16)
    elif mode == "glu":
        half = acc.shape[1] // 2
        o_refs[0][...] = acc[:, :half] * jax.nn.sigmoid(acc[:, half:])
    else:
        o_refs[0][...] = jax.nn.sigmoid(acc).astype(BF16)


def _proj(h, w, mode, tm, tnw, out_dtypes):
    n, d = h.shape
    ncol = w.shape[1]
    tno = tnw // 2 if mode == "glu" else tnw
    nout = ncol // 2 if mode == "glu" else ncol
    outs = pl.pallas_call(
        functools.partial(_proj_kernel, mode=mode),
        grid=(n // tm, ncol // tnw),
        in_specs=[pl.BlockSpec((tm, d), lambda i, j: (i, 0)),
                  pl.BlockSpec((d, tnw), lambda i, j: (0, j))],
        out_specs=[pl.BlockSpec((tm, tno), lambda i, j: (i, j)) for _ in out_dtypes],
        out_shape=[jax.ShapeDtypeStruct((n, nout), dt) for dt in out_dtypes],
        compiler_params=_cparams("arbitrary", "arbitrary"),
        name="proj_" + mode,
    )(h, w)
    return outs


def _kv_state_kernel(h_ref, w_ref, o_ref):
    o_ref[...] = jnp.dot(h_ref[...], w_ref[...], preferred_element_type=F32)


def _kv_state(h, w_kv, n_prompt, seq):
    n, d = h.shape
    batch = n_prompt // seq
    per_seq = seq // WINDOW
    steps = batch + (n - n_prompt) // WINDOW

    def row_block(i):
        return jnp.where(i < batch, (i + 1) * per_seq - 1, n_prompt // WINDOW + i - batch)

    return pl.pallas_call(
        _kv_state_kernel,
        grid=(steps,),
        in_specs=[pl.BlockSpec((WINDOW, d), lambda i: (row_block(i), 0)),
                  pl.BlockSpec((d, 2 * KV_DIM), lambda i: (0, 0))],
        out_specs=pl.BlockSpec((WINDOW, 2 * KV_DIM), lambda i: (i, 0)),
        out_shape=jax.ShapeDtypeStruct((steps * WINDOW, 2 * KV_DIM), F32),
        compiler_params=_cparams("arbitrary"),
        name="kv_state",
    )(h, w_kv)


def _t5_bucket(rel):
    nb = NUM_BUCKETS // 2
    n = -rel
    ret = jnp.where(n < 0, nb, 0)
    n = jnp.abs(n)
    max_exact = nb // 2
    nf = jnp.maximum(n, 1).astype(F32)
    large = max_exact + (jnp.log(nf / max_exact) / math.log(MAX_DISTANCE / max_exact)
                         * (nb - max_exact)).astype(jnp.int32)
    large = jnp.minimum(large, nb - 1)
    return ret + jnp.where(n < max_exact, n, large)


def _bias_kernel(tbl_ref, bkt_ref, o_ref):
    bkt = bkt_ref[...]
    nq = bkt.shape[0]
    for k in range(N_KV_HEADS):
        for g in range(Q_PER_KV):
            acc = jnp.zeros(bkt.shape, F32)
            for b in range(NUM_BUCKETS):
                acc = jnp.where(bkt == b, tbl_ref[b, k * Q_PER_KV + g], acc)
            o_ref[k, g * nq:(g + 1) * nq, :] = acc


def _rel_bias(rel_table, rel):
    nq, nk = rel.shape
    return pl.pallas_call(
        _bias_kernel,
        in_specs=[pl.BlockSpec(memory_space=pltpu.SMEM),
                  pl.BlockSpec((nq, nk), lambda: (0, 0))],
        out_specs=pl.BlockSpec((N_KV_HEADS, Q_PER_KV * nq, nk), lambda: (0, 0, 0)),
        out_shape=jax.ShapeDtypeStruct((N_KV_HEADS, Q_PER_KV * nq, nk), F32),
        name="rel_bias",
    )(rel_table.astype(F32), _t5_bucket(rel))


PAIR_BAND = 2 * CHUNK + WINDOW
HEADS_PER_COL = LANES // HEAD_DIM
COLS_PER_KV = Q_PER_KV // HEADS_PER_COL


def _pair_bias_kernel(tbl_ref, bkt_ref, o_ref):
    for e in range(2):
        bkt = bkt_ref[e]
        for k in range(N_KV_HEADS):
            for col in range(COLS_PER_KV):
                for half in range(HEADS_PER_COL):
                    head = k * Q_PER_KV + col * HEADS_PER_COL + half
                    acc = jnp.full(bkt.shape, NEG_INF, F32)
                    for b in range(NUM_BUCKETS):
                        acc = jnp.where(bkt == b, tbl_ref[b, head], acc)
                    r0 = (e * COLS_PER_KV + col) * CHUNK
                    o_ref[k, r0:r0 + CHUNK, half * PAIR_BAND:(half + 1) * PAIR_BAND] = acc


def _pair_bias(rel_table):
    qi = jnp.arange(CHUNK, dtype=jnp.int32)[None, :, None]
    kj = jnp.arange(PAIR_BAND, dtype=jnp.int32)[None, None, :]
    e = jnp.arange(2, dtype=jnp.int32)[:, None, None]
    rel = (kj - WINDOW) - (e * CHUNK + qi)
    key_chunk = kj // CHUNK - e
    seen = (key_chunk >= 0) & (key_chunk <= WIN_CHUNKS)
    bkt = jnp.where(seen, _t5_bucket(rel), -1)
    rows = 2 * COLS_PER_KV * CHUNK
    return pl.pallas_call(
        _pair_bias_kernel,
        in_specs=[pl.BlockSpec(memory_space=pltpu.SMEM),
                  pl.BlockSpec((2, CHUNK, PAIR_BAND), lambda: (0, 0, 0))],
        out_specs=pl.BlockSpec((N_KV_HEADS, rows, HEADS_PER_COL * PAIR_BAND), lambda: (0, 0, 0)),
        out_shape=jax.ShapeDtypeStruct((N_KV_HEADS, rows, HEADS_PER_COL * PAIR_BAND), F32),
        name="pair_bias",
    )(rel_table.astype(F32), bkt)


def _attend(qc, kk_all, vv_all, bias_ref, sink_ref, mask_thr):
    nq = qc.shape[0]
    nk = kk_all.shape[0]
    pieces = []
    for k in range(N_KV_HEADS):
        qs = jnp.concatenate(
            [qc[:, (k * Q_PER_KV + g) * HEAD_DIM:(k * Q_PER_KV + g + 1) * HEAD_DIM]
             for g in range(Q_PER_KV)], axis=0)
        kk = kk_all[:, k * HEAD_DIM:(k + 1) * HEAD_DIM]
        vv = vv_all[:, k * HEAD_DIM:(k + 1) * HEAD_DIM]
        lg = lax.dot_general(qs, kk, (((1,), (1,)), ((), ())),
                             preferred_element_type=F32) * (HEAD_DIM ** -0.5) + bias_ref[k]
        if mask_thr is not None:
            col = lax.broadcasted_iota(jnp.int32, (Q_PER_KV * nq, nk), 1)
            lg = jnp.where(col < mask_thr, NEG_INF, lg)
        s = sink_ref[k]
        m = jnp.maximum(jnp.max(lg, axis=-1, keepdims=True), s)
        p = jnp.exp(lg - m)
        den = jnp.sum(p, axis=-1, keepdims=True) + jnp.exp(s - m)
        o = jnp.dot(p.astype(BF16), vv, preferred_element_type=F32) / den
        pieces.extend(o[g * nq:(g + 1) * nq, :] for g in range(Q_PER_KV))
    return jnp.concatenate(pieces, axis=1)


def _attn_prompt_kernel(q_ref, kv_ref, halo_ref, bias_ref, sink_ref, o_ref, ka, kb, va, vb, *, pairs):
    t = pl.program_id(1)
    kvcat = jnp.concatenate([halo_ref[...], kv_ref[...]], axis=0)
    low = lax.broadcasted_iota(jnp.int32, (kvcat.shape[0], LANES), 1) < HEAD_DIM
    zero = jnp.zeros((kvcat.shape[0], LANES), BF16)
    ones_low = jnp.where(low, 1.0, 0.0).astype(BF16)
    ones_high = jnp.where(low, 0.0, 1.0).astype(BF16)
    kv_cols = KV_DIM // LANES
    for col in range(2 * kv_cols):
        x = kvcat[:, col * LANES:(col + 1) * LANES]
        xs = jnp.concatenate([x[:, HEAD_DIM:], x[:, :HEAD_DIM]], axis=1)
        dst_a, dst_b = (ka, kb) if col < kv_cols else (va, vb)
        k0 = (col % kv_cols) * HEADS_PER_COL
        dst_a[k0, :, 0:LANES] = jnp.where(low, x, zero)
        dst_b[k0, :, 0:LANES] = jnp.where(low, zero, xs)
        dst_a[k0 + 1, :, 0:LANES] = jnp.where(low, xs, zero)
        dst_b[k0 + 1, :, 0:LANES] = jnp.where(low, zero, x)
    for k in range(N_KV_HEADS):
        va[k, :, LANES:2 * LANES] = ones_low
        vb[k, :, LANES:2 * LANES] = ones_high

    out_low = lax.broadcasted_iota(jnp.int32, (2 * COLS_PER_KV * CHUNK, LANES), 1) < HEAD_DIM
    for p in range(pairs):
        r0 = p * 2 * CHUNK
        for k in range(N_KV_HEADS):
            lhs = jnp.concatenate(
                [q_ref[r0 + e * CHUNK:r0 + (e + 1) * CHUNK, (k * COLS_PER_KV + c) * LANES:(k * COLS_PER_KV + c + 1) * LANES]
                 for e in range(2) for c in range(COLS_PER_KV)], axis=0)
            keys = jnp.concatenate([ka[k, r0:r0 + PAIR_BAND, :], kb[k, r0:r0 + PAIR_BAND, :]], axis=0)
            lg = lax.dot_general(lhs, keys, (((1,), (1,)), ((), ())),
                                 preferred_element_type=F32) * (HEAD_DIM ** -0.5) + bias_ref[k]
            probs, sink_terms = [], []
            for half in range(HEADS_PER_COL):
                seg = lg[:, half * PAIR_BAND:(half + 1) * PAIR_BAND]
                if p == 0:
                    kcol = lax.broadcasted_iota(jnp.int32, seg.shape, 1)
                    seg = jnp.where(kcol < jnp.where(t == 0, WINDOW, 0), NEG_INF, seg)
                s = sink_ref[k, half]
                folded = seg[:, :LANES]
                for j in range(1, PAIR_BAND // LANES):
                    folded = jnp.maximum(folded, seg[:, j * LANES:(j + 1) * LANES])
                m = jnp.maximum(jnp.max(folded, axis=-1, keepdims=True), s)
                pr = jnp.exp(seg - jnp.concatenate([m] * (PAIR_BAND // LANES), axis=1))
                sink_terms.append(jnp.exp(s - m))
                probs.append(pr.astype(BF16))
            vals = jnp.concatenate([va[k, r0:r0 + PAIR_BAND, :], vb[k, r0:r0 + PAIR_BAND, :]], axis=0)
            oe = jnp.dot(jnp.concatenate(probs, axis=1), vals, preferred_element_type=F32)
            o = oe[:, :LANES] / (oe[:, LANES:] + jnp.where(out_low, sink_terms[0], sink_terms[1]))
            for e in range(2):
                for c in range(COLS_PER_KV):
                    rr = (e * COLS_PER_KV + c) * CHUNK
                    o_ref[r0 + e * CHUNK:r0 + (e + 1) * CHUNK,
                          (k * COLS_PER_KV + c) * LANES:(k * COLS_PER_KV + c + 1) * LANES] = (
                        o[rr:rr + CHUNK, :].astype(o_ref.dtype))


def _attn_prompt(q, kv, bias, sink_rows, n_prompt, seq, pairs):
    rows = pairs * 2 * CHUNK
    tiles = seq // rows
    halo_per_tile = rows // WINDOW
    keys = WINDOW + rows
    return pl.pallas_call(
        functools.partial(_attn_prompt_kernel, pairs=pairs),
        grid=(n_prompt // seq, tiles),
        scratch_shapes=[pltpu.VMEM((N_KV_HEADS, keys, LANES), BF16) for _ in range(2)]
        + [pltpu.VMEM((N_KV_HEADS, keys, 2 * LANES), BF16) for _ in range(2)],
        in_specs=[pl.BlockSpec((rows, ATTN_DIM), lambda b, t: (b * tiles + t, 0)),
                  pl.BlockSpec((rows, 2 * KV_DIM), lambda b, t: (b * tiles + t, 0)),
                  pl.BlockSpec((WINDOW, 2 * KV_DIM),
                               lambda b, t: (jnp.maximum((b * tiles + t) * halo_per_tile - 1, 0), 0)),
                  pl.BlockSpec(bias.shape, lambda b, t: (0, 0, 0)),
                  pl.BlockSpec(sink_rows.shape, lambda b, t: (0, 0, 0, 0))],
        out_specs=pl.BlockSpec((rows, ATTN_DIM), lambda b, t: (b * tiles + t, 0)),
        out_shape=jax.ShapeDtypeStruct((n_prompt, ATTN_DIM), BF16),
        compiler_params=_cparams("arbitrary", "arbitrary"),
        name="attn_prompt",
    )(q, kv, kv, bias, sink_rows)


def _attn_sample_kernel(q_ref, kv_ref, ck_ref, cv_ref, bias_ref, sink_ref, o_ref):
    kv = kv_ref[...]
    kk = jnp.concatenate([ck_ref[0].astype(BF16), kv[:, :KV_DIM]], axis=0)
    vv = jnp.concatenate([cv_ref[0].astype(BF16), kv[:, KV_DIM:]], axis=0)
    o = _attend(q_ref[...], kk, vv, bias_ref, sink_ref, None)
    o_ref[...] = o.astype(o_ref.dtype)


def _attn_sample(q, kv, cache_k, cache_v, bias, sink_rows, n_prompt, t_new):
    dec_batch, buf, _ = cache_k.shape
    first = n_prompt // t_new
    return pl.pallas_call(
        _attn_sample_kernel,
        grid=(dec_batch,),
        in_specs=[pl.BlockSpec((t_new, ATTN_DIM), lambda s: (first + s, 0)),
                  pl.BlockSpec((t_new, 2 * KV_DIM), lambda s: (first + s, 0)),
                  pl.BlockSpec((1, buf, KV_DIM), lambda s: (s, 0, 0)),
                  pl.BlockSpec((1, buf, KV_DIM), lambda s: (s, 0, 0)),
                  pl.BlockSpec(bias.shape, lambda s: (0, 0, 0)),
                  pl.BlockSpec(sink_rows.shape, lambda s: (0, 0, 0))],
        out_specs=pl.BlockSpec((t_new, ATTN_DIM), lambda s: (s, 0)),
        out_shape=jax.ShapeDtypeStruct((dec_batch * t_new, ATTN_DIM), BF16),
        compiler_params=_cparams("arbitrary"),
        name="attn_sample",
    )(q, kv, cache_k, cache_v, bias, sink_rows)


def _route(lt, eidx_ref, wtok_ref, rank_ref, cnt_ref, cnt_scr):
    gl = [lt[r:r + 1, :] for r in range(N_GROUPS)]
    gmax = gl[0]
    gsel = jnp.zeros(gl[0].shape, jnp.int32)
    for r in range(1, N_GROUPS):
        better = gl[r] > gmax
        gsel = jnp.where(better, r, gsel)
        gmax = jnp.maximum(gmax, gl[r])
    gexp = [jnp.exp(v - gmax) for v in gl]
    gsum = gexp[0]
    for r in range(1, N_GROUPS):
        gsum = gsum + gexp[r]
    psel = jnp.zeros(gl[0].shape, F32)
    for r in range(N_GROUPS):
        psel = jnp.where(gsel == r, gexp[r] / gsum, psel)
    el = jnp.zeros((EXPERTS_PER_GROUP, lt.shape[1]), F32)
    for r in range(N_GROUPS):
        lo = EXPERT_ROW0 + r * EXPERTS_PER_GROUP
        el = jnp.where(gsel == r, lt[lo:lo + EXPERTS_PER_GROUP, :], el)
    emax = jnp.max(el, axis=0, keepdims=True)
    ee = jnp.exp(el - emax)
    pin = ee / jnp.sum(ee, axis=0, keepdims=True)
    idx = lax.broadcasted_iota(jnp.int32, pin.shape, 0)
    p1 = jnp.max(pin, axis=0, keepdims=True)
    i1 = jnp.min(jnp.where(pin == p1, idx, EXPERTS_PER_GROUP), axis=0, keepdims=True)
    rest = jnp.where(idx == i1, -1.0, pin)
    p2 = jnp.max(rest, axis=0, keepdims=True)
    i2 = jnp.min(jnp.where(rest == p2, idx, EXPERTS_PER_GROUP), axis=0, keepdims=True)
    tot = p1 + p2
    e1 = gsel * EXPERTS_PER_GROUP + i1
    e2 = gsel * EXPERTS_PER_GROUP + i2
    eidx_ref[0:1, :] = e1
    eidx_ref[1:2, :] = e2
    wtok_ref[0:1, :] = psel * p1 / tot
    wtok_ref[1:2, :] = psel * p2 / tot

    t = lt.shape[1]
    eiota = lax.broadcasted_iota(jnp.int32, (N_EXPERTS, t), 0)
    oh1 = (eiota == e1).astype(F32)
    oh2 = (eiota == e2).astype(F32)
    both = oh1 + oh2
    before = (lax.broadcasted_iota(jnp.int32, (t, t), 0)
              < lax.broadcasted_iota(jnp.int32, (t, t), 1)).astype(BF16)
    prior = jnp.dot(both.astype(BF16), before, preferred_element_type=F32) + cnt_scr[:, 0:1]
    rank_ref[0:1, :] = jnp.sum(oh1 * prior, axis=0, keepdims=True).astype(jnp.int32)
    rank_ref[1:2, :] = jnp.sum(oh2 * prior, axis=0, keepdims=True).astype(jnp.int32)
    cnt_scr[...] = cnt_scr[...] + jnp.sum(both, axis=1, keepdims=True)
    cnt_ref[...] = cnt_scr[...]


def _post_kernel(xp_ref, xs_ref, u_ref, uhalo_ref, sconv_ref, op_ref, os_ref, gates_ref,
                 gate1_ref, shift2_ref, scale2_ref,
                 cw_ref, cb_ref, lng_ref, lnb_ref, wpw2_ref, bpw2_ref, wao_ref, wout_ref,
                 n2_ref, wrt_ref, brt_ref,
                 x1_ref, h2_ref, eidx_ref, wtok_ref, rank_ref, cnt_ref,
                 uext, shift_scr, d_scr, s_scr, o_scr, mix_scr, h2b_scr, cnt_scr,
                 *, groups, prompt_tiles, tiles_per_seq):
    i = pl.program_id(0)
    d_model = xp_ref.shape[1]

    @pl.when(i == 0)
    def _():
        cnt_scr[...] = jnp.zeros(cnt_scr.shape, F32)
    nch = cw_ref.shape[0]
    cw = uext.shape[2]
    hist = uext.shape[1] - GROUP_ROWS
    lead = hist - (CONV_WIDTH - 1)

    def put_hist(g, rows):
        for c in range(nch):
            uext[g * nch + c, 0:hist, :] = rows[:, c * cw:(c + 1) * cw]

    @pl.when(i < prompt_tiles)
    def _():
        first = (i % tiles_per_seq) == 0
        put_hist(0, jnp.where(first, 0.0, uhalo_ref[...]))
        for g in range(1, groups):
            put_hist(g, u_ref[(g - 1) * GROUP_ROWS:g * GROUP_ROWS, :])
        o_scr[...] = op_ref[...]

    @pl.when(i >= prompt_tiles)
    def _():
        for g in range(groups):
            put_hist(g, sconv_ref[g])
        o_scr[...] = os_ref[...]

    for g in range(groups):
        for c in range(nch):
            uext[g * nch + c, hist:hist + GROUP_ROWS, :] = (
                u_ref[g * GROUP_ROWS:(g + 1) * GROUP_ROWS, c * cw:(c + 1) * cw])

    span = shift_scr.shape[1]

    def conv_chunk(k, carry):
        c = k % nch
        win_all = uext[k]
        for s in range(1, SUBLANES):
            shift_scr[s, :, :] = pltpu.roll(win_all, win_all.shape[0] - s, 0)[0:span, :]
        acc = None
        for j in range(CONV_WIDTH):
            base, s = divmod(lead + j, SUBLANES)
            rows = pl.ds(base * SUBLANES, GROUP_ROWS)
            win = uext[k, rows, :] if s == 0 else shift_scr[s, rows, :]
            term = win * cw_ref[c, j:j + 1, :]
            acc = term if acc is None else acc + term
        d_scr[k] = acc + cb_ref[c]
        return carry

    lax.fori_loop(0, groups * nch, conv_chunk, 0)

    for g in range(groups):
        dd = jnp.concatenate([d_scr[g * nch + c] for c in range(nch)], axis=1)
        mu = jnp.mean(dd, axis=-1, keepdims=True)
        var = jnp.mean(jnp.square(dd - mu), axis=-1, keepdims=True)
        y = (dd - mu) * lax.rsqrt(var + LN_EPS) * lng_ref[...] + lnb_ref[...]
        s_scr[g * GROUP_ROWS:(g + 1) * GROUP_ROWS, :] = (y * jax.nn.sigmoid(y)).astype(BF16)

    conv_out = jnp.dot(s_scr[...], wpw2_ref[...], preferred_element_type=F32) + bpw2_ref[...]
    attn_out = jnp.dot(o_scr[...], wao_ref[...], preferred_element_type=F32)
    merged = (gates_ref[:, :d_model].astype(F32) * attn_out
              + gates_ref[:, d_model:].astype(F32) * conv_out)
    mix_scr[...] = jnp.dot(merged.astype(BF16), wout_ref[...], preferred_element_type=F32)

    def residual(x_ref):
        def res_group(g, carry):
            r = pl.multiple_of(g * GROUP_ROWS, GROUP_ROWS)
            x1 = (x_ref[pl.ds(r, GROUP_ROWS), :]
                  + gate1_ref[pl.ds(g, 1), :] * mix_scr[pl.ds(r, GROUP_ROWS), :])
            x1_ref[pl.ds(r, GROUP_ROWS), :] = x1
            h = _rms_mod(x1, n2_ref[...], scale2_ref[pl.ds(g, 1), :], shift2_ref[pl.ds(g, 1), :])
            h2b_scr[pl.ds(r, GROUP_ROWS), :] = h.astype(BF16)
            h2_ref[pl.ds(r, GROUP_ROWS), :] = h
            return carry

        lax.fori_loop(0, groups, res_group, 0)

    pl.when(i < prompt_tiles)(lambda: residual(xp_ref))
    pl.when(i >= prompt_tiles)(lambda: residual(xs_ref))

    lt = lax.dot_general(wrt_ref[...], h2b_scr[...], (((1,), (1,)), ((), ())),
                         preferred_element_type=F32) + brt_ref[...]
    _route(lt, eidx_ref, wtok_ref, rank_ref, cnt_ref, cnt_scr)


def _post(x_p, x_s, u, sconv_pad, o_p, o_s, gates, gate1_g, shift2_g, scale2_g, p, n_prompt, seq, tm):
    d = x_p.shape[1]
    n = x_p.shape[0] + x_s.shape[0]
    cdim = u.shape[1]
    nch = cdim // CONV_CHUNK
    groups = tm // GROUP_ROWS
    prompt_tiles = n_prompt // tm
    const = lambda shape: pl.BlockSpec(shape, lambda i: (0,) * len(shape))
    row = lambda w: pl.BlockSpec((tm, w), lambda i: (i, 0))
    grp = pl.BlockSpec((groups, d), lambda i: (i, 0))
    return pl.pallas_call(
        functools.partial(_post_kernel, groups=groups, prompt_tiles=prompt_tiles,
                          tiles_per_seq=seq // tm),
        grid=(n // tm,),
        in_specs=_two_stream_specs(tm, d, prompt_tiles) + [
                  row(cdim),
                  pl.BlockSpec((GROUP_ROWS, cdim), lambda i: (jnp.maximum(i * groups - 1, 0), 0)),
                  pl.BlockSpec((groups, GROUP_ROWS, cdim),
                               lambda i: (jnp.maximum(i - prompt_tiles, 0), 0, 0))]
                 + _two_stream_specs(tm, ATTN_DIM, prompt_tiles) + [
                  row(2 * d), grp, grp, grp,
                  const((nch, CONV_WIDTH, CONV_CHUNK)), const((nch, 1, CONV_CHUNK)),
                  const((1, cdim)), const((1, cdim)),
                  const((cdim, d)), const((1, d)), const((ATTN_DIM, d)), const((d, d)),
                  const((1, d)), const((ROUTER_ROWS, d)), const((ROUTER_ROWS, 1))],
        out_specs=[row(d),
                   row(d),
                   pl.BlockSpec((2, tm), lambda i: (0, i)),
                   pl.BlockSpec((2, tm), lambda i: (0, i)),
                   pl.BlockSpec((2, tm), lambda i: (0, i)),
                   pl.BlockSpec((N_EXPERTS, LANES), lambda i: (0, 0))],
        out_shape=[jax.ShapeDtypeStruct((n, d), F32),
                   jax.ShapeDtypeStruct((n, d), F32),
                   jax.ShapeDtypeStruct((2, n), jnp.int32),
                   jax.ShapeDtypeStruct((2, n), F32),
                   jax.ShapeDtypeStruct((2, n), jnp.int32),
                   jax.ShapeDtypeStruct((N_EXPERTS, LANES), F32)],
        scratch_shapes=[pltpu.VMEM((groups * nch, 2 * GROUP_ROWS, CONV_CHUNK), F32),
                        pltpu.VMEM((SUBLANES, 2 * GROUP_ROWS - SUBLANES, CONV_CHUNK), F32),
                        pltpu.VMEM((groups * nch, GROUP_ROWS, CONV_CHUNK), F32),
                        pltpu.VMEM((tm, cdim), BF16),
                        pltpu.VMEM((tm, ATTN_DIM), BF16),
                        pltpu.VMEM((tm, d), F32),
                        pltpu.VMEM((tm, d), BF16),
                        pltpu.VMEM((N_EXPERTS, LANES), F32)],
        compiler_params=_cparams("arbitrary"),
        name="post",
    )(x_p, x_s, u, u, sconv_pad, o_p, o_s, gates, gate1_g, shift2_g, scale2_g,
      p["conv_w"], p["conv_b"], p["ln_g"], p["ln_b"], p["w_pw2"], p["b_pw2"], p["w_ao"], p["w_out"],
      p["n2"], p["w_rt"], p["b_rt"])


TABLE_GROUP = 8


def _grouped_tables(tab, group=TABLE_GROUP):
    steps, w = tab.shape
    assert steps % group == 0
    nxt = jnp.concatenate([tab[group::group], tab[-1:]], axis=0)
    return jnp.concatenate([tab.reshape(steps // group, group * w), nxt], axis=1).reshape(
        steps // group, 1, (group + 1) * w)

def _moe_kernel(blk_e_ref, nused_ref, src_ref, roww_ref,
                h2_hbm, wg_ref, wu_ref, wd_ref, y_ref,
                xbuf, x16, wg_b, wu_b, wd_b, gsem):
    i = pl.program_id(0)
    nused = nused_ref[0]
    slot = i % 2
    base = (i % TABLE_GROUP) * MOE_BLOCK

    def gather(first, sl):
        for r in range(MOE_BLOCK):
            pltpu.make_async_copy(h2_hbm.at[pl.ds(src_ref[0, 0, first + r], 1), :],
                                  xbuf.at[sl, pl.ds(r, 1), :], gsem.at[sl]).start(priority=r % 2)

    def gather_wait(sl):
        pltpu.make_async_copy(h2_hbm.at[pl.ds(0, MOE_BLOCK), :], xbuf.at[sl], gsem.at[sl]).wait()

    @pl.when(i == 0)
    def _():
        gather(base, 0)

    @pl.when(i >= nused)
    def _():
        y_ref[...] = jnp.zeros(y_ref.shape, F32)

    @pl.when(i < nused)
    def _():
        changed = jnp.logical_or(i == 0, blk_e_ref[i] != blk_e_ref[jnp.maximum(i - 1, 0)])

        @pl.when(changed)
        def _():
            wg_b[...] = wg_ref[0].astype(BF16)
            wu_b[...] = wu_ref[0].astype(BF16)
            wd_b[...] = wd_ref[0].astype(BF16)

        gather_wait(slot)
        x16[...] = xbuf[slot].astype(BF16)
        gather(base + MOE_BLOCK, 1 - slot)
        x = x16[...]
        hg = jnp.dot(x, wg_b[...], preferred_element_type=F32)
        hu = jnp.dot(x, wu_b[...], preferred_element_type=F32)
        hid = (hg * jax.nn.sigmoid(hg) * hu).astype(BF16)
        y_ref[...] = jnp.dot(hid, wd_b[...], preferred_element_type=F32) * roww_ref[...]

        @pl.when(i == nused - 1)
        def _():
            gather_wait(1 - slot)


def _moe(h2, blk_e, nused, src, roww, w_gate, w_up, w_down):
    n_blocks = blk_e.shape[0]
    n_exp, d, de = w_gate.shape
    src_tables = _grouped_tables(src.reshape(n_blocks, MOE_BLOCK))
    grid_spec = pltpu.PrefetchScalarGridSpec(
        num_scalar_prefetch=2,
        grid=(n_blocks,),
        in_specs=[pl.BlockSpec((1, 1, src_tables.shape[2]), lambda i, be, nu: (i // TABLE_GROUP, 0, 0),
                               memory_space=pltpu.SMEM),
                  pl.BlockSpec((MOE_BLOCK, 1), lambda i, be, nu: (i, 0)),
                  pl.BlockSpec(memory_space=pl.ANY),
                  pl.BlockSpec((1, d, de), lambda i, be, nu: (be[i], 0, 0)),
                  pl.BlockSpec((1, d, de), lambda i, be, nu: (be[i], 0, 0)),
                  pl.BlockSpec((1, de, d), lambda i, be, nu: (be[i], 0, 0))],
        out_specs=pl.BlockSpec((MOE_BLOCK, d), lambda i, be, nu: (i, 0)),
        scratch_shapes=[pltpu.VMEM((2, MOE_BLOCK, d), F32),
                        pltpu.VMEM((MOE_BLOCK, d), BF16),
                        pltpu.VMEM((d, de), BF16), pltpu.VMEM((d, de), BF16), pltpu.VMEM((de, d), BF16),
                        pltpu.SemaphoreType.DMA((2,))],
    )
    return pl.pallas_call(
        _moe_kernel,
        grid_spec=grid_spec,
        out_shape=jax.ShapeDtypeStruct((n_blocks * MOE_BLOCK, d), F32),
        compiler_params=_cparams("arbitrary"),
        name="moe",
    )(blk_e, nused, src_tables, roww, h2, w_gate, w_up, w_down)


def _dispatch(eidx, wtok, rank, counts, n):
    a_tot = 2 * n
    experts = jnp.arange(N_EXPERTS, dtype=jnp.int32)
    padded = (counts + MOE_BLOCK - 1) // MOE_BLOCK * MOE_BLOCK
    pad_end = jnp.sum(jnp.where(experts[None, :] <= experts[:, None], padded[None, :], 0), axis=1)
    pad_start = pad_end - padded
    dest = jnp.sum(jnp.where(eidx[:, :, None] == experts, pad_start, 0), axis=-1) + rank
    n_blocks = -(-(a_tot + N_EXPERTS * (MOE_BLOCK - 1)) // MOE_BLOCK)
    n_blocks = -(-n_blocks // TABLE_GROUP) * TABLE_GROUP
    n_rows = n_blocks * MOE_BLOCK
    tok = jnp.tile(jnp.arange(n, dtype=jnp.int32), 2)
    upd = jnp.stack([tok, lax.bitcast_convert_type(wtok.reshape(-1), jnp.int32)], axis=1)
    rows = jnp.zeros((n_rows, 2), jnp.int32).at[dest.reshape(-1)].set(upd, unique_indices=True)
    blk_start = jnp.arange(n_blocks, dtype=jnp.int32) * MOE_BLOCK
    blk_e = jnp.minimum(jnp.sum((pad_end[None, :] <= blk_start[:, None]).astype(jnp.int32), axis=1),
                        N_EXPERTS - 1)
    nused = (pad_end[-1] // MOE_BLOCK).reshape(1)
    roww = lax.bitcast_convert_type(rows[:, 1], F32)
    return blk_e, nused, rows[:, 0].reshape(n_blocks, 1, MOE_BLOCK), roww.reshape(n_rows, 1), dest


def _final_kernel(d0_ref, d1_ref, x1_ref, gate2_ref, fg_ref, yb_hbm,
                  op_ref, os_ref, ybuf, sem, *, groups, prompt_tiles, tiles, table_group):
    i = pl.program_id(0)
    slot = i % 2
    tm = x1_ref.shape[0]
    base = (i % table_group) * tm

    def gather(first, sl):
        for r in range(tm):
            pltpu.make_async_copy(yb_hbm.at[pl.ds(d0_ref[0, 0, first + r], 1), :],
                                  ybuf.at[sl, 0, pl.ds(r, 1), :], sem.at[sl]).start(priority=0)
            pltpu.make_async_copy(yb_hbm.at[pl.ds(d1_ref[0, 0, first + r], 1), :],
                                  ybuf.at[sl, 1, pl.ds(r, 1), :], sem.at[sl]).start(priority=1)

    def gather_wait(sl):
        for k in range(2):
            pltpu.make_async_copy(yb_hbm.at[pl.ds(0, tm), :], ybuf.at[sl, k], sem.at[sl]).wait()

    @pl.when(i == 0)
    def _():
        gather(base, 0)

    gather_wait(slot)
    gather(base + tm, 1 - slot)

    def run(o_ref):
        def body(g, carry):
            r = pl.multiple_of(g * GROUP_ROWS, GROUP_ROWS)
            moe = ybuf[slot, 0, pl.ds(r, GROUP_ROWS), :] + ybuf[slot, 1, pl.ds(r, GROUP_ROWS), :]
            x2 = x1_ref[pl.ds(r, GROUP_ROWS), :] + gate2_ref[pl.ds(g, 1), :] * moe
            ms = jnp.mean(x2 * x2, axis=-1, keepdims=True)
            o_ref[pl.ds(r, GROUP_ROWS), :] = x2 * lax.rsqrt(ms + RMS_EPS) * fg_ref[...]
            return carry

        lax.fori_loop(0, groups, body, 0)

    pl.when(i < prompt_tiles)(lambda: run(op_ref))
    pl.when(i >= prompt_tiles)(lambda: run(os_ref))

    @pl.when(i == tiles - 1)
    def _():
        gather_wait(1 - slot)


def _final(x1, yb, dest, gate2_g, final_g, n_prompt, tm):
    n, d = x1.shape
    groups = tm // GROUP_ROWS
    tiles = n // tm
    prompt_tiles = n_prompt // tm
    table_group = max(g for g in range(1, TABLE_GROUP + 1) if tiles % g == 0)
    tables = [_grouped_tables(dest[k].reshape(tiles, tm), table_group) for k in range(2)]
    idx_spec = pl.BlockSpec((1, 1, tables[0].shape[2]), lambda i: (i // table_group, 0, 0),
                            memory_space=pltpu.SMEM)
    return pl.pallas_call(
        functools.partial(_final_kernel, groups=groups, prompt_tiles=prompt_tiles, tiles=tiles,
                          table_group=table_group),
        grid=(tiles,),
        in_specs=[idx_spec, idx_spec,
                  pl.BlockSpec((tm, d), lambda i: (i, 0)),
                  pl.BlockSpec((groups, d), lambda i: (i, 0)),
                  pl.BlockSpec((1, d), lambda i: (0, 0)),
                  pl.BlockSpec(memory_space=pl.ANY)],
        out_specs=_two_stream_specs(tm, d, prompt_tiles),
        out_shape=[jax.ShapeDtypeStruct((n_prompt, d), F32),
                   jax.ShapeDtypeStruct((n - n_prompt, d), F32)],
        scratch_shapes=[pltpu.VMEM((2, 2, tm, d), F32), pltpu.SemaphoreType.DMA((2,))],
        compiler_params=_cparams("arbitrary"),
        name="final",
    )(tables[0], tables[1], x1, gate2_g, final_g.reshape(1, d), yb)


def _layer(x_p, x_s, c_all, cache_k, cache_v, sconv, rel_table, lp, seq, t_new):
    (n1, n2, w_ada, b_ada, w_in, sink, w_ao, conv_w, conv_b, ln_g, ln_b, w_pw2, b_pw2, w_out,
     w_grp, b_grp, w_rt, b_rt, w_gate, w_up, w_down) = lp
    n_prompt, d = x_p.shape
    n = n_prompt + x_s.shape[0]
    batch = n_prompt // seq
    dec_batch = (n - n_prompt) // t_new
    cdim = conv_w.shape[1]
    tm = 256

    c_rows = -(-c_all.shape[0] // 8) * 8
    c_pad = jnp.pad(c_all, ((0, c_rows - c_all.shape[0]), (0, 0)))
    mods = _ada(c_pad, w_ada, b_ada)

    def per_group(k):
        m = mods[:, k * d:(k + 1) * d]
        mp = jnp.broadcast_to(m[:batch, None, :], (batch, seq // GROUP_ROWS, d))
        return jnp.concatenate([mp.reshape(batch * (seq // GROUP_ROWS), d), m[batch:batch + dec_batch]],
                               axis=0)

    shift1, scale1, gate1, shift2, scale2, gate2 = [per_group(k) for k in range(6)]

    h1 = _norm1(x_p, x_s, n1, scale1, shift1, 512)

    k_off = ATTN_DIM
    glu_off = k_off + 2 * KV_DIM
    gate_off = glu_off + 2 * cdim
    w_in_b = w_in.astype(BF16)
    tglu = 256
    w_glu = jnp.stack([w_in_b[:, glu_off:glu_off + cdim].reshape(d, cdim // tglu, tglu),
                       w_in_b[:, glu_off + cdim:gate_off].reshape(d, cdim // tglu, tglu)],
                      axis=2).reshape(d, 2 * cdim)
    tmm = 1536
    (q,) = _proj(h1, w_in_b[:, :k_off], "qkv", tmm, 512, [BF16])
    (kv,) = _proj(h1, w_in_b[:, k_off:glu_off], "qkv", tmm, 512, [BF16])
    kv_state = _kv_state(h1, w_in_b[:, k_off:glu_off], n_prompt, seq)
    (u,) = _proj(h1, w_glu, "glu", tmm, 2 * tglu, [F32])
    (gates,) = _proj(h1, w_in_b[:, gate_off:], "gate", tmm, 512, [BF16])

    bias_p = _pair_bias(rel_table)
    buf = cache_k.shape[1]
    kpos = jnp.concatenate([jnp.arange(buf, dtype=jnp.int32) - buf, jnp.arange(t_new, dtype=jnp.int32)])
    bias_s = _rel_bias(rel_table, kpos[None, :] - jnp.arange(t_new, dtype=jnp.int32)[:, None])
    sink_f = sink.astype(F32).reshape(N_KV_HEADS, Q_PER_KV, 1, 1)
    sink_kch = sink.astype(F32).reshape(N_KV_HEADS, 1, COLS_PER_KV, HEADS_PER_COL).transpose(0, 3, 1, 2)
    sink_p = jnp.broadcast_to(sink_kch[..., None, None],
                              (N_KV_HEADS, HEADS_PER_COL, 2, COLS_PER_KV, CHUNK, LANES)).reshape(
        N_KV_HEADS, HEADS_PER_COL, 2 * COLS_PER_KV * CHUNK, LANES)
    sink_s = jnp.broadcast_to(sink_f, (N_KV_HEADS, Q_PER_KV, t_new, 1)).reshape(N_KV_HEADS, Q_PER_KV * t_new, 1)
    o_p = _attn_prompt(q, kv, bias_p, sink_p, n_prompt, seq, 2)
    o_s = _attn_sample(q, kv, cache_k.reshape(dec_batch, buf, KV_DIM), cache_v.reshape(dec_batch, buf, KV_DIM),
                       bias_s, sink_s, n_prompt, t_new)

    hist = GROUP_ROWS
    sconv_pad = jnp.pad(sconv, ((0, 0), (hist - sconv.shape[1], 0), (0, 0)))
    w_rt_t = jnp.zeros((ROUTER_ROWS, d), F32)
    w_rt_t = w_rt_t.at[:N_GROUPS].set(w_grp.T).at[EXPERT_ROW0:EXPERT_ROW0 + N_EXPERTS].set(w_rt.T)
    b_rt_t = jnp.zeros((ROUTER_ROWS, 1), F32)
    b_rt_t = b_rt_t.at[:N_GROUPS, 0].set(b_grp.astype(F32)).at[EXPERT_ROW0:EXPERT_ROW0 + N_EXPERTS, 0].set(
        b_rt.astype(F32))
    nch = cdim // CONV_CHUNK
    conv_w_c = conv_w.reshape(CONV_WIDTH, nch, CONV_CHUNK).transpose(1, 0, 2)
    params = dict(conv_w=conv_w_c, conv_b=conv_b.reshape(nch, 1, CONV_CHUNK), ln_g=ln_g.reshape(1, cdim),
                  ln_b=ln_b.reshape(1, cdim), w_pw2=w_pw2.astype(BF16), b_pw2=b_pw2.reshape(1, d),
                  w_ao=w_ao.astype(BF16), w_out=w_out.astype(BF16), n2=n2.reshape(1, d),
                  w_rt=w_rt_t.astype(BF16), b_rt=b_rt_t)
    x1, h2, eidx, wtok, rank, cnt = _post(x_p, x_s, u, sconv_pad, o_p, o_s, gates, gate1, shift2, scale2,
                                             params, n_prompt, seq, tm)

    blk_e, nused, src, roww, dest = _dispatch(eidx, wtok, rank, cnt[:, 0].astype(jnp.int32), n)
    yb = _moe(h2, blk_e, nused, src, roww, w_gate, w_up, w_down)
    return x1, yb, dest, gate2, kv_state, u


def kernel(x_prompt, x_sample, c_prompt, c_sample, cache_k, cache_v, state_conv, rel_bias_table, norm1_g, norm2_g, w_ada, b_ada, w_in, attn_sink, w_attn_o, conv_w, conv_b, conv_ln_g, conv_ln_b, w_pw2, b_pw2, w_out, w_group, b_group, w_router, b_router, w_gate, w_up, w_down, final_g):
    batch, seq, d = x_prompt.shape
    dec_batch, t_new, _ = x_sample.shape
    depth = norm1_g.shape[0]
    assert depth == 1, "single trunk layer"
    assert t_new == GROUP_ROWS and seq % GROUP_ROWS == 0
    n_prompt = batch * seq
    n = n_prompt + dec_batch * t_new
    x_p = x_prompt.reshape(n_prompt, d)
    x_s = x_sample.reshape(dec_batch * t_new, d)
    c_all = jnp.concatenate([c_prompt, c_sample], axis=0)
    l = 0
    lp = (norm1_g[l], norm2_g[l], w_ada[l], b_ada[l], w_in[l], attn_sink[l], w_attn_o[l], conv_w[l],
          conv_b[l], conv_ln_g[l], conv_ln_b[l], w_pw2[l], b_pw2[l], w_out[l], w_group[l], b_group[l],
          w_router[l], b_router[l], w_gate[l], w_up[l], w_down[l])
    x1, yb, dest, gate2, kv_state, u = _layer(x_p, x_s, c_all, cache_k[l], cache_v[l], state_conv[l],
                                              rel_bias_table, lp, seq, t_new)
    out_p, out_s = _final(x1, yb, dest, gate2, final_g, n_prompt, 256)

    y_prompt = out_p.reshape(batch, seq, d)
    y_sample = out_s.reshape(dec_batch, t_new, d)
    cdim = u.shape[1]
    kvp = kv_state[:batch * WINDOW].reshape(batch, WINDOW, 2 * KV_DIM)
    new_k_prompt = kvp[..., :KV_DIM].reshape(1, batch, WINDOW, N_KV_HEADS, HEAD_DIM)
    new_v_prompt = kvp[..., KV_DIM:].reshape(1, batch, WINDOW, N_KV_HEADS, HEAD_DIM)
    new_conv_prompt = u[:n_prompt].reshape(batch, seq, cdim)[:, -(CONV_WIDTH - 1):][None]
    kvs = kv_state[batch * WINDOW:].reshape(dec_batch, t_new, 2 * KV_DIM)
    buf = cache_k.shape[2]
    k_new = kvs[..., :KV_DIM].reshape(dec_batch, t_new, N_KV_HEADS, HEAD_DIM)
    v_new = kvs[..., KV_DIM:].reshape(dec_batch, t_new, N_KV_HEADS, HEAD_DIM)
    new_k_sample = jnp.concatenate([cache_k[l], k_new], axis=1)[:, -buf:][None]
    new_v_sample = jnp.concatenate([cache_v[l], v_new], axis=1)[:, -buf:][None]
    us = u[n_prompt:].reshape(dec_batch, t_new, cdim)
    new_conv_sample = jnp.concatenate([state_conv[l], us], axis=1)[:, -(CONV_WIDTH - 1):][None]
    return (y_prompt, y_sample, new_k_prompt, new_v_prompt, new_conv_prompt,
            new_k_sample, new_v_sample, new_conv_sample)
```

```python
import functools
import math

import jax
import jax.numpy as jnp
from jax import lax
from jax.experimental import pallas as pl
from jax.experimental.pallas import tpu as pltpu

F32 = jnp.float32
BF16 = jnp.bfloat16

CHUNK = 64
HEAD_DIM = 64
N_Q_HEADS = 16
N_KV_HEADS = 4
Q_PER_KV = N_Q_HEADS // N_KV_HEADS
ATTN_DIM = N_Q_HEADS * HEAD_DIM
KV_DIM = N_KV_HEADS * HEAD_DIM
WINDOW = 128
WIN_CHUNKS = WINDOW // CHUNK
BAND = (WIN_CHUNKS + 1) * CHUNK
CONV_WIDTH = 31
NUM_BUCKETS = 32
MAX_DISTANCE = 128
N_GROUPS = 4
EXPERTS_PER_GROUP = 8
N_EXPERTS = N_GROUPS * EXPERTS_PER_GROUP
MOE_BLOCK = 256
RMS_EPS = 1e-6
LN_EPS = 1e-5
NEG_INF = -1e30

GROUP_ROWS = 32
LANES = 128
SUBLANES = 8
CONV_CHUNK = 256
ROUTER_ROWS = 64
EXPERT_ROW0 = 8
V7X_VMEM_LIMIT = 56 * 1024 * 1024


def _cparams(*sem):
    return pltpu.CompilerParams(dimension_semantics=sem, vmem_limit_bytes=V7X_VMEM_LIMIT)


def _ada_kernel(c_ref, w_ref, b_ref, o_ref):
    c = c_ref[...]
    s = c * jax.nn.sigmoid(c)
    o_ref[...] = jnp.dot(s.astype(BF16), w_ref[...].astype(BF16),
                         preferred_element_type=F32) + b_ref[...]


def _ada(c_all, w_ada, b_ada):
    rows, d = c_all.shape
    ncol = w_ada.shape[1]
    tn = 1024
    return pl.pallas_call(
        _ada_kernel,
        grid=(ncol // tn,),
        in_specs=[pl.BlockSpec((rows, d), lambda j: (0, 0)),
                  pl.BlockSpec((d, tn), lambda j: (0, j)),
                  pl.BlockSpec((1, tn), lambda j: (0, j))],
        out_specs=pl.BlockSpec((rows, tn), lambda j: (0, j)),
        out_shape=jax.ShapeDtypeStruct((rows, ncol), F32),
        compiler_params=_cparams("arbitrary"),
        name="ada",
    )(c_all, w_ada, b_ada.reshape(1, ncol))


def _rms_mod(x, gain, scale, shift):
    ms = jnp.mean(x * x, axis=-1, keepdims=True)
    y = x * lax.rsqrt(ms + RMS_EPS) * gain
    return y * (1.0 + scale) + shift


def _two_stream_specs(tm, width, prompt_tiles):
    return [pl.BlockSpec((tm, width), lambda i: (jnp.minimum(i, prompt_tiles - 1), 0)),
            pl.BlockSpec((tm, width), lambda i: (jnp.maximum(i - prompt_tiles, 0), 0))]


def _norm1_kernel(xp_ref, xs_ref, g_ref, sc_ref, sh_ref, h_ref, *, groups, prompt_tiles):
    def run(x_ref):
        def body(gi, carry):
            r = pl.multiple_of(gi * GROUP_ROWS, GROUP_ROWS)
            h = _rms_mod(x_ref[pl.ds(r, GROUP_ROWS), :], g_ref[...],
                         sc_ref[pl.ds(gi, 1), :], sh_ref[pl.ds(gi, 1), :])
            h_ref[pl.ds(r, GROUP_ROWS), :] = h.astype(h_ref.dtype)
            return carry

        lax.fori_loop(0, groups, body, 0)

    i = pl.program_id(0)
    pl.when(i < prompt_tiles)(lambda: run(xp_ref))
    pl.when(i >= prompt_tiles)(lambda: run(xs_ref))


def _norm1(x_p, x_s, gain, scale_g, shift_g, tm):
    d = x_p.shape[1]
    n = x_p.shape[0] + x_s.shape[0]
    groups = tm // GROUP_ROWS
    prompt_tiles = x_p.shape[0] // tm
    return pl.pallas_call(
        functools.partial(_norm1_kernel, groups=groups, prompt_tiles=prompt_tiles),
        grid=(n // tm,),
        in_specs=_two_stream_specs(tm, d, prompt_tiles) + [
            pl.BlockSpec((1, d), lambda i: (0, 0)),
            pl.BlockSpec((groups, d), lambda i: (i, 0)),
            pl.BlockSpec((groups, d), lambda i: (i, 0))],
        out_specs=pl.BlockSpec((tm, d), lambda i: (i, 0)),
        out_shape=jax.ShapeDtypeStruct((n, d), BF16),
        compiler_params=_cparams("arbitrary"),
        name="norm1",
    )(x_p, x_s, gain.reshape(1, d), scale_g, shift_g)


def _proj_kernel(h_ref, w_ref, *o_refs, mode):
    acc = jnp.dot(h_ref[...], w_ref[...], preferred_element_type=F32)
    if mode == "qkv":
        o_refs[0][...] = acc.astype(BF16)
    elif mode == "glu":
        half = acc.shape[1] // 2
        o_refs[0][...] = acc[:, :half] * jax.nn.sigmoid(acc[:, half:])
    else:
        o_refs[0][...] = jax.nn.sigmoid(acc).astype(BF16)


def _proj(h, w, mode, tm, tnw, out_dtypes):
    n, d = h.shape
    ncol = w.shape[1]
    tno = tnw // 2 if mode == "glu" else tnw
    nout = ncol // 2 if mode == "glu" else ncol
    outs = pl.pallas_call(
        functools.partial(_proj_kernel, mode=mode),
        grid=(n // tm, ncol // tnw),
        in_specs=[pl.BlockSpec((tm, d), lambda i, j: (i, 0)),
                  pl.BlockSpec((d, tnw), lambda i, j: (0, j))],
        out_specs=[pl.BlockSpec((tm, tno), lambda i, j: (i, j)) for _ in out_dtypes],
        out_shape=[jax.ShapeDtypeStruct((n, nout), dt) for dt in out_dtypes],
        compiler_params=_cparams("arbitrary", "arbitrary"),
        name="proj_" + mode,
    )(h, w)
    return outs


def _kv_state_kernel(h_ref, w_ref, o_ref):
    o_ref[...] = jnp.dot(h_ref[...], w_ref[...], preferred_element_type=F32)


def _kv_state(h, w_kv, n_prompt, seq):
    n, d = h.shape
    batch = n_prompt // seq
    per_seq = seq // WINDOW
    steps = batch + (n - n_prompt) // WINDOW

    def row_block(i):
        return jnp.where(i < batch, (i + 1) * per_seq - 1, n_prompt // WINDOW + i - batch)

    return pl.pallas_call(
        _kv_state_kernel,
        grid=(steps,),
        in_specs=[pl.BlockSpec((WINDOW, d), lambda i: (row_block(i), 0)),
                  pl.BlockSpec((d, 2 * KV_DIM), lambda i: (0, 0))],
        out_specs=pl.BlockSpec((WINDOW, 2 * KV_DIM), lambda i: (i, 0)),
        out_shape=jax.ShapeDtypeStruct((steps * WINDOW, 2 * KV_DIM), F32),
        compiler_params=_cparams("arbitrary"),
        name="kv_state",
    )(h, w_kv)


def _t5_bucket(rel):
    nb = NUM_BUCKETS // 2
    n = -rel
    ret = jnp.where(n < 0, nb, 0)
    n = jnp.abs(n)
    max_exact = nb // 2
    nf = jnp.maximum(n, 1).astype(F32)
    large = max_exact + (jnp.log(nf / max_exact) / math.log(MAX_DISTANCE / max_exact)
                         * (nb - max_exact)).astype(jnp.int32)
    large = jnp.minimum(large, nb - 1)
    return ret + jnp.where(n < max_exact, n, large)


def _bias_kernel(tbl_ref, bkt_ref, o_ref):
    bkt = bkt_ref[...]
    nq = bkt.shape[0]
    for k in range(N_KV_HEADS):
        for g in range(Q_PER_KV):
            acc = jnp.zeros(bkt.shape, F32)
            for b in range(NUM_BUCKETS):
                acc = jnp.where(bkt == b, tbl_ref[b, k * Q_PER_KV + g], acc)
            o_ref[k, g * nq:(g + 1) * nq, :] = acc


def _rel_bias(rel_table, rel):
    nq, nk = rel.shape
    return pl.pallas_call(
        _bias_kernel,
        in_specs=[pl.BlockSpec(memory_space=pltpu.SMEM),
                  pl.BlockSpec((nq, nk), lambda: (0, 0))],
        out_specs=pl.BlockSpec((N_KV_HEADS, Q_PER_KV * nq, nk), lambda: (0, 0, 0)),
        out_shape=jax.ShapeDtypeStruct((N_KV_HEADS, Q_PER_KV * nq, nk), F32),
        name="rel_bias",
    )(rel_table.astype(F32), _t5_bucket(rel))


PAIR_BAND = 2 * CHUNK + WINDOW
HEADS_PER_COL = LANES // HEAD_DIM
COLS_PER_KV = Q_PER_KV // HEADS_PER_COL


def _pair_bias_kernel(tbl_ref, bkt_ref, o_ref):
    for e in range(2):
        bkt = bkt_ref[e]
        for k in range(N_KV_HEADS):
            for col in range(COLS_PER_KV):
                for half in range(HEADS_PER_COL):
                    head = k * Q_PER_KV + col * HEADS_PER_COL + half
                    acc = jnp.full(bkt.shape, NEG_INF, F32)
                    for b in range(NUM_BUCKETS):
                        acc = jnp.where(bkt == b, tbl_ref[b, head], acc)
                    r0 = (e * COLS_PER_KV + col) * CHUNK
                    o_ref[k, r0:r0 + CHUNK, half * PAIR_BAND:(half + 1) * PAIR_BAND] = acc


def _pair_bias(rel_table):
    qi = jnp.arange(CHUNK, dtype=jnp.int32)[None, :, None]
    kj = jnp.arange(PAIR_BAND, dtype=jnp.int32)[None, None, :]
    e = jnp.arange(2, dtype=jnp.int32)[:, None, None]
    rel = (kj - WINDOW) - (e * CHUNK + qi)
    key_chunk = kj // CHUNK - e
    seen = (key_chunk >= 0) & (key_chunk <= WIN_CHUNKS)
    bkt = jnp.where(seen, _t5_bucket(rel), -1)
    rows = 2 * COLS_PER_KV * CHUNK
    return pl.pallas_call(
        _pair_bias_kernel,
        in_specs=[pl.BlockSpec(memory_space=pltpu.SMEM),
                  pl.BlockSpec((2, CHUNK, PAIR_BAND), lambda: (0, 0, 0))],
        out_specs=pl.BlockSpec((N_KV_HEADS, rows, HEADS_PER_COL * PAIR_BAND), lambda: (0, 0, 0)),
        out_shape=jax.ShapeDtypeStruct((N_KV_HEADS, rows, HEADS_PER_COL * PAIR_BAND), F32),
        name="pair_bias",
    )(rel_table.astype(F32), bkt)


def _attend(qc, kk_all, vv_all, bias_ref, sink_ref, mask_thr):
    nq = qc.shape[0]
    nk = kk_all.shape[0]
    pieces = []
    for k in range(N_KV_HEADS):
        qs = jnp.concatenate(
            [qc[:, (k * Q_PER_KV + g) * HEAD_DIM:(k * Q_PER_KV + g + 1) * HEAD_DIM]
             for g in range(Q_PER_KV)], axis=0)
        kk = kk_all[:, k * HEAD_DIM:(k + 1) * HEAD_DIM]
        vv = vv_all[:, k * HEAD_DIM:(k + 1) * HEAD_DIM]
        lg = lax.dot_general(qs, kk, (((1,), (1,)), ((), ())),
                             preferred_element_type=F32) * (HEAD_DIM ** -0.5) + bias_ref[k]
        if mask_thr is not None:
            col = lax.broadcasted_iota(jnp.int32, (Q_PER_KV * nq, nk), 1)
            lg = jnp.where(col < mask_thr, NEG_INF, lg)
        s = sink_ref[k]
        m = jnp.maximum(jnp.max(lg, axis=-1, keepdims=True), s)
        p = jnp.exp(lg - m)
        den = jnp.sum(p, axis=-1, keepdims=True) + jnp.exp(s - m)
        o = jnp.dot(p.astype(BF16), vv, preferred_element_type=F32) / den
        pieces.extend(o[g * nq:(g + 1) * nq, :] for g in range(Q_PER_KV))
    return jnp.concatenate(pieces, axis=1)


def _attn_prompt_kernel(q_ref, kv_ref, halo_ref, bias_ref, sink_ref, o_ref, ka, kb, va, vb, *, pairs):
    t = pl.program_id(1)
    kvcat = jnp.concatenate([halo_ref[...], kv_ref[...]], axis=0)
    low = lax.broadcasted_iota(jnp.int32, (kvcat.shape[0], LANES), 1) < HEAD_DIM
    zero = jnp.zeros((kvcat.shape[0], LANES), BF16)
    ones_low = jnp.where(low, 1.0, 0.0).astype(BF16)
    ones_high = jnp.where(low, 0.0, 1.0).astype(BF16)
    kv_cols = KV_DIM // LANES
    for col in range(2 * kv_cols):
        x = kvcat[:, col * LANES:(col + 1) * LANES]
        xs = jnp.concatenate([x[:, HEAD_DIM:], x[:, :HEAD_DIM]], axis=1)
        dst_a, dst_b = (ka, kb) if col < kv_cols else (va, vb)
        k0 = (col % kv_cols) * HEADS_PER_COL
        dst_a[k0, :, 0:LANES] = jnp.where(low, x, zero)
        dst_b[k0, :, 0:LANES] = jnp.where(low, zero, xs)
        dst_a[k0 + 1, :, 0:LANES] = jnp.where(low, xs, zero)
        dst_b[k0 + 1, :, 0:LANES] = jnp.where(low, zero, x)
    for k in range(N_KV_HEADS):
        va[k, :, LANES:2 * LANES] = ones_low
        vb[k, :, LANES:2 * LANES] = ones_high

    out_low = lax.broadcasted_iota(jnp.int32, (2 * COLS_PER_KV * CHUNK, LANES), 1) < HEAD_DIM
    for p in range(pairs):
        r0 = p * 2 * CHUNK
        for k in range(N_KV_HEADS):
            lhs = jnp.concatenate(
                [q_ref[r0 + e * CHUNK:r0 + (e + 1) * CHUNK, (k * COLS_PER_KV + c) * LANES:(k * COLS_PER_KV + c + 1) * LANES]
                 for e in range(2) for c in range(COLS_PER_KV)], axis=0)
            keys = jnp.concatenate([ka[k, r0:r0 + PAIR_BAND, :], kb[k, r0:r0 + PAIR_BAND, :]], axis=0)
            lg = lax.dot_general(lhs, keys, (((1,), (1,)), ((), ())),
                                 preferred_element_type=F32) * (HEAD_DIM ** -0.5) + bias_ref[k]
            probs, sink_terms = [], []
            for half in range(HEADS_PER_COL):
                seg = lg[:, half * PAIR_BAND:(half + 1) * PAIR_BAND]
                if p == 0:
                    kcol = lax.broadcasted_iota(jnp.int32, seg.shape, 1)
                    seg = jnp.where(kcol < jnp.where(t == 0, WINDOW, 0), NEG_INF, seg)
                s = sink_ref[k, half]
                folded = seg[:, :LANES]
                for j in range(1, PAIR_BAND // LANES):
                    folded = jnp.maximum(folded, seg[:, j * LANES:(j + 1) * LANES])
                m = jnp.maximum(jnp.max(folded, axis=-1, keepdims=True), s)
                pr = jnp.exp(seg - jnp.concatenate([m] * (PAIR_BAND // LANES), axis=1))
                sink_terms.append(jnp.exp(s - m))
                probs.append(pr.astype(BF16))
            vals = jnp.concatenate([va[k, r0:r0 + PAIR_BAND, :], vb[k, r0:r0 + PAIR_BAND, :]], axis=0)
            oe = jnp.dot(jnp.concatenate(probs, axis=1), vals, preferred_element_type=F32)
            o = oe[:, :LANES] / (oe[:, LANES:] + jnp.where(out_low, sink_terms[0], sink_terms[1]))
            for e in range(2):
                for c in range(COLS_PER_KV):
                    rr = (e * COLS_PER_KV + c) * CHUNK
                    o_ref[r0 + e * CHUNK:r0 + (e + 1) * CHUNK,
                          (k * COLS_PER_KV + c) * LANES:(k * COLS_PER_KV + c + 1) * LANES] = (
                        o[rr:rr + CHUNK, :].astype(o_ref.dtype))


def _attn_prompt(q, kv, bias, sink_rows, n_prompt, seq, pairs):
    rows = pairs * 2 * CHUNK
    tiles = seq // rows
    halo_per_tile = rows // WINDOW
    keys = WINDOW + rows
    return pl.pallas_call(
        functools.partial(_attn_prompt_kernel, pairs=pairs),
        grid=(n_prompt // seq, tiles),
        scratch_shapes=[pltpu.VMEM((N_KV_HEADS, keys, LANES), BF16) for _ in range(2)]
        + [pltpu.VMEM((N_KV_HEADS, keys, 2 * LANES), BF16) for _ in range(2)],
        in_specs=[pl.BlockSpec((rows, ATTN_DIM), lambda b, t: (b * tiles + t, 0)),
                  pl.BlockSpec((rows, 2 * KV_DIM), lambda b, t: (b * tiles + t, 0)),
                  pl.BlockSpec((WINDOW, 2 * KV_DIM),
                               lambda b, t: (jnp.maximum((b * tiles + t) * halo_per_tile - 1, 0), 0)),
                  pl.BlockSpec(bias.shape, lambda b, t: (0, 0, 0)),
                  pl.BlockSpec(sink_rows.shape, lambda b, t: (0, 0, 0, 0))],
        out_specs=pl.BlockSpec((rows, ATTN_DIM), lambda b, t: (b * tiles + t, 0)),
        out_shape=jax.ShapeDtypeStruct((n_prompt, ATTN_DIM), BF16),
        compiler_params=_cparams("arbitrary", "arbitrary"),
        name="attn_prompt",
    )(q, kv, kv, bias, sink_rows)


def _attn_sample_kernel(q_ref, kv_ref, ck_ref, cv_ref, bias_ref, sink_ref, o_ref):
    kv = kv_ref[...]
    kk = jnp.concatenate([ck_ref[0].astype(BF16), kv[:, :KV_DIM]], axis=0)
    vv = jnp.concatenate([cv_ref[0].astype(BF16), kv[:, KV_DIM:]], axis=0)
    o = _attend(q_ref[...], kk, vv, bias_ref, sink_ref, None)
    o_ref[...] = o.astype(o_ref.dtype)


def _attn_sample(q, kv, cache_k, cache_v, bias, sink_rows, n_prompt, t_new):
    dec_batch, buf, _ = cache_k.shape
    first = n_prompt // t_new
    return pl.pallas_call(
        _attn_sample_kernel,
        grid=(dec_batch,),
        in_specs=[pl.BlockSpec((t_new, ATTN_DIM), lambda s: (first + s, 0)),
                  pl.BlockSpec((t_new, 2 * KV_DIM), lambda s: (first + s, 0)),
                  pl.BlockSpec((1, buf, KV_DIM), lambda s: (s, 0, 0)),
                  pl.BlockSpec((1, buf, KV_DIM), lambda s: (s, 0, 0)),
                  pl.BlockSpec(bias.shape, lambda s: (0, 0, 0)),
                  pl.BlockSpec(sink_rows.shape, lambda s: (0, 0, 0))],
        out_specs=pl.BlockSpec((t_new, ATTN_DIM), lambda s: (s, 0)),
        out_shape=jax.ShapeDtypeStruct((dec_batch * t_new, ATTN_DIM), BF16),
        compiler_params=_cparams("arbitrary"),
        name="attn_sample",
    )(q, kv, cache_k, cache_v, bias, sink_rows)


def _route(lt, eidx_ref, wtok_ref, rank_ref, cnt_ref, cnt_scr):
    gl = [lt[r:r + 1, :] for r in range(N_GROUPS)]
    gmax = gl[0]
    gsel = jnp.zeros(gl[0].shape, jnp.int32)
    for r in range(1, N_GROUPS):
        better = gl[r] > gmax
        gsel = jnp.where(better, r, gsel)
        gmax = jnp.maximum(gmax, gl[r])
    gexp = [jnp.exp(v - gmax) for v in gl]
    gsum = gexp[0]
    for r in range(1, N_GROUPS):
        gsum = gsum + gexp[r]
    psel = jnp.zeros(gl[0].shape, F32)
    for r in range(N_GROUPS):
        psel = jnp.where(gsel == r, gexp[r] / gsum, psel)
    el = jnp.zeros((EXPERTS_PER_GROUP, lt.shape[1]), F32)
    for r in range(N_GROUPS):
        lo = EXPERT_ROW0 + r * EXPERTS_PER_GROUP
        el = jnp.where(gsel == r, lt[lo:lo + EXPERTS_PER_GROUP, :], el)
    emax = jnp.max(el, axis=0, keepdims=True)
    ee = jnp.exp(el - emax)
    pin = ee / jnp.sum(ee, axis=0, keepdims=True)
    idx = lax.broadcasted_iota(jnp.int32, pin.shape, 0)
    p1 = jnp.max(pin, axis=0, keepdims=True)
    i1 = jnp.min(jnp.where(pin == p1, idx, EXPERTS_PER_GROUP), axis=0, keepdims=True)
    rest = jnp.where(idx == i1, -1.0, pin)
    p2 = jnp.max(rest, axis=0, keepdims=True)
    i2 = jnp.min(jnp.where(rest == p2, idx, EXPERTS_PER_GROUP), axis=0, keepdims=True)
    tot = p1 + p2
    e1 = gsel * EXPERTS_PER_GROUP + i1
    e2 = gsel * EXPERTS_PER_GROUP + i2
    eidx_ref[0:1, :] = e1
    eidx_ref[1:2, :] = e2
    wtok_ref[0:1, :] = psel * p1 / tot
    wtok_ref[1:2, :] = psel * p2 / tot

    t = lt.shape[1]
    eiota = lax.broadcasted_iota(jnp.int32, (N_EXPERTS, t), 0)
    oh1 = (eiota == e1).astype(F32)
    oh2 = (eiota == e2).astype(F32)
    both = oh1 + oh2
    before = (lax.broadcasted_iota(jnp.int32, (t, t), 0)
              < lax.broadcasted_iota(jnp.int32, (t, t), 1)).astype(BF16)
    prior = jnp.dot(both.astype(BF16), before, preferred_element_type=F32) + cnt_scr[:, 0:1]
    rank_ref[0:1, :] = jnp.sum(oh1 * prior, axis=0, keepdims=True).astype(jnp.int32)
    rank_ref[1:2, :] = jnp.sum(oh2 * prior, axis=0, keepdims=True).astype(jnp.int32)
    cnt_scr[...] = cnt_scr[...] + jnp.sum(both, axis=1, keepdims=True)
    cnt_ref[...] = cnt_scr[...]


def _post_kernel(xp_ref, xs_ref, u_ref, uhalo_ref, sconv_ref, op_ref, os_ref, gates_ref,
                 gate1_ref, shift2_ref, scale2_ref,
                 cw_ref, cb_ref, lng_ref, lnb_ref, wpw2_ref, bpw2_ref, wao_ref, wout_ref,
                 n2_ref, wrt_ref, brt_ref,
                 x1_ref, h2_ref, eidx_ref, wtok_ref, rank_ref, cnt_ref,
                 uext, shift_scr, d_scr, s_scr, o_scr, mix_scr, h2b_scr, cnt_scr,
                 *, groups, prompt_tiles, tiles_per_seq):
    i = pl.program_id(0)
    d_model = xp_ref.shape[1]

    @pl.when(i == 0)
    def _():
        cnt_scr[...] = jnp.zeros(cnt_scr.shape, F32)
    nch = cw_ref.shape[0]
    cw = uext.shape[2]
    hist = uext.shape[1] - GROUP_ROWS
    lead = hist - (CONV_WIDTH - 1)

    def put_hist(g, rows):
        for c in range(nch):
            uext[g * nch + c, 0:hist, :] = rows[:, c * cw:(c + 1) * cw]

    @pl.when(i < prompt_tiles)
    def _():
        first = (i % tiles_per_seq) == 0
        put_hist(0, jnp.where(first, 0.0, uhalo_ref[...]))
        for g in range(1, groups):
            put_hist(g, u_ref[(g - 1) * GROUP_ROWS:g * GROUP_ROWS, :])
        o_scr[...] = op_ref[...]

    @pl.when(i >= prompt_tiles)
    def _():
        for g in range(groups):
            put_hist(g, sconv_ref[g])
        o_scr[...] = os_ref[...]

    for g in range(groups):
        for c in range(nch):
            uext[g * nch + c, hist:hist + GROUP_ROWS, :] = (
                u_ref[g * GROUP_ROWS:(g + 1) * GROUP_ROWS, c * cw:(c + 1) * cw])

    span = shift_scr.shape[1]

    def conv_chunk(k, carry):
        c = k % nch
        win_all = uext[k]
        for s in range(1, SUBLANES):
            shift_scr[s, :, :] = pltpu.roll(win_all, win_all.shape[0] - s, 0)[0:span, :]
        acc = None
        for j in range(CONV_WIDTH):
            base, s = divmod(lead + j, SUBLANES)
            rows = pl.ds(base * SUBLANES, GROUP_ROWS)
            win = uext[k, rows, :] if s == 0 else shift_scr[s, rows, :]
            term = win * cw_ref[c, j:j + 1, :]
            acc = term if acc is None else acc + term
        d_scr[k] = acc + cb_ref[c]
        return carry

    lax.fori_loop(0, groups * nch, conv_chunk, 0)

    for g in range(groups):
        dd = jnp.concatenate([d_scr[g * nch + c] for c in range(nch)], axis=1)
        mu = jnp.mean(dd, axis=-1, keepdims=True)
        var = jnp.mean(jnp.square(dd - mu), axis=-1, keepdims=True)
        y = (dd - mu) * lax.rsqrt(var + LN_EPS) * lng_ref[...] + lnb_ref[...]
        s_scr[g * GROUP_ROWS:(g + 1) * GROUP_ROWS, :] = (y * jax.nn.sigmoid(y)).astype(BF16)

    conv_out = jnp.dot(s_scr[...], wpw2_ref[...], preferred_element_type=F32) + bpw2_ref[...]
    attn_out = jnp.dot(o_scr[...], wao_ref[...], preferred_element_type=F32)
    merged = (gates_ref[:, :d_model].astype(F32) * attn_out
              + gates_ref[:, d_model:].astype(F32) * conv_out)
    mix_scr[...] = jnp.dot(merged.astype(BF16), wout_ref[...], preferred_element_type=F32)

    def residual(x_ref):
        def res_group(g, carry):
            r = pl.multiple_of(g * GROUP_ROWS, GROUP_ROWS)
            x1 = (x_ref[pl.ds(r, GROUP_ROWS), :]
                  + gate1_ref[pl.ds(g, 1), :] * mix_scr[pl.ds(r, GROUP_ROWS), :])
            x1_ref[pl.ds(r, GROUP_ROWS), :] = x1
            h = _rms_mod(x1, n2_ref[...], scale2_ref[pl.ds(g, 1), :], shift2_ref[pl.ds(g, 1), :])
            h2b_scr[pl.ds(r, GROUP_ROWS), :] = h.astype(BF16)
            h2_ref[pl.ds(r, GROUP_ROWS), :] = h
            return carry

        lax.fori_loop(0, groups, res_group, 0)

    pl.when(i < prompt_tiles)(lambda: residual(xp_ref))
    pl.when(i >= prompt_tiles)(lambda: residual(xs_ref))

    lt = lax.dot_general(wrt_ref[...], h2b_scr[...], (((1,), (1,)), ((), ())),
                         preferred_element_type=F32) + brt_ref[...]
    _route(lt, eidx_ref, wtok_ref, rank_ref, cnt_ref, cnt_scr)


def _post(x_p, x_s, u, sconv_pad, o_p, o_s, gates, gate1_g, shift2_g, scale2_g, p, n_prompt, seq, tm):
    d = x_p.shape[1]
    n = x_p.shape[0] + x_s.shape[0]
    cdim = u.shape[1]
    nch = cdim // CONV_CHUNK
    groups = tm // GROUP_ROWS
    prompt_tiles = n_prompt // tm
    const = lambda shape: pl.BlockSpec(shape, lambda i: (0,) * len(shape))
    row = lambda w: pl.BlockSpec((tm, w), lambda i: (i, 0))
    grp = pl.BlockSpec((groups, d), lambda i: (i, 0))
    return pl.pallas_call(
        functools.partial(_post_kernel, groups=groups, prompt_tiles=prompt_tiles,
                          tiles_per_seq=seq // tm),
        grid=(n // tm,),
        in_specs=_two_stream_specs(tm, d, prompt_tiles) + [
                  row(cdim),
                  pl.BlockSpec((GROUP_ROWS, cdim), lambda i: (jnp.maximum(i * groups - 1, 0), 0)),
                  pl.BlockSpec((groups, GROUP_ROWS, cdim),
                               lambda i: (jnp.maximum(i - prompt_tiles, 0), 0, 0))]
                 + _two_stream_specs(tm, ATTN_DIM, prompt_tiles) + [
                  row(2 * d), grp, grp, grp,
                  const((nch, CONV_WIDTH, CONV_CHUNK)), const((nch, 1, CONV_CHUNK)),
                  const((1, cdim)), const((1, cdim)),
                  const((cdim, d)), const((1, d)), const((ATTN_DIM, d)), const((d, d)),
                  const((1, d)), const((ROUTER_ROWS, d)), const((ROUTER_ROWS, 1))],
        out_specs=[row(d),
                   row(d),
                   pl.BlockSpec((2, tm), lambda i: (0, i)),
                   pl.BlockSpec((2, tm), lambda i: (0, i)),
                   pl.BlockSpec((2, tm), lambda i: (0, i)),
                   pl.BlockSpec((N_EXPERTS, LANES), lambda i: (0, 0))],
        out_shape=[jax.ShapeDtypeStruct((n, d), F32),
                   jax.ShapeDtypeStruct((n, d), F32),
                   jax.ShapeDtypeStruct((2, n), jnp.int32),
                   jax.ShapeDtypeStruct((2, n), F32),
                   jax.ShapeDtypeStruct((2, n), jnp.int32),
                   jax.ShapeDtypeStruct((N_EXPERTS, LANES), F32)],
        scratch_shapes=[pltpu.VMEM((groups * nch, 2 * GROUP_ROWS, CONV_CHUNK), F32),
                        pltpu.VMEM((SUBLANES, 2 * GROUP_ROWS - SUBLANES, CONV_CHUNK), F32),
                        pltpu.VMEM((groups * nch, GROUP_ROWS, CONV_CHUNK), F32),
                        pltpu.VMEM((tm, cdim), BF16),
                        pltpu.VMEM((tm, ATTN_DIM), BF16),
                        pltpu.VMEM((tm, d), F32),
                        pltpu.VMEM((tm, d), BF16),
                        pltpu.VMEM((N_EXPERTS, LANES), F32)],
        compiler_params=_cparams("arbitrary"),
        name="post",
    )(x_p, x_s, u, u, sconv_pad, o_p, o_s, gates, gate1_g, shift2_g, scale2_g,
      p["conv_w"], p["conv_b"], p["ln_g"], p["ln_b"], p["w_pw2"], p["b_pw2"], p["w_ao"], p["w_out"],
      p["n2"], p["w_rt"], p["b_rt"])


TABLE_GROUP = 8


def _grouped_tables(tab, group=TABLE_GROUP):
    steps, w = tab.shape
    assert steps % group == 0
    nxt = jnp.concatenate([tab[group::group], tab[-1:]], axis=0)
    return jnp.concatenate([tab.reshape(steps // group, group * w), nxt], axis=1).reshape(
        steps // group, 1, (group + 1) * w)

def _moe_kernel(blk_e_ref, nused_ref, src_ref, roww_ref,
                xp_ref, wg_ref, wu_ref, wd_ref, y_ref,
                xbuf, x16, wg_b, wu_b, wd_b, gsem):
    i = pl.program_id(0)
    nused = nused_ref[0]
    slot = i % 2
    base = (i % TABLE_GROUP) * MOE_BLOCK

    def gather(first, sl):
        for r in range(MOE_BLOCK):
            pltpu.make_async_copy(h2_hbm.at[pl.ds(src_ref[0, 0, first + r], 1), :],
                                  xbuf.at[sl, pl.ds(r, 1), :], gsem.at[sl]).start(priority=r % 2)

    def gather_wait(sl):
        pltpu.make_async_copy(h2_hbm.at[pl.ds(0, MOE_BLOCK), :], xbuf.at[sl], gsem.at[sl]).wait()

    @pl.when(i >= nused)
    def _():
        y_ref[...] = jnp.zeros(y_ref.shape, F32)

    @pl.when(i < nused)
    def _():
        changed = jnp.logical_or(i == 0, blk_e_ref[i] != blk_e_ref[jnp.maximum(i - 1, 0)])

        @pl.when(changed)
        def _():
            wg_b[...] = wg_ref[0].astype(BF16)
            wu_b[...] = wu_ref[0].astype(BF16)
            wd_b[...] = wd_ref[0].astype(BF16)

        x = xp_ref[...].astype(BF16)
        hg = jnp.dot(x, wg_b[...], preferred_element_type=F32)
        hu = jnp.dot(x, wu_b[...], preferred_element_type=F32)
        hid = (hg * jax.nn.sigmoid(hg) * hu).astype(BF16)
        y_ref[...] = jnp.dot(hid, wd_b[...], preferred_element_type=F32) * roww_ref[...]


def _moe(h2, blk_e, nused, src, roww, w_gate, w_up, w_down):
    n_blocks = blk_e.shape[0]
    n_exp, d, de = w_gate.shape
    src_tables = _grouped_tables(src.reshape(n_blocks, MOE_BLOCK))
    xb = jnp.take(h2, src.reshape(-1), axis=0)
    grid_spec = pltpu.PrefetchScalarGridSpec(
        num_scalar_prefetch=2,
        grid=(n_blocks,),
        in_specs=[pl.BlockSpec((1, 1, src_tables.shape[2]), lambda i, be, nu: (i // TABLE_GROUP, 0, 0),
                               memory_space=pltpu.SMEM),
                  pl.BlockSpec((MOE_BLOCK, 1), lambda i, be, nu: (i, 0)),
                  pl.BlockSpec((MOE_BLOCK, d), lambda i, be, nu: (i, 0)),
                  pl.BlockSpec((1, d, de), lambda i, be, nu: (be[i], 0, 0)),
                  pl.BlockSpec((1, d, de), lambda i, be, nu: (be[i], 0, 0)),
                  pl.BlockSpec((1, de, d), lambda i, be, nu: (be[i], 0, 0))],
        out_specs=pl.BlockSpec((MOE_BLOCK, d), lambda i, be, nu: (i, 0)),
        scratch_shapes=[pltpu.VMEM((2, MOE_BLOCK, d), F32),
                        pltpu.VMEM((MOE_BLOCK, d), BF16),
                        pltpu.VMEM((d, de), BF16), pltpu.VMEM((d, de), BF16), pltpu.VMEM((de, d), BF16),
                        pltpu.SemaphoreType.DMA((2,))],
    )
    return pl.pallas_call(
        _moe_kernel,
        grid_spec=grid_spec,
        out_shape=jax.ShapeDtypeStruct((n_blocks * MOE_BLOCK, d), F32),
        compiler_params=_cparams("arbitrary"),
        name="moe",
    )(blk_e, nused, src_tables, roww, xb, w_gate, w_up, w_down)


def _dispatch(eidx, wtok, rank, counts, n):
    a_tot = 2 * n
    experts = jnp.arange(N_EXPERTS, dtype=jnp.int32)
    padded = (counts + MOE_BLOCK - 1) // MOE_BLOCK * MOE_BLOCK
    pad_end = jnp.sum(jnp.where(experts[None, :] <= experts[:, None], padded[None, :], 0), axis=1)
    pad_start = pad_end - padded
    dest = jnp.sum(jnp.where(eidx[:, :, None] == experts, pad_start, 0), axis=-1) + rank
    n_blocks = -(-(a_tot + N_EXPERTS * (MOE_BLOCK - 1)) // MOE_BLOCK)
    n_blocks = -(-n_blocks // TABLE_GROUP) * TABLE_GROUP
    n_rows = n_blocks * MOE_BLOCK
    tok = jnp.tile(jnp.arange(n, dtype=jnp.int32), 2)
    upd = jnp.stack([tok, lax.bitcast_convert_type(wtok.reshape(-1), jnp.int32)], axis=1)
    rows = jnp.zeros((n_rows, 2), jnp.int32).at[dest.reshape(-1)].set(upd, unique_indices=True)
    blk_start = jnp.arange(n_blocks, dtype=jnp.int32) * MOE_BLOCK
    blk_e = jnp.minimum(jnp.sum((pad_end[None, :] <= blk_start[:, None]).astype(jnp.int32), axis=1),
                        N_EXPERTS - 1)
    nused = (pad_end[-1] // MOE_BLOCK).reshape(1)
    roww = lax.bitcast_convert_type(rows[:, 1], F32)
    return blk_e, nused, rows[:, 0].reshape(n_blocks, 1, MOE_BLOCK), roww.reshape(n_rows, 1), dest


def _final_kernel(d0_ref, d1_ref, x1_ref, gate2_ref, fg_ref, yb_hbm,
                  op_ref, os_ref, ybuf, sem, *, groups, prompt_tiles, tiles, table_group):
    i = pl.program_id(0)
    slot = i % 2
    tm = x1_ref.shape[0]
    base = (i % table_group) * tm

    def gather(first, sl):
        for r in range(tm):
            pltpu.make_async_copy(yb_hbm.at[pl.ds(d0_ref[0, 0, first + r], 1), :],
                                  ybuf.at[sl, 0, pl.ds(r, 1), :], sem.at[sl]).start(priority=0)
            pltpu.make_async_copy(yb_hbm.at[pl.ds(d1_ref[0, 0, first + r], 1), :],
                                  ybuf.at[sl, 1, pl.ds(r, 1), :], sem.at[sl]).start(priority=1)

    def gather_wait(sl):
        for k in range(2):
            pltpu.make_async_copy(yb_hbm.at[pl.ds(0, tm), :], ybuf.at[sl, k], sem.at[sl]).wait()

    @pl.when(i == 0)
    def _():
        gather(base, 0)

    gather_wait(slot)
    gather(base + tm, 1 - slot)

    def run(o_ref):
        def body(g, carry):
            r = pl.multiple_of(g * GROUP_ROWS, GROUP_ROWS)
            moe = ybuf[slot, 0, pl.ds(r, GROUP_ROWS), :] + ybuf[slot, 1, pl.ds(r, GROUP_ROWS), :]
            x2 = x1_ref[pl.ds(r, GROUP_ROWS), :] + gate2_ref[pl.ds(g, 1), :] * moe
            ms = jnp.mean(x2 * x2, axis=-1, keepdims=True)
            o_ref[pl.ds(r, GROUP_ROWS), :] = x2 * lax.rsqrt(ms + RMS_EPS) * fg_ref[...]
            return carry

        lax.fori_loop(0, groups, body, 0)

    pl.when(i < prompt_tiles)(lambda: run(op_ref))
    pl.when(i >= prompt_tiles)(lambda: run(os_ref))

    @pl.when(i == tiles - 1)
    def _():
        gather_wait(1 - slot)


def _final(x1, yb, dest, gate2_g, final_g, n_prompt, tm):
    n, d = x1.shape
    groups = tm // GROUP_ROWS
    tiles = n // tm
    prompt_tiles = n_prompt // tm
    table_group = max(g for g in range(1, TABLE_GROUP + 1) if tiles % g == 0)
    tables = [_grouped_tables(dest[k].reshape(tiles, tm), table_group) for k in range(2)]
    idx_spec = pl.BlockSpec((1, 1, tables[0].shape[2]), lambda i: (i // table_group, 0, 0),
                            memory_space=pltpu.SMEM)
    return pl.pallas_call(
        functools.partial(_final_kernel, groups=groups, prompt_tiles=prompt_tiles, tiles=tiles,
                          table_group=table_group),
        grid=(tiles,),
        in_specs=[idx_spec, idx_spec,
                  pl.BlockSpec((tm, d), lambda i: (i, 0)),
                  pl.BlockSpec((groups, d), lambda i: (i, 0)),
                  pl.BlockSpec((1, d), lambda i: (0, 0)),
                  pl.BlockSpec(memory_space=pl.ANY)],
        out_specs=_two_stream_specs(tm, d, prompt_tiles),
        out_shape=[jax.ShapeDtypeStruct((n_prompt, d), F32),
                   jax.ShapeDtypeStruct((n - n_prompt, d), F32)],
        scratch_shapes=[pltpu.VMEM((2, 2, tm, d), F32), pltpu.SemaphoreType.DMA((2,))],
        compiler_params=_cparams("arbitrary"),
        name="final",
    )(tables[0], tables[1], x1, gate2_g, final_g.reshape(1, d), yb)


def _layer(x_p, x_s, c_all, cache_k, cache_v, sconv, rel_table, lp, seq, t_new):
    (n1, n2, w_ada, b_ada, w_in, sink, w_ao, conv_w, conv_b, ln_g, ln_b, w_pw2, b_pw2, w_out,
     w_grp, b_grp, w_rt, b_rt, w_gate, w_up, w_down) = lp
    n_prompt, d = x_p.shape
    n = n_prompt + x_s.shape[0]
    batch = n_prompt // seq
    dec_batch = (n - n_prompt) // t_new
    cdim = conv_w.shape[1]
    tm = 256

    c_rows = -(-c_all.shape[0] // 8) * 8
    c_pad = jnp.pad(c_all, ((0, c_rows - c_all.shape[0]), (0, 0)))
    mods = _ada(c_pad, w_ada, b_ada)

    def per_group(k):
        m = mods[:, k * d:(k + 1) * d]
        mp = jnp.broadcast_to(m[:batch, None, :], (batch, seq // GROUP_ROWS, d))
        return jnp.concatenate([mp.reshape(batch * (seq // GROUP_ROWS), d), m[batch:batch + dec_batch]],
                               axis=0)

    shift1, scale1, gate1, shift2, scale2, gate2 = [per_group(k) for k in range(6)]

    h1 = _norm1(x_p, x_s, n1, scale1, shift1, 512)

    k_off = ATTN_DIM
    glu_off = k_off + 2 * KV_DIM
    gate_off = glu_off + 2 * cdim
    w_in_b = w_in.astype(BF16)
    tglu = 256
    w_glu = jnp.stack([w_in_b[:, glu_off:glu_off + cdim].reshape(d, cdim // tglu, tglu),
                       w_in_b[:, glu_off + cdim:gate_off].reshape(d, cdim // tglu, tglu)],
                      axis=2).reshape(d, 2 * cdim)
    tmm = 1536
    (q,) = _proj(h1, w_in_b[:, :k_off], "qkv", tmm, 512, [BF16])
    (kv,) = _proj(h1, w_in_b[:, k_off:glu_off], "qkv", tmm, 512, [BF16])
    kv_state = _kv_state(h1, w_in_b[:, k_off:glu_off], n_prompt, seq)
    (u,) = _proj(h1, w_glu, "glu", tmm, 2 * tglu, [F32])
    (gates,) = _proj(h1, w_in_b[:, gate_off:], "gate", tmm, 512, [BF16])

    bias_p = _pair_bias(rel_table)
    buf = cache_k.shape[1]
    kpos = jnp.concatenate([jnp.arange(buf, dtype=jnp.int32) - buf, jnp.arange(t_new, dtype=jnp.int32)])
    bias_s = _rel_bias(rel_table, kpos[None, :] - jnp.arange(t_new, dtype=jnp.int32)[:, None])
    sink_f = sink.astype(F32).reshape(N_KV_HEADS, Q_PER_KV, 1, 1)
    sink_kch = sink.astype(F32).reshape(N_KV_HEADS, 1, COLS_PER_KV, HEADS_PER_COL).transpose(0, 3, 1, 2)
    sink_p = jnp.broadcast_to(sink_kch[..., None, None],
                              (N_KV_HEADS, HEADS_PER_COL, 2, COLS_PER_KV, CHUNK, LANES)).reshape(
        N_KV_HEADS, HEADS_PER_COL, 2 * COLS_PER_KV * CHUNK, LANES)
    sink_s = jnp.broadcast_to(sink_f, (N_KV_HEADS, Q_PER_KV, t_new, 1)).reshape(N_KV_HEADS, Q_PER_KV * t_new, 1)
    o_p = _attn_prompt(q, kv, bias_p, sink_p, n_prompt, seq, 2)
    o_s = _attn_sample(q, kv, cache_k.reshape(dec_batch, buf, KV_DIM), cache_v.reshape(dec_batch, buf, KV_DIM),
                       bias_s, sink_s, n_prompt, t_new)

    hist = GROUP_ROWS
    sconv_pad = jnp.pad(sconv, ((0, 0), (hist - sconv.shape[1], 0), (0, 0)))
    w_rt_t = jnp.zeros((ROUTER_ROWS, d), F32)
    w_rt_t = w_rt_t.at[:N_GROUPS].set(w_grp.T).at[EXPERT_ROW0:EXPERT_ROW0 + N_EXPERTS].set(w_rt.T)
    b_rt_t = jnp.zeros((ROUTER_ROWS, 1), F32)
    b_rt_t = b_rt_t.at[:N_GROUPS, 0].set(b_grp.astype(F32)).at[EXPERT_ROW0:EXPERT_ROW0 + N_EXPERTS, 0].set(
        b_rt.astype(F32))
    nch = cdim // CONV_CHUNK
    conv_w_c = conv_w.reshape(CONV_WIDTH, nch, CONV_CHUNK).transpose(1, 0, 2)
    params = dict(conv_w=conv_w_c, conv_b=conv_b.reshape(nch, 1, CONV_CHUNK), ln_g=ln_g.reshape(1, cdim),
                  ln_b=ln_b.reshape(1, cdim), w_pw2=w_pw2.astype(BF16), b_pw2=b_pw2.reshape(1, d),
                  w_ao=w_ao.astype(BF16), w_out=w_out.astype(BF16), n2=n2.reshape(1, d),
                  w_rt=w_rt_t.astype(BF16), b_rt=b_rt_t)
    x1, h2, eidx, wtok, rank, cnt = _post(x_p, x_s, u, sconv_pad, o_p, o_s, gates, gate1, shift2, scale2,
                                             params, n_prompt, seq, tm)

    blk_e, nused, src, roww, dest = _dispatch(eidx, wtok, rank, cnt[:, 0].astype(jnp.int32), n)
    yb = _moe(h2, blk_e, nused, src, roww, w_gate, w_up, w_down)
    return x1, yb, dest, gate2, kv_state, u


def kernel(x_prompt, x_sample, c_prompt, c_sample, cache_k, cache_v, state_conv, rel_bias_table, norm1_g, norm2_g, w_ada, b_ada, w_in, attn_sink, w_attn_o, conv_w, conv_b, conv_ln_g, conv_ln_b, w_pw2, b_pw2, w_out, w_group, b_group, w_router, b_router, w_gate, w_up, w_down, final_g):
    batch, seq, d = x_prompt.shape
    dec_batch, t_new, _ = x_sample.shape
    depth = norm1_g.shape[0]
    assert depth == 1, "single trunk layer"
    assert t_new == GROUP_ROWS and seq % GROUP_ROWS == 0
    n_prompt = batch * seq
    n = n_prompt + dec_batch * t_new
    x_p = x_prompt.reshape(n_prompt, d)
    x_s = x_sample.reshape(dec_batch * t_new, d)
    c_all = jnp.concatenate([c_prompt, c_sample], axis=0)
    l = 0
    lp = (norm1_g[l], norm2_g[l], w_ada[l], b_ada[l], w_in[l], attn_sink[l], w_attn_o[l], conv_w[l],
          conv_b[l], conv_ln_g[l], conv_ln_b[l], w_pw2[l], b_pw2[l], w_out[l], w_group[l], b_group[l],
          w_router[l], b_router[l], w_gate[l], w_up[l], w_down[l])
    x1, yb, dest, gate2, kv_state, u = _layer(x_p, x_s, c_all, cache_k[l], cache_v[l], state_conv[l],
                                              rel_bias_table, lp, seq, t_new)
    out_p, out_s = _final(x1, yb, dest, gate2, final_g, n_prompt, 256)

    y_prompt = out_p.reshape(batch, seq, d)
    y_sample = out_s.reshape(dec_batch, t_new, d)
    cdim = u.shape[1]
    kvp = kv_state[:batch * WINDOW].reshape(batch, WINDOW, 2 * KV_DIM)
    new_k_prompt = kvp[..., :KV_DIM].reshape(1, batch, WINDOW, N_KV_HEADS, HEAD_DIM)
    new_v_prompt = kvp[..., KV_DIM:].reshape(1, batch, WINDOW, N_KV_HEADS, HEAD_DIM)
    new_conv_prompt = u[:n_prompt].reshape(batch, seq, cdim)[:, -(CONV_WIDTH - 1):][None]
    kvs = kv_state[batch * WINDOW:].reshape(dec_batch, t_new, 2 * KV_DIM)
    buf = cache_k.shape[2]
    k_new = kvs[..., :KV_DIM].reshape(dec_batch, t_new, N_KV_HEADS, HEAD_DIM)
    v_new = kvs[..., KV_DIM:].reshape(dec_batch, t_new, N_KV_HEADS, HEAD_DIM)
    new_k_sample = jnp.concatenate([cache_k[l], k_new], axis=1)[:, -buf:][None]
    new_v_sample = jnp.concatenate([cache_v[l], v_new], axis=1)[:, -buf:][None]
    us = u[n_prompt:].reshape(dec_batch, t_new, cdim)
    new_conv_sample = jnp.concatenate([state_conv[l], us], axis=1)[:, -(CONV_WIDTH - 1):][None]
    return (y_prompt, y_sample, new_k_prompt, new_v_prompt, new_conv_prompt,
            new_k_sample, new_v_sample, new_conv_sample)
```

```python
import functools
import math

import jax
import jax.numpy as jnp
from jax import lax
from jax.experimental import pallas as pl
from jax.experimental.pallas import tpu as pltpu

F32 = jnp.float32
BF16 = jnp.bfloat16

CHUNK = 64
HEAD_DIM = 64
N_Q_HEADS = 16
N_KV_HEADS = 4
Q_PER_KV = N_Q_HEADS // N_KV_HEADS
ATTN_DIM = N_Q_HEADS * HEAD_DIM
KV_DIM = N_KV_HEADS * HEAD_DIM
WINDOW = 128
WIN_CHUNKS = WINDOW // CHUNK
BAND = (WIN_CHUNKS + 1) * CHUNK
CONV_WIDTH = 31
NUM_BUCKETS = 32
MAX_DISTANCE = 128
N_GROUPS = 4
EXPERTS_PER_GROUP = 8
N_EXPERTS = N_GROUPS * EXPERTS_PER_GROUP
MOE_BLOCK = 256
RMS_EPS = 1e-6
LN_EPS = 1e-5
NEG_INF = -1e30

GROUP_ROWS = 32
LANES = 128
SUBLANES = 8
CONV_CHUNK = 256
ROUTER_ROWS = 64
EXPERT_ROW0 = 8
V7X_VMEM_LIMIT = 56 * 1024 * 1024


def _cparams(*sem):
    return pltpu.CompilerParams(dimension_semantics=sem, vmem_limit_bytes=V7X_VMEM_LIMIT)


def _ada_kernel(c_ref, w_ref, b_ref, o_ref):
    c = c_ref[...]
    s = c * jax.nn.sigmoid(c)
    o_ref[...] = jnp.dot(s.astype(BF16), w_ref[...].astype(BF16),
                         preferred_element_type=F32) + b_ref[...]


def _ada(c_all, w_ada, b_ada):
    rows, d = c_all.shape
    ncol = w_ada.shape[1]
    tn = 1024
    return pl.pallas_call(
        _ada_kernel,
        grid=(ncol // tn,),
        in_specs=[pl.BlockSpec((rows, d), lambda j: (0, 0)),
                  pl.BlockSpec((d, tn), lambda j: (0, j)),
                  pl.BlockSpec((1, tn), lambda j: (0, j))],
        out_specs=pl.BlockSpec((rows, tn), lambda j: (0, j)),
        out_shape=jax.ShapeDtypeStruct((rows, ncol), F32),
        compiler_params=_cparams("arbitrary"),
        name="ada",
    )(c_all, w_ada, b_ada.reshape(1, ncol))


def _rms_mod(x, gain, scale, shift):
    ms = jnp.mean(x * x, axis=-1, keepdims=True)
    y = x * lax.rsqrt(ms + RMS_EPS) * gain
    return y * (1.0 + scale) + shift


def _two_stream_specs(tm, width, prompt_tiles):
    return [pl.BlockSpec((tm, width), lambda i: (jnp.minimum(i, prompt_tiles - 1), 0)),
            pl.BlockSpec((tm, width), lambda i: (jnp.maximum(i - prompt_tiles, 0), 0))]


def _norm1_kernel(xp_ref, xs_ref, g_ref, sc_ref, sh_ref, h_ref, *, groups, prompt_tiles):
    def run(x_ref):
        def body(gi, carry):
            r = pl.multiple_of(gi * GROUP_ROWS, GROUP_ROWS)
            h = _rms_mod(x_ref[pl.ds(r, GROUP_ROWS), :], g_ref[...],
                         sc_ref[pl.ds(gi, 1), :], sh_ref[pl.ds(gi, 1), :])
            h_ref[pl.ds(r, GROUP_ROWS), :] = h.astype(h_ref.dtype)
            return carry

        lax.fori_loop(0, groups, body, 0)

    i = pl.program_id(0)
    pl.when(i < prompt_tiles)(lambda: run(xp_ref))
    pl.when(i >= prompt_tiles)(lambda: run(xs_ref))


def _norm1(x_p, x_s, gain, scale_g, shift_g, tm):
    d = x_p.shape[1]
    n = x_p.shape[0] + x_s.shape[0]
    groups = tm // GROUP_ROWS
    prompt_tiles = x_p.shape[0] // tm
    return pl.pallas_call(
        functools.partial(_norm1_kernel, groups=groups, prompt_tiles=prompt_tiles),
        grid=(n // tm,),
        in_specs=_two_stream_specs(tm, d, prompt_tiles) + [
            pl.BlockSpec((1, d), lambda i: (0, 0)),
            pl.BlockSpec((groups, d), lambda i: (i, 0)),
            pl.BlockSpec((groups, d), lambda i: (i, 0))],
        out_specs=pl.BlockSpec((tm, d), lambda i: (i, 0)),
        out_shape=jax.ShapeDtypeStruct((n, d), BF16),
        compiler_params=_cparams("arbitrary"),
        name="norm1",
    )(x_p, x_s, gain.reshape(1, d), scale_g, shift_g)


def _proj_kernel(h_ref, w_ref, *o_refs, mode):
    acc = jnp.dot(h_ref[...], w_ref[...], preferred_element_type=F32)
    if mode == "qkv":
        o_refs[0][...] = acc.astype(BF16)
    elif mode == "glu":
        half = acc.shape[1] // 2
        o_refs[0][...] = acc[:, :half] * jax.nn.sigmoid(acc[:, half:])
    else:
        o_refs[0][...] = jax.nn.sigmoid(acc).astype(BF16)


def _proj(h, w, mode, tm, tnw, out_dtypes):
    n, d = h.shape
    ncol = w.shape[1]
    tno = tnw // 2 if mode == "glu" else tnw
    nout = ncol // 2 if mode == "glu" else ncol
    outs = pl.pallas_call(
        functools.partial(_proj_kernel, mode=mode),
        grid=(n // tm, ncol // tnw),
        in_specs=[pl.BlockSpec((tm, d), lambda i, j: (i, 0)),
                  pl.BlockSpec((d, tnw), lambda i, j: (0, j))],
        out_specs=[pl.BlockSpec((tm, tno), lambda i, j: (i, j)) for _ in out_dtypes],
        out_shape=[jax.ShapeDtypeStruct((n, nout), dt) for dt in out_dtypes],
        compiler_params=_cparams("arbitrary", "arbitrary"),
        name="proj_" + mode,
    )(h, w)
    return outs


def _kv_state_kernel(h_ref, w_ref, o_ref):
    o_ref[...] = jnp.dot(h_ref[...], w_ref[...], preferred_element_type=F32)


def _kv_state(h, w_kv, n_prompt, seq):
    n, d = h.shape
    batch = n_prompt // seq
    per_seq = seq // WINDOW
    steps = batch + (n - n_prompt) // WINDOW

    def row_block(i):
        return jnp.where(i < batch, (i + 1) * per_seq - 1, n_prompt // WINDOW + i - batch)

    return pl.pallas_call(
        _kv_state_kernel,
        grid=(steps,),
        in_specs=[pl.BlockSpec((WINDOW, d), lambda i: (row_block(i), 0)),
                  pl.BlockSpec((d, 2 * KV_DIM), lambda i: (0, 0))],
        out_specs=pl.BlockSpec((WINDOW, 2 * KV_DIM), lambda i: (i, 0)),
        out_shape=jax.ShapeDtypeStruct((steps * WINDOW, 2 * KV_DIM), F32),
        compiler_params=_cparams("arbitrary"),
        name="kv_state",
    )(h, w_kv)


def _t5_bucket(rel):
    nb = NUM_BUCKETS // 2
    n = -rel
    ret = jnp.where(n < 0, nb, 0)
    n = jnp.abs(n)
    max_exact = nb // 2
    nf = jnp.maximum(n, 1).astype(F32)
    large = max_exact + (jnp.log(nf / max_exact) / math.log(MAX_DISTANCE / max_exact)
                         * (nb - max_exact)).astype(jnp.int32)
    large = jnp.minimum(large, nb - 1)
    return ret + jnp.where(n < max_exact, n, large)


def _bias_kernel(tbl_ref, bkt_ref, o_ref):
    bkt = bkt_ref[...]
    nq = bkt.shape[0]
    for k in range(N_KV_HEADS):
        for g in range(Q_PER_KV):
            acc = jnp.zeros(bkt.shape, F32)
            for b in range(NUM_BUCKETS):
                acc = jnp.where(bkt == b, tbl_ref[b, k * Q_PER_KV + g], acc)
            o_ref[k, g * nq:(g + 1) * nq, :] = acc


def _rel_bias(rel_table, rel):
    nq, nk = rel.shape
    return pl.pallas_call(
        _bias_kernel,
        in_specs=[pl.BlockSpec(memory_space=pltpu.SMEM),
                  pl.BlockSpec((nq, nk), lambda: (0, 0))],
        out_specs=pl.BlockSpec((N_KV_HEADS, Q_PER_KV * nq, nk), lambda: (0, 0, 0)),
        out_shape=jax.ShapeDtypeStruct((N_KV_HEADS, Q_PER_KV * nq, nk), F32),
        name="rel_bias",
    )(rel_table.astype(F32), _t5_bucket(rel))


PAIR_BAND = 2 * CHUNK + WINDOW
HEADS_PER_COL = LANES // HEAD_DIM
COLS_PER_KV = Q_PER_KV // HEADS_PER_COL


def _pair_bias_kernel(tbl_ref, bkt_ref, o_ref):
    for e in range(2):
        bkt = bkt_ref[e]
        for k in range(N_KV_HEADS):
            for col in range(COLS_PER_KV):
                for half in range(HEADS_PER_COL):
                    head = k * Q_PER_KV + col * HEADS_PER_COL + half
                    acc = jnp.full(bkt.shape, NEG_INF, F32)
                    for b in range(NUM_BUCKETS):
                        acc = jnp.where(bkt == b, tbl_ref[b, head], acc)
                    r0 = (e * COLS_PER_KV + col) * CHUNK
                    o_ref[k, r0:r0 + CHUNK, half * PAIR_BAND:(half + 1) * PAIR_BAND] = acc


def _pair_bias(rel_table):
    qi = jnp.arange(CHUNK, dtype=jnp.int32)[None, :, None]
    kj = jnp.arange(PAIR_BAND, dtype=jnp.int32)[None, None, :]
    e = jnp.arange(2, dtype=jnp.int32)[:, None, None]
    rel = (kj - WINDOW) - (e * CHUNK + qi)
    key_chunk = kj // CHUNK - e
    seen = (key_chunk >= 0) & (key_chunk <= WIN_CHUNKS)
    bkt = jnp.where(seen, _t5_bucket(rel), -1)
    rows = 2 * COLS_PER_KV * CHUNK
    return pl.pallas_call(
        _pair_bias_kernel,
        in_specs=[pl.BlockSpec(memory_space=pltpu.SMEM),
                  pl.BlockSpec((2, CHUNK, PAIR_BAND), lambda: (0, 0, 0))],
        out_specs=pl.BlockSpec((N_KV_HEADS, rows, HEADS_PER_COL * PAIR_BAND), lambda: (0, 0, 0)),
        out_shape=jax.ShapeDtypeStruct((N_KV_HEADS, rows, HEADS_PER_COL * PAIR_BAND), F32),
        name="pair_bias",
    )(rel_table.astype(F32), bkt)


def _attend(qc, kk_all, vv_all, bias_ref, sink_ref, mask_thr):
    nq = qc.shape[0]
    nk = kk_all.shape[0]
    pieces = []
    for k in range(N_KV_HEADS):
        qs = jnp.concatenate(
            [qc[:, (k * Q_PER_KV + g) * HEAD_DIM:(k * Q_PER_KV + g + 1) * HEAD_DIM]
             for g in range(Q_PER_KV)], axis=0)
        kk = kk_all[:, k * HEAD_DIM:(k + 1) * HEAD_DIM]
        vv = vv_all[:, k * HEAD_DIM:(k + 1) * HEAD_DIM]
        lg = lax.dot_general(qs, kk, (((1,), (1,)), ((), ())),
                             preferred_element_type=F32) * (HEAD_DIM ** -0.5) + bias_ref[k]
        if mask_thr is not None:
            col = lax.broadcasted_iota(jnp.int32, (Q_PER_KV * nq, nk), 1)
            lg = jnp.where(col < mask_thr, NEG_INF, lg)
        s = sink_ref[k]
        m = jnp.maximum(jnp.max(lg, axis=-1, keepdims=True), s)
        p = jnp.exp(lg - m)
        den = jnp.sum(p, axis=-1, keepdims=True) + jnp.exp(s - m)
        o = jnp.dot(p.astype(BF16), vv, preferred_element_type=F32) / den
        pieces.extend(o[g * nq:(g + 1) * nq, :] for g in range(Q_PER_KV))
    return jnp.concatenate(pieces, axis=1)


def _attn_prompt_kernel(q_ref, kv_ref, halo_ref, bias_ref, sink_ref, o_ref, ka, kb, va, vb, *, pairs):
    t = pl.program_id(1)
    kvcat = jnp.concatenate([halo_ref[...], kv_ref[...]], axis=0)
    low = lax.broadcasted_iota(jnp.int32, (kvcat.shape[0], LANES), 1) < HEAD_DIM
    zero = jnp.zeros((kvcat.shape[0], LANES), BF16)
    ones_low = jnp.where(low, 1.0, 0.0).astype(BF16)
    ones_high = jnp.where(low, 0.0, 1.0).astype(BF16)
    kv_cols = KV_DIM // LANES
    for col in range(2 * kv_cols):
        x = kvcat[:, col * LANES:(col + 1) * LANES]
        xs = jnp.concatenate([x[:, HEAD_DIM:], x[:, :HEAD_DIM]], axis=1)
        dst_a, dst_b = (ka, kb) if col < kv_cols else (va, vb)
        k0 = (col % kv_cols) * HEADS_PER_COL
        dst_a[k0, :, 0:LANES] = jnp.where(low, x, zero)
        dst_b[k0, :, 0:LANES] = jnp.where(low, zero, xs)
        dst_a[k0 + 1, :, 0:LANES] = jnp.where(low, xs, zero)
        dst_b[k0 + 1, :, 0:LANES] = jnp.where(low, zero, x)
    for k in range(N_KV_HEADS):
        va[k, :, LANES:2 * LANES] = ones_low
        vb[k, :, LANES:2 * LANES] = ones_high

    out_low = lax.broadcasted_iota(jnp.int32, (2 * COLS_PER_KV * CHUNK, LANES), 1) < HEAD_DIM
    for p in range(pairs):
        r0 = p * 2 * CHUNK
        for k in range(N_KV_HEADS):
            lhs = jnp.concatenate(
                [q_ref[r0 + e * CHUNK:r0 + (e + 1) * CHUNK, (k * COLS_PER_KV + c) * LANES:(k * COLS_PER_KV + c + 1) * LANES]
                 for e in range(2) for c in range(COLS_PER_KV)], axis=0)
            keys = jnp.concatenate([ka[k, r0:r0 + PAIR_BAND, :], kb[k, r0:r0 + PAIR_BAND, :]], axis=0)
            lg = lax.dot_general(lhs, keys, (((1,), (1,)), ((), ())),
                                 preferred_element_type=F32) * (HEAD_DIM ** -0.5) + bias_ref[k]
            probs, sink_terms = [], []
            for half in range(HEADS_PER_COL):
                seg = lg[:, half * PAIR_BAND:(half + 1) * PAIR_BAND]
                if p == 0:
                    kcol = lax.broadcasted_iota(jnp.int32, seg.shape, 1)
                    seg = jnp.where(kcol < jnp.where(t == 0, WINDOW, 0), NEG_INF, seg)
                s = sink_ref[k, half]
                folded = seg[:, :LANES]
                for j in range(1, PAIR_BAND // LANES):
                    folded = jnp.maximum(folded, seg[:, j * LANES:(j + 1) * LANES])
                m = jnp.maximum(jnp.max(folded, axis=-1, keepdims=True), s)
                pr = jnp.exp(seg - jnp.concatenate([m] * (PAIR_BAND // LANES), axis=1))
                sink_terms.append(jnp.exp(s - m))
                probs.append(pr.astype(BF16))
            vals = jnp.concatenate([va[k, r0:r0 + PAIR_BAND, :], vb[k, r0:r0 + PAIR_BAND, :]], axis=0)
            oe = jnp.dot(jnp.concatenate(probs, axis=1), vals, preferred_element_type=F32)
            o = oe[:, :LANES] / (oe[:, LANES:] + jnp.where(out_low, sink_terms[0], sink_terms[1]))
            for e in range(2):
                for c in range(COLS_PER_KV):
                    rr = (e * COLS_PER_KV + c) * CHUNK
                    o_ref[r0 + e * CHUNK:r0 + (e + 1) * CHUNK,
                          (k * COLS_PER_KV + c) * LANES:(k * COLS_PER_KV + c + 1) * LANES] = (
                        o[rr:rr + CHUNK, :].astype(o_ref.dtype))


def _attn_prompt(q, kv, bias, sink_rows, n_prompt, seq, pairs):
    rows = pairs * 2 * CHUNK
    tiles = seq // rows
    halo_per_tile = rows // WINDOW
    keys = WINDOW + rows
    return pl.pallas_call(
        functools.partial(_attn_prompt_kernel, pairs=pairs),
        grid=(n_prompt // seq, tiles),
        scratch_shapes=[pltpu.VMEM((N_KV_HEADS, keys, LANES), BF16) for _ in range(2)]
        + [pltpu.VMEM((N_KV_HEADS, keys, 2 * LANES), BF16) for _ in range(2)],
        in_specs=[pl.BlockSpec((rows, ATTN_DIM), lambda b, t: (b * tiles + t, 0)),
                  pl.BlockSpec((rows, 2 * KV_DIM), lambda b, t: (b * tiles + t, 0)),
                  pl.BlockSpec((WINDOW, 2 * KV_DIM),
                               lambda b, t: (jnp.maximum((b * tiles + t) * halo_per_tile - 1, 0), 0)),
                  pl.BlockSpec(bias.shape, lambda b, t: (0, 0, 0)),
                  pl.BlockSpec(sink_rows.shape, lambda b, t: (0, 0, 0, 0))],
        out_specs=pl.BlockSpec((rows, ATTN_DIM), lambda b, t: (b * tiles + t, 0)),
        out_shape=jax.ShapeDtypeStruct((n_prompt, ATTN_DIM), BF16),
        compiler_params=_cparams("arbitrary", "arbitrary"),
        name="attn_prompt",
    )(q, kv, kv, bias, sink_rows)


def _attn_sample_kernel(q_ref, kv_ref, ck_ref, cv_ref, bias_ref, sink_ref, o_ref):
    kv = kv_ref[...]
    kk = jnp.concatenate([ck_ref[0].astype(BF16), kv[:, :KV_DIM]], axis=0)
    vv = jnp.concatenate([cv_ref[0].astype(BF16), kv[:, KV_DIM:]], axis=0)
    o = _attend(q_ref[...], kk, vv, bias_ref, sink_ref, None)
    o_ref[...] = o.astype(o_ref.dtype)


def _attn_sample(q, kv, cache_k, cache_v, bias, sink_rows, n_prompt, t_new):
    dec_batch, buf, _ = cache_k.shape
    first = n_prompt // t_new
    return pl.pallas_call(
        _attn_sample_kernel,
        grid=(dec_batch,),
        in_specs=[pl.BlockSpec((t_new, ATTN_DIM), lambda s: (first + s, 0)),
                  pl.BlockSpec((t_new, 2 * KV_DIM), lambda s: (first + s, 0)),
                  pl.BlockSpec((1, buf, KV_DIM), lambda s: (s, 0, 0)),
                  pl.BlockSpec((1, buf, KV_DIM), lambda s: (s, 0, 0)),
                  pl.BlockSpec(bias.shape, lambda s: (0, 0, 0)),
                  pl.BlockSpec(sink_rows.shape, lambda s: (0, 0, 0))],
        out_specs=pl.BlockSpec((t_new, ATTN_DIM), lambda s: (s, 0)),
        out_shape=jax.ShapeDtypeStruct((dec_batch * t_new, ATTN_DIM), BF16),
        compiler_params=_cparams("arbitrary"),
        name="attn_sample",
    )(q, kv, cache_k, cache_v, bias, sink_rows)


def _route(lt, eidx_ref, wtok_ref, rank_ref, cnt_ref, cnt_scr):
    gl = [lt[r:r + 1, :] for r in range(N_GROUPS)]
    gmax = gl[0]
    gsel = jnp.zeros(gl[0].shape, jnp.int32)
    for r in range(1, N_GROUPS):
        better = gl[r] > gmax
        gsel = jnp.where(better, r, gsel)
        gmax = jnp.maximum(gmax, gl[r])
    gexp = [jnp.exp(v - gmax) for v in gl]
    gsum = gexp[0]
    for r in range(1, N_GROUPS):
        gsum = gsum + gexp[r]
    psel = jnp.zeros(gl[0].shape, F32)
    for r in range(N_GROUPS):
        psel = jnp.where(gsel == r, gexp[r] / gsum, psel)
    el = jnp.zeros((EXPERTS_PER_GROUP, lt.shape[1]), F32)
    for r in range(N_GROUPS):
        lo = EXPERT_ROW0 + r * EXPERTS_PER_GROUP
        el = jnp.where(gsel == r, lt[lo:lo + EXPERTS_PER_GROUP, :], el)
    emax = jnp.max(el, axis=0, keepdims=True)
    ee = jnp.exp(el - emax)
    pin = ee / jnp.sum(ee, axis=0, keepdims=True)
    idx = lax.broadcasted_iota(jnp.int32, pin.shape, 0)
    p1 = jnp.max(pin, axis=0, keepdims=True)
    i1 = jnp.min(jnp.where(pin == p1, idx, EXPERTS_PER_GROUP), axis=0, keepdims=True)
    rest = jnp.where(idx == i1, -1.0, pin)
    p2 = jnp.max(rest, axis=0, keepdims=True)
    i2 = jnp.min(jnp.where(rest == p2, idx, EXPERTS_PER_GROUP), axis=0, keepdims=True)
    tot = p1 + p2
    e1 = gsel * EXPERTS_PER_GROUP + i1
    e2 = gsel * EXPERTS_PER_GROUP + i2
    eidx_ref[0:1, :] = e1
    eidx_ref[1:2, :] = e2
    wtok_ref[0:1, :] = psel * p1 / tot
    wtok_ref[1:2, :] = psel * p2 / tot

    t = lt.shape[1]
    eiota = lax.broadcasted_iota(jnp.int32, (N_EXPERTS, t), 0)
    oh1 = (eiota == e1).astype(F32)
    oh2 = (eiota == e2).astype(F32)
    both = oh1 + oh2
    before = (lax.broadcasted_iota(jnp.int32, (t, t), 0)
              < lax.broadcasted_iota(jnp.int32, (t, t), 1)).astype(BF16)
    prior = jnp.dot(both.astype(BF16), before, preferred_element_type=F32) + cnt_scr[:, 0:1]
    rank_ref[0:1, :] = jnp.sum(oh1 * prior, axis=0, keepdims=True).astype(jnp.int32)
    rank_ref[1:2, :] = jnp.sum(oh2 * prior, axis=0, keepdims=True).astype(jnp.int32)
    cnt_scr[...] = cnt_scr[...] + jnp.sum(both, axis=1, keepdims=True)
    cnt_ref[...] = cnt_scr[...]


def _post_kernel(xp_ref, xs_ref, u_ref, uhalo_ref, sconv_ref, op_ref, os_ref, gates_ref,
                 gate1_ref, shift2_ref, scale2_ref,
                 cw_ref, cb_ref, lng_ref, lnb_ref, wpw2_ref, bpw2_ref, wao_ref, wout_ref,
                 n2_ref, wrt_ref, brt_ref,
                 x1_ref, h2_ref, eidx_ref, wtok_ref, rank_ref, cnt_ref,
                 uext, shift_scr, d_scr, s_scr, o_scr, mix_scr, h2b_scr, cnt_scr,
                 *, groups, prompt_tiles, tiles_per_seq):
    i = pl.program_id(0)
    d_model = xp_ref.shape[1]

    @pl.when(i == 0)
    def _():
        cnt_scr[...] = jnp.zeros(cnt_scr.shape, F32)
    nch = cw_ref.shape[0]
    cw = uext.shape[2]
    hist = uext.shape[1] - GROUP_ROWS
    lead = hist - (CONV_WIDTH - 1)

    def put_hist(g, rows):
        for c in range(nch):
            uext[g * nch + c, 0:hist, :] = rows[:, c * cw:(c + 1) * cw]

    @pl.when(i < prompt_tiles)
    def _():
        first = (i % tiles_per_seq) == 0
        put_hist(0, jnp.where(first, 0.0, uhalo_ref[...]))
        for g in range(1, groups):
            put_hist(g, u_ref[(g - 1) * GROUP_ROWS:g * GROUP_ROWS, :])
        o_scr[...] = op_ref[...]

    @pl.when(i >= prompt_tiles)
    def _():
        for g in range(groups):
            put_hist(g, sconv_ref[g])
        o_scr[...] = os_ref[...]

    for g in range(groups):
        for c in range(nch):
            uext[g * nch + c, hist:hist + GROUP_ROWS, :] = (
                u_ref[g * GROUP_ROWS:(g + 1) * GROUP_ROWS, c * cw:(c + 1) * cw])

    span = shift_scr.shape[1]

    def conv_chunk(k, carry):
        c = k % nch
        win_all = uext[k]
        for s in range(1, SUBLANES):
            shift_scr[s, :, :] = pltpu.roll(win_all, win_all.shape[0] - s, 0)[0:span, :]
        acc = None
        for j in range(CONV_WIDTH):
            base, s = divmod(lead + j, SUBLANES)
            rows = pl.ds(base * SUBLANES, GROUP_ROWS)
            win = uext[k, rows, :] if s == 0 else shift_scr[s, rows, :]
            term = win * cw_ref[c, j:j + 1, :]
            acc = term if acc is None else acc + term
        d_scr[k] = acc + cb_ref[c]
        return carry

    lax.fori_loop(0, groups * nch, conv_chunk, 0)

    for g in range(groups):
        dd = jnp.concatenate([d_scr[g * nch + c] for c in range(nch)], axis=1)
        mu = jnp.mean(dd, axis=-1, keepdims=True)
        var = jnp.mean(jnp.square(dd - mu), axis=-1, keepdims=True)
        y = (dd - mu) * lax.rsqrt(var + LN_EPS) * lng_ref[...] + lnb_ref[...]
        s_scr[g * GROUP_ROWS:(g + 1) * GROUP_ROWS, :] = (y * jax.nn.sigmoid(y)).astype(BF16)

    conv_out = jnp.dot(s_scr[...], wpw2_ref[...], preferred_element_type=F32) + bpw2_ref[...]
    attn_out = jnp.dot(o_scr[...], wao_ref[...], preferred_element_type=F32)
    merged = (gates_ref[:, :d_model].astype(F32) * attn_out
              + gates_ref[:, d_model:].astype(F32) * conv_out)
    mix_scr[...] = jnp.dot(merged.astype(BF16), wout_ref[...], preferred_element_type=F32)

    sub = d_model // LANES

    def residual(x_ref):
        def res_group(g, carry):
            r = pl.multiple_of(g * GROUP_ROWS, GROUP_ROWS)
            x1 = (x_ref[pl.ds(r, GROUP_ROWS), :]
                  + gate1_ref[pl.ds(g, 1), :] * mix_scr[pl.ds(r, GROUP_ROWS), :])
            x1_ref[pl.ds(r, GROUP_ROWS), :] = x1
            h = _rms_mod(x1, n2_ref[...], scale2_ref[pl.ds(g, 1), :], shift2_ref[pl.ds(g, 1), :])
            h2b_scr[pl.ds(r, GROUP_ROWS), :] = h.astype(BF16)
            for s in range(sub):
                h2_ref[pl.ds(r * sub + s, GROUP_ROWS, stride=sub), :] = h[:, s * LANES:(s + 1) * LANES]
            return carry

        lax.fori_loop(0, groups, res_group, 0)

    pl.when(i < prompt_tiles)(lambda: residual(xp_ref))
    pl.when(i >= prompt_tiles)(lambda: residual(xs_ref))

    lt = lax.dot_general(wrt_ref[...], h2b_scr[...], (((1,), (1,)), ((), ())),
                         preferred_element_type=F32) + brt_ref[...]
    _route(lt, eidx_ref, wtok_ref, rank_ref, cnt_ref, cnt_scr)


def _post(x_p, x_s, u, sconv_pad, o_p, o_s, gates, gate1_g, shift2_g, scale2_g, p, n_prompt, seq, tm):
    d = x_p.shape[1]
    n = x_p.shape[0] + x_s.shape[0]
    cdim = u.shape[1]
    nch = cdim // CONV_CHUNK
    groups = tm // GROUP_ROWS
    prompt_tiles = n_prompt // tm
    const = lambda shape: pl.BlockSpec(shape, lambda i: (0,) * len(shape))
    row = lambda w: pl.BlockSpec((tm, w), lambda i: (i, 0))
    grp = pl.BlockSpec((groups, d), lambda i: (i, 0))
    return pl.pallas_call(
        functools.partial(_post_kernel, groups=groups, prompt_tiles=prompt_tiles,
                          tiles_per_seq=seq // tm),
        grid=(n // tm,),
        in_specs=_two_stream_specs(tm, d, prompt_tiles) + [
                  row(cdim),
                  pl.BlockSpec((GROUP_ROWS, cdim), lambda i: (jnp.maximum(i * groups - 1, 0), 0)),
                  pl.BlockSpec((groups, GROUP_ROWS, cdim),
                               lambda i: (jnp.maximum(i - prompt_tiles, 0), 0, 0))]
                 + _two_stream_specs(tm, ATTN_DIM, prompt_tiles) + [
                  row(2 * d), grp, grp, grp,
                  const((nch, CONV_WIDTH, CONV_CHUNK)), const((nch, 1, CONV_CHUNK)),
                  const((1, cdim)), const((1, cdim)),
                  const((cdim, d)), const((1, d)), const((ATTN_DIM, d)), const((d, d)),
                  const((1, d)), const((ROUTER_ROWS, d)), const((ROUTER_ROWS, 1))],
        out_specs=[row(d),
                   pl.BlockSpec((tm * (d // LANES), LANES), lambda i: (i, 0)),
                   pl.BlockSpec((2, tm), lambda i: (0, i)),
                   pl.BlockSpec((2, tm), lambda i: (0, i)),
                   pl.BlockSpec((2, tm), lambda i: (0, i)),
                   pl.BlockSpec((N_EXPERTS, LANES), lambda i: (0, 0))],
        out_shape=[jax.ShapeDtypeStruct((n, d), F32),
                   jax.ShapeDtypeStruct((n * (d // LANES), LANES), F32),
                   jax.ShapeDtypeStruct((2, n), jnp.int32),
                   jax.ShapeDtypeStruct((2, n), F32),
                   jax.ShapeDtypeStruct((2, n), jnp.int32),
                   jax.ShapeDtypeStruct((N_EXPERTS, LANES), F32)],
        scratch_shapes=[pltpu.VMEM((groups * nch, 2 * GROUP_ROWS, CONV_CHUNK), F32),
                        pltpu.VMEM((SUBLANES, 2 * GROUP_ROWS - SUBLANES, CONV_CHUNK), F32),
                        pltpu.VMEM((groups * nch, GROUP_ROWS, CONV_CHUNK), F32),
                        pltpu.VMEM((tm, cdim), BF16),
                        pltpu.VMEM((tm, ATTN_DIM), BF16),
                        pltpu.VMEM((tm, d), F32),
                        pltpu.VMEM((tm, d), BF16),
                        pltpu.VMEM((N_EXPERTS, LANES), F32)],
        compiler_params=_cparams("arbitrary"),
        name="post",
    )(x_p, x_s, u, u, sconv_pad, o_p, o_s, gates, gate1_g, shift2_g, scale2_g,
      p["conv_w"], p["conv_b"], p["ln_g"], p["ln_b"], p["w_pw2"], p["b_pw2"], p["w_ao"], p["w_out"],
      p["n2"], p["w_rt"], p["b_rt"])


TABLE_GROUP = 8


def _grouped_tables(tab, group=TABLE_GROUP):
    steps, w = tab.shape
    assert steps % group == 0
    nxt = jnp.concatenate([tab[group::group], tab[-1:]], axis=0)
    return jnp.concatenate([tab.reshape(steps // group, group * w), nxt], axis=1).reshape(
        steps // group, 1, (group + 1) * w)

def _moe_kernel(blk_e_ref, nused_ref, src_ref, roww_ref,
                h2_hbm, wg_ref, wu_ref, wd_ref, y_ref,
                xbuf, x16, wg_b, wu_b, wd_b, gsem, *, sub):
    i = pl.program_id(0)
    nused = nused_ref[0]
    slot = i % 2
    base = (i % TABLE_GROUP) * MOE_BLOCK
    rows = MOE_BLOCK * sub

    def gather(first, sl):
        for r in range(MOE_BLOCK):
            off = pl.multiple_of(src_ref[0, 0, first + r], sub)
            pltpu.make_async_copy(h2_hbm.at[pl.ds(off, sub), :],
                                  xbuf.at[sl, pl.ds(r * sub, sub), :], gsem.at[sl]).start(priority=r % 2)

    def gather_wait(sl):
        pltpu.make_async_copy(h2_hbm.at[pl.ds(0, rows), :], xbuf.at[sl], gsem.at[sl]).wait()

    @pl.when(i == 0)
    def _():
        gather(base, 0)

    @pl.when(i >= nused)
    def _():
        y_ref[...] = jnp.zeros(y_ref.shape, F32)

    @pl.when(i < nused)
    def _():
        changed = jnp.logical_or(i == 0, blk_e_ref[i] != blk_e_ref[jnp.maximum(i - 1, 0)])

        @pl.when(changed)
        def _():
            wg_b[...] = wg_ref[0].astype(BF16)
            wu_b[...] = wu_ref[0].astype(BF16)
            wd_b[...] = wd_ref[0].astype(BF16)

        gather_wait(slot)
        for s in range(sub):
            x16[:, s * LANES:(s + 1) * LANES] = xbuf[slot, pl.ds(s, MOE_BLOCK, stride=sub), :].astype(BF16)
        gather(base + MOE_BLOCK, 1 - slot)
        x = x16[...]
        hg = jnp.dot(x, wg_b[...], preferred_element_type=F32)
        hu = jnp.dot(x, wu_b[...], preferred_element_type=F32)
        hid = (hg * jax.nn.sigmoid(hg) * hu).astype(BF16)
        y_ref[...] = jnp.dot(hid, wd_b[...], preferred_element_type=F32) * roww_ref[...]

        @pl.when(i == nused - 1)
        def _():
            gather_wait(1 - slot)


def _moe(h2_tok, blk_e, nused, src, roww, w_gate, w_up, w_down):
    n_blocks = blk_e.shape[0]
    n_exp, d, de = w_gate.shape
    lanes = h2_tok.shape[1]
    sub = d // lanes
    src_tables = _grouped_tables(src.reshape(n_blocks, MOE_BLOCK) * sub)
    grid_spec = pltpu.PrefetchScalarGridSpec(
        num_scalar_prefetch=2,
        grid=(n_blocks,),
        in_specs=[pl.BlockSpec((1, 1, src_tables.shape[2]), lambda i, be, nu: (i // TABLE_GROUP, 0, 0),
                               memory_space=pltpu.SMEM),
                  pl.BlockSpec((MOE_BLOCK, 1), lambda i, be, nu: (i, 0)),
                  pl.BlockSpec(memory_space=pl.ANY),
                  pl.BlockSpec((1, d, de), lambda i, be, nu: (be[i], 0, 0)),
                  pl.BlockSpec((1, d, de), lambda i, be, nu: (be[i], 0, 0)),
                  pl.BlockSpec((1, de, d), lambda i, be, nu: (be[i], 0, 0))],
        out_specs=pl.BlockSpec((MOE_BLOCK, d), lambda i, be, nu: (i, 0)),
        scratch_shapes=[pltpu.VMEM((2, MOE_BLOCK * sub, lanes), F32),
                        pltpu.VMEM((MOE_BLOCK, d), BF16),
                        pltpu.VMEM((d, de), BF16), pltpu.VMEM((d, de), BF16), pltpu.VMEM((de, d), BF16),
                        pltpu.SemaphoreType.DMA((2,))],
    )
    return pl.pallas_call(
        functools.partial(_moe_kernel, sub=sub),
        grid_spec=grid_spec,
        out_shape=jax.ShapeDtypeStruct((n_blocks * MOE_BLOCK, d), F32),
        compiler_params=_cparams("arbitrary"),
        name="moe",
    )(blk_e, nused, src_tables, roww, h2_tok, w_gate, w_up, w_down)


def _dispatch(eidx, wtok, rank, counts, n):
    a_tot = 2 * n
    experts = jnp.arange(N_EXPERTS, dtype=jnp.int32)
    padded = (counts + MOE_BLOCK - 1) // MOE_BLOCK * MOE_BLOCK
    pad_end = jnp.sum(jnp.where(experts[None, :] <= experts[:, None], padded[None, :], 0), axis=1)
    pad_start = pad_end - padded
    dest = jnp.sum(jnp.where(eidx[:, :, None] == experts, pad_start, 0), axis=-1) + rank
    n_blocks = -(-(a_tot + N_EXPERTS * (MOE_BLOCK - 1)) // MOE_BLOCK)
    n_blocks = -(-n_blocks // TABLE_GROUP) * TABLE_GROUP
    n_rows = n_blocks * MOE_BLOCK
    tok = jnp.tile(jnp.arange(n, dtype=jnp.int32), 2)
    upd = jnp.stack([tok, lax.bitcast_convert_type(wtok.reshape(-1), jnp.int32)], axis=1)
    rows = jnp.zeros((n_rows, 2), jnp.int32).at[dest.reshape(-1)].set(upd, unique_indices=True)
    blk_start = jnp.arange(n_blocks, dtype=jnp.int32) * MOE_BLOCK
    blk_e = jnp.minimum(jnp.sum((pad_end[None, :] <= blk_start[:, None]).astype(jnp.int32), axis=1),
                        N_EXPERTS - 1)
    nused = (pad_end[-1] // MOE_BLOCK).reshape(1)
    roww = lax.bitcast_convert_type(rows[:, 1], F32)
    return blk_e, nused, rows[:, 0].reshape(n_blocks, 1, MOE_BLOCK), roww.reshape(n_rows, 1), dest


def _final_kernel(d0_ref, d1_ref, x1_ref, gate2_ref, fg_ref, yb_hbm,
                  op_ref, os_ref, ybuf, sem, *, groups, prompt_tiles, tiles, table_group):
    i = pl.program_id(0)
    slot = i % 2
    tm = x1_ref.shape[0]
    base = (i % table_group) * tm

    def gather(first, sl):
        for r in range(tm):
            pltpu.make_async_copy(yb_hbm.at[pl.ds(d0_ref[0, 0, first + r], 1), :],
                                  ybuf.at[sl, 0, pl.ds(r, 1), :], sem.at[sl]).start(priority=0)
            pltpu.make_async_copy(yb_hbm.at[pl.ds(d1_ref[0, 0, first + r], 1), :],
                                  ybuf.at[sl, 1, pl.ds(r, 1), :], sem.at[sl]).start(priority=1)

    def gather_wait(sl):
        for k in range(2):
            pltpu.make_async_copy(yb_hbm.at[pl.ds(0, tm), :], ybuf.at[sl, k], sem.at[sl]).wait()

    @pl.when(i == 0)
    def _():
        gather(base, 0)

    gather_wait(slot)
    gather(base + tm, 1 - slot)

    def run(o_ref):
        def body(g, carry):
            r = pl.multiple_of(g * GROUP_ROWS, GROUP_ROWS)
            moe = ybuf[slot, 0, pl.ds(r, GROUP_ROWS), :] + ybuf[slot, 1, pl.ds(r, GROUP_ROWS), :]
            x2 = x1_ref[pl.ds(r, GROUP_ROWS), :] + gate2_ref[pl.ds(g, 1), :] * moe
            ms = jnp.mean(x2 * x2, axis=-1, keepdims=True)
            o_ref[pl.ds(r, GROUP_ROWS), :] = x2 * lax.rsqrt(ms + RMS_EPS) * fg_ref[...]
            return carry

        lax.fori_loop(0, groups, body, 0)

    pl.when(i < prompt_tiles)(lambda: run(op_ref))
    pl.when(i >= prompt_tiles)(lambda: run(os_ref))

    @pl.when(i == tiles - 1)
    def _():
        gather_wait(1 - slot)


def _final(x1, yb, dest, gate2_g, final_g, n_prompt, tm):
    n, d = x1.shape
    groups = tm // GROUP_ROWS
    tiles = n // tm
    prompt_tiles = n_prompt // tm
    table_group = max(g for g in range(1, TABLE_GROUP + 1) if tiles % g == 0)
    tables = [_grouped_tables(dest[k].reshape(tiles, tm), table_group) for k in range(2)]
    idx_spec = pl.BlockSpec((1, 1, tables[0].shape[2]), lambda i: (i // table_group, 0, 0),
                            memory_space=pltpu.SMEM)
    return pl.pallas_call(
        functools.partial(_final_kernel, groups=groups, prompt_tiles=prompt_tiles, tiles=tiles,
                          table_group=table_group),
        grid=(tiles,),
        in_specs=[idx_spec, idx_spec,
                  pl.BlockSpec((tm, d), lambda i: (i, 0)),
                  pl.BlockSpec((groups, d), lambda i: (i, 0)),
                  pl.BlockSpec((1, d), lambda i: (0, 0)),
                  pl.BlockSpec(memory_space=pl.ANY)],
        out_specs=_two_stream_specs(tm, d, prompt_tiles),
        out_shape=[jax.ShapeDtypeStruct((n_prompt, d), F32),
                   jax.ShapeDtypeStruct((n - n_prompt, d), F32)],
        scratch_shapes=[pltpu.VMEM((2, 2, tm, d), F32), pltpu.SemaphoreType.DMA((2,))],
        compiler_params=_cparams("arbitrary"),
        name="final",
    )(tables[0], tables[1], x1, gate2_g, final_g.reshape(1, d), yb)


def _layer(x_p, x_s, c_all, cache_k, cache_v, sconv, rel_table, lp, seq, t_new):
    (n1, n2, w_ada, b_ada, w_in, sink, w_ao, conv_w, conv_b, ln_g, ln_b, w_pw2, b_pw2, w_out,
     w_grp, b_grp, w_rt, b_rt, w_gate, w_up, w_down) = lp
    n_prompt, d = x_p.shape
    n = n_prompt + x_s.shape[0]
    batch = n_prompt // seq
    dec_batch = (n - n_prompt) // t_new
    cdim = conv_w.shape[1]
    tm = 256

    c_rows = -(-c_all.shape[0] // 8) * 8
    c_pad = jnp.pad(c_all, ((0, c_rows - c_all.shape[0]), (0, 0)))
    mods = _ada(c_pad, w_ada, b_ada)

    def per_group(k):
        m = mods[:, k * d:(k + 1) * d]
        mp = jnp.broadcast_to(m[:batch, None, :], (batch, seq // GROUP_ROWS, d))
        return jnp.concatenate([mp.reshape(batch * (seq // GROUP_ROWS), d), m[batch:batch + dec_batch]],
                               axis=0)

    shift1, scale1, gate1, shift2, scale2, gate2 = [per_group(k) for k in range(6)]

    h1 = _norm1(x_p, x_s, n1, scale1, shift1, 512)

    k_off = ATTN_DIM
    glu_off = k_off + 2 * KV_DIM
    gate_off = glu_off + 2 * cdim
    w_in_b = w_in.astype(BF16)
    tglu = 256
    w_glu = jnp.stack([w_in_b[:, glu_off:glu_off + cdim].reshape(d, cdim // tglu, tglu),
                       w_in_b[:, glu_off + cdim:gate_off].reshape(d, cdim // tglu, tglu)],
                      axis=2).reshape(d, 2 * cdim)
    tmm = 1536
    (q,) = _proj(h1, w_in_b[:, :k_off], "qkv", tmm, 512, [BF16])
    (kv,) = _proj(h1, w_in_b[:, k_off:glu_off], "qkv", tmm, 512, [BF16])
    kv_state = _kv_state(h1, w_in_b[:, k_off:glu_off], n_prompt, seq)
    (u,) = _proj(h1, w_glu, "glu", tmm, 2 * tglu, [F32])
    (gates,) = _proj(h1, w_in_b[:, gate_off:], "gate", tmm, 512, [BF16])

    bias_p = _pair_bias(rel_table)
    buf = cache_k.shape[1]
    kpos = jnp.concatenate([jnp.arange(buf, dtype=jnp.int32) - buf, jnp.arange(t_new, dtype=jnp.int32)])
    bias_s = _rel_bias(rel_table, kpos[None, :] - jnp.arange(t_new, dtype=jnp.int32)[:, None])
    sink_f = sink.astype(F32).reshape(N_KV_HEADS, Q_PER_KV, 1, 1)
    sink_kch = sink.astype(F32).reshape(N_KV_HEADS, 1, COLS_PER_KV, HEADS_PER_COL).transpose(0, 3, 1, 2)
    sink_p = jnp.broadcast_to(sink_kch[..., None, None],
                              (N_KV_HEADS, HEADS_PER_COL, 2, COLS_PER_KV, CHUNK, LANES)).reshape(
        N_KV_HEADS, HEADS_PER_COL, 2 * COLS_PER_KV * CHUNK, LANES)
    sink_s = jnp.broadcast_to(sink_f, (N_KV_HEADS, Q_PER_KV, t_new, 1)).reshape(N_KV_HEADS, Q_PER_KV * t_new, 1)
    o_p = _attn_prompt(q, kv, bias_p, sink_p, n_prompt, seq, 2)
    o_s = _attn_sample(q, kv, cache_k.reshape(dec_batch, buf, KV_DIM), cache_v.reshape(dec_batch, buf, KV_DIM),
                       bias_s, sink_s, n_prompt, t_new)

    hist = GROUP_ROWS
    sconv_pad = jnp.pad(sconv, ((0, 0), (hist - sconv.shape[1], 0), (0, 0)))
    w_rt_t = jnp.zeros((ROUTER_ROWS, d), F32)
    w_rt_t = w_rt_t.at[:N_GROUPS].set(w_grp.T).at[EXPERT_ROW0:EXPERT_ROW0 + N_EXPERTS].set(w_rt.T)
    b_rt_t = jnp.zeros((ROUTER_ROWS, 1), F32)
    b_rt_t = b_rt_t.at[:N_GROUPS, 0].set(b_grp.astype(F32)).at[EXPERT_ROW0:EXPERT_ROW0 + N_EXPERTS, 0].set(
        b_rt.astype(F32))
    nch = cdim // CONV_CHUNK
    conv_w_c = conv_w.reshape(CONV_WIDTH, nch, CONV_CHUNK).transpose(1, 0, 2)
    params = dict(conv_w=conv_w_c, conv_b=conv_b.reshape(nch, 1, CONV_CHUNK), ln_g=ln_g.reshape(1, cdim),
                  ln_b=ln_b.reshape(1, cdim), w_pw2=w_pw2.astype(BF16), b_pw2=b_pw2.reshape(1, d),
                  w_ao=w_ao.astype(BF16), w_out=w_out.astype(BF16), n2=n2.reshape(1, d),
                  w_rt=w_rt_t.astype(BF16), b_rt=b_rt_t)
    x1, h2, eidx, wtok, rank, cnt = _post(x_p, x_s, u, sconv_pad, o_p, o_s, gates, gate1, shift2, scale2,
                                             params, n_prompt, seq, tm)

    blk_e, nused, src, roww, dest = _dispatch(eidx, wtok, rank, cnt[:, 0].astype(jnp.int32), n)
    yb = _moe(h2, blk_e, nused, src, roww, w_gate, w_up, w_down)
    return x1, yb, dest, gate2, kv_state, u


def kernel(x_prompt, x_sample, c_prompt, c_sample, cache_k, cache_v, state_conv, rel_bias_table, norm1_g, norm2_g, w_ada, b_ada, w_in, attn_sink, w_attn_o, conv_w, conv_b, conv_ln_g, conv_ln_b, w_pw2, b_pw2, w_out, w_group, b_group, w_router, b_router, w_gate, w_up, w_down, final_g):
    batch, seq, d = x_prompt.shape
    dec_batch, t_new, _ = x_sample.shape
    depth = norm1_g.shape[0]
    assert depth == 1, "single trunk layer"
    assert t_new == GROUP_ROWS and seq % GROUP_ROWS == 0
    n_prompt = batch * seq
    n = n_prompt + dec_batch * t_new
    x_p = x_prompt.reshape(n_prompt, d)
    x_s = x_sample.reshape(dec_batch * t_new, d)
    c_all = jnp.concatenate([c_prompt, c_sample], axis=0)
    l = 0
    lp = tuple(a.reshape(a.shape[1:]) for a in (
        norm1_g, norm2_g, w_ada, b_ada, w_in, attn_sink, w_attn_o, conv_w, conv_b, conv_ln_g, conv_ln_b,
        w_pw2, b_pw2, w_out, w_group, b_group, w_router, b_router, w_gate, w_up, w_down))
    x1, yb, dest, gate2, kv_state, u = _layer(x_p, x_s, c_all, cache_k[l], cache_v[l], state_conv[l],
                                              rel_bias_table, lp, seq, t_new)
    out_p, out_s = _final(x1, yb, dest, gate2, final_g, n_prompt, 256)

    y_prompt = out_p.reshape(batch, seq, d)
    y_sample = out_s.reshape(dec_batch, t_new, d)
    cdim = u.shape[1]
    kvp = kv_state[:batch * WINDOW].reshape(batch, WINDOW, 2 * KV_DIM)
    new_k_prompt = kvp[..., :KV_DIM].reshape(1, batch, WINDOW, N_KV_HEADS, HEAD_DIM)
    new_v_prompt = kvp[..., KV_DIM:].reshape(1, batch, WINDOW, N_KV_HEADS, HEAD_DIM)
    new_conv_prompt = u[:n_prompt].reshape(batch, seq, cdim)[:, -(CONV_WIDTH - 1):][None]
    kvs = kv_state[batch * WINDOW:].reshape(dec_batch, t_new, 2 * KV_DIM)
    buf = cache_k.shape[2]
    k_new = kvs[..., :KV_DIM].reshape(dec_batch, t_new, N_KV_HEADS, HEAD_DIM)
    v_new = kvs[..., KV_DIM:].reshape(dec_batch, t_new, N_KV_HEADS, HEAD_DIM)
    new_k_sample = jnp.concatenate([cache_k[l], k_new], axis=1)[:, -buf:][None]
    new_v_sample = jnp.concatenate([cache_v[l], v_new], axis=1)[:, -buf:][None]
    us = u[n_prompt:].reshape(dec_batch, t_new, cdim)
    new_conv_sample = jnp.concatenate([state_conv[l], us], axis=1)[:, -(CONV_WIDTH - 1):][None]
    return (y_prompt, y_sample, new_k_prompt, new_v_prompt, new_conv_prompt,
            new_k_sample, new_v_sample, new_conv_sample)
```

```python
import functools
import math

import jax
import jax.numpy as jnp
from jax import lax
from jax.experimental import pallas as pl
from jax.experimental.pallas import tpu as pltpu

F32 = jnp.float32
BF16 = jnp.bfloat16

CHUNK = 64
HEAD_DIM = 64
N_Q_HEADS = 16
N_KV_HEADS = 4
Q_PER_KV = N_Q_HEADS // N_KV_HEADS
ATTN_DIM = N_Q_HEADS * HEAD_DIM
KV_DIM = N_KV_HEADS * HEAD_DIM
WINDOW = 128
WIN_CHUNKS = WINDOW // CHUNK
BAND = (WIN_CHUNKS + 1) * CHUNK
CONV_WIDTH = 31
NUM_BUCKETS = 32
MAX_DISTANCE = 128
N_GROUPS = 4
EXPERTS_PER_GROUP = 8
N_EXPERTS = N_GROUPS * EXPERTS_PER_GROUP
MOE_BLOCK = 256
RMS_EPS = 1e-6
LN_EPS = 1e-5
NEG_INF = -1e30

GROUP_ROWS = 32
LANES = 128
SUBLANES = 8
CONV_CHUNK = 256
ROUTER_ROWS = 64
EXPERT_ROW0 = 8
V7X_VMEM_LIMIT = 56 * 1024 * 1024


def _cparams(*sem):
    return pltpu.CompilerParams(dimension_semantics=sem, vmem_limit_bytes=V7X_VMEM_LIMIT)


def _ada_kernel(c_ref, w_ref, b_ref, o_ref):
    c = c_ref[...]
    s = c * jax.nn.sigmoid(c)
    o_ref[...] = jnp.dot(s.astype(BF16), w_ref[...].astype(BF16),
                         preferred_element_type=F32) + b_ref[...]


def _ada(c_all, w_ada, b_ada):
    rows, d = c_all.shape
    ncol = w_ada.shape[1]
    tn = 1024
    return pl.pallas_call(
        _ada_kernel,
        grid=(ncol // tn,),
        in_specs=[pl.BlockSpec((rows, d), lambda j: (0, 0)),
                  pl.BlockSpec((d, tn), lambda j: (0, j)),
                  pl.BlockSpec((1, tn), lambda j: (0, j))],
        out_specs=pl.BlockSpec((rows, tn), lambda j: (0, j)),
        out_shape=jax.ShapeDtypeStruct((rows, ncol), F32),
        compiler_params=_cparams("arbitrary"),
        name="ada",
    )(c_all, w_ada, b_ada.reshape(1, ncol))


def _rms_mod(x, gain, scale, shift):
    ms = jnp.mean(x * x, axis=-1, keepdims=True)
    y = x * lax.rsqrt(ms + RMS_EPS) * gain
    return y * (1.0 + scale) + shift


def _two_stream_specs(tm, width, prompt_tiles):
    return [pl.BlockSpec((tm, width), lambda i: (jnp.minimum(i, prompt_tiles - 1), 0)),
            pl.BlockSpec((tm, width), lambda i: (jnp.maximum(i - prompt_tiles, 0), 0))]


def _norm1_kernel(xp_ref, xs_ref, g_ref, sc_ref, sh_ref, h_ref, *, groups, prompt_tiles):
    def run(x_ref):
        def body(gi, carry):
            r = pl.multiple_of(gi * GROUP_ROWS, GROUP_ROWS)
            h = _rms_mod(x_ref[pl.ds(r, GROUP_ROWS), :], g_ref[...],
                         sc_ref[pl.ds(gi, 1), :], sh_ref[pl.ds(gi, 1), :])
            h_ref[pl.ds(r, GROUP_ROWS), :] = h.astype(h_ref.dtype)
            return carry

        lax.fori_loop(0, groups, body, 0)

    i = pl.program_id(0)
    pl.when(i < prompt_tiles)(lambda: run(xp_ref))
    pl.when(i >= prompt_tiles)(lambda: run(xs_ref))


def _norm1(x_p, x_s, gain, scale_g, shift_g, tm):
    d = x_p.shape[1]
    n = x_p.shape[0] + x_s.shape[0]
    groups = tm // GROUP_ROWS
    prompt_tiles = x_p.shape[0] // tm
    return pl.pallas_call(
        functools.partial(_norm1_kernel, groups=groups, prompt_tiles=prompt_tiles),
        grid=(n // tm,),
        in_specs=_two_stream_specs(tm, d, prompt_tiles) + [
            pl.BlockSpec((1, d), lambda i: (0, 0)),
            pl.BlockSpec((groups, d), lambda i: (i, 0)),
            pl.BlockSpec((groups, d), lambda i: (i, 0))],
        out_specs=pl.BlockSpec((tm, d), lambda i: (i, 0)),
        out_shape=jax.ShapeDtypeStruct((n, d), BF16),
        compiler_params=_cparams("arbitrary"),
        name="norm1",
    )(x_p, x_s, gain.reshape(1, d), scale_g, shift_g)


def _proj_kernel(h_ref, w_ref, *o_refs, mode):
    acc = jnp.dot(h_ref[...], w_ref[...], preferred_element_type=F32)
    if mode == "qkv":
        o_refs[0][...] = acc.astype(BF16)
    elif mode == "glu":
        half = acc.shape[1] // 2
        o_refs[0][...] = acc[:, :half] * jax.nn.sigmoid(acc[:, half:])
    else:
        o_refs[0][...] = jax.nn.sigmoid(acc).astype(BF16)


def _proj(h, w, mode, tm, tnw, out_dtypes):
    n, d = h.shape
    ncol = w.shape[1]
    tno = tnw // 2 if mode == "glu" else tnw
    nout = ncol // 2 if mode == "glu" else ncol
    outs = pl.pallas_call(
        functools.partial(_proj_kernel, mode=mode),
        grid=(n // tm, ncol // tnw),
        in_specs=[pl.BlockSpec((tm, d), lambda i, j: (i, 0)),
                  pl.BlockSpec((d, tnw), lambda i, j: (0, j))],
        out_specs=[pl.BlockSpec((tm, tno), lambda i, j: (i, j)) for _ in out_dtypes],
        out_shape=[jax.ShapeDtypeStruct((n, nout), dt) for dt in out_dtypes],
        compiler_params=_cparams("arbitrary", "arbitrary"),
        name="proj_" + mode,
    )(h, w)
    return outs


def _kv_state_kernel(h_ref, w_ref, o_ref):
    o_ref[...] = jnp.dot(h_ref[...], w_ref[...], preferred_element_type=F32)


def _kv_state(h, w_kv, n_prompt, seq):
    n, d = h.shape
    batch = n_prompt // seq
    per_seq = seq // WINDOW
    steps = batch + (n - n_prompt) // WINDOW

    def row_block(i):
        return jnp.where(i < batch, (i + 1) * per_seq - 1, n_prompt // WINDOW + i - batch)

    return pl.pallas_call(
        _kv_state_kernel,
        grid=(steps,),
        in_specs=[pl.BlockSpec((WINDOW, d), lambda i: (row_block(i), 0)),
                  pl.BlockSpec((d, 2 * KV_DIM), lambda i: (0, 0))],
        out_specs=pl.BlockSpec((WINDOW, 2 * KV_DIM), lambda i: (i, 0)),
        out_shape=jax.ShapeDtypeStruct((steps * WINDOW, 2 * KV_DIM), F32),
        compiler_params=_cparams("arbitrary"),
        name="kv_state",
    )(h, w_kv)


def _t5_bucket(rel):
    nb = NUM_BUCKETS // 2
    n = -rel
    ret = jnp.where(n < 0, nb, 0)
    n = jnp.abs(n)
    max_exact = nb // 2
    nf = jnp.maximum(n, 1).astype(F32)
    large = max_exact + (jnp.log(nf / max_exact) / math.log(MAX_DISTANCE / max_exact)
                         * (nb - max_exact)).astype(jnp.int32)
    large = jnp.minimum(large, nb - 1)
    return ret + jnp.where(n < max_exact, n, large)


def _bias_kernel(tbl_ref, bkt_ref, o_ref):
    bkt = bkt_ref[...]
    nq = bkt.shape[0]
    for k in range(N_KV_HEADS):
        for g in range(Q_PER_KV):
            acc = jnp.zeros(bkt.shape, F32)
            for b in range(NUM_BUCKETS):
                acc = jnp.where(bkt == b, tbl_ref[b, k * Q_PER_KV + g], acc)
            o_ref[k, g * nq:(g + 1) * nq, :] = acc


def _rel_bias(rel_table, rel):
    nq, nk = rel.shape
    return pl.pallas_call(
        _bias_kernel,
        in_specs=[pl.BlockSpec(memory_space=pltpu.SMEM),
                  pl.BlockSpec((nq, nk), lambda: (0, 0))],
        out_specs=pl.BlockSpec((N_KV_HEADS, Q_PER_KV * nq, nk), lambda: (0, 0, 0)),
        out_shape=jax.ShapeDtypeStruct((N_KV_HEADS, Q_PER_KV * nq, nk), F32),
        name="rel_bias",
    )(rel_table.astype(F32), _t5_bucket(rel))


PAIR_BAND = 2 * CHUNK + WINDOW
HEADS_PER_COL = LANES // HEAD_DIM
COLS_PER_KV = Q_PER_KV // HEADS_PER_COL


def _pair_bias_kernel(tbl_ref, bkt_ref, o_ref):
    for e in range(2):
        bkt = bkt_ref[e]
        for k in range(N_KV_HEADS):
            for col in range(COLS_PER_KV):
                for half in range(HEADS_PER_COL):
                    head = k * Q_PER_KV + col * HEADS_PER_COL + half
                    acc = jnp.full(bkt.shape, NEG_INF, F32)
                    for b in range(NUM_BUCKETS):
                        acc = jnp.where(bkt == b, tbl_ref[b, head], acc)
                    r0 = (e * COLS_PER_KV + col) * CHUNK
                    o_ref[k, r0:r0 + CHUNK, half * PAIR_BAND:(half + 1) * PAIR_BAND] = acc


def _pair_bias(rel_table):
    qi = jnp.arange(CHUNK, dtype=jnp.int32)[None, :, None]
    kj = jnp.arange(PAIR_BAND, dtype=jnp.int32)[None, None, :]
    e = jnp.arange(2, dtype=jnp.int32)[:, None, None]
    rel = (kj - WINDOW) - (e * CHUNK + qi)
    key_chunk = kj // CHUNK - e
    seen = (key_chunk >= 0) & (key_chunk <= WIN_CHUNKS)
    bkt = jnp.where(seen, _t5_bucket(rel), -1)
    rows = 2 * COLS_PER_KV * CHUNK
    return pl.pallas_call(
        _pair_bias_kernel,
        in_specs=[pl.BlockSpec(memory_space=pltpu.SMEM),
                  pl.BlockSpec((2, CHUNK, PAIR_BAND), lambda: (0, 0, 0))],
        out_specs=pl.BlockSpec((N_KV_HEADS, rows, HEADS_PER_COL * PAIR_BAND), lambda: (0, 0, 0)),
        out_shape=jax.ShapeDtypeStruct((N_KV_HEADS, rows, HEADS_PER_COL * PAIR_BAND), F32),
        name="pair_bias",
    )(rel_table.astype(F32), bkt)


def _attend(qc, kk_all, vv_all, bias_ref, sink_ref, mask_thr):
    nq = qc.shape[0]
    nk = kk_all.shape[0]
    pieces = []
    for k in range(N_KV_HEADS):
        qs = jnp.concatenate(
            [qc[:, (k * Q_PER_KV + g) * HEAD_DIM:(k * Q_PER_KV + g + 1) * HEAD_DIM]
             for g in range(Q_PER_KV)], axis=0)
        kk = kk_all[:, k * HEAD_DIM:(k + 1) * HEAD_DIM]
        vv = vv_all[:, k * HEAD_DIM:(k + 1) * HEAD_DIM]
        lg = lax.dot_general(qs, kk, (((1,), (1,)), ((), ())),
                             preferred_element_type=F32) * (HEAD_DIM ** -0.5) + bias_ref[k]
        if mask_thr is not None:
            col = lax.broadcasted_iota(jnp.int32, (Q_PER_KV * nq, nk), 1)
            lg = jnp.where(col < mask_thr, NEG_INF, lg)
        s = sink_ref[k]
        m = jnp.maximum(jnp.max(lg, axis=-1, keepdims=True), s)
        p = jnp.exp(lg - m)
        den = jnp.sum(p, axis=-1, keepdims=True) + jnp.exp(s - m)
        o = jnp.dot(p.astype(BF16), vv, preferred_element_type=F32) / den
        pieces.extend(o[g * nq:(g + 1) * nq, :] for g in range(Q_PER_KV))
    return jnp.concatenate(pieces, axis=1)


def _attn_prompt_kernel(q_ref, kv_ref, halo_ref, bias_ref, sink_ref, o_ref, ka, kb, va, vb, *, pairs):
    t = pl.program_id(1)
    kvcat = jnp.concatenate([halo_ref[...], kv_ref[...]], axis=0)
    low = lax.broadcasted_iota(jnp.int32, (kvcat.shape[0], LANES), 1) < HEAD_DIM
    zero = jnp.zeros((kvcat.shape[0], LANES), BF16)
    ones_low = jnp.where(low, 1.0, 0.0).astype(BF16)
    ones_high = jnp.where(low, 0.0, 1.0).astype(BF16)
    kv_cols = KV_DIM // LANES
    for col in range(2 * kv_cols):
        x = kvcat[:, col * LANES:(col + 1) * LANES]
        xs = jnp.concatenate([x[:, HEAD_DIM:], x[:, :HEAD_DIM]], axis=1)
        dst_a, dst_b = (ka, kb) if col < kv_cols else (va, vb)
        k0 = (col % kv_cols) * HEADS_PER_COL
        dst_a[k0, :, 0:LANES] = jnp.where(low, x, zero)
        dst_b[k0, :, 0:LANES] = jnp.where(low, zero, xs)
        dst_a[k0 + 1, :, 0:LANES] = jnp.where(low, xs, zero)
        dst_b[k0 + 1, :, 0:LANES] = jnp.where(low, zero, x)
    for k in range(N_KV_HEADS):
        va[k, :, LANES:2 * LANES] = ones_low
        vb[k, :, LANES:2 * LANES] = ones_high

    out_low = lax.broadcasted_iota(jnp.int32, (2 * COLS_PER_KV * CHUNK, LANES), 1) < HEAD_DIM
    for p in range(pairs):
        r0 = p * 2 * CHUNK
        for k in range(N_KV_HEADS):
            lhs = jnp.concatenate(
                [q_ref[r0 + e * CHUNK:r0 + (e + 1) * CHUNK, (k * COLS_PER_KV + c) * LANES:(k * COLS_PER_KV + c + 1) * LANES]
                 for e in range(2) for c in range(COLS_PER_KV)], axis=0)
            keys = jnp.concatenate([ka[k, r0:r0 + PAIR_BAND, :], kb[k, r0:r0 + PAIR_BAND, :]], axis=0)
            lg = lax.dot_general(lhs, keys, (((1,), (1,)), ((), ())),
                                 preferred_element_type=F32) * (HEAD_DIM ** -0.5) + bias_ref[k]
            probs, sink_terms = [], []
            for half in range(HEADS_PER_COL):
                seg = lg[:, half * PAIR_BAND:(half + 1) * PAIR_BAND]
                if p == 0:
                    kcol = lax.broadcasted_iota(jnp.int32, seg.shape, 1)
                    seg = jnp.where(kcol < jnp.where(t == 0, WINDOW, 0), NEG_INF, seg)
                s = sink_ref[k, half]
                folded = seg[:, :LANES]
                for j in range(1, PAIR_BAND // LANES):
                    folded = jnp.maximum(folded, seg[:, j * LANES:(j + 1) * LANES])
                m = jnp.maximum(jnp.max(folded, axis=-1, keepdims=True), s)
                pr = jnp.exp(seg - jnp.concatenate([m] * (PAIR_BAND // LANES), axis=1))
                sink_terms.append(jnp.exp(s - m))
                probs.append(pr.astype(BF16))
            vals = jnp.concatenate([va[k, r0:r0 + PAIR_BAND, :], vb[k, r0:r0 + PAIR_BAND, :]], axis=0)
            oe = jnp.dot(jnp.concatenate(probs, axis=1), vals, preferred_element_type=F32)
            o = oe[:, :LANES] / (oe[:, LANES:] + jnp.where(out_low, sink_terms[0], sink_terms[1]))
            for e in range(2):
                for c in range(COLS_PER_KV):
                    rr = (e * COLS_PER_KV + c) * CHUNK
                    o_ref[r0 + e * CHUNK:r0 + (e + 1) * CHUNK,
                          (k * COLS_PER_KV + c) * LANES:(k * COLS_PER_KV + c + 1) * LANES] = (
                        o[rr:rr + CHUNK, :].astype(o_ref.dtype))


def _attn_prompt(q, kv, bias, sink_rows, n_prompt, seq, pairs):
    rows = pairs * 2 * CHUNK
    tiles = seq // rows
    halo_per_tile = rows // WINDOW
    keys = WINDOW + rows
    return pl.pallas_call(
        functools.partial(_attn_prompt_kernel, pairs=pairs),
        grid=(n_prompt // seq, tiles),
        scratch_shapes=[pltpu.VMEM((N_KV_HEADS, keys, LANES), BF16) for _ in range(2)]
        + [pltpu.VMEM((N_KV_HEADS, keys, 2 * LANES), BF16) for _ in range(2)],
        in_specs=[pl.BlockSpec((rows, ATTN_DIM), lambda b, t: (b * tiles + t, 0)),
                  pl.BlockSpec((rows, 2 * KV_DIM), lambda b, t: (b * tiles + t, 0)),
                  pl.BlockSpec((WINDOW, 2 * KV_DIM),
                               lambda b, t: (jnp.maximum((b * tiles + t) * halo_per_tile - 1, 0), 0)),
                  pl.BlockSpec(bias.shape, lambda b, t: (0, 0, 0)),
                  pl.BlockSpec(sink_rows.shape, lambda b, t: (0, 0, 0, 0))],
        out_specs=pl.BlockSpec((rows, ATTN_DIM), lambda b, t: (b * tiles + t, 0)),
        out_shape=jax.ShapeDtypeStruct((n_prompt, ATTN_DIM), BF16),
        compiler_params=_cparams("arbitrary", "arbitrary"),
        name="attn_prompt",
    )(q, kv, kv, bias, sink_rows)


def _attn_sample_kernel(q_ref, kv_ref, ck_ref, cv_ref, bias_ref, sink_ref, o_ref):
    kv = kv_ref[...]
    kk = jnp.concatenate([ck_ref[0].astype(BF16), kv[:, :KV_DIM]], axis=0)
    vv = jnp.concatenate([cv_ref[0].astype(BF16), kv[:, KV_DIM:]], axis=0)
    o = _attend(q_ref[...], kk, vv, bias_ref, sink_ref, None)
    o_ref[...] = o.astype(o_ref.dtype)


def _attn_sample(q, kv, cache_k, cache_v, bias, sink_rows, n_prompt, t_new):
    dec_batch, buf, _ = cache_k.shape
    first = n_prompt // t_new
    return pl.pallas_call(
        _attn_sample_kernel,
        grid=(dec_batch,),
        in_specs=[pl.BlockSpec((t_new, ATTN_DIM), lambda s: (first + s, 0)),
                  pl.BlockSpec((t_new, 2 * KV_DIM), lambda s: (first + s, 0)),
                  pl.BlockSpec((1, buf, KV_DIM), lambda s: (s, 0, 0)),
                  pl.BlockSpec((1, buf, KV_DIM), lambda s: (s, 0, 0)),
                  pl.BlockSpec(bias.shape, lambda s: (0, 0, 0)),
                  pl.BlockSpec(sink_rows.shape, lambda s: (0, 0, 0))],
        out_specs=pl.BlockSpec((t_new, ATTN_DIM), lambda s: (s, 0)),
        out_shape=jax.ShapeDtypeStruct((dec_batch * t_new, ATTN_DIM), BF16),
        compiler_params=_cparams("arbitrary"),
        name="attn_sample",
    )(q, kv, cache_k, cache_v, bias, sink_rows)


def _route(lt, eidx_ref, wtok_ref, rank_ref, cnt_ref, cnt_scr):
    gl = [lt[r:r + 1, :] for r in range(N_GROUPS)]
    gmax = gl[0]
    gsel = jnp.zeros(gl[0].shape, jnp.int32)
    for r in range(1, N_GROUPS):
        better = gl[r] > gmax
        gsel = jnp.where(better, r, gsel)
        gmax = jnp.maximum(gmax, gl[r])
    gexp = [jnp.exp(v - gmax) for v in gl]
    gsum = gexp[0]
    for r in range(1, N_GROUPS):
        gsum = gsum + gexp[r]
    psel = jnp.zeros(gl[0].shape, F32)
    for r in range(N_GROUPS):
        psel = jnp.where(gsel == r, gexp[r] / gsum, psel)
    el = jnp.zeros((EXPERTS_PER_GROUP, lt.shape[1]), F32)
    for r in range(N_GROUPS):
        lo = EXPERT_ROW0 + r * EXPERTS_PER_GROUP
        el = jnp.where(gsel == r, lt[lo:lo + EXPERTS_PER_GROUP, :], el)
    emax = jnp.max(el, axis=0, keepdims=True)
    ee = jnp.exp(el - emax)
    pin = ee / jnp.sum(ee, axis=0, keepdims=True)
    idx = lax.broadcasted_iota(jnp.int32, pin.shape, 0)
    p1 = jnp.max(pin, axis=0, keepdims=True)
    i1 = jnp.min(jnp.where(pin == p1, idx, EXPERTS_PER_GROUP), axis=0, keepdims=True)
    rest = jnp.where(idx == i1, -1.0, pin)
    p2 = jnp.max(rest, axis=0, keepdims=True)
    i2 = jnp.min(jnp.where(rest == p2, idx, EXPERTS_PER_GROUP), axis=0, keepdims=True)
    tot = p1 + p2
    e1 = gsel * EXPERTS_PER_GROUP + i1
    e2 = gsel * EXPERTS_PER_GROUP + i2
    eidx_ref[0:1, :] = e1
    eidx_ref[1:2, :] = e2
    wtok_ref[0:1, :] = psel * p1 / tot
    wtok_ref[1:2, :] = psel * p2 / tot

    t = lt.shape[1]
    eiota = lax.broadcasted_iota(jnp.int32, (N_EXPERTS, t), 0)
    oh1 = (eiota == e1).astype(F32)
    oh2 = (eiota == e2).astype(F32)
    both = oh1 + oh2
    before = (lax.broadcasted_iota(jnp.int32, (t, t), 0)
              < lax.broadcasted_iota(jnp.int32, (t, t), 1)).astype(BF16)
    prior = jnp.dot(both.astype(BF16), before, preferred_element_type=F32) + cnt_scr[:, 0:1]
    rank_ref[0:1, :] = jnp.sum(oh1 * prior, axis=0, keepdims=True).astype(jnp.int32)
    rank_ref[1:2, :] = jnp.sum(oh2 * prior, axis=0, keepdims=True).astype(jnp.int32)
    cnt_scr[...] = cnt_scr[...] + jnp.sum(both, axis=1, keepdims=True)
    cnt_ref[...] = cnt_scr[...]


def _post_kernel(xp_ref, xs_ref, u_ref, uhalo_ref, sconv_ref, op_ref, os_ref, gates_ref,
                 gate1_ref, shift2_ref, scale2_ref,
                 cw_ref, cb_ref, lng_ref, lnb_ref, wpw2_ref, bpw2_ref, wao_ref, wout_ref,
                 n2_ref, wrt_ref, brt_ref,
                 x1_ref, h2_ref, eidx_ref, wtok_ref, rank_ref, cnt_ref,
                 uext, shift_scr, d_scr, s_scr, o_scr, mix_scr, h2b_scr, cnt_scr,
                 *, groups, prompt_tiles, tiles_per_seq):
    i = pl.program_id(0)
    d_model = xp_ref.shape[1]

    @pl.when(i == 0)
    def _():
        cnt_scr[...] = jnp.zeros(cnt_scr.shape, F32)
    nch = cw_ref.shape[0]
    cw = uext.shape[2]
    hist = uext.shape[1] - GROUP_ROWS
    lead = hist - (CONV_WIDTH - 1)

    def put_hist(g, rows):
        for c in range(nch):
            uext[g * nch + c, 0:hist, :] = rows[:, c * cw:(c + 1) * cw]

    @pl.when(i < prompt_tiles)
    def _():
        first = (i % tiles_per_seq) == 0
        put_hist(0, jnp.where(first, 0.0, uhalo_ref[...]))
        for g in range(1, groups):
            put_hist(g, u_ref[(g - 1) * GROUP_ROWS:g * GROUP_ROWS, :])
        o_scr[...] = op_ref[...]

    @pl.when(i >= prompt_tiles)
    def _():
        for g in range(groups):
            put_hist(g, sconv_ref[g])
        o_scr[...] = os_ref[...]

    for g in range(groups):
        for c in range(nch):
            uext[g * nch + c, hist:hist + GROUP_ROWS, :] = (
                u_ref[g * GROUP_ROWS:(g + 1) * GROUP_ROWS, c * cw:(c + 1) * cw])

    span = shift_scr.shape[1]

    def conv_chunk(k, carry):
        c = k % nch
        win_all = uext[k]
        for s in range(1, SUBLANES):
            shift_scr[s, :, :] = pltpu.roll(win_all, win_all.shape[0] - s, 0)[0:span, :]
        acc = None
        for j in range(CONV_WIDTH):
            base, s = divmod(lead + j, SUBLANES)
            rows = pl.ds(base * SUBLANES, GROUP_ROWS)
            win = uext[k, rows, :] if s == 0 else shift_scr[s, rows, :]
            term = win * cw_ref[c, j:j + 1, :]
            acc = term if acc is None else acc + term
        d_scr[k] = acc + cb_ref[c]
        return carry

    lax.fori_loop(0, groups * nch, conv_chunk, 0)

    for g in range(groups):
        dd = jnp.concatenate([d_scr[g * nch + c] for c in range(nch)], axis=1)
        mu = jnp.mean(dd, axis=-1, keepdims=True)
        var = jnp.mean(jnp.square(dd - mu), axis=-1, keepdims=True)
        y = (dd - mu) * lax.rsqrt(var + LN_EPS) * lng_ref[...] + lnb_ref[...]
        s_scr[g * GROUP_ROWS:(g + 1) * GROUP_ROWS, :] = (y * jax.nn.sigmoid(y)).astype(BF16)

    conv_out = jnp.dot(s_scr[...], wpw2_ref[...], preferred_element_type=F32) + bpw2_ref[...]
    attn_out = jnp.dot(o_scr[...], wao_ref[...], preferred_element_type=F32)
    merged = (gates_ref[:, :d_model].astype(F32) * attn_out
              + gates_ref[:, d_model:].astype(F32) * conv_out)
    mix_scr[...] = jnp.dot(merged.astype(BF16), wout_ref[...], preferred_element_type=F32)

    def residual(x_ref):
        def res_group(g, carry):
            r = pl.multiple_of(g * GROUP_ROWS, GROUP_ROWS)
            x1 = (x_ref[pl.ds(r, GROUP_ROWS), :]
                  + gate1_ref[pl.ds(g, 1), :] * mix_scr[pl.ds(r, GROUP_ROWS), :])
            x1_ref[pl.ds(r, GROUP_ROWS), :] = x1
            h = _rms_mod(x1, n2_ref[...], scale2_ref[pl.ds(g, 1), :], shift2_ref[pl.ds(g, 1), :])
            h2b_scr[pl.ds(r, GROUP_ROWS), :] = h.astype(BF16)
            h2_ref[pl.ds(r, GROUP_ROWS), :] = h
            return carry

        lax.fori_loop(0, groups, res_group, 0)

    pl.when(i < prompt_tiles)(lambda: residual(xp_ref))
    pl.when(i >= prompt_tiles)(lambda: residual(xs_ref))

    lt = lax.dot_general(wrt_ref[...], h2b_scr[...], (((1,), (1,)), ((), ())),
                         preferred_element_type=F32) + brt_ref[...]
    _route(lt, eidx_ref, wtok_ref, rank_ref, cnt_ref, cnt_scr)


def _post(x_p, x_s, u, sconv_pad, o_p, o_s, gates, gate1_g, shift2_g, scale2_g, p, n_prompt, seq, tm):
    d = x_p.shape[1]
    n = x_p.shape[0] + x_s.shape[0]
    cdim = u.shape[1]
    nch = cdim // CONV_CHUNK
    groups = tm // GROUP_ROWS
    prompt_tiles = n_prompt // tm
    const = lambda shape: pl.BlockSpec(shape, lambda i: (0,) * len(shape))
    row = lambda w: pl.BlockSpec((tm, w), lambda i: (i, 0))
    grp = pl.BlockSpec((groups, d), lambda i: (i, 0))
    return pl.pallas_call(
        functools.partial(_post_kernel, groups=groups, prompt_tiles=prompt_tiles,
                          tiles_per_seq=seq // tm),
        grid=(n // tm,),
        in_specs=_two_stream_specs(tm, d, prompt_tiles) + [
                  row(cdim),
                  pl.BlockSpec((GROUP_ROWS, cdim), lambda i: (jnp.maximum(i * groups - 1, 0), 0)),
                  pl.BlockSpec((groups, GROUP_ROWS, cdim),
                               lambda i: (jnp.maximum(i - prompt_tiles, 0), 0, 0))]
                 + _two_stream_specs(tm, ATTN_DIM, prompt_tiles) + [
                  row(2 * d), grp, grp, grp,
                  const((nch, CONV_WIDTH, CONV_CHUNK)), const((nch, 1, CONV_CHUNK)),
                  const((1, cdim)), const((1, cdim)),
                  const((cdim, d)), const((1, d)), const((ATTN_DIM, d)), const((d, d)),
                  const((1, d)), const((ROUTER_ROWS, d)), const((ROUTER_ROWS, 1))],
        out_specs=[row(d),
                   row(d),
                   pl.BlockSpec((2, tm), lambda i: (0, i)),
                   pl.BlockSpec((2, tm), lambda i: (0, i)),
                   pl.BlockSpec((2, tm), lambda i: (0, i)),
                   pl.BlockSpec((N_EXPERTS, LANES), lambda i: (0, 0))],
        out_shape=[jax.ShapeDtypeStruct((n, d), F32),
                   jax.ShapeDtypeStruct((n, d), F32),
                   jax.ShapeDtypeStruct((2, n), jnp.int32),
                   jax.ShapeDtypeStruct((2, n), F32),
                   jax.ShapeDtypeStruct((2, n), jnp.int32),
                   jax.ShapeDtypeStruct((N_EXPERTS, LANES), F32)],
        scratch_shapes=[pltpu.VMEM((groups * nch, 2 * GROUP_ROWS, CONV_CHUNK), F32),
                        pltpu.VMEM((SUBLANES, 2 * GROUP_ROWS - SUBLANES, CONV_CHUNK), F32),
                        pltpu.VMEM((groups * nch, GROUP_ROWS, CONV_CHUNK), F32),
                        pltpu.VMEM((tm, cdim), BF16),
                        pltpu.VMEM((tm, ATTN_DIM), BF16),
                        pltpu.VMEM((tm, d), F32),
                        pltpu.VMEM((tm, d), BF16),
                        pltpu.VMEM((N_EXPERTS, LANES), F32)],
        compiler_params=_cparams("arbitrary"),
        name="post",
    )(x_p, x_s, u, u, sconv_pad, o_p, o_s, gates, gate1_g, shift2_g, scale2_g,
      p["conv_w"], p["conv_b"], p["ln_g"], p["ln_b"], p["w_pw2"], p["b_pw2"], p["w_ao"], p["w_out"],
      p["n2"], p["w_rt"], p["b_rt"])


TABLE_GROUP = 8
GATHER_AHEAD = 2


def _grouped_tables(tab, group=TABLE_GROUP, ahead=1):
    steps, w = tab.shape
    assert steps % group == 0
    padded = jnp.concatenate([tab] + [tab[-1:]] * ahead, axis=0)
    cols = [tab.reshape(steps // group, group * w)] + [padded[group + a::group][:steps // group]
                                                         for a in range(ahead)]
    return jnp.concatenate(cols, axis=1).reshape(steps // group, 1, (group + ahead) * w)

def _moe_kernel(blk_e_ref, nused_ref, src_ref, roww_ref,
                h2_hbm, wg_ref, wu_ref, wd_ref, y_ref,
                xbuf, x16, wg_b, wu_b, wd_b, gsem):
    i = pl.program_id(0)
    nused = nused_ref[0]
    nslots = GATHER_AHEAD + 1
    slot = i % nslots
    base = (i % TABLE_GROUP) * MOE_BLOCK

    def gather(first, sl):
        for r in range(MOE_BLOCK):
            pltpu.make_async_copy(h2_hbm.at[pl.ds(src_ref[0, 0, first + r], 1), :],
                                  xbuf.at[sl, pl.ds(r, 1), :], gsem.at[sl]).start(priority=r % 2)

    def gather_wait(sl):
        pltpu.make_async_copy(h2_hbm.at[pl.ds(0, MOE_BLOCK), :], xbuf.at[sl], gsem.at[sl]).wait()

    @pl.when(i == 0)
    def _():
        for a in range(GATHER_AHEAD):
            gather(base + a * MOE_BLOCK, a)

    @pl.when(i >= nused)
    def _():
        y_ref[...] = jnp.zeros(y_ref.shape, F32)

    @pl.when(i < nused)
    def _():
        changed = jnp.logical_or(i == 0, blk_e_ref[i] != blk_e_ref[jnp.maximum(i - 1, 0)])

        @pl.when(changed)
        def _():
            wg_b[...] = wg_ref[0].astype(BF16)
            wu_b[...] = wu_ref[0].astype(BF16)
            wd_b[...] = wd_ref[0].astype(BF16)

        gather_wait(slot)
        x16[...] = xbuf[slot].astype(BF16)
        gather(base + GATHER_AHEAD * MOE_BLOCK, (i + GATHER_AHEAD) % nslots)
        x = x16[...]
        hg = jnp.dot(x, wg_b[...], preferred_element_type=F32)
        hu = jnp.dot(x, wu_b[...], preferred_element_type=F32)
        hid = (hg * jax.nn.sigmoid(hg) * hu).astype(BF16)
        y_ref[...] = jnp.dot(hid, wd_b[...], preferred_element_type=F32) * roww_ref[...]

        @pl.when(i == nused - 1)
        def _():
            for a in range(1, GATHER_AHEAD + 1):
                gather_wait((i + a) % nslots)


def _moe(h2, blk_e, nused, src, roww, w_gate, w_up, w_down):
    n_blocks = blk_e.shape[0]
    n_exp, d, de = w_gate.shape
    src_tables = _grouped_tables(src.reshape(n_blocks, MOE_BLOCK), ahead=GATHER_AHEAD)
    grid_spec = pltpu.PrefetchScalarGridSpec(
        num_scalar_prefetch=2,
        grid=(n_blocks,),
        in_specs=[pl.BlockSpec((1, 1, src_tables.shape[2]), lambda i, be, nu: (i // TABLE_GROUP, 0, 0),
                               memory_space=pltpu.SMEM),
                  pl.BlockSpec((MOE_BLOCK, 1), lambda i, be, nu: (i, 0)),
                  pl.BlockSpec(memory_space=pl.ANY),
                  pl.BlockSpec((1, d, de), lambda i, be, nu: (be[i], 0, 0)),
                  pl.BlockSpec((1, d, de), lambda i, be, nu: (be[i], 0, 0)),
                  pl.BlockSpec((1, de, d), lambda i, be, nu: (be[i], 0, 0))],
        out_specs=pl.BlockSpec((MOE_BLOCK, d), lambda i, be, nu: (i, 0)),
        scratch_shapes=[pltpu.VMEM((GATHER_AHEAD + 1, MOE_BLOCK, d), F32),
                        pltpu.VMEM((MOE_BLOCK, d), BF16),
                        pltpu.VMEM((d, de), BF16), pltpu.VMEM((d, de), BF16), pltpu.VMEM((de, d), BF16),
                        pltpu.SemaphoreType.DMA((GATHER_AHEAD + 1,))],
    )
    return pl.pallas_call(
        _moe_kernel,
        grid_spec=grid_spec,
        out_shape=jax.ShapeDtypeStruct((n_blocks * MOE_BLOCK, d), F32),
        compiler_params=_cparams("arbitrary"),
        name="moe",
    )(blk_e, nused, src_tables, roww, h2, w_gate, w_up, w_down)


def _dispatch(eidx, wtok, rank, counts, n):
    a_tot = 2 * n
    experts = jnp.arange(N_EXPERTS, dtype=jnp.int32)
    padded = (counts + MOE_BLOCK - 1) // MOE_BLOCK * MOE_BLOCK
    pad_end = jnp.sum(jnp.where(experts[None, :] <= experts[:, None], padded[None, :], 0), axis=1)
    pad_start = pad_end - padded
    dest = jnp.sum(jnp.where(eidx[:, :, None] == experts, pad_start, 0), axis=-1) + rank
    n_blocks = -(-(a_tot + N_EXPERTS * (MOE_BLOCK - 1)) // MOE_BLOCK)
    n_blocks = -(-n_blocks // TABLE_GROUP) * TABLE_GROUP
    n_rows = n_blocks * MOE_BLOCK
    tok = jnp.tile(jnp.arange(n, dtype=jnp.int32), 2)
    upd = jnp.stack([tok, lax.bitcast_convert_type(wtok.reshape(-1), jnp.int32)], axis=1)
    rows = jnp.zeros((n_rows, 2), jnp.int32).at[dest.reshape(-1)].set(upd, unique_indices=True)
    blk_start = jnp.arange(n_blocks, dtype=jnp.int32) * MOE_BLOCK
    blk_e = jnp.minimum(jnp.sum((pad_end[None, :] <= blk_start[:, None]).astype(jnp.int32), axis=1),
                        N_EXPERTS - 1)
    nused = (pad_end[-1] // MOE_BLOCK).reshape(1)
    roww = lax.bitcast_convert_type(rows[:, 1], F32)
    return blk_e, nused, rows[:, 0].reshape(n_blocks, 1, MOE_BLOCK), roww.reshape(n_rows, 1), dest


def _final_kernel(d0_ref, d1_ref, x1_ref, gate2_ref, fg_ref, yb_hbm,
                  op_ref, os_ref, ybuf, sem, *, groups, prompt_tiles, tiles, table_group):
    i = pl.program_id(0)
    slot = i % 2
    tm = x1_ref.shape[0]
    base = (i % table_group) * tm

    def gather(first, sl):
        for r in range(tm):
            pltpu.make_async_copy(yb_hbm.at[pl.ds(d0_ref[0, 0, first + r], 1), :],
                                  ybuf.at[sl, 0, pl.ds(r, 1), :], sem.at[sl]).start(priority=0)
            pltpu.make_async_copy(yb_hbm.at[pl.ds(d1_ref[0, 0, first + r], 1), :],
                                  ybuf.at[sl, 1, pl.ds(r, 1), :], sem.at[sl]).start(priority=1)

    def gather_wait(sl):
        for k in range(2):
            pltpu.make_async_copy(yb_hbm.at[pl.ds(0, tm), :], ybuf.at[sl, k], sem.at[sl]).wait()

    @pl.when(i == 0)
    def _():
        gather(base, 0)

    gather_wait(slot)
    gather(base + tm, 1 - slot)

    def run(o_ref):
        def body(g, carry):
            r = pl.multiple_of(g * GROUP_ROWS, GROUP_ROWS)
            moe = ybuf[slot, 0, pl.ds(r, GROUP_ROWS), :] + ybuf[slot, 1, pl.ds(r, GROUP_ROWS), :]
            x2 = x1_ref[pl.ds(r, GROUP_ROWS), :] + gate2_ref[pl.ds(g, 1), :] * moe
            ms = jnp.mean(x2 * x2, axis=-1, keepdims=True)
            o_ref[pl.ds(r, GROUP_ROWS), :] = x2 * lax.rsqrt(ms + RMS_EPS) * fg_ref[...]
            return carry

        lax.fori_loop(0, groups, body, 0)

    pl.when(i < prompt_tiles)(lambda: run(op_ref))
    pl.when(i >= prompt_tiles)(lambda: run(os_ref))

    @pl.when(i == tiles - 1)
    def _():
        gather_wait(1 - slot)


def _final(x1, yb, dest, gate2_g, final_g, n_prompt, tm):
    n, d = x1.shape
    groups = tm // GROUP_ROWS
    tiles = n // tm
    prompt_tiles = n_prompt // tm
    table_group = max(g for g in range(1, TABLE_GROUP + 1) if tiles % g == 0)
    tables = [_grouped_tables(dest[k].reshape(tiles, tm), table_group) for k in range(2)]
    idx_spec = pl.BlockSpec((1, 1, tables[0].shape[2]), lambda i: (i // table_group, 0, 0),
                            memory_space=pltpu.SMEM)
    return pl.pallas_call(
        functools.partial(_final_kernel, groups=groups, prompt_tiles=prompt_tiles, tiles=tiles,
                          table_group=table_group),
        grid=(tiles,),
        in_specs=[idx_spec, idx_spec,
                  pl.BlockSpec((tm, d), lambda i: (i, 0)),
                  pl.BlockSpec((groups, d), lambda i: (i, 0)),
                  pl.BlockSpec((1, d), lambda i: (0, 0)),
                  pl.BlockSpec(memory_space=pl.ANY)],
        out_specs=_two_stream_specs(tm, d, prompt_tiles),
        out_shape=[jax.ShapeDtypeStruct((n_prompt, d), F32),
                   jax.ShapeDtypeStruct((n - n_prompt, d), F32)],
        scratch_shapes=[pltpu.VMEM((2, 2, tm, d), F32), pltpu.SemaphoreType.DMA((2,))],
        compiler_params=_cparams("arbitrary"),
        name="final",
    )(tables[0], tables[1], x1, gate2_g, final_g.reshape(1, d), yb)


def _layer(x_p, x_s, c_all, cache_k, cache_v, sconv, rel_table, lp, seq, t_new):
    (n1, n2, w_ada, b_ada, w_in, sink, w_ao, conv_w, conv_b, ln_g, ln_b, w_pw2, b_pw2, w_out,
     w_grp, b_grp, w_rt, b_rt, w_gate, w_up, w_down) = lp
    n_prompt, d = x_p.shape
    n = n_prompt + x_s.shape[0]
    batch = n_prompt // seq
    dec_batch = (n - n_prompt) // t_new
    cdim = conv_w.shape[1]
    tm = 256

    c_rows = -(-c_all.shape[0] // 8) * 8
    c_pad = jnp.pad(c_all, ((0, c_rows - c_all.shape[0]), (0, 0)))
    mods = _ada(c_pad, w_ada, b_ada)

    def per_group(k):
        m = mods[:, k * d:(k + 1) * d]
        mp = jnp.broadcast_to(m[:batch, None, :], (batch, seq // GROUP_ROWS, d))
        return jnp.concatenate([mp.reshape(batch * (seq // GROUP_ROWS), d), m[batch:batch + dec_batch]],
                               axis=0)

    shift1, scale1, gate1, shift2, scale2, gate2 = [per_group(k) for k in range(6)]

    h1 = _norm1(x_p, x_s, n1, scale1, shift1, 512)

    k_off = ATTN_DIM
    glu_off = k_off + 2 * KV_DIM
    gate_off = glu_off + 2 * cdim
    w_in_b = w_in.astype(BF16)
    tglu = 256
    w_glu = jnp.stack([w_in_b[:, glu_off:glu_off + cdim].reshape(d, cdim // tglu, tglu),
                       w_in_b[:, glu_off + cdim:gate_off].reshape(d, cdim // tglu, tglu)],
                      axis=2).reshape(d, 2 * cdim)
    tmm = 1536
    (q,) = _proj(h1, w_in_b[:, :k_off], "qkv", tmm, 512, [BF16])
    (kv,) = _proj(h1, w_in_b[:, k_off:glu_off], "qkv", tmm, 512, [BF16])
    kv_state = _kv_state(h1, w_in_b[:, k_off:glu_off], n_prompt, seq)
    (u,) = _proj(h1, w_glu, "glu", tmm, 2 * tglu, [F32])
    (gates,) = _proj(h1, w_in_b[:, gate_off:], "gate", tmm, 1024, [BF16])

    bias_p = _pair_bias(rel_table)
    buf = cache_k.shape[1]
    kpos = jnp.concatenate([jnp.arange(buf, dtype=jnp.int32) - buf, jnp.arange(t_new, dtype=jnp.int32)])
    bias_s = _rel_bias(rel_table, kpos[None, :] - jnp.arange(t_new, dtype=jnp.int32)[:, None])
    sink_f = sink.astype(F32).reshape(N_KV_HEADS, Q_PER_KV, 1, 1)
    sink_kch = sink.astype(F32).reshape(N_KV_HEADS, 1, COLS_PER_KV, HEADS_PER_COL).transpose(0, 3, 1, 2)
    sink_p = jnp.broadcast_to(sink_kch[..., None, None],
                              (N_KV_HEADS, HEADS_PER_COL, 2, COLS_PER_KV, CHUNK, LANES)).reshape(
        N_KV_HEADS, HEADS_PER_COL, 2 * COLS_PER_KV * CHUNK, LANES)
    sink_s = jnp.broadcast_to(sink_f, (N_KV_HEADS, Q_PER_KV, t_new, 1)).reshape(N_KV_HEADS, Q_PER_KV * t_new, 1)
    o_p = _attn_prompt(q, kv, bias_p, sink_p, n_prompt, seq, 2)
    o_s = _attn_sample(q, kv, cache_k.reshape(dec_batch, buf, KV_DIM), cache_v.reshape(dec_batch, buf, KV_DIM),
                       bias_s, sink_s, n_prompt, t_new)

    hist = GROUP_ROWS
    sconv_pad = jnp.pad(sconv, ((0, 0), (hist - sconv.shape[1], 0), (0, 0)))
    w_rt_t = jnp.zeros((ROUTER_ROWS, d), F32)
    w_rt_t = w_rt_t.at[:N_GROUPS].set(w_grp.T).at[EXPERT_ROW0:EXPERT_ROW0 + N_EXPERTS].set(w_rt.T)
    b_rt_t = jnp.zeros((ROUTER_ROWS, 1), F32)
    b_rt_t = b_rt_t.at[:N_GROUPS, 0].set(b_grp.astype(F32)).at[EXPERT_ROW0:EXPERT_ROW0 + N_EXPERTS, 0].set(
        b_rt.astype(F32))
    nch = cdim // CONV_CHUNK
    conv_w_c = conv_w.reshape(CONV_WIDTH, nch, CONV_CHUNK).transpose(1, 0, 2)
    params = dict(conv_w=conv_w_c, conv_b=conv_b.reshape(nch, 1, CONV_CHUNK), ln_g=ln_g.reshape(1, cdim),
                  ln_b=ln_b.reshape(1, cdim), w_pw2=w_pw2.astype(BF16), b_pw2=b_pw2.reshape(1, d),
                  w_ao=w_ao.astype(BF16), w_out=w_out.astype(BF16), n2=n2.reshape(1, d),
                  w_rt=w_rt_t.astype(BF16), b_rt=b_rt_t)
    x1, h2, eidx, wtok, rank, cnt = _post(x_p, x_s, u, sconv_pad, o_p, o_s, gates, gate1, shift2, scale2,
                                             params, n_prompt, seq, tm)

    blk_e, nused, src, roww, dest = _dispatch(eidx, wtok, rank, cnt[:, 0].astype(jnp.int32), n)
    yb = _moe(h2, blk_e, nused, src, roww, w_gate, w_up, w_down)
    return x1, yb, dest, gate2, kv_state, u


def kernel(x_prompt, x_sample, c_prompt, c_sample, cache_k, cache_v, state_conv, rel_bias_table, norm1_g, norm2_g, w_ada, b_ada, w_in, attn_sink, w_attn_o, conv_w, conv_b, conv_ln_g, conv_ln_b, w_pw2, b_pw2, w_out, w_group, b_group, w_router, b_router, w_gate, w_up, w_down, final_g):
    batch, seq, d = x_prompt.shape
    dec_batch, t_new, _ = x_sample.shape
    depth = norm1_g.shape[0]
    assert depth == 1, "single trunk layer"
    assert t_new == GROUP_ROWS and seq % GROUP_ROWS == 0
    n_prompt = batch * seq
    n = n_prompt + dec_batch * t_new
    x_p = x_prompt.reshape(n_prompt, d)
    x_s = x_sample.reshape(dec_batch * t_new, d)
    c_all = jnp.concatenate([c_prompt, c_sample], axis=0)
    l = 0
    lp = tuple(a.reshape(a.shape[1:]) for a in (
        norm1_g, norm2_g, w_ada, b_ada, w_in, attn_sink, w_attn_o, conv_w, conv_b, conv_ln_g, conv_ln_b,
        w_pw2, b_pw2, w_out, w_group, b_group, w_router, b_router, w_gate, w_up, w_down))
    x1, yb, dest, gate2, kv_state, u = _layer(x_p, x_s, c_all, cache_k[l], cache_v[l], state_conv[l],
                                              rel_bias_table, lp, seq, t_new)
    out_p, out_s = _final(x1, yb, dest, gate2, final_g, n_prompt, 256)

    y_prompt = out_p.reshape(batch, seq, d)
    y_sample = out_s.reshape(dec_batch, t_new, d)
    cdim = u.shape[1]
    kvp = kv_state[:batch * WINDOW].reshape(batch, WINDOW, 2 * KV_DIM)
    new_k_prompt = kvp[..., :KV_DIM].reshape(1, batch, WINDOW, N_KV_HEADS, HEAD_DIM)
    new_v_prompt = kvp[..., KV_DIM:].reshape(1, batch, WINDOW, N_KV_HEADS, HEAD_DIM)
    new_conv_prompt = jnp.stack([u[(b + 1) * seq - (CONV_WIDTH - 1):(b + 1) * seq] for b in range(batch)])[None]
    kvs = kv_state[batch * WINDOW:].reshape(dec_batch, t_new, 2 * KV_DIM)
    buf = cache_k.shape[2]
    k_new = kvs[..., :KV_DIM].reshape(dec_batch, t_new, N_KV_HEADS, HEAD_DIM)
    v_new = kvs[..., KV_DIM:].reshape(dec_batch, t_new, N_KV_HEADS, HEAD_DIM)
    new_k_sample = jnp.concatenate([cache_k[l], k_new], axis=1)[:, -buf:][None]
    new_v_sample = jnp.concatenate([cache_v[l], v_new], axis=1)[:, -buf:][None]
    us = u[n_prompt:].reshape(dec_batch, t_new, cdim)
    new_conv_sample = jnp.concatenate([state_conv[l], us], axis=1)[:, -(CONV_WIDTH - 1):][None]
    return (y_prompt, y_sample, new_k_prompt, new_v_prompt, new_conv_prompt,
            new_k_sample, new_v_sample, new_conv_sample)
```

```python
import functools
import math

import jax
import jax.numpy as jnp
from jax import lax
from jax.experimental import pallas as pl
from jax.experimental.pallas import tpu as pltpu

F32 = jnp.float32
BF16 = jnp.bfloat16

CHUNK = 64
HEAD_DIM = 64
N_Q_HEADS = 16
N_KV_HEADS = 4
Q_PER_KV = N_Q_HEADS // N_KV_HEADS
ATTN_DIM = N_Q_HEADS * HEAD_DIM
KV_DIM = N_KV_HEADS * HEAD_DIM
WINDOW = 128
WIN_CHUNKS = WINDOW // CHUNK
BAND = (WIN_CHUNKS + 1) * CHUNK
CONV_WIDTH = 31
NUM_BUCKETS = 32
MAX_DISTANCE = 128
N_GROUPS = 4
EXPERTS_PER_GROUP = 8
N_EXPERTS = N_GROUPS * EXPERTS_PER_GROUP
MOE_BLOCK = 256
RMS_EPS = 1e-6
LN_EPS = 1e-5
NEG_INF = -1e30

GROUP_ROWS = 32
LANES = 128
SUBLANES = 8
CONV_CHUNK = 256
ROUTER_ROWS = 64
EXPERT_ROW0 = 8
V7X_VMEM_LIMIT = 56 * 1024 * 1024


def _cparams(*sem):
    return pltpu.CompilerParams(dimension_semantics=sem, vmem_limit_bytes=V7X_VMEM_LIMIT)


def _ada_kernel(c_ref, w_ref, b_ref, o_ref):
    c = c_ref[...]
    s = c * jax.nn.sigmoid(c)
    o_ref[...] = jnp.dot(s.astype(BF16), w_ref[...].astype(BF16),
                         preferred_element_type=F32) + b_ref[...]


def _ada(c_all, w_ada, b_ada):
    rows, d = c_all.shape
    ncol = w_ada.shape[1]
    tn = 1024
    return pl.pallas_call(
        _ada_kernel,
        grid=(ncol // tn,),
        in_specs=[pl.BlockSpec((rows, d), lambda j: (0, 0)),
                  pl.BlockSpec((d, tn), lambda j: (0, j)),
                  pl.BlockSpec((1, tn), lambda j: (0, j))],
        out_specs=pl.BlockSpec((rows, tn), lambda j: (0, j)),
        out_shape=jax.ShapeDtypeStruct((rows, ncol), F32),
        compiler_params=_cparams("arbitrary"),
        name="ada",
    )(c_all, w_ada, b_ada.reshape(1, ncol))


def _rms_mod(x, gain, scale, shift):
    ms = jnp.mean(x * x, axis=-1, keepdims=True)
    y = x * lax.rsqrt(ms + RMS_EPS) * gain
    return y * (1.0 + scale) + shift


def _pack_bf16_pairs(xb):
    w = xb.shape[1] // 2
    bits = lax.bitcast_convert_type(xb.astype(F32), jnp.uint32)
    return (bits[:, :w] >> 16) | (bits[:, w:] & jnp.uint32(0xFFFF0000))


def _unpack_bf16_pairs(p):
    lo = lax.bitcast_convert_type(p << 16, F32)
    hi = lax.bitcast_convert_type(p & jnp.uint32(0xFFFF0000), F32)
    return lo, hi


def _two_stream_specs(tm, width, prompt_tiles):
    return [pl.BlockSpec((tm, width), lambda i: (jnp.minimum(i, prompt_tiles - 1), 0)),
            pl.BlockSpec((tm, width), lambda i: (jnp.maximum(i - prompt_tiles, 0), 0))]


def _norm1_kernel(xp_ref, xs_ref, g_ref, sc_ref, sh_ref, h_ref, *, groups, prompt_tiles):
    def run(x_ref):
        def body(gi, carry):
            r = pl.multiple_of(gi * GROUP_ROWS, GROUP_ROWS)
            h = _rms_mod(x_ref[pl.ds(r, GROUP_ROWS), :], g_ref[...],
                         sc_ref[pl.ds(gi, 1), :], sh_ref[pl.ds(gi, 1), :])
            h_ref[pl.ds(r, GROUP_ROWS), :] = h.astype(h_ref.dtype)
            return carry

        lax.fori_loop(0, groups, body, 0)

    i = pl.program_id(0)
    pl.when(i < prompt_tiles)(lambda: run(xp_ref))
    pl.when(i >= prompt_tiles)(lambda: run(xs_ref))


def _norm1(x_p, x_s, gain, scale_g, shift_g, tm):
    d = x_p.shape[1]
    n = x_p.shape[0] + x_s.shape[0]
    groups = tm // GROUP_ROWS
    prompt_tiles = x_p.shape[0] // tm
    return pl.pallas_call(
        functools.partial(_norm1_kernel, groups=groups, prompt_tiles=prompt_tiles),
        grid=(n // tm,),
        in_specs=_two_stream_specs(tm, d, prompt_tiles) + [
            pl.BlockSpec((1, d), lambda i: (0, 0)),
            pl.BlockSpec((groups, d), lambda i: (i, 0)),
            pl.BlockSpec((groups, d), lambda i: (i, 0))],
        out_specs=pl.BlockSpec((tm, d), lambda i: (i, 0)),
        out_shape=jax.ShapeDtypeStruct((n, d), BF16),
        compiler_params=_cparams("arbitrary"),
        name="norm1",
    )(x_p, x_s, gain.reshape(1, d), scale_g, shift_g)


def _proj_kernel(h_ref, w_ref, *o_refs, mode):
    acc = jnp.dot(h_ref[...], w_ref[...], preferred_element_type=F32)
    if mode == "qkv":
        o_refs[0][...] = acc.astype(BF16)
    elif mode == "glu":
        half = acc.shape[1] // 2
        o_refs[0][...] = acc[:, :half] * jax.nn.sigmoid(acc[:, half:])
    else:
        o_refs[0][...] = jax.nn.sigmoid(acc).astype(BF16)


def _proj(h, w, mode, tm, tnw, out_dtypes):
    n, d = h.shape
    ncol = w.shape[1]
    tno = tnw // 2 if mode == "glu" else tnw
    nout = ncol // 2 if mode == "glu" else ncol
    outs = pl.pallas_call(
        functools.partial(_proj_kernel, mode=mode),
        grid=(n // tm, ncol // tnw),
        in_specs=[pl.BlockSpec((tm, d), lambda i, j: (i, 0)),
                  pl.BlockSpec((d, tnw), lambda i, j: (0, j))],
        out_specs=[pl.BlockSpec((tm, tno), lambda i, j: (i, j)) for _ in out_dtypes],
        out_shape=[jax.ShapeDtypeStruct((n, nout), dt) for dt in out_dtypes],
        compiler_params=_cparams("arbitrary", "arbitrary"),
        name="proj_" + mode,
    )(h, w)
    return outs


def _kv_state_kernel(h_ref, w_ref, o_ref):
    o_ref[...] = jnp.dot(h_ref[...], w_ref[...], preferred_element_type=F32)


def _kv_state(h, w_kv, n_prompt, seq):
    n, d = h.shape
    batch = n_prompt // seq
    per_seq = seq // WINDOW
    steps = batch + (n - n_prompt) // WINDOW

    def row_block(i):
        return jnp.where(i < batch, (i + 1) * per_seq - 1, n_prompt // WINDOW + i - batch)

    return pl.pallas_call(
        _kv_state_kernel,
        grid=(steps,),
        in_specs=[pl.BlockSpec((WINDOW, d), lambda i: (row_block(i), 0)),
                  pl.BlockSpec((d, 2 * KV_DIM), lambda i: (0, 0))],
        out_specs=pl.BlockSpec((WINDOW, 2 * KV_DIM), lambda i: (i, 0)),
        out_shape=jax.ShapeDtypeStruct((steps * WINDOW, 2 * KV_DIM), F32),
        compiler_params=_cparams("arbitrary"),
        name="kv_state",
    )(h, w_kv)


def _t5_bucket(rel):
    nb = NUM_BUCKETS // 2
    n = -rel
    ret = jnp.where(n < 0, nb, 0)
    n = jnp.abs(n)
    max_exact = nb // 2
    nf = jnp.maximum(n, 1).astype(F32)
    large = max_exact + (jnp.log(nf / max_exact) / math.log(MAX_DISTANCE / max_exact)
                         * (nb - max_exact)).astype(jnp.int32)
    large = jnp.minimum(large, nb - 1)
    return ret + jnp.where(n < max_exact, n, large)


def _bias_kernel(tbl_ref, bkt_ref, o_ref):
    bkt = bkt_ref[...]
    nq = bkt.shape[0]
    for k in range(N_KV_HEADS):
        for g in range(Q_PER_KV):
            acc = jnp.zeros(bkt.shape, F32)
            for b in range(NUM_BUCKETS):
                acc = jnp.where(bkt == b, tbl_ref[b, k * Q_PER_KV + g], acc)
            o_ref[k, g * nq:(g + 1) * nq, :] = acc


def _rel_bias(rel_table, rel):
    nq, nk = rel.shape
    return pl.pallas_call(
        _bias_kernel,
        in_specs=[pl.BlockSpec(memory_space=pltpu.SMEM),
                  pl.BlockSpec((nq, nk), lambda: (0, 0))],
        out_specs=pl.BlockSpec((N_KV_HEADS, Q_PER_KV * nq, nk), lambda: (0, 0, 0)),
        out_shape=jax.ShapeDtypeStruct((N_KV_HEADS, Q_PER_KV * nq, nk), F32),
        name="rel_bias",
    )(rel_table.astype(F32), _t5_bucket(rel))


PAIR_BAND = 2 * CHUNK + WINDOW
HEADS_PER_COL = LANES // HEAD_DIM
COLS_PER_KV = Q_PER_KV // HEADS_PER_COL


def _pair_bias_kernel(tbl_ref, bkt_ref, o_ref):
    for e in range(2):
        bkt = bkt_ref[e]
        for k in range(N_KV_HEADS):
            for col in range(COLS_PER_KV):
                for half in range(HEADS_PER_COL):
                    head = k * Q_PER_KV + col * HEADS_PER_COL + half
                    acc = jnp.full(bkt.shape, NEG_INF, F32)
                    for b in range(NUM_BUCKETS):
                        acc = jnp.where(bkt == b, tbl_ref[b, head], acc)
                    r0 = (e * COLS_PER_KV + col) * CHUNK
                    o_ref[k, r0:r0 + CHUNK, half * PAIR_BAND:(half + 1) * PAIR_BAND] = acc


def _pair_bias(rel_table):
    qi = jnp.arange(CHUNK, dtype=jnp.int32)[None, :, None]
    kj = jnp.arange(PAIR_BAND, dtype=jnp.int32)[None, None, :]
    e = jnp.arange(2, dtype=jnp.int32)[:, None, None]
    rel = (kj - WINDOW) - (e * CHUNK + qi)
    key_chunk = kj // CHUNK - e
    seen = (key_chunk >= 0) & (key_chunk <= WIN_CHUNKS)
    bkt = jnp.where(seen, _t5_bucket(rel), -1)
    rows = 2 * COLS_PER_KV * CHUNK
    return pl.pallas_call(
        _pair_bias_kernel,
        in_specs=[pl.BlockSpec(memory_space=pltpu.SMEM),
                  pl.BlockSpec((2, CHUNK, PAIR_BAND), lambda: (0, 0, 0))],
        out_specs=pl.BlockSpec((N_KV_HEADS, rows, HEADS_PER_COL * PAIR_BAND), lambda: (0, 0, 0)),
        out_shape=jax.ShapeDtypeStruct((N_KV_HEADS, rows, HEADS_PER_COL * PAIR_BAND), F32),
        name="pair_bias",
    )(rel_table.astype(F32), bkt)


def _attend(qc, kk_all, vv_all, bias_ref, sink_ref, mask_thr):
    nq = qc.shape[0]
    nk = kk_all.shape[0]
    pieces = []
    for k in range(N_KV_HEADS):
        qs = jnp.concatenate(
            [qc[:, (k * Q_PER_KV + g) * HEAD_DIM:(k * Q_PER_KV + g + 1) * HEAD_DIM]
             for g in range(Q_PER_KV)], axis=0)
        kk = kk_all[:, k * HEAD_DIM:(k + 1) * HEAD_DIM]
        vv = vv_all[:, k * HEAD_DIM:(k + 1) * HEAD_DIM]
        lg = lax.dot_general(qs, kk, (((1,), (1,)), ((), ())),
                             preferred_element_type=F32) * (HEAD_DIM ** -0.5) + bias_ref[k]
        if mask_thr is not None:
            col = lax.broadcasted_iota(jnp.int32, (Q_PER_KV * nq, nk), 1)
            lg = jnp.where(col < mask_thr, NEG_INF, lg)
        s = sink_ref[k]
        m = jnp.maximum(jnp.max(lg, axis=-1, keepdims=True), s)
        p = jnp.exp(lg - m)
        den = jnp.sum(p, axis=-1, keepdims=True) + jnp.exp(s - m)
        o = jnp.dot(p.astype(BF16), vv, preferred_element_type=F32) / den
        pieces.extend(o[g * nq:(g + 1) * nq, :] for g in range(Q_PER_KV))
    return jnp.concatenate(pieces, axis=1)


def _attn_prompt_kernel(q_ref, kv_ref, halo_ref, bias_ref, sink_ref, o_ref, ka, kb, va, vb, *, pairs):
    t = pl.program_id(1)
    kvcat = jnp.concatenate([halo_ref[...], kv_ref[...]], axis=0)
    low = lax.broadcasted_iota(jnp.int32, (kvcat.shape[0], LANES), 1) < HEAD_DIM
    zero = jnp.zeros((kvcat.shape[0], LANES), BF16)
    ones_low = jnp.where(low, 1.0, 0.0).astype(BF16)
    ones_high = jnp.where(low, 0.0, 1.0).astype(BF16)
    kv_cols = KV_DIM // LANES
    for col in range(2 * kv_cols):
        x = kvcat[:, col * LANES:(col + 1) * LANES]
        xs = jnp.concatenate([x[:, HEAD_DIM:], x[:, :HEAD_DIM]], axis=1)
        dst_a, dst_b = (ka, kb) if col < kv_cols else (va, vb)
        k0 = (col % kv_cols) * HEADS_PER_COL
        dst_a[k0, :, 0:LANES] = jnp.where(low, x, zero)
        dst_b[k0, :, 0:LANES] = jnp.where(low, zero, xs)
        dst_a[k0 + 1, :, 0:LANES] = jnp.where(low, xs, zero)
        dst_b[k0 + 1, :, 0:LANES] = jnp.where(low, zero, x)
    for k in range(N_KV_HEADS):
        va[k, :, LANES:2 * LANES] = ones_low
        vb[k, :, LANES:2 * LANES] = ones_high

    out_low = lax.broadcasted_iota(jnp.int32, (2 * COLS_PER_KV * CHUNK, LANES), 1) < HEAD_DIM
    for p in range(pairs):
        r0 = p * 2 * CHUNK
        for k in range(N_KV_HEADS):
            lhs = jnp.concatenate(
                [q_ref[r0 + e * CHUNK:r0 + (e + 1) * CHUNK, (k * COLS_PER_KV + c) * LANES:(k * COLS_PER_KV + c + 1) * LANES]
                 for e in range(2) for c in range(COLS_PER_KV)], axis=0)
            keys = jnp.concatenate([ka[k, r0:r0 + PAIR_BAND, :], kb[k, r0:r0 + PAIR_BAND, :]], axis=0)
            lg = lax.dot_general(lhs, keys, (((1,), (1,)), ((), ())),
                                 preferred_element_type=F32) * (HEAD_DIM ** -0.5) + bias_ref[k]
            probs, sink_terms = [], []
            for half in range(HEADS_PER_COL):
                seg = lg[:, half * PAIR_BAND:(half + 1) * PAIR_BAND]
                if p == 0:
                    kcol = lax.broadcasted_iota(jnp.int32, seg.shape, 1)
                    seg = jnp.where(kcol < jnp.where(t == 0, WINDOW, 0), NEG_INF, seg)
                s = sink_ref[k, half]
                folded = seg[:, :LANES]
                for j in range(1, PAIR_BAND // LANES):
                    folded = jnp.maximum(folded, seg[:, j * LANES:(j + 1) * LANES])
                m = jnp.maximum(jnp.max(folded, axis=-1, keepdims=True), s)
                pr = jnp.exp(seg - jnp.concatenate([m] * (PAIR_BAND // LANES), axis=1))
                sink_terms.append(jnp.exp(s - m))
                probs.append(pr.astype(BF16))
            vals = jnp.concatenate([va[k, r0:r0 + PAIR_BAND, :], vb[k, r0:r0 + PAIR_BAND, :]], axis=0)
            oe = jnp.dot(jnp.concatenate(probs, axis=1), vals, preferred_element_type=F32)
            o = oe[:, :LANES] / (oe[:, LANES:] + jnp.where(out_low, sink_terms[0], sink_terms[1]))
            for e in range(2):
                for c in range(COLS_PER_KV):
                    rr = (e * COLS_PER_KV + c) * CHUNK
                    o_ref[r0 + e * CHUNK:r0 + (e + 1) * CHUNK,
                          (k * COLS_PER_KV + c) * LANES:(k * COLS_PER_KV + c + 1) * LANES] = (
                        o[rr:rr + CHUNK, :].astype(o_ref.dtype))


def _attn_prompt(q, kv, bias, sink_rows, n_prompt, seq, pairs):
    rows = pairs * 2 * CHUNK
    tiles = seq // rows
    halo_per_tile = rows // WINDOW
    keys = WINDOW + rows
    return pl.pallas_call(
        functools.partial(_attn_prompt_kernel, pairs=pairs),
        grid=(n_prompt // seq, tiles),
        scratch_shapes=[pltpu.VMEM((N_KV_HEADS, keys, LANES), BF16) for _ in range(2)]
        + [pltpu.VMEM((N_KV_HEADS, keys, 2 * LANES), BF16) for _ in range(2)],
        in_specs=[pl.BlockSpec((rows, ATTN_DIM), lambda b, t: (b * tiles + t, 0)),
                  pl.BlockSpec((rows, 2 * KV_DIM), lambda b, t: (b * tiles + t, 0)),
                  pl.BlockSpec((WINDOW, 2 * KV_DIM),
                               lambda b, t: (jnp.maximum((b * tiles + t) * halo_per_tile - 1, 0), 0)),
                  pl.BlockSpec(bias.shape, lambda b, t: (0, 0, 0)),
                  pl.BlockSpec(sink_rows.shape, lambda b, t: (0, 0, 0, 0))],
        out_specs=pl.BlockSpec((rows, ATTN_DIM), lambda b, t: (b * tiles + t, 0)),
        out_shape=jax.ShapeDtypeStruct((n_prompt, ATTN_DIM), BF16),
        compiler_params=_cparams("arbitrary", "arbitrary"),
        name="attn_prompt",
    )(q, kv, kv, bias, sink_rows)


def _attn_sample_kernel(q_ref, kv_ref, ck_ref, cv_ref, bias_ref, sink_ref, o_ref):
    kv = kv_ref[...]
    kk = jnp.concatenate([ck_ref[0].astype(BF16), kv[:, :KV_DIM]], axis=0)
    vv = jnp.concatenate([cv_ref[0].astype(BF16), kv[:, KV_DIM:]], axis=0)
    o = _attend(q_ref[...], kk, vv, bias_ref, sink_ref, None)
    o_ref[...] = o.astype(o_ref.dtype)


def _attn_sample(q, kv, cache_k, cache_v, bias, sink_rows, n_prompt, t_new):
    dec_batch, buf, _ = cache_k.shape
    first = n_prompt // t_new
    return pl.pallas_call(
        _attn_sample_kernel,
        grid=(dec_batch,),
        in_specs=[pl.BlockSpec((t_new, ATTN_DIM), lambda s: (first + s, 0)),
                  pl.BlockSpec((t_new, 2 * KV_DIM), lambda s: (first + s, 0)),
                  pl.BlockSpec((1, buf, KV_DIM), lambda s: (s, 0, 0)),
                  pl.BlockSpec((1, buf, KV_DIM), lambda s: (s, 0, 0)),
                  pl.BlockSpec(bias.shape, lambda s: (0, 0, 0)),
                  pl.BlockSpec(sink_rows.shape, lambda s: (0, 0, 0))],
        out_specs=pl.BlockSpec((t_new, ATTN_DIM), lambda s: (s, 0)),
        out_shape=jax.ShapeDtypeStruct((dec_batch * t_new, ATTN_DIM), BF16),
        compiler_params=_cparams("arbitrary"),
        name="attn_sample",
    )(q, kv, cache_k, cache_v, bias, sink_rows)


def _route(lt, eidx_ref, wtok_ref, rank_ref, cnt_ref, cnt_scr):
    gl = [lt[r:r + 1, :] for r in range(N_GROUPS)]
    gmax = gl[0]
    gsel = jnp.zeros(gl[0].shape, jnp.int32)
    for r in range(1, N_GROUPS):
        better = gl[r] > gmax
        gsel = jnp.where(better, r, gsel)
        gmax = jnp.maximum(gmax, gl[r])
    gexp = [jnp.exp(v - gmax) for v in gl]
    gsum = gexp[0]
    for r in range(1, N_GROUPS):
        gsum = gsum + gexp[r]
    psel = jnp.zeros(gl[0].shape, F32)
    for r in range(N_GROUPS):
        psel = jnp.where(gsel == r, gexp[r] / gsum, psel)
    el = jnp.zeros((EXPERTS_PER_GROUP, lt.shape[1]), F32)
    for r in range(N_GROUPS):
        lo = EXPERT_ROW0 + r * EXPERTS_PER_GROUP
        el = jnp.where(gsel == r, lt[lo:lo + EXPERTS_PER_GROUP, :], el)
    emax = jnp.max(el, axis=0, keepdims=True)
    ee = jnp.exp(el - emax)
    pin = ee / jnp.sum(ee, axis=0, keepdims=True)
    idx = lax.broadcasted_iota(jnp.int32, pin.shape, 0)
    p1 = jnp.max(pin, axis=0, keepdims=True)
    i1 = jnp.min(jnp.where(pin == p1, idx, EXPERTS_PER_GROUP), axis=0, keepdims=True)
    rest = jnp.where(idx == i1, -1.0, pin)
    p2 = jnp.max(rest, axis=0, keepdims=True)
    i2 = jnp.min(jnp.where(rest == p2, idx, EXPERTS_PER_GROUP), axis=0, keepdims=True)
    tot = p1 + p2
    e1 = gsel * EXPERTS_PER_GROUP + i1
    e2 = gsel * EXPERTS_PER_GROUP + i2
    eidx_ref[0:1, :] = e1
    eidx_ref[1:2, :] = e2
    wtok_ref[0:1, :] = psel * p1 / tot
    wtok_ref[1:2, :] = psel * p2 / tot

    t = lt.shape[1]
    eiota = lax.broadcasted_iota(jnp.int32, (N_EXPERTS, t), 0)
    oh1 = (eiota == e1).astype(F32)
    oh2 = (eiota == e2).astype(F32)
    both = oh1 + oh2
    before = (lax.broadcasted_iota(jnp.int32, (t, t), 0)
              < lax.broadcasted_iota(jnp.int32, (t, t), 1)).astype(BF16)
    prior = jnp.dot(both.astype(BF16), before, preferred_element_type=F32) + cnt_scr[:, 0:1]
    rank_ref[0:1, :] = jnp.sum(oh1 * prior, axis=0, keepdims=True).astype(jnp.int32)
    rank_ref[1:2, :] = jnp.sum(oh2 * prior, axis=0, keepdims=True).astype(jnp.int32)
    cnt_scr[...] = cnt_scr[...] + jnp.sum(both, axis=1, keepdims=True)
    cnt_ref[...] = cnt_scr[...]


def _post_kernel(xp_ref, xs_ref, u_ref, uhalo_ref, sconv_ref, op_ref, os_ref, gates_ref,
                 gate1_ref, shift2_ref, scale2_ref,
                 cw_ref, cb_ref, lng_ref, lnb_ref, wpw2_ref, bpw2_ref, wao_ref, wout_ref,
                 n2_ref, wrt_ref, brt_ref,
                 x1_ref, h2_ref, eidx_ref, wtok_ref, rank_ref, cnt_ref,
                 uext, shift_scr, d_scr, s_scr, o_scr, mix_scr, h2b_scr, cnt_scr,
                 *, groups, prompt_tiles, tiles_per_seq):
    i = pl.program_id(0)
    d_model = xp_ref.shape[1]

    @pl.when(i == 0)
    def _():
        cnt_scr[...] = jnp.zeros(cnt_scr.shape, F32)
    nch = cw_ref.shape[0]
    cw = uext.shape[2]
    hist = uext.shape[1] - GROUP_ROWS
    lead = hist - (CONV_WIDTH - 1)

    def put_hist(g, rows):
        for c in range(nch):
            uext[g * nch + c, 0:hist, :] = rows[:, c * cw:(c + 1) * cw]

    @pl.when(i < prompt_tiles)
    def _():
        first = (i % tiles_per_seq) == 0
        put_hist(0, jnp.where(first, 0.0, uhalo_ref[...]))
        for g in range(1, groups):
            put_hist(g, u_ref[(g - 1) * GROUP_ROWS:g * GROUP_ROWS, :])
        o_scr[...] = op_ref[...]

    @pl.when(i >= prompt_tiles)
    def _():
        for g in range(groups):
            put_hist(g, sconv_ref[g])
        o_scr[...] = os_ref[...]

    for g in range(groups):
        for c in range(nch):
            uext[g * nch + c, hist:hist + GROUP_ROWS, :] = (
                u_ref[g * GROUP_ROWS:(g + 1) * GROUP_ROWS, c * cw:(c + 1) * cw])

    span = shift_scr.shape[1]

    def conv_chunk(k, carry):
        c = k % nch
        win_all = uext[k]
        for s in range(1, SUBLANES):
            shift_scr[s, :, :] = pltpu.roll(win_all, win_all.shape[0] - s, 0)[0:span, :]
        acc = None
        for j in range(CONV_WIDTH):
            base, s = divmod(lead + j, SUBLANES)
            rows = pl.ds(base * SUBLANES, GROUP_ROWS)
            win = uext[k, rows, :] if s == 0 else shift_scr[s, rows, :]
            term = win * cw_ref[c, j:j + 1, :]
            acc = term if acc is None else acc + term
        d_scr[k] = acc + cb_ref[c]
        return carry

    lax.fori_loop(0, groups * nch, conv_chunk, 0)

    for g in range(groups):
        dd = jnp.concatenate([d_scr[g * nch + c] for c in range(nch)], axis=1)
        mu = jnp.mean(dd, axis=-1, keepdims=True)
        var = jnp.mean(jnp.square(dd - mu), axis=-1, keepdims=True)
        y = (dd - mu) * lax.rsqrt(var + LN_EPS) * lng_ref[...] + lnb_ref[...]
        s_scr[g * GROUP_ROWS:(g + 1) * GROUP_ROWS, :] = (y * jax.nn.sigmoid(y)).astype(BF16)

    conv_out = jnp.dot(s_scr[...], wpw2_ref[...], preferred_element_type=F32) + bpw2_ref[...]
    attn_out = jnp.dot(o_scr[...], wao_ref[...], preferred_element_type=F32)
    merged = (gates_ref[:, :d_model].astype(F32) * attn_out
              + gates_ref[:, d_model:].astype(F32) * conv_out)
    mix_scr[...] = jnp.dot(merged.astype(BF16), wout_ref[...], preferred_element_type=F32)

    def residual(x_ref):
        def res_group(g, carry):
            r = pl.multiple_of(g * GROUP_ROWS, GROUP_ROWS)
            x1 = (x_ref[pl.ds(r, GROUP_ROWS), :]
                  + gate1_ref[pl.ds(g, 1), :] * mix_scr[pl.ds(r, GROUP_ROWS), :])
            x1_ref[pl.ds(r, GROUP_ROWS), :] = x1
            h = _rms_mod(x1, n2_ref[...], scale2_ref[pl.ds(g, 1), :], shift2_ref[pl.ds(g, 1), :])
            hb = h.astype(BF16)
            h2b_scr[pl.ds(r, GROUP_ROWS), :] = hb
            h2_ref[pl.ds(r, GROUP_ROWS), :] = _pack_bf16_pairs(hb)
            return carry

        lax.fori_loop(0, groups, res_group, 0)

    pl.when(i < prompt_tiles)(lambda: residual(xp_ref))
    pl.when(i >= prompt_tiles)(lambda: residual(xs_ref))

    lt = lax.dot_general(wrt_ref[...], h2b_scr[...], (((1,), (1,)), ((), ())),
                         preferred_element_type=F32) + brt_ref[...]
    _route(lt, eidx_ref, wtok_ref, rank_ref, cnt_ref, cnt_scr)


def _post(x_p, x_s, u, sconv_pad, o_p, o_s, gates, gate1_g, shift2_g, scale2_g, p, n_prompt, seq, tm):
    d = x_p.shape[1]
    n = x_p.shape[0] + x_s.shape[0]
    cdim = u.shape[1]
    nch = cdim // CONV_CHUNK
    groups = tm // GROUP_ROWS
    prompt_tiles = n_prompt // tm
    const = lambda shape: pl.BlockSpec(shape, lambda i: (0,) * len(shape))
    row = lambda w: pl.BlockSpec((tm, w), lambda i: (i, 0))
    grp = pl.BlockSpec((groups, d), lambda i: (i, 0))
    return pl.pallas_call(
        functools.partial(_post_kernel, groups=groups, prompt_tiles=prompt_tiles,
                          tiles_per_seq=seq // tm),
        grid=(n // tm,),
        in_specs=_two_stream_specs(tm, d, prompt_tiles) + [
                  row(cdim),
                  pl.BlockSpec((GROUP_ROWS, cdim), lambda i: (jnp.maximum(i * groups - 1, 0), 0)),
                  pl.BlockSpec((groups, GROUP_ROWS, cdim),
                               lambda i: (jnp.maximum(i - prompt_tiles, 0), 0, 0))]
                 + _two_stream_specs(tm, ATTN_DIM, prompt_tiles) + [
                  row(2 * d), grp, grp, grp,
                  const((nch, CONV_WIDTH, CONV_CHUNK)), const((nch, 1, CONV_CHUNK)),
                  const((1, cdim)), const((1, cdim)),
                  const((cdim, d)), const((1, d)), const((ATTN_DIM, d)), const((d, d)),
                  const((1, d)), const((ROUTER_ROWS, d)), const((ROUTER_ROWS, 1))],
        out_specs=[row(d),
                   row(d // 2),
                   pl.BlockSpec((2, tm), lambda i: (0, i)),
                   pl.BlockSpec((2, tm), lambda i: (0, i)),
                   pl.BlockSpec((2, tm), lambda i: (0, i)),
                   pl.BlockSpec((N_EXPERTS, LANES), lambda i: (0, 0))],
        out_shape=[jax.ShapeDtypeStruct((n, d), F32),
                   jax.ShapeDtypeStruct((n, d // 2), jnp.uint32),
                   jax.ShapeDtypeStruct((2, n), jnp.int32),
                   jax.ShapeDtypeStruct((2, n), F32),
                   jax.ShapeDtypeStruct((2, n), jnp.int32),
                   jax.ShapeDtypeStruct((N_EXPERTS, LANES), F32)],
        scratch_shapes=[pltpu.VMEM((groups * nch, 2 * GROUP_ROWS, CONV_CHUNK), F32),
                        pltpu.VMEM((SUBLANES, 2 * GROUP_ROWS - SUBLANES, CONV_CHUNK), F32),
                        pltpu.VMEM((groups * nch, GROUP_ROWS, CONV_CHUNK), F32),
                        pltpu.VMEM((tm, cdim), BF16),
                        pltpu.VMEM((tm, ATTN_DIM), BF16),
                        pltpu.VMEM((tm, d), F32),
                        pltpu.VMEM((tm, d), BF16),
                        pltpu.VMEM((N_EXPERTS, LANES), F32)],
        compiler_params=_cparams("arbitrary"),
        name="post",
    )(x_p, x_s, u, u, sconv_pad, o_p, o_s, gates, gate1_g, shift2_g, scale2_g,
      p["conv_w"], p["conv_b"], p["ln_g"], p["ln_b"], p["w_pw2"], p["b_pw2"], p["w_ao"], p["w_out"],
      p["n2"], p["w_rt"], p["b_rt"])


TABLE_GROUP = 8
GATHER_AHEAD = 4


def _grouped_tables(tab, group=TABLE_GROUP, ahead=1):
    steps, w = tab.shape
    assert steps % group == 0
    padded = jnp.concatenate([tab] + [tab[-1:]] * ahead, axis=0)
    cols = [tab.reshape(steps // group, group * w)] + [padded[group + a::group][:steps // group]
                                                         for a in range(ahead)]
    return jnp.concatenate(cols, axis=1).reshape(steps // group, 1, (group + ahead) * w)

def _moe_kernel(blk_e_ref, nused_ref, src_ref, roww_ref,
                h2_hbm, wg_ref, wu_ref, wd_ref, y_ref,
                xbuf, x16, wg_b, wu_b, wd_b, gsem):
    i = pl.program_id(0)
    nused = nused_ref[0]
    nslots = GATHER_AHEAD + 1
    slot = i % nslots
    base = (i % TABLE_GROUP) * MOE_BLOCK

    def gather(first, sl):
        for r in range(MOE_BLOCK):
            pltpu.make_async_copy(h2_hbm.at[pl.ds(src_ref[0, 0, first + r], 1), :],
                                  xbuf.at[sl, pl.ds(r, 1), :], gsem.at[sl]).start(priority=r % 2)

    def gather_wait(sl):
        pltpu.make_async_copy(h2_hbm.at[pl.ds(0, MOE_BLOCK), :], xbuf.at[sl], gsem.at[sl]).wait()

    @pl.when(i == 0)
    def _():
        for a in range(GATHER_AHEAD):
            gather(base + a * MOE_BLOCK, a)

    @pl.when(i >= nused)
    def _():
        y_ref[...] = jnp.zeros(y_ref.shape, y_ref.dtype)

    @pl.when(i < nused)
    def _():
        changed = jnp.logical_or(i == 0, blk_e_ref[i] != blk_e_ref[jnp.maximum(i - 1, 0)])

        @pl.when(changed)
        def _():
            wg_b[...] = wg_ref[0].astype(BF16)
            wu_b[...] = wu_ref[0].astype(BF16)
            wd_b[...] = wd_ref[0].astype(BF16)

        gather_wait(slot)
        x_lo, x_hi = _unpack_bf16_pairs(xbuf[slot])
        x16[:, :x_lo.shape[1]] = x_lo.astype(BF16)
        x16[:, x_lo.shape[1]:] = x_hi.astype(BF16)
        gather(base + GATHER_AHEAD * MOE_BLOCK, (i + GATHER_AHEAD) % nslots)
        x = x16[...]
        hg = jnp.dot(x, wg_b[...], preferred_element_type=F32)
        hu = jnp.dot(x, wu_b[...], preferred_element_type=F32)
        hid = (hg * jax.nn.sigmoid(hg) * hu).astype(BF16)
        y = jnp.dot(hid, wd_b[...], preferred_element_type=F32) * roww_ref[...]
        y_ref[...] = _pack_bf16_pairs(y.astype(BF16))

        @pl.when(i == nused - 1)
        def _():
            for a in range(1, GATHER_AHEAD + 1):
                gather_wait((i + a) % nslots)


def _moe(h2, blk_e, nused, src, roww, w_gate, w_up, w_down):
    n_blocks = blk_e.shape[0]
    n_exp, d, de = w_gate.shape
    src_tables = _grouped_tables(src.reshape(n_blocks, MOE_BLOCK), ahead=GATHER_AHEAD)
    grid_spec = pltpu.PrefetchScalarGridSpec(
        num_scalar_prefetch=2,
        grid=(n_blocks,),
        in_specs=[pl.BlockSpec((1, 1, src_tables.shape[2]), lambda i, be, nu: (i // TABLE_GROUP, 0, 0),
                               memory_space=pltpu.SMEM),
                  pl.BlockSpec((MOE_BLOCK, 1), lambda i, be, nu: (i, 0)),
                  pl.BlockSpec(memory_space=pl.ANY),
                  pl.BlockSpec((1, d, de), lambda i, be, nu: (be[i], 0, 0)),
                  pl.BlockSpec((1, d, de), lambda i, be, nu: (be[i], 0, 0)),
                  pl.BlockSpec((1, de, d), lambda i, be, nu: (be[i], 0, 0))],
        out_specs=pl.BlockSpec((MOE_BLOCK, d // 2), lambda i, be, nu: (i, 0)),
        scratch_shapes=[pltpu.VMEM((GATHER_AHEAD + 1, MOE_BLOCK, d // 2), jnp.uint32),
                        pltpu.VMEM((MOE_BLOCK, d), BF16),
                        pltpu.VMEM((d, de), BF16), pltpu.VMEM((d, de), BF16), pltpu.VMEM((de, d), BF16),
                        pltpu.SemaphoreType.DMA((GATHER_AHEAD + 1,))],
    )
    return pl.pallas_call(
        _moe_kernel,
        grid_spec=grid_spec,
        out_shape=jax.ShapeDtypeStruct((n_blocks * MOE_BLOCK, d // 2), jnp.uint32),
        compiler_params=_cparams("arbitrary"),
        name="moe",
    )(blk_e, nused, src_tables, roww, h2, w_gate, w_up, w_down)


def _dispatch(eidx, wtok, rank, counts, n):
    a_tot = 2 * n
    experts = jnp.arange(N_EXPERTS, dtype=jnp.int32)
    padded = (counts + MOE_BLOCK - 1) // MOE_BLOCK * MOE_BLOCK
    pad_end = jnp.sum(jnp.where(experts[None, :] <= experts[:, None], padded[None, :], 0), axis=1)
    pad_start = pad_end - padded
    dest = jnp.sum(jnp.where(eidx[:, :, None] == experts, pad_start, 0), axis=-1) + rank
    n_blocks = -(-(a_tot + N_EXPERTS * (MOE_BLOCK - 1)) // MOE_BLOCK)
    n_blocks = -(-n_blocks // TABLE_GROUP) * TABLE_GROUP
    n_rows = n_blocks * MOE_BLOCK
    tok = jnp.tile(jnp.arange(n, dtype=jnp.int32), 2)
    upd = jnp.stack([tok, lax.bitcast_convert_type(wtok.reshape(-1), jnp.int32)], axis=1)
    rows = jnp.zeros((n_rows, 2), jnp.int32).at[dest.reshape(-1)].set(upd, unique_indices=True)
    blk_start = jnp.arange(n_blocks, dtype=jnp.int32) * MOE_BLOCK
    blk_e = jnp.minimum(jnp.sum((pad_end[None, :] <= blk_start[:, None]).astype(jnp.int32), axis=1),
                        N_EXPERTS - 1)
    nused = (pad_end[-1] // MOE_BLOCK).reshape(1)
    roww = lax.bitcast_convert_type(rows[:, 1], F32)
    return blk_e, nused, rows[:, 0].reshape(n_blocks, 1, MOE_BLOCK), roww.reshape(n_rows, 1), dest


def _final_kernel(d0_ref, d1_ref, x1_ref, gate2_ref, fg_ref, yb_hbm,
                  op_ref, os_ref, ybuf, sem, *, groups, prompt_tiles, tiles, table_group):
    i = pl.program_id(0)
    slot = i % 2
    tm = x1_ref.shape[0]
    base = (i % table_group) * tm

    def gather(first, sl):
        for r in range(tm):
            pltpu.make_async_copy(yb_hbm.at[pl.ds(d0_ref[0, 0, first + r], 1), :],
                                  ybuf.at[sl, 0, pl.ds(r, 1), :], sem.at[sl]).start(priority=0)
            pltpu.make_async_copy(yb_hbm.at[pl.ds(d1_ref[0, 0, first + r], 1), :],
                                  ybuf.at[sl, 1, pl.ds(r, 1), :], sem.at[sl]).start(priority=1)

    def gather_wait(sl):
        for k in range(2):
            pltpu.make_async_copy(yb_hbm.at[pl.ds(0, tm), :], ybuf.at[sl, k], sem.at[sl]).wait()

    @pl.when(i == 0)
    def _():
        gather(base, 0)

    gather_wait(slot)
    gather(base + tm, 1 - slot)

    def run(o_ref):
        def body(g, carry):
            r = pl.multiple_of(g * GROUP_ROWS, GROUP_ROWS)
            lo0, hi0 = _unpack_bf16_pairs(ybuf[slot, 0, pl.ds(r, GROUP_ROWS), :])
            lo1, hi1 = _unpack_bf16_pairs(ybuf[slot, 1, pl.ds(r, GROUP_ROWS), :])
            moe = jnp.concatenate([lo0 + lo1, hi0 + hi1], axis=1)
            x2 = x1_ref[pl.ds(r, GROUP_ROWS), :] + gate2_ref[pl.ds(g, 1), :] * moe
            ms = jnp.mean(x2 * x2, axis=-1, keepdims=True)
            o_ref[pl.ds(r, GROUP_ROWS), :] = x2 * lax.rsqrt(ms + RMS_EPS) * fg_ref[...]
            return carry

        lax.fori_loop(0, groups, body, 0)

    pl.when(i < prompt_tiles)(lambda: run(op_ref))
    pl.when(i >= prompt_tiles)(lambda: run(os_ref))

    @pl.when(i == tiles - 1)
    def _():
        gather_wait(1 - slot)


def _final(x1, yb, dest, gate2_g, final_g, n_prompt, tm):
    n, d = x1.shape
    groups = tm // GROUP_ROWS
    tiles = n // tm
    prompt_tiles = n_prompt // tm
    table_group = max(g for g in range(1, TABLE_GROUP + 1) if tiles % g == 0)
    tables = [_grouped_tables(dest[k].reshape(tiles, tm), table_group) for k in range(2)]
    idx_spec = pl.BlockSpec((1, 1, tables[0].shape[2]), lambda i: (i // table_group, 0, 0),
                            memory_space=pltpu.SMEM)
    return pl.pallas_call(
        functools.partial(_final_kernel, groups=groups, prompt_tiles=prompt_tiles, tiles=tiles,
                          table_group=table_group),
        grid=(tiles,),
        in_specs=[idx_spec, idx_spec,
                  pl.BlockSpec((tm, d), lambda i: (i, 0)),
                  pl.BlockSpec((groups, d), lambda i: (i, 0)),
                  pl.BlockSpec((1, d), lambda i: (0, 0)),
                  pl.BlockSpec(memory_space=pl.ANY)],
        out_specs=_two_stream_specs(tm, d, prompt_tiles),
        out_shape=[jax.ShapeDtypeStruct((n_prompt, d), F32),
                   jax.ShapeDtypeStruct((n - n_prompt, d), F32)],
        scratch_shapes=[pltpu.VMEM((2, 2, tm, d // 2), jnp.uint32), pltpu.SemaphoreType.DMA((2,))],
        compiler_params=_cparams("arbitrary"),
        name="final",
    )(tables[0], tables[1], x1, gate2_g, final_g.reshape(1, d), yb)


def _layer(x_p, x_s, c_all, cache_k, cache_v, sconv, rel_table, lp, seq, t_new):
    (n1, n2, w_ada, b_ada, w_in, sink, w_ao, conv_w, conv_b, ln_g, ln_b, w_pw2, b_pw2, w_out,
     w_grp, b_grp, w_rt, b_rt, w_gate, w_up, w_down) = lp
    n_prompt, d = x_p.shape
    n = n_prompt + x_s.shape[0]
    batch = n_prompt // seq
    dec_batch = (n - n_prompt) // t_new
    cdim = conv_w.shape[1]
    tm = 256

    c_rows = -(-c_all.shape[0] // 8) * 8
    c_pad = jnp.pad(c_all, ((0, c_rows - c_all.shape[0]), (0, 0)))
    mods = _ada(c_pad, w_ada, b_ada)

    def per_group(k):
        m = mods[:, k * d:(k + 1) * d]
        mp = jnp.broadcast_to(m[:batch, None, :], (batch, seq // GROUP_ROWS, d))
        return jnp.concatenate([mp.reshape(batch * (seq // GROUP_ROWS), d), m[batch:batch + dec_batch]],
                               axis=0)

    shift1, scale1, gate1, shift2, scale2, gate2 = [per_group(k) for k in range(6)]

    h1 = _norm1(x_p, x_s, n1, scale1, shift1, 512)

    k_off = ATTN_DIM
    glu_off = k_off + 2 * KV_DIM
    gate_off = glu_off + 2 * cdim
    w_in_b = w_in.astype(BF16)
    tglu = 512
    w_glu = jnp.stack([w_in_b[:, glu_off:glu_off + cdim].reshape(d, cdim // tglu, tglu),
                       w_in_b[:, glu_off + cdim:gate_off].reshape(d, cdim // tglu, tglu)],
                      axis=2).reshape(d, 2 * cdim)
    tmm = 1536
    (q,) = _proj(h1, w_in_b[:, :k_off], "qkv", tmm, 1024, [BF16])
    (kv,) = _proj(h1, w_in_b[:, k_off:glu_off], "qkv", tmm, 512, [BF16])
    kv_state = _kv_state(h1, w_in_b[:, k_off:glu_off], n_prompt, seq)
    (u,) = _proj(h1, w_glu, "glu", tmm, 2 * tglu, [F32])
    (gates,) = _proj(h1, w_in_b[:, gate_off:], "gate", tmm, 1024, [BF16])

    bias_p = _pair_bias(rel_table)
    buf = cache_k.shape[1]
    kpos = jnp.concatenate([jnp.arange(buf, dtype=jnp.int32) - buf, jnp.arange(t_new, dtype=jnp.int32)])
    bias_s = _rel_bias(rel_table, kpos[None, :] - jnp.arange(t_new, dtype=jnp.int32)[:, None])
    sink_f = sink.astype(F32).reshape(N_KV_HEADS, Q_PER_KV, 1, 1)
    sink_kch = sink.astype(F32).reshape(N_KV_HEADS, 1, COLS_PER_KV, HEADS_PER_COL).transpose(0, 3, 1, 2)
    sink_p = jnp.broadcast_to(sink_kch[..., None, None],
                              (N_KV_HEADS, HEADS_PER_COL, 2, COLS_PER_KV, CHUNK, LANES)).reshape(
        N_KV_HEADS, HEADS_PER_COL, 2 * COLS_PER_KV * CHUNK, LANES)
    sink_s = jnp.broadcast_to(sink_f, (N_KV_HEADS, Q_PER_KV, t_new, 1)).reshape(N_KV_HEADS, Q_PER_KV * t_new, 1)
    o_p = _attn_prompt(q, kv, bias_p, sink_p, n_prompt, seq, 2)
    o_s = _attn_sample(q, kv, cache_k.reshape(dec_batch, buf, KV_DIM), cache_v.reshape(dec_batch, buf, KV_DIM),
                       bias_s, sink_s, n_prompt, t_new)

    hist = GROUP_ROWS
    sconv_pad = jnp.pad(sconv, ((0, 0), (hist - sconv.shape[1], 0), (0, 0)))
    w_rt_t = jnp.zeros((ROUTER_ROWS, d), F32)
    w_rt_t = w_rt_t.at[:N_GROUPS].set(w_grp.T).at[EXPERT_ROW0:EXPERT_ROW0 + N_EXPERTS].set(w_rt.T)
    b_rt_t = jnp.zeros((ROUTER_ROWS, 1), F32)
    b_rt_t = b_rt_t.at[:N_GROUPS, 0].set(b_grp.astype(F32)).at[EXPERT_ROW0:EXPERT_ROW0 + N_EXPERTS, 0].set(
        b_rt.astype(F32))
    nch = cdim // CONV_CHUNK
    conv_w_c = conv_w.reshape(CONV_WIDTH, nch, CONV_CHUNK).transpose(1, 0, 2)
    params = dict(conv_w=conv_w_c, conv_b=conv_b.reshape(nch, 1, CONV_CHUNK), ln_g=ln_g.reshape(1, cdim),
                  ln_b=ln_b.reshape(1, cdim), w_pw2=w_pw2.astype(BF16), b_pw2=b_pw2.reshape(1, d),
                  w_ao=w_ao.astype(BF16), w_out=w_out.astype(BF16), n2=n2.reshape(1, d),
                  w_rt=w_rt_t.astype(BF16), b_rt=b_rt_t)
    x1, h2, eidx, wtok, rank, cnt = _post(x_p, x_s, u, sconv_pad, o_p, o_s, gates, gate1, shift2, scale2,
                                             params, n_prompt, seq, tm)

    blk_e, nused, src, roww, dest = _dispatch(eidx, wtok, rank, cnt[:, 0].astype(jnp.int32), n)
    yb = _moe(h2, blk_e, nused, src, roww, w_gate, w_up, w_down)
    return x1, yb, dest, gate2, kv_state, u


def kernel(x_prompt, x_sample, c_prompt, c_sample, cache_k, cache_v, state_conv, rel_bias_table, norm1_g, norm2_g, w_ada, b_ada, w_in, attn_sink, w_attn_o, conv_w, conv_b, conv_ln_g, conv_ln_b, w_pw2, b_pw2, w_out, w_group, b_group, w_router, b_router, w_gate, w_up, w_down, final_g):
    batch, seq, d = x_prompt.shape
    dec_batch, t_new, _ = x_sample.shape
    depth = norm1_g.shape[0]
    assert depth == 1, "single trunk layer"
    assert t_new == GROUP_ROWS and seq % GROUP_ROWS == 0
    n_prompt = batch * seq
    n = n_prompt + dec_batch * t_new
    x_p = x_prompt.reshape(n_prompt, d)
    x_s = x_sample.reshape(dec_batch * t_new, d)
    c_all = jnp.concatenate([c_prompt, c_sample], axis=0)
    l = 0
    lp = tuple(a.reshape(a.shape[1:]) for a in (
        norm1_g, norm2_g, w_ada, b_ada, w_in, attn_sink, w_attn_o, conv_w, conv_b, conv_ln_g, conv_ln_b,
        w_pw2, b_pw2, w_out, w_group, b_group, w_router, b_router, w_gate, w_up, w_down))
    x1, yb, dest, gate2, kv_state, u = _layer(x_p, x_s, c_all, cache_k[l], cache_v[l], state_conv[l],
                                              rel_bias_table, lp, seq, t_new)
    out_p, out_s = _final(x1, yb, dest, gate2, final_g, n_prompt, 256)

    y_prompt = out_p.reshape(batch, seq, d)
    y_sample = out_s.reshape(dec_batch, t_new, d)
    cdim = u.shape[1]
    kvp = kv_state[:batch * WINDOW].reshape(batch, WINDOW, 2 * KV_DIM)
    new_k_prompt = kvp[..., :KV_DIM].reshape(1, batch, WINDOW, N_KV_HEADS, HEAD_DIM)
    new_v_prompt = kvp[..., KV_DIM:].reshape(1, batch, WINDOW, N_KV_HEADS, HEAD_DIM)
    new_conv_prompt = jnp.stack([u[(b + 1) * seq - (CONV_WIDTH - 1):(b + 1) * seq] for b in range(batch)])[None]
    kvs = kv_state[batch * WINDOW:].reshape(dec_batch, t_new, 2 * KV_DIM)
    buf = cache_k.shape[2]
    k_new = kvs[..., :KV_DIM].reshape(dec_batch, t_new, N_KV_HEADS, HEAD_DIM)
    v_new = kvs[..., KV_DIM:].reshape(dec_batch, t_new, N_KV_HEADS, HEAD_DIM)
    new_k_sample = jnp.concatenate([cache_k[l], k_new], axis=1)[:, -buf:][None]
    new_v_sample = jnp.concatenate([cache_v[l], v_new], axis=1)[:, -buf:][None]
    us = u[n_prompt:].reshape(dec_batch, t_new, cdim)
    new_conv_sample = jnp.concatenate([state_conv[l], us], axis=1)[:, -(CONV_WIDTH - 1):][None]
    return (y_prompt, y_sample, new_k_prompt, new_v_prompt, new_conv_prompt,
            new_k_sample, new_v_sample, new_conv_sample)
```

```python
import functools
import math

import jax
import jax.numpy as jnp
from jax import lax
from jax.experimental import pallas as pl
from jax.experimental.pallas import tpu as pltpu

F32 = jnp.float32
BF16 = jnp.bfloat16

CHUNK = 64
HEAD_DIM = 64
N_Q_HEADS = 16
N_KV_HEADS = 4
Q_PER_KV = N_Q_HEADS // N_KV_HEADS
ATTN_DIM = N_Q_HEADS * HEAD_DIM
KV_DIM = N_KV_HEADS * HEAD_DIM
WINDOW = 128
WIN_CHUNKS = WINDOW // CHUNK
BAND = (WIN_CHUNKS + 1) * CHUNK
CONV_WIDTH = 31
NUM_BUCKETS = 32
MAX_DISTANCE = 128
N_GROUPS = 4
EXPERTS_PER_GROUP = 8
N_EXPERTS = N_GROUPS * EXPERTS_PER_GROUP
MOE_BLOCK = 256
RMS_EPS = 1e-6
LN_EPS = 1e-5
NEG_INF = -1e30

GROUP_ROWS = 32
LANES = 128
SUBLANES = 8
CONV_CHUNK = 256
ROUTER_ROWS = 64
EXPERT_ROW0 = 8
V7X_VMEM_LIMIT = 56 * 1024 * 1024


def _cparams(*sem):
    return pltpu.CompilerParams(dimension_semantics=sem, vmem_limit_bytes=V7X_VMEM_LIMIT)


def _ada_kernel(c_ref, w_ref, b_ref, o_ref):
    c = c_ref[...]
    s = c * jax.nn.sigmoid(c)
    o_ref[...] = jnp.dot(s.astype(BF16), w_ref[...].astype(BF16),
                         preferred_element_type=F32) + b_ref[...]


def _ada(c_all, w_ada, b_ada):
    rows, d = c_all.shape
    ncol = w_ada.shape[1]
    tn = 1024
    return pl.pallas_call(
        _ada_kernel,
        grid=(ncol // tn,),
        in_specs=[pl.BlockSpec((rows, d), lambda j: (0, 0)),
                  pl.BlockSpec((d, tn), lambda j: (0, j)),
                  pl.BlockSpec((1, tn), lambda j: (0, j))],
        out_specs=pl.BlockSpec((rows, tn), lambda j: (0, j)),
        out_shape=jax.ShapeDtypeStruct((rows, ncol), F32),
        compiler_params=_cparams("arbitrary"),
        name="ada",
    )(c_all, w_ada, b_ada.reshape(1, ncol))


def _rms_mod(x, gain, scale, shift):
    ms = jnp.mean(x * x, axis=-1, keepdims=True)
    y = x * lax.rsqrt(ms + RMS_EPS) * gain
    return y * (1.0 + scale) + shift


def _pack_bf16_pairs(xb):
    w = xb.shape[1] // 2
    bits = lax.bitcast_convert_type(xb.astype(F32), jnp.uint32)
    return (bits[:, :w] >> 16) | (bits[:, w:] & jnp.uint32(0xFFFF0000))


def _unpack_bf16_pairs(p):
    lo = lax.bitcast_convert_type(p << 16, F32)
    hi = lax.bitcast_convert_type(p & jnp.uint32(0xFFFF0000), F32)
    return lo, hi


def _two_stream_specs(tm, width, prompt_tiles):
    return [pl.BlockSpec((tm, width), lambda i: (jnp.minimum(i, prompt_tiles - 1), 0)),
            pl.BlockSpec((tm, width), lambda i: (jnp.maximum(i - prompt_tiles, 0), 0))]


def _norm1_kernel(xp_ref, xs_ref, g_ref, sc_ref, sh_ref, h_ref, *, groups, prompt_tiles):
    def run(x_ref):
        def body(gi, carry):
            r = pl.multiple_of(gi * GROUP_ROWS, GROUP_ROWS)
            h = _rms_mod(x_ref[pl.ds(r, GROUP_ROWS), :], g_ref[...],
                         sc_ref[pl.ds(gi, 1), :], sh_ref[pl.ds(gi, 1), :])
            h_ref[pl.ds(r, GROUP_ROWS), :] = h.astype(h_ref.dtype)
            return carry

        lax.fori_loop(0, groups, body, 0)

    i = pl.program_id(0)
    pl.when(i < prompt_tiles)(lambda: run(xp_ref))
    pl.when(i >= prompt_tiles)(lambda: run(xs_ref))


def _norm1(x_p, x_s, gain, scale_g, shift_g, tm):
    d = x_p.shape[1]
    n = x_p.shape[0] + x_s.shape[0]
    groups = tm // GROUP_ROWS
    prompt_tiles = x_p.shape[0] // tm
    return pl.pallas_call(
        functools.partial(_norm1_kernel, groups=groups, prompt_tiles=prompt_tiles),
        grid=(n // tm,),
        in_specs=_two_stream_specs(tm, d, prompt_tiles) + [
            pl.BlockSpec((1, d), lambda i: (0, 0)),
            pl.BlockSpec((groups, d), lambda i: (i, 0)),
            pl.BlockSpec((groups, d), lambda i: (i, 0))],
        out_specs=pl.BlockSpec((tm, d), lambda i: (i, 0)),
        out_shape=jax.ShapeDtypeStruct((n, d), BF16),
        compiler_params=_cparams("arbitrary"),
        name="norm1",
    )(x_p, x_s, gain.reshape(1, d), scale_g, shift_g)


def _proj_kernel(h_ref, w_ref, *o_refs, mode):
    acc = jnp.dot(h_ref[...], w_ref[...], preferred_element_type=F32)
    if mode == "qkv":
        o_refs[0][...] = acc.astype(BF16)
    elif mode == "glu":
        half = acc.shape[1] // 2
        o_refs[0][...] = acc[:, :half] * jax.nn.sigmoid(acc[:, half:])
    else:
        o_refs[0][...] = jax.nn.sigmoid(acc).astype(BF16)


def _proj(h, w, mode, tm, tnw, out_dtypes):
    n, d = h.shape
    ncol = w.shape[1]
    tno = tnw // 2 if mode == "glu" else tnw
    nout = ncol // 2 if mode == "glu" else ncol
    outs = pl.pallas_call(
        functools.partial(_proj_kernel, mode=mode),
        grid=(n // tm, ncol // tnw),
        in_specs=[pl.BlockSpec((tm, d), lambda i, j: (i, 0)),
                  pl.BlockSpec((d, tnw), lambda i, j: (0, j))],
        out_specs=[pl.BlockSpec((tm, tno), lambda i, j: (i, j)) for _ in out_dtypes],
        out_shape=[jax.ShapeDtypeStruct((n, nout), dt) for dt in out_dtypes],
        compiler_params=_cparams("arbitrary", "arbitrary"),
        name="proj_" + mode,
    )(h, w)
    return outs


def _kv_state_kernel(h_ref, w_ref, o_ref):
    o_ref[...] = jnp.dot(h_ref[...], w_ref[...], preferred_element_type=F32)


def _kv_state(h, w_kv, n_prompt, seq):
    n, d = h.shape
    batch = n_prompt // seq
    per_seq = seq // WINDOW
    steps = batch + (n - n_prompt) // WINDOW

    def row_block(i):
        return jnp.where(i < batch, (i + 1) * per_seq - 1, n_prompt // WINDOW + i - batch)

    return pl.pallas_call(
        _kv_state_kernel,
        grid=(steps,),
        in_specs=[pl.BlockSpec((WINDOW, d), lambda i: (row_block(i), 0)),
                  pl.BlockSpec((d, 2 * KV_DIM), lambda i: (0, 0))],
        out_specs=pl.BlockSpec((WINDOW, 2 * KV_DIM), lambda i: (i, 0)),
        out_shape=jax.ShapeDtypeStruct((steps * WINDOW, 2 * KV_DIM), F32),
        compiler_params=_cparams("arbitrary"),
        name="kv_state",
    )(h, w_kv)


def _t5_bucket(rel):
    nb = NUM_BUCKETS // 2
    n = -rel
    ret = jnp.where(n < 0, nb, 0)
    n = jnp.abs(n)
    max_exact = nb // 2
    nf = jnp.maximum(n, 1).astype(F32)
    large = max_exact + (jnp.log(nf / max_exact) / math.log(MAX_DISTANCE / max_exact)
                         * (nb - max_exact)).astype(jnp.int32)
    large = jnp.minimum(large, nb - 1)
    return ret + jnp.where(n < max_exact, n, large)


def _bias_kernel(tbl_ref, bkt_ref, o_ref):
    bkt = bkt_ref[...]
    nq = bkt.shape[0]
    for k in range(N_KV_HEADS):
        for g in range(Q_PER_KV):
            acc = jnp.zeros(bkt.shape, F32)
            for b in range(NUM_BUCKETS):
                acc = jnp.where(bkt == b, tbl_ref[b, k * Q_PER_KV + g], acc)
            o_ref[k, g * nq:(g + 1) * nq, :] = acc


def _rel_bias(rel_table, rel):
    nq, nk = rel.shape
    return pl.pallas_call(
        _bias_kernel,
        in_specs=[pl.BlockSpec(memory_space=pltpu.SMEM),
                  pl.BlockSpec((nq, nk), lambda: (0, 0))],
        out_specs=pl.BlockSpec((N_KV_HEADS, Q_PER_KV * nq, nk), lambda: (0, 0, 0)),
        out_shape=jax.ShapeDtypeStruct((N_KV_HEADS, Q_PER_KV * nq, nk), F32),
        name="rel_bias",
    )(rel_table.astype(F32), _t5_bucket(rel))


PAIR_BAND = 2 * CHUNK + WINDOW
HEADS_PER_COL = LANES // HEAD_DIM
COLS_PER_KV = Q_PER_KV // HEADS_PER_COL


def _pair_bias_kernel(tbl_ref, bkt_ref, o_ref):
    for e in range(2):
        bkt = bkt_ref[e]
        for k in range(N_KV_HEADS):
            for col in range(COLS_PER_KV):
                for half in range(HEADS_PER_COL):
                    head = k * Q_PER_KV + col * HEADS_PER_COL + half
                    acc = jnp.full(bkt.shape, NEG_INF, F32)
                    for b in range(NUM_BUCKETS):
                        acc = jnp.where(bkt == b, tbl_ref[b, head], acc)
                    r0 = (e * COLS_PER_KV + col) * CHUNK
                    o_ref[k, r0:r0 + CHUNK, half * PAIR_BAND:(half + 1) * PAIR_BAND] = acc


def _pair_bias(rel_table):
    qi = jnp.arange(CHUNK, dtype=jnp.int32)[None, :, None]
    kj = jnp.arange(PAIR_BAND, dtype=jnp.int32)[None, None, :]
    e = jnp.arange(2, dtype=jnp.int32)[:, None, None]
    rel = (kj - WINDOW) - (e * CHUNK + qi)
    key_chunk = kj // CHUNK - e
    seen = (key_chunk >= 0) & (key_chunk <= WIN_CHUNKS)
    bkt = jnp.where(seen, _t5_bucket(rel), -1)
    rows = 2 * COLS_PER_KV * CHUNK
    return pl.pallas_call(
        _pair_bias_kernel,
        in_specs=[pl.BlockSpec(memory_space=pltpu.SMEM),
                  pl.BlockSpec((2, CHUNK, PAIR_BAND), lambda: (0, 0, 0))],
        out_specs=pl.BlockSpec((N_KV_HEADS, rows, HEADS_PER_COL * PAIR_BAND), lambda: (0, 0, 0)),
        out_shape=jax.ShapeDtypeStruct((N_KV_HEADS, rows, HEADS_PER_COL * PAIR_BAND), F32),
        name="pair_bias",
    )(rel_table.astype(F32), bkt)


def _attend(qc, kk_all, vv_all, bias_ref, sink_ref, mask_thr):
    nq = qc.shape[0]
    nk = kk_all.shape[0]
    pieces = []
    for k in range(N_KV_HEADS):
        qs = jnp.concatenate(
            [qc[:, (k * Q_PER_KV + g) * HEAD_DIM:(k * Q_PER_KV + g + 1) * HEAD_DIM]
             for g in range(Q_PER_KV)], axis=0)
        kk = kk_all[:, k * HEAD_DIM:(k + 1) * HEAD_DIM]
        vv = vv_all[:, k * HEAD_DIM:(k + 1) * HEAD_DIM]
        lg = lax.dot_general(qs, kk, (((1,), (1,)), ((), ())),
                             preferred_element_type=F32) * (HEAD_DIM ** -0.5) + bias_ref[k]
        if mask_thr is not None:
            col = lax.broadcasted_iota(jnp.int32, (Q_PER_KV * nq, nk), 1)
            lg = jnp.where(col < mask_thr, NEG_INF, lg)
        s = sink_ref[k]
        m = jnp.maximum(jnp.max(lg, axis=-1, keepdims=True), s)
        p = jnp.exp(lg - m)
        den = jnp.sum(p, axis=-1, keepdims=True) + jnp.exp(s - m)
        o = jnp.dot(p.astype(BF16), vv, preferred_element_type=F32) / den
        pieces.extend(o[g * nq:(g + 1) * nq, :] for g in range(Q_PER_KV))
    return jnp.concatenate(pieces, axis=1)


def _attn_prompt_kernel(q_ref, kv_ref, halo_ref, bias_ref, sink_ref, o_ref, ka, kb, va, vb, *, pairs):
    t = pl.program_id(1)
    kvcat = jnp.concatenate([halo_ref[...], kv_ref[...]], axis=0)
    low = lax.broadcasted_iota(jnp.int32, (kvcat.shape[0], LANES), 1) < HEAD_DIM
    zero = jnp.zeros((kvcat.shape[0], LANES), BF16)
    ones_low = jnp.where(low, 1.0, 0.0).astype(BF16)
    ones_high = jnp.where(low, 0.0, 1.0).astype(BF16)
    kv_cols = KV_DIM // LANES
    for col in range(2 * kv_cols):
        x = kvcat[:, col * LANES:(col + 1) * LANES]
        xs = jnp.concatenate([x[:, HEAD_DIM:], x[:, :HEAD_DIM]], axis=1)
        dst_a, dst_b = (ka, kb) if col < kv_cols else (va, vb)
        k0 = (col % kv_cols) * HEADS_PER_COL
        dst_a[k0, :, 0:LANES] = jnp.where(low, x, zero)
        dst_b[k0, :, 0:LANES] = jnp.where(low, zero, xs)
        dst_a[k0 + 1, :, 0:LANES] = jnp.where(low, xs, zero)
        dst_b[k0 + 1, :, 0:LANES] = jnp.where(low, zero, x)
    for k in range(N_KV_HEADS):
        va[k, :, LANES:2 * LANES] = ones_low
        vb[k, :, LANES:2 * LANES] = ones_high

    out_low = lax.broadcasted_iota(jnp.int32, (2 * COLS_PER_KV * CHUNK, LANES), 1) < HEAD_DIM
    for p in range(pairs):
        r0 = p * 2 * CHUNK
        for k in range(N_KV_HEADS):
            lhs = jnp.concatenate(
                [q_ref[r0 + e * CHUNK:r0 + (e + 1) * CHUNK, (k * COLS_PER_KV + c) * LANES:(k * COLS_PER_KV + c + 1) * LANES]
                 for e in range(2) for c in range(COLS_PER_KV)], axis=0)
            keys = jnp.concatenate([ka[k, r0:r0 + PAIR_BAND, :], kb[k, r0:r0 + PAIR_BAND, :]], axis=0)
            lg = lax.dot_general(lhs, keys, (((1,), (1,)), ((), ())),
                                 preferred_element_type=F32) * (HEAD_DIM ** -0.5) + bias_ref[k]
            probs, sink_terms = [], []
            for half in range(HEADS_PER_COL):
                seg = lg[:, half * PAIR_BAND:(half + 1) * PAIR_BAND]
                if p == 0:
                    kcol = lax.broadcasted_iota(jnp.int32, seg.shape, 1)
                    seg = jnp.where(kcol < jnp.where(t == 0, WINDOW, 0), NEG_INF, seg)
                s = sink_ref[k, half]
                folded = seg[:, :LANES]
                for j in range(1, PAIR_BAND // LANES):
                    folded = jnp.maximum(folded, seg[:, j * LANES:(j + 1) * LANES])
                m = jnp.maximum(jnp.max(folded, axis=-1, keepdims=True), s)
                pr = jnp.exp(seg - jnp.concatenate([m] * (PAIR_BAND // LANES), axis=1))
                sink_terms.append(jnp.exp(s - m))
                probs.append(pr.astype(BF16))
            vals = jnp.concatenate([va[k, r0:r0 + PAIR_BAND, :], vb[k, r0:r0 + PAIR_BAND, :]], axis=0)
            oe = jnp.dot(jnp.concatenate(probs, axis=1), vals, preferred_element_type=F32)
            o = oe[:, :LANES] / (oe[:, LANES:] + jnp.where(out_low, sink_terms[0], sink_terms[1]))
            for e in range(2):
                for c in range(COLS_PER_KV):
                    rr = (e * COLS_PER_KV + c) * CHUNK
                    o_ref[r0 + e * CHUNK:r0 + (e + 1) * CHUNK,
                          (k * COLS_PER_KV + c) * LANES:(k * COLS_PER_KV + c + 1) * LANES] = (
                        o[rr:rr + CHUNK, :].astype(o_ref.dtype))


def _attn_prompt(q, kv, bias, sink_rows, n_prompt, seq, pairs):
    rows = pairs * 2 * CHUNK
    tiles = seq // rows
    halo_per_tile = rows // WINDOW
    keys = WINDOW + rows
    return pl.pallas_call(
        functools.partial(_attn_prompt_kernel, pairs=pairs),
        grid=(n_prompt // seq, tiles),
        scratch_shapes=[pltpu.VMEM((N_KV_HEADS, keys, LANES), BF16) for _ in range(2)]
        + [pltpu.VMEM((N_KV_HEADS, keys, 2 * LANES), BF16) for _ in range(2)],
        in_specs=[pl.BlockSpec((rows, ATTN_DIM), lambda b, t: (b * tiles + t, 0)),
                  pl.BlockSpec((rows, 2 * KV_DIM), lambda b, t: (b * tiles + t, 0)),
                  pl.BlockSpec((WINDOW, 2 * KV_DIM),
                               lambda b, t: (jnp.maximum((b * tiles + t) * halo_per_tile - 1, 0), 0)),
                  pl.BlockSpec(bias.shape, lambda b, t: (0, 0, 0)),
                  pl.BlockSpec(sink_rows.shape, lambda b, t: (0, 0, 0, 0))],
        out_specs=pl.BlockSpec((rows, ATTN_DIM), lambda b, t: (b * tiles + t, 0)),
        out_shape=jax.ShapeDtypeStruct((n_prompt, ATTN_DIM), BF16),
        compiler_params=_cparams("arbitrary", "arbitrary"),
        name="attn_prompt",
    )(q, kv, kv, bias, sink_rows)


def _attn_sample_kernel(q_ref, kv_ref, ck_ref, cv_ref, bias_ref, sink_ref, o_ref):
    kv = kv_ref[...]
    kk = jnp.concatenate([ck_ref[0].astype(BF16), kv[:, :KV_DIM]], axis=0)
    vv = jnp.concatenate([cv_ref[0].astype(BF16), kv[:, KV_DIM:]], axis=0)
    o = _attend(q_ref[...], kk, vv, bias_ref, sink_ref, None)
    o_ref[...] = o.astype(o_ref.dtype)


def _attn_sample(q, kv, cache_k, cache_v, bias, sink_rows, n_prompt, t_new):
    dec_batch, buf, _ = cache_k.shape
    first = n_prompt // t_new
    return pl.pallas_call(
        _attn_sample_kernel,
        grid=(dec_batch,),
        in_specs=[pl.BlockSpec((t_new, ATTN_DIM), lambda s: (first + s, 0)),
                  pl.BlockSpec((t_new, 2 * KV_DIM), lambda s: (first + s, 0)),
                  pl.BlockSpec((1, buf, KV_DIM), lambda s: (s, 0, 0)),
                  pl.BlockSpec((1, buf, KV_DIM), lambda s: (s, 0, 0)),
                  pl.BlockSpec(bias.shape, lambda s: (0, 0, 0)),
                  pl.BlockSpec(sink_rows.shape, lambda s: (0, 0, 0))],
        out_specs=pl.BlockSpec((t_new, ATTN_DIM), lambda s: (s, 0)),
        out_shape=jax.ShapeDtypeStruct((dec_batch * t_new, ATTN_DIM), BF16),
        compiler_params=_cparams("arbitrary"),
        name="attn_sample",
    )(q, kv, cache_k, cache_v, bias, sink_rows)


def _route(lt, eidx_ref, wtok_ref, rank_ref, cnt_ref, cnt_scr):
    gl = [lt[r:r + 1, :] for r in range(N_GROUPS)]
    gmax = gl[0]
    gsel = jnp.zeros(gl[0].shape, jnp.int32)
    for r in range(1, N_GROUPS):
        better = gl[r] > gmax
        gsel = jnp.where(better, r, gsel)
        gmax = jnp.maximum(gmax, gl[r])
    gexp = [jnp.exp(v - gmax) for v in gl]
    gsum = gexp[0]
    for r in range(1, N_GROUPS):
        gsum = gsum + gexp[r]
    psel = jnp.zeros(gl[0].shape, F32)
    for r in range(N_GROUPS):
        psel = jnp.where(gsel == r, gexp[r] / gsum, psel)
    el = jnp.zeros((EXPERTS_PER_GROUP, lt.shape[1]), F32)
    for r in range(N_GROUPS):
        lo = EXPERT_ROW0 + r * EXPERTS_PER_GROUP
        el = jnp.where(gsel == r, lt[lo:lo + EXPERTS_PER_GROUP, :], el)
    emax = jnp.max(el, axis=0, keepdims=True)
    ee = jnp.exp(el - emax)
    pin = ee / jnp.sum(ee, axis=0, keepdims=True)
    idx = lax.broadcasted_iota(jnp.int32, pin.shape, 0)
    p1 = jnp.max(pin, axis=0, keepdims=True)
    i1 = jnp.min(jnp.where(pin == p1, idx, EXPERTS_PER_GROUP), axis=0, keepdims=True)
    rest = jnp.where(idx == i1, -1.0, pin)
    p2 = jnp.max(rest, axis=0, keepdims=True)
    i2 = jnp.min(jnp.where(rest == p2, idx, EXPERTS_PER_GROUP), axis=0, keepdims=True)
    tot = p1 + p2
    e1 = gsel * EXPERTS_PER_GROUP + i1
    e2 = gsel * EXPERTS_PER_GROUP + i2
    eidx_ref[0:1, :] = e1
    eidx_ref[1:2, :] = e2
    wtok_ref[0:1, :] = psel * p1 / tot
    wtok_ref[1:2, :] = psel * p2 / tot

    t = lt.shape[1]
    eiota = lax.broadcasted_iota(jnp.int32, (N_EXPERTS, t), 0)
    oh1 = (eiota == e1).astype(F32)
    oh2 = (eiota == e2).astype(F32)
    both = oh1 + oh2
    before = (lax.broadcasted_iota(jnp.int32, (t, t), 0)
              < lax.broadcasted_iota(jnp.int32, (t, t), 1)).astype(BF16)
    prior = jnp.dot(both.astype(BF16), before, preferred_element_type=F32) + cnt_scr[:, 0:1]
    rank_ref[0:1, :] = jnp.sum(oh1 * prior, axis=0, keepdims=True).astype(jnp.int32)
    rank_ref[1:2, :] = jnp.sum(oh2 * prior, axis=0, keepdims=True).astype(jnp.int32)
    cnt_scr[...] = cnt_scr[...] + jnp.sum(both, axis=1, keepdims=True)
    cnt_ref[...] = cnt_scr[...]


def _post_kernel(xp_ref, xs_ref, u_ref, uhalo_ref, sconv_ref, op_ref, os_ref, gates_ref,
                 gate1_ref, shift2_ref, scale2_ref,
                 cw_ref, cb_ref, lng_ref, lnb_ref, wpw2_ref, bpw2_ref, wao_ref, wout_ref,
                 n2_ref, wrt_ref, brt_ref,
                 x1_ref, h2_ref, eidx_ref, wtok_ref, rank_ref, cnt_ref,
                 uext, shift_scr, d_scr, s_scr, o_scr, mix_scr, h2b_scr, cnt_scr,
                 *, groups, prompt_tiles, tiles_per_seq):
    i = pl.program_id(0)
    d_model = xp_ref.shape[1]

    @pl.when(i == 0)
    def _():
        cnt_scr[...] = jnp.zeros(cnt_scr.shape, F32)
    nch = cw_ref.shape[0]
    cw = uext.shape[2]
    hist = uext.shape[1] - GROUP_ROWS
    lead = hist - (CONV_WIDTH - 1)

    def put_hist(g, rows):
        for c in range(nch):
            uext[g * nch + c, 0:hist, :] = rows[:, c * cw:(c + 1) * cw]

    @pl.when(i < prompt_tiles)
    def _():
        first = (i % tiles_per_seq) == 0
        put_hist(0, jnp.where(first, 0.0, uhalo_ref[...]))
        for g in range(1, groups):
            put_hist(g, u_ref[(g - 1) * GROUP_ROWS:g * GROUP_ROWS, :])
        o_scr[...] = op_ref[...]

    @pl.when(i >= prompt_tiles)
    def _():
        for g in range(groups):
            put_hist(g, sconv_ref[g])
        o_scr[...] = os_ref[...]

    for g in range(groups):
        for c in range(nch):
            uext[g * nch + c, hist:hist + GROUP_ROWS, :] = (
                u_ref[g * GROUP_ROWS:(g + 1) * GROUP_ROWS, c * cw:(c + 1) * cw])

    span = shift_scr.shape[1]

    def conv_chunk(k, carry):
        c = k % nch
        win_all = uext[k]
        for s in range(1, SUBLANES):
            shift_scr[s, :, :] = pltpu.roll(win_all, win_all.shape[0] - s, 0)[0:span, :]
        acc = None
        for j in range(CONV_WIDTH):
            base, s = divmod(lead + j, SUBLANES)
            rows = pl.ds(base * SUBLANES, GROUP_ROWS)
            win = uext[k, rows, :] if s == 0 else shift_scr[s, rows, :]
            term = win * cw_ref[c, j:j + 1, :]
            acc = term if acc is None else acc + term
        d_scr[k] = acc + cb_ref[c]
        return carry

    lax.fori_loop(0, groups * nch, conv_chunk, 0)

    for g in range(groups):
        dd = jnp.concatenate([d_scr[g * nch + c] for c in range(nch)], axis=1)
        mu = jnp.mean(dd, axis=-1, keepdims=True)
        var = jnp.mean(jnp.square(dd - mu), axis=-1, keepdims=True)
        y = (dd - mu) * lax.rsqrt(var + LN_EPS) * lng_ref[...] + lnb_ref[...]
        s_scr[g * GROUP_ROWS:(g + 1) * GROUP_ROWS, :] = (y * jax.nn.sigmoid(y)).astype(BF16)

    conv_out = jnp.dot(s_scr[...], wpw2_ref[...], preferred_element_type=F32) + bpw2_ref[...]
    attn_out = jnp.dot(o_scr[...], wao_ref[...], preferred_element_type=F32)
    merged = (gates_ref[:, :d_model].astype(F32) * attn_out
              + gates_ref[:, d_model:].astype(F32) * conv_out)
    mix_scr[...] = jnp.dot(merged.astype(BF16), wout_ref[...], preferred_element_type=F32)

    def residual(x_ref):
        def res_group(g, carry):
            r = pl.multiple_of(g * GROUP_ROWS, GROUP_ROWS)
            x1 = (x_ref[pl.ds(r, GROUP_ROWS), :]
                  + gate1_ref[pl.ds(g, 1), :] * mix_scr[pl.ds(r, GROUP_ROWS), :])
            x1_ref[pl.ds(r, GROUP_ROWS), :] = x1
            h = _rms_mod(x1, n2_ref[...], scale2_ref[pl.ds(g, 1), :], shift2_ref[pl.ds(g, 1), :])
            hb = h.astype(BF16)
            h2b_scr[pl.ds(r, GROUP_ROWS), :] = hb
            h2_ref[pl.ds(r, GROUP_ROWS), :] = _pack_bf16_pairs(hb)
            return carry

        lax.fori_loop(0, groups, res_group, 0)

    pl.when(i < prompt_tiles)(lambda: residual(xp_ref))
    pl.when(i >= prompt_tiles)(lambda: residual(xs_ref))

    lt = lax.dot_general(wrt_ref[...], h2b_scr[...], (((1,), (1,)), ((), ())),
                         preferred_element_type=F32) + brt_ref[...]
    _route(lt, eidx_ref, wtok_ref, rank_ref, cnt_ref, cnt_scr)


def _post(x_p, x_s, u, sconv_pad, o_p, o_s, gates, gate1_g, shift2_g, scale2_g, p, n_prompt, seq, tm):
    d = x_p.shape[1]
    n = x_p.shape[0] + x_s.shape[0]
    cdim = u.shape[1]
    nch = cdim // CONV_CHUNK
    groups = tm // GROUP_ROWS
    prompt_tiles = n_prompt // tm
    const = lambda shape: pl.BlockSpec(shape, lambda i: (0,) * len(shape))
    row = lambda w: pl.BlockSpec((tm, w), lambda i: (i, 0))
    grp = pl.BlockSpec((groups, d), lambda i: (i, 0))
    return pl.pallas_call(
        functools.partial(_post_kernel, groups=groups, prompt_tiles=prompt_tiles,
                          tiles_per_seq=seq // tm),
        grid=(n // tm,),
        in_specs=_two_stream_specs(tm, d, prompt_tiles) + [
                  row(cdim),
                  pl.BlockSpec((GROUP_ROWS, cdim), lambda i: (jnp.maximum(i * groups - 1, 0), 0)),
                  pl.BlockSpec((groups, GROUP_ROWS, cdim),
                               lambda i: (jnp.maximum(i - prompt_tiles, 0), 0, 0))]
                 + _two_stream_specs(tm, ATTN_DIM, prompt_tiles) + [
                  row(2 * d), grp, grp, grp,
                  const((nch, CONV_WIDTH, CONV_CHUNK)), const((nch, 1, CONV_CHUNK)),
                  const((1, cdim)), const((1, cdim)),
                  const((cdim, d)), const((1, d)), const((ATTN_DIM, d)), const((d, d)),
                  const((1, d)), const((ROUTER_ROWS, d)), const((ROUTER_ROWS, 1))],
        out_specs=[row(d),
                   row(d // 2),
                   pl.BlockSpec((2, tm), lambda i: (0, i)),
                   pl.BlockSpec((2, tm), lambda i: (0, i)),
                   pl.BlockSpec((2, tm), lambda i: (0, i)),
                   pl.BlockSpec((N_EXPERTS, LANES), lambda i: (0, 0))],
        out_shape=[jax.ShapeDtypeStruct((n, d), F32),
                   jax.ShapeDtypeStruct((n, d // 2), jnp.uint32),
                   jax.ShapeDtypeStruct((2, n), jnp.int32),
                   jax.ShapeDtypeStruct((2, n), F32),
                   jax.ShapeDtypeStruct((2, n), jnp.int32),
                   jax.ShapeDtypeStruct((N_EXPERTS, LANES), F32)],
        scratch_shapes=[pltpu.VMEM((groups * nch, 2 * GROUP_ROWS, CONV_CHUNK), F32),
                        pltpu.VMEM((SUBLANES, 2 * GROUP_ROWS - SUBLANES, CONV_CHUNK), F32),
                        pltpu.VMEM((groups * nch, GROUP_ROWS, CONV_CHUNK), F32),
                        pltpu.VMEM((tm, cdim), BF16),
                        pltpu.VMEM((tm, ATTN_DIM), BF16),
                        pltpu.VMEM((tm, d), F32),
                        pltpu.VMEM((tm, d), BF16),
                        pltpu.VMEM((N_EXPERTS, LANES), F32)],
        compiler_params=_cparams("arbitrary"),
        name="post",
    )(x_p, x_s, u, u, sconv_pad, o_p, o_s, gates, gate1_g, shift2_g, scale2_g,
      p["conv_w"], p["conv_b"], p["ln_g"], p["ln_b"], p["w_pw2"], p["b_pw2"], p["w_ao"], p["w_out"],
      p["n2"], p["w_rt"], p["b_rt"])


TABLE_GROUP = 8
GATHER_AHEAD = 4


def _grouped_tables(tab, group=TABLE_GROUP, ahead=1):
    steps, w = tab.shape
    assert steps % group == 0
    padded = jnp.concatenate([tab] + [tab[-1:]] * ahead, axis=0)
    cols = [tab.reshape(steps // group, group * w)] + [padded[group + a::group][:steps // group]
                                                         for a in range(ahead)]
    return jnp.concatenate(cols, axis=1).reshape(steps // group, 1, (group + ahead) * w)

def _moe_kernel(blk_e_ref, next_e_ref, nused_ref, src_ref, roww_ref,
                h2_hbm, wg_hbm, wu_hbm, wd_hbm, y_ref,
                xbuf, x16, wg_f, wu_f, wd_f, wg_b, wu_b, wd_b, gsem, wsem):
    i = pl.program_id(0)
    nused = nused_ref[0]
    nslots = GATHER_AHEAD + 1
    slot = i % nslots
    base = (i % TABLE_GROUP) * MOE_BLOCK
    weights = ((wg_hbm, wg_f, wg_b), (wu_hbm, wu_f, wu_b), (wd_hbm, wd_f, wd_b))

    def gather(first, sl):
        for r in range(MOE_BLOCK):
            pltpu.make_async_copy(h2_hbm.at[pl.ds(src_ref[0, 0, first + r], 1), :],
                                  xbuf.at[sl, pl.ds(r, 1), :], gsem.at[sl]).start(priority=r % 2)

    def gather_wait(sl):
        pltpu.make_async_copy(h2_hbm.at[pl.ds(0, MOE_BLOCK), :], xbuf.at[sl], gsem.at[sl]).wait()

    def weights_start(e):
        for hbm, f32_buf, _ in weights:
            pltpu.make_async_copy(hbm.at[e], f32_buf, wsem.at[0]).start()

    def weights_wait():
        for hbm, f32_buf, _ in weights:
            pltpu.make_async_copy(hbm.at[0], f32_buf, wsem.at[0]).wait()

    @pl.when(i == 0)
    def _():
        weights_start(blk_e_ref[0])
        for a in range(GATHER_AHEAD):
            gather(base + a * MOE_BLOCK, a)

    @pl.when(i >= nused)
    def _():
        y_ref[...] = jnp.zeros(y_ref.shape, y_ref.dtype)

    @pl.when(i < nused)
    def _():
        changed = jnp.logical_or(i == 0, blk_e_ref[i] != blk_e_ref[jnp.maximum(i - 1, 0)])

        @pl.when(changed)
        def _():
            weights_wait()
            for _, f32_buf, b16_buf in weights:
                b16_buf[...] = f32_buf[...].astype(BF16)
            weights_start(next_e_ref[i])

        gather_wait(slot)
        x_lo, x_hi = _unpack_bf16_pairs(xbuf[slot])
        x16[:, :x_lo.shape[1]] = x_lo.astype(BF16)
        x16[:, x_lo.shape[1]:] = x_hi.astype(BF16)
        gather(base + GATHER_AHEAD * MOE_BLOCK, (i + GATHER_AHEAD) % nslots)
        x = x16[...]
        hg = jnp.dot(x, wg_b[...], preferred_element_type=F32)
        hu = jnp.dot(x, wu_b[...], preferred_element_type=F32)
        hid = (hg * jax.nn.sigmoid(hg) * hu).astype(BF16)
        y = jnp.dot(hid, wd_b[...], preferred_element_type=F32) * roww_ref[...]
        y_ref[...] = _pack_bf16_pairs(y.astype(BF16))

        @pl.when(i == nused - 1)
        def _():
            weights_wait()
            for a in range(1, GATHER_AHEAD + 1):
                gather_wait((i + a) % nslots)


def _moe(h2, blk_e, next_e, nused, src, roww, w_gate, w_up, w_down):
    n_blocks = blk_e.shape[0]
    n_exp, d, de = w_gate.shape
    src_tables = _grouped_tables(src.reshape(n_blocks, MOE_BLOCK), ahead=GATHER_AHEAD)
    grid_spec = pltpu.PrefetchScalarGridSpec(
        num_scalar_prefetch=3,
        grid=(n_blocks,),
        in_specs=[pl.BlockSpec((1, 1, src_tables.shape[2]), lambda i, be, ne, nu: (i // TABLE_GROUP, 0, 0),
                               memory_space=pltpu.SMEM),
                  pl.BlockSpec((MOE_BLOCK, 1), lambda i, be, ne, nu: (i, 0)),
                  pl.BlockSpec(memory_space=pl.ANY),
                  pl.BlockSpec(memory_space=pl.ANY),
                  pl.BlockSpec(memory_space=pl.ANY),
                  pl.BlockSpec(memory_space=pl.ANY)],
        out_specs=pl.BlockSpec((MOE_BLOCK, d // 2), lambda i, be, ne, nu: (i, 0)),
        scratch_shapes=[pltpu.VMEM((GATHER_AHEAD + 1, MOE_BLOCK, d // 2), jnp.uint32),
                        pltpu.VMEM((MOE_BLOCK, d), BF16),
                        pltpu.VMEM((d, de), F32), pltpu.VMEM((d, de), F32), pltpu.VMEM((de, d), F32),
                        pltpu.VMEM((d, de), BF16), pltpu.VMEM((d, de), BF16), pltpu.VMEM((de, d), BF16),
                        pltpu.SemaphoreType.DMA((GATHER_AHEAD + 1,)), pltpu.SemaphoreType.DMA((1,))],
    )
    return pl.pallas_call(
        _moe_kernel,
        grid_spec=grid_spec,
        out_shape=jax.ShapeDtypeStruct((n_blocks * MOE_BLOCK, d // 2), jnp.uint32),
        compiler_params=_cparams("arbitrary"),
        name="moe",
    )(blk_e, next_e, nused, src_tables, roww, h2, w_gate, w_up, w_down)


def _dispatch(eidx, wtok, rank, counts, n):
    a_tot = 2 * n
    experts = jnp.arange(N_EXPERTS, dtype=jnp.int32)
    padded = (counts + MOE_BLOCK - 1) // MOE_BLOCK * MOE_BLOCK
    pad_end = jnp.sum(jnp.where(experts[None, :] <= experts[:, None], padded[None, :], 0), axis=1)
    pad_start = pad_end - padded
    dest = jnp.sum(jnp.where(eidx[:, :, None] == experts, pad_start, 0), axis=-1) + rank
    n_blocks = -(-(a_tot + N_EXPERTS * (MOE_BLOCK - 1)) // MOE_BLOCK)
    n_blocks = -(-n_blocks // TABLE_GROUP) * TABLE_GROUP
    n_rows = n_blocks * MOE_BLOCK
    tok = jnp.tile(jnp.arange(n, dtype=jnp.int32), 2)
    upd = jnp.stack([tok, lax.bitcast_convert_type(wtok.reshape(-1), jnp.int32)], axis=1)
    rows = jnp.zeros((n_rows, 2), jnp.int32).at[dest.reshape(-1)].set(upd, unique_indices=True)
    blk_start = jnp.arange(n_blocks, dtype=jnp.int32) * MOE_BLOCK
    blk_e = jnp.minimum(jnp.sum((pad_end[None, :] <= blk_start[:, None]).astype(jnp.int32), axis=1),
                        N_EXPERTS - 1)
    nused = (pad_end[-1] // MOE_BLOCK).reshape(1)
    roww = lax.bitcast_convert_type(rows[:, 1], F32)
    later = jnp.where((counts > 0)[None, :] & (experts[None, :] > experts[:, None]), experts[None, :], N_EXPERTS)
    next_of = jnp.min(later, axis=1)
    next_of = jnp.where(next_of < N_EXPERTS, next_of, experts)
    next_e = jnp.sum(jnp.where(blk_e[:, None] == experts[None, :], next_of[None, :], 0), axis=1)
    return blk_e, next_e, nused, rows[:, 0].reshape(n_blocks, 1, MOE_BLOCK), roww.reshape(n_rows, 1), dest


def _final_kernel(d0_ref, d1_ref, x1_ref, gate2_ref, fg_ref, yb_hbm,
                  op_ref, os_ref, ybuf, sem, *, groups, prompt_tiles, tiles, table_group):
    i = pl.program_id(0)
    slot = i % 2
    tm = x1_ref.shape[0]
    base = (i % table_group) * tm

    def gather(first, sl):
        for r in range(tm):
            pltpu.make_async_copy(yb_hbm.at[pl.ds(d0_ref[0, 0, first + r], 1), :],
                                  ybuf.at[sl, 0, pl.ds(r, 1), :], sem.at[sl]).start(priority=0)
            pltpu.make_async_copy(yb_hbm.at[pl.ds(d1_ref[0, 0, first + r], 1), :],
                                  ybuf.at[sl, 1, pl.ds(r, 1), :], sem.at[sl]).start(priority=1)

    def gather_wait(sl):
        for k in range(2):
            pltpu.make_async_copy(yb_hbm.at[pl.ds(0, tm), :], ybuf.at[sl, k], sem.at[sl]).wait()

    @pl.when(i == 0)
    def _():
        gather(base, 0)

    gather_wait(slot)
    gather(base + tm, 1 - slot)

    def run(o_ref):
        def body(g, carry):
            r = pl.multiple_of(g * GROUP_ROWS, GROUP_ROWS)
            lo0, hi0 = _unpack_bf16_pairs(ybuf[slot, 0, pl.ds(r, GROUP_ROWS), :])
            lo1, hi1 = _unpack_bf16_pairs(ybuf[slot, 1, pl.ds(r, GROUP_ROWS), :])
            moe = jnp.concatenate([lo0 + lo1, hi0 + hi1], axis=1)
            x2 = x1_ref[pl.ds(r, GROUP_ROWS), :] + gate2_ref[pl.ds(g, 1), :] * moe
            ms = jnp.mean(x2 * x2, axis=-1, keepdims=True)
            o_ref[pl.ds(r, GROUP_ROWS), :] = x2 * lax.rsqrt(ms + RMS_EPS) * fg_ref[...]
            return carry

        lax.fori_loop(0, groups, body, 0)

    pl.when(i < prompt_tiles)(lambda: run(op_ref))
    pl.when(i >= prompt_tiles)(lambda: run(os_ref))

    @pl.when(i == tiles - 1)
    def _():
        gather_wait(1 - slot)


def _final(x1, yb, dest, gate2_g, final_g, n_prompt, tm):
    n, d = x1.shape
    groups = tm // GROUP_ROWS
    tiles = n // tm
    prompt_tiles = n_prompt // tm
    table_group = max(g for g in range(1, TABLE_GROUP + 1) if tiles % g == 0)
    tables = [_grouped_tables(dest[k].reshape(tiles, tm), table_group) for k in range(2)]
    idx_spec = pl.BlockSpec((1, 1, tables[0].shape[2]), lambda i: (i // table_group, 0, 0),
                            memory_space=pltpu.SMEM)
    return pl.pallas_call(
        functools.partial(_final_kernel, groups=groups, prompt_tiles=prompt_tiles, tiles=tiles,
                          table_group=table_group),
        grid=(tiles,),
        in_specs=[idx_spec, idx_spec,
                  pl.BlockSpec((tm, d), lambda i: (i, 0)),
                  pl.BlockSpec((groups, d), lambda i: (i, 0)),
                  pl.BlockSpec((1, d), lambda i: (0, 0)),
                  pl.BlockSpec(memory_space=pl.ANY)],
        out_specs=_two_stream_specs(tm, d, prompt_tiles),
        out_shape=[jax.ShapeDtypeStruct((n_prompt, d), F32),
                   jax.ShapeDtypeStruct((n - n_prompt, d), F32)],
        scratch_shapes=[pltpu.VMEM((2, 2, tm, d // 2), jnp.uint32), pltpu.SemaphoreType.DMA((2,))],
        compiler_params=_cparams("arbitrary"),
        name="final",
    )(tables[0], tables[1], x1, gate2_g, final_g.reshape(1, d), yb)


def _layer(x_p, x_s, c_all, cache_k, cache_v, sconv, rel_table, lp, seq, t_new):
    (n1, n2, w_ada, b_ada, w_in, sink, w_ao, conv_w, conv_b, ln_g, ln_b, w_pw2, b_pw2, w_out,
     w_grp, b_grp, w_rt, b_rt, w_gate, w_up, w_down) = lp
    n_prompt, d = x_p.shape
    n = n_prompt + x_s.shape[0]
    batch = n_prompt // seq
    dec_batch = (n - n_prompt) // t_new
    cdim = conv_w.shape[1]
    tm = 256

    c_rows = -(-c_all.shape[0] // 8) * 8
    c_pad = jnp.pad(c_all, ((0, c_rows - c_all.shape[0]), (0, 0)))
    mods = _ada(c_pad, w_ada, b_ada)

    def per_group(k):
        m = mods[:, k * d:(k + 1) * d]
        mp = jnp.broadcast_to(m[:batch, None, :], (batch, seq // GROUP_ROWS, d))
        return jnp.concatenate([mp.reshape(batch * (seq // GROUP_ROWS), d), m[batch:batch + dec_batch]],
                               axis=0)

    shift1, scale1, gate1, shift2, scale2, gate2 = [per_group(k) for k in range(6)]

    h1 = _norm1(x_p, x_s, n1, scale1, shift1, 512)

    k_off = ATTN_DIM
    glu_off = k_off + 2 * KV_DIM
    gate_off = glu_off + 2 * cdim
    w_in_b = w_in.astype(BF16)
    tglu = 512
    w_glu = jnp.stack([w_in_b[:, glu_off:glu_off + cdim].reshape(d, cdim // tglu, tglu),
                       w_in_b[:, glu_off + cdim:gate_off].reshape(d, cdim // tglu, tglu)],
                      axis=2).reshape(d, 2 * cdim)
    tmm = 1536
    (q,) = _proj(h1, w_in_b[:, :k_off], "qkv", tmm, 1024, [BF16])
    (kv,) = _proj(h1, w_in_b[:, k_off:glu_off], "qkv", tmm, 512, [BF16])
    kv_state = _kv_state(h1, w_in_b[:, k_off:glu_off], n_prompt, seq)
    (u,) = _proj(h1, w_glu, "glu", tmm, 2 * tglu, [F32])
    (gates,) = _proj(h1, w_in_b[:, gate_off:], "gate", tmm, 1024, [BF16])

    bias_p = _pair_bias(rel_table)
    buf = cache_k.shape[1]
    kpos = jnp.concatenate([jnp.arange(buf, dtype=jnp.int32) - buf, jnp.arange(t_new, dtype=jnp.int32)])
    bias_s = _rel_bias(rel_table, kpos[None, :] - jnp.arange(t_new, dtype=jnp.int32)[:, None])
    sink_f = sink.astype(F32).reshape(N_KV_HEADS, Q_PER_KV, 1, 1)
    sink_kch = sink.astype(F32).reshape(N_KV_HEADS, 1, COLS_PER_KV, HEADS_PER_COL).transpose(0, 3, 1, 2)
    sink_p = jnp.broadcast_to(sink_kch[..., None, None],
                              (N_KV_HEADS, HEADS_PER_COL, 2, COLS_PER_KV, CHUNK, LANES)).reshape(
        N_KV_HEADS, HEADS_PER_COL, 2 * COLS_PER_KV * CHUNK, LANES)
    sink_s = jnp.broadcast_to(sink_f, (N_KV_HEADS, Q_PER_KV, t_new, 1)).reshape(N_KV_HEADS, Q_PER_KV * t_new, 1)
    o_p = _attn_prompt(q, kv, bias_p, sink_p, n_prompt, seq, 2)
    o_s = _attn_sample(q, kv, cache_k.reshape(dec_batch, buf, KV_DIM), cache_v.reshape(dec_batch, buf, KV_DIM),
                       bias_s, sink_s, n_prompt, t_new)

    hist = GROUP_ROWS
    sconv_pad = jnp.pad(sconv, ((0, 0), (hist - sconv.shape[1], 0), (0, 0)))
    w_rt_t = jnp.zeros((ROUTER_ROWS, d), F32)
    w_rt_t = w_rt_t.at[:N_GROUPS].set(w_grp.T).at[EXPERT_ROW0:EXPERT_ROW0 + N_EXPERTS].set(w_rt.T)
    b_rt_t = jnp.zeros((ROUTER_ROWS, 1), F32)
    b_rt_t = b_rt_t.at[:N_GROUPS, 0].set(b_grp.astype(F32)).at[EXPERT_ROW0:EXPERT_ROW0 + N_EXPERTS, 0].set(
        b_rt.astype(F32))
    nch = cdim // CONV_CHUNK
    conv_w_c = conv_w.reshape(CONV_WIDTH, nch, CONV_CHUNK).transpose(1, 0, 2)
    params = dict(conv_w=conv_w_c, conv_b=conv_b.reshape(nch, 1, CONV_CHUNK), ln_g=ln_g.reshape(1, cdim),
                  ln_b=ln_b.reshape(1, cdim), w_pw2=w_pw2.astype(BF16), b_pw2=b_pw2.reshape(1, d),
                  w_ao=w_ao.astype(BF16), w_out=w_out.astype(BF16), n2=n2.reshape(1, d),
                  w_rt=w_rt_t.astype(BF16), b_rt=b_rt_t)
    x1, h2, eidx, wtok, rank, cnt = _post(x_p, x_s, u, sconv_pad, o_p, o_s, gates, gate1, shift2, scale2,
                                             params, n_prompt, seq, tm)

    blk_e, next_e, nused, src, roww, dest = _dispatch(eidx, wtok, rank, cnt[:, 0].astype(jnp.int32), n)
    yb = _moe(h2, blk_e, next_e, nused, src, roww, w_gate, w_up, w_down)
    return x1, yb, dest, gate2, kv_state, u


def kernel(x_prompt, x_sample, c_prompt, c_sample, cache_k, cache_v, state_conv, rel_bias_table, norm1_g, norm2_g, w_ada, b_ada, w_in, attn_sink, w_attn_o, conv_w, conv_b, conv_ln_g, conv_ln_b, w_pw2, b_pw2, w_out, w_group, b_group, w_router, b_router, w_gate, w_up, w_down, final_g):
    batch, seq, d = x_prompt.shape
    dec_batch, t_new, _ = x_sample.shape
    depth = norm1_g.shape[0]
    assert depth == 1, "single trunk layer"
    assert t_new == GROUP_ROWS and seq % GROUP_ROWS == 0
    n_prompt = batch * seq
    n = n_prompt + dec_batch * t_new
    x_p = x_prompt.reshape(n_prompt, d)
    x_s = x_sample.reshape(dec_batch * t_new, d)
    c_all = jnp.concatenate([c_prompt, c_sample], axis=0)
    l = 0
    lp = tuple(a.reshape(a.shape[1:]) for a in (
        norm1_g, norm2_g, w_ada, b_ada, w_in, attn_sink, w_attn_o, conv_w, conv_b, conv_ln_g, conv_ln_b,
        w_pw2, b_pw2, w_out, w_group, b_group, w_router, b_router, w_gate, w_up, w_down))
    x1, yb, dest, gate2, kv_state, u = _layer(x_p, x_s, c_all, cache_k[l], cache_v[l], state_conv[l],
                                              rel_bias_table, lp, seq, t_new)
    out_p, out_s = _final(x1, yb, dest, gate2, final_g, n_prompt, 256)

    y_prompt = out_p.reshape(batch, seq, d)
    y_sample = out_s.reshape(dec_batch, t_new, d)
    cdim = u.shape[1]
    kvp = kv_state[:batch * WINDOW].reshape(batch, WINDOW, 2 * KV_DIM)
    new_k_prompt = kvp[..., :KV_DIM].reshape(1, batch, WINDOW, N_KV_HEADS, HEAD_DIM)
    new_v_prompt = kvp[..., KV_DIM:].reshape(1, batch, WINDOW, N_KV_HEADS, HEAD_DIM)
    new_conv_prompt = jnp.stack([u[(b + 1) * seq - (CONV_WIDTH - 1):(b + 1) * seq] for b in range(batch)])[None]
    kvs = kv_state[batch * WINDOW:].reshape(dec_batch, t_new, 2 * KV_DIM)
    buf = cache_k.shape[2]
    k_new = kvs[..., :KV_DIM].reshape(dec_batch, t_new, N_KV_HEADS, HEAD_DIM)
    v_new = kvs[..., KV_DIM:].reshape(dec_batch, t_new, N_KV_HEADS, HEAD_DIM)
    new_k_sample = jnp.concatenate([cache_k[l], k_new], axis=1)[:, -buf:][None]
    new_v_sample = jnp.concatenate([cache_v[l], v_new], axis=1)[:, -buf:][None]
    us = u[n_prompt:].reshape(dec_batch, t_new, cdim)
    new_conv_sample = jnp.concatenate([state_conv[l], us], axis=1)[:, -(CONV_WIDTH - 1):][None]
    return (y_prompt, y_sample, new_k_prompt, new_v_prompt, new_conv_prompt,
            new_k_sample, new_v_sample, new_conv_sample)
```

```python
import functools
import math

import jax
import jax.numpy as jnp
from jax import lax
from jax.experimental import pallas as pl
from jax.experimental.pallas import tpu as pltpu

F32 = jnp.float32
BF16 = jnp.bfloat16

CHUNK = 64
HEAD_DIM = 64
N_Q_HEADS = 16
N_KV_HEADS = 4
Q_PER_KV = N_Q_HEADS // N_KV_HEADS
ATTN_DIM = N_Q_HEADS * HEAD_DIM
KV_DIM = N_KV_HEADS * HEAD_DIM
WINDOW = 128
WIN_CHUNKS = WINDOW // CHUNK
BAND = (WIN_CHUNKS + 1) * CHUNK
CONV_WIDTH = 31
NUM_BUCKETS = 32
MAX_DISTANCE = 128
N_GROUPS = 4
EXPERTS_PER_GROUP = 8
N_EXPERTS = N_GROUPS * EXPERTS_PER_GROUP
MOE_BLOCK = 256
RMS_EPS = 1e-6
LN_EPS = 1e-5
NEG_INF = -1e30

GROUP_ROWS = 32
LANES = 128
SUBLANES = 8
CONV_CHUNK = 256
ROUTER_ROWS = 64
EXPERT_ROW0 = 8
V7X_VMEM_LIMIT = 56 * 1024 * 1024


def _cparams(*sem):
    return pltpu.CompilerParams(dimension_semantics=sem, vmem_limit_bytes=V7X_VMEM_LIMIT)


def _ada_kernel(c_ref, w_ref, b_ref, o_ref):
    c = c_ref[...]
    s = c * jax.nn.sigmoid(c)
    o_ref[...] = jnp.dot(s.astype(BF16), w_ref[...].astype(BF16),
                         preferred_element_type=F32) + b_ref[...]


def _ada(c_all, w_ada, b_ada):
    rows, d = c_all.shape
    ncol = w_ada.shape[1]
    tn = 1024
    return pl.pallas_call(
        _ada_kernel,
        grid=(ncol // tn,),
        in_specs=[pl.BlockSpec((rows, d), lambda j: (0, 0)),
                  pl.BlockSpec((d, tn), lambda j: (0, j)),
                  pl.BlockSpec((1, tn), lambda j: (0, j))],
        out_specs=pl.BlockSpec((rows, tn), lambda j: (0, j)),
        out_shape=jax.ShapeDtypeStruct((rows, ncol), F32),
        compiler_params=_cparams("arbitrary"),
        name="ada",
    )(c_all, w_ada, b_ada.reshape(1, ncol))


def _rms_mod(x, gain, scale, shift):
    ms = jnp.mean(x * x, axis=-1, keepdims=True)
    y = x * lax.rsqrt(ms + RMS_EPS) * gain
    return y * (1.0 + scale) + shift


def _pack_bf16_pairs(xb):
    w = xb.shape[1] // 2
    bits = lax.bitcast_convert_type(xb.astype(F32), jnp.uint32)
    return (bits[:, :w] >> 16) | (bits[:, w:] & jnp.uint32(0xFFFF0000))


def _unpack_bf16_pairs(p):
    lo = lax.bitcast_convert_type(p << 16, F32)
    hi = lax.bitcast_convert_type(p & jnp.uint32(0xFFFF0000), F32)
    return lo, hi


def _two_stream_specs(tm, width, prompt_tiles):
    return [pl.BlockSpec((tm, width), lambda i: (jnp.minimum(i, prompt_tiles - 1), 0)),
            pl.BlockSpec((tm, width), lambda i: (jnp.maximum(i - prompt_tiles, 0), 0))]


def _norm1_kernel(xp_ref, xs_ref, g_ref, sc_ref, sh_ref, h_ref, *, groups, prompt_tiles):
    def run(x_ref):
        def body(gi, carry):
            r = pl.multiple_of(gi * GROUP_ROWS, GROUP_ROWS)
            h = _rms_mod(x_ref[pl.ds(r, GROUP_ROWS), :], g_ref[...],
                         sc_ref[pl.ds(gi, 1), :], sh_ref[pl.ds(gi, 1), :])
            h_ref[pl.ds(r, GROUP_ROWS), :] = h.astype(h_ref.dtype)
            return carry

        lax.fori_loop(0, groups, body, 0)

    i = pl.program_id(0)
    pl.when(i < prompt_tiles)(lambda: run(xp_ref))
    pl.when(i >= prompt_tiles)(lambda: run(xs_ref))


def _norm1(x_p, x_s, gain, scale_g, shift_g, tm):
    d = x_p.shape[1]
    n = x_p.shape[0] + x_s.shape[0]
    groups = tm // GROUP_ROWS
    prompt_tiles = x_p.shape[0] // tm
    return pl.pallas_call(
        functools.partial(_norm1_kernel, groups=groups, prompt_tiles=prompt_tiles),
        grid=(n // tm,),
        in_specs=_two_stream_specs(tm, d, prompt_tiles) + [
            pl.BlockSpec((1, d), lambda i: (0, 0)),
            pl.BlockSpec((groups, d), lambda i: (i, 0)),
            pl.BlockSpec((groups, d), lambda i: (i, 0))],
        out_specs=pl.BlockSpec((tm, d), lambda i: (i, 0)),
        out_shape=jax.ShapeDtypeStruct((n, d), BF16),
        compiler_params=_cparams("arbitrary"),
        name="norm1",
    )(x_p, x_s, gain.reshape(1, d), scale_g, shift_g)


def _proj_kernel(h_ref, w_ref, *o_refs, mode):
    acc = jnp.dot(h_ref[...], w_ref[...], preferred_element_type=F32)
    if mode == "qkv":
        o_refs[0][...] = acc.astype(BF16)
    elif mode == "glu":
        half = acc.shape[1] // 2
        o_refs[0][...] = acc[:, :half] * jax.nn.sigmoid(acc[:, half:])
    else:
        o_refs[0][...] = jax.nn.sigmoid(acc).astype(BF16)


def _proj(h, w, mode, tm, tnw, out_dtypes):
    n, d = h.shape
    ncol = w.shape[1]
    tno = tnw // 2 if mode == "glu" else tnw
    nout = ncol // 2 if mode == "glu" else ncol
    outs = pl.pallas_call(
        functools.partial(_proj_kernel, mode=mode),
        grid=(n // tm, ncol // tnw),
        in_specs=[pl.BlockSpec((tm, d), lambda i, j: (i, 0)),
                  pl.BlockSpec((d, tnw), lambda i, j: (0, j))],
        out_specs=[pl.BlockSpec((tm, tno), lambda i, j: (i, j)) for _ in out_dtypes],
        out_shape=[jax.ShapeDtypeStruct((n, nout), dt) for dt in out_dtypes],
        compiler_params=_cparams("arbitrary", "arbitrary"),
        name="proj_" + mode,
    )(h, w)
    return outs


def _kv_state_kernel(h_ref, w_ref, o_ref):
    o_ref[...] = jnp.dot(h_ref[...], w_ref[...], preferred_element_type=F32)


def _kv_state(h, w_kv, n_prompt, seq):
    n, d = h.shape
    batch = n_prompt // seq
    per_seq = seq // WINDOW
    steps = batch + (n - n_prompt) // WINDOW

    def row_block(i):
        return jnp.where(i < batch, (i + 1) * per_seq - 1, n_prompt // WINDOW + i - batch)

    return pl.pallas_call(
        _kv_state_kernel,
        grid=(steps,),
        in_specs=[pl.BlockSpec((WINDOW, d), lambda i: (row_block(i), 0)),
                  pl.BlockSpec((d, 2 * KV_DIM), lambda i: (0, 0))],
        out_specs=pl.BlockSpec((WINDOW, 2 * KV_DIM), lambda i: (i, 0)),
        out_shape=jax.ShapeDtypeStruct((steps * WINDOW, 2 * KV_DIM), F32),
        compiler_params=_cparams("arbitrary"),
        name="kv_state",
    )(h, w_kv)


def _t5_bucket(rel):
    nb = NUM_BUCKETS // 2
    n = -rel
    ret = jnp.where(n < 0, nb, 0)
    n = jnp.abs(n)
    max_exact = nb // 2
    nf = jnp.maximum(n, 1).astype(F32)
    large = max_exact + (jnp.log(nf / max_exact) / math.log(MAX_DISTANCE / max_exact)
                         * (nb - max_exact)).astype(jnp.int32)
    large = jnp.minimum(large, nb - 1)
    return ret + jnp.where(n < max_exact, n, large)


def _bias_kernel(tbl_ref, bkt_ref, o_ref):
    bkt = bkt_ref[...]
    nq = bkt.shape[0]
    for k in range(N_KV_HEADS):
        for g in range(Q_PER_KV):
            acc = jnp.zeros(bkt.shape, F32)
            for b in range(NUM_BUCKETS):
                acc = jnp.where(bkt == b, tbl_ref[b, k * Q_PER_KV + g], acc)
            o_ref[k, g * nq:(g + 1) * nq, :] = acc


def _rel_bias(rel_table, rel):
    nq, nk = rel.shape
    return pl.pallas_call(
        _bias_kernel,
        in_specs=[pl.BlockSpec(memory_space=pltpu.SMEM),
                  pl.BlockSpec((nq, nk), lambda: (0, 0))],
        out_specs=pl.BlockSpec((N_KV_HEADS, Q_PER_KV * nq, nk), lambda: (0, 0, 0)),
        out_shape=jax.ShapeDtypeStruct((N_KV_HEADS, Q_PER_KV * nq, nk), F32),
        name="rel_bias",
    )(rel_table.astype(F32), _t5_bucket(rel))


PAIR_BAND = 2 * CHUNK + WINDOW
HEADS_PER_COL = LANES // HEAD_DIM
COLS_PER_KV = Q_PER_KV // HEADS_PER_COL


def _pair_bias_kernel(tbl_ref, bkt_ref, o_ref):
    for e in range(2):
        bkt = bkt_ref[e]
        for k in range(N_KV_HEADS):
            for col in range(COLS_PER_KV):
                for half in range(HEADS_PER_COL):
                    head = k * Q_PER_KV + col * HEADS_PER_COL + half
                    acc = jnp.full(bkt.shape, NEG_INF, F32)
                    for b in range(NUM_BUCKETS):
                        acc = jnp.where(bkt == b, tbl_ref[b, head], acc)
                    r0 = (e * COLS_PER_KV + col) * CHUNK
                    o_ref[k, r0:r0 + CHUNK, half * PAIR_BAND:(half + 1) * PAIR_BAND] = acc


def _pair_bias(rel_table):
    qi = jnp.arange(CHUNK, dtype=jnp.int32)[None, :, None]
    kj = jnp.arange(PAIR_BAND, dtype=jnp.int32)[None, None, :]
    e = jnp.arange(2, dtype=jnp.int32)[:, None, None]
    rel = (kj - WINDOW) - (e * CHUNK + qi)
    key_chunk = kj // CHUNK - e
    seen = (key_chunk >= 0) & (key_chunk <= WIN_CHUNKS)
    bkt = jnp.where(seen, _t5_bucket(rel), -1)
    rows = 2 * COLS_PER_KV * CHUNK
    return pl.pallas_call(
        _pair_bias_kernel,
        in_specs=[pl.BlockSpec(memory_space=pltpu.SMEM),
                  pl.BlockSpec((2, CHUNK, PAIR_BAND), lambda: (0, 0, 0))],
        out_specs=pl.BlockSpec((N_KV_HEADS, rows, HEADS_PER_COL * PAIR_BAND), lambda: (0, 0, 0)),
        out_shape=jax.ShapeDtypeStruct((N_KV_HEADS, rows, HEADS_PER_COL * PAIR_BAND), F32),
        name="pair_bias",
    )(rel_table.astype(F32), bkt)


def _attend(qc, kk_all, vv_all, bias_ref, sink_ref, mask_thr):
    nq = qc.shape[0]
    nk = kk_all.shape[0]
    pieces = []
    for k in range(N_KV_HEADS):
        qs = jnp.concatenate(
            [qc[:, (k * Q_PER_KV + g) * HEAD_DIM:(k * Q_PER_KV + g + 1) * HEAD_DIM]
             for g in range(Q_PER_KV)], axis=0)
        kk = kk_all[:, k * HEAD_DIM:(k + 1) * HEAD_DIM]
        vv = vv_all[:, k * HEAD_DIM:(k + 1) * HEAD_DIM]
        lg = lax.dot_general(qs, kk, (((1,), (1,)), ((), ())),
                             preferred_element_type=F32) * (HEAD_DIM ** -0.5) + bias_ref[k]
        if mask_thr is not None:
            col = lax.broadcasted_iota(jnp.int32, (Q_PER_KV * nq, nk), 1)
            lg = jnp.where(col < mask_thr, NEG_INF, lg)
        s = sink_ref[k]
        m = jnp.maximum(jnp.max(lg, axis=-1, keepdims=True), s)
        p = jnp.exp(lg - m)
        den = jnp.sum(p, axis=-1, keepdims=True) + jnp.exp(s - m)
        o = jnp.dot(p.astype(BF16), vv, preferred_element_type=F32) / den
        pieces.extend(o[g * nq:(g + 1) * nq, :] for g in range(Q_PER_KV))
    return jnp.concatenate(pieces, axis=1)


def _attn_prompt_kernel(q_ref, kv_ref, halo_ref, bias_ref, sink_ref, o_ref, ka, kb, va, vb, *, pairs):
    t = pl.program_id(1)
    kvcat = jnp.concatenate([halo_ref[...], kv_ref[...]], axis=0)
    low = lax.broadcasted_iota(jnp.int32, (kvcat.shape[0], LANES), 1) < HEAD_DIM
    zero = jnp.zeros((kvcat.shape[0], LANES), BF16)
    ones_low = jnp.where(low, 1.0, 0.0).astype(BF16)
    ones_high = jnp.where(low, 0.0, 1.0).astype(BF16)
    kv_cols = KV_DIM // LANES
    for col in range(2 * kv_cols):
        x = kvcat[:, col * LANES:(col + 1) * LANES]
        xs = jnp.concatenate([x[:, HEAD_DIM:], x[:, :HEAD_DIM]], axis=1)
        dst_a, dst_b = (ka, kb) if col < kv_cols else (va, vb)
        k0 = (col % kv_cols) * HEADS_PER_COL
        dst_a[k0, :, 0:LANES] = jnp.where(low, x, zero)
        dst_b[k0, :, 0:LANES] = jnp.where(low, zero, xs)
        dst_a[k0 + 1, :, 0:LANES] = jnp.where(low, xs, zero)
        dst_b[k0 + 1, :, 0:LANES] = jnp.where(low, zero, x)
    for k in range(N_KV_HEADS):
        va[k, :, LANES:2 * LANES] = ones_low
        vb[k, :, LANES:2 * LANES] = ones_high

    out_low = lax.broadcasted_iota(jnp.int32, (2 * COLS_PER_KV * CHUNK, LANES), 1) < HEAD_DIM
    for p in range(pairs):
        r0 = p * 2 * CHUNK
        for k in range(N_KV_HEADS):
            lhs = jnp.concatenate(
                [q_ref[r0 + e * CHUNK:r0 + (e + 1) * CHUNK, (k * COLS_PER_KV + c) * LANES:(k * COLS_PER_KV + c + 1) * LANES]
                 for e in range(2) for c in range(COLS_PER_KV)], axis=0)
            keys = jnp.concatenate([ka[k, r0:r0 + PAIR_BAND, :], kb[k, r0:r0 + PAIR_BAND, :]], axis=0)
            lg = lax.dot_general(lhs, keys, (((1,), (1,)), ((), ())),
                                 preferred_element_type=F32) * (HEAD_DIM ** -0.5) + bias_ref[k]
            probs, sink_terms = [], []
            for half in range(HEADS_PER_COL):
                seg = lg[:, half * PAIR_BAND:(half + 1) * PAIR_BAND]
                if p == 0:
                    kcol = lax.broadcasted_iota(jnp.int32, seg.shape, 1)
                    seg = jnp.where(kcol < jnp.where(t == 0, WINDOW, 0), NEG_INF, seg)
                s = sink_ref[k, half]
                folded = seg[:, :LANES]
                for j in range(1, PAIR_BAND // LANES):
                    folded = jnp.maximum(folded, seg[:, j * LANES:(j + 1) * LANES])
                m = jnp.maximum(jnp.max(folded, axis=-1, keepdims=True), s)
                pr = jnp.exp(seg - jnp.concatenate([m] * (PAIR_BAND // LANES), axis=1))
                sink_terms.append(jnp.exp(s - m))
                probs.append(pr.astype(BF16))
            vals = jnp.concatenate([va[k, r0:r0 + PAIR_BAND, :], vb[k, r0:r0 + PAIR_BAND, :]], axis=0)
            oe = jnp.dot(jnp.concatenate(probs, axis=1), vals, preferred_element_type=F32)
            o = oe[:, :LANES] / (oe[:, LANES:] + jnp.where(out_low, sink_terms[0], sink_terms[1]))
            for e in range(2):
                for c in range(COLS_PER_KV):
                    rr = (e * COLS_PER_KV + c) * CHUNK
                    o_ref[r0 + e * CHUNK:r0 + (e + 1) * CHUNK,
                          (k * COLS_PER_KV + c) * LANES:(k * COLS_PER_KV + c + 1) * LANES] = (
                        o[rr:rr + CHUNK, :].astype(o_ref.dtype))


def _attn_prompt(q, kv, bias, sink_rows, n_prompt, seq, pairs):
    rows = pairs * 2 * CHUNK
    tiles = seq // rows
    halo_per_tile = rows // WINDOW
    keys = WINDOW + rows
    return pl.pallas_call(
        functools.partial(_attn_prompt_kernel, pairs=pairs),
        grid=(n_prompt // seq, tiles),
        scratch_shapes=[pltpu.VMEM((N_KV_HEADS, keys, LANES), BF16) for _ in range(2)]
        + [pltpu.VMEM((N_KV_HEADS, keys, 2 * LANES), BF16) for _ in range(2)],
        in_specs=[pl.BlockSpec((rows, ATTN_DIM), lambda b, t: (b * tiles + t, 0)),
                  pl.BlockSpec((rows, 2 * KV_DIM), lambda b, t: (b * tiles + t, 0)),
                  pl.BlockSpec((WINDOW, 2 * KV_DIM),
                               lambda b, t: (jnp.maximum((b * tiles + t) * halo_per_tile - 1, 0), 0)),
                  pl.BlockSpec(bias.shape, lambda b, t: (0, 0, 0)),
                  pl.BlockSpec(sink_rows.shape, lambda b, t: (0, 0, 0, 0))],
        out_specs=pl.BlockSpec((rows, ATTN_DIM), lambda b, t: (b * tiles + t, 0)),
        out_shape=jax.ShapeDtypeStruct((n_prompt, ATTN_DIM), BF16),
        compiler_params=_cparams("arbitrary", "arbitrary"),
        name="attn_prompt",
    )(q, kv, kv, bias, sink_rows)


def _attn_sample_kernel(q_ref, kv_ref, ck_ref, cv_ref, bias_ref, sink_ref, o_ref):
    kv = kv_ref[...]
    kk = jnp.concatenate([ck_ref[0].astype(BF16), kv[:, :KV_DIM]], axis=0)
    vv = jnp.concatenate([cv_ref[0].astype(BF16), kv[:, KV_DIM:]], axis=0)
    o = _attend(q_ref[...], kk, vv, bias_ref, sink_ref, None)
    o_ref[...] = o.astype(o_ref.dtype)


def _attn_sample(q, kv, cache_k, cache_v, bias, sink_rows, n_prompt, t_new):
    dec_batch, buf, _ = cache_k.shape
    first = n_prompt // t_new
    return pl.pallas_call(
        _attn_sample_kernel,
        grid=(dec_batch,),
        in_specs=[pl.BlockSpec((t_new, ATTN_DIM), lambda s: (first + s, 0)),
                  pl.BlockSpec((t_new, 2 * KV_DIM), lambda s: (first + s, 0)),
                  pl.BlockSpec((1, buf, KV_DIM), lambda s: (s, 0, 0)),
                  pl.BlockSpec((1, buf, KV_DIM), lambda s: (s, 0, 0)),
                  pl.BlockSpec(bias.shape, lambda s: (0, 0, 0)),
                  pl.BlockSpec(sink_rows.shape, lambda s: (0, 0, 0))],
        out_specs=pl.BlockSpec((t_new, ATTN_DIM), lambda s: (s, 0)),
        out_shape=jax.ShapeDtypeStruct((dec_batch * t_new, ATTN_DIM), BF16),
        compiler_params=_cparams("arbitrary"),
        name="attn_sample",
    )(q, kv, cache_k, cache_v, bias, sink_rows)


def _route(lt, eidx_ref, wtok_ref, rank_ref, cnt_ref, cnt_scr):
    gl = [lt[r:r + 1, :] for r in range(N_GROUPS)]
    gmax = gl[0]
    gsel = jnp.zeros(gl[0].shape, jnp.int32)
    for r in range(1, N_GROUPS):
        better = gl[r] > gmax
        gsel = jnp.where(better, r, gsel)
        gmax = jnp.maximum(gmax, gl[r])
    gexp = [jnp.exp(v - gmax) for v in gl]
    gsum = gexp[0]
    for r in range(1, N_GROUPS):
        gsum = gsum + gexp[r]
    psel = jnp.zeros(gl[0].shape, F32)
    for r in range(N_GROUPS):
        psel = jnp.where(gsel == r, gexp[r] / gsum, psel)
    el = jnp.zeros((EXPERTS_PER_GROUP, lt.shape[1]), F32)
    for r in range(N_GROUPS):
        lo = EXPERT_ROW0 + r * EXPERTS_PER_GROUP
        el = jnp.where(gsel == r, lt[lo:lo + EXPERTS_PER_GROUP, :], el)
    emax = jnp.max(el, axis=0, keepdims=True)
    ee = jnp.exp(el - emax)
    pin = ee / jnp.sum(ee, axis=0, keepdims=True)
    idx = lax.broadcasted_iota(jnp.int32, pin.shape, 0)
    p1 = jnp.max(pin, axis=0, keepdims=True)
    i1 = jnp.min(jnp.where(pin == p1, idx, EXPERTS_PER_GROUP), axis=0, keepdims=True)
    rest = jnp.where(idx == i1, -1.0, pin)
    p2 = jnp.max(rest, axis=0, keepdims=True)
    i2 = jnp.min(jnp.where(rest == p2, idx, EXPERTS_PER_GROUP), axis=0, keepdims=True)
    tot = p1 + p2
    e1 = gsel * EXPERTS_PER_GROUP + i1
    e2 = gsel * EXPERTS_PER_GROUP + i2
    eidx_ref[0:1, :] = e1
    eidx_ref[1:2, :] = e2
    wtok_ref[0:1, :] = psel * p1 / tot
    wtok_ref[1:2, :] = psel * p2 / tot

    t = lt.shape[1]
    eiota = lax.broadcasted_iota(jnp.int32, (N_EXPERTS, t), 0)
    oh1 = (eiota == e1).astype(F32)
    oh2 = (eiota == e2).astype(F32)
    both = oh1 + oh2
    before = (lax.broadcasted_iota(jnp.int32, (t, t), 0)
              < lax.broadcasted_iota(jnp.int32, (t, t), 1)).astype(BF16)
    prior = jnp.dot(both.astype(BF16), before, preferred_element_type=F32) + cnt_scr[:, 0:1]
    rank_ref[0:1, :] = jnp.sum(oh1 * prior, axis=0, keepdims=True).astype(jnp.int32)
    rank_ref[1:2, :] = jnp.sum(oh2 * prior, axis=0, keepdims=True).astype(jnp.int32)
    cnt_scr[...] = cnt_scr[...] + jnp.sum(both, axis=1, keepdims=True)
    cnt_ref[...] = cnt_scr[...]


def _post_kernel(xp_ref, xs_ref, u_ref, uhalo_ref, sconv_ref, op_ref, os_ref, gates_ref,
                 gate1_ref, shift2_ref, scale2_ref,
                 cw_ref, cb_ref, lng_ref, lnb_ref, wpw2_ref, bpw2_ref, wao_ref, wout_ref,
                 n2_ref, wrt_ref, brt_ref,
                 x1_ref, h2_ref, eidx_ref, wtok_ref, rank_ref, cnt_ref,
                 uext, shift_scr, d_scr, s_scr, o_scr, mix_scr, h2b_scr, cnt_scr,
                 *, groups, prompt_tiles, tiles_per_seq):
    i = pl.program_id(0)
    d_model = xp_ref.shape[1]

    @pl.when(i == 0)
    def _():
        cnt_scr[...] = jnp.zeros(cnt_scr.shape, F32)
    nch = cw_ref.shape[0]
    cw = uext.shape[2]
    hist = uext.shape[1] - GROUP_ROWS
    lead = hist - (CONV_WIDTH - 1)

    def put_hist(g, rows):
        for c in range(nch):
            uext[g * nch + c, 0:hist, :] = rows[:, c * cw:(c + 1) * cw]

    @pl.when(i < prompt_tiles)
    def _():
        first = (i % tiles_per_seq) == 0
        put_hist(0, jnp.where(first, 0.0, uhalo_ref[...]))
        for g in range(1, groups):
            put_hist(g, u_ref[(g - 1) * GROUP_ROWS:g * GROUP_ROWS, :])
        o_scr[...] = op_ref[...]

    @pl.when(i >= prompt_tiles)
    def _():
        for g in range(groups):
            put_hist(g, sconv_ref[g])
        o_scr[...] = os_ref[...]

    for g in range(groups):
        for c in range(nch):
            uext[g * nch + c, hist:hist + GROUP_ROWS, :] = (
                u_ref[g * GROUP_ROWS:(g + 1) * GROUP_ROWS, c * cw:(c + 1) * cw])

    span = shift_scr.shape[1]

    def conv_chunk(k, carry):
        c = k % nch
        win_all = uext[k]
        for s in range(1, SUBLANES):
            shift_scr[s, :, :] = pltpu.roll(win_all, win_all.shape[0] - s, 0)[0:span, :]
        acc = None
        for j in range(CONV_WIDTH):
            base, s = divmod(lead + j, SUBLANES)
            rows = pl.ds(base * SUBLANES, GROUP_ROWS)
            win = uext[k, rows, :] if s == 0 else shift_scr[s, rows, :]
            term = win * cw_ref[c, j:j + 1, :]
            acc = term if acc is None else acc + term
        d_scr[k] = acc + cb_ref[c]
        return carry

    lax.fori_loop(0, groups * nch, conv_chunk, 0)

    for g in range(groups):
        dd = jnp.concatenate([d_scr[g * nch + c] for c in range(nch)], axis=1)
        mu = jnp.mean(dd, axis=-1, keepdims=True)
        var = jnp.mean(jnp.square(dd - mu), axis=-1, keepdims=True)
        y = (dd - mu) * lax.rsqrt(var + LN_EPS) * lng_ref[...] + lnb_ref[...]
        s_scr[g * GROUP_ROWS:(g + 1) * GROUP_ROWS, :] = (y * jax.nn.sigmoid(y)).astype(BF16)

    conv_out = jnp.dot(s_scr[...], wpw2_ref[...], preferred_element_type=F32) + bpw2_ref[...]
    attn_out = jnp.dot(o_scr[...], wao_ref[...], preferred_element_type=F32)
    merged = (gates_ref[:, :d_model].astype(F32) * attn_out
              + gates_ref[:, d_model:].astype(F32) * conv_out)
    mix_scr[...] = jnp.dot(merged.astype(BF16), wout_ref[...], preferred_element_type=F32)

    def residual(x_ref):
        def res_group(g, carry):
            r = pl.multiple_of(g * GROUP_ROWS, GROUP_ROWS)
            x1 = (x_ref[pl.ds(r, GROUP_ROWS), :]
                  + gate1_ref[pl.ds(g, 1), :] * mix_scr[pl.ds(r, GROUP_ROWS), :])
            x1_ref[pl.ds(r, GROUP_ROWS), :] = x1
            h = _rms_mod(x1, n2_ref[...], scale2_ref[pl.ds(g, 1), :], shift2_ref[pl.ds(g, 1), :])
            hb = h.astype(BF16)
            h2b_scr[pl.ds(r, GROUP_ROWS), :] = hb
            h2_ref[pl.ds(r, GROUP_ROWS), :] = _pack_bf16_pairs(hb)
            return carry

        lax.fori_loop(0, groups, res_group, 0)

    pl.when(i < prompt_tiles)(lambda: residual(xp_ref))
    pl.when(i >= prompt_tiles)(lambda: residual(xs_ref))

    lt = lax.dot_general(wrt_ref[...], h2b_scr[...], (((1,), (1,)), ((), ())),
                         preferred_element_type=F32) + brt_ref[...]
    _route(lt, eidx_ref, wtok_ref, rank_ref, cnt_ref, cnt_scr)


def _post(x_p, x_s, u, sconv_pad, o_p, o_s, gates, gate1_g, shift2_g, scale2_g, p, n_prompt, seq, tm):
    d = x_p.shape[1]
    n = x_p.shape[0] + x_s.shape[0]
    cdim = u.shape[1]
    nch = cdim // CONV_CHUNK
    groups = tm // GROUP_ROWS
    prompt_tiles = n_prompt // tm
    const = lambda shape: pl.BlockSpec(shape, lambda i: (0,) * len(shape))
    row = lambda w: pl.BlockSpec((tm, w), lambda i: (i, 0))
    grp = pl.BlockSpec((groups, d), lambda i: (i, 0))
    return pl.pallas_call(
        functools.partial(_post_kernel, groups=groups, prompt_tiles=prompt_tiles,
                          tiles_per_seq=seq // tm),
        grid=(n // tm,),
        in_specs=_two_stream_specs(tm, d, prompt_tiles) + [
                  row(cdim),
                  pl.BlockSpec((GROUP_ROWS, cdim), lambda i: (jnp.maximum(i * groups - 1, 0), 0)),
                  pl.BlockSpec((groups, GROUP_ROWS, cdim),
                               lambda i: (jnp.maximum(i - prompt_tiles, 0), 0, 0))]
                 + _two_stream_specs(tm, ATTN_DIM, prompt_tiles) + [
                  row(2 * d), grp, grp, grp,
                  const((nch, CONV_WIDTH, CONV_CHUNK)), const((nch, 1, CONV_CHUNK)),
                  const((1, cdim)), const((1, cdim)),
                  const((cdim, d)), const((1, d)), const((ATTN_DIM, d)), const((d, d)),
                  const((1, d)), const((ROUTER_ROWS, d)), const((ROUTER_ROWS, 1))],
        out_specs=[row(d),
                   row(d // 2),
                   pl.BlockSpec((2, tm), lambda i: (0, i)),
                   pl.BlockSpec((2, tm), lambda i: (0, i)),
                   pl.BlockSpec((2, tm), lambda i: (0, i)),
                   pl.BlockSpec((N_EXPERTS, LANES), lambda i: (0, 0))],
        out_shape=[jax.ShapeDtypeStruct((n, d), F32),
                   jax.ShapeDtypeStruct((n, d // 2), jnp.uint32),
                   jax.ShapeDtypeStruct((2, n), jnp.int32),
                   jax.ShapeDtypeStruct((2, n), F32),
                   jax.ShapeDtypeStruct((2, n), jnp.int32),
                   jax.ShapeDtypeStruct((N_EXPERTS, LANES), F32)],
        scratch_shapes=[pltpu.VMEM((groups * nch, 2 * GROUP_ROWS, CONV_CHUNK), F32),
                        pltpu.VMEM((SUBLANES, 2 * GROUP_ROWS - SUBLANES, CONV_CHUNK), F32),
                        pltpu.VMEM((groups * nch, GROUP_ROWS, CONV_CHUNK), F32),
                        pltpu.VMEM((tm, cdim), BF16),
                        pltpu.VMEM((tm, ATTN_DIM), BF16),
                        pltpu.VMEM((tm, d), F32),
                        pltpu.VMEM((tm, d), BF16),
                        pltpu.VMEM((N_EXPERTS, LANES), F32)],
        compiler_params=_cparams("arbitrary"),
        name="post",
    )(x_p, x_s, u, u, sconv_pad, o_p, o_s, gates, gate1_g, shift2_g, scale2_g,
      p["conv_w"], p["conv_b"], p["ln_g"], p["ln_b"], p["w_pw2"], p["b_pw2"], p["w_ao"], p["w_out"],
      p["n2"], p["w_rt"], p["b_rt"])


TABLE_GROUP = 8
MOE_SEGMENTS = 8
GATHER_AHEAD = 4


def _grouped_tables(tab, group=TABLE_GROUP, ahead=1):
    steps, w = tab.shape
    assert steps % group == 0
    padded = jnp.concatenate([tab] + [tab[-1:]] * ahead, axis=0)
    cols = [tab.reshape(steps // group, group * w)] + [padded[group + a::group][:steps // group]
                                                         for a in range(ahead)]
    return jnp.concatenate(cols, axis=1).reshape(steps // group, 1, (group + ahead) * w)

def _moe_kernel(blk_e_ref, next_e_ref, nused_ref, src_ref, roww_ref,
                h2_hbm, wg_hbm, wu_hbm, wd_hbm, y_ref,
                xbuf, x16, hg_scr, hu_scr, hid_scr, y_scr, wg_f, wu_f, wd_f, wg_b, wu_b, wd_b, gsem, wsem):
    i = pl.program_id(0)
    nused = nused_ref[0]
    nslots = GATHER_AHEAD + 1
    slot = i % nslots
    base = (i % TABLE_GROUP) * MOE_BLOCK
    weights = ((wg_hbm, wg_f, wg_b), (wu_hbm, wu_f, wu_b), (wd_hbm, wd_f, wd_b))

    def gather(first, sl, r0=0, r1=MOE_BLOCK):
        for r in range(r0, r1):
            pltpu.make_async_copy(h2_hbm.at[pl.ds(src_ref[0, 0, first + r], 1), :],
                                  xbuf.at[sl, pl.ds(r, 1), :], gsem.at[sl]).start(priority=r % 2)

    def gather_wait(sl):
        pltpu.make_async_copy(h2_hbm.at[pl.ds(0, MOE_BLOCK), :], xbuf.at[sl], gsem.at[sl]).wait()

    def weights_start(e):
        for hbm, f32_buf, _ in weights:
            pltpu.make_async_copy(hbm.at[e], f32_buf, wsem.at[0]).start()

    def weights_wait():
        for hbm, f32_buf, _ in weights:
            pltpu.make_async_copy(hbm.at[0], f32_buf, wsem.at[0]).wait()

    @pl.when(i == 0)
    def _():
        weights_start(blk_e_ref[0])
        for a in range(GATHER_AHEAD):
            gather(base + a * MOE_BLOCK, a)

    @pl.when(i >= nused)
    def _():
        y_ref[...] = jnp.zeros(y_ref.shape, y_ref.dtype)

    @pl.when(i < nused)
    def _():
        changed = jnp.logical_or(i == 0, blk_e_ref[i] != blk_e_ref[jnp.maximum(i - 1, 0)])

        @pl.when(changed)
        def _():
            weights_wait()
            for _, f32_buf, b16_buf in weights:
                b16_buf[...] = f32_buf[...].astype(BF16)
            weights_start(next_e_ref[i])

        gather_wait(slot)
        x_lo, x_hi = _unpack_bf16_pairs(xbuf[slot])
        x16[:, :x_lo.shape[1]] = x_lo.astype(BF16)
        x16[:, x_lo.shape[1]:] = x_hi.astype(BF16)

        one = jnp.minimum(nused, 1)
        de = wg_b.shape[1]
        dm = wd_b.shape[1]
        per = MOE_BLOCK // MOE_SEGMENTS
        ahead_slot = (i + GATHER_AHEAD) % nslots
        ahead_base = base + GATHER_AHEAD * MOE_BLOCK

        def segment(k, body):
            def run(_, carry):
                gather(ahead_base, ahead_slot, k * per, (k + 1) * per)
                body()
                return carry
            lax.fori_loop(0, one, run, 0)

        def up(dst, w_b, cols):
            dst[:, cols] = jnp.dot(x16[...], w_b[:, cols], preferred_element_type=F32)

        half_e = de // 2
        for k in range(4):
            cols = slice((k % 2) * half_e, (k % 2 + 1) * half_e)
            dst, w_b = (hg_scr, wg_b) if k < 2 else (hu_scr, wu_b)
            segment(k, functools.partial(up, dst, w_b, cols))

        quarter = dm // 4

        def down(q):
            cols = slice(q * quarter, (q + 1) * quarter)
            if q == 0:
                hg = hg_scr[...]
                hid_scr[...] = (hg * jax.nn.sigmoid(hg) * hu_scr[...]).astype(BF16)
            yq = jnp.dot(hid_scr[...], wd_b[:, cols], preferred_element_type=F32) * roww_ref[...]
            if q < 2:
                y_scr[:, cols] = yq
            else:
                lo = y_scr[:, (q - 2) * quarter:(q - 1) * quarter]
                y_ref[:, (q - 2) * quarter:(q - 1) * quarter] = _pack_bf16_pairs(
                    jnp.concatenate([lo, yq], axis=1).astype(BF16))

        for q in range(4):
            segment(4 + q, lambda q=q: down(q))

        @pl.when(i == nused - 1)
        def _():
            weights_wait()
            for a in range(1, GATHER_AHEAD + 1):
                gather_wait((i + a) % nslots)


def _moe(h2, blk_e, next_e, nused, src, roww, w_gate, w_up, w_down):
    n_blocks = blk_e.shape[0]
    n_exp, d, de = w_gate.shape
    src_tables = _grouped_tables(src.reshape(n_blocks, MOE_BLOCK), ahead=GATHER_AHEAD)
    grid_spec = pltpu.PrefetchScalarGridSpec(
        num_scalar_prefetch=3,
        grid=(n_blocks,),
        in_specs=[pl.BlockSpec((1, 1, src_tables.shape[2]), lambda i, be, ne, nu: (i // TABLE_GROUP, 0, 0),
                               memory_space=pltpu.SMEM),
                  pl.BlockSpec((MOE_BLOCK, 1), lambda i, be, ne, nu: (i, 0)),
                  pl.BlockSpec(memory_space=pl.ANY),
                  pl.BlockSpec(memory_space=pl.ANY),
                  pl.BlockSpec(memory_space=pl.ANY),
                  pl.BlockSpec(memory_space=pl.ANY)],
        out_specs=pl.BlockSpec((MOE_BLOCK, d // 2), lambda i, be, ne, nu: (i, 0)),
        scratch_shapes=[pltpu.VMEM((GATHER_AHEAD + 1, MOE_BLOCK, d // 2), jnp.uint32),
                        pltpu.VMEM((MOE_BLOCK, d), BF16),
                        pltpu.VMEM((MOE_BLOCK, de), F32), pltpu.VMEM((MOE_BLOCK, de), F32),
                        pltpu.VMEM((MOE_BLOCK, de), BF16), pltpu.VMEM((MOE_BLOCK, d // 2), F32),
                        pltpu.VMEM((d, de), F32), pltpu.VMEM((d, de), F32), pltpu.VMEM((de, d), F32),
                        pltpu.VMEM((d, de), BF16), pltpu.VMEM((d, de), BF16), pltpu.VMEM((de, d), BF16),
                        pltpu.SemaphoreType.DMA((GATHER_AHEAD + 1,)), pltpu.SemaphoreType.DMA((1,))],
    )
    return pl.pallas_call(
        _moe_kernel,
        grid_spec=grid_spec,
        out_shape=jax.ShapeDtypeStruct((n_blocks * MOE_BLOCK, d // 2), jnp.uint32),
        compiler_params=_cparams("arbitrary"),
        name="moe",
    )(blk_e, next_e, nused, src_tables, roww, h2, w_gate, w_up, w_down)


def _dispatch(eidx, wtok, rank, counts, n):
    a_tot = 2 * n
    experts = jnp.arange(N_EXPERTS, dtype=jnp.int32)
    padded = (counts + MOE_BLOCK - 1) // MOE_BLOCK * MOE_BLOCK
    pad_end = jnp.sum(jnp.where(experts[None, :] <= experts[:, None], padded[None, :], 0), axis=1)
    pad_start = pad_end - padded
    dest = jnp.sum(jnp.where(eidx[:, :, None] == experts, pad_start, 0), axis=-1) + rank
    n_blocks = -(-(a_tot + N_EXPERTS * (MOE_BLOCK - 1)) // MOE_BLOCK)
    n_blocks = -(-n_blocks // TABLE_GROUP) * TABLE_GROUP
    n_rows = n_blocks * MOE_BLOCK
    tok = jnp.tile(jnp.arange(n, dtype=jnp.int32), 2)
    upd = jnp.stack([tok, lax.bitcast_convert_type(wtok.reshape(-1), jnp.int32)], axis=1)
    rows = jnp.zeros((n_rows, 2), jnp.int32).at[dest.reshape(-1)].set(upd, unique_indices=True)
    blk_start = jnp.arange(n_blocks, dtype=jnp.int32) * MOE_BLOCK
    blk_e = jnp.minimum(jnp.sum((pad_end[None, :] <= blk_start[:, None]).astype(jnp.int32), axis=1),
                        N_EXPERTS - 1)
    nused = (pad_end[-1] // MOE_BLOCK).reshape(1)
    roww = lax.bitcast_convert_type(rows[:, 1], F32)
    later = jnp.where((counts > 0)[None, :] & (experts[None, :] > experts[:, None]), experts[None, :], N_EXPERTS)
    next_of = jnp.min(later, axis=1)
    next_of = jnp.where(next_of < N_EXPERTS, next_of, experts)
    next_e = jnp.sum(jnp.where(blk_e[:, None] == experts[None, :], next_of[None, :], 0), axis=1)
    return blk_e, next_e, nused, rows[:, 0].reshape(n_blocks, 1, MOE_BLOCK), roww.reshape(n_rows, 1), dest


def _final_kernel(d0_ref, d1_ref, x1_ref, gate2_ref, fg_ref, yb_hbm,
                  op_ref, os_ref, ybuf, sem, *, groups, prompt_tiles, tiles, table_group):
    i = pl.program_id(0)
    slot = i % 2
    tm = x1_ref.shape[0]
    base = (i % table_group) * tm

    def gather(first, sl):
        for r in range(tm):
            pltpu.make_async_copy(yb_hbm.at[pl.ds(d0_ref[0, 0, first + r], 1), :],
                                  ybuf.at[sl, 0, pl.ds(r, 1), :], sem.at[sl]).start(priority=0)
            pltpu.make_async_copy(yb_hbm.at[pl.ds(d1_ref[0, 0, first + r], 1), :],
                                  ybuf.at[sl, 1, pl.ds(r, 1), :], sem.at[sl]).start(priority=1)

    def gather_wait(sl):
        for k in range(2):
            pltpu.make_async_copy(yb_hbm.at[pl.ds(0, tm), :], ybuf.at[sl, k], sem.at[sl]).wait()

    @pl.when(i == 0)
    def _():
        gather(base, 0)

    gather_wait(slot)
    gather(base + tm, 1 - slot)

    def run(o_ref):
        def body(g, carry):
            r = pl.multiple_of(g * GROUP_ROWS, GROUP_ROWS)
            lo0, hi0 = _unpack_bf16_pairs(ybuf[slot, 0, pl.ds(r, GROUP_ROWS), :])
            lo1, hi1 = _unpack_bf16_pairs(ybuf[slot, 1, pl.ds(r, GROUP_ROWS), :])
            moe = jnp.concatenate([lo0 + lo1, hi0 + hi1], axis=1)
            x2 = x1_ref[pl.ds(r, GROUP_ROWS), :] + gate2_ref[pl.ds(g, 1), :] * moe
            ms = jnp.mean(x2 * x2, axis=-1, keepdims=True)
            o_ref[pl.ds(r, GROUP_ROWS), :] = x2 * lax.rsqrt(ms + RMS_EPS) * fg_ref[...]
            return carry

        lax.fori_loop(0, groups, body, 0)

    pl.when(i < prompt_tiles)(lambda: run(op_ref))
    pl.when(i >= prompt_tiles)(lambda: run(os_ref))

    @pl.when(i == tiles - 1)
    def _():
        gather_wait(1 - slot)


def _final(x1, yb, dest, gate2_g, final_g, n_prompt, tm):
    n, d = x1.shape
    groups = tm // GROUP_ROWS
    tiles = n // tm
    prompt_tiles = n_prompt // tm
    table_group = max(g for g in range(1, TABLE_GROUP + 1) if tiles % g == 0)
    tables = [_grouped_tables(dest[k].reshape(tiles, tm), table_group) for k in range(2)]
    idx_spec = pl.BlockSpec((1, 1, tables[0].shape[2]), lambda i: (i // table_group, 0, 0),
                            memory_space=pltpu.SMEM)
    return pl.pallas_call(
        functools.partial(_final_kernel, groups=groups, prompt_tiles=prompt_tiles, tiles=tiles,
                          table_group=table_group),
        grid=(tiles,),
        in_specs=[idx_spec, idx_spec,
                  pl.BlockSpec((tm, d), lambda i: (i, 0)),
                  pl.BlockSpec((groups, d), lambda i: (i, 0)),
                  pl.BlockSpec((1, d), lambda i: (0, 0)),
                  pl.BlockSpec(memory_space=pl.ANY)],
        out_specs=_two_stream_specs(tm, d, prompt_tiles),
        out_shape=[jax.ShapeDtypeStruct((n_prompt, d), F32),
                   jax.ShapeDtypeStruct((n - n_prompt, d), F32)],
        scratch_shapes=[pltpu.VMEM((2, 2, tm, d // 2), jnp.uint32), pltpu.SemaphoreType.DMA((2,))],
        compiler_params=_cparams("arbitrary"),
        name="final",
    )(tables[0], tables[1], x1, gate2_g, final_g.reshape(1, d), yb)


def _layer(x_p, x_s, c_all, cache_k, cache_v, sconv, rel_table, lp, seq, t_new):
    (n1, n2, w_ada, b_ada, w_in, sink, w_ao, conv_w, conv_b, ln_g, ln_b, w_pw2, b_pw2, w_out,
     w_grp, b_grp, w_rt, b_rt, w_gate, w_up, w_down) = lp
    n_prompt, d = x_p.shape
    n = n_prompt + x_s.shape[0]
    batch = n_prompt // seq
    dec_batch = (n - n_prompt) // t_new
    cdim = conv_w.shape[1]
    tm = 256

    c_rows = -(-c_all.shape[0] // 8) * 8
    c_pad = jnp.pad(c_all, ((0, c_rows - c_all.shape[0]), (0, 0)))
    mods = _ada(c_pad, w_ada, b_ada)

    def per_group(k):
        m = mods[:, k * d:(k + 1) * d]
        mp = jnp.broadcast_to(m[:batch, None, :], (batch, seq // GROUP_ROWS, d))
        return jnp.concatenate([mp.reshape(batch * (seq // GROUP_ROWS), d), m[batch:batch + dec_batch]],
                               axis=0)

    shift1, scale1, gate1, shift2, scale2, gate2 = [per_group(k) for k in range(6)]

    h1 = _norm1(x_p, x_s, n1, scale1, shift1, 512)

    k_off = ATTN_DIM
    glu_off = k_off + 2 * KV_DIM
    gate_off = glu_off + 2 * cdim
    w_in_b = w_in.astype(BF16)
    tglu = 512
    w_glu = jnp.stack([w_in_b[:, glu_off:glu_off + cdim].reshape(d, cdim // tglu, tglu),
                       w_in_b[:, glu_off + cdim:gate_off].reshape(d, cdim // tglu, tglu)],
                      axis=2).reshape(d, 2 * cdim)
    tmm = 1536
    (q,) = _proj(h1, w_in_b[:, :k_off], "qkv", tmm, 1024, [BF16])
    (kv,) = _proj(h1, w_in_b[:, k_off:glu_off], "qkv", tmm, 512, [BF16])
    kv_state = _kv_state(h1, w_in_b[:, k_off:glu_off], n_prompt, seq)
    (u,) = _proj(h1, w_glu, "glu", tmm, 2 * tglu, [F32])
    (gates,) = _proj(h1, w_in_b[:, gate_off:], "gate", tmm, 1024, [BF16])

    bias_p = _pair_bias(rel_table)
    buf = cache_k.shape[1]
    kpos = jnp.concatenate([jnp.arange(buf, dtype=jnp.int32) - buf, jnp.arange(t_new, dtype=jnp.int32)])
    bias_s = _rel_bias(rel_table, kpos[None, :] - jnp.arange(t_new, dtype=jnp.int32)[:, None])
    sink_f = sink.astype(F32).reshape(N_KV_HEADS, Q_PER_KV, 1, 1)
    sink_kch = sink.astype(F32).reshape(N_KV_HEADS, 1, COLS_PER_KV, HEADS_PER_COL).transpose(0, 3, 1, 2)
    sink_p = jnp.broadcast_to(sink_kch[..., None, None],
                              (N_KV_HEADS, HEADS_PER_COL, 2, COLS_PER_KV, CHUNK, LANES)).reshape(
        N_KV_HEADS, HEADS_PER_COL, 2 * COLS_PER_KV * CHUNK, LANES)
    sink_s = jnp.broadcast_to(sink_f, (N_KV_HEADS, Q_PER_KV, t_new, 1)).reshape(N_KV_HEADS, Q_PER_KV * t_new, 1)
    o_p = _attn_prompt(q, kv, bias_p, sink_p, n_prompt, seq, 2)
    o_s = _attn_sample(q, kv, cache_k.reshape(dec_batch, buf, KV_DIM), cache_v.reshape(dec_batch, buf, KV_DIM),
                       bias_s, sink_s, n_prompt, t_new)

    hist = GROUP_ROWS
    sconv_pad = jnp.pad(sconv, ((0, 0), (hist - sconv.shape[1], 0), (0, 0)))
    w_rt_t = jnp.zeros((ROUTER_ROWS, d), F32)
    w_rt_t = w_rt_t.at[:N_GROUPS].set(w_grp.T).at[EXPERT_ROW0:EXPERT_ROW0 + N_EXPERTS].set(w_rt.T)
    b_rt_t = jnp.zeros((ROUTER_ROWS, 1), F32)
    b_rt_t = b_rt_t.at[:N_GROUPS, 0].set(b_grp.astype(F32)).at[EXPERT_ROW0:EXPERT_ROW0 + N_EXPERTS, 0].set(
        b_rt.astype(F32))
    nch = cdim // CONV_CHUNK
    conv_w_c = conv_w.reshape(CONV_WIDTH, nch, CONV_CHUNK).transpose(1, 0, 2)
    params = dict(conv_w=conv_w_c, conv_b=conv_b.reshape(nch, 1, CONV_CHUNK), ln_g=ln_g.reshape(1, cdim),
                  ln_b=ln_b.reshape(1, cdim), w_pw2=w_pw2.astype(BF16), b_pw2=b_pw2.reshape(1, d),
                  w_ao=w_ao.astype(BF16), w_out=w_out.astype(BF16), n2=n2.reshape(1, d),
                  w_rt=w_rt_t.astype(BF16), b_rt=b_rt_t)
    x1, h2, eidx, wtok, rank, cnt = _post(x_p, x_s, u, sconv_pad, o_p, o_s, gates, gate1, shift2, scale2,
                                             params, n_prompt, seq, tm)

    blk_e, next_e, nused, src, roww, dest = _dispatch(eidx, wtok, rank, cnt[:, 0].astype(jnp.int32), n)
    yb = _moe(h2, blk_e, next_e, nused, src, roww, w_gate, w_up, w_down)
    return x1, yb, dest, gate2, kv_state, u


def kernel(x_prompt, x_sample, c_prompt, c_sample, cache_k, cache_v, state_conv, rel_bias_table, norm1_g, norm2_g, w_ada, b_ada, w_in, attn_sink, w_attn_o, conv_w, conv_b, conv_ln_g, conv_ln_b, w_pw2, b_pw2, w_out, w_group, b_group, w_router, b_router, w_gate, w_up, w_down, final_g):
    batch, seq, d = x_prompt.shape
    dec_batch, t_new, _ = x_sample.shape
    depth = norm1_g.shape[0]
    assert depth == 1, "single trunk layer"
    assert t_new == GROUP_ROWS and seq % GROUP_ROWS == 0
    n_prompt = batch * seq
    n = n_prompt + dec_batch * t_new
    x_p = x_prompt.reshape(n_prompt, d)
    x_s = x_sample.reshape(dec_batch * t_new, d)
    c_all = jnp.concatenate([c_prompt, c_sample], axis=0)
    l = 0
    lp = tuple(a.reshape(a.shape[1:]) for a in (
        norm1_g, norm2_g, w_ada, b_ada, w_in, attn_sink, w_attn_o, conv_w, conv_b, conv_ln_g, conv_ln_b,
        w_pw2, b_pw2, w_out, w_group, b_group, w_router, b_router, w_gate, w_up, w_down))
    x1, yb, dest, gate2, kv_state, u = _layer(x_p, x_s, c_all, cache_k[l], cache_v[l], state_conv[l],
                                              rel_bias_table, lp, seq, t_new)
    out_p, out_s = _final(x1, yb, dest, gate2, final_g, n_prompt, 256)

    y_prompt = out_p.reshape(batch, seq, d)
    y_sample = out_s.reshape(dec_batch, t_new, d)
    cdim = u.shape[1]
    kvp = kv_state[:batch * WINDOW].reshape(batch, WINDOW, 2 * KV_DIM)
    new_k_prompt = kvp[..., :KV_DIM].reshape(1, batch, WINDOW, N_KV_HEADS, HEAD_DIM)
    new_v_prompt = kvp[..., KV_DIM:].reshape(1, batch, WINDOW, N_KV_HEADS, HEAD_DIM)
    new_conv_prompt = jnp.stack([u[(b + 1) * seq - (CONV_WIDTH - 1):(b + 1) * seq] for b in range(batch)])[None]
    kvs = kv_state[batch * WINDOW:].reshape(dec_batch, t_new, 2 * KV_DIM)
    buf = cache_k.shape[2]
    k_new = kvs[..., :KV_DIM].reshape(dec_batch, t_new, N_KV_HEADS, HEAD_DIM)
    v_new = kvs[..., KV_DIM:].reshape(dec_batch, t_new, N_KV_HEADS, HEAD_DIM)
    new_k_sample = jnp.concatenate([cache_k[l], k_new], axis=1)[:, -buf:][None]
    new_v_sample = jnp.concatenate([cache_v[l], v_new], axis=1)[:, -buf:][None]
    us = u[n_prompt:].reshape(dec_batch, t_new, cdim)
    new_conv_sample = jnp.concatenate([state_conv[l], us], axis=1)[:, -(CONV_WIDTH - 1):][None]
    return (y_prompt, y_sample, new_k_prompt, new_v_prompt, new_conv_prompt,
            new_k_sample, new_v_sample, new_conv_sample)
```

```python
import functools
import math

import jax
import jax.numpy as jnp
from jax import lax
from jax.experimental import pallas as pl
from jax.experimental.pallas import tpu as pltpu

F32 = jnp.float32
BF16 = jnp.bfloat16

CHUNK = 64
HEAD_DIM = 64
N_Q_HEADS = 16
N_KV_HEADS = 4
Q_PER_KV = N_Q_HEADS // N_KV_HEADS
ATTN_DIM = N_Q_HEADS * HEAD_DIM
KV_DIM = N_KV_HEADS * HEAD_DIM
WINDOW = 128
WIN_CHUNKS = WINDOW // CHUNK
BAND = (WIN_CHUNKS + 1) * CHUNK
CONV_WIDTH = 31
NUM_BUCKETS = 32
MAX_DISTANCE = 128
N_GROUPS = 4
EXPERTS_PER_GROUP = 8
N_EXPERTS = N_GROUPS * EXPERTS_PER_GROUP
MOE_BLOCK = 256
RMS_EPS = 1e-6
LN_EPS = 1e-5
NEG_INF = -1e30

GROUP_ROWS = 32
LANES = 128
SUBLANES = 8
CONV_CHUNK = 256
ROUTER_ROWS = 64
EXPERT_ROW0 = 8
V7X_VMEM_LIMIT = 56 * 1024 * 1024


def _cparams(*sem):
    return pltpu.CompilerParams(dimension_semantics=sem, vmem_limit_bytes=V7X_VMEM_LIMIT)


def _ada_kernel(c_ref, w_ref, b_ref, o_ref):
    c = c_ref[...]
    s = c * jax.nn.sigmoid(c)
    o_ref[...] = jnp.dot(s.astype(BF16), w_ref[...].astype(BF16),
                         preferred_element_type=F32) + b_ref[...]


def _ada(c_all, w_ada, b_ada):
    rows, d = c_all.shape
    ncol = w_ada.shape[1]
    tn = 1024
    return pl.pallas_call(
        _ada_kernel,
        grid=(ncol // tn,),
        in_specs=[pl.BlockSpec((rows, d), lambda j: (0, 0)),
                  pl.BlockSpec((d, tn), lambda j: (0, j)),
                  pl.BlockSpec((1, tn), lambda j: (0, j))],
        out_specs=pl.BlockSpec((rows, tn), lambda j: (0, j)),
        out_shape=jax.ShapeDtypeStruct((rows, ncol), F32),
        compiler_params=_cparams("arbitrary"),
        name="ada",
    )(c_all, w_ada, b_ada.reshape(1, ncol))


def _rms_mod(x, gain, scale, shift):
    ms = jnp.mean(x * x, axis=-1, keepdims=True)
    y = x * lax.rsqrt(ms + RMS_EPS) * gain
    return y * (1.0 + scale) + shift


def _pack_bf16_pairs(xb):
    w = xb.shape[1] // 2
    bits = lax.bitcast_convert_type(xb.astype(F32), jnp.uint32)
    return (bits[:, :w] >> 16) | (bits[:, w:] & jnp.uint32(0xFFFF0000))


def _unpack_bf16_pairs(p):
    lo = lax.bitcast_convert_type(p << 16, F32)
    hi = lax.bitcast_convert_type(p & jnp.uint32(0xFFFF0000), F32)
    return lo, hi


def _two_stream_specs(tm, width, prompt_tiles):
    return [pl.BlockSpec((tm, width), lambda i: (jnp.minimum(i, prompt_tiles - 1), 0)),
            pl.BlockSpec((tm, width), lambda i: (jnp.maximum(i - prompt_tiles, 0), 0))]


def _norm1_kernel(xp_ref, xs_ref, g_ref, sc_ref, sh_ref, h_ref, *, groups, prompt_tiles):
    def run(x_ref):
        def body(gi, carry):
            r = pl.multiple_of(gi * GROUP_ROWS, GROUP_ROWS)
            h = _rms_mod(x_ref[pl.ds(r, GROUP_ROWS), :], g_ref[...],
                         sc_ref[pl.ds(gi, 1), :], sh_ref[pl.ds(gi, 1), :])
            h_ref[pl.ds(r, GROUP_ROWS), :] = h.astype(h_ref.dtype)
            return carry

        lax.fori_loop(0, groups, body, 0)

    i = pl.program_id(0)
    pl.when(i < prompt_tiles)(lambda: run(xp_ref))
    pl.when(i >= prompt_tiles)(lambda: run(xs_ref))


def _norm1(x_p, x_s, gain, scale_g, shift_g, tm):
    d = x_p.shape[1]
    n = x_p.shape[0] + x_s.shape[0]
    groups = tm // GROUP_ROWS
    prompt_tiles = x_p.shape[0] // tm
    return pl.pallas_call(
        functools.partial(_norm1_kernel, groups=groups, prompt_tiles=prompt_tiles),
        grid=(n // tm,),
        in_specs=_two_stream_specs(tm, d, prompt_tiles) + [
            pl.BlockSpec((1, d), lambda i: (0, 0)),
            pl.BlockSpec((groups, d), lambda i: (i, 0)),
            pl.BlockSpec((groups, d), lambda i: (i, 0))],
        out_specs=pl.BlockSpec((tm, d), lambda i: (i, 0)),
        out_shape=jax.ShapeDtypeStruct((n, d), BF16),
        compiler_params=_cparams("arbitrary"),
        name="norm1",
    )(x_p, x_s, gain.reshape(1, d), scale_g, shift_g)


def _proj_kernel(h_ref, w_ref, *o_refs, mode):
    acc = jnp.dot(h_ref[...], w_ref[...], preferred_element_type=F32)
    if mode == "qkv":
        o_refs[0][...] = acc.astype(BF16)
    elif mode == "glu":
        half = acc.shape[1] // 2
        o_refs[0][...] = acc[:, :half] * jax.nn.sigmoid(acc[:, half:])
    else:
        o_refs[0][...] = jax.nn.sigmoid(acc).astype(BF16)


def _proj(h, w, mode, tm, tnw, out_dtypes):
    n, d = h.shape
    ncol = w.shape[1]
    tno = tnw // 2 if mode == "glu" else tnw
    nout = ncol // 2 if mode == "glu" else ncol
    outs = pl.pallas_call(
        functools.partial(_proj_kernel, mode=mode),
        grid=(n // tm, ncol // tnw),
        in_specs=[pl.BlockSpec((tm, d), lambda i, j: (i, 0)),
                  pl.BlockSpec((d, tnw), lambda i, j: (0, j))],
        out_specs=[pl.BlockSpec((tm, tno), lambda i, j: (i, j)) for _ in out_dtypes],
        out_shape=[jax.ShapeDtypeStruct((n, nout), dt) for dt in out_dtypes],
        compiler_params=_cparams("arbitrary", "arbitrary"),
        name="proj_" + mode,
    )(h, w)
    return outs


def _kv_state_kernel(h_ref, w_ref, o_ref):
    o_ref[...] = jnp.dot(h_ref[...], w_ref[...], preferred_element_type=F32)


def _kv_state(h, w_kv, n_prompt, seq):
    n, d = h.shape
    batch = n_prompt // seq
    per_seq = seq // WINDOW
    steps = batch + (n - n_prompt) // WINDOW

    def row_block(i):
        return jnp.where(i < batch, (i + 1) * per_seq - 1, n_prompt // WINDOW + i - batch)

    return pl.pallas_call(
        _kv_state_kernel,
        grid=(steps,),
        in_specs=[pl.BlockSpec((WINDOW, d), lambda i: (row_block(i), 0)),
                  pl.BlockSpec((d, 2 * KV_DIM), lambda i: (0, 0))],
        out_specs=pl.BlockSpec((WINDOW, 2 * KV_DIM), lambda i: (i, 0)),
        out_shape=jax.ShapeDtypeStruct((steps * WINDOW, 2 * KV_DIM), F32),
        compiler_params=_cparams("arbitrary"),
        name="kv_state",
    )(h, w_kv)


def _t5_bucket(rel):
    nb = NUM_BUCKETS // 2
    n = -rel
    ret = jnp.where(n < 0, nb, 0)
    n = jnp.abs(n)
    max_exact = nb // 2
    nf = jnp.maximum(n, 1).astype(F32)
    large = max_exact + (jnp.log(nf / max_exact) / math.log(MAX_DISTANCE / max_exact)
                         * (nb - max_exact)).astype(jnp.int32)
    large = jnp.minimum(large, nb - 1)
    return ret + jnp.where(n < max_exact, n, large)


def _bias_kernel(tbl_ref, bkt_ref, o_ref):
    bkt = bkt_ref[...]
    nq = bkt.shape[0]
    for k in range(N_KV_HEADS):
        for g in range(Q_PER_KV):
            acc = jnp.zeros(bkt.shape, F32)
            for b in range(NUM_BUCKETS):
                acc = jnp.where(bkt == b, tbl_ref[b, k * Q_PER_KV + g], acc)
            o_ref[k, g * nq:(g + 1) * nq, :] = acc


def _rel_bias(rel_table, rel):
    nq, nk = rel.shape
    return pl.pallas_call(
        _bias_kernel,
        in_specs=[pl.BlockSpec(memory_space=pltpu.SMEM),
                  pl.BlockSpec((nq, nk), lambda: (0, 0))],
        out_specs=pl.BlockSpec((N_KV_HEADS, Q_PER_KV * nq, nk), lambda: (0, 0, 0)),
        out_shape=jax.ShapeDtypeStruct((N_KV_HEADS, Q_PER_KV * nq, nk), F32),
        name="rel_bias",
    )(rel_table.astype(F32), _t5_bucket(rel))


PAIR_BAND = 2 * CHUNK + WINDOW
HEADS_PER_COL = LANES // HEAD_DIM
COLS_PER_KV = Q_PER_KV // HEADS_PER_COL


def _pair_bias_kernel(tbl_ref, bkt_ref, o_ref):
    for e in range(2):
        bkt = bkt_ref[e]
        for k in range(N_KV_HEADS):
            for col in range(COLS_PER_KV):
                for half in range(HEADS_PER_COL):
                    head = k * Q_PER_KV + col * HEADS_PER_COL + half
                    acc = jnp.full(bkt.shape, NEG_INF, F32)
                    for b in range(NUM_BUCKETS):
                        acc = jnp.where(bkt == b, tbl_ref[b, head], acc)
                    r0 = (e * COLS_PER_KV + col) * CHUNK
                    o_ref[k, r0:r0 + CHUNK, half * PAIR_BAND:(half + 1) * PAIR_BAND] = acc


def _pair_bias(rel_table):
    qi = jnp.arange(CHUNK, dtype=jnp.int32)[None, :, None]
    kj = jnp.arange(PAIR_BAND, dtype=jnp.int32)[None, None, :]
    e = jnp.arange(2, dtype=jnp.int32)[:, None, None]
    rel = (kj - WINDOW) - (e * CHUNK + qi)
    key_chunk = kj // CHUNK - e
    seen = (key_chunk >= 0) & (key_chunk <= WIN_CHUNKS)
    bkt = jnp.where(seen, _t5_bucket(rel), -1)
    rows = 2 * COLS_PER_KV * CHUNK
    return pl.pallas_call(
        _pair_bias_kernel,
        in_specs=[pl.BlockSpec(memory_space=pltpu.SMEM),
                  pl.BlockSpec((2, CHUNK, PAIR_BAND), lambda: (0, 0, 0))],
        out_specs=pl.BlockSpec((N_KV_HEADS, rows, HEADS_PER_COL * PAIR_BAND), lambda: (0, 0, 0)),
        out_shape=jax.ShapeDtypeStruct((N_KV_HEADS, rows, HEADS_PER_COL * PAIR_BAND), F32),
        name="pair_bias",
    )(rel_table.astype(F32), bkt)


def _attend(qc, kk_all, vv_all, bias_ref, sink_ref, mask_thr):
    nq = qc.shape[0]
    nk = kk_all.shape[0]
    pieces = []
    for k in range(N_KV_HEADS):
        qs = jnp.concatenate(
            [qc[:, (k * Q_PER_KV + g) * HEAD_DIM:(k * Q_PER_KV + g + 1) * HEAD_DIM]
             for g in range(Q_PER_KV)], axis=0)
        kk = kk_all[:, k * HEAD_DIM:(k + 1) * HEAD_DIM]
        vv = vv_all[:, k * HEAD_DIM:(k + 1) * HEAD_DIM]
        lg = lax.dot_general(qs, kk, (((1,), (1,)), ((), ())),
                             preferred_element_type=F32) * (HEAD_DIM ** -0.5) + bias_ref[k]
        if mask_thr is not None:
            col = lax.broadcasted_iota(jnp.int32, (Q_PER_KV * nq, nk), 1)
            lg = jnp.where(col < mask_thr, NEG_INF, lg)
        s = sink_ref[k]
        m = jnp.maximum(jnp.max(lg, axis=-1, keepdims=True), s)
        p = jnp.exp(lg - m)
        den = jnp.sum(p, axis=-1, keepdims=True) + jnp.exp(s - m)
        o = jnp.dot(p.astype(BF16), vv, preferred_element_type=F32) / den
        pieces.extend(o[g * nq:(g + 1) * nq, :] for g in range(Q_PER_KV))
    return jnp.concatenate(pieces, axis=1)


def _attn_prompt_kernel(q_ref, kv_ref, halo_ref, bias_ref, sink_ref, o_ref, ka, kb, va, vb, *, pairs):
    t = pl.program_id(1)
    kvcat = jnp.concatenate([halo_ref[...], kv_ref[...]], axis=0)
    low = lax.broadcasted_iota(jnp.int32, (kvcat.shape[0], LANES), 1) < HEAD_DIM
    zero = jnp.zeros((kvcat.shape[0], LANES), BF16)
    ones_low = jnp.where(low, 1.0, 0.0).astype(BF16)
    ones_high = jnp.where(low, 0.0, 1.0).astype(BF16)
    kv_cols = KV_DIM // LANES
    for col in range(2 * kv_cols):
        x = kvcat[:, col * LANES:(col + 1) * LANES]
        xs = jnp.concatenate([x[:, HEAD_DIM:], x[:, :HEAD_DIM]], axis=1)
        dst_a, dst_b = (ka, kb) if col < kv_cols else (va, vb)
        k0 = (col % kv_cols) * HEADS_PER_COL
        dst_a[k0, :, 0:LANES] = jnp.where(low, x, zero)
        dst_b[k0, :, 0:LANES] = jnp.where(low, zero, xs)
        dst_a[k0 + 1, :, 0:LANES] = jnp.where(low, xs, zero)
        dst_b[k0 + 1, :, 0:LANES] = jnp.where(low, zero, x)
    for k in range(N_KV_HEADS):
        va[k, :, LANES:2 * LANES] = ones_low
        vb[k, :, LANES:2 * LANES] = ones_high

    out_low = lax.broadcasted_iota(jnp.int32, (2 * COLS_PER_KV * CHUNK, LANES), 1) < HEAD_DIM
    for p in range(pairs):
        r0 = p * 2 * CHUNK
        for k in range(N_KV_HEADS):
            lhs = jnp.concatenate(
                [q_ref[r0 + e * CHUNK:r0 + (e + 1) * CHUNK, (k * COLS_PER_KV + c) * LANES:(k * COLS_PER_KV + c + 1) * LANES]
                 for e in range(2) for c in range(COLS_PER_KV)], axis=0)
            keys = jnp.concatenate([ka[k, r0:r0 + PAIR_BAND, :], kb[k, r0:r0 + PAIR_BAND, :]], axis=0)
            lg = lax.dot_general(lhs, keys, (((1,), (1,)), ((), ())),
                                 preferred_element_type=F32) * (HEAD_DIM ** -0.5) + bias_ref[k]
            probs, sink_terms = [], []
            for half in range(HEADS_PER_COL):
                seg = lg[:, half * PAIR_BAND:(half + 1) * PAIR_BAND]
                if p == 0:
                    kcol = lax.broadcasted_iota(jnp.int32, seg.shape, 1)
                    seg = jnp.where(kcol < jnp.where(t == 0, WINDOW, 0), NEG_INF, seg)
                s = sink_ref[k, half]
                folded = seg[:, :LANES]
                for j in range(1, PAIR_BAND // LANES):
                    folded = jnp.maximum(folded, seg[:, j * LANES:(j + 1) * LANES])
                m = jnp.maximum(jnp.max(folded, axis=-1, keepdims=True), s)
                pr = jnp.exp(seg - jnp.concatenate([m] * (PAIR_BAND // LANES), axis=1))
                sink_terms.append(jnp.exp(s - m))
                probs.append(pr.astype(BF16))
            vals = jnp.concatenate([va[k, r0:r0 + PAIR_BAND, :], vb[k, r0:r0 + PAIR_BAND, :]], axis=0)
            oe = jnp.dot(jnp.concatenate(probs, axis=1), vals, preferred_element_type=F32)
            o = oe[:, :LANES] / (oe[:, LANES:] + jnp.where(out_low, sink_terms[0], sink_terms[1]))
            for e in range(2):
                for c in range(COLS_PER_KV):
                    rr = (e * COLS_PER_KV + c) * CHUNK
                    o_ref[r0 + e * CHUNK:r0 + (e + 1) * CHUNK,
                          (k * COLS_PER_KV + c) * LANES:(k * COLS_PER_KV + c + 1) * LANES] = (
                        o[rr:rr + CHUNK, :].astype(o_ref.dtype))


def _attn_prompt(q, kv, bias, sink_rows, n_prompt, seq, pairs):
    rows = pairs * 2 * CHUNK
    tiles = seq // rows
    halo_per_tile = rows // WINDOW
    keys = WINDOW + rows
    return pl.pallas_call(
        functools.partial(_attn_prompt_kernel, pairs=pairs),
        grid=(n_prompt // seq, tiles),
        scratch_shapes=[pltpu.VMEM((N_KV_HEADS, keys, LANES), BF16) for _ in range(2)]
        + [pltpu.VMEM((N_KV_HEADS, keys, 2 * LANES), BF16) for _ in range(2)],
        in_specs=[pl.BlockSpec((rows, ATTN_DIM), lambda b, t: (b * tiles + t, 0)),
                  pl.BlockSpec((rows, 2 * KV_DIM), lambda b, t: (b * tiles + t, 0)),
                  pl.BlockSpec((WINDOW, 2 * KV_DIM),
                               lambda b, t: (jnp.maximum((b * tiles + t) * halo_per_tile - 1, 0), 0)),
                  pl.BlockSpec(bias.shape, lambda b, t: (0, 0, 0)),
                  pl.BlockSpec(sink_rows.shape, lambda b, t: (0, 0, 0, 0))],
        out_specs=pl.BlockSpec((rows, ATTN_DIM), lambda b, t: (b * tiles + t, 0)),
        out_shape=jax.ShapeDtypeStruct((n_prompt, ATTN_DIM), BF16),
        compiler_params=_cparams("arbitrary", "arbitrary"),
        name="attn_prompt",
    )(q, kv, kv, bias, sink_rows)


def _attn_sample_kernel(q_ref, kv_ref, ck_ref, cv_ref, bias_ref, sink_ref, o_ref):
    kv = kv_ref[...]
    kk = jnp.concatenate([ck_ref[0].astype(BF16), kv[:, :KV_DIM]], axis=0)
    vv = jnp.concatenate([cv_ref[0].astype(BF16), kv[:, KV_DIM:]], axis=0)
    o = _attend(q_ref[...], kk, vv, bias_ref, sink_ref, None)
    o_ref[...] = o.astype(o_ref.dtype)


def _attn_sample(q, kv, cache_k, cache_v, bias, sink_rows, n_prompt, t_new):
    dec_batch, buf, _ = cache_k.shape
    first = n_prompt // t_new
    return pl.pallas_call(
        _attn_sample_kernel,
        grid=(dec_batch,),
        in_specs=[pl.BlockSpec((t_new, ATTN_DIM), lambda s: (first + s, 0)),
                  pl.BlockSpec((t_new, 2 * KV_DIM), lambda s: (first + s, 0)),
                  pl.BlockSpec((1, buf, KV_DIM), lambda s: (s, 0, 0)),
                  pl.BlockSpec((1, buf, KV_DIM), lambda s: (s, 0, 0)),
                  pl.BlockSpec(bias.shape, lambda s: (0, 0, 0)),
                  pl.BlockSpec(sink_rows.shape, lambda s: (0, 0, 0))],
        out_specs=pl.BlockSpec((t_new, ATTN_DIM), lambda s: (s, 0)),
        out_shape=jax.ShapeDtypeStruct((dec_batch * t_new, ATTN_DIM), BF16),
        compiler_params=_cparams("arbitrary"),
        name="attn_sample",
    )(q, kv, cache_k, cache_v, bias, sink_rows)


def _route(lt, eidx_ref, wtok_ref, rank_ref, cnt_ref, cnt_scr):
    gl = [lt[r:r + 1, :] for r in range(N_GROUPS)]
    gmax = gl[0]
    gsel = jnp.zeros(gl[0].shape, jnp.int32)
    for r in range(1, N_GROUPS):
        better = gl[r] > gmax
        gsel = jnp.where(better, r, gsel)
        gmax = jnp.maximum(gmax, gl[r])
    gexp = [jnp.exp(v - gmax) for v in gl]
    gsum = gexp[0]
    for r in range(1, N_GROUPS):
        gsum = gsum + gexp[r]
    psel = jnp.zeros(gl[0].shape, F32)
    for r in range(N_GROUPS):
        psel = jnp.where(gsel == r, gexp[r] / gsum, psel)
    el = jnp.zeros((EXPERTS_PER_GROUP, lt.shape[1]), F32)
    for r in range(N_GROUPS):
        lo = EXPERT_ROW0 + r * EXPERTS_PER_GROUP
        el = jnp.where(gsel == r, lt[lo:lo + EXPERTS_PER_GROUP, :], el)
    emax = jnp.max(el, axis=0, keepdims=True)
    ee = jnp.exp(el - emax)
    pin = ee / jnp.sum(ee, axis=0, keepdims=True)
    idx = lax.broadcasted_iota(jnp.int32, pin.shape, 0)
    p1 = jnp.max(pin, axis=0, keepdims=True)
    i1 = jnp.min(jnp.where(pin == p1, idx, EXPERTS_PER_GROUP), axis=0, keepdims=True)
    rest = jnp.where(idx == i1, -1.0, pin)
    p2 = jnp.max(rest, axis=0, keepdims=True)
    i2 = jnp.min(jnp.where(rest == p2, idx, EXPERTS_PER_GROUP), axis=0, keepdims=True)
    tot = p1 + p2
    e1 = gsel * EXPERTS_PER_GROUP + i1
    e2 = gsel * EXPERTS_PER_GROUP + i2
    eidx_ref[0:1, :] = e1
    eidx_ref[1:2, :] = e2
    wtok_ref[0:1, :] = psel * p1 / tot
    wtok_ref[1:2, :] = psel * p2 / tot

    t = lt.shape[1]
    eiota = lax.broadcasted_iota(jnp.int32, (N_EXPERTS, t), 0)
    oh1 = (eiota == e1).astype(F32)
    oh2 = (eiota == e2).astype(F32)
    both = oh1 + oh2
    before = (lax.broadcasted_iota(jnp.int32, (t, t), 0)
              < lax.broadcasted_iota(jnp.int32, (t, t), 1)).astype(BF16)
    prior = jnp.dot(both.astype(BF16), before, preferred_element_type=F32) + cnt_scr[:, 0:1]
    rank_ref[0:1, :] = jnp.sum(oh1 * prior, axis=0, keepdims=True).astype(jnp.int32)
    rank_ref[1:2, :] = jnp.sum(oh2 * prior, axis=0, keepdims=True).astype(jnp.int32)
    cnt_scr[...] = cnt_scr[...] + jnp.sum(both, axis=1, keepdims=True)
    cnt_ref[...] = cnt_scr[...]


def _post_kernel(xp_ref, xs_ref, u_ref, uhalo_ref, sconv_ref, op_ref, os_ref, gates_ref,
                 gate1_ref, shift2_ref, scale2_ref,
                 cw_ref, cb_ref, lng_ref, lnb_ref, wpw2_ref, bpw2_ref, wao_ref, wout_ref,
                 n2_ref, wrt_ref, brt_ref,
                 x1_ref, h2_ref, eidx_ref, wtok_ref, rank_ref, cnt_ref,
                 uext, shift_scr, d_scr, s_scr, o_scr, mix_scr, h2b_scr, cnt_scr,
                 *, groups, prompt_tiles, tiles_per_seq):
    i = pl.program_id(0)
    d_model = xp_ref.shape[1]

    @pl.when(i == 0)
    def _():
        cnt_scr[...] = jnp.zeros(cnt_scr.shape, F32)
    nch = cw_ref.shape[0]
    cw = uext.shape[2]
    hist = uext.shape[1] - GROUP_ROWS
    lead = hist - (CONV_WIDTH - 1)

    def put_hist(g, rows):
        for c in range(nch):
            uext[g * nch + c, 0:hist, :] = rows[:, c * cw:(c + 1) * cw]

    @pl.when(i < prompt_tiles)
    def _():
        first = (i % tiles_per_seq) == 0
        put_hist(0, jnp.where(first, 0.0, uhalo_ref[...]))
        for g in range(1, groups):
            put_hist(g, u_ref[(g - 1) * GROUP_ROWS:g * GROUP_ROWS, :])
        o_scr[...] = op_ref[...]

    @pl.when(i >= prompt_tiles)
    def _():
        for g in range(groups):
            put_hist(g, sconv_ref[g])
        o_scr[...] = os_ref[...]

    for g in range(groups):
        for c in range(nch):
            uext[g * nch + c, hist:hist + GROUP_ROWS, :] = (
                u_ref[g * GROUP_ROWS:(g + 1) * GROUP_ROWS, c * cw:(c + 1) * cw])

    span = shift_scr.shape[1]

    def conv_chunk(k, carry):
        c = k % nch
        win_all = uext[k]
        for s in range(1, SUBLANES):
            shift_scr[s, :, :] = pltpu.roll(win_all, win_all.shape[0] - s, 0)[0:span, :]
        acc = None
        for j in range(CONV_WIDTH):
            base, s = divmod(lead + j, SUBLANES)
            rows = pl.ds(base * SUBLANES, GROUP_ROWS)
            win = uext[k, rows, :] if s == 0 else shift_scr[s, rows, :]
            term = win * cw_ref[c, j:j + 1, :]
            acc = term if acc is None else acc + term
        d_scr[k] = acc + cb_ref[c]
        return carry

    lax.fori_loop(0, groups * nch, conv_chunk, 0)

    for g in range(groups):
        dd = jnp.concatenate([d_scr[g * nch + c] for c in range(nch)], axis=1)
        mu = jnp.mean(dd, axis=-1, keepdims=True)
        var = jnp.mean(jnp.square(dd - mu), axis=-1, keepdims=True)
        y = (dd - mu) * lax.rsqrt(var + LN_EPS) * lng_ref[...] + lnb_ref[...]
        s_scr[g * GROUP_ROWS:(g + 1) * GROUP_ROWS, :] = (y * jax.nn.sigmoid(y)).astype(BF16)

    conv_out = jnp.dot(s_scr[...], wpw2_ref[...], preferred_element_type=F32) + bpw2_ref[...]
    attn_out = jnp.dot(o_scr[...], wao_ref[...], preferred_element_type=F32)
    merged = (gates_ref[:, :d_model].astype(F32) * attn_out
              + gates_ref[:, d_model:].astype(F32) * conv_out)
    mix_scr[...] = jnp.dot(merged.astype(BF16), wout_ref[...], preferred_element_type=F32)

    def residual(x_ref):
        def res_group(g, carry):
            r = pl.multiple_of(g * GROUP_ROWS, GROUP_ROWS)
            x1 = (x_ref[pl.ds(r, GROUP_ROWS), :]
                  + gate1_ref[pl.ds(g, 1), :] * mix_scr[pl.ds(r, GROUP_ROWS), :])
            x1_ref[pl.ds(r, GROUP_ROWS), :] = x1
            h = _rms_mod(x1, n2_ref[...], scale2_ref[pl.ds(g, 1), :], shift2_ref[pl.ds(g, 1), :])
            hb = h.astype(BF16)
            h2b_scr[pl.ds(r, GROUP_ROWS), :] = hb
            h2_ref[pl.ds(r, GROUP_ROWS), :] = _pack_bf16_pairs(hb)
            return carry

        lax.fori_loop(0, groups, res_group, 0)

    pl.when(i < prompt_tiles)(lambda: residual(xp_ref))
    pl.when(i >= prompt_tiles)(lambda: residual(xs_ref))

    lt = lax.dot_general(wrt_ref[...], h2b_scr[...], (((1,), (1,)), ((), ())),
                         preferred_element_type=F32) + brt_ref[...]
    _route(lt, eidx_ref, wtok_ref, rank_ref, cnt_ref, cnt_scr)


def _post(x_p, x_s, u, sconv_pad, o_p, o_s, gates, gate1_g, shift2_g, scale2_g, p, n_prompt, seq, tm):
    d = x_p.shape[1]
    n = x_p.shape[0] + x_s.shape[0]
    cdim = u.shape[1]
    nch = cdim // CONV_CHUNK
    groups = tm // GROUP_ROWS
    prompt_tiles = n_prompt // tm
    const = lambda shape: pl.BlockSpec(shape, lambda i: (0,) * len(shape))
    row = lambda w: pl.BlockSpec((tm, w), lambda i: (i, 0))
    grp = pl.BlockSpec((groups, d), lambda i: (i, 0))
    return pl.pallas_call(
        functools.partial(_post_kernel, groups=groups, prompt_tiles=prompt_tiles,
                          tiles_per_seq=seq // tm),
        grid=(n // tm,),
        in_specs=_two_stream_specs(tm, d, prompt_tiles) + [
                  row(cdim),
                  pl.BlockSpec((GROUP_ROWS, cdim), lambda i: (jnp.maximum(i * groups - 1, 0), 0)),
                  pl.BlockSpec((groups, GROUP_ROWS, cdim),
                               lambda i: (jnp.maximum(i - prompt_tiles, 0), 0, 0))]
                 + _two_stream_specs(tm, ATTN_DIM, prompt_tiles) + [
                  row(2 * d), grp, grp, grp,
                  const((nch, CONV_WIDTH, CONV_CHUNK)), const((nch, 1, CONV_CHUNK)),
                  const((1, cdim)), const((1, cdim)),
                  const((cdim, d)), const((1, d)), const((ATTN_DIM, d)), const((d, d)),
                  const((1, d)), const((ROUTER_ROWS, d)), const((ROUTER_ROWS, 1))],
        out_specs=[row(d),
                   row(d // 2),
                   pl.BlockSpec((2, tm), lambda i: (0, i)),
                   pl.BlockSpec((2, tm), lambda i: (0, i)),
                   pl.BlockSpec((2, tm), lambda i: (0, i)),
                   pl.BlockSpec((N_EXPERTS, LANES), lambda i: (0, 0))],
        out_shape=[jax.ShapeDtypeStruct((n, d), F32),
                   jax.ShapeDtypeStruct((n, d // 2), jnp.uint32),
                   jax.ShapeDtypeStruct((2, n), jnp.int32),
                   jax.ShapeDtypeStruct((2, n), F32),
                   jax.ShapeDtypeStruct((2, n), jnp.int32),
                   jax.ShapeDtypeStruct((N_EXPERTS, LANES), F32)],
        scratch_shapes=[pltpu.VMEM((groups * nch, 2 * GROUP_ROWS, CONV_CHUNK), F32),
                        pltpu.VMEM((SUBLANES, 2 * GROUP_ROWS - SUBLANES, CONV_CHUNK), F32),
                        pltpu.VMEM((groups * nch, GROUP_ROWS, CONV_CHUNK), F32),
                        pltpu.VMEM((tm, cdim), BF16),
                        pltpu.VMEM((tm, ATTN_DIM), BF16),
                        pltpu.VMEM((tm, d), F32),
                        pltpu.VMEM((tm, d), BF16),
                        pltpu.VMEM((N_EXPERTS, LANES), F32)],
        compiler_params=_cparams("arbitrary"),
        name="post",
    )(x_p, x_s, u, u, sconv_pad, o_p, o_s, gates, gate1_g, shift2_g, scale2_g,
      p["conv_w"], p["conv_b"], p["ln_g"], p["ln_b"], p["w_pw2"], p["b_pw2"], p["w_ao"], p["w_out"],
      p["n2"], p["w_rt"], p["b_rt"])


TABLE_GROUP = 8
GATHER_AHEAD = 4


def _grouped_tables(tab, group=TABLE_GROUP, ahead=1):
    steps, w = tab.shape
    assert steps % group == 0
    padded = jnp.concatenate([tab] + [tab[-1:]] * ahead, axis=0)
    cols = [tab.reshape(steps // group, group * w)] + [padded[group + a::group][:steps // group]
                                                         for a in range(ahead)]
    return jnp.concatenate(cols, axis=1).reshape(steps // group, 1, (group + ahead) * w)

def _moe_kernel(blk_e_ref, next_e_ref, nused_ref, src_ref, roww_ref,
                h2_hbm, wg_hbm, wu_hbm, wd_hbm, y_ref,
                xbuf, x16, wg_f, wu_f, wd_f, wg_b, wu_b, wd_b, gsem, wsem):
    i = pl.program_id(0)
    nused = nused_ref[0]
    nslots = GATHER_AHEAD + 1
    slot = i % nslots
    base = (i % TABLE_GROUP) * MOE_BLOCK
    weights = ((wg_hbm, wg_f, wg_b), (wu_hbm, wu_f, wu_b), (wd_hbm, wd_f, wd_b))

    def gather(first, sl):
        for r in range(MOE_BLOCK):
            pltpu.make_async_copy(h2_hbm.at[pl.ds(src_ref[0, 0, first + r], 1), :],
                                  xbuf.at[sl, pl.ds(r, 1), :], gsem.at[sl]).start(priority=0)

    def gather_wait(sl):
        pltpu.make_async_copy(h2_hbm.at[pl.ds(0, MOE_BLOCK), :], xbuf.at[sl], gsem.at[sl]).wait()

    def weights_start(e):
        for hbm, f32_buf, _ in weights:
            pltpu.make_async_copy(hbm.at[e], f32_buf, wsem.at[0]).start(priority=1)

    def weights_wait():
        for hbm, f32_buf, _ in weights:
            pltpu.make_async_copy(hbm.at[0], f32_buf, wsem.at[0]).wait()

    @pl.when(i == 0)
    def _():
        weights_start(blk_e_ref[0])
        for a in range(GATHER_AHEAD):
            gather(base + a * MOE_BLOCK, a)

    @pl.when(i >= nused)
    def _():
        y_ref[...] = jnp.zeros(y_ref.shape, y_ref.dtype)

    @pl.when(i < nused)
    def _():
        changed = jnp.logical_or(i == 0, blk_e_ref[i] != blk_e_ref[jnp.maximum(i - 1, 0)])

        @pl.when(changed)
        def _():
            weights_wait()
            for _, f32_buf, b16_buf in weights:
                b16_buf[...] = f32_buf[...].astype(BF16)
            weights_start(next_e_ref[i])

        gather_wait(slot)
        x_lo, x_hi = _unpack_bf16_pairs(xbuf[slot])
        x16[:, :x_lo.shape[1]] = x_lo.astype(BF16)
        x16[:, x_lo.shape[1]:] = x_hi.astype(BF16)

        gather(base + GATHER_AHEAD * MOE_BLOCK, (i + GATHER_AHEAD) % nslots)
        x = x16[...]
        hg = jnp.dot(x, wg_b[...], preferred_element_type=F32)
        hu = jnp.dot(x, wu_b[...], preferred_element_type=F32)
        hid = (hg * jax.nn.sigmoid(hg) * hu).astype(BF16)
        y = jnp.dot(hid, wd_b[...], preferred_element_type=F32) * roww_ref[...]
        y_ref[...] = _pack_bf16_pairs(y.astype(BF16))

        @pl.when(i == nused - 1)
        def _():
            weights_wait()
            for a in range(1, GATHER_AHEAD + 1):
                gather_wait((i + a) % nslots)


def _moe(h2, blk_e, next_e, nused, src, roww, w_gate, w_up, w_down):
    n_blocks = blk_e.shape[0]
    n_exp, d, de = w_gate.shape
    src_tables = _grouped_tables(src.reshape(n_blocks, MOE_BLOCK), ahead=GATHER_AHEAD)
    grid_spec = pltpu.PrefetchScalarGridSpec(
        num_scalar_prefetch=3,
        grid=(n_blocks,),
        in_specs=[pl.BlockSpec((1, 1, src_tables.shape[2]), lambda i, be, ne, nu: (i // TABLE_GROUP, 0, 0),
                               memory_space=pltpu.SMEM),
                  pl.BlockSpec((MOE_BLOCK, 1), lambda i, be, ne, nu: (i, 0)),
                  pl.BlockSpec(memory_space=pl.ANY),
                  pl.BlockSpec(memory_space=pl.ANY),
                  pl.BlockSpec(memory_space=pl.ANY),
                  pl.BlockSpec(memory_space=pl.ANY)],
        out_specs=pl.BlockSpec((MOE_BLOCK, d // 2), lambda i, be, ne, nu: (i, 0)),
        scratch_shapes=[pltpu.VMEM((GATHER_AHEAD + 1, MOE_BLOCK, d // 2), jnp.uint32),
                        pltpu.VMEM((MOE_BLOCK, d), BF16),
                        pltpu.VMEM((d, de), F32), pltpu.VMEM((d, de), F32), pltpu.VMEM((de, d), F32),
                        pltpu.VMEM((d, de), BF16), pltpu.VMEM((d, de), BF16), pltpu.VMEM((de, d), BF16),
                        pltpu.SemaphoreType.DMA((GATHER_AHEAD + 1,)), pltpu.SemaphoreType.DMA((1,))],
    )
    return pl.pallas_call(
        _moe_kernel,
        grid_spec=grid_spec,
        out_shape=jax.ShapeDtypeStruct((n_blocks * MOE_BLOCK, d // 2), jnp.uint32),
        compiler_params=_cparams("arbitrary"),
        name="moe",
    )(blk_e, next_e, nused, src_tables, roww, h2, w_gate, w_up, w_down)


def _dispatch(eidx, wtok, rank, counts, n):
    a_tot = 2 * n
    experts = jnp.arange(N_EXPERTS, dtype=jnp.int32)
    padded = (counts + MOE_BLOCK - 1) // MOE_BLOCK * MOE_BLOCK
    pad_end = jnp.sum(jnp.where(experts[None, :] <= experts[:, None], padded[None, :], 0), axis=1)
    pad_start = pad_end - padded
    dest = jnp.sum(jnp.where(eidx[:, :, None] == experts, pad_start, 0), axis=-1) + rank
    n_blocks = -(-(a_tot + N_EXPERTS * (MOE_BLOCK - 1)) // MOE_BLOCK)
    n_blocks = -(-n_blocks // TABLE_GROUP) * TABLE_GROUP
    n_rows = n_blocks * MOE_BLOCK
    tok = jnp.tile(jnp.arange(n, dtype=jnp.int32), 2)
    upd = jnp.stack([tok, lax.bitcast_convert_type(wtok.reshape(-1), jnp.int32)], axis=1)
    rows = jnp.zeros((n_rows, 2), jnp.int32).at[dest.reshape(-1)].set(upd, unique_indices=True)
    blk_start = jnp.arange(n_blocks, dtype=jnp.int32) * MOE_BLOCK
    blk_e = jnp.minimum(jnp.sum((pad_end[None, :] <= blk_start[:, None]).astype(jnp.int32), axis=1),
                        N_EXPERTS - 1)
    nused = (pad_end[-1] // MOE_BLOCK).reshape(1)
    roww = lax.bitcast_convert_type(rows[:, 1], F32)
    later = jnp.where((counts > 0)[None, :] & (experts[None, :] > experts[:, None]), experts[None, :], N_EXPERTS)
    next_of = jnp.min(later, axis=1)
    next_of = jnp.where(next_of < N_EXPERTS, next_of, experts)
    next_e = jnp.sum(jnp.where(blk_e[:, None] == experts[None, :], next_of[None, :], 0), axis=1)
    return blk_e, next_e, nused, rows[:, 0].reshape(n_blocks, 1, MOE_BLOCK), roww.reshape(n_rows, 1), dest


def _final_kernel(d0_ref, d1_ref, x1_ref, gate2_ref, fg_ref, yb_hbm,
                  op_ref, os_ref, ybuf, sem, *, groups, prompt_tiles, tiles, table_group):
    i = pl.program_id(0)
    slot = i % 2
    tm = x1_ref.shape[0]
    base = (i % table_group) * tm

    def gather(first, sl):
        for r in range(tm):
            pltpu.make_async_copy(yb_hbm.at[pl.ds(d0_ref[0, 0, first + r], 1), :],
                                  ybuf.at[sl, 0, pl.ds(r, 1), :], sem.at[sl]).start(priority=0)
            pltpu.make_async_copy(yb_hbm.at[pl.ds(d1_ref[0, 0, first + r], 1), :],
                                  ybuf.at[sl, 1, pl.ds(r, 1), :], sem.at[sl]).start(priority=1)

    def gather_wait(sl):
        for k in range(2):
            pltpu.make_async_copy(yb_hbm.at[pl.ds(0, tm), :], ybuf.at[sl, k], sem.at[sl]).wait()

    @pl.when(i == 0)
    def _():
        gather(base, 0)

    gather_wait(slot)
    gather(base + tm, 1 - slot)

    def run(o_ref):
        def body(g, carry):
            r = pl.multiple_of(g * GROUP_ROWS, GROUP_ROWS)
            lo0, hi0 = _unpack_bf16_pairs(ybuf[slot, 0, pl.ds(r, GROUP_ROWS), :])
            lo1, hi1 = _unpack_bf16_pairs(ybuf[slot, 1, pl.ds(r, GROUP_ROWS), :])
            moe = jnp.concatenate([lo0 + lo1, hi0 + hi1], axis=1)
            x2 = x1_ref[pl.ds(r, GROUP_ROWS), :] + gate2_ref[pl.ds(g, 1), :] * moe
            ms = jnp.mean(x2 * x2, axis=-1, keepdims=True)
            o_ref[pl.ds(r, GROUP_ROWS), :] = x2 * lax.rsqrt(ms + RMS_EPS) * fg_ref[...]
            return carry

        lax.fori_loop(0, groups, body, 0)

    pl.when(i < prompt_tiles)(lambda: run(op_ref))
    pl.when(i >= prompt_tiles)(lambda: run(os_ref))

    @pl.when(i == tiles - 1)
    def _():
        gather_wait(1 - slot)


def _final(x1, yb, dest, gate2_g, final_g, n_prompt, tm):
    n, d = x1.shape
    groups = tm // GROUP_ROWS
    tiles = n // tm
    prompt_tiles = n_prompt // tm
    table_group = max(g for g in range(1, TABLE_GROUP + 1) if tiles % g == 0)
    tables = [_grouped_tables(dest[k].reshape(tiles, tm), table_group) for k in range(2)]
    idx_spec = pl.BlockSpec((1, 1, tables[0].shape[2]), lambda i: (i // table_group, 0, 0),
                            memory_space=pltpu.SMEM)
    return pl.pallas_call(
        functools.partial(_final_kernel, groups=groups, prompt_tiles=prompt_tiles, tiles=tiles,
                          table_group=table_group),
        grid=(tiles,),
        in_specs=[idx_spec, idx_spec,
                  pl.BlockSpec((tm, d), lambda i: (i, 0)),
                  pl.BlockSpec((groups, d), lambda i: (i, 0)),
                  pl.BlockSpec((1, d), lambda i: (0, 0)),
                  pl.BlockSpec(memory_space=pl.ANY)],
        out_specs=_two_stream_specs(tm, d, prompt_tiles),
        out_shape=[jax.ShapeDtypeStruct((n_prompt, d), F32),
                   jax.ShapeDtypeStruct((n - n_prompt, d), F32)],
        scratch_shapes=[pltpu.VMEM((2, 2, tm, d // 2), jnp.uint32), pltpu.SemaphoreType.DMA((2,))],
        compiler_params=_cparams("arbitrary"),
        name="final",
    )(tables[0], tables[1], x1, gate2_g, final_g.reshape(1, d), yb)


def _layer(x_p, x_s, c_all, cache_k, cache_v, sconv, rel_table, lp, seq, t_new):
    (n1, n2, w_ada, b_ada, w_in, sink, w_ao, conv_w, conv_b, ln_g, ln_b, w_pw2, b_pw2, w_out,
     w_grp, b_grp, w_rt, b_rt, w_gate, w_up, w_down) = lp
    n_prompt, d = x_p.shape
    n = n_prompt + x_s.shape[0]
    batch = n_prompt // seq
    dec_batch = (n - n_prompt) // t_new
    cdim = conv_w.shape[1]
    tm = 256

    c_rows = -(-c_all.shape[0] // 8) * 8
    c_pad = jnp.pad(c_all, ((0, c_rows - c_all.shape[0]), (0, 0)))
    mods = _ada(c_pad, w_ada, b_ada)

    def per_group(k):
        m = mods[:, k * d:(k + 1) * d]
        mp = jnp.broadcast_to(m[:batch, None, :], (batch, seq // GROUP_ROWS, d))
        return jnp.concatenate([mp.reshape(batch * (seq // GROUP_ROWS), d), m[batch:batch + dec_batch]],
                               axis=0)

    shift1, scale1, gate1, shift2, scale2, gate2 = [per_group(k) for k in range(6)]

    h1 = _norm1(x_p, x_s, n1, scale1, shift1, 512)

    k_off = ATTN_DIM
    glu_off = k_off + 2 * KV_DIM
    gate_off = glu_off + 2 * cdim
    w_in_b = w_in.astype(BF16)
    tglu = 512
    w_glu = jnp.stack([w_in_b[:, glu_off:glu_off + cdim].reshape(d, cdim // tglu, tglu),
                       w_in_b[:, glu_off + cdim:gate_off].reshape(d, cdim // tglu, tglu)],
                      axis=2).reshape(d, 2 * cdim)
    tmm = 1536
    (q,) = _proj(h1, w_in_b[:, :k_off], "qkv", tmm, 1024, [BF16])
    (kv,) = _proj(h1, w_in_b[:, k_off:glu_off], "qkv", tmm, 512, [BF16])
    kv_state = _kv_state(h1, w_in_b[:, k_off:glu_off], n_prompt, seq)
    (u,) = _proj(h1, w_glu, "glu", tmm, 2 * tglu, [F32])
    (gates,) = _proj(h1, w_in_b[:, gate_off:], "gate", tmm, 1024, [BF16])

    bias_p = _pair_bias(rel_table)
    buf = cache_k.shape[1]
    kpos = jnp.concatenate([jnp.arange(buf, dtype=jnp.int32) - buf, jnp.arange(t_new, dtype=jnp.int32)])
    bias_s = _rel_bias(rel_table, kpos[None, :] - jnp.arange(t_new, dtype=jnp.int32)[:, None])
    sink_f = sink.astype(F32).reshape(N_KV_HEADS, Q_PER_KV, 1, 1)
    sink_kch = sink.astype(F32).reshape(N_KV_HEADS, 1, COLS_PER_KV, HEADS_PER_COL).transpose(0, 3, 1, 2)
    sink_p = jnp.broadcast_to(sink_kch[..., None, None],
                              (N_KV_HEADS, HEADS_PER_COL, 2, COLS_PER_KV, CHUNK, LANES)).reshape(
        N_KV_HEADS, HEADS_PER_COL, 2 * COLS_PER_KV * CHUNK, LANES)
    sink_s = jnp.broadcast_to(sink_f, (N_KV_HEADS, Q_PER_KV, t_new, 1)).reshape(N_KV_HEADS, Q_PER_KV * t_new, 1)
    o_p = _attn_prompt(q, kv, bias_p, sink_p, n_prompt, seq, 4)
    o_s = _attn_sample(q, kv, cache_k.reshape(dec_batch, buf, KV_DIM), cache_v.reshape(dec_batch, buf, KV_DIM),
                       bias_s, sink_s, n_prompt, t_new)

    hist = GROUP_ROWS
    sconv_pad = jnp.pad(sconv, ((0, 0), (hist - sconv.shape[1], 0), (0, 0)))
    w_rt_t = jnp.zeros((ROUTER_ROWS, d), F32)
    w_rt_t = w_rt_t.at[:N_GROUPS].set(w_grp.T).at[EXPERT_ROW0:EXPERT_ROW0 + N_EXPERTS].set(w_rt.T)
    b_rt_t = jnp.zeros((ROUTER_ROWS, 1), F32)
    b_rt_t = b_rt_t.at[:N_GROUPS, 0].set(b_grp.astype(F32)).at[EXPERT_ROW0:EXPERT_ROW0 + N_EXPERTS, 0].set(
        b_rt.astype(F32))
    nch = cdim // CONV_CHUNK
    conv_w_c = conv_w.reshape(CONV_WIDTH, nch, CONV_CHUNK).transpose(1, 0, 2)
    params = dict(conv_w=conv_w_c, conv_b=conv_b.reshape(nch, 1, CONV_CHUNK), ln_g=ln_g.reshape(1, cdim),
                  ln_b=ln_b.reshape(1, cdim), w_pw2=w_pw2.astype(BF16), b_pw2=b_pw2.reshape(1, d),
                  w_ao=w_ao.astype(BF16), w_out=w_out.astype(BF16), n2=n2.reshape(1, d),
                  w_rt=w_rt_t.astype(BF16), b_rt=b_rt_t)
    x1, h2, eidx, wtok, rank, cnt = _post(x_p, x_s, u, sconv_pad, o_p, o_s, gates, gate1, shift2, scale2,
                                             params, n_prompt, seq, tm)

    blk_e, next_e, nused, src, roww, dest = _dispatch(eidx, wtok, rank, cnt[:, 0].astype(jnp.int32), n)
    yb = _moe(h2, blk_e, next_e, nused, src, roww, w_gate, w_up, w_down)
    return x1, yb, dest, gate2, kv_state, u


def kernel(x_prompt, x_sample, c_prompt, c_sample, cache_k, cache_v, state_conv, rel_bias_table, norm1_g, norm2_g, w_ada, b_ada, w_in, attn_sink, w_attn_o, conv_w, conv_b, conv_ln_g, conv_ln_b, w_pw2, b_pw2, w_out, w_group, b_group, w_router, b_router, w_gate, w_up, w_down, final_g):
    batch, seq, d = x_prompt.shape
    dec_batch, t_new, _ = x_sample.shape
    depth = norm1_g.shape[0]
    assert depth == 1, "single trunk layer"
    assert t_new == GROUP_ROWS and seq % GROUP_ROWS == 0
    n_prompt = batch * seq
    n = n_prompt + dec_batch * t_new
    x_p = x_prompt.reshape(n_prompt, d)
    x_s = x_sample.reshape(dec_batch * t_new, d)
    c_all = jnp.concatenate([c_prompt, c_sample], axis=0)
    l = 0
    lp = tuple(a.reshape(a.shape[1:]) for a in (
        norm1_g, norm2_g, w_ada, b_ada, w_in, attn_sink, w_attn_o, conv_w, conv_b, conv_ln_g, conv_ln_b,
        w_pw2, b_pw2, w_out, w_group, b_group, w_router, b_router, w_gate, w_up, w_down))
    x1, yb, dest, gate2, kv_state, u = _layer(x_p, x_s, c_all, cache_k[l], cache_v[l], state_conv[l],
                                              rel_bias_table, lp, seq, t_new)
    out_p, out_s = _final(x1, yb, dest, gate2, final_g, n_prompt, 256)

    y_prompt = out_p.reshape(batch, seq, d)
    y_sample = out_s.reshape(dec_batch, t_new, d)
    cdim = u.shape[1]
    kvp = kv_state[:batch * WINDOW].reshape(batch, WINDOW, 2 * KV_DIM)
    new_k_prompt = kvp[..., :KV_DIM].reshape(1, batch, WINDOW, N_KV_HEADS, HEAD_DIM)
    new_v_prompt = kvp[..., KV_DIM:].reshape(1, batch, WINDOW, N_KV_HEADS, HEAD_DIM)
    new_conv_prompt = jnp.stack([u[(b + 1) * seq - (CONV_WIDTH - 1):(b + 1) * seq] for b in range(batch)])[None]
    kvs = kv_state[batch * WINDOW:].reshape(dec_batch, t_new, 2 * KV_DIM)
    buf = cache_k.shape[2]
    k_new = kvs[..., :KV_DIM].reshape(dec_batch, t_new, N_KV_HEADS, HEAD_DIM)
    v_new = kvs[..., KV_DIM:].reshape(dec_batch, t_new, N_KV_HEADS, HEAD_DIM)
    new_k_sample = jnp.concatenate([cache_k[l], k_new], axis=1)[:, -buf:][None]
    new_v_sample = jnp.concatenate([cache_v[l], v_new], axis=1)[:, -buf:][None]
    us = u[n_prompt:].reshape(dec_batch, t_new, cdim)
    new_conv_sample = jnp.concatenate([state_conv[l], us], axis=1)[:, -(CONV_WIDTH - 1):][None]
    return (y_prompt, y_sample, new_k_prompt, new_v_prompt, new_conv_prompt,
            new_k_sample, new_v_sample, new_conv_sample)
```

```python
import functools
import math

import jax
import jax.numpy as jnp
from jax import lax
from jax.experimental import pallas as pl
from jax.experimental.pallas import tpu as pltpu

F32 = jnp.float32
BF16 = jnp.bfloat16

CHUNK = 64
HEAD_DIM = 64
N_Q_HEADS = 16
N_KV_HEADS = 4
Q_PER_KV = N_Q_HEADS // N_KV_HEADS
ATTN_DIM = N_Q_HEADS * HEAD_DIM
KV_DIM = N_KV_HEADS * HEAD_DIM
WINDOW = 128
WIN_CHUNKS = WINDOW // CHUNK
BAND = (WIN_CHUNKS + 1) * CHUNK
CONV_WIDTH = 31
NUM_BUCKETS = 32
MAX_DISTANCE = 128
N_GROUPS = 4
EXPERTS_PER_GROUP = 8
N_EXPERTS = N_GROUPS * EXPERTS_PER_GROUP
MOE_BLOCK = 256
RMS_EPS = 1e-6
LN_EPS = 1e-5
NEG_INF = -1e30

GROUP_ROWS = 32
LANES = 128
SUBLANES = 8
CONV_CHUNK = 256
ROUTER_ROWS = 64
EXPERT_ROW0 = 8
V7X_VMEM_LIMIT = 56 * 1024 * 1024


def _cparams(*sem):
    return pltpu.CompilerParams(dimension_semantics=sem, vmem_limit_bytes=V7X_VMEM_LIMIT)


def _ada_kernel(c_ref, w_ref, b_ref, o_ref):
    c = c_ref[...]
    s = c * jax.nn.sigmoid(c)
    o_ref[...] = jnp.dot(s.astype(BF16), w_ref[...].astype(BF16),
                         preferred_element_type=F32) + b_ref[...]


def _ada(c_all, w_ada, b_ada):
    rows, d = c_all.shape
    ncol = w_ada.shape[1]
    tn = 1024
    return pl.pallas_call(
        _ada_kernel,
        grid=(ncol // tn,),
        in_specs=[pl.BlockSpec((rows, d), lambda j: (0, 0)),
                  pl.BlockSpec((d, tn), lambda j: (0, j)),
                  pl.BlockSpec((1, tn), lambda j: (0, j))],
        out_specs=pl.BlockSpec((rows, tn), lambda j: (0, j)),
        out_shape=jax.ShapeDtypeStruct((rows, ncol), F32),
        compiler_params=_cparams("arbitrary"),
        name="ada",
    )(c_all, w_ada, b_ada.reshape(1, ncol))


def _rms_mod(x, gain, scale, shift):
    ms = jnp.mean(x * x, axis=-1, keepdims=True)
    y = x * lax.rsqrt(ms + RMS_EPS) * gain
    return y * (1.0 + scale) + shift


def _pack_bf16_pairs(xb):
    w = xb.shape[1] // 2
    bits = lax.bitcast_convert_type(xb.astype(F32), jnp.uint32)
    return (bits[:, :w] >> 16) | (bits[:, w:] & jnp.uint32(0xFFFF0000))


def _unpack_bf16_pairs(p):
    lo = lax.bitcast_convert_type(p << 16, F32)
    hi = lax.bitcast_convert_type(p & jnp.uint32(0xFFFF0000), F32)
    return lo, hi


def _two_stream_specs(tm, width, prompt_tiles):
    return [pl.BlockSpec((tm, width), lambda i: (jnp.minimum(i, prompt_tiles - 1), 0)),
            pl.BlockSpec((tm, width), lambda i: (jnp.maximum(i - prompt_tiles, 0), 0))]


def _norm1_kernel(xp_ref, xs_ref, g_ref, sc_ref, sh_ref, h_ref, *, groups, prompt_tiles):
    def run(x_ref):
        def body(gi, carry):
            r = pl.multiple_of(gi * GROUP_ROWS, GROUP_ROWS)
            h = _rms_mod(x_ref[pl.ds(r, GROUP_ROWS), :], g_ref[...],
                         sc_ref[pl.ds(gi, 1), :], sh_ref[pl.ds(gi, 1), :])
            h_ref[pl.ds(r, GROUP_ROWS), :] = h.astype(h_ref.dtype)
            return carry

        lax.fori_loop(0, groups, body, 0)

    i = pl.program_id(0)
    pl.when(i < prompt_tiles)(lambda: run(xp_ref))
    pl.when(i >= prompt_tiles)(lambda: run(xs_ref))


def _norm1(x_p, x_s, gain, scale_g, shift_g, tm):
    d = x_p.shape[1]
    n = x_p.shape[0] + x_s.shape[0]
    groups = tm // GROUP_ROWS
    prompt_tiles = x_p.shape[0] // tm
    return pl.pallas_call(
        functools.partial(_norm1_kernel, groups=groups, prompt_tiles=prompt_tiles),
        grid=(n // tm,),
        in_specs=_two_stream_specs(tm, d, prompt_tiles) + [
            pl.BlockSpec((1, d), lambda i: (0, 0)),
            pl.BlockSpec((groups, d), lambda i: (i, 0)),
            pl.BlockSpec((groups, d), lambda i: (i, 0))],
        out_specs=pl.BlockSpec((tm, d), lambda i: (i, 0)),
        out_shape=jax.ShapeDtypeStruct((n, d), BF16),
        compiler_params=_cparams("arbitrary"),
        name="norm1",
    )(x_p, x_s, gain.reshape(1, d), scale_g, shift_g)


def _proj_kernel(h_ref, w_ref, *o_refs, mode):
    acc = jnp.dot(h_ref[...], w_ref[...], preferred_element_type=F32)
    if mode == "qkv":
        o_refs[0][...] = acc.astype(BF16)
    elif mode == "glu":
        half = acc.shape[1] // 2
        o_refs[0][...] = acc[:, :half] * jax.nn.sigmoid(acc[:, half:])
    else:
        o_refs[0][...] = jax.nn.sigmoid(acc).astype(BF16)


def _proj(h, w, mode, tm, tnw, out_dtypes):
    n, d = h.shape
    ncol = w.shape[1]
    tno = tnw // 2 if mode == "glu" else tnw
    nout = ncol // 2 if mode == "glu" else ncol
    outs = pl.pallas_call(
        functools.partial(_proj_kernel, mode=mode),
        grid=(n // tm, ncol // tnw),
        in_specs=[pl.BlockSpec((tm, d), lambda i, j: (i, 0)),
                  pl.BlockSpec((d, tnw), lambda i, j: (0, j))],
        out_specs=[pl.BlockSpec((tm, tno), lambda i, j: (i, j)) for _ in out_dtypes],
        out_shape=[jax.ShapeDtypeStruct((n, nout), dt) for dt in out_dtypes],
        compiler_params=_cparams("arbitrary", "arbitrary"),
        name="proj_" + mode,
    )(h, w)
    return outs


def _kv_state_kernel(h_ref, w_ref, o_ref):
    o_ref[...] = jnp.dot(h_ref[...], w_ref[...], preferred_element_type=F32)


def _kv_state(h, w_kv, n_prompt, seq):
    n, d = h.shape
    batch = n_prompt // seq
    per_seq = seq // WINDOW
    steps = batch + (n - n_prompt) // WINDOW

    def row_block(i):
        return jnp.where(i < batch, (i + 1) * per_seq - 1, n_prompt // WINDOW + i - batch)

    return pl.pallas_call(
        _kv_state_kernel,
        grid=(steps,),
        in_specs=[pl.BlockSpec((WINDOW, d), lambda i: (row_block(i), 0)),
                  pl.BlockSpec((d, 2 * KV_DIM), lambda i: (0, 0))],
        out_specs=pl.BlockSpec((WINDOW, 2 * KV_DIM), lambda i: (i, 0)),
        out_shape=jax.ShapeDtypeStruct((steps * WINDOW, 2 * KV_DIM), F32),
        compiler_params=_cparams("arbitrary"),
        name="kv_state",
    )(h, w_kv)


def _t5_bucket(rel):
    nb = NUM_BUCKETS // 2
    n = -rel
    ret = jnp.where(n < 0, nb, 0)
    n = jnp.abs(n)
    max_exact = nb // 2
    nf = jnp.maximum(n, 1).astype(F32)
    large = max_exact + (jnp.log(nf / max_exact) / math.log(MAX_DISTANCE / max_exact)
                         * (nb - max_exact)).astype(jnp.int32)
    large = jnp.minimum(large, nb - 1)
    return ret + jnp.where(n < max_exact, n, large)


def _bias_kernel(tbl_ref, bkt_ref, o_ref):
    bkt = bkt_ref[...]
    nq = bkt.shape[0]
    for k in range(N_KV_HEADS):
        for g in range(Q_PER_KV):
            acc = jnp.zeros(bkt.shape, F32)
            for b in range(NUM_BUCKETS):
                acc = jnp.where(bkt == b, tbl_ref[b, k * Q_PER_KV + g], acc)
            o_ref[k, g * nq:(g + 1) * nq, :] = acc


def _rel_bias(rel_table, rel):
    nq, nk = rel.shape
    return pl.pallas_call(
        _bias_kernel,
        in_specs=[pl.BlockSpec(memory_space=pltpu.SMEM),
                  pl.BlockSpec((nq, nk), lambda: (0, 0))],
        out_specs=pl.BlockSpec((N_KV_HEADS, Q_PER_KV * nq, nk), lambda: (0, 0, 0)),
        out_shape=jax.ShapeDtypeStruct((N_KV_HEADS, Q_PER_KV * nq, nk), F32),
        name="rel_bias",
    )(rel_table.astype(F32), _t5_bucket(rel))


PAIR_BAND = 2 * CHUNK + WINDOW
HEADS_PER_COL = LANES // HEAD_DIM
COLS_PER_KV = Q_PER_KV // HEADS_PER_COL


def _pair_bias_kernel(tbl_ref, bkt_ref, o_ref):
    for e in range(2):
        bkt = bkt_ref[e]
        for k in range(N_KV_HEADS):
            for col in range(COLS_PER_KV):
                for half in range(HEADS_PER_COL):
                    head = k * Q_PER_KV + col * HEADS_PER_COL + half
                    acc = jnp.full(bkt.shape, NEG_INF, F32)
                    for b in range(NUM_BUCKETS):
                        acc = jnp.where(bkt == b, tbl_ref[b, head], acc)
                    r0 = (e * COLS_PER_KV + col) * CHUNK
                    o_ref[k, r0:r0 + CHUNK, half * PAIR_BAND:(half + 1) * PAIR_BAND] = acc


def _pair_bias(rel_table):
    qi = jnp.arange(CHUNK, dtype=jnp.int32)[None, :, None]
    kj = jnp.arange(PAIR_BAND, dtype=jnp.int32)[None, None, :]
    e = jnp.arange(2, dtype=jnp.int32)[:, None, None]
    rel = (kj - WINDOW) - (e * CHUNK + qi)
    key_chunk = kj // CHUNK - e
    seen = (key_chunk >= 0) & (key_chunk <= WIN_CHUNKS)
    bkt = jnp.where(seen, _t5_bucket(rel), -1)
    rows = 2 * COLS_PER_KV * CHUNK
    return pl.pallas_call(
        _pair_bias_kernel,
        in_specs=[pl.BlockSpec(memory_space=pltpu.SMEM),
                  pl.BlockSpec((2, CHUNK, PAIR_BAND), lambda: (0, 0, 0))],
        out_specs=pl.BlockSpec((N_KV_HEADS, rows, HEADS_PER_COL * PAIR_BAND), lambda: (0, 0, 0)),
        out_shape=jax.ShapeDtypeStruct((N_KV_HEADS, rows, HEADS_PER_COL * PAIR_BAND), F32),
        name="pair_bias",
    )(rel_table.astype(F32), bkt)


def _attend(qc, kk_all, vv_all, bias_ref, sink_ref, mask_thr):
    nq = qc.shape[0]
    nk = kk_all.shape[0]
    pieces = []
    for k in range(N_KV_HEADS):
        qs = jnp.concatenate(
            [qc[:, (k * Q_PER_KV + g) * HEAD_DIM:(k * Q_PER_KV + g + 1) * HEAD_DIM]
             for g in range(Q_PER_KV)], axis=0)
        kk = kk_all[:, k * HEAD_DIM:(k + 1) * HEAD_DIM]
        vv = vv_all[:, k * HEAD_DIM:(k + 1) * HEAD_DIM]
        lg = lax.dot_general(qs, kk, (((1,), (1,)), ((), ())),
                             preferred_element_type=F32) * (HEAD_DIM ** -0.5) + bias_ref[k]
        if mask_thr is not None:
            col = lax.broadcasted_iota(jnp.int32, (Q_PER_KV * nq, nk), 1)
            lg = jnp.where(col < mask_thr, NEG_INF, lg)
        s = sink_ref[k]
        m = jnp.maximum(jnp.max(lg, axis=-1, keepdims=True), s)
        p = jnp.exp(lg - m)
        den = jnp.sum(p, axis=-1, keepdims=True) + jnp.exp(s - m)
        o = jnp.dot(p.astype(BF16), vv, preferred_element_type=F32) / den
        pieces.extend(o[g * nq:(g + 1) * nq, :] for g in range(Q_PER_KV))
    return jnp.concatenate(pieces, axis=1)


def _attn_prompt_kernel(q_ref, kv_ref, halo_ref, bias_ref, sink_ref, o_ref, ka, kb, va, vb, *, pairs):
    t = pl.program_id(1)
    kvcat = jnp.concatenate([halo_ref[...], kv_ref[...]], axis=0)
    low = lax.broadcasted_iota(jnp.int32, (kvcat.shape[0], LANES), 1) < HEAD_DIM
    zero = jnp.zeros((kvcat.shape[0], LANES), BF16)
    ones_low = jnp.where(low, 1.0, 0.0).astype(BF16)
    ones_high = jnp.where(low, 0.0, 1.0).astype(BF16)
    kv_cols = KV_DIM // LANES
    for col in range(2 * kv_cols):
        x = kvcat[:, col * LANES:(col + 1) * LANES]
        xs = jnp.concatenate([x[:, HEAD_DIM:], x[:, :HEAD_DIM]], axis=1)
        dst_a, dst_b = (ka, kb) if col < kv_cols else (va, vb)
        k0 = (col % kv_cols) * HEADS_PER_COL
        dst_a[k0, :, 0:LANES] = jnp.where(low, x, zero)
        dst_b[k0, :, 0:LANES] = jnp.where(low, zero, xs)
        dst_a[k0 + 1, :, 0:LANES] = jnp.where(low, xs, zero)
        dst_b[k0 + 1, :, 0:LANES] = jnp.where(low, zero, x)
    for k in range(N_KV_HEADS):
        va[k, :, LANES:2 * LANES] = ones_low
        vb[k, :, LANES:2 * LANES] = ones_high

    out_low = lax.broadcasted_iota(jnp.int32, (2 * COLS_PER_KV * CHUNK, LANES), 1) < HEAD_DIM
    for p in range(pairs):
        r0 = p * 2 * CHUNK
        for k in range(N_KV_HEADS):
            lhs = jnp.concatenate(
                [q_ref[r0 + e * CHUNK:r0 + (e + 1) * CHUNK, (k * COLS_PER_KV + c) * LANES:(k * COLS_PER_KV + c + 1) * LANES]
                 for e in range(2) for c in range(COLS_PER_KV)], axis=0)
            keys = jnp.concatenate([ka[k, r0:r0 + PAIR_BAND, :], kb[k, r0:r0 + PAIR_BAND, :]], axis=0)
            lg = lax.dot_general(lhs, keys, (((1,), (1,)), ((), ())),
                                 preferred_element_type=F32) * (HEAD_DIM ** -0.5) + bias_ref[k]
            probs, sink_terms = [], []
            for half in range(HEADS_PER_COL):
                seg = lg[:, half * PAIR_BAND:(half + 1) * PAIR_BAND]
                if p == 0:
                    kcol = lax.broadcasted_iota(jnp.int32, seg.shape, 1)
                    seg = jnp.where(kcol < jnp.where(t == 0, WINDOW, 0), NEG_INF, seg)
                s = sink_ref[k, half]
                folded = seg[:, :LANES]
                for j in range(1, PAIR_BAND // LANES):
                    folded = jnp.maximum(folded, seg[:, j * LANES:(j + 1) * LANES])
                m = jnp.maximum(jnp.max(folded, axis=-1, keepdims=True), s)
                pr = jnp.exp(seg - jnp.concatenate([m] * (PAIR_BAND // LANES), axis=1))
                sink_terms.append(jnp.exp(s - m))
                probs.append(pr.astype(BF16))
            vals = jnp.concatenate([va[k, r0:r0 + PAIR_BAND, :], vb[k, r0:r0 + PAIR_BAND, :]], axis=0)
            oe = jnp.dot(jnp.concatenate(probs, axis=1), vals, preferred_element_type=F32)
            o = oe[:, :LANES] / (oe[:, LANES:] + jnp.where(out_low, sink_terms[0], sink_terms[1]))
            for e in range(2):
                for c in range(COLS_PER_KV):
                    rr = (e * COLS_PER_KV + c) * CHUNK
                    o_ref[r0 + e * CHUNK:r0 + (e + 1) * CHUNK,
                          (k * COLS_PER_KV + c) * LANES:(k * COLS_PER_KV + c + 1) * LANES] = (
                        o[rr:rr + CHUNK, :].astype(o_ref.dtype))


def _attn_prompt(q, kv, bias, sink_rows, n_prompt, seq, pairs):
    rows = pairs * 2 * CHUNK
    tiles = seq // rows
    halo_per_tile = rows // WINDOW
    keys = WINDOW + rows
    return pl.pallas_call(
        functools.partial(_attn_prompt_kernel, pairs=pairs),
        grid=(n_prompt // seq, tiles),
        scratch_shapes=[pltpu.VMEM((N_KV_HEADS, keys, LANES), BF16) for _ in range(2)]
        + [pltpu.VMEM((N_KV_HEADS, keys, 2 * LANES), BF16) for _ in range(2)],
        in_specs=[pl.BlockSpec((rows, ATTN_DIM), lambda b, t: (b * tiles + t, 0)),
                  pl.BlockSpec((rows, 2 * KV_DIM), lambda b, t: (b * tiles + t, 0)),
                  pl.BlockSpec((WINDOW, 2 * KV_DIM),
                               lambda b, t: (jnp.maximum((b * tiles + t) * halo_per_tile - 1, 0), 0)),
                  pl.BlockSpec(bias.shape, lambda b, t: (0, 0, 0)),
                  pl.BlockSpec(sink_rows.shape, lambda b, t: (0, 0, 0, 0))],
        out_specs=pl.BlockSpec((rows, ATTN_DIM), lambda b, t: (b * tiles + t, 0)),
        out_shape=jax.ShapeDtypeStruct((n_prompt, ATTN_DIM), BF16),
        compiler_params=_cparams("arbitrary", "arbitrary"),
        name="attn_prompt",
    )(q, kv, kv, bias, sink_rows)


def _attn_sample_kernel(q_ref, kv_ref, ck_ref, cv_ref, bias_ref, sink_ref, o_ref):
    kv = kv_ref[...]
    kk = jnp.concatenate([ck_ref[0].astype(BF16), kv[:, :KV_DIM]], axis=0)
    vv = jnp.concatenate([cv_ref[0].astype(BF16), kv[:, KV_DIM:]], axis=0)
    o = _attend(q_ref[...], kk, vv, bias_ref, sink_ref, None)
    o_ref[...] = o.astype(o_ref.dtype)


def _attn_sample(q, kv, cache_k, cache_v, bias, sink_rows, n_prompt, t_new):
    dec_batch, buf, _ = cache_k.shape
    first = n_prompt // t_new
    return pl.pallas_call(
        _attn_sample_kernel,
        grid=(dec_batch,),
        in_specs=[pl.BlockSpec((t_new, ATTN_DIM), lambda s: (first + s, 0)),
                  pl.BlockSpec((t_new, 2 * KV_DIM), lambda s: (first + s, 0)),
                  pl.BlockSpec((1, buf, KV_DIM), lambda s: (s, 0, 0)),
                  pl.BlockSpec((1, buf, KV_DIM), lambda s: (s, 0, 0)),
                  pl.BlockSpec(bias.shape, lambda s: (0, 0, 0)),
                  pl.BlockSpec(sink_rows.shape, lambda s: (0, 0, 0))],
        out_specs=pl.BlockSpec((t_new, ATTN_DIM), lambda s: (s, 0)),
        out_shape=jax.ShapeDtypeStruct((dec_batch * t_new, ATTN_DIM), BF16),
        compiler_params=_cparams("arbitrary"),
        name="attn_sample",
    )(q, kv, cache_k, cache_v, bias, sink_rows)


def _route(lt, eidx_ref, wtok_ref, rank_ref, cnt_ref, cnt_scr):
    gl = [lt[r:r + 1, :] for r in range(N_GROUPS)]
    gmax = gl[0]
    gsel = jnp.zeros(gl[0].shape, jnp.int32)
    for r in range(1, N_GROUPS):
        better = gl[r] > gmax
        gsel = jnp.where(better, r, gsel)
        gmax = jnp.maximum(gmax, gl[r])
    gexp = [jnp.exp(v - gmax) for v in gl]
    gsum = gexp[0]
    for r in range(1, N_GROUPS):
        gsum = gsum + gexp[r]
    psel = jnp.zeros(gl[0].shape, F32)
    for r in range(N_GROUPS):
        psel = jnp.where(gsel == r, gexp[r] / gsum, psel)
    el = jnp.zeros((EXPERTS_PER_GROUP, lt.shape[1]), F32)
    for r in range(N_GROUPS):
        lo = EXPERT_ROW0 + r * EXPERTS_PER_GROUP
        el = jnp.where(gsel == r, lt[lo:lo + EXPERTS_PER_GROUP, :], el)
    emax = jnp.max(el, axis=0, keepdims=True)
    ee = jnp.exp(el - emax)
    pin = ee / jnp.sum(ee, axis=0, keepdims=True)
    idx = lax.broadcasted_iota(jnp.int32, pin.shape, 0)
    p1 = jnp.max(pin, axis=0, keepdims=True)
    i1 = jnp.min(jnp.where(pin == p1, idx, EXPERTS_PER_GROUP), axis=0, keepdims=True)
    rest = jnp.where(idx == i1, -1.0, pin)
    p2 = jnp.max(rest, axis=0, keepdims=True)
    i2 = jnp.min(jnp.where(rest == p2, idx, EXPERTS_PER_GROUP), axis=0, keepdims=True)
    tot = p1 + p2
    e1 = gsel * EXPERTS_PER_GROUP + i1
    e2 = gsel * EXPERTS_PER_GROUP + i2
    eidx_ref[0:1, :] = e1
    eidx_ref[1:2, :] = e2
    wtok_ref[0:1, :] = psel * p1 / tot
    wtok_ref[1:2, :] = psel * p2 / tot

    t = lt.shape[1]
    eiota = lax.broadcasted_iota(jnp.int32, (N_EXPERTS, t), 0)
    oh1 = (eiota == e1).astype(F32)
    oh2 = (eiota == e2).astype(F32)
    both = oh1 + oh2
    before = (lax.broadcasted_iota(jnp.int32, (t, t), 0)
              < lax.broadcasted_iota(jnp.int32, (t, t), 1)).astype(BF16)
    prior = jnp.dot(both.astype(BF16), before, preferred_element_type=F32) + cnt_scr[:, 0:1]
    rank_ref[0:1, :] = jnp.sum(oh1 * prior, axis=0, keepdims=True).astype(jnp.int32)
    rank_ref[1:2, :] = jnp.sum(oh2 * prior, axis=0, keepdims=True).astype(jnp.int32)
    cnt_scr[...] = cnt_scr[...] + jnp.sum(both, axis=1, keepdims=True)
    cnt_ref[...] = cnt_scr[...]


def _post_kernel(xp_ref, xs_ref, u_ref, uhalo_ref, sconv_ref, op_ref, os_ref, gates_ref,
                 gate1_ref, shift2_ref, scale2_ref,
                 cw_ref, cb_ref, lng_ref, lnb_ref, wpw2_ref, bpw2_ref, wao_ref, wout_ref,
                 n2_ref, wrt_ref, brt_ref,
                 x1_ref, h2_ref, eidx_ref, wtok_ref, rank_ref, cnt_ref,
                 uext, shift_scr, d_scr, s_scr, o_scr, mix_scr, h2b_scr, cnt_scr,
                 *, groups, prompt_tiles, tiles_per_seq):
    i = pl.program_id(0)
    d_model = xp_ref.shape[1]

    @pl.when(i == 0)
    def _():
        cnt_scr[...] = jnp.zeros(cnt_scr.shape, F32)
    nch = cw_ref.shape[0]
    cw = uext.shape[2]
    hist = uext.shape[1] - GROUP_ROWS
    lead = hist - (CONV_WIDTH - 1)

    def put_hist(g, rows):
        for c in range(nch):
            uext[g * nch + c, 0:hist, :] = rows[:, c * cw:(c + 1) * cw]

    @pl.when(i < prompt_tiles)
    def _():
        first = (i % tiles_per_seq) == 0
        put_hist(0, jnp.where(first, 0.0, uhalo_ref[...]))
        for g in range(1, groups):
            put_hist(g, u_ref[(g - 1) * GROUP_ROWS:g * GROUP_ROWS, :])
        o_scr[...] = op_ref[...]

    @pl.when(i >= prompt_tiles)
    def _():
        for g in range(groups):
            put_hist(g, sconv_ref[g])
        o_scr[...] = os_ref[...]

    for g in range(groups):
        for c in range(nch):
            uext[g * nch + c, hist:hist + GROUP_ROWS, :] = (
                u_ref[g * GROUP_ROWS:(g + 1) * GROUP_ROWS, c * cw:(c + 1) * cw])

    span = shift_scr.shape[1]

    def conv_chunk(k, carry):
        c = k % nch
        win_all = uext[k]
        for s in range(1, SUBLANES):
            shift_scr[s, :, :] = pltpu.roll(win_all, win_all.shape[0] - s, 0)[0:span, :]
        acc = None
        for j in range(CONV_WIDTH):
            base, s = divmod(lead + j, SUBLANES)
            rows = pl.ds(base * SUBLANES, GROUP_ROWS)
            win = uext[k, rows, :] if s == 0 else shift_scr[s, rows, :]
            term = win * cw_ref[c, j:j + 1, :]
            acc = term if acc is None else acc + term
        d_scr[k] = acc + cb_ref[c]
        return carry

    lax.fori_loop(0, groups * nch, conv_chunk, 0)

    for g in range(groups):
        dd = jnp.concatenate([d_scr[g * nch + c] for c in range(nch)], axis=1)
        mu = jnp.mean(dd, axis=-1, keepdims=True)
        var = jnp.mean(jnp.square(dd - mu), axis=-1, keepdims=True)
        y = (dd - mu) * lax.rsqrt(var + LN_EPS) * lng_ref[...] + lnb_ref[...]
        s_scr[g * GROUP_ROWS:(g + 1) * GROUP_ROWS, :] = (y * jax.nn.sigmoid(y)).astype(BF16)

    conv_out = jnp.dot(s_scr[...], wpw2_ref[...], preferred_element_type=F32) + bpw2_ref[...]
    attn_out = jnp.dot(o_scr[...], wao_ref[...], preferred_element_type=F32)
    merged = (gates_ref[:, :d_model].astype(F32) * attn_out
              + gates_ref[:, d_model:].astype(F32) * conv_out)
    mix_scr[...] = jnp.dot(merged.astype(BF16), wout_ref[...], preferred_element_type=F32)

    def residual(x_ref):
        def res_group(g, carry):
            r = pl.multiple_of(g * GROUP_ROWS, GROUP_ROWS)
            x1 = (x_ref[pl.ds(r, GROUP_ROWS), :]
                  + gate1_ref[pl.ds(g, 1), :] * mix_scr[pl.ds(r, GROUP_ROWS), :])
            x1_ref[pl.ds(r, GROUP_ROWS), :] = x1
            h = _rms_mod(x1, n2_ref[...], scale2_ref[pl.ds(g, 1), :], shift2_ref[pl.ds(g, 1), :])
            hb = h.astype(BF16)
            h2b_scr[pl.ds(r, GROUP_ROWS), :] = hb
            h2_ref[pl.ds(r, GROUP_ROWS), :] = _pack_bf16_pairs(hb)
            return carry

        lax.fori_loop(0, groups, res_group, 0)

    pl.when(i < prompt_tiles)(lambda: residual(xp_ref))
    pl.when(i >= prompt_tiles)(lambda: residual(xs_ref))

    lt = lax.dot_general(wrt_ref[...], h2b_scr[...], (((1,), (1,)), ((), ())),
                         preferred_element_type=F32) + brt_ref[...]
    _route(lt, eidx_ref, wtok_ref, rank_ref, cnt_ref, cnt_scr)


def _post(x_p, x_s, u, sconv_pad, o_p, o_s, gates, gate1_g, shift2_g, scale2_g, p, n_prompt, seq, tm):
    d = x_p.shape[1]
    n = x_p.shape[0] + x_s.shape[0]
    cdim = u.shape[1]
    nch = cdim // CONV_CHUNK
    groups = tm // GROUP_ROWS
    prompt_tiles = n_prompt // tm
    const = lambda shape: pl.BlockSpec(shape, lambda i: (0,) * len(shape))
    row = lambda w: pl.BlockSpec((tm, w), lambda i: (i, 0))
    grp = pl.BlockSpec((groups, d), lambda i: (i, 0))
    return pl.pallas_call(
        functools.partial(_post_kernel, groups=groups, prompt_tiles=prompt_tiles,
                          tiles_per_seq=seq // tm),
        grid=(n // tm,),
        in_specs=_two_stream_specs(tm, d, prompt_tiles) + [
                  row(cdim),
                  pl.BlockSpec((GROUP_ROWS, cdim), lambda i: (jnp.maximum(i * groups - 1, 0), 0)),
                  pl.BlockSpec((groups, GROUP_ROWS, cdim),
                               lambda i: (jnp.maximum(i - prompt_tiles, 0), 0, 0))]
                 + _two_stream_specs(tm, ATTN_DIM, prompt_tiles) + [
                  row(2 * d), grp, grp, grp,
                  const((nch, CONV_WIDTH, CONV_CHUNK)), const((nch, 1, CONV_CHUNK)),
                  const((1, cdim)), const((1, cdim)),
                  const((cdim, d)), const((1, d)), const((ATTN_DIM, d)), const((d, d)),
                  const((1, d)), const((ROUTER_ROWS, d)), const((ROUTER_ROWS, 1))],
        out_specs=[row(d),
                   row(d // 2),
                   pl.BlockSpec((2, tm), lambda i: (0, i)),
                   pl.BlockSpec((2, tm), lambda i: (0, i)),
                   pl.BlockSpec((2, tm), lambda i: (0, i)),
                   pl.BlockSpec((N_EXPERTS, LANES), lambda i: (0, 0))],
        out_shape=[jax.ShapeDtypeStruct((n, d), F32),
                   jax.ShapeDtypeStruct((n, d // 2), jnp.uint32),
                   jax.ShapeDtypeStruct((2, n), jnp.int32),
                   jax.ShapeDtypeStruct((2, n), F32),
                   jax.ShapeDtypeStruct((2, n), jnp.int32),
                   jax.ShapeDtypeStruct((N_EXPERTS, LANES), F32)],
        scratch_shapes=[pltpu.VMEM((groups * nch, 2 * GROUP_ROWS, CONV_CHUNK), F32),
                        pltpu.VMEM((SUBLANES, 2 * GROUP_ROWS - SUBLANES, CONV_CHUNK), F32),
                        pltpu.VMEM((groups * nch, GROUP_ROWS, CONV_CHUNK), F32),
                        pltpu.VMEM((tm, cdim), BF16),
                        pltpu.VMEM((tm, ATTN_DIM), BF16),
                        pltpu.VMEM((tm, d), F32),
                        pltpu.VMEM((tm, d), BF16),
                        pltpu.VMEM((N_EXPERTS, LANES), F32)],
        compiler_params=_cparams("arbitrary"),
        name="post",
    )(x_p, x_s, u, u, sconv_pad, o_p, o_s, gates, gate1_g, shift2_g, scale2_g,
      p["conv_w"], p["conv_b"], p["ln_g"], p["ln_b"], p["w_pw2"], p["b_pw2"], p["w_ao"], p["w_out"],
      p["n2"], p["w_rt"], p["b_rt"])


TABLE_GROUP = 8
GATHER_AHEAD = 8
FINAL_AHEAD = 2


def _grouped_tables(tab, group=TABLE_GROUP, ahead=1):
    steps, w = tab.shape
    assert steps % group == 0
    padded = jnp.concatenate([tab] + [tab[-1:]] * ahead, axis=0)
    cols = [tab.reshape(steps // group, group * w)] + [padded[group + a::group][:steps // group]
                                                         for a in range(ahead)]
    return jnp.concatenate(cols, axis=1).reshape(steps // group, 1, (group + ahead) * w)

def _moe_kernel(blk_e_ref, next_e_ref, nused_ref, src_ref, roww_ref,
                h2_hbm, wg_hbm, wu_hbm, wd_hbm, y_ref,
                xbuf, x16, wg_f, wu_f, wd_f, wg_b, wu_b, wd_b, gsem, wsem):
    i = pl.program_id(0)
    nused = nused_ref[0]
    nslots = GATHER_AHEAD + 1
    slot = i % nslots
    base = (i % TABLE_GROUP) * MOE_BLOCK
    weights = ((wg_hbm, wg_f, wg_b), (wu_hbm, wu_f, wu_b), (wd_hbm, wd_f, wd_b))

    def gather(first, sl):
        for r in range(MOE_BLOCK):
            pltpu.make_async_copy(h2_hbm.at[pl.ds(src_ref[0, 0, first + r], 1), :],
                                  xbuf.at[sl, pl.ds(r, 1), :], gsem.at[sl]).start(priority=0)

    def gather_wait(sl):
        pltpu.make_async_copy(h2_hbm.at[pl.ds(0, MOE_BLOCK), :], xbuf.at[sl], gsem.at[sl]).wait()

    def weights_start(e):
        for hbm, f32_buf, _ in weights:
            pltpu.make_async_copy(hbm.at[e], f32_buf, wsem.at[0]).start(priority=1)

    def weights_wait():
        for hbm, f32_buf, _ in weights:
            pltpu.make_async_copy(hbm.at[0], f32_buf, wsem.at[0]).wait()

    @pl.when(i == 0)
    def _():
        weights_start(blk_e_ref[0])
        for a in range(GATHER_AHEAD):
            gather(base + a * MOE_BLOCK, a)

    @pl.when(i >= nused)
    def _():
        y_ref[...] = jnp.zeros(y_ref.shape, y_ref.dtype)

    @pl.when(i < nused)
    def _():
        changed = jnp.logical_or(i == 0, blk_e_ref[i] != blk_e_ref[jnp.maximum(i - 1, 0)])

        @pl.when(changed)
        def _():
            weights_wait()
            for _, f32_buf, b16_buf in weights:
                b16_buf[...] = f32_buf[...].astype(BF16)
            weights_start(next_e_ref[i])

        gather_wait(slot)
        x_lo, x_hi = _unpack_bf16_pairs(xbuf[slot])
        x16[:, :x_lo.shape[1]] = x_lo.astype(BF16)
        x16[:, x_lo.shape[1]:] = x_hi.astype(BF16)

        gather(base + GATHER_AHEAD * MOE_BLOCK, (i + GATHER_AHEAD) % nslots)
        x = x16[...]
        hg = jnp.dot(x, wg_b[...], preferred_element_type=F32)
        hu = jnp.dot(x, wu_b[...], preferred_element_type=F32)
        hid = (hg * jax.nn.sigmoid(hg) * hu).astype(BF16)
        y = jnp.dot(hid, wd_b[...], preferred_element_type=F32) * roww_ref[...]
        y_ref[...] = _pack_bf16_pairs(y.astype(BF16))

        @pl.when(i == nused - 1)
        def _():
            weights_wait()
            for a in range(1, GATHER_AHEAD + 1):
                gather_wait((i + a) % nslots)


def _moe(h2, blk_e, next_e, nused, src, roww, w_gate, w_up, w_down):
    n_blocks = blk_e.shape[0]
    n_exp, d, de = w_gate.shape
    src_tables = _grouped_tables(src.reshape(n_blocks, MOE_BLOCK), ahead=GATHER_AHEAD)
    grid_spec = pltpu.PrefetchScalarGridSpec(
        num_scalar_prefetch=3,
        grid=(n_blocks,),
        in_specs=[pl.BlockSpec((1, 1, src_tables.shape[2]), lambda i, be, ne, nu: (i // TABLE_GROUP, 0, 0),
                               memory_space=pltpu.SMEM),
                  pl.BlockSpec((MOE_BLOCK, 1), lambda i, be, ne, nu: (i, 0)),
                  pl.BlockSpec(memory_space=pl.ANY),
                  pl.BlockSpec(memory_space=pl.ANY),
                  pl.BlockSpec(memory_space=pl.ANY),
                  pl.BlockSpec(memory_space=pl.ANY)],
        out_specs=pl.BlockSpec((MOE_BLOCK, d // 2), lambda i, be, ne, nu: (i, 0)),
        scratch_shapes=[pltpu.VMEM((GATHER_AHEAD + 1, MOE_BLOCK, d // 2), jnp.uint32),
                        pltpu.VMEM((MOE_BLOCK, d), BF16),
                        pltpu.VMEM((d, de), F32), pltpu.VMEM((d, de), F32), pltpu.VMEM((de, d), F32),
                        pltpu.VMEM((d, de), BF16), pltpu.VMEM((d, de), BF16), pltpu.VMEM((de, d), BF16),
                        pltpu.SemaphoreType.DMA((GATHER_AHEAD + 1,)), pltpu.SemaphoreType.DMA((1,))],
    )
    return pl.pallas_call(
        _moe_kernel,
        grid_spec=grid_spec,
        out_shape=jax.ShapeDtypeStruct((n_blocks * MOE_BLOCK, d // 2), jnp.uint32),
        compiler_params=_cparams("arbitrary"),
        name="moe",
    )(blk_e, next_e, nused, src_tables, roww, h2, w_gate, w_up, w_down)


def _dispatch(eidx, wtok, rank, counts, n):
    a_tot = 2 * n
    experts = jnp.arange(N_EXPERTS, dtype=jnp.int32)
    padded = (counts + MOE_BLOCK - 1) // MOE_BLOCK * MOE_BLOCK
    pad_end = jnp.sum(jnp.where(experts[None, :] <= experts[:, None], padded[None, :], 0), axis=1)
    pad_start = pad_end - padded
    dest = jnp.sum(jnp.where(eidx[:, :, None] == experts, pad_start, 0), axis=-1) + rank
    n_blocks = -(-(a_tot + N_EXPERTS * (MOE_BLOCK - 1)) // MOE_BLOCK)
    n_blocks = -(-n_blocks // TABLE_GROUP) * TABLE_GROUP
    n_rows = n_blocks * MOE_BLOCK
    tok = jnp.tile(jnp.arange(n, dtype=jnp.int32), 2)
    upd = jnp.stack([tok, lax.bitcast_convert_type(wtok.reshape(-1), jnp.int32)], axis=1)
    rows = jnp.zeros((n_rows, 2), jnp.int32).at[dest.reshape(-1)].set(upd, unique_indices=True)
    blk_start = jnp.arange(n_blocks, dtype=jnp.int32) * MOE_BLOCK
    blk_e = jnp.minimum(jnp.sum((pad_end[None, :] <= blk_start[:, None]).astype(jnp.int32), axis=1),
                        N_EXPERTS - 1)
    nused = (pad_end[-1] // MOE_BLOCK).reshape(1)
    roww = lax.bitcast_convert_type(rows[:, 1], F32)
    later = jnp.where((counts > 0)[None, :] & (experts[None, :] > experts[:, None]), experts[None, :], N_EXPERTS)
    next_of = jnp.min(later, axis=1)
    next_of = jnp.where(next_of < N_EXPERTS, next_of, experts)
    next_e = jnp.sum(jnp.where(blk_e[:, None] == experts[None, :], next_of[None, :], 0), axis=1)
    return blk_e, next_e, nused, rows[:, 0].reshape(n_blocks, 1, MOE_BLOCK), roww.reshape(n_rows, 1), dest


def _final_kernel(d0_ref, d1_ref, x1_ref, gate2_ref, fg_ref, yb_hbm,
                  op_ref, os_ref, ybuf, sem, *, groups, prompt_tiles, tiles, table_group):
    i = pl.program_id(0)
    nslots = FINAL_AHEAD + 1
    slot = i % nslots
    tm = x1_ref.shape[0]
    base = (i % table_group) * tm

    def gather(first, sl):
        for r in range(tm):
            pltpu.make_async_copy(yb_hbm.at[pl.ds(d0_ref[0, 0, first + r], 1), :],
                                  ybuf.at[sl, 0, pl.ds(r, 1), :], sem.at[sl]).start(priority=0)
            pltpu.make_async_copy(yb_hbm.at[pl.ds(d1_ref[0, 0, first + r], 1), :],
                                  ybuf.at[sl, 1, pl.ds(r, 1), :], sem.at[sl]).start(priority=1)

    def gather_wait(sl):
        for k in range(2):
            pltpu.make_async_copy(yb_hbm.at[pl.ds(0, tm), :], ybuf.at[sl, k], sem.at[sl]).wait()

    @pl.when(i == 0)
    def _():
        for a in range(FINAL_AHEAD):
            gather(base + a * tm, a)

    gather_wait(slot)
    gather(base + FINAL_AHEAD * tm, (i + FINAL_AHEAD) % nslots)

    def run(o_ref):
        def body(g, carry):
            r = pl.multiple_of(g * GROUP_ROWS, GROUP_ROWS)
            lo0, hi0 = _unpack_bf16_pairs(ybuf[slot, 0, pl.ds(r, GROUP_ROWS), :])
            lo1, hi1 = _unpack_bf16_pairs(ybuf[slot, 1, pl.ds(r, GROUP_ROWS), :])
            moe = jnp.concatenate([lo0 + lo1, hi0 + hi1], axis=1)
            x2 = x1_ref[pl.ds(r, GROUP_ROWS), :] + gate2_ref[pl.ds(g, 1), :] * moe
            ms = jnp.mean(x2 * x2, axis=-1, keepdims=True)
            o_ref[pl.ds(r, GROUP_ROWS), :] = x2 * lax.rsqrt(ms + RMS_EPS) * fg_ref[...]
            return carry

        lax.fori_loop(0, groups, body, 0)

    pl.when(i < prompt_tiles)(lambda: run(op_ref))
    pl.when(i >= prompt_tiles)(lambda: run(os_ref))

    @pl.when(i == tiles - 1)
    def _():
        for a in range(1, FINAL_AHEAD + 1):
            gather_wait((i + a) % nslots)


def _final(x1, yb, dest, gate2_g, final_g, n_prompt, tm):
    n, d = x1.shape
    groups = tm // GROUP_ROWS
    tiles = n // tm
    prompt_tiles = n_prompt // tm
    table_group = max(g for g in range(1, TABLE_GROUP + 1) if tiles % g == 0)
    tables = [_grouped_tables(dest[k].reshape(tiles, tm), table_group, FINAL_AHEAD) for k in range(2)]
    idx_spec = pl.BlockSpec((1, 1, tables[0].shape[2]), lambda i: (i // table_group, 0, 0),
                            memory_space=pltpu.SMEM)
    return pl.pallas_call(
        functools.partial(_final_kernel, groups=groups, prompt_tiles=prompt_tiles, tiles=tiles,
                          table_group=table_group),
        grid=(tiles,),
        in_specs=[idx_spec, idx_spec,
                  pl.BlockSpec((tm, d), lambda i: (i, 0)),
                  pl.BlockSpec((groups, d), lambda i: (i, 0)),
                  pl.BlockSpec((1, d), lambda i: (0, 0)),
                  pl.BlockSpec(memory_space=pl.ANY)],
        out_specs=_two_stream_specs(tm, d, prompt_tiles),
        out_shape=[jax.ShapeDtypeStruct((n_prompt, d), F32),
                   jax.ShapeDtypeStruct((n - n_prompt, d), F32)],
        scratch_shapes=[pltpu.VMEM((FINAL_AHEAD + 1, 2, tm, d // 2), jnp.uint32),
                        pltpu.SemaphoreType.DMA((FINAL_AHEAD + 1,))],
        compiler_params=_cparams("arbitrary"),
        name="final",
    )(tables[0], tables[1], x1, gate2_g, final_g.reshape(1, d), yb)


def _layer(x_p, x_s, c_all, cache_k, cache_v, sconv, rel_table, lp, seq, t_new):
    (n1, n2, w_ada, b_ada, w_in, sink, w_ao, conv_w, conv_b, ln_g, ln_b, w_pw2, b_pw2, w_out,
     w_grp, b_grp, w_rt, b_rt, w_gate, w_up, w_down) = lp
    n_prompt, d = x_p.shape
    n = n_prompt + x_s.shape[0]
    batch = n_prompt // seq
    dec_batch = (n - n_prompt) // t_new
    cdim = conv_w.shape[1]
    tm = 256

    c_rows = -(-c_all.shape[0] // 8) * 8
    c_pad = jnp.pad(c_all, ((0, c_rows - c_all.shape[0]), (0, 0)))
    mods = _ada(c_pad, w_ada, b_ada)

    def per_group(k):
        m = mods[:, k * d:(k + 1) * d]
        mp = jnp.broadcast_to(m[:batch, None, :], (batch, seq // GROUP_ROWS, d))
        return jnp.concatenate([mp.reshape(batch * (seq // GROUP_ROWS), d), m[batch:batch + dec_batch]],
                               axis=0)

    shift1, scale1, gate1, shift2, scale2, gate2 = [per_group(k) for k in range(6)]

    h1 = _norm1(x_p, x_s, n1, scale1, shift1, 512)

    k_off = ATTN_DIM
    glu_off = k_off + 2 * KV_DIM
    gate_off = glu_off + 2 * cdim
    w_in_b = w_in.astype(BF16)
    tglu = 512
    w_glu = jnp.stack([w_in_b[:, glu_off:glu_off + cdim].reshape(d, cdim // tglu, tglu),
                       w_in_b[:, glu_off + cdim:gate_off].reshape(d, cdim // tglu, tglu)],
                      axis=2).reshape(d, 2 * cdim)
    tmm = 1536
    (q,) = _proj(h1, w_in_b[:, :k_off], "qkv", tmm, 1024, [BF16])
    (kv,) = _proj(h1, w_in_b[:, k_off:glu_off], "qkv", tmm, 512, [BF16])
    kv_state = _kv_state(h1, w_in_b[:, k_off:glu_off], n_prompt, seq)
    (u,) = _proj(h1, w_glu, "glu", tmm, 2 * tglu, [F32])
    (gates,) = _proj(h1, w_in_b[:, gate_off:], "gate", tmm, 1024, [BF16])

    bias_p = _pair_bias(rel_table)
    buf = cache_k.shape[1]
    kpos = jnp.concatenate([jnp.arange(buf, dtype=jnp.int32) - buf, jnp.arange(t_new, dtype=jnp.int32)])
    bias_s = _rel_bias(rel_table, kpos[None, :] - jnp.arange(t_new, dtype=jnp.int32)[:, None])
    sink_f = sink.astype(F32).reshape(N_KV_HEADS, Q_PER_KV, 1, 1)
    sink_kch = sink.astype(F32).reshape(N_KV_HEADS, 1, COLS_PER_KV, HEADS_PER_COL).transpose(0, 3, 1, 2)
    sink_p = jnp.broadcast_to(sink_kch[..., None, None],
                              (N_KV_HEADS, HEADS_PER_COL, 2, COLS_PER_KV, CHUNK, LANES)).reshape(
        N_KV_HEADS, HEADS_PER_COL, 2 * COLS_PER_KV * CHUNK, LANES)
    sink_s = jnp.broadcast_to(sink_f, (N_KV_HEADS, Q_PER_KV, t_new, 1)).reshape(N_KV_HEADS, Q_PER_KV * t_new, 1)
    o_p = _attn_prompt(q, kv, bias_p, sink_p, n_prompt, seq, 4)
    o_s = _attn_sample(q, kv, cache_k.reshape(dec_batch, buf, KV_DIM), cache_v.reshape(dec_batch, buf, KV_DIM),
                       bias_s, sink_s, n_prompt, t_new)

    hist = GROUP_ROWS
    sconv_pad = jnp.pad(sconv, ((0, 0), (hist - sconv.shape[1], 0), (0, 0)))
    w_rt_t = jnp.zeros((ROUTER_ROWS, d), F32)
    w_rt_t = w_rt_t.at[:N_GROUPS].set(w_grp.T).at[EXPERT_ROW0:EXPERT_ROW0 + N_EXPERTS].set(w_rt.T)
    b_rt_t = jnp.zeros((ROUTER_ROWS, 1), F32)
    b_rt_t = b_rt_t.at[:N_GROUPS, 0].set(b_grp.astype(F32)).at[EXPERT_ROW0:EXPERT_ROW0 + N_EXPERTS, 0].set(
        b_rt.astype(F32))
    nch = cdim // CONV_CHUNK
    conv_w_c = conv_w.reshape(CONV_WIDTH, nch, CONV_CHUNK).transpose(1, 0, 2)
    params = dict(conv_w=conv_w_c, conv_b=conv_b.reshape(nch, 1, CONV_CHUNK), ln_g=ln_g.reshape(1, cdim),
                  ln_b=ln_b.reshape(1, cdim), w_pw2=w_pw2.astype(BF16), b_pw2=b_pw2.reshape(1, d),
                  w_ao=w_ao.astype(BF16), w_out=w_out.astype(BF16), n2=n2.reshape(1, d),
                  w_rt=w_rt_t.astype(BF16), b_rt=b_rt_t)
    x1, h2, eidx, wtok, rank, cnt = _post(x_p, x_s, u, sconv_pad, o_p, o_s, gates, gate1, shift2, scale2,
                                             params, n_prompt, seq, tm)

    blk_e, next_e, nused, src, roww, dest = _dispatch(eidx, wtok, rank, cnt[:, 0].astype(jnp.int32), n)
    yb = _moe(h2, blk_e, next_e, nused, src, roww, w_gate, w_up, w_down)
    return x1, yb, dest, gate2, kv_state, u


def kernel(x_prompt, x_sample, c_prompt, c_sample, cache_k, cache_v, state_conv, rel_bias_table, norm1_g, norm2_g, w_ada, b_ada, w_in, attn_sink, w_attn_o, conv_w, conv_b, conv_ln_g, conv_ln_b, w_pw2, b_pw2, w_out, w_group, b_group, w_router, b_router, w_gate, w_up, w_down, final_g):
    batch, seq, d = x_prompt.shape
    dec_batch, t_new, _ = x_sample.shape
    depth = norm1_g.shape[0]
    assert depth == 1, "single trunk layer"
    assert t_new == GROUP_ROWS and seq % GROUP_ROWS == 0
    n_prompt = batch * seq
    n = n_prompt + dec_batch * t_new
    x_p = x_prompt.reshape(n_prompt, d)
    x_s = x_sample.reshape(dec_batch * t_new, d)
    c_all = jnp.concatenate([c_prompt, c_sample], axis=0)
    l = 0
    lp = tuple(a.reshape(a.shape[1:]) for a in (
        norm1_g, norm2_g, w_ada, b_ada, w_in, attn_sink, w_attn_o, conv_w, conv_b, conv_ln_g, conv_ln_b,
        w_pw2, b_pw2, w_out, w_group, b_group, w_router, b_router, w_gate, w_up, w_down))
    x1, yb, dest, gate2, kv_state, u = _layer(x_p, x_s, c_all, cache_k[l], cache_v[l], state_conv[l],
                                              rel_bias_table, lp, seq, t_new)
    out_p, out_s = _final(x1, yb, dest, gate2, final_g, n_prompt, 256)

    y_prompt = out_p.reshape(batch, seq, d)
    y_sample = out_s.reshape(dec_batch, t_new, d)
    cdim = u.shape[1]
    kvp = kv_state[:batch * WINDOW].reshape(batch, WINDOW, 2 * KV_DIM)
    new_k_prompt = kvp[..., :KV_DIM].reshape(1, batch, WINDOW, N_KV_HEADS, HEAD_DIM)
    new_v_prompt = kvp[..., KV_DIM:].reshape(1, batch, WINDOW, N_KV_HEADS, HEAD_DIM)
    new_conv_prompt = jnp.stack([u[(b + 1) * seq - (CONV_WIDTH - 1):(b + 1) * seq] for b in range(batch)])[None]
    kvs = kv_state[batch * WINDOW:].reshape(dec_batch, t_new, 2 * KV_DIM)
    buf = cache_k.shape[2]
    k_new = kvs[..., :KV_DIM].reshape(dec_batch, t_new, N_KV_HEADS, HEAD_DIM)
    v_new = kvs[..., KV_DIM:].reshape(dec_batch, t_new, N_KV_HEADS, HEAD_DIM)
    new_k_sample = jnp.concatenate([cache_k[l], k_new], axis=1)[:, -buf:][None]
    new_v_sample = jnp.concatenate([cache_v[l], v_new], axis=1)[:, -buf:][None]
    us = u[n_prompt:].reshape(dec_batch, t_new, cdim)
    new_conv_sample = jnp.concatenate([state_conv[l], us], axis=1)[:, -(CONV_WIDTH - 1):][None]
    return (y_prompt, y_sample, new_k_prompt, new_v_prompt, new_conv_prompt,
            new_k_sample, new_v_sample, new_conv_sample)
```

```python
import functools
import math

import jax
import jax.numpy as jnp
from jax import lax
from jax.experimental import pallas as pl
from jax.experimental.pallas import tpu as pltpu

F32 = jnp.float32
BF16 = jnp.bfloat16

CHUNK = 64
HEAD_DIM = 64
N_Q_HEADS = 16
N_KV_HEADS = 4
Q_PER_KV = N_Q_HEADS // N_KV_HEADS
ATTN_DIM = N_Q_HEADS * HEAD_DIM
KV_DIM = N_KV_HEADS * HEAD_DIM
WINDOW = 128
WIN_CHUNKS = WINDOW // CHUNK
CONV_WIDTH = 31
NUM_BUCKETS = 32
MAX_DISTANCE = 128
N_GROUPS = 4
EXPERTS_PER_GROUP = 8
N_EXPERTS = N_GROUPS * EXPERTS_PER_GROUP
MOE_BLOCK = 256
RMS_EPS = 1e-6
LN_EPS = 1e-5
NEG_INF = -1e30

GROUP_ROWS = 32
LANES = 128
SUBLANES = 8
CONV_CHUNK = 256
ROUTER_ROWS = 64
EXPERT_ROW0 = 8

NORM_ROWS = 512
PROJ_ROWS = 1536
PROJ_COLS = 1024
TOKEN_TILE = 256
ATTN_PAIRS = 4
V7X_VMEM_LIMIT = 56 * 1024 * 1024


def _cparams(*sem):
    return pltpu.CompilerParams(dimension_semantics=sem, vmem_limit_bytes=V7X_VMEM_LIMIT)


def _ada_kernel(c_ref, w_ref, b_ref, o_ref):
    c = c_ref[...]
    s = c * jax.nn.sigmoid(c)
    o_ref[...] = jnp.dot(s.astype(BF16), w_ref[...].astype(BF16),
                         preferred_element_type=F32) + b_ref[...]


def _ada(c_all, w_ada, b_ada):
    rows, d = c_all.shape
    ncol = w_ada.shape[1]
    tn = 1024
    return pl.pallas_call(
        _ada_kernel,
        grid=(ncol // tn,),
        in_specs=[pl.BlockSpec((rows, d), lambda j: (0, 0)),
                  pl.BlockSpec((d, tn), lambda j: (0, j)),
                  pl.BlockSpec((1, tn), lambda j: (0, j))],
        out_specs=pl.BlockSpec((rows, tn), lambda j: (0, j)),
        out_shape=jax.ShapeDtypeStruct((rows, ncol), F32),
        compiler_params=_cparams("arbitrary"),
        name="ada",
    )(c_all, w_ada, b_ada.reshape(1, ncol))


def _rms_mod(x, gain, scale, shift):
    ms = jnp.mean(x * x, axis=-1, keepdims=True)
    y = x * lax.rsqrt(ms + RMS_EPS) * gain
    return y * (1.0 + scale) + shift


def _pack_bf16_pairs(xb):
    w = xb.shape[1] // 2
    bits = lax.bitcast_convert_type(xb.astype(F32), jnp.uint32)
    return (bits[:, :w] >> 16) | (bits[:, w:] & jnp.uint32(0xFFFF0000))


def _unpack_bf16_pairs(p):
    lo = lax.bitcast_convert_type(p << 16, F32)
    hi = lax.bitcast_convert_type(p & jnp.uint32(0xFFFF0000), F32)
    return lo, hi


def _two_stream_specs(tm, width, prompt_tiles):
    return [pl.BlockSpec((tm, width), lambda i: (jnp.minimum(i, prompt_tiles - 1), 0)),
            pl.BlockSpec((tm, width), lambda i: (jnp.maximum(i - prompt_tiles, 0), 0))]


def _norm1_kernel(xp_ref, xs_ref, g_ref, sc_ref, sh_ref, h_ref, *, groups, prompt_tiles):
    def run(x_ref):
        def body(gi, carry):
            r = pl.multiple_of(gi * GROUP_ROWS, GROUP_ROWS)
            h = _rms_mod(x_ref[pl.ds(r, GROUP_ROWS), :], g_ref[...],
                         sc_ref[pl.ds(gi, 1), :], sh_ref[pl.ds(gi, 1), :])
            h_ref[pl.ds(r, GROUP_ROWS), :] = h.astype(h_ref.dtype)
            return carry

        lax.fori_loop(0, groups, body, 0)

    i = pl.program_id(0)
    pl.when(i < prompt_tiles)(lambda: run(xp_ref))
    pl.when(i >= prompt_tiles)(lambda: run(xs_ref))


def _norm1(x_p, x_s, gain, scale_g, shift_g, tm):
    d = x_p.shape[1]
    n = x_p.shape[0] + x_s.shape[0]
    groups = tm // GROUP_ROWS
    prompt_tiles = x_p.shape[0] // tm
    return pl.pallas_call(
        functools.partial(_norm1_kernel, groups=groups, prompt_tiles=prompt_tiles),
        grid=(n // tm,),
        in_specs=_two_stream_specs(tm, d, prompt_tiles) + [
            pl.BlockSpec((1, d), lambda i: (0, 0)),
            pl.BlockSpec((groups, d), lambda i: (i, 0)),
            pl.BlockSpec((groups, d), lambda i: (i, 0))],
        out_specs=pl.BlockSpec((tm, d), lambda i: (i, 0)),
        out_shape=jax.ShapeDtypeStruct((n, d), BF16),
        compiler_params=_cparams("arbitrary"),
        name="norm1",
    )(x_p, x_s, gain.reshape(1, d), scale_g, shift_g)


def _proj_kernel(h_ref, w_ref, *o_refs, mode):
    acc = jnp.dot(h_ref[...], w_ref[...], preferred_element_type=F32)
    if mode == "qkv":
        o_refs[0][...] = acc.astype(BF16)
    elif mode == "glu":
        half = acc.shape[1] // 2
        o_refs[0][...] = acc[:, :half] * jax.nn.sigmoid(acc[:, half:])
    else:
        o_refs[0][...] = jax.nn.sigmoid(acc).astype(BF16)


def _proj(h, w, mode, tm, tnw, out_dtypes):
    n, d = h.shape
    ncol = w.shape[1]
    tno = tnw // 2 if mode == "glu" else tnw
    nout = ncol // 2 if mode == "glu" else ncol
    outs = pl.pallas_call(
        functools.partial(_proj_kernel, mode=mode),
        grid=(n // tm, ncol // tnw),
        in_specs=[pl.BlockSpec((tm, d), lambda i, j: (i, 0)),
                  pl.BlockSpec((d, tnw), lambda i, j: (0, j))],
        out_specs=[pl.BlockSpec((tm, tno), lambda i, j: (i, j)) for _ in out_dtypes],
        out_shape=[jax.ShapeDtypeStruct((n, nout), dt) for dt in out_dtypes],
        compiler_params=_cparams("arbitrary", "arbitrary"),
        name="proj_" + mode,
    )(h, w)
    return outs


def _kv_state_kernel(h_ref, w_ref, o_ref):
    o_ref[...] = jnp.dot(h_ref[...], w_ref[...], preferred_element_type=F32)


def _kv_state(h, w_kv, n_prompt, seq):
    n, d = h.shape
    batch = n_prompt // seq
    per_seq = seq // WINDOW
    steps = batch + (n - n_prompt) // WINDOW

    def row_block(i):
        return jnp.where(i < batch, (i + 1) * per_seq - 1, n_prompt // WINDOW + i - batch)

    return pl.pallas_call(
        _kv_state_kernel,
        grid=(steps,),
        in_specs=[pl.BlockSpec((WINDOW, d), lambda i: (row_block(i), 0)),
                  pl.BlockSpec((d, 2 * KV_DIM), lambda i: (0, 0))],
        out_specs=pl.BlockSpec((WINDOW, 2 * KV_DIM), lambda i: (i, 0)),
        out_shape=jax.ShapeDtypeStruct((steps * WINDOW, 2 * KV_DIM), F32),
        compiler_params=_cparams("arbitrary"),
        name="kv_state",
    )(h, w_kv)


def _t5_bucket(rel):
    nb = NUM_BUCKETS // 2
    n = -rel
    ret = jnp.where(n < 0, nb, 0)
    n = jnp.abs(n)
    max_exact = nb // 2
    nf = jnp.maximum(n, 1).astype(F32)
    large = max_exact + (jnp.log(nf / max_exact) / math.log(MAX_DISTANCE / max_exact)
                         * (nb - max_exact)).astype(jnp.int32)
    large = jnp.minimum(large, nb - 1)
    return ret + jnp.where(n < max_exact, n, large)


def _bias_kernel(tbl_ref, bkt_ref, o_ref):
    bkt = bkt_ref[...]
    nq = bkt.shape[0]
    for k in range(N_KV_HEADS):
        for g in range(Q_PER_KV):
            acc = jnp.zeros(bkt.shape, F32)
            for b in range(NUM_BUCKETS):
                acc = jnp.where(bkt == b, tbl_ref[b, k * Q_PER_KV + g], acc)
            o_ref[k, g * nq:(g + 1) * nq, :] = acc


def _rel_bias(rel_table, rel):
    nq, nk = rel.shape
    return pl.pallas_call(
        _bias_kernel,
        in_specs=[pl.BlockSpec(memory_space=pltpu.SMEM),
                  pl.BlockSpec((nq, nk), lambda: (0, 0))],
        out_specs=pl.BlockSpec((N_KV_HEADS, Q_PER_KV * nq, nk), lambda: (0, 0, 0)),
        out_shape=jax.ShapeDtypeStruct((N_KV_HEADS, Q_PER_KV * nq, nk), F32),
        name="rel_bias",
    )(rel_table.astype(F32), _t5_bucket(rel))


PAIR_BAND = 2 * CHUNK + WINDOW
HEADS_PER_COL = LANES // HEAD_DIM
COLS_PER_KV = Q_PER_KV // HEADS_PER_COL


def _pair_bias_kernel(tbl_ref, bkt_ref, o_ref):
    for e in range(2):
        bkt = bkt_ref[e]
        for k in range(N_KV_HEADS):
            for col in range(COLS_PER_KV):
                for half in range(HEADS_PER_COL):
                    head = k * Q_PER_KV + col * HEADS_PER_COL + half
                    acc = jnp.full(bkt.shape, NEG_INF, F32)
                    for b in range(NUM_BUCKETS):
                        acc = jnp.where(bkt == b, tbl_ref[b, head], acc)
                    r0 = (e * COLS_PER_KV + col) * CHUNK
                    o_ref[k, r0:r0 + CHUNK, half * PAIR_BAND:(half + 1) * PAIR_BAND] = acc


def _pair_bias(rel_table):
    qi = jnp.arange(CHUNK, dtype=jnp.int32)[None, :, None]
    kj = jnp.arange(PAIR_BAND, dtype=jnp.int32)[None, None, :]
    e = jnp.arange(2, dtype=jnp.int32)[:, None, None]
    rel = (kj - WINDOW) - (e * CHUNK + qi)
    key_chunk = kj // CHUNK - e
    seen = (key_chunk >= 0) & (key_chunk <= WIN_CHUNKS)
    bkt = jnp.where(seen, _t5_bucket(rel), -1)
    rows = 2 * COLS_PER_KV * CHUNK
    return pl.pallas_call(
        _pair_bias_kernel,
        in_specs=[pl.BlockSpec(memory_space=pltpu.SMEM),
                  pl.BlockSpec((2, CHUNK, PAIR_BAND), lambda: (0, 0, 0))],
        out_specs=pl.BlockSpec((N_KV_HEADS, rows, HEADS_PER_COL * PAIR_BAND), lambda: (0, 0, 0)),
        out_shape=jax.ShapeDtypeStruct((N_KV_HEADS, rows, HEADS_PER_COL * PAIR_BAND), F32),
        name="pair_bias",
    )(rel_table.astype(F32), bkt)


def _attend(qc, kk_all, vv_all, bias_ref, sink_ref, mask_thr):
    nq = qc.shape[0]
    nk = kk_all.shape[0]
    pieces = []
    for k in range(N_KV_HEADS):
        qs = jnp.concatenate(
            [qc[:, (k * Q_PER_KV + g) * HEAD_DIM:(k * Q_PER_KV + g + 1) * HEAD_DIM]
             for g in range(Q_PER_KV)], axis=0)
        kk = kk_all[:, k * HEAD_DIM:(k + 1) * HEAD_DIM]
        vv = vv_all[:, k * HEAD_DIM:(k + 1) * HEAD_DIM]
        lg = lax.dot_general(qs, kk, (((1,), (1,)), ((), ())),
                             preferred_element_type=F32) * (HEAD_DIM ** -0.5) + bias_ref[k]
        if mask_thr is not None:
            col = lax.broadcasted_iota(jnp.int32, (Q_PER_KV * nq, nk), 1)
            lg = jnp.where(col < mask_thr, NEG_INF, lg)
        s = sink_ref[k]
        m = jnp.maximum(jnp.max(lg, axis=-1, keepdims=True), s)
        p = jnp.exp(lg - m)
        den = jnp.sum(p, axis=-1, keepdims=True) + jnp.exp(s - m)
        o = jnp.dot(p.astype(BF16), vv, preferred_element_type=F32) / den
        pieces.extend(o[g * nq:(g + 1) * nq, :] for g in range(Q_PER_KV))
    return jnp.concatenate(pieces, axis=1)


def _attn_prompt_kernel(q_ref, kv_ref, halo_ref, bias_ref, sink_ref, o_ref, ka, kb, va, vb, *, pairs):
    t = pl.program_id(1)
    kvcat = jnp.concatenate([halo_ref[...], kv_ref[...]], axis=0)
    low = lax.broadcasted_iota(jnp.int32, (kvcat.shape[0], LANES), 1) < HEAD_DIM
    zero = jnp.zeros((kvcat.shape[0], LANES), BF16)
    ones_low = jnp.where(low, 1.0, 0.0).astype(BF16)
    ones_high = jnp.where(low, 0.0, 1.0).astype(BF16)
    kv_cols = KV_DIM // LANES
    for col in range(2 * kv_cols):
        x = kvcat[:, col * LANES:(col + 1) * LANES]
        xs = jnp.concatenate([x[:, HEAD_DIM:], x[:, :HEAD_DIM]], axis=1)
        dst_a, dst_b = (ka, kb) if col < kv_cols else (va, vb)
        k0 = (col % kv_cols) * HEADS_PER_COL
        dst_a[k0, :, 0:LANES] = jnp.where(low, x, zero)
        dst_b[k0, :, 0:LANES] = jnp.where(low, zero, xs)
        dst_a[k0 + 1, :, 0:LANES] = jnp.where(low, xs, zero)
        dst_b[k0 + 1, :, 0:LANES] = jnp.where(low, zero, x)
    for k in range(N_KV_HEADS):
        va[k, :, LANES:2 * LANES] = ones_low
        vb[k, :, LANES:2 * LANES] = ones_high

    out_low = lax.broadcasted_iota(jnp.int32, (2 * COLS_PER_KV * CHUNK, LANES), 1) < HEAD_DIM
    for p in range(pairs):
        r0 = p * 2 * CHUNK
        for k in range(N_KV_HEADS):
            lhs = jnp.concatenate(
                [q_ref[r0 + e * CHUNK:r0 + (e + 1) * CHUNK, (k * COLS_PER_KV + c) * LANES:(k * COLS_PER_KV + c + 1) * LANES]
                 for e in range(2) for c in range(COLS_PER_KV)], axis=0)
            keys = jnp.concatenate([ka[k, r0:r0 + PAIR_BAND, :], kb[k, r0:r0 + PAIR_BAND, :]], axis=0)
            lg = lax.dot_general(lhs, keys, (((1,), (1,)), ((), ())),
                                 preferred_element_type=F32) * (HEAD_DIM ** -0.5) + bias_ref[k]
            probs, sink_terms = [], []
            for half in range(HEADS_PER_COL):
                seg = lg[:, half * PAIR_BAND:(half + 1) * PAIR_BAND]
                if p == 0:
                    kcol = lax.broadcasted_iota(jnp.int32, seg.shape, 1)
                    seg = jnp.where(kcol < jnp.where(t == 0, WINDOW, 0), NEG_INF, seg)
                s = sink_ref[k, half]
                folded = seg[:, :LANES]
                for j in range(1, PAIR_BAND // LANES):
                    folded = jnp.maximum(folded, seg[:, j * LANES:(j + 1) * LANES])
                m = jnp.maximum(jnp.max(folded, axis=-1, keepdims=True), s)
                pr = jnp.exp(seg - jnp.concatenate([m] * (PAIR_BAND // LANES), axis=1))
                sink_terms.append(jnp.exp(s - m))
                probs.append(pr.astype(BF16))
            vals = jnp.concatenate([va[k, r0:r0 + PAIR_BAND, :], vb[k, r0:r0 + PAIR_BAND, :]], axis=0)
            oe = jnp.dot(jnp.concatenate(probs, axis=1), vals, preferred_element_type=F32)
            o = oe[:, :LANES] / (oe[:, LANES:] + jnp.where(out_low, sink_terms[0], sink_terms[1]))
            for e in range(2):
                for c in range(COLS_PER_KV):
                    rr = (e * COLS_PER_KV + c) * CHUNK
                    o_ref[r0 + e * CHUNK:r0 + (e + 1) * CHUNK,
                          (k * COLS_PER_KV + c) * LANES:(k * COLS_PER_KV + c + 1) * LANES] = (
                        o[rr:rr + CHUNK, :].astype(o_ref.dtype))


def _attn_prompt(q, kv, bias, sink_rows, n_prompt, seq, pairs):
    rows = pairs * 2 * CHUNK
    tiles = seq // rows
    halo_per_tile = rows // WINDOW
    keys = WINDOW + rows
    return pl.pallas_call(
        functools.partial(_attn_prompt_kernel, pairs=pairs),
        grid=(n_prompt // seq, tiles),
        scratch_shapes=[pltpu.VMEM((N_KV_HEADS, keys, LANES), BF16) for _ in range(2)]
        + [pltpu.VMEM((N_KV_HEADS, keys, 2 * LANES), BF16) for _ in range(2)],
        in_specs=[pl.BlockSpec((rows, ATTN_DIM), lambda b, t: (b * tiles + t, 0)),
                  pl.BlockSpec((rows, 2 * KV_DIM), lambda b, t: (b * tiles + t, 0)),
                  pl.BlockSpec((WINDOW, 2 * KV_DIM),
                               lambda b, t: (jnp.maximum((b * tiles + t) * halo_per_tile - 1, 0), 0)),
                  pl.BlockSpec(bias.shape, lambda b, t: (0, 0, 0)),
                  pl.BlockSpec(sink_rows.shape, lambda b, t: (0, 0, 0, 0))],
        out_specs=pl.BlockSpec((rows, ATTN_DIM), lambda b, t: (b * tiles + t, 0)),
        out_shape=jax.ShapeDtypeStruct((n_prompt, ATTN_DIM), BF16),
        compiler_params=_cparams("arbitrary", "arbitrary"),
        name="attn_prompt",
    )(q, kv, kv, bias, sink_rows)


def _attn_sample_kernel(q_ref, kv_ref, ck_ref, cv_ref, bias_ref, sink_ref, o_ref):
    kv = kv_ref[...]
    kk = jnp.concatenate([ck_ref[0].astype(BF16), kv[:, :KV_DIM]], axis=0)
    vv = jnp.concatenate([cv_ref[0].astype(BF16), kv[:, KV_DIM:]], axis=0)
    o = _attend(q_ref[...], kk, vv, bias_ref, sink_ref, None)
    o_ref[...] = o.astype(o_ref.dtype)


def _attn_sample(q, kv, cache_k, cache_v, bias, sink_rows, n_prompt, t_new):
    dec_batch, buf, _ = cache_k.shape
    first = n_prompt // t_new
    return pl.pallas_call(
        _attn_sample_kernel,
        grid=(dec_batch,),
        in_specs=[pl.BlockSpec((t_new, ATTN_DIM), lambda s: (first + s, 0)),
                  pl.BlockSpec((t_new, 2 * KV_DIM), lambda s: (first + s, 0)),
                  pl.BlockSpec((1, buf, KV_DIM), lambda s: (s, 0, 0)),
                  pl.BlockSpec((1, buf, KV_DIM), lambda s: (s, 0, 0)),
                  pl.BlockSpec(bias.shape, lambda s: (0, 0, 0)),
                  pl.BlockSpec(sink_rows.shape, lambda s: (0, 0, 0))],
        out_specs=pl.BlockSpec((t_new, ATTN_DIM), lambda s: (s, 0)),
        out_shape=jax.ShapeDtypeStruct((dec_batch * t_new, ATTN_DIM), BF16),
        compiler_params=_cparams("arbitrary"),
        name="attn_sample",
    )(q, kv, cache_k, cache_v, bias, sink_rows)


def _route(lt, t, counted, eidx_ref, wtok_ref, rank_ref, cnt_ref, cnt_scr):
    gl = [lt[r:r + 1, :] for r in range(N_GROUPS)]
    gmax = gl[0]
    gsel = jnp.zeros(gl[0].shape, jnp.int32)
    for r in range(1, N_GROUPS):
        better = gl[r] > gmax
        gsel = jnp.where(better, r, gsel)
        gmax = jnp.maximum(gmax, gl[r])
    gexp = [jnp.exp(v - gmax) for v in gl]
    gsum = gexp[0]
    for r in range(1, N_GROUPS):
        gsum = gsum + gexp[r]
    psel = jnp.zeros(gl[0].shape, F32)
    for r in range(N_GROUPS):
        psel = jnp.where(gsel == r, gexp[r] / gsum, psel)
    el = jnp.zeros((EXPERTS_PER_GROUP, lt.shape[1]), F32)
    for r in range(N_GROUPS):
        lo = EXPERT_ROW0 + r * EXPERTS_PER_GROUP
        el = jnp.where(gsel == r, lt[lo:lo + EXPERTS_PER_GROUP, :], el)
    emax = jnp.max(el, axis=0, keepdims=True)
    ee = jnp.exp(el - emax)
    pin = ee / jnp.sum(ee, axis=0, keepdims=True)
    idx = lax.broadcasted_iota(jnp.int32, pin.shape, 0)
    p1 = jnp.max(pin, axis=0, keepdims=True)
    i1 = jnp.min(jnp.where(pin == p1, idx, EXPERTS_PER_GROUP), axis=0, keepdims=True)
    rest = jnp.where(idx == i1, -1.0, pin)
    p2 = jnp.max(rest, axis=0, keepdims=True)
    i2 = jnp.min(jnp.where(rest == p2, idx, EXPERTS_PER_GROUP), axis=0, keepdims=True)
    tot = p1 + p2
    e1 = gsel * EXPERTS_PER_GROUP + i1
    e2 = gsel * EXPERTS_PER_GROUP + i2
    eidx_ref[t, 0:1, :] = e1
    eidx_ref[t, 1:2, :] = e2
    wtok_ref[t, 0:1, :] = psel * p1 / tot
    wtok_ref[t, 1:2, :] = psel * p2 / tot

    rows = lt.shape[1]
    eiota = lax.broadcasted_iota(jnp.int32, (N_EXPERTS, rows), 0)
    oh1 = (eiota == e1).astype(F32)
    oh2 = (eiota == e2).astype(F32)
    both = oh1 + oh2
    before = (lax.broadcasted_iota(jnp.int32, (rows, rows), 0)
              < lax.broadcasted_iota(jnp.int32, (rows, rows), 1)).astype(BF16)
    prior = jnp.dot(both.astype(BF16), before, preferred_element_type=F32) + cnt_scr[:, 0:1]
    rank_ref[t, 0:1, :] = jnp.sum(oh1 * prior, axis=0, keepdims=True).astype(jnp.int32)
    rank_ref[t, 1:2, :] = jnp.sum(oh2 * prior, axis=0, keepdims=True).astype(jnp.int32)
    cnt_scr[...] = cnt_scr[...] + jnp.where(counted, jnp.sum(both, axis=1, keepdims=True), 0.0)
    cnt_ref[...] = cnt_scr[...]


def _post_kernel(xp_ref, xs_ref, u_ref, uhalo_ref, sconv_ref, op_ref, os_ref, gates_ref,
                 gate1_ref, shift2_ref, scale2_ref,
                 cw_ref, cb_ref, lng_ref, lnb_ref, wpw2_ref, bpw2_ref, wao_ref, wout_ref,
                 n2_ref, wrt_ref, brt_ref,
                 x1_ref, h2_ref, eidx_ref, wtok_ref, rank_ref, cnt_ref,
                 uext, shift_scr, d_scr, s_scr, o_scr, mix_scr, h2b_scr, cnt_scr,
                 *, groups, prompt_tiles, tiles_per_seq, tiles):
    i = pl.program_id(0)
    d_model = xp_ref.shape[1]

    @pl.when(i == 0)
    def _():
        cnt_scr[...] = jnp.zeros(cnt_scr.shape, F32)
        h2b_scr[...] = jnp.zeros(h2b_scr.shape, BF16)
    nch = cw_ref.shape[0]
    cw = uext.shape[2]
    hist = uext.shape[1] - GROUP_ROWS
    lead = hist - (CONV_WIDTH - 1)

    def put_hist(g, rows):
        for c in range(nch):
            uext[g * nch + c, 0:hist, :] = rows[:, c * cw:(c + 1) * cw]

    @pl.when(i < prompt_tiles)
    def _():
        first = (i % tiles_per_seq) == 0
        put_hist(0, jnp.where(first, 0.0, uhalo_ref[...]))
        for g in range(1, groups):
            put_hist(g, u_ref[(g - 1) * GROUP_ROWS:g * GROUP_ROWS, :])
        o_scr[...] = op_ref[...]

    @pl.when(i >= prompt_tiles)
    def _():
        for g in range(groups):
            put_hist(g, sconv_ref[g])
        o_scr[...] = os_ref[...]

    for g in range(groups):
        for c in range(nch):
            uext[g * nch + c, hist:hist + GROUP_ROWS, :] = (
                u_ref[g * GROUP_ROWS:(g + 1) * GROUP_ROWS, c * cw:(c + 1) * cw])

    span = shift_scr.shape[1]

    def conv_chunk(k, carry):
        c = k % nch
        win_all = uext[k]
        for s in range(1, SUBLANES):
            shift_scr[s, :, :] = pltpu.roll(win_all, win_all.shape[0] - s, 0)[0:span, :]
        acc = None
        for j in range(CONV_WIDTH):
            base, s = divmod(lead + j, SUBLANES)
            rows = pl.ds(base * SUBLANES, GROUP_ROWS)
            win = uext[k, rows, :] if s == 0 else shift_scr[s, rows, :]
            term = win * cw_ref[c, j:j + 1, :]
            acc = term if acc is None else acc + term
        d_scr[k] = acc + cb_ref[c]
        return carry

    lax.fori_loop(0, groups * nch, conv_chunk, 0)

    for g in range(groups):
        dd = jnp.concatenate([d_scr[g * nch + c] for c in range(nch)], axis=1)
        mu = jnp.mean(dd, axis=-1, keepdims=True)
        var = jnp.mean(jnp.square(dd - mu), axis=-1, keepdims=True)
        y = (dd - mu) * lax.rsqrt(var + LN_EPS) * lng_ref[...] + lnb_ref[...]
        s_scr[g * GROUP_ROWS:(g + 1) * GROUP_ROWS, :] = (y * jax.nn.sigmoid(y)).astype(BF16)

    def route_tile(t, counted):
        lt = lax.dot_general(wrt_ref[...], h2b_scr[...], (((1,), (1,)), ((), ())),
                             preferred_element_type=F32) + brt_ref[...]
        _route(lt, t, counted, eidx_ref, wtok_ref, rank_ref, cnt_ref, cnt_scr)

    route_tile(jnp.maximum(i - 1, 0), i > 0)

    conv_out = jnp.dot(s_scr[...], wpw2_ref[...], preferred_element_type=F32) + bpw2_ref[...]
    attn_out = jnp.dot(o_scr[...], wao_ref[...], preferred_element_type=F32)
    merged = (gates_ref[:, :d_model].astype(F32) * attn_out
              + gates_ref[:, d_model:].astype(F32) * conv_out)
    mix_scr[...] = jnp.dot(merged.astype(BF16), wout_ref[...], preferred_element_type=F32)

    def residual(x_ref):
        def res_group(g, carry):
            r = pl.multiple_of(g * GROUP_ROWS, GROUP_ROWS)
            x1 = (x_ref[pl.ds(r, GROUP_ROWS), :]
                  + gate1_ref[pl.ds(g, 1), :] * mix_scr[pl.ds(r, GROUP_ROWS), :])
            x1_ref[pl.ds(r, GROUP_ROWS), :] = x1
            h = _rms_mod(x1, n2_ref[...], scale2_ref[pl.ds(g, 1), :], shift2_ref[pl.ds(g, 1), :])
            hb = h.astype(BF16)
            h2b_scr[pl.ds(r, GROUP_ROWS), :] = hb
            h2_ref[pl.ds(r, GROUP_ROWS), :] = _pack_bf16_pairs(hb)
            return carry

        lax.fori_loop(0, groups, res_group, 0)

    pl.when(i < prompt_tiles)(lambda: residual(xp_ref))
    pl.when(i >= prompt_tiles)(lambda: residual(xs_ref))

    @pl.when(i == tiles - 1)
    def _():
        route_tile(i, True)


def _post(x_p, x_s, u, sconv_pad, o_p, o_s, gates, gate1_g, shift2_g, scale2_g, p, n_prompt, seq, tm):
    d = x_p.shape[1]
    n = x_p.shape[0] + x_s.shape[0]
    cdim = u.shape[1]
    nch = cdim // CONV_CHUNK
    groups = tm // GROUP_ROWS
    tiles = n // tm
    prompt_tiles = n_prompt // tm
    const = lambda shape: pl.BlockSpec(shape, lambda i: (0,) * len(shape))
    row = lambda w: pl.BlockSpec((tm, w), lambda i: (i, 0))
    grp = pl.BlockSpec((groups, d), lambda i: (i, 0))
    outs = pl.pallas_call(
        functools.partial(_post_kernel, groups=groups, prompt_tiles=prompt_tiles,
                          tiles_per_seq=seq // tm, tiles=tiles),
        grid=(tiles,),
        in_specs=_two_stream_specs(tm, d, prompt_tiles) + [
                  row(cdim),
                  pl.BlockSpec((GROUP_ROWS, cdim), lambda i: (jnp.maximum(i * groups - 1, 0), 0)),
                  pl.BlockSpec((groups, GROUP_ROWS, cdim),
                               lambda i: (jnp.maximum(i - prompt_tiles, 0), 0, 0))]
                 + _two_stream_specs(tm, ATTN_DIM, prompt_tiles) + [
                  row(2 * d), grp, grp, grp,
                  const((nch, CONV_WIDTH, CONV_CHUNK)), const((nch, 1, CONV_CHUNK)),
                  const((1, cdim)), const((1, cdim)),
                  const((cdim, d)), const((1, d)), const((ATTN_DIM, d)), const((d, d)),
                  const((1, d)), const((ROUTER_ROWS, d)), const((ROUTER_ROWS, 1))],
        out_specs=[row(d),
                   row(d // 2),
                   const((tiles, 2, tm)), const((tiles, 2, tm)), const((tiles, 2, tm)),
                   pl.BlockSpec((N_EXPERTS, LANES), lambda i: (0, 0))],
        out_shape=[jax.ShapeDtypeStruct((n, d), F32),
                   jax.ShapeDtypeStruct((n, d // 2), jnp.uint32),
                   jax.ShapeDtypeStruct((tiles, 2, tm), jnp.int32),
                   jax.ShapeDtypeStruct((tiles, 2, tm), F32),
                   jax.ShapeDtypeStruct((tiles, 2, tm), jnp.int32),
                   jax.ShapeDtypeStruct((N_EXPERTS, LANES), F32)],
        scratch_shapes=[pltpu.VMEM((groups * nch, 2 * GROUP_ROWS, CONV_CHUNK), F32),
                        pltpu.VMEM((SUBLANES, 2 * GROUP_ROWS - SUBLANES, CONV_CHUNK), F32),
                        pltpu.VMEM((groups * nch, GROUP_ROWS, CONV_CHUNK), F32),
                        pltpu.VMEM((tm, cdim), BF16),
                        pltpu.VMEM((tm, ATTN_DIM), BF16),
                        pltpu.VMEM((tm, d), F32),
                        pltpu.VMEM((tm, d), BF16),
                        pltpu.VMEM((N_EXPERTS, LANES), F32)],
        compiler_params=_cparams("arbitrary"),
        name="post",
    )(x_p, x_s, u, u, sconv_pad, o_p, o_s, gates, gate1_g, shift2_g, scale2_g,
      p["conv_w"], p["conv_b"], p["ln_g"], p["ln_b"], p["w_pw2"], p["b_pw2"], p["w_ao"], p["w_out"],
      p["n2"], p["w_rt"], p["b_rt"])
    x1, h2, eidx, wtok, rank, cnt = outs
    per_slot = lambda a: a.transpose(1, 0, 2).reshape(2, n)
    return x1, h2, per_slot(eidx), per_slot(wtok), per_slot(rank), cnt


TABLE_GROUP = 8
GATHER_AHEAD = 4
FINAL_AHEAD = 1


def _grouped_tables(tab, group=TABLE_GROUP, ahead=1):
    steps, w = tab.shape
    assert steps % group == 0
    padded = jnp.concatenate([tab] + [tab[-1:]] * ahead, axis=0)
    cols = [tab.reshape(steps // group, group * w)] + [padded[group + a::group][:steps // group]
                                                         for a in range(ahead)]
    return jnp.concatenate(cols, axis=1).reshape(steps // group, 1, (group + ahead) * w)

def _moe_kernel(blk_e_ref, next_e_ref, nused_ref, src_ref, roww_ref,
                h2_hbm, wg_hbm, wu_hbm, wd_hbm, y_ref,
                xbuf, x16, wg_f, wu_f, wd_f, wg_b, wu_b, wd_b, gsem, wsem):
    i = pl.program_id(0)
    nused = nused_ref[0]
    nslots = GATHER_AHEAD + 1
    slot = i % nslots
    base = (i % TABLE_GROUP) * MOE_BLOCK
    weights = ((wg_hbm, wg_f, wg_b), (wu_hbm, wu_f, wu_b), (wd_hbm, wd_f, wd_b))

    def gather(first, sl):
        for r in range(MOE_BLOCK):
            pltpu.make_async_copy(h2_hbm.at[pl.ds(src_ref[0, 0, first + r], 1), :],
                                  xbuf.at[sl, pl.ds(r, 1), :], gsem.at[sl]).start(priority=0)

    def gather_wait(sl):
        pltpu.make_async_copy(h2_hbm.at[pl.ds(0, MOE_BLOCK), :], xbuf.at[sl], gsem.at[sl]).wait()

    def weights_start(e):
        for hbm, f32_buf, _ in weights:
            pltpu.make_async_copy(hbm.at[e], f32_buf, wsem.at[0]).start(priority=1)

    def weights_wait():
        for hbm, f32_buf, _ in weights:
            pltpu.make_async_copy(hbm.at[0], f32_buf, wsem.at[0]).wait()

    @pl.when(i == 0)
    def _():
        weights_start(blk_e_ref[0])
        for a in range(GATHER_AHEAD):
            gather(base + a * MOE_BLOCK, a)

    @pl.when(i >= nused)
    def _():
        y_ref[...] = jnp.zeros(y_ref.shape, y_ref.dtype)

    @pl.when(i < nused)
    def _():
        changed = jnp.logical_or(i == 0, blk_e_ref[i] != blk_e_ref[jnp.maximum(i - 1, 0)])

        @pl.when(changed)
        def _():
            weights_wait()
            for _, f32_buf, b16_buf in weights:
                b16_buf[...] = f32_buf[...].astype(BF16)
            weights_start(next_e_ref[i])

        gather_wait(slot)
        x_lo, x_hi = _unpack_bf16_pairs(xbuf[slot])
        x16[:, :x_lo.shape[1]] = x_lo.astype(BF16)
        x16[:, x_lo.shape[1]:] = x_hi.astype(BF16)

        gather(base + GATHER_AHEAD * MOE_BLOCK, (i + GATHER_AHEAD) % nslots)
        x = x16[...]
        hg = jnp.dot(x, wg_b[...], preferred_element_type=F32)
        hu = jnp.dot(x, wu_b[...], preferred_element_type=F32)
        hid = (hg * jax.nn.sigmoid(hg) * hu).astype(BF16)
        y = jnp.dot(hid, wd_b[...], preferred_element_type=F32) * roww_ref[...]
        y_ref[...] = _pack_bf16_pairs(y.astype(BF16))

        @pl.when(i == nused - 1)
        def _():
            weights_wait()
            for a in range(1, GATHER_AHEAD + 1):
                gather_wait((i + a) % nslots)


def _moe(h2, blk_e, next_e, nused, src, roww, w_gate, w_up, w_down):
    n_blocks = blk_e.shape[0]
    n_exp, d, de = w_gate.shape
    src_tables = _grouped_tables(src.reshape(n_blocks, MOE_BLOCK), ahead=GATHER_AHEAD)
    grid_spec = pltpu.PrefetchScalarGridSpec(
        num_scalar_prefetch=3,
        grid=(n_blocks,),
        in_specs=[pl.BlockSpec((1, 1, src_tables.shape[2]), lambda i, be, ne, nu: (i // TABLE_GROUP, 0, 0),
                               memory_space=pltpu.SMEM),
                  pl.BlockSpec((MOE_BLOCK, 1), lambda i, be, ne, nu: (i, 0)),
                  pl.BlockSpec(memory_space=pl.ANY),
                  pl.BlockSpec(memory_space=pl.ANY),
                  pl.BlockSpec(memory_space=pl.ANY),
                  pl.BlockSpec(memory_space=pl.ANY)],
        out_specs=pl.BlockSpec((MOE_BLOCK, d // 2), lambda i, be, ne, nu: (i, 0)),
        scratch_shapes=[pltpu.VMEM((GATHER_AHEAD + 1, MOE_BLOCK, d // 2), jnp.uint32),
                        pltpu.VMEM((MOE_BLOCK, d), BF16),
                        pltpu.VMEM((d, de), F32), pltpu.VMEM((d, de), F32), pltpu.VMEM((de, d), F32),
                        pltpu.VMEM((d, de), BF16), pltpu.VMEM((d, de), BF16), pltpu.VMEM((de, d), BF16),
                        pltpu.SemaphoreType.DMA((GATHER_AHEAD + 1,)), pltpu.SemaphoreType.DMA((1,))],
    )
    return pl.pallas_call(
        _moe_kernel,
        grid_spec=grid_spec,
        out_shape=jax.ShapeDtypeStruct((n_blocks * MOE_BLOCK, d // 2), jnp.uint32),
        compiler_params=_cparams("arbitrary"),
        name="moe",
    )(blk_e, next_e, nused, src_tables, roww, h2, w_gate, w_up, w_down)


def _dispatch(eidx, wtok, rank, counts, n):
    a_tot = 2 * n
    experts = jnp.arange(N_EXPERTS, dtype=jnp.int32)
    padded = (counts + MOE_BLOCK - 1) // MOE_BLOCK * MOE_BLOCK
    pad_end = jnp.sum(jnp.where(experts[None, :] <= experts[:, None], padded[None, :], 0), axis=1)
    pad_start = pad_end - padded
    dest = jnp.sum(jnp.where(eidx[:, :, None] == experts, pad_start, 0), axis=-1) + rank
    n_blocks = -(-(a_tot + N_EXPERTS * (MOE_BLOCK - 1)) // MOE_BLOCK)
    n_blocks = -(-n_blocks // TABLE_GROUP) * TABLE_GROUP
    n_rows = n_blocks * MOE_BLOCK
    tok = jnp.tile(jnp.arange(n, dtype=jnp.int32), 2)
    upd = jnp.stack([tok, lax.bitcast_convert_type(wtok.reshape(-1), jnp.int32)], axis=1)
    rows = jnp.zeros((n_rows, 2), jnp.int32).at[dest.reshape(-1)].set(upd, unique_indices=True)
    blk_start = jnp.arange(n_blocks, dtype=jnp.int32) * MOE_BLOCK
    blk_e = jnp.minimum(jnp.sum((pad_end[None, :] <= blk_start[:, None]).astype(jnp.int32), axis=1),
                        N_EXPERTS - 1)
    nused = (pad_end[-1] // MOE_BLOCK).reshape(1)
    roww = lax.bitcast_convert_type(rows[:, 1], F32)
    later = jnp.where((counts > 0)[None, :] & (experts[None, :] > experts[:, None]), experts[None, :], N_EXPERTS)
    next_of = jnp.min(later, axis=1)
    next_of = jnp.where(next_of < N_EXPERTS, next_of, experts)
    next_e = jnp.sum(jnp.where(blk_e[:, None] == experts[None, :], next_of[None, :], 0), axis=1)
    return blk_e, next_e, nused, rows[:, 0].reshape(n_blocks, 1, MOE_BLOCK), roww.reshape(n_rows, 1), dest


def _final_kernel(d0_ref, d1_ref, x1_ref, gate2_ref, fg_ref, yb_hbm,
                  op_ref, os_ref, ybuf, sem, *, groups, prompt_tiles, tiles, table_group):
    i = pl.program_id(0)
    nslots = FINAL_AHEAD + 1
    slot = i % nslots
    tm = x1_ref.shape[0]
    base = (i % table_group) * tm

    def gather(first, sl):
        for r in range(tm):
            pltpu.make_async_copy(yb_hbm.at[pl.ds(d0_ref[0, 0, first + r], 1), :],
                                  ybuf.at[sl, 0, pl.ds(r, 1), :], sem.at[sl]).start(priority=0)
            pltpu.make_async_copy(yb_hbm.at[pl.ds(d1_ref[0, 0, first + r], 1), :],
                                  ybuf.at[sl, 1, pl.ds(r, 1), :], sem.at[sl]).start(priority=1)

    def gather_wait(sl):
        for k in range(2):
            pltpu.make_async_copy(yb_hbm.at[pl.ds(0, tm), :], ybuf.at[sl, k], sem.at[sl]).wait()

    @pl.when(i == 0)
    def _():
        for a in range(FINAL_AHEAD):
            gather(base + a * tm, a)

    gather_wait(slot)
    gather(base + FINAL_AHEAD * tm, (i + FINAL_AHEAD) % nslots)

    def run(o_ref):
        def body(g, carry):
            r = pl.multiple_of(g * GROUP_ROWS, GROUP_ROWS)
            lo0, hi0 = _unpack_bf16_pairs(ybuf[slot, 0, pl.ds(r, GROUP_ROWS), :])
            lo1, hi1 = _unpack_bf16_pairs(ybuf[slot, 1, pl.ds(r, GROUP_ROWS), :])
            moe = jnp.concatenate([lo0 + lo1, hi0 + hi1], axis=1)
            x2 = x1_ref[pl.ds(r, GROUP_ROWS), :] + gate2_ref[pl.ds(g, 1), :] * moe
            ms = jnp.mean(x2 * x2, axis=-1, keepdims=True)
            o_ref[pl.ds(r, GROUP_ROWS), :] = x2 * lax.rsqrt(ms + RMS_EPS) * fg_ref[...]
            return carry

        lax.fori_loop(0, groups, body, 0)

    pl.when(i < prompt_tiles)(lambda: run(op_ref))
    pl.when(i >= prompt_tiles)(lambda: run(os_ref))

    @pl.when(i == tiles - 1)
    def _():
        for a in range(1, FINAL_AHEAD + 1):
            gather_wait((i + a) % nslots)


def _final(x1, yb, dest, gate2_g, final_g, n_prompt, tm):
    n, d = x1.shape
    groups = tm // GROUP_ROWS
    tiles = n // tm
    prompt_tiles = n_prompt // tm
    table_group = max(g for g in range(1, TABLE_GROUP + 1) if tiles % g == 0)
    tables = [_grouped_tables(dest[k].reshape(tiles, tm), table_group, FINAL_AHEAD) for k in range(2)]
    idx_spec = pl.BlockSpec((1, 1, tables[0].shape[2]), lambda i: (i // table_group, 0, 0),
                            memory_space=pltpu.SMEM)
    return pl.pallas_call(
        functools.partial(_final_kernel, groups=groups, prompt_tiles=prompt_tiles, tiles=tiles,
                          table_group=table_group),
        grid=(tiles,),
        in_specs=[idx_spec, idx_spec,
                  pl.BlockSpec((tm, d), lambda i: (i, 0)),
                  pl.BlockSpec((groups, d), lambda i: (i, 0)),
                  pl.BlockSpec((1, d), lambda i: (0, 0)),
                  pl.BlockSpec(memory_space=pl.ANY)],
        out_specs=_two_stream_specs(tm, d, prompt_tiles),
        out_shape=[jax.ShapeDtypeStruct((n_prompt, d), F32),
                   jax.ShapeDtypeStruct((n - n_prompt, d), F32)],
        scratch_shapes=[pltpu.VMEM((FINAL_AHEAD + 1, 2, tm, d // 2), jnp.uint32),
                        pltpu.SemaphoreType.DMA((FINAL_AHEAD + 1,))],
        compiler_params=_cparams("arbitrary"),
        name="final",
    )(tables[0], tables[1], x1, gate2_g, final_g.reshape(1, d), yb)


def _layer(x_p, x_s, c_all, cache_k, cache_v, sconv, rel_table, lp, seq, t_new):
    (n1, n2, w_ada, b_ada, w_in, sink, w_ao, conv_w, conv_b, ln_g, ln_b, w_pw2, b_pw2, w_out,
     w_grp, b_grp, w_rt, b_rt, w_gate, w_up, w_down) = lp
    n_prompt, d = x_p.shape
    n = n_prompt + x_s.shape[0]
    batch = n_prompt // seq
    dec_batch = (n - n_prompt) // t_new
    cdim = conv_w.shape[1]
    tm = TOKEN_TILE

    c_rows = -(-c_all.shape[0] // 8) * 8
    c_pad = jnp.pad(c_all, ((0, c_rows - c_all.shape[0]), (0, 0)))
    mods = _ada(c_pad, w_ada, b_ada)

    def per_group(k):
        m = mods[:, k * d:(k + 1) * d]
        mp = jnp.broadcast_to(m[:batch, None, :], (batch, seq // GROUP_ROWS, d))
        return jnp.concatenate([mp.reshape(batch * (seq // GROUP_ROWS), d), m[batch:batch + dec_batch]],
                               axis=0)

    shift1, scale1, gate1, shift2, scale2, gate2 = [per_group(k) for k in range(6)]

    h1 = _norm1(x_p, x_s, n1, scale1, shift1, NORM_ROWS)

    k_off = ATTN_DIM
    glu_off = k_off + 2 * KV_DIM
    gate_off = glu_off + 2 * cdim
    w_in_b = w_in.astype(BF16)
    tglu = PROJ_COLS // 2
    w_glu = jnp.stack([w_in_b[:, glu_off:glu_off + cdim].reshape(d, cdim // tglu, tglu),
                       w_in_b[:, glu_off + cdim:gate_off].reshape(d, cdim // tglu, tglu)],
                      axis=2).reshape(d, 2 * cdim)
    tmm = PROJ_ROWS
    (q,) = _proj(h1, w_in_b[:, :k_off], "qkv", tmm, PROJ_COLS, [BF16])
    (kv,) = _proj(h1, w_in_b[:, k_off:glu_off], "qkv", tmm, 2 * KV_DIM, [BF16])
    kv_state = _kv_state(h1, w_in_b[:, k_off:glu_off], n_prompt, seq)
    (u,) = _proj(h1, w_glu, "glu", tmm, 2 * tglu, [F32])
    (gates,) = _proj(h1, w_in_b[:, gate_off:], "gate", tmm, PROJ_COLS, [BF16])

    bias_p = _pair_bias(rel_table)
    buf = cache_k.shape[1]
    kpos = jnp.concatenate([jnp.arange(buf, dtype=jnp.int32) - buf, jnp.arange(t_new, dtype=jnp.int32)])
    bias_s = _rel_bias(rel_table, kpos[None, :] - jnp.arange(t_new, dtype=jnp.int32)[:, None])
    sink_f = sink.astype(F32).reshape(N_KV_HEADS, Q_PER_KV, 1, 1)
    sink_kch = sink.astype(F32).reshape(N_KV_HEADS, 1, COLS_PER_KV, HEADS_PER_COL).transpose(0, 3, 1, 2)
    sink_p = jnp.broadcast_to(sink_kch[..., None, None],
                              (N_KV_HEADS, HEADS_PER_COL, 2, COLS_PER_KV, CHUNK, LANES)).reshape(
        N_KV_HEADS, HEADS_PER_COL, 2 * COLS_PER_KV * CHUNK, LANES)
    sink_s = jnp.broadcast_to(sink_f, (N_KV_HEADS, Q_PER_KV, t_new, 1)).reshape(N_KV_HEADS, Q_PER_KV * t_new, 1)
    o_p = _attn_prompt(q, kv, bias_p, sink_p, n_prompt, seq, ATTN_PAIRS)
    o_s = _attn_sample(q, kv, cache_k.reshape(dec_batch, buf, KV_DIM), cache_v.reshape(dec_batch, buf, KV_DIM),
                       bias_s, sink_s, n_prompt, t_new)

    hist = GROUP_ROWS
    sconv_pad = jnp.pad(sconv, ((0, 0), (hist - sconv.shape[1], 0), (0, 0)))
    w_rt_t = jnp.zeros((ROUTER_ROWS, d), F32)
    w_rt_t = w_rt_t.at[:N_GROUPS].set(w_grp.T).at[EXPERT_ROW0:EXPERT_ROW0 + N_EXPERTS].set(w_rt.T)
    b_rt_t = jnp.zeros((ROUTER_ROWS, 1), F32)
    b_rt_t = b_rt_t.at[:N_GROUPS, 0].set(b_grp.astype(F32)).at[EXPERT_ROW0:EXPERT_ROW0 + N_EXPERTS, 0].set(
        b_rt.astype(F32))
    nch = cdim // CONV_CHUNK
    conv_w_c = conv_w.reshape(CONV_WIDTH, nch, CONV_CHUNK).transpose(1, 0, 2)
    params = dict(conv_w=conv_w_c, conv_b=conv_b.reshape(nch, 1, CONV_CHUNK), ln_g=ln_g.reshape(1, cdim),
                  ln_b=ln_b.reshape(1, cdim), w_pw2=w_pw2.astype(BF16), b_pw2=b_pw2.reshape(1, d),
                  w_ao=w_ao.astype(BF16), w_out=w_out.astype(BF16), n2=n2.reshape(1, d),
                  w_rt=w_rt_t.astype(BF16), b_rt=b_rt_t)
    x1, h2, eidx, wtok, rank, cnt = _post(x_p, x_s, u, sconv_pad, o_p, o_s, gates, gate1, shift2, scale2,
                                             params, n_prompt, seq, tm)

    blk_e, next_e, nused, src, roww, dest = _dispatch(eidx, wtok, rank, cnt[:, 0].astype(jnp.int32), n)
    yb = _moe(h2, blk_e, next_e, nused, src, roww, w_gate, w_up, w_down)
    return x1, yb, dest, gate2, kv_state, u


def kernel(x_prompt, x_sample, c_prompt, c_sample, cache_k, cache_v, state_conv, rel_bias_table, norm1_g, norm2_g, w_ada, b_ada, w_in, attn_sink, w_attn_o, conv_w, conv_b, conv_ln_g, conv_ln_b, w_pw2, b_pw2, w_out, w_group, b_group, w_router, b_router, w_gate, w_up, w_down, final_g):
    batch, seq, d = x_prompt.shape
    dec_batch, t_new, _ = x_sample.shape
    depth = norm1_g.shape[0]
    assert depth == 1, "single trunk layer"
    assert t_new == GROUP_ROWS and seq % GROUP_ROWS == 0
    n_prompt = batch * seq
    n = n_prompt + dec_batch * t_new
    x_p = x_prompt.reshape(n_prompt, d)
    x_s = x_sample.reshape(dec_batch * t_new, d)
    c_all = jnp.concatenate([c_prompt, c_sample], axis=0)
    l = 0
    lp = tuple(a.reshape(a.shape[1:]) for a in (
        norm1_g, norm2_g, w_ada, b_ada, w_in, attn_sink, w_attn_o, conv_w, conv_b, conv_ln_g, conv_ln_b,
        w_pw2, b_pw2, w_out, w_group, b_group, w_router, b_router, w_gate, w_up, w_down))
    x1, yb, dest, gate2, kv_state, u = _layer(x_p, x_s, c_all, cache_k[l], cache_v[l], state_conv[l],
                                              rel_bias_table, lp, seq, t_new)
    out_p, out_s = _final(x1, yb, dest, gate2, final_g, n_prompt, TOKEN_TILE)

    y_prompt = out_p.reshape(batch, seq, d)
    y_sample = out_s.reshape(dec_batch, t_new, d)
    cdim = u.shape[1]
    kvp = kv_state[:batch * WINDOW].reshape(batch, WINDOW, 2 * KV_DIM)
    new_k_prompt = kvp[..., :KV_DIM].reshape(1, batch, WINDOW, N_KV_HEADS, HEAD_DIM)
    new_v_prompt = kvp[..., KV_DIM:].reshape(1, batch, WINDOW, N_KV_HEADS, HEAD_DIM)
    new_conv_prompt = jnp.stack([u[(b + 1) * seq - (CONV_WIDTH - 1):(b + 1) * seq] for b in range(batch)])[None]
    kvs = kv_state[batch * WINDOW:].reshape(dec_batch, t_new, 2 * KV_DIM)
    buf = cache_k.shape[2]
    k_new = kvs[..., :KV_DIM].reshape(dec_batch, t_new, N_KV_HEADS, HEAD_DIM)
    v_new = kvs[..., KV_DIM:].reshape(dec_batch, t_new, N_KV_HEADS, HEAD_DIM)
    new_k_sample = jnp.concatenate([cache_k[l], k_new], axis=1)[:, -buf:][None]
    new_v_sample = jnp.concatenate([cache_v[l], v_new], axis=1)[:, -buf:][None]
    us = u[n_prompt:].reshape(dec_batch, t_new, cdim)
    new_conv_sample = jnp.concatenate([state_conv[l], us], axis=1)[:, -(CONV_WIDTH - 1):][None]
    return (y_prompt, y_sample, new_k_prompt, new_v_prompt, new_conv_prompt,
            new_k_sample, new_v_sample, new_conv_sample)
```

```python
import functools
import math

import jax
import jax.numpy as jnp
from jax import lax
from jax.experimental import pallas as pl
from jax.experimental.pallas import tpu as pltpu

F32 = jnp.float32
BF16 = jnp.bfloat16

CHUNK = 64
HEAD_DIM = 64
N_Q_HEADS = 16
N_KV_HEADS = 4
Q_PER_KV = N_Q_HEADS // N_KV_HEADS
ATTN_DIM = N_Q_HEADS * HEAD_DIM
KV_DIM = N_KV_HEADS * HEAD_DIM
WINDOW = 128
WIN_CHUNKS = WINDOW // CHUNK
CONV_WIDTH = 31
NUM_BUCKETS = 32
MAX_DISTANCE = 128
N_GROUPS = 4
EXPERTS_PER_GROUP = 8
N_EXPERTS = N_GROUPS * EXPERTS_PER_GROUP
MOE_BLOCK = 128
RMS_EPS = 1e-6
LN_EPS = 1e-5
NEG_INF = -1e30

GROUP_ROWS = 32
LANES = 128
SUBLANES = 8
CONV_CHUNK = 256
ROUTER_ROWS = 64
EXPERT_ROW0 = 8

NORM_ROWS = 512
PROJ_ROWS = 1536
PROJ_COLS = 1024
TOKEN_TILE = 256
ATTN_PAIRS = 4
V7X_VMEM_LIMIT = 56 * 1024 * 1024


def _cparams(*sem):
    return pltpu.CompilerParams(dimension_semantics=sem, vmem_limit_bytes=V7X_VMEM_LIMIT)


def _ada_kernel(c_ref, w_ref, b_ref, o_ref):
    c = c_ref[...]
    s = c * jax.nn.sigmoid(c)
    o_ref[...] = jnp.dot(s.astype(BF16), w_ref[...].astype(BF16),
                         preferred_element_type=F32) + b_ref[...]


def _ada(c_all, w_ada, b_ada):
    rows, d = c_all.shape
    ncol = w_ada.shape[1]
    tn = 1024
    return pl.pallas_call(
        _ada_kernel,
        grid=(ncol // tn,),
        in_specs=[pl.BlockSpec((rows, d), lambda j: (0, 0)),
                  pl.BlockSpec((d, tn), lambda j: (0, j)),
                  pl.BlockSpec((1, tn), lambda j: (0, j))],
        out_specs=pl.BlockSpec((rows, tn), lambda j: (0, j)),
        out_shape=jax.ShapeDtypeStruct((rows, ncol), F32),
        compiler_params=_cparams("arbitrary"),
        name="ada",
    )(c_all, w_ada, b_ada.reshape(1, ncol))


def _rms_mod(x, gain, scale, shift):
    ms = jnp.mean(x * x, axis=-1, keepdims=True)
    y = x * lax.rsqrt(ms + RMS_EPS) * gain
    return y * (1.0 + scale) + shift


def _pack_bf16_pairs(xb):
    w = xb.shape[1] // 2
    bits = lax.bitcast_convert_type(xb.astype(F32), jnp.uint32)
    return (bits[:, :w] >> 16) | (bits[:, w:] & jnp.uint32(0xFFFF0000))


def _unpack_bf16_pairs(p):
    lo = lax.bitcast_convert_type(p << 16, F32)
    hi = lax.bitcast_convert_type(p & jnp.uint32(0xFFFF0000), F32)
    return lo, hi


def _two_stream_specs(tm, width, prompt_tiles):
    return [pl.BlockSpec((tm, width), lambda i: (jnp.minimum(i, prompt_tiles - 1), 0)),
            pl.BlockSpec((tm, width), lambda i: (jnp.maximum(i - prompt_tiles, 0), 0))]


def _norm1_kernel(xp_ref, xs_ref, g_ref, sc_ref, sh_ref, h_ref, *, groups, prompt_tiles):
    def run(x_ref):
        def body(gi, carry):
            r = pl.multiple_of(gi * GROUP_ROWS, GROUP_ROWS)
            h = _rms_mod(x_ref[pl.ds(r, GROUP_ROWS), :], g_ref[...],
                         sc_ref[pl.ds(gi, 1), :], sh_ref[pl.ds(gi, 1), :])
            h_ref[pl.ds(r, GROUP_ROWS), :] = h.astype(h_ref.dtype)
            return carry

        lax.fori_loop(0, groups, body, 0)

    i = pl.program_id(0)
    pl.when(i < prompt_tiles)(lambda: run(xp_ref))
    pl.when(i >= prompt_tiles)(lambda: run(xs_ref))


def _norm1(x_p, x_s, gain, scale_g, shift_g, tm):
    d = x_p.shape[1]
    n = x_p.shape[0] + x_s.shape[0]
    groups = tm // GROUP_ROWS
    prompt_tiles = x_p.shape[0] // tm
    return pl.pallas_call(
        functools.partial(_norm1_kernel, groups=groups, prompt_tiles=prompt_tiles),
        grid=(n // tm,),
        in_specs=_two_stream_specs(tm, d, prompt_tiles) + [
            pl.BlockSpec((1, d), lambda i: (0, 0)),
            pl.BlockSpec((groups, d), lambda i: (i, 0)),
            pl.BlockSpec((groups, d), lambda i: (i, 0))],
        out_specs=pl.BlockSpec((tm, d), lambda i: (i, 0)),
        out_shape=jax.ShapeDtypeStruct((n, d), BF16),
        compiler_params=_cparams("arbitrary"),
        name="norm1",
    )(x_p, x_s, gain.reshape(1, d), scale_g, shift_g)


def _proj_kernel(h_ref, w_ref, *o_refs, mode):
    acc = jnp.dot(h_ref[...], w_ref[...], preferred_element_type=F32)
    if mode == "qkv":
        o_refs[0][...] = acc.astype(BF16)
    elif mode == "glu":
        half = acc.shape[1] // 2
        o_refs[0][...] = acc[:, :half] * jax.nn.sigmoid(acc[:, half:])
    else:
        o_refs[0][...] = jax.nn.sigmoid(acc).astype(BF16)


def _proj(h, w, mode, tm, tnw, out_dtypes):
    n, d = h.shape
    ncol = w.shape[1]
    tno = tnw // 2 if mode == "glu" else tnw
    nout = ncol // 2 if mode == "glu" else ncol
    outs = pl.pallas_call(
        functools.partial(_proj_kernel, mode=mode),
        grid=(n // tm, ncol // tnw),
        in_specs=[pl.BlockSpec((tm, d), lambda i, j: (i, 0)),
                  pl.BlockSpec((d, tnw), lambda i, j: (0, j))],
        out_specs=[pl.BlockSpec((tm, tno), lambda i, j: (i, j)) for _ in out_dtypes],
        out_shape=[jax.ShapeDtypeStruct((n, nout), dt) for dt in out_dtypes],
        compiler_params=_cparams("arbitrary", "arbitrary"),
        name="proj_" + mode,
    )(h, w)
    return outs


def _kv_state_kernel(h_ref, w_ref, o_ref):
    o_ref[...] = jnp.dot(h_ref[...], w_ref[...], preferred_element_type=F32)


def _kv_state(h, w_kv, n_prompt, seq):
    n, d = h.shape
    batch = n_prompt // seq
    per_seq = seq // WINDOW
    steps = batch + (n - n_prompt) // WINDOW

    def row_block(i):
        return jnp.where(i < batch, (i + 1) * per_seq - 1, n_prompt // WINDOW + i - batch)

    return pl.pallas_call(
        _kv_state_kernel,
        grid=(steps,),
        in_specs=[pl.BlockSpec((WINDOW, d), lambda i: (row_block(i), 0)),
                  pl.BlockSpec((d, 2 * KV_DIM), lambda i: (0, 0))],
        out_specs=pl.BlockSpec((WINDOW, 2 * KV_DIM), lambda i: (i, 0)),
        out_shape=jax.ShapeDtypeStruct((steps * WINDOW, 2 * KV_DIM), F32),
        compiler_params=_cparams("arbitrary"),
        name="kv_state",
    )(h, w_kv)


def _t5_bucket(rel):
    nb = NUM_BUCKETS // 2
    n = -rel
    ret = jnp.where(n < 0, nb, 0)
    n = jnp.abs(n)
    max_exact = nb // 2
    nf = jnp.maximum(n, 1).astype(F32)
    large = max_exact + (jnp.log(nf / max_exact) / math.log(MAX_DISTANCE / max_exact)
                         * (nb - max_exact)).astype(jnp.int32)
    large = jnp.minimum(large, nb - 1)
    return ret + jnp.where(n < max_exact, n, large)


def _bias_kernel(tbl_ref, bkt_ref, o_ref):
    bkt = bkt_ref[...]
    nq = bkt.shape[0]
    for k in range(N_KV_HEADS):
        for g in range(Q_PER_KV):
            acc = jnp.zeros(bkt.shape, F32)
            for b in range(NUM_BUCKETS):
                acc = jnp.where(bkt == b, tbl_ref[b, k * Q_PER_KV + g], acc)
            o_ref[k, g * nq:(g + 1) * nq, :] = acc


def _rel_bias(rel_table, rel):
    nq, nk = rel.shape
    return pl.pallas_call(
        _bias_kernel,
        in_specs=[pl.BlockSpec(memory_space=pltpu.SMEM),
                  pl.BlockSpec((nq, nk), lambda: (0, 0))],
        out_specs=pl.BlockSpec((N_KV_HEADS, Q_PER_KV * nq, nk), lambda: (0, 0, 0)),
        out_shape=jax.ShapeDtypeStruct((N_KV_HEADS, Q_PER_KV * nq, nk), F32),
        name="rel_bias",
    )(rel_table.astype(F32), _t5_bucket(rel))


PAIR_BAND = 2 * CHUNK + WINDOW
HEADS_PER_COL = LANES // HEAD_DIM
COLS_PER_KV = Q_PER_KV // HEADS_PER_COL


def _pair_bias_kernel(tbl_ref, bkt_ref, o_ref):
    for e in range(2):
        bkt = bkt_ref[e]
        for k in range(N_KV_HEADS):
            for col in range(COLS_PER_KV):
                for half in range(HEADS_PER_COL):
                    head = k * Q_PER_KV + col * HEADS_PER_COL + half
                    acc = jnp.full(bkt.shape, NEG_INF, F32)
                    for b in range(NUM_BUCKETS):
                        acc = jnp.where(bkt == b, tbl_ref[b, head], acc)
                    r0 = (e * COLS_PER_KV + col) * CHUNK
                    o_ref[k, r0:r0 + CHUNK, half * PAIR_BAND:(half + 1) * PAIR_BAND] = acc


def _pair_bias(rel_table):
    qi = jnp.arange(CHUNK, dtype=jnp.int32)[None, :, None]
    kj = jnp.arange(PAIR_BAND, dtype=jnp.int32)[None, None, :]
    e = jnp.arange(2, dtype=jnp.int32)[:, None, None]
    rel = (kj - WINDOW) - (e * CHUNK + qi)
    key_chunk = kj // CHUNK - e
    seen = (key_chunk >= 0) & (key_chunk <= WIN_CHUNKS)
    bkt = jnp.where(seen, _t5_bucket(rel), -1)
    rows = 2 * COLS_PER_KV * CHUNK
    return pl.pallas_call(
        _pair_bias_kernel,
        in_specs=[pl.BlockSpec(memory_space=pltpu.SMEM),
                  pl.BlockSpec((2, CHUNK, PAIR_BAND), lambda: (0, 0, 0))],
        out_specs=pl.BlockSpec((N_KV_HEADS, rows, HEADS_PER_COL * PAIR_BAND), lambda: (0, 0, 0)),
        out_shape=jax.ShapeDtypeStruct((N_KV_HEADS, rows, HEADS_PER_COL * PAIR_BAND), F32),
        name="pair_bias",
    )(rel_table.astype(F32), bkt)


def _attend(qc, kk_all, vv_all, bias_ref, sink_ref, mask_thr):
    nq = qc.shape[0]
    nk = kk_all.shape[0]
    pieces = []
    for k in range(N_KV_HEADS):
        qs = jnp.concatenate(
            [qc[:, (k * Q_PER_KV + g) * HEAD_DIM:(k * Q_PER_KV + g + 1) * HEAD_DIM]
             for g in range(Q_PER_KV)], axis=0)
        kk = kk_all[:, k * HEAD_DIM:(k + 1) * HEAD_DIM]
        vv = vv_all[:, k * HEAD_DIM:(k + 1) * HEAD_DIM]
        lg = lax.dot_general(qs, kk, (((1,), (1,)), ((), ())),
                             preferred_element_type=F32) * (HEAD_DIM ** -0.5) + bias_ref[k]
        if mask_thr is not None:
            col = lax.broadcasted_iota(jnp.int32, (Q_PER_KV * nq, nk), 1)
            lg = jnp.where(col < mask_thr, NEG_INF, lg)
        s = sink_ref[k]
        m = jnp.maximum(jnp.max(lg, axis=-1, keepdims=True), s)
        p = jnp.exp(lg - m)
        den = jnp.sum(p, axis=-1, keepdims=True) + jnp.exp(s - m)
        o = jnp.dot(p.astype(BF16), vv, preferred_element_type=F32) / den
        pieces.extend(o[g * nq:(g + 1) * nq, :] for g in range(Q_PER_KV))
    return jnp.concatenate(pieces, axis=1)


def _attn_prompt_kernel(q_ref, kv_ref, halo_ref, bias_ref, sink_ref, o_ref, ka, kb, va, vb, *, pairs):
    t = pl.program_id(1)
    kvcat = jnp.concatenate([halo_ref[...], kv_ref[...]], axis=0)
    low = lax.broadcasted_iota(jnp.int32, (kvcat.shape[0], LANES), 1) < HEAD_DIM
    zero = jnp.zeros((kvcat.shape[0], LANES), BF16)
    ones_low = jnp.where(low, 1.0, 0.0).astype(BF16)
    ones_high = jnp.where(low, 0.0, 1.0).astype(BF16)
    kv_cols = KV_DIM // LANES
    for col in range(2 * kv_cols):
        x = kvcat[:, col * LANES:(col + 1) * LANES]
        xs = jnp.concatenate([x[:, HEAD_DIM:], x[:, :HEAD_DIM]], axis=1)
        dst_a, dst_b = (ka, kb) if col < kv_cols else (va, vb)
        k0 = (col % kv_cols) * HEADS_PER_COL
        dst_a[k0, :, 0:LANES] = jnp.where(low, x, zero)
        dst_b[k0, :, 0:LANES] = jnp.where(low, zero, xs)
        dst_a[k0 + 1, :, 0:LANES] = jnp.where(low, xs, zero)
        dst_b[k0 + 1, :, 0:LANES] = jnp.where(low, zero, x)
    for k in range(N_KV_HEADS):
        va[k, :, LANES:2 * LANES] = ones_low
        vb[k, :, LANES:2 * LANES] = ones_high

    out_low = lax.broadcasted_iota(jnp.int32, (2 * COLS_PER_KV * CHUNK, LANES), 1) < HEAD_DIM
    for p in range(pairs):
        r0 = p * 2 * CHUNK
        for k in range(N_KV_HEADS):
            lhs = jnp.concatenate(
                [q_ref[r0 + e * CHUNK:r0 + (e + 1) * CHUNK, (k * COLS_PER_KV + c) * LANES:(k * COLS_PER_KV + c + 1) * LANES]
                 for e in range(2) for c in range(COLS_PER_KV)], axis=0)
            keys = jnp.concatenate([ka[k, r0:r0 + PAIR_BAND, :], kb[k, r0:r0 + PAIR_BAND, :]], axis=0)
            lg = lax.dot_general(lhs, keys, (((1,), (1,)), ((), ())),
                                 preferred_element_type=F32) * (HEAD_DIM ** -0.5) + bias_ref[k]
            probs, sink_terms = [], []
            for half in range(HEADS_PER_COL):
                seg = lg[:, half * PAIR_BAND:(half + 1) * PAIR_BAND]
                if p == 0:
                    kcol = lax.broadcasted_iota(jnp.int32, seg.shape, 1)
                    seg = jnp.where(kcol < jnp.where(t == 0, WINDOW, 0), NEG_INF, seg)
                s = sink_ref[k, half]
                folded = seg[:, :LANES]
                for j in range(1, PAIR_BAND // LANES):
                    folded = jnp.maximum(folded, seg[:, j * LANES:(j + 1) * LANES])
                m = jnp.maximum(jnp.max(folded, axis=-1, keepdims=True), s)
                pr = jnp.exp(seg - jnp.concatenate([m] * (PAIR_BAND // LANES), axis=1))
                sink_terms.append(jnp.exp(s - m))
                probs.append(pr.astype(BF16))
            vals = jnp.concatenate([va[k, r0:r0 + PAIR_BAND, :], vb[k, r0:r0 + PAIR_BAND, :]], axis=0)
            oe = jnp.dot(jnp.concatenate(probs, axis=1), vals, preferred_element_type=F32)
            o = oe[:, :LANES] / (oe[:, LANES:] + jnp.where(out_low, sink_terms[0], sink_terms[1]))
            for e in range(2):
                for c in range(COLS_PER_KV):
                    rr = (e * COLS_PER_KV + c) * CHUNK
                    o_ref[r0 + e * CHUNK:r0 + (e + 1) * CHUNK,
                          (k * COLS_PER_KV + c) * LANES:(k * COLS_PER_KV + c + 1) * LANES] = (
                        o[rr:rr + CHUNK, :].astype(o_ref.dtype))


def _attn_prompt(q, kv, bias, sink_rows, n_prompt, seq, pairs):
    rows = pairs * 2 * CHUNK
    tiles = seq // rows
    halo_per_tile = rows // WINDOW
    keys = WINDOW + rows
    return pl.pallas_call(
        functools.partial(_attn_prompt_kernel, pairs=pairs),
        grid=(n_prompt // seq, tiles),
        scratch_shapes=[pltpu.VMEM((N_KV_HEADS, keys, LANES), BF16) for _ in range(2)]
        + [pltpu.VMEM((N_KV_HEADS, keys, 2 * LANES), BF16) for _ in range(2)],
        in_specs=[pl.BlockSpec((rows, ATTN_DIM), lambda b, t: (b * tiles + t, 0)),
                  pl.BlockSpec((rows, 2 * KV_DIM), lambda b, t: (b * tiles + t, 0)),
                  pl.BlockSpec((WINDOW, 2 * KV_DIM),
                               lambda b, t: (jnp.maximum((b * tiles + t) * halo_per_tile - 1, 0), 0)),
                  pl.BlockSpec(bias.shape, lambda b, t: (0, 0, 0)),
                  pl.BlockSpec(sink_rows.shape, lambda b, t: (0, 0, 0, 0))],
        out_specs=pl.BlockSpec((rows, ATTN_DIM), lambda b, t: (b * tiles + t, 0)),
        out_shape=jax.ShapeDtypeStruct((n_prompt, ATTN_DIM), BF16),
        compiler_params=_cparams("arbitrary", "arbitrary"),
        name="attn_prompt",
    )(q, kv, kv, bias, sink_rows)


def _attn_sample_kernel(q_ref, kv_ref, ck_ref, cv_ref, bias_ref, sink_ref, o_ref):
    kv = kv_ref[...]
    kk = jnp.concatenate([ck_ref[0].astype(BF16), kv[:, :KV_DIM]], axis=0)
    vv = jnp.concatenate([cv_ref[0].astype(BF16), kv[:, KV_DIM:]], axis=0)
    o = _attend(q_ref[...], kk, vv, bias_ref, sink_ref, None)
    o_ref[...] = o.astype(o_ref.dtype)


def _attn_sample(q, kv, cache_k, cache_v, bias, sink_rows, n_prompt, t_new):
    dec_batch, buf, _ = cache_k.shape
    first = n_prompt // t_new
    return pl.pallas_call(
        _attn_sample_kernel,
        grid=(dec_batch,),
        in_specs=[pl.BlockSpec((t_new, ATTN_DIM), lambda s: (first + s, 0)),
                  pl.BlockSpec((t_new, 2 * KV_DIM), lambda s: (first + s, 0)),
                  pl.BlockSpec((1, buf, KV_DIM), lambda s: (s, 0, 0)),
                  pl.BlockSpec((1, buf, KV_DIM), lambda s: (s, 0, 0)),
                  pl.BlockSpec(bias.shape, lambda s: (0, 0, 0)),
                  pl.BlockSpec(sink_rows.shape, lambda s: (0, 0, 0))],
        out_specs=pl.BlockSpec((t_new, ATTN_DIM), lambda s: (s, 0)),
        out_shape=jax.ShapeDtypeStruct((dec_batch * t_new, ATTN_DIM), BF16),
        compiler_params=_cparams("arbitrary"),
        name="attn_sample",
    )(q, kv, cache_k, cache_v, bias, sink_rows)


def _route(lt, t, counted, eidx_ref, wtok_ref, rank_ref, cnt_ref, cnt_scr):
    gl = [lt[r:r + 1, :] for r in range(N_GROUPS)]
    gmax = gl[0]
    gsel = jnp.zeros(gl[0].shape, jnp.int32)
    for r in range(1, N_GROUPS):
        better = gl[r] > gmax
        gsel = jnp.where(better, r, gsel)
        gmax = jnp.maximum(gmax, gl[r])
    gexp = [jnp.exp(v - gmax) for v in gl]
    gsum = gexp[0]
    for r in range(1, N_GROUPS):
        gsum = gsum + gexp[r]
    psel = jnp.zeros(gl[0].shape, F32)
    for r in range(N_GROUPS):
        psel = jnp.where(gsel == r, gexp[r] / gsum, psel)
    el = jnp.zeros((EXPERTS_PER_GROUP, lt.shape[1]), F32)
    for r in range(N_GROUPS):
        lo = EXPERT_ROW0 + r * EXPERTS_PER_GROUP
        el = jnp.where(gsel == r, lt[lo:lo + EXPERTS_PER_GROUP, :], el)
    emax = jnp.max(el, axis=0, keepdims=True)
    ee = jnp.exp(el - emax)
    pin = ee / jnp.sum(ee, axis=0, keepdims=True)
    idx = lax.broadcasted_iota(jnp.int32, pin.shape, 0)
    p1 = jnp.max(pin, axis=0, keepdims=True)
    i1 = jnp.min(jnp.where(pin == p1, idx, EXPERTS_PER_GROUP), axis=0, keepdims=True)
    rest = jnp.where(idx == i1, -1.0, pin)
    p2 = jnp.max(rest, axis=0, keepdims=True)
    i2 = jnp.min(jnp.where(rest == p2, idx, EXPERTS_PER_GROUP), axis=0, keepdims=True)
    tot = p1 + p2
    e1 = gsel * EXPERTS_PER_GROUP + i1
    e2 = gsel * EXPERTS_PER_GROUP + i2
    eidx_ref[t, 0:1, :] = e1
    eidx_ref[t, 1:2, :] = e2
    wtok_ref[t, 0:1, :] = psel * p1 / tot
    wtok_ref[t, 1:2, :] = psel * p2 / tot

    rows = lt.shape[1]
    eiota = lax.broadcasted_iota(jnp.int32, (N_EXPERTS, rows), 0)
    oh1 = (eiota == e1).astype(F32)
    oh2 = (eiota == e2).astype(F32)
    both = oh1 + oh2
    before = (lax.broadcasted_iota(jnp.int32, (rows, rows), 0)
              < lax.broadcasted_iota(jnp.int32, (rows, rows), 1)).astype(BF16)
    prior = jnp.dot(both.astype(BF16), before, preferred_element_type=F32) + cnt_scr[:, 0:1]
    rank_ref[t, 0:1, :] = jnp.sum(oh1 * prior, axis=0, keepdims=True).astype(jnp.int32)
    rank_ref[t, 1:2, :] = jnp.sum(oh2 * prior, axis=0, keepdims=True).astype(jnp.int32)
    cnt_scr[...] = cnt_scr[...] + jnp.where(counted, jnp.sum(both, axis=1, keepdims=True), 0.0)
    cnt_ref[...] = cnt_scr[...]


def _post_kernel(xp_ref, xs_ref, u_ref, uhalo_ref, sconv_ref, op_ref, os_ref, gates_ref,
                 gate1_ref, shift2_ref, scale2_ref,
                 cw_ref, cb_ref, lng_ref, lnb_ref, wpw2_ref, bpw2_ref, wao_ref, wout_ref,
                 n2_ref, wrt_ref, brt_ref,
                 x1_ref, h2_ref, eidx_ref, wtok_ref, rank_ref, cnt_ref,
                 uext, shift_scr, d_scr, s_scr, o_scr, mix_scr, h2b_scr, cnt_scr,
                 *, groups, prompt_tiles, tiles_per_seq, tiles):
    i = pl.program_id(0)
    d_model = xp_ref.shape[1]

    @pl.when(i == 0)
    def _():
        cnt_scr[...] = jnp.zeros(cnt_scr.shape, F32)
        h2b_scr[...] = jnp.zeros(h2b_scr.shape, BF16)
    nch = cw_ref.shape[0]
    cw = uext.shape[2]
    hist = uext.shape[1] - GROUP_ROWS
    lead = hist - (CONV_WIDTH - 1)

    def put_hist(g, rows):
        for c in range(nch):
            uext[g * nch + c, 0:hist, :] = rows[:, c * cw:(c + 1) * cw]

    @pl.when(i < prompt_tiles)
    def _():
        first = (i % tiles_per_seq) == 0
        put_hist(0, jnp.where(first, 0.0, uhalo_ref[...]))
        for g in range(1, groups):
            put_hist(g, u_ref[(g - 1) * GROUP_ROWS:g * GROUP_ROWS, :])
        o_scr[...] = op_ref[...]

    @pl.when(i >= prompt_tiles)
    def _():
        for g in range(groups):
            put_hist(g, sconv_ref[g])
        o_scr[...] = os_ref[...]

    for g in range(groups):
        for c in range(nch):
            uext[g * nch + c, hist:hist + GROUP_ROWS, :] = (
                u_ref[g * GROUP_ROWS:(g + 1) * GROUP_ROWS, c * cw:(c + 1) * cw])

    span = shift_scr.shape[1]

    def conv_chunk(k, carry):
        c = k % nch
        win_all = uext[k]
        for s in range(1, SUBLANES):
            shift_scr[s, :, :] = pltpu.roll(win_all, win_all.shape[0] - s, 0)[0:span, :]
        acc = None
        for j in range(CONV_WIDTH):
            base, s = divmod(lead + j, SUBLANES)
            rows = pl.ds(base * SUBLANES, GROUP_ROWS)
            win = uext[k, rows, :] if s == 0 else shift_scr[s, rows, :]
            term = win * cw_ref[c, j:j + 1, :]
            acc = term if acc is None else acc + term
        d_scr[k] = acc + cb_ref[c]
        return carry

    lax.fori_loop(0, groups * nch, conv_chunk, 0)

    for g in range(groups):
        dd = jnp.concatenate([d_scr[g * nch + c] for c in range(nch)], axis=1)
        mu = jnp.mean(dd, axis=-1, keepdims=True)
        var = jnp.mean(jnp.square(dd - mu), axis=-1, keepdims=True)
        y = (dd - mu) * lax.rsqrt(var + LN_EPS) * lng_ref[...] + lnb_ref[...]
        s_scr[g * GROUP_ROWS:(g + 1) * GROUP_ROWS, :] = (y * jax.nn.sigmoid(y)).astype(BF16)

    def route_tile(t, counted):
        lt = lax.dot_general(wrt_ref[...], h2b_scr[...], (((1,), (1,)), ((), ())),
                             preferred_element_type=F32) + brt_ref[...]
        _route(lt, t, counted, eidx_ref, wtok_ref, rank_ref, cnt_ref, cnt_scr)

    route_tile(jnp.maximum(i - 1, 0), i > 0)

    conv_out = jnp.dot(s_scr[...], wpw2_ref[...], preferred_element_type=F32) + bpw2_ref[...]
    attn_out = jnp.dot(o_scr[...], wao_ref[...], preferred_element_type=F32)
    merged = (gates_ref[:, :d_model].astype(F32) * attn_out
              + gates_ref[:, d_model:].astype(F32) * conv_out)
    mix_scr[...] = jnp.dot(merged.astype(BF16), wout_ref[...], preferred_element_type=F32)

    def residual(x_ref):
        def res_group(g, carry):
            r = pl.multiple_of(g * GROUP_ROWS, GROUP_ROWS)
            x1 = (x_ref[pl.ds(r, GROUP_ROWS), :]
                  + gate1_ref[pl.ds(g, 1), :] * mix_scr[pl.ds(r, GROUP_ROWS), :])
            x1_ref[pl.ds(r, GROUP_ROWS), :] = x1
            h = _rms_mod(x1, n2_ref[...], scale2_ref[pl.ds(g, 1), :], shift2_ref[pl.ds(g, 1), :])
            hb = h.astype(BF16)
            h2b_scr[pl.ds(r, GROUP_ROWS), :] = hb
            h2_ref[pl.ds(r, GROUP_ROWS), :] = _pack_bf16_pairs(hb)
            return carry

        lax.fori_loop(0, groups, res_group, 0)

    pl.when(i < prompt_tiles)(lambda: residual(xp_ref))
    pl.when(i >= prompt_tiles)(lambda: residual(xs_ref))

    @pl.when(i == tiles - 1)
    def _():
        route_tile(i, True)


def _post(x_p, x_s, u, sconv_pad, o_p, o_s, gates, gate1_g, shift2_g, scale2_g, p, n_prompt, seq, tm):
    d = x_p.shape[1]
    n = x_p.shape[0] + x_s.shape[0]
    cdim = u.shape[1]
    nch = cdim // CONV_CHUNK
    groups = tm // GROUP_ROWS
    tiles = n // tm
    prompt_tiles = n_prompt // tm
    const = lambda shape: pl.BlockSpec(shape, lambda i: (0,) * len(shape))
    row = lambda w: pl.BlockSpec((tm, w), lambda i: (i, 0))
    grp = pl.BlockSpec((groups, d), lambda i: (i, 0))
    outs = pl.pallas_call(
        functools.partial(_post_kernel, groups=groups, prompt_tiles=prompt_tiles,
                          tiles_per_seq=seq // tm, tiles=tiles),
        grid=(tiles,),
        in_specs=_two_stream_specs(tm, d, prompt_tiles) + [
                  row(cdim),
                  pl.BlockSpec((GROUP_ROWS, cdim), lambda i: (jnp.maximum(i * groups - 1, 0), 0)),
                  pl.BlockSpec((groups, GROUP_ROWS, cdim),
                               lambda i: (jnp.maximum(i - prompt_tiles, 0), 0, 0))]
                 + _two_stream_specs(tm, ATTN_DIM, prompt_tiles) + [
                  row(2 * d), grp, grp, grp,
                  const((nch, CONV_WIDTH, CONV_CHUNK)), const((nch, 1, CONV_CHUNK)),
                  const((1, cdim)), const((1, cdim)),
                  const((cdim, d)), const((1, d)), const((ATTN_DIM, d)), const((d, d)),
                  const((1, d)), const((ROUTER_ROWS, d)), const((ROUTER_ROWS, 1))],
        out_specs=[row(d),
                   row(d // 2),
                   const((tiles, 2, tm)), const((tiles, 2, tm)), const((tiles, 2, tm)),
                   pl.BlockSpec((N_EXPERTS, LANES), lambda i: (0, 0))],
        out_shape=[jax.ShapeDtypeStruct((n, d), F32),
                   jax.ShapeDtypeStruct((n, d // 2), jnp.uint32),
                   jax.ShapeDtypeStruct((tiles, 2, tm), jnp.int32),
                   jax.ShapeDtypeStruct((tiles, 2, tm), F32),
                   jax.ShapeDtypeStruct((tiles, 2, tm), jnp.int32),
                   jax.ShapeDtypeStruct((N_EXPERTS, LANES), F32)],
        scratch_shapes=[pltpu.VMEM((groups * nch, 2 * GROUP_ROWS, CONV_CHUNK), F32),
                        pltpu.VMEM((SUBLANES, 2 * GROUP_ROWS - SUBLANES, CONV_CHUNK), F32),
                        pltpu.VMEM((groups * nch, GROUP_ROWS, CONV_CHUNK), F32),
                        pltpu.VMEM((tm, cdim), BF16),
                        pltpu.VMEM((tm, ATTN_DIM), BF16),
                        pltpu.VMEM((tm, d), F32),
                        pltpu.VMEM((tm, d), BF16),
                        pltpu.VMEM((N_EXPERTS, LANES), F32)],
        compiler_params=_cparams("arbitrary"),
        name="post",
    )(x_p, x_s, u, u, sconv_pad, o_p, o_s, gates, gate1_g, shift2_g, scale2_g,
      p["conv_w"], p["conv_b"], p["ln_g"], p["ln_b"], p["w_pw2"], p["b_pw2"], p["w_ao"], p["w_out"],
      p["n2"], p["w_rt"], p["b_rt"])
    x1, h2, eidx, wtok, rank, cnt = outs
    per_slot = lambda a: a.transpose(1, 0, 2).reshape(2, n)
    return x1, h2, per_slot(eidx), per_slot(wtok), per_slot(rank), cnt


TABLE_GROUP = 8
GATHER_AHEAD = 8
FINAL_AHEAD = 1


def _grouped_tables(tab, group=TABLE_GROUP, ahead=1):
    steps, w = tab.shape
    assert steps % group == 0
    padded = jnp.concatenate([tab] + [tab[-1:]] * ahead, axis=0)
    cols = [tab.reshape(steps // group, group * w)] + [padded[group + a::group][:steps // group]
                                                         for a in range(ahead)]
    return jnp.concatenate(cols, axis=1).reshape(steps // group, 1, (group + ahead) * w)

def _moe_kernel(blk_e_ref, next_e_ref, nused_ref, src_ref, roww_ref,
                h2_hbm, wg_hbm, wu_hbm, wd_hbm, y_ref,
                xbuf, x16, wg_f, wu_f, wd_f, wg_b, wu_b, wd_b, gsem, wsem):
    i = pl.program_id(0)
    nused = nused_ref[0]
    nslots = GATHER_AHEAD + 1
    slot = i % nslots
    base = (i % TABLE_GROUP) * MOE_BLOCK
    weights = ((wg_hbm, wg_f, wg_b), (wu_hbm, wu_f, wu_b), (wd_hbm, wd_f, wd_b))

    def gather(first, sl):
        for r in range(MOE_BLOCK):
            pltpu.make_async_copy(h2_hbm.at[pl.ds(src_ref[0, 0, first + r], 1), :],
                                  xbuf.at[sl, pl.ds(r, 1), :], gsem.at[sl]).start(priority=0)

    def gather_wait(sl):
        pltpu.make_async_copy(h2_hbm.at[pl.ds(0, MOE_BLOCK), :], xbuf.at[sl], gsem.at[sl]).wait()

    def weights_start(e):
        for hbm, f32_buf, _ in weights:
            pltpu.make_async_copy(hbm.at[e], f32_buf, wsem.at[0]).start(priority=1)

    def weights_wait():
        for hbm, f32_buf, _ in weights:
            pltpu.make_async_copy(hbm.at[0], f32_buf, wsem.at[0]).wait()

    @pl.when(i == 0)
    def _():
        weights_start(blk_e_ref[0])
        for a in range(GATHER_AHEAD):
            gather(base + a * MOE_BLOCK, a)

    @pl.when(i >= nused)
    def _():
        y_ref[...] = jnp.zeros(y_ref.shape, y_ref.dtype)

    @pl.when(i < nused)
    def _():
        changed = jnp.logical_or(i == 0, blk_e_ref[i] != blk_e_ref[jnp.maximum(i - 1, 0)])

        @pl.when(changed)
        def _():
            weights_wait()
            for _, f32_buf, b16_buf in weights:
                b16_buf[...] = f32_buf[...].astype(BF16)
            weights_start(next_e_ref[i])

        gather_wait(slot)
        x_lo, x_hi = _unpack_bf16_pairs(xbuf[slot])
        x16[:, :x_lo.shape[1]] = x_lo.astype(BF16)
        x16[:, x_lo.shape[1]:] = x_hi.astype(BF16)

        gather(base + GATHER_AHEAD * MOE_BLOCK, (i + GATHER_AHEAD) % nslots)
        x = x16[...]
        hg = jnp.dot(x, wg_b[...], preferred_element_type=F32)
        hu = jnp.dot(x, wu_b[...], preferred_element_type=F32)
        hid = (hg * jax.nn.sigmoid(hg) * hu).astype(BF16)
        y = jnp.dot(hid, wd_b[...], preferred_element_type=F32) * roww_ref[...]
        y_ref[...] = _pack_bf16_pairs(y.astype(BF16))

        @pl.when(i == nused - 1)
        def _():
            weights_wait()
            for a in range(1, GATHER_AHEAD + 1):
                gather_wait((i + a) % nslots)


def _moe(h2, blk_e, next_e, nused, src, roww, w_gate, w_up, w_down):
    n_blocks = blk_e.shape[0]
    n_exp, d, de = w_gate.shape
    src_tables = _grouped_tables(src.reshape(n_blocks, MOE_BLOCK), ahead=GATHER_AHEAD)
    grid_spec = pltpu.PrefetchScalarGridSpec(
        num_scalar_prefetch=3,
        grid=(n_blocks,),
        in_specs=[pl.BlockSpec((1, 1, src_tables.shape[2]), lambda i, be, ne, nu: (i // TABLE_GROUP, 0, 0),
                               memory_space=pltpu.SMEM),
                  pl.BlockSpec((MOE_BLOCK, 1), lambda i, be, ne, nu: (i, 0)),
                  pl.BlockSpec(memory_space=pl.ANY),
                  pl.BlockSpec(memory_space=pl.ANY),
                  pl.BlockSpec(memory_space=pl.ANY),
                  pl.BlockSpec(memory_space=pl.ANY)],
        out_specs=pl.BlockSpec((MOE_BLOCK, d // 2), lambda i, be, ne, nu: (i, 0)),
        scratch_shapes=[pltpu.VMEM((GATHER_AHEAD + 1, MOE_BLOCK, d // 2), jnp.uint32),
                        pltpu.VMEM((MOE_BLOCK, d), BF16),
                        pltpu.VMEM((d, de), F32), pltpu.VMEM((d, de), F32), pltpu.VMEM((de, d), F32),
                        pltpu.VMEM((d, de), BF16), pltpu.VMEM((d, de), BF16), pltpu.VMEM((de, d), BF16),
                        pltpu.SemaphoreType.DMA((GATHER_AHEAD + 1,)), pltpu.SemaphoreType.DMA((1,))],
    )
    return pl.pallas_call(
        _moe_kernel,
        grid_spec=grid_spec,
        out_shape=jax.ShapeDtypeStruct((n_blocks * MOE_BLOCK, d // 2), jnp.uint32),
        compiler_params=_cparams("arbitrary"),
        name="moe",
    )(blk_e, next_e, nused, src_tables, roww, h2, w_gate, w_up, w_down)


def _dispatch(eidx, wtok, rank, counts, n):
    a_tot = 2 * n
    experts = jnp.arange(N_EXPERTS, dtype=jnp.int32)
    padded = (counts + MOE_BLOCK - 1) // MOE_BLOCK * MOE_BLOCK
    pad_end = jnp.sum(jnp.where(experts[None, :] <= experts[:, None], padded[None, :], 0), axis=1)
    pad_start = pad_end - padded
    dest = jnp.sum(jnp.where(eidx[:, :, None] == experts, pad_start, 0), axis=-1) + rank
    n_blocks = -(-(a_tot + N_EXPERTS * (MOE_BLOCK - 1)) // MOE_BLOCK)
    n_blocks = -(-n_blocks // TABLE_GROUP) * TABLE_GROUP
    n_rows = n_blocks * MOE_BLOCK
    tok = jnp.tile(jnp.arange(n, dtype=jnp.int32), 2)
    upd = jnp.stack([tok, lax.bitcast_convert_type(wtok.reshape(-1), jnp.int32)], axis=1)
    rows = jnp.zeros((n_rows, 2), jnp.int32).at[dest.reshape(-1)].set(
        upd, unique_indices=True, mode="promise_in_bounds")
    blk_start = jnp.arange(n_blocks, dtype=jnp.int32) * MOE_BLOCK
    blk_e = jnp.minimum(jnp.sum((pad_end[None, :] <= blk_start[:, None]).astype(jnp.int32), axis=1),
                        N_EXPERTS - 1)
    nused = (pad_end[-1] // MOE_BLOCK).reshape(1)
    roww = lax.bitcast_convert_type(rows[:, 1], F32)
    later = jnp.where((counts > 0)[None, :] & (experts[None, :] > experts[:, None]), experts[None, :], N_EXPERTS)
    next_of = jnp.min(later, axis=1)
    next_of = jnp.where(next_of < N_EXPERTS, next_of, experts)
    next_e = jnp.sum(jnp.where(blk_e[:, None] == experts[None, :], next_of[None, :], 0), axis=1)
    return blk_e, next_e, nused, rows[:, 0].reshape(n_blocks, 1, MOE_BLOCK), roww.reshape(n_rows, 1), dest


def _final_kernel(d0_ref, d1_ref, x1_ref, gate2_ref, fg_ref, yb_hbm,
                  op_ref, os_ref, ybuf, sem, *, groups, prompt_tiles, tiles, table_group):
    i = pl.program_id(0)
    nslots = FINAL_AHEAD + 1
    slot = i % nslots
    tm = x1_ref.shape[0]
    base = (i % table_group) * tm

    def gather(first, sl):
        for r in range(tm):
            pltpu.make_async_copy(yb_hbm.at[pl.ds(d0_ref[0, 0, first + r], 1), :],
                                  ybuf.at[sl, 0, pl.ds(r, 1), :], sem.at[sl]).start(priority=0)
            pltpu.make_async_copy(yb_hbm.at[pl.ds(d1_ref[0, 0, first + r], 1), :],
                                  ybuf.at[sl, 1, pl.ds(r, 1), :], sem.at[sl]).start(priority=1)

    def gather_wait(sl):
        for k in range(2):
            pltpu.make_async_copy(yb_hbm.at[pl.ds(0, tm), :], ybuf.at[sl, k], sem.at[sl]).wait()

    @pl.when(i == 0)
    def _():
        for a in range(FINAL_AHEAD):
            gather(base + a * tm, a)

    gather_wait(slot)
    gather(base + FINAL_AHEAD * tm, (i + FINAL_AHEAD) % nslots)

    def run(o_ref):
        def body(g, carry):
            r = pl.multiple_of(g * GROUP_ROWS, GROUP_ROWS)
            lo0, hi0 = _unpack_bf16_pairs(ybuf[slot, 0, pl.ds(r, GROUP_ROWS), :])
            lo1, hi1 = _unpack_bf16_pairs(ybuf[slot, 1, pl.ds(r, GROUP_ROWS), :])
            moe = jnp.concatenate([lo0 + lo1, hi0 + hi1], axis=1)
            x2 = x1_ref[pl.ds(r, GROUP_ROWS), :] + gate2_ref[pl.ds(g, 1), :] * moe
            ms = jnp.mean(x2 * x2, axis=-1, keepdims=True)
            o_ref[pl.ds(r, GROUP_ROWS), :] = x2 * lax.rsqrt(ms + RMS_EPS) * fg_ref[...]
            return carry

        lax.fori_loop(0, groups, body, 0)

    pl.when(i < prompt_tiles)(lambda: run(op_ref))
    pl.when(i >= prompt_tiles)(lambda: run(os_ref))

    @pl.when(i == tiles - 1)
    def _():
        for a in range(1, FINAL_AHEAD + 1):
            gather_wait((i + a) % nslots)


def _final(x1, yb, dest, gate2_g, final_g, n_prompt, tm):
    n, d = x1.shape
    groups = tm // GROUP_ROWS
    tiles = n // tm
    prompt_tiles = n_prompt // tm
    table_group = max(g for g in range(1, TABLE_GROUP + 1) if tiles % g == 0)
    tables = [_grouped_tables(dest[k].reshape(tiles, tm), table_group, FINAL_AHEAD) for k in range(2)]
    idx_spec = pl.BlockSpec((1, 1, tables[0].shape[2]), lambda i: (i // table_group, 0, 0),
                            memory_space=pltpu.SMEM)
    return pl.pallas_call(
        functools.partial(_final_kernel, groups=groups, prompt_tiles=prompt_tiles, tiles=tiles,
                          table_group=table_group),
        grid=(tiles,),
        in_specs=[idx_spec, idx_spec,
                  pl.BlockSpec((tm, d), lambda i: (i, 0)),
                  pl.BlockSpec((groups, d), lambda i: (i, 0)),
                  pl.BlockSpec((1, d), lambda i: (0, 0)),
                  pl.BlockSpec(memory_space=pl.ANY)],
        out_specs=_two_stream_specs(tm, d, prompt_tiles),
        out_shape=[jax.ShapeDtypeStruct((n_prompt, d), F32),
                   jax.ShapeDtypeStruct((n - n_prompt, d), F32)],
        scratch_shapes=[pltpu.VMEM((FINAL_AHEAD + 1, 2, tm, d // 2), jnp.uint32),
                        pltpu.SemaphoreType.DMA((FINAL_AHEAD + 1,))],
        compiler_params=_cparams("arbitrary"),
        name="final",
    )(tables[0], tables[1], x1, gate2_g, final_g.reshape(1, d), yb)


def _layer(x_p, x_s, c_all, cache_k, cache_v, sconv, rel_table, lp, seq, t_new):
    (n1, n2, w_ada, b_ada, w_in, sink, w_ao, conv_w, conv_b, ln_g, ln_b, w_pw2, b_pw2, w_out,
     w_grp, b_grp, w_rt, b_rt, w_gate, w_up, w_down) = lp
    n_prompt, d = x_p.shape
    n = n_prompt + x_s.shape[0]
    batch = n_prompt // seq
    dec_batch = (n - n_prompt) // t_new
    cdim = conv_w.shape[1]
    tm = TOKEN_TILE

    c_rows = -(-c_all.shape[0] // 8) * 8
    c_pad = jnp.pad(c_all, ((0, c_rows - c_all.shape[0]), (0, 0)))
    mods = _ada(c_pad, w_ada, b_ada)

    def per_group(k):
        m = mods[:, k * d:(k + 1) * d]
        mp = jnp.broadcast_to(m[:batch, None, :], (batch, seq // GROUP_ROWS, d))
        return jnp.concatenate([mp.reshape(batch * (seq // GROUP_ROWS), d), m[batch:batch + dec_batch]],
                               axis=0)

    shift1, scale1, gate1, shift2, scale2, gate2 = [per_group(k) for k in range(6)]

    h1 = _norm1(x_p, x_s, n1, scale1, shift1, NORM_ROWS)

    k_off = ATTN_DIM
    glu_off = k_off + 2 * KV_DIM
    gate_off = glu_off + 2 * cdim
    w_in_b = w_in.astype(BF16)
    tglu = PROJ_COLS // 2
    w_glu = jnp.stack([w_in_b[:, glu_off:glu_off + cdim].reshape(d, cdim // tglu, tglu),
                       w_in_b[:, glu_off + cdim:gate_off].reshape(d, cdim // tglu, tglu)],
                      axis=2).reshape(d, 2 * cdim)
    tmm = PROJ_ROWS
    (q,) = _proj(h1, w_in_b[:, :k_off], "qkv", tmm, PROJ_COLS, [BF16])
    (kv,) = _proj(h1, w_in_b[:, k_off:glu_off], "qkv", tmm, 2 * KV_DIM, [BF16])
    kv_state = _kv_state(h1, w_in_b[:, k_off:glu_off], n_prompt, seq)
    (u,) = _proj(h1, w_glu, "glu", tmm, 2 * tglu, [F32])
    (gates,) = _proj(h1, w_in_b[:, gate_off:], "gate", tmm, PROJ_COLS, [BF16])

    bias_p = _pair_bias(rel_table)
    buf = cache_k.shape[1]
    kpos = jnp.concatenate([jnp.arange(buf, dtype=jnp.int32) - buf, jnp.arange(t_new, dtype=jnp.int32)])
    bias_s = _rel_bias(rel_table, kpos[None, :] - jnp.arange(t_new, dtype=jnp.int32)[:, None])
    sink_f = sink.astype(F32).reshape(N_KV_HEADS, Q_PER_KV, 1, 1)
    sink_kch = sink.astype(F32).reshape(N_KV_HEADS, 1, COLS_PER_KV, HEADS_PER_COL).transpose(0, 3, 1, 2)
    sink_p = jnp.broadcast_to(sink_kch[..., None, None],
                              (N_KV_HEADS, HEADS_PER_COL, 2, COLS_PER_KV, CHUNK, LANES)).reshape(
        N_KV_HEADS, HEADS_PER_COL, 2 * COLS_PER_KV * CHUNK, LANES)
    sink_s = jnp.broadcast_to(sink_f, (N_KV_HEADS, Q_PER_KV, t_new, 1)).reshape(N_KV_HEADS, Q_PER_KV * t_new, 1)
    o_p = _attn_prompt(q, kv, bias_p, sink_p, n_prompt, seq, ATTN_PAIRS)
    o_s = _attn_sample(q, kv, cache_k.reshape(dec_batch, buf, KV_DIM), cache_v.reshape(dec_batch, buf, KV_DIM),
                       bias_s, sink_s, n_prompt, t_new)

    hist = GROUP_ROWS
    sconv_pad = jnp.pad(sconv, ((0, 0), (hist - sconv.shape[1], 0), (0, 0)))
    w_rt_t = jnp.zeros((ROUTER_ROWS, d), F32)
    w_rt_t = w_rt_t.at[:N_GROUPS].set(w_grp.T).at[EXPERT_ROW0:EXPERT_ROW0 + N_EXPERTS].set(w_rt.T)
    b_rt_t = jnp.zeros((ROUTER_ROWS, 1), F32)
    b_rt_t = b_rt_t.at[:N_GROUPS, 0].set(b_grp.astype(F32)).at[EXPERT_ROW0:EXPERT_ROW0 + N_EXPERTS, 0].set(
        b_rt.astype(F32))
    nch = cdim // CONV_CHUNK
    conv_w_c = conv_w.reshape(CONV_WIDTH, nch, CONV_CHUNK).transpose(1, 0, 2)
    params = dict(conv_w=conv_w_c, conv_b=conv_b.reshape(nch, 1, CONV_CHUNK), ln_g=ln_g.reshape(1, cdim),
                  ln_b=ln_b.reshape(1, cdim), w_pw2=w_pw2.astype(BF16), b_pw2=b_pw2.reshape(1, d),
                  w_ao=w_ao.astype(BF16), w_out=w_out.astype(BF16), n2=n2.reshape(1, d),
                  w_rt=w_rt_t.astype(BF16), b_rt=b_rt_t)
    x1, h2, eidx, wtok, rank, cnt = _post(x_p, x_s, u, sconv_pad, o_p, o_s, gates, gate1, shift2, scale2,
                                             params, n_prompt, seq, tm)

    blk_e, next_e, nused, src, roww, dest = _dispatch(eidx, wtok, rank, cnt[:, 0].astype(jnp.int32), n)
    yb = _moe(h2, blk_e, next_e, nused, src, roww, w_gate, w_up, w_down)
    return x1, yb, dest, gate2, kv_state, u


def kernel(x_prompt, x_sample, c_prompt, c_sample, cache_k, cache_v, state_conv, rel_bias_table, norm1_g, norm2_g, w_ada, b_ada, w_in, attn_sink, w_attn_o, conv_w, conv_b, conv_ln_g, conv_ln_b, w_pw2, b_pw2, w_out, w_group, b_group, w_router, b_router, w_gate, w_up, w_down, final_g):
    batch, seq, d = x_prompt.shape
    dec_batch, t_new, _ = x_sample.shape
    depth = norm1_g.shape[0]
    assert depth == 1, "single trunk layer"
    assert t_new == GROUP_ROWS and seq % GROUP_ROWS == 0
    n_prompt = batch * seq
    n = n_prompt + dec_batch * t_new
    x_p = x_prompt.reshape(n_prompt, d)
    x_s = x_sample.reshape(dec_batch * t_new, d)
    c_all = jnp.concatenate([c_prompt, c_sample], axis=0)
    l = 0
    lp = tuple(a.reshape(a.shape[1:]) for a in (
        norm1_g, norm2_g, w_ada, b_ada, w_in, attn_sink, w_attn_o, conv_w, conv_b, conv_ln_g, conv_ln_b,
        w_pw2, b_pw2, w_out, w_group, b_group, w_router, b_router, w_gate, w_up, w_down))
    x1, yb, dest, gate2, kv_state, u = _layer(x_p, x_s, c_all, cache_k[l], cache_v[l], state_conv[l],
                                              rel_bias_table, lp, seq, t_new)
    out_p, out_s = _final(x1, yb, dest, gate2, final_g, n_prompt, TOKEN_TILE)

    y_prompt = out_p.reshape(batch, seq, d)
    y_sample = out_s.reshape(dec_batch, t_new, d)
    cdim = u.shape[1]
    kvp = kv_state[:batch * WINDOW].reshape(batch, WINDOW, 2 * KV_DIM)
    new_k_prompt = kvp[..., :KV_DIM].reshape(1, batch, WINDOW, N_KV_HEADS, HEAD_DIM)
    new_v_prompt = kvp[..., KV_DIM:].reshape(1, batch, WINDOW, N_KV_HEADS, HEAD_DIM)
    new_conv_prompt = jnp.stack([u[(b + 1) * seq - (CONV_WIDTH - 1):(b + 1) * seq] for b in range(batch)])[None]
    kvs = kv_state[batch * WINDOW:].reshape(dec_batch, t_new, 2 * KV_DIM)
    buf = cache_k.shape[2]
    k_new = kvs[..., :KV_DIM].reshape(dec_batch, t_new, N_KV_HEADS, HEAD_DIM)
    v_new = kvs[..., KV_DIM:].reshape(dec_batch, t_new, N_KV_HEADS, HEAD_DIM)
    new_k_sample = jnp.concatenate([cache_k[l], k_new], axis=1)[:, -buf:][None]
    new_v_sample = jnp.concatenate([cache_v[l], v_new], axis=1)[:, -buf:][None]
    us = u[n_prompt:].reshape(dec_batch, t_new, cdim)
    new_conv_sample = jnp.concatenate([state_conv[l], us], axis=1)[:, -(CONV_WIDTH - 1):][None]
    return (y_prompt, y_sample, new_k_prompt, new_v_prompt, new_conv_prompt,
            new_k_sample, new_v_sample, new_conv_sample)
```

```python
import functools
import math

import jax
import jax.numpy as jnp
from jax import lax
from jax.experimental import pallas as pl
from jax.experimental.pallas import tpu as pltpu

F32 = jnp.float32
BF16 = jnp.bfloat16

CHUNK = 64
HEAD_DIM = 64
N_Q_HEADS = 16
N_KV_HEADS = 4
Q_PER_KV = N_Q_HEADS // N_KV_HEADS
ATTN_DIM = N_Q_HEADS * HEAD_DIM
KV_DIM = N_KV_HEADS * HEAD_DIM
WINDOW = 128
WIN_CHUNKS = WINDOW // CHUNK
CONV_WIDTH = 31
NUM_BUCKETS = 32
MAX_DISTANCE = 128
N_GROUPS = 4
EXPERTS_PER_GROUP = 8
N_EXPERTS = N_GROUPS * EXPERTS_PER_GROUP
MOE_BLOCK = 128
RMS_EPS = 1e-6
LN_EPS = 1e-5
NEG_INF = -1e30

GROUP_ROWS = 32
LANES = 128
SUBLANES = 8
CONV_CHUNK = 256
ROUTER_ROWS = 64
EXPERT_ROW0 = 8

NORM_ROWS = 512
PROJ_ROWS = 1536
PROJ_COLS = 1024
TOKEN_TILE = 256
ATTN_PAIRS = 4
V7X_VMEM_LIMIT = 56 * 1024 * 1024


def _cparams(*sem):
    return pltpu.CompilerParams(dimension_semantics=sem, vmem_limit_bytes=V7X_VMEM_LIMIT)


def _ada_kernel(c_ref, w_ref, b_ref, o_ref):
    c = c_ref[...]
    s = c * jax.nn.sigmoid(c)
    o_ref[...] = jnp.dot(s.astype(BF16), w_ref[...].astype(BF16),
                         preferred_element_type=F32) + b_ref[...]


def _ada(c_all, w_ada, b_ada):
    rows, d = c_all.shape
    ncol = w_ada.shape[1]
    tn = 1024
    return pl.pallas_call(
        _ada_kernel,
        grid=(ncol // tn,),
        in_specs=[pl.BlockSpec((rows, d), lambda j: (0, 0)),
                  pl.BlockSpec((d, tn), lambda j: (0, j)),
                  pl.BlockSpec((1, tn), lambda j: (0, j))],
        out_specs=pl.BlockSpec((rows, tn), lambda j: (0, j)),
        out_shape=jax.ShapeDtypeStruct((rows, ncol), F32),
        compiler_params=_cparams("arbitrary"),
        name="ada",
    )(c_all, w_ada, b_ada.reshape(1, ncol))


def _rms_mod(x, gain, scale, shift):
    ms = jnp.mean(x * x, axis=-1, keepdims=True)
    y = x * lax.rsqrt(ms + RMS_EPS) * gain
    return y * (1.0 + scale) + shift


def _pack_bf16_pairs(xb):
    w = xb.shape[1] // 2
    bits = lax.bitcast_convert_type(xb.astype(F32), jnp.uint32)
    return (bits[:, :w] >> 16) | (bits[:, w:] & jnp.uint32(0xFFFF0000))


def _unpack_bf16_pairs(p):
    lo = lax.bitcast_convert_type(p << 16, F32)
    hi = lax.bitcast_convert_type(p & jnp.uint32(0xFFFF0000), F32)
    return lo, hi


def _two_stream_specs(tm, width, prompt_tiles):
    return [pl.BlockSpec((tm, width), lambda i: (jnp.minimum(i, prompt_tiles - 1), 0)),
            pl.BlockSpec((tm, width), lambda i: (jnp.maximum(i - prompt_tiles, 0), 0))]


def _norm1_kernel(xp_ref, xs_ref, g_ref, sc_ref, sh_ref, h_ref, *, groups, prompt_tiles):
    def run(x_ref):
        def body(gi, carry):
            r = pl.multiple_of(gi * GROUP_ROWS, GROUP_ROWS)
            h = _rms_mod(x_ref[pl.ds(r, GROUP_ROWS), :], g_ref[...],
                         sc_ref[pl.ds(gi, 1), :], sh_ref[pl.ds(gi, 1), :])
            h_ref[pl.ds(r, GROUP_ROWS), :] = h.astype(h_ref.dtype)
            return carry

        lax.fori_loop(0, groups, body, 0, unroll=2)

    i = pl.program_id(0)
    pl.when(i < prompt_tiles)(lambda: run(xp_ref))
    pl.when(i >= prompt_tiles)(lambda: run(xs_ref))


def _norm1(x_p, x_s, gain, scale_g, shift_g, tm):
    d = x_p.shape[1]
    n = x_p.shape[0] + x_s.shape[0]
    groups = tm // GROUP_ROWS
    prompt_tiles = x_p.shape[0] // tm
    return pl.pallas_call(
        functools.partial(_norm1_kernel, groups=groups, prompt_tiles=prompt_tiles),
        grid=(n // tm,),
        in_specs=_two_stream_specs(tm, d, prompt_tiles) + [
            pl.BlockSpec((1, d), lambda i: (0, 0)),
            pl.BlockSpec((groups, d), lambda i: (i, 0)),
            pl.BlockSpec((groups, d), lambda i: (i, 0))],
        out_specs=pl.BlockSpec((tm, d), lambda i: (i, 0)),
        out_shape=jax.ShapeDtypeStruct((n, d), BF16),
        compiler_params=_cparams("arbitrary"),
        name="norm1",
    )(x_p, x_s, gain.reshape(1, d), scale_g, shift_g)


def _proj_kernel(h_ref, w_ref, *o_refs, mode):
    acc = jnp.dot(h_ref[...], w_ref[...], preferred_element_type=F32)
    if mode == "qkv":
        o_refs[0][...] = acc.astype(BF16)
    elif mode == "glu":
        half = acc.shape[1] // 2
        o_refs[0][...] = acc[:, :half] * jax.nn.sigmoid(acc[:, half:])
    else:
        o_refs[0][...] = jax.nn.sigmoid(acc).astype(BF16)


def _proj(h, w, mode, tm, tnw, out_dtypes):
    n, d = h.shape
    ncol = w.shape[1]
    tno = tnw // 2 if mode == "glu" else tnw
    nout = ncol // 2 if mode == "glu" else ncol
    outs = pl.pallas_call(
        functools.partial(_proj_kernel, mode=mode),
        grid=(n // tm, ncol // tnw),
        in_specs=[pl.BlockSpec((tm, d), lambda i, j: (i, 0)),
                  pl.BlockSpec((d, tnw), lambda i, j: (0, j))],
        out_specs=[pl.BlockSpec((tm, tno), lambda i, j: (i, j)) for _ in out_dtypes],
        out_shape=[jax.ShapeDtypeStruct((n, nout), dt) for dt in out_dtypes],
        compiler_params=_cparams("arbitrary", "arbitrary"),
        name="proj_" + mode,
    )(h, w)
    return outs


def _kv_state_kernel(h_ref, w_ref, o_ref):
    o_ref[...] = jnp.dot(h_ref[...], w_ref[...], preferred_element_type=F32)


def _kv_state(h, w_kv, n_prompt, seq):
    n, d = h.shape
    batch = n_prompt // seq
    per_seq = seq // WINDOW
    steps = batch + (n - n_prompt) // WINDOW

    def row_block(i):
        return jnp.where(i < batch, (i + 1) * per_seq - 1, n_prompt // WINDOW + i - batch)

    return pl.pallas_call(
        _kv_state_kernel,
        grid=(steps,),
        in_specs=[pl.BlockSpec((WINDOW, d), lambda i: (row_block(i), 0)),
                  pl.BlockSpec((d, 2 * KV_DIM), lambda i: (0, 0))],
        out_specs=pl.BlockSpec((WINDOW, 2 * KV_DIM), lambda i: (i, 0)),
        out_shape=jax.ShapeDtypeStruct((steps * WINDOW, 2 * KV_DIM), F32),
        compiler_params=_cparams("arbitrary"),
        name="kv_state",
    )(h, w_kv)


def _t5_bucket(rel):
    nb = NUM_BUCKETS // 2
    n = -rel
    ret = jnp.where(n < 0, nb, 0)
    n = jnp.abs(n)
    max_exact = nb // 2
    nf = jnp.maximum(n, 1).astype(F32)
    large = max_exact + (jnp.log(nf / max_exact) / math.log(MAX_DISTANCE / max_exact)
                         * (nb - max_exact)).astype(jnp.int32)
    large = jnp.minimum(large, nb - 1)
    return ret + jnp.where(n < max_exact, n, large)


def _bias_kernel(tbl_ref, bkt_ref, o_ref):
    bkt = bkt_ref[...]
    nq = bkt.shape[0]
    for k in range(N_KV_HEADS):
        for g in range(Q_PER_KV):
            acc = jnp.zeros(bkt.shape, F32)
            for b in range(NUM_BUCKETS):
                acc = jnp.where(bkt == b, tbl_ref[b, k * Q_PER_KV + g], acc)
            o_ref[k, g * nq:(g + 1) * nq, :] = acc


def _rel_bias(rel_table, rel):
    nq, nk = rel.shape
    return pl.pallas_call(
        _bias_kernel,
        in_specs=[pl.BlockSpec(memory_space=pltpu.SMEM),
                  pl.BlockSpec((nq, nk), lambda: (0, 0))],
        out_specs=pl.BlockSpec((N_KV_HEADS, Q_PER_KV * nq, nk), lambda: (0, 0, 0)),
        out_shape=jax.ShapeDtypeStruct((N_KV_HEADS, Q_PER_KV * nq, nk), F32),
        name="rel_bias",
    )(rel_table.astype(F32), _t5_bucket(rel))


PAIR_BAND = 2 * CHUNK + WINDOW
HEADS_PER_COL = LANES // HEAD_DIM
COLS_PER_KV = Q_PER_KV // HEADS_PER_COL


def _pair_bias_kernel(tbl_ref, bkt_ref, o_ref):
    for e in range(2):
        bkt = bkt_ref[e]
        for k in range(N_KV_HEADS):
            for col in range(COLS_PER_KV):
                for half in range(HEADS_PER_COL):
                    head = k * Q_PER_KV + col * HEADS_PER_COL + half
                    acc = jnp.full(bkt.shape, NEG_INF, F32)
                    for b in range(NUM_BUCKETS):
                        acc = jnp.where(bkt == b, tbl_ref[b, head], acc)
                    r0 = (e * COLS_PER_KV + col) * CHUNK
                    o_ref[k, r0:r0 + CHUNK, half * PAIR_BAND:(half + 1) * PAIR_BAND] = acc


def _pair_bias(rel_table):
    qi = jnp.arange(CHUNK, dtype=jnp.int32)[None, :, None]
    kj = jnp.arange(PAIR_BAND, dtype=jnp.int32)[None, None, :]
    e = jnp.arange(2, dtype=jnp.int32)[:, None, None]
    rel = (kj - WINDOW) - (e * CHUNK + qi)
    key_chunk = kj // CHUNK - e
    seen = (key_chunk >= 0) & (key_chunk <= WIN_CHUNKS)
    bkt = jnp.where(seen, _t5_bucket(rel), -1)
    rows = 2 * COLS_PER_KV * CHUNK
    return pl.pallas_call(
        _pair_bias_kernel,
        in_specs=[pl.BlockSpec(memory_space=pltpu.SMEM),
                  pl.BlockSpec((2, CHUNK, PAIR_BAND), lambda: (0, 0, 0))],
        out_specs=pl.BlockSpec((N_KV_HEADS, rows, HEADS_PER_COL * PAIR_BAND), lambda: (0, 0, 0)),
        out_shape=jax.ShapeDtypeStruct((N_KV_HEADS, rows, HEADS_PER_COL * PAIR_BAND), F32),
        name="pair_bias",
    )(rel_table.astype(F32), bkt)


def _attend(qc, kk_all, vv_all, bias_ref, sink_ref, mask_thr):
    nq = qc.shape[0]
    nk = kk_all.shape[0]
    pieces = []
    for k in range(N_KV_HEADS):
        qs = jnp.concatenate(
            [qc[:, (k * Q_PER_KV + g) * HEAD_DIM:(k * Q_PER_KV + g + 1) * HEAD_DIM]
             for g in range(Q_PER_KV)], axis=0)
        kk = kk_all[:, k * HEAD_DIM:(k + 1) * HEAD_DIM]
        vv = vv_all[:, k * HEAD_DIM:(k + 1) * HEAD_DIM]
        lg = lax.dot_general(qs, kk, (((1,), (1,)), ((), ())),
                             preferred_element_type=F32) * (HEAD_DIM ** -0.5) + bias_ref[k]
        if mask_thr is not None:
            col = lax.broadcasted_iota(jnp.int32, (Q_PER_KV * nq, nk), 1)
            lg = jnp.where(col < mask_thr, NEG_INF, lg)
        s = sink_ref[k]
        m = jnp.maximum(jnp.max(lg, axis=-1, keepdims=True), s)
        p = jnp.exp(lg - m)
        den = jnp.sum(p, axis=-1, keepdims=True) + jnp.exp(s - m)
        o = jnp.dot(p.astype(BF16), vv, preferred_element_type=F32) / den
        pieces.extend(o[g * nq:(g + 1) * nq, :] for g in range(Q_PER_KV))
    return jnp.concatenate(pieces, axis=1)


def _attn_prompt_kernel(q_ref, kv_ref, halo_ref, bias_ref, sink_ref, o_ref, ka, kb, va, vb, *, pairs):
    t = pl.program_id(1)
    kvcat = jnp.concatenate([halo_ref[...], kv_ref[...]], axis=0)
    low = lax.broadcasted_iota(jnp.int32, (kvcat.shape[0], LANES), 1) < HEAD_DIM
    zero = jnp.zeros((kvcat.shape[0], LANES), BF16)
    ones_low = jnp.where(low, 1.0, 0.0).astype(BF16)
    ones_high = jnp.where(low, 0.0, 1.0).astype(BF16)
    kv_cols = KV_DIM // LANES
    for col in range(2 * kv_cols):
        x = kvcat[:, col * LANES:(col + 1) * LANES]
        xs = jnp.concatenate([x[:, HEAD_DIM:], x[:, :HEAD_DIM]], axis=1)
        dst_a, dst_b = (ka, kb) if col < kv_cols else (va, vb)
        k0 = (col % kv_cols) * HEADS_PER_COL
        dst_a[k0, :, 0:LANES] = jnp.where(low, x, zero)
        dst_b[k0, :, 0:LANES] = jnp.where(low, zero, xs)
        dst_a[k0 + 1, :, 0:LANES] = jnp.where(low, xs, zero)
        dst_b[k0 + 1, :, 0:LANES] = jnp.where(low, zero, x)
    for k in range(N_KV_HEADS):
        va[k, :, LANES:2 * LANES] = ones_low
        vb[k, :, LANES:2 * LANES] = ones_high

    out_low = lax.broadcasted_iota(jnp.int32, (2 * COLS_PER_KV * CHUNK, LANES), 1) < HEAD_DIM
    for p in range(pairs):
        r0 = p * 2 * CHUNK
        for k in range(N_KV_HEADS):
            lhs = jnp.concatenate(
                [q_ref[r0 + e * CHUNK:r0 + (e + 1) * CHUNK, (k * COLS_PER_KV + c) * LANES:(k * COLS_PER_KV + c + 1) * LANES]
                 for e in range(2) for c in range(COLS_PER_KV)], axis=0)
            keys = jnp.concatenate([ka[k, r0:r0 + PAIR_BAND, :], kb[k, r0:r0 + PAIR_BAND, :]], axis=0)
            lg = lax.dot_general(lhs, keys, (((1,), (1,)), ((), ())),
                                 preferred_element_type=F32) * (HEAD_DIM ** -0.5) + bias_ref[k]
            probs, sink_terms = [], []
            for half in range(HEADS_PER_COL):
                seg = lg[:, half * PAIR_BAND:(half + 1) * PAIR_BAND]
                if p == 0:
                    kcol = lax.broadcasted_iota(jnp.int32, seg.shape, 1)
                    seg = jnp.where(kcol < jnp.where(t == 0, WINDOW, 0), NEG_INF, seg)
                s = sink_ref[k, half]
                folded = seg[:, :LANES]
                for j in range(1, PAIR_BAND // LANES):
                    folded = jnp.maximum(folded, seg[:, j * LANES:(j + 1) * LANES])
                m = jnp.maximum(jnp.max(folded, axis=-1, keepdims=True), s)
                pr = jnp.exp(seg - jnp.concatenate([m] * (PAIR_BAND // LANES), axis=1))
                sink_terms.append(jnp.exp(s - m))
                probs.append(pr.astype(BF16))
            vals = jnp.concatenate([va[k, r0:r0 + PAIR_BAND, :], vb[k, r0:r0 + PAIR_BAND, :]], axis=0)
            oe = jnp.dot(jnp.concatenate(probs, axis=1), vals, preferred_element_type=F32)
            o = oe[:, :LANES] / (oe[:, LANES:] + jnp.where(out_low, sink_terms[0], sink_terms[1]))
            for e in range(2):
                for c in range(COLS_PER_KV):
                    rr = (e * COLS_PER_KV + c) * CHUNK
                    o_ref[r0 + e * CHUNK:r0 + (e + 1) * CHUNK,
                          (k * COLS_PER_KV + c) * LANES:(k * COLS_PER_KV + c + 1) * LANES] = (
                        o[rr:rr + CHUNK, :].astype(o_ref.dtype))


def _attn_prompt(q, kv, bias, sink_rows, n_prompt, seq, pairs):
    rows = pairs * 2 * CHUNK
    tiles = seq // rows
    halo_per_tile = rows // WINDOW
    keys = WINDOW + rows
    return pl.pallas_call(
        functools.partial(_attn_prompt_kernel, pairs=pairs),
        grid=(n_prompt // seq, tiles),
        scratch_shapes=[pltpu.VMEM((N_KV_HEADS, keys, LANES), BF16) for _ in range(2)]
        + [pltpu.VMEM((N_KV_HEADS, keys, 2 * LANES), BF16) for _ in range(2)],
        in_specs=[pl.BlockSpec((rows, ATTN_DIM), lambda b, t: (b * tiles + t, 0)),
                  pl.BlockSpec((rows, 2 * KV_DIM), lambda b, t: (b * tiles + t, 0)),
                  pl.BlockSpec((WINDOW, 2 * KV_DIM),
                               lambda b, t: (jnp.maximum((b * tiles + t) * halo_per_tile - 1, 0), 0)),
                  pl.BlockSpec(bias.shape, lambda b, t: (0, 0, 0)),
                  pl.BlockSpec(sink_rows.shape, lambda b, t: (0, 0, 0, 0))],
        out_specs=pl.BlockSpec((rows, ATTN_DIM), lambda b, t: (b * tiles + t, 0)),
        out_shape=jax.ShapeDtypeStruct((n_prompt, ATTN_DIM), BF16),
        compiler_params=_cparams("arbitrary", "arbitrary"),
        name="attn_prompt",
    )(q, kv, kv, bias, sink_rows)


def _attn_sample_kernel(q_ref, kv_ref, ck_ref, cv_ref, bias_ref, sink_ref, o_ref):
    kv = kv_ref[...]
    kk = jnp.concatenate([ck_ref[0].astype(BF16), kv[:, :KV_DIM]], axis=0)
    vv = jnp.concatenate([cv_ref[0].astype(BF16), kv[:, KV_DIM:]], axis=0)
    o = _attend(q_ref[...], kk, vv, bias_ref, sink_ref, None)
    o_ref[...] = o.astype(o_ref.dtype)


def _attn_sample(q, kv, cache_k, cache_v, bias, sink_rows, n_prompt, t_new):
    dec_batch, buf, _ = cache_k.shape
    first = n_prompt // t_new
    return pl.pallas_call(
        _attn_sample_kernel,
        grid=(dec_batch,),
        in_specs=[pl.BlockSpec((t_new, ATTN_DIM), lambda s: (first + s, 0)),
                  pl.BlockSpec((t_new, 2 * KV_DIM), lambda s: (first + s, 0)),
                  pl.BlockSpec((1, buf, KV_DIM), lambda s: (s, 0, 0)),
                  pl.BlockSpec((1, buf, KV_DIM), lambda s: (s, 0, 0)),
                  pl.BlockSpec(bias.shape, lambda s: (0, 0, 0)),
                  pl.BlockSpec(sink_rows.shape, lambda s: (0, 0, 0))],
        out_specs=pl.BlockSpec((t_new, ATTN_DIM), lambda s: (s, 0)),
        out_shape=jax.ShapeDtypeStruct((dec_batch * t_new, ATTN_DIM), BF16),
        compiler_params=_cparams("arbitrary"),
        name="attn_sample",
    )(q, kv, cache_k, cache_v, bias, sink_rows)


def _route(lt, t, counted, eidx_ref, wtok_ref, rank_ref, cnt_ref, cnt_scr):
    gl = [lt[r:r + 1, :] for r in range(N_GROUPS)]
    gmax = gl[0]
    gsel = jnp.zeros(gl[0].shape, jnp.int32)
    for r in range(1, N_GROUPS):
        better = gl[r] > gmax
        gsel = jnp.where(better, r, gsel)
        gmax = jnp.maximum(gmax, gl[r])
    gexp = [jnp.exp(v - gmax) for v in gl]
    gsum = gexp[0]
    for r in range(1, N_GROUPS):
        gsum = gsum + gexp[r]
    psel = jnp.zeros(gl[0].shape, F32)
    for r in range(N_GROUPS):
        psel = jnp.where(gsel == r, gexp[r] / gsum, psel)
    el = jnp.zeros((EXPERTS_PER_GROUP, lt.shape[1]), F32)
    for r in range(N_GROUPS):
        lo = EXPERT_ROW0 + r * EXPERTS_PER_GROUP
        el = jnp.where(gsel == r, lt[lo:lo + EXPERTS_PER_GROUP, :], el)
    emax = jnp.max(el, axis=0, keepdims=True)
    ee = jnp.exp(el - emax)
    pin = ee / jnp.sum(ee, axis=0, keepdims=True)
    idx = lax.broadcasted_iota(jnp.int32, pin.shape, 0)
    p1 = jnp.max(pin, axis=0, keepdims=True)
    i1 = jnp.min(jnp.where(pin == p1, idx, EXPERTS_PER_GROUP), axis=0, keepdims=True)
    rest = jnp.where(idx == i1, -1.0, pin)
    p2 = jnp.max(rest, axis=0, keepdims=True)
    i2 = jnp.min(jnp.where(rest == p2, idx, EXPERTS_PER_GROUP), axis=0, keepdims=True)
    tot = p1 + p2
    e1 = gsel * EXPERTS_PER_GROUP + i1
    e2 = gsel * EXPERTS_PER_GROUP + i2
    eidx_ref[t, 0:1, :] = e1
    eidx_ref[t, 1:2, :] = e2
    wtok_ref[t, 0:1, :] = psel * p1 / tot
    wtok_ref[t, 1:2, :] = psel * p2 / tot

    rows = lt.shape[1]
    eiota = lax.broadcasted_iota(jnp.int32, (N_EXPERTS, rows), 0)
    oh1 = (eiota == e1).astype(F32)
    oh2 = (eiota == e2).astype(F32)
    both = oh1 + oh2
    before = (lax.broadcasted_iota(jnp.int32, (rows, rows), 0)
              < lax.broadcasted_iota(jnp.int32, (rows, rows), 1)).astype(BF16)
    prior = jnp.dot(both.astype(BF16), before, preferred_element_type=F32) + cnt_scr[:, 0:1]
    rank_ref[t, 0:1, :] = jnp.sum(oh1 * prior, axis=0, keepdims=True).astype(jnp.int32)
    rank_ref[t, 1:2, :] = jnp.sum(oh2 * prior, axis=0, keepdims=True).astype(jnp.int32)
    cnt_scr[...] = cnt_scr[...] + jnp.where(counted, jnp.sum(both, axis=1, keepdims=True), 0.0)
    cnt_ref[...] = cnt_scr[...]


def _post_kernel(xp_ref, xs_ref, u_ref, uhalo_ref, sconv_ref, op_ref, os_ref, gates_ref,
                 gate1_ref, shift2_ref, scale2_ref,
                 cw_ref, cb_ref, lng_ref, lnb_ref, wpw2_ref, bpw2_ref, wao_ref, wout_ref,
                 n2_ref, wrt_ref, brt_ref,
                 x1_ref, h2_ref, eidx_ref, wtok_ref, rank_ref, cnt_ref,
                 uext, shift_scr, d_scr, s_scr, o_scr, mix_scr, h2b_scr, cnt_scr,
                 *, groups, prompt_tiles, tiles_per_seq, tiles):
    i = pl.program_id(0)
    d_model = xp_ref.shape[1]

    @pl.when(i == 0)
    def _():
        cnt_scr[...] = jnp.zeros(cnt_scr.shape, F32)
        h2b_scr[...] = jnp.zeros(h2b_scr.shape, BF16)
    nch = cw_ref.shape[0]
    cw = uext.shape[2]
    hist = uext.shape[1] - GROUP_ROWS
    lead = hist - (CONV_WIDTH - 1)

    def put_hist(g, rows):
        for c in range(nch):
            uext[g * nch + c, 0:hist, :] = rows[:, c * cw:(c + 1) * cw]

    @pl.when(i < prompt_tiles)
    def _():
        first = (i % tiles_per_seq) == 0
        put_hist(0, jnp.where(first, 0.0, uhalo_ref[...]))
        for g in range(1, groups):
            put_hist(g, u_ref[(g - 1) * GROUP_ROWS:g * GROUP_ROWS, :])
        o_scr[...] = op_ref[...]

    @pl.when(i >= prompt_tiles)
    def _():
        for g in range(groups):
            put_hist(g, sconv_ref[g])
        o_scr[...] = os_ref[...]

    for g in range(groups):
        for c in range(nch):
            uext[g * nch + c, hist:hist + GROUP_ROWS, :] = (
                u_ref[g * GROUP_ROWS:(g + 1) * GROUP_ROWS, c * cw:(c + 1) * cw])

    span = shift_scr.shape[1]

    def conv_chunk(k, carry):
        c = k % nch
        win_all = uext[k]
        for s in range(1, SUBLANES):
            shift_scr[s, :, :] = pltpu.roll(win_all, win_all.shape[0] - s, 0)[0:span, :]
        acc = None
        for j in range(CONV_WIDTH):
            base, s = divmod(lead + j, SUBLANES)
            rows = pl.ds(base * SUBLANES, GROUP_ROWS)
            win = uext[k, rows, :] if s == 0 else shift_scr[s, rows, :]
            term = win * cw_ref[c, j:j + 1, :]
            acc = term if acc is None else acc + term
        d_scr[k] = acc + cb_ref[c]
        return carry

    lax.fori_loop(0, groups * nch, conv_chunk, 0)

    for g in range(groups):
        dd = jnp.concatenate([d_scr[g * nch + c] for c in range(nch)], axis=1)
        mu = jnp.mean(dd, axis=-1, keepdims=True)
        var = jnp.mean(jnp.square(dd - mu), axis=-1, keepdims=True)
        y = (dd - mu) * lax.rsqrt(var + LN_EPS) * lng_ref[...] + lnb_ref[...]
        s_scr[g * GROUP_ROWS:(g + 1) * GROUP_ROWS, :] = (y * jax.nn.sigmoid(y)).astype(BF16)

    def route_tile(t, counted):
        lt = lax.dot_general(wrt_ref[...], h2b_scr[...], (((1,), (1,)), ((), ())),
                             preferred_element_type=F32) + brt_ref[...]
        _route(lt, t, counted, eidx_ref, wtok_ref, rank_ref, cnt_ref, cnt_scr)

    route_tile(jnp.maximum(i - 1, 0), i > 0)

    conv_out = jnp.dot(s_scr[...], wpw2_ref[...], preferred_element_type=F32) + bpw2_ref[...]
    attn_out = jnp.dot(o_scr[...], wao_ref[...], preferred_element_type=F32)
    merged = (gates_ref[:, :d_model].astype(F32) * attn_out
              + gates_ref[:, d_model:].astype(F32) * conv_out)
    mix_scr[...] = jnp.dot(merged.astype(BF16), wout_ref[...], preferred_element_type=F32)

    def residual(x_ref):
        def res_group(g, carry):
            r = pl.multiple_of(g * GROUP_ROWS, GROUP_ROWS)
            x1 = (x_ref[pl.ds(r, GROUP_ROWS), :]
                  + gate1_ref[pl.ds(g, 1), :] * mix_scr[pl.ds(r, GROUP_ROWS), :])
            x1_ref[pl.ds(r, GROUP_ROWS), :] = x1
            h = _rms_mod(x1, n2_ref[...], scale2_ref[pl.ds(g, 1), :], shift2_ref[pl.ds(g, 1), :])
            hb = h.astype(BF16)
            h2b_scr[pl.ds(r, GROUP_ROWS), :] = hb
            h2_ref[pl.ds(r, GROUP_ROWS), :] = _pack_bf16_pairs(hb)
            return carry

        lax.fori_loop(0, groups, res_group, 0, unroll=2)

    pl.when(i < prompt_tiles)(lambda: residual(xp_ref))
    pl.when(i >= prompt_tiles)(lambda: residual(xs_ref))

    @pl.when(i == tiles - 1)
    def _():
        route_tile(i, True)


def _post(x_p, x_s, u, sconv_pad, o_p, o_s, gates, gate1_g, shift2_g, scale2_g, p, n_prompt, seq, tm):
    d = x_p.shape[1]
    n = x_p.shape[0] + x_s.shape[0]
    cdim = u.shape[1]
    nch = cdim // CONV_CHUNK
    groups = tm // GROUP_ROWS
    tiles = n // tm
    prompt_tiles = n_prompt // tm
    const = lambda shape: pl.BlockSpec(shape, lambda i: (0,) * len(shape))
    row = lambda w: pl.BlockSpec((tm, w), lambda i: (i, 0))
    grp = pl.BlockSpec((groups, d), lambda i: (i, 0))
    outs = pl.pallas_call(
        functools.partial(_post_kernel, groups=groups, prompt_tiles=prompt_tiles,
                          tiles_per_seq=seq // tm, tiles=tiles),
        grid=(tiles,),
        in_specs=_two_stream_specs(tm, d, prompt_tiles) + [
                  row(cdim),
                  pl.BlockSpec((GROUP_ROWS, cdim), lambda i: (jnp.maximum(i * groups - 1, 0), 0)),
                  pl.BlockSpec((groups, GROUP_ROWS, cdim),
                               lambda i: (jnp.maximum(i - prompt_tiles, 0), 0, 0))]
                 + _two_stream_specs(tm, ATTN_DIM, prompt_tiles) + [
                  row(2 * d), grp, grp, grp,
                  const((nch, CONV_WIDTH, CONV_CHUNK)), const((nch, 1, CONV_CHUNK)),
                  const((1, cdim)), const((1, cdim)),
                  const((cdim, d)), const((1, d)), const((ATTN_DIM, d)), const((d, d)),
                  const((1, d)), const((ROUTER_ROWS, d)), const((ROUTER_ROWS, 1))],
        out_specs=[row(d),
                   row(d // 2),
                   const((tiles, 2, tm)), const((tiles, 2, tm)), const((tiles, 2, tm)),
                   pl.BlockSpec((N_EXPERTS, LANES), lambda i: (0, 0))],
        out_shape=[jax.ShapeDtypeStruct((n, d), F32),
                   jax.ShapeDtypeStruct((n, d // 2), jnp.uint32),
                   jax.ShapeDtypeStruct((tiles, 2, tm), jnp.int32),
                   jax.ShapeDtypeStruct((tiles, 2, tm), F32),
                   jax.ShapeDtypeStruct((tiles, 2, tm), jnp.int32),
                   jax.ShapeDtypeStruct((N_EXPERTS, LANES), F32)],
        scratch_shapes=[pltpu.VMEM((groups * nch, 2 * GROUP_ROWS, CONV_CHUNK), F32),
                        pltpu.VMEM((SUBLANES, 2 * GROUP_ROWS - SUBLANES, CONV_CHUNK), F32),
                        pltpu.VMEM((groups * nch, GROUP_ROWS, CONV_CHUNK), F32),
                        pltpu.VMEM((tm, cdim), BF16),
                        pltpu.VMEM((tm, ATTN_DIM), BF16),
                        pltpu.VMEM((tm, d), F32),
                        pltpu.VMEM((tm, d), BF16),
                        pltpu.VMEM((N_EXPERTS, LANES), F32)],
        compiler_params=_cparams("arbitrary"),
        name="post",
    )(x_p, x_s, u, u, sconv_pad, o_p, o_s, gates, gate1_g, shift2_g, scale2_g,
      p["conv_w"], p["conv_b"], p["ln_g"], p["ln_b"], p["w_pw2"], p["b_pw2"], p["w_ao"], p["w_out"],
      p["n2"], p["w_rt"], p["b_rt"])
    x1, h2, eidx, wtok, rank, cnt = outs
    per_slot = lambda a: a.transpose(1, 0, 2).reshape(2, n)
    return x1, h2, per_slot(eidx), per_slot(wtok), per_slot(rank), cnt


TABLE_GROUP = 8
GATHER_AHEAD = 8
FINAL_AHEAD = 1


def _grouped_tables(tab, group=TABLE_GROUP, ahead=1):
    steps, w = tab.shape
    assert steps % group == 0
    padded = jnp.concatenate([tab] + [tab[-1:]] * ahead, axis=0)
    cols = [tab.reshape(steps // group, group * w)] + [padded[group + a::group][:steps // group]
                                                         for a in range(ahead)]
    return jnp.concatenate(cols, axis=1).reshape(steps // group, 1, (group + ahead) * w)

def _moe_kernel(blk_e_ref, next_e_ref, nused_ref, src_ref, roww_ref,
                h2_hbm, wg_hbm, wu_hbm, wd_hbm, y_ref,
                xbuf, x16, wg_f, wu_f, wd_f, wg_b, wu_b, wd_b, gsem, wsem):
    i = pl.program_id(0)
    nused = nused_ref[0]
    nslots = GATHER_AHEAD + 1
    slot = i % nslots
    base = (i % TABLE_GROUP) * MOE_BLOCK
    weights = ((wg_hbm, wg_f, wg_b), (wu_hbm, wu_f, wu_b), (wd_hbm, wd_f, wd_b))

    def gather(first, sl):
        for r in range(MOE_BLOCK):
            pltpu.make_async_copy(h2_hbm.at[pl.ds(src_ref[0, 0, first + r], 1), :],
                                  xbuf.at[sl, pl.ds(r, 1), :], gsem.at[sl]).start(priority=0)

    def gather_wait(sl):
        pltpu.make_async_copy(h2_hbm.at[pl.ds(0, MOE_BLOCK), :], xbuf.at[sl], gsem.at[sl]).wait()

    def weights_start(e):
        for hbm, f32_buf, _ in weights:
            pltpu.make_async_copy(hbm.at[e], f32_buf, wsem.at[0]).start(priority=1)

    def weights_wait():
        for hbm, f32_buf, _ in weights:
            pltpu.make_async_copy(hbm.at[0], f32_buf, wsem.at[0]).wait()

    @pl.when(i == 0)
    def _():
        weights_start(blk_e_ref[0])
        for a in range(GATHER_AHEAD):
            gather(base + a * MOE_BLOCK, a)

    @pl.when(i >= nused)
    def _():
        y_ref[...] = jnp.zeros(y_ref.shape, y_ref.dtype)

    @pl.when(i < nused)
    def _():
        changed = jnp.logical_or(i == 0, blk_e_ref[i] != blk_e_ref[jnp.maximum(i - 1, 0)])

        @pl.when(changed)
        def _():
            weights_wait()
            for _, f32_buf, b16_buf in weights:
                b16_buf[...] = f32_buf[...].astype(BF16)
            weights_start(next_e_ref[i])

        gather_wait(slot)
        x_lo, x_hi = _unpack_bf16_pairs(xbuf[slot])
        x16[:, :x_lo.shape[1]] = x_lo.astype(BF16)
        x16[:, x_lo.shape[1]:] = x_hi.astype(BF16)

        gather(base + GATHER_AHEAD * MOE_BLOCK, (i + GATHER_AHEAD) % nslots)
        x = x16[...]
        hg = jnp.dot(x, wg_b[...], preferred_element_type=F32)
        hu = jnp.dot(x, wu_b[...], preferred_element_type=F32)
        hid = (hg * jax.nn.sigmoid(hg) * hu).astype(BF16)
        y = jnp.dot(hid, wd_b[...], preferred_element_type=F32) * roww_ref[...]
        y_ref[...] = _pack_bf16_pairs(y.astype(BF16))

        @pl.when(i == nused - 1)
        def _():
            weights_wait()
            for a in range(1, GATHER_AHEAD + 1):
                gather_wait((i + a) % nslots)


def _moe(h2, blk_e, next_e, nused, src, roww, w_gate, w_up, w_down):
    n_blocks = blk_e.shape[0]
    n_exp, d, de = w_gate.shape
    src_tables = _grouped_tables(src.reshape(n_blocks, MOE_BLOCK), ahead=GATHER_AHEAD)
    grid_spec = pltpu.PrefetchScalarGridSpec(
        num_scalar_prefetch=3,
        grid=(n_blocks,),
        in_specs=[pl.BlockSpec((1, 1, src_tables.shape[2]), lambda i, be, ne, nu: (i // TABLE_GROUP, 0, 0),
                               memory_space=pltpu.SMEM),
                  pl.BlockSpec((MOE_BLOCK, 1), lambda i, be, ne, nu: (i, 0)),
                  pl.BlockSpec(memory_space=pl.ANY),
                  pl.BlockSpec(memory_space=pl.ANY),
                  pl.BlockSpec(memory_space=pl.ANY),
                  pl.BlockSpec(memory_space=pl.ANY)],
        out_specs=pl.BlockSpec((MOE_BLOCK, d // 2), lambda i, be, ne, nu: (i, 0)),
        scratch_shapes=[pltpu.VMEM((GATHER_AHEAD + 1, MOE_BLOCK, d // 2), jnp.uint32),
                        pltpu.VMEM((MOE_BLOCK, d), BF16),
                        pltpu.VMEM((d, de), F32), pltpu.VMEM((d, de), F32), pltpu.VMEM((de, d), F32),
                        pltpu.VMEM((d, de), BF16), pltpu.VMEM((d, de), BF16), pltpu.VMEM((de, d), BF16),
                        pltpu.SemaphoreType.DMA((GATHER_AHEAD + 1,)), pltpu.SemaphoreType.DMA((1,))],
    )
    return pl.pallas_call(
        _moe_kernel,
        grid_spec=grid_spec,
        out_shape=jax.ShapeDtypeStruct((n_blocks * MOE_BLOCK, d // 2), jnp.uint32),
        compiler_params=_cparams("arbitrary"),
        name="moe",
    )(blk_e, next_e, nused, src_tables, roww, h2, w_gate, w_up, w_down)


def _dispatch(eidx, wtok, rank, counts, n):
    a_tot = 2 * n
    experts = jnp.arange(N_EXPERTS, dtype=jnp.int32)
    padded = (counts + MOE_BLOCK - 1) // MOE_BLOCK * MOE_BLOCK
    pad_end = jnp.sum(jnp.where(experts[None, :] <= experts[:, None], padded[None, :], 0), axis=1)
    pad_start = pad_end - padded
    dest = jnp.sum(jnp.where(eidx[:, :, None] == experts, pad_start, 0), axis=-1) + rank
    n_blocks = -(-(a_tot + N_EXPERTS * (MOE_BLOCK - 1)) // MOE_BLOCK)
    n_blocks = -(-n_blocks // TABLE_GROUP) * TABLE_GROUP
    n_rows = n_blocks * MOE_BLOCK
    tok = jnp.tile(jnp.arange(n, dtype=jnp.int32), 2)
    upd = jnp.stack([tok, lax.bitcast_convert_type(wtok.reshape(-1), jnp.int32)], axis=1)
    rows = jnp.zeros((n_rows, 2), jnp.int32).at[dest.reshape(-1)].set(
        upd, unique_indices=True, mode="promise_in_bounds")
    blk_start = jnp.arange(n_blocks, dtype=jnp.int32) * MOE_BLOCK
    blk_e = jnp.minimum(jnp.sum((pad_end[None, :] <= blk_start[:, None]).astype(jnp.int32), axis=1),
                        N_EXPERTS - 1)
    nused = (pad_end[-1] // MOE_BLOCK).reshape(1)
    roww = lax.bitcast_convert_type(rows[:, 1], F32)
    later = jnp.where((counts > 0)[None, :] & (experts[None, :] > experts[:, None]), experts[None, :], N_EXPERTS)
    next_of = jnp.min(later, axis=1)
    next_of = jnp.where(next_of < N_EXPERTS, next_of, experts)
    next_e = jnp.sum(jnp.where(blk_e[:, None] == experts[None, :], next_of[None, :], 0), axis=1)
    return blk_e, next_e, nused, rows[:, 0].reshape(n_blocks, 1, MOE_BLOCK), roww.reshape(n_rows, 1), dest


def _final_kernel(d0_ref, d1_ref, x1_ref, gate2_ref, fg_ref, yb_hbm,
                  op_ref, os_ref, ybuf, sem, *, groups, prompt_tiles, tiles, table_group):
    i = pl.program_id(0)
    nslots = FINAL_AHEAD + 1
    slot = i % nslots
    tm = x1_ref.shape[0]
    base = (i % table_group) * tm

    def gather(first, sl):
        for r in range(tm):
            pltpu.make_async_copy(yb_hbm.at[pl.ds(d0_ref[0, 0, first + r], 1), :],
                                  ybuf.at[sl, 0, pl.ds(r, 1), :], sem.at[sl]).start(priority=0)
            pltpu.make_async_copy(yb_hbm.at[pl.ds(d1_ref[0, 0, first + r], 1), :],
                                  ybuf.at[sl, 1, pl.ds(r, 1), :], sem.at[sl]).start(priority=1)

    def gather_wait(sl):
        for k in range(2):
            pltpu.make_async_copy(yb_hbm.at[pl.ds(0, tm), :], ybuf.at[sl, k], sem.at[sl]).wait()

    @pl.when(i == 0)
    def _():
        for a in range(FINAL_AHEAD):
            gather(base + a * tm, a)

    gather_wait(slot)
    gather(base + FINAL_AHEAD * tm, (i + FINAL_AHEAD) % nslots)

    def run(o_ref):
        def body(g, carry):
            r = pl.multiple_of(g * GROUP_ROWS, GROUP_ROWS)
            lo0, hi0 = _unpack_bf16_pairs(ybuf[slot, 0, pl.ds(r, GROUP_ROWS), :])
            lo1, hi1 = _unpack_bf16_pairs(ybuf[slot, 1, pl.ds(r, GROUP_ROWS), :])
            moe = jnp.concatenate([lo0 + lo1, hi0 + hi1], axis=1)
            x2 = x1_ref[pl.ds(r, GROUP_ROWS), :] + gate2_ref[pl.ds(g, 1), :] * moe
            ms = jnp.mean(x2 * x2, axis=-1, keepdims=True)
            o_ref[pl.ds(r, GROUP_ROWS), :] = x2 * lax.rsqrt(ms + RMS_EPS) * fg_ref[...]
            return carry

        lax.fori_loop(0, groups, body, 0)

    pl.when(i < prompt_tiles)(lambda: run(op_ref))
    pl.when(i >= prompt_tiles)(lambda: run(os_ref))

    @pl.when(i == tiles - 1)
    def _():
        for a in range(1, FINAL_AHEAD + 1):
            gather_wait((i + a) % nslots)


def _final(x1, yb, dest, gate2_g, final_g, n_prompt, tm):
    n, d = x1.shape
    groups = tm // GROUP_ROWS
    tiles = n // tm
    prompt_tiles = n_prompt // tm
    table_group = max(g for g in range(1, TABLE_GROUP + 1) if tiles % g == 0)
    tables = [_grouped_tables(dest[k].reshape(tiles, tm), table_group, FINAL_AHEAD) for k in range(2)]
    idx_spec = pl.BlockSpec((1, 1, tables[0].shape[2]), lambda i: (i // table_group, 0, 0),
                            memory_space=pltpu.SMEM)
    return pl.pallas_call(
        functools.partial(_final_kernel, groups=groups, prompt_tiles=prompt_tiles, tiles=tiles,
                          table_group=table_group),
        grid=(tiles,),
        in_specs=[idx_spec, idx_spec,
                  pl.BlockSpec((tm, d), lambda i: (i, 0)),
                  pl.BlockSpec((groups, d), lambda i: (i, 0)),
                  pl.BlockSpec((1, d), lambda i: (0, 0)),
                  pl.BlockSpec(memory_space=pl.ANY)],
        out_specs=_two_stream_specs(tm, d, prompt_tiles),
        out_shape=[jax.ShapeDtypeStruct((n_prompt, d), F32),
                   jax.ShapeDtypeStruct((n - n_prompt, d), F32)],
        scratch_shapes=[pltpu.VMEM((FINAL_AHEAD + 1, 2, tm, d // 2), jnp.uint32),
                        pltpu.SemaphoreType.DMA((FINAL_AHEAD + 1,))],
        compiler_params=_cparams("arbitrary"),
        name="final",
    )(tables[0], tables[1], x1, gate2_g, final_g.reshape(1, d), yb)


def _layer(x_p, x_s, c_all, cache_k, cache_v, sconv, rel_table, lp, seq, t_new):
    (n1, n2, w_ada, b_ada, w_in, sink, w_ao, conv_w, conv_b, ln_g, ln_b, w_pw2, b_pw2, w_out,
     w_grp, b_grp, w_rt, b_rt, w_gate, w_up, w_down) = lp
    n_prompt, d = x_p.shape
    n = n_prompt + x_s.shape[0]
    batch = n_prompt // seq
    dec_batch = (n - n_prompt) // t_new
    cdim = conv_w.shape[1]
    tm = TOKEN_TILE

    c_rows = -(-c_all.shape[0] // 8) * 8
    c_pad = jnp.pad(c_all, ((0, c_rows - c_all.shape[0]), (0, 0)))
    mods = _ada(c_pad, w_ada, b_ada)

    def per_group(k):
        m = mods[:, k * d:(k + 1) * d]
        mp = jnp.broadcast_to(m[:batch, None, :], (batch, seq // GROUP_ROWS, d))
        return jnp.concatenate([mp.reshape(batch * (seq // GROUP_ROWS), d), m[batch:batch + dec_batch]],
                               axis=0)

    shift1, scale1, gate1, shift2, scale2, gate2 = [per_group(k) for k in range(6)]

    h1 = _norm1(x_p, x_s, n1, scale1, shift1, NORM_ROWS)

    k_off = ATTN_DIM
    glu_off = k_off + 2 * KV_DIM
    gate_off = glu_off + 2 * cdim
    w_in_b = w_in.astype(BF16)
    tglu = PROJ_COLS // 2
    w_glu = jnp.stack([w_in_b[:, glu_off:glu_off + cdim].reshape(d, cdim // tglu, tglu),
                       w_in_b[:, glu_off + cdim:gate_off].reshape(d, cdim // tglu, tglu)],
                      axis=2).reshape(d, 2 * cdim)
    tmm = PROJ_ROWS
    (q,) = _proj(h1, w_in_b[:, :k_off], "qkv", tmm, PROJ_COLS, [BF16])
    (kv,) = _proj(h1, w_in_b[:, k_off:glu_off], "qkv", tmm, 2 * KV_DIM, [BF16])
    kv_state = _kv_state(h1, w_in_b[:, k_off:glu_off], n_prompt, seq)
    (u,) = _proj(h1, w_glu, "glu", tmm, 2 * tglu, [F32])
    (gates,) = _proj(h1, w_in_b[:, gate_off:], "gate", tmm, PROJ_COLS, [BF16])

    bias_p = _pair_bias(rel_table)
    buf = cache_k.shape[1]
    kpos = jnp.concatenate([jnp.arange(buf, dtype=jnp.int32) - buf, jnp.arange(t_new, dtype=jnp.int32)])
    bias_s = _rel_bias(rel_table, kpos[None, :] - jnp.arange(t_new, dtype=jnp.int32)[:, None])
    sink_f = sink.astype(F32).reshape(N_KV_HEADS, Q_PER_KV, 1, 1)
    sink_kch = sink.astype(F32).reshape(N_KV_HEADS, 1, COLS_PER_KV, HEADS_PER_COL).transpose(0, 3, 1, 2)
    sink_p = jnp.broadcast_to(sink_kch[..., None, None],
                              (N_KV_HEADS, HEADS_PER_COL, 2, COLS_PER_KV, CHUNK, LANES)).reshape(
        N_KV_HEADS, HEADS_PER_COL, 2 * COLS_PER_KV * CHUNK, LANES)
    sink_s = jnp.broadcast_to(sink_f, (N_KV_HEADS, Q_PER_KV, t_new, 1)).reshape(N_KV_HEADS, Q_PER_KV * t_new, 1)
    o_p = _attn_prompt(q, kv, bias_p, sink_p, n_prompt, seq, ATTN_PAIRS)
    o_s = _attn_sample(q, kv, cache_k.reshape(dec_batch, buf, KV_DIM), cache_v.reshape(dec_batch, buf, KV_DIM),
                       bias_s, sink_s, n_prompt, t_new)

    hist = GROUP_ROWS
    sconv_pad = jnp.pad(sconv, ((0, 0), (hist - sconv.shape[1], 0), (0, 0)))
    w_rt_t = jnp.zeros((ROUTER_ROWS, d), F32)
    w_rt_t = w_rt_t.at[:N_GROUPS].set(w_grp.T).at[EXPERT_ROW0:EXPERT_ROW0 + N_EXPERTS].set(w_rt.T)
    b_rt_t = jnp.zeros((ROUTER_ROWS, 1), F32)
    b_rt_t = b_rt_t.at[:N_GROUPS, 0].set(b_grp.astype(F32)).at[EXPERT_ROW0:EXPERT_ROW0 + N_EXPERTS, 0].set(
        b_rt.astype(F32))
    nch = cdim // CONV_CHUNK
    conv_w_c = conv_w.reshape(CONV_WIDTH, nch, CONV_CHUNK).transpose(1, 0, 2)
    params = dict(conv_w=conv_w_c, conv_b=conv_b.reshape(nch, 1, CONV_CHUNK), ln_g=ln_g.reshape(1, cdim),
                  ln_b=ln_b.reshape(1, cdim), w_pw2=w_pw2.astype(BF16), b_pw2=b_pw2.reshape(1, d),
                  w_ao=w_ao.astype(BF16), w_out=w_out.astype(BF16), n2=n2.reshape(1, d),
                  w_rt=w_rt_t.astype(BF16), b_rt=b_rt_t)
    x1, h2, eidx, wtok, rank, cnt = _post(x_p, x_s, u, sconv_pad, o_p, o_s, gates, gate1, shift2, scale2,
                                             params, n_prompt, seq, tm)

    blk_e, next_e, nused, src, roww, dest = _dispatch(eidx, wtok, rank, cnt[:, 0].astype(jnp.int32), n)
    yb = _moe(h2, blk_e, next_e, nused, src, roww, w_gate, w_up, w_down)
    return x1, yb, dest, gate2, kv_state, u


def kernel(x_prompt, x_sample, c_prompt, c_sample, cache_k, cache_v, state_conv, rel_bias_table, norm1_g, norm2_g, w_ada, b_ada, w_in, attn_sink, w_attn_o, conv_w, conv_b, conv_ln_g, conv_ln_b, w_pw2, b_pw2, w_out, w_group, b_group, w_router, b_router, w_gate, w_up, w_down, final_g):
    batch, seq, d = x_prompt.shape
    dec_batch, t_new, _ = x_sample.shape
    depth = norm1_g.shape[0]
    assert depth == 1, "single trunk layer"
    assert t_new == GROUP_ROWS and seq % GROUP_ROWS == 0
    n_prompt = batch * seq
    n = n_prompt + dec_batch * t_new
    x_p = x_prompt.reshape(n_prompt, d)
    x_s = x_sample.reshape(dec_batch * t_new, d)
    c_all = jnp.concatenate([c_prompt, c_sample], axis=0)
    l = 0
    lp = tuple(a.reshape(a.shape[1:]) for a in (
        norm1_g, norm2_g, w_ada, b_ada, w_in, attn_sink, w_attn_o, conv_w, conv_b, conv_ln_g, conv_ln_b,
        w_pw2, b_pw2, w_out, w_group, b_group, w_router, b_router, w_gate, w_up, w_down))
    x1, yb, dest, gate2, kv_state, u = _layer(x_p, x_s, c_all, cache_k[l], cache_v[l], state_conv[l],
                                              rel_bias_table, lp, seq, t_new)
    out_p, out_s = _final(x1, yb, dest, gate2, final_g, n_prompt, TOKEN_TILE)

    y_prompt = out_p.reshape(batch, seq, d)
    y_sample = out_s.reshape(dec_batch, t_new, d)
    cdim = u.shape[1]
    kvp = kv_state[:batch * WINDOW].reshape(batch, WINDOW, 2 * KV_DIM)
    new_k_prompt = kvp[..., :KV_DIM].reshape(1, batch, WINDOW, N_KV_HEADS, HEAD_DIM)
    new_v_prompt = kvp[..., KV_DIM:].reshape(1, batch, WINDOW, N_KV_HEADS, HEAD_DIM)
    new_conv_prompt = jnp.stack([u[(b + 1) * seq - (CONV_WIDTH - 1):(b + 1) * seq] for b in range(batch)])[None]
    kvs = kv_state[batch * WINDOW:].reshape(dec_batch, t_new, 2 * KV_DIM)
    buf = cache_k.shape[2]
    k_new = kvs[..., :KV_DIM].reshape(dec_batch, t_new, N_KV_HEADS, HEAD_DIM)
    v_new = kvs[..., KV_DIM:].reshape(dec_batch, t_new, N_KV_HEADS, HEAD_DIM)
    new_k_sample = jnp.concatenate([cache_k[l], k_new], axis=1)[:, -buf:][None]
    new_v_sample = jnp.concatenate([cache_v[l], v_new], axis=1)[:, -buf:][None]
    us = u[n_prompt:].reshape(dec_batch, t_new, cdim)
    new_conv_sample = jnp.concatenate([state_conv[l], us], axis=1)[:, -(CONV_WIDTH - 1):][None]
    return (y_prompt, y_sample, new_k_prompt, new_v_prompt, new_conv_prompt,
            new_k_sample, new_v_sample, new_conv_sample)
```

```python
import functools
import math

import jax
import jax.numpy as jnp
from jax import lax
from jax.experimental import pallas as pl
from jax.experimental.pallas import tpu as pltpu

F32 = jnp.float32
BF16 = jnp.bfloat16

CHUNK = 64
HEAD_DIM = 64
N_Q_HEADS = 16
N_KV_HEADS = 4
Q_PER_KV = N_Q_HEADS // N_KV_HEADS
ATTN_DIM = N_Q_HEADS * HEAD_DIM
KV_DIM = N_KV_HEADS * HEAD_DIM
WINDOW = 128
WIN_CHUNKS = WINDOW // CHUNK
CONV_WIDTH = 31
NUM_BUCKETS = 32
MAX_DISTANCE = 128
N_GROUPS = 4
EXPERTS_PER_GROUP = 8
N_EXPERTS = N_GROUPS * EXPERTS_PER_GROUP
MOE_BLOCK = 128
RMS_EPS = 1e-6
LN_EPS = 1e-5
NEG_INF = -1e30

GROUP_ROWS = 32
LANES = 128
SUBLANES = 8
CONV_CHUNK = 256
ROUTER_ROWS = 64
EXPERT_ROW0 = 8

NORM_ROWS = 512
PROJ_ROWS = 1536
PROJ_COLS = 1024
TOKEN_TILE = 256
ATTN_PAIRS = 8
V7X_VMEM_LIMIT = 56 * 1024 * 1024


def _cparams(*sem):
    return pltpu.CompilerParams(dimension_semantics=sem, vmem_limit_bytes=V7X_VMEM_LIMIT)


def _ada_kernel(c_ref, w_ref, b_ref, o_ref):
    c = c_ref[...]
    s = c * jax.nn.sigmoid(c)
    o_ref[...] = jnp.dot(s.astype(BF16), w_ref[...].astype(BF16),
                         preferred_element_type=F32) + b_ref[...]


def _ada(c_all, w_ada, b_ada):
    rows, d = c_all.shape
    ncol = w_ada.shape[1]
    tn = 1024
    return pl.pallas_call(
        _ada_kernel,
        grid=(ncol // tn,),
        in_specs=[pl.BlockSpec((rows, d), lambda j: (0, 0)),
                  pl.BlockSpec((d, tn), lambda j: (0, j)),
                  pl.BlockSpec((1, tn), lambda j: (0, j))],
        out_specs=pl.BlockSpec((rows, tn), lambda j: (0, j)),
        out_shape=jax.ShapeDtypeStruct((rows, ncol), F32),
        compiler_params=_cparams("arbitrary"),
        name="ada",
    )(c_all, w_ada, b_ada.reshape(1, ncol))


def _rms_mod(x, gain, scale, shift):
    ms = jnp.mean(x * x, axis=-1, keepdims=True)
    y = x * lax.rsqrt(ms + RMS_EPS) * gain
    return y * (1.0 + scale) + shift


def _pack_bf16_pairs(xb):
    w = xb.shape[1] // 2
    bits = lax.bitcast_convert_type(xb.astype(F32), jnp.uint32)
    return (bits[:, :w] >> 16) | (bits[:, w:] & jnp.uint32(0xFFFF0000))


def _unpack_bf16_pairs(p):
    lo = lax.bitcast_convert_type(p << 16, F32)
    hi = lax.bitcast_convert_type(p & jnp.uint32(0xFFFF0000), F32)
    return lo, hi


def _two_stream_specs(tm, width, prompt_tiles):
    return [pl.BlockSpec((tm, width), lambda i: (jnp.minimum(i, prompt_tiles - 1), 0)),
            pl.BlockSpec((tm, width), lambda i: (jnp.maximum(i - prompt_tiles, 0), 0))]


def _norm1_kernel(xp_ref, xs_ref, g_ref, sc_ref, sh_ref, h_ref, *, groups, prompt_tiles):
    def run(x_ref):
        def body(gi, carry):
            r = pl.multiple_of(gi * GROUP_ROWS, GROUP_ROWS)
            h = _rms_mod(x_ref[pl.ds(r, GROUP_ROWS), :], g_ref[...],
                         sc_ref[pl.ds(gi, 1), :], sh_ref[pl.ds(gi, 1), :])
            h_ref[pl.ds(r, GROUP_ROWS), :] = h.astype(h_ref.dtype)
            return carry

        lax.fori_loop(0, groups, body, 0, unroll=4)

    i = pl.program_id(0)
    pl.when(i < prompt_tiles)(lambda: run(xp_ref))
    pl.when(i >= prompt_tiles)(lambda: run(xs_ref))


def _norm1(x_p, x_s, gain, scale_g, shift_g, tm):
    d = x_p.shape[1]
    n = x_p.shape[0] + x_s.shape[0]
    groups = tm // GROUP_ROWS
    prompt_tiles = x_p.shape[0] // tm
    return pl.pallas_call(
        functools.partial(_norm1_kernel, groups=groups, prompt_tiles=prompt_tiles),
        grid=(n // tm,),
        in_specs=_two_stream_specs(tm, d, prompt_tiles) + [
            pl.BlockSpec((1, d), lambda i: (0, 0)),
            pl.BlockSpec((groups, d), lambda i: (i, 0)),
            pl.BlockSpec((groups, d), lambda i: (i, 0))],
        out_specs=pl.BlockSpec((tm, d), lambda i: (i, 0)),
        out_shape=jax.ShapeDtypeStruct((n, d), BF16),
        compiler_params=_cparams("arbitrary"),
        name="norm1",
    )(x_p, x_s, gain.reshape(1, d), scale_g, shift_g)


def _proj_kernel(h_ref, w_ref, *o_refs, mode):
    acc = jnp.dot(h_ref[...], w_ref[...], preferred_element_type=F32)
    if mode == "qkv":
        o_refs[0][...] = acc.astype(BF16)
    elif mode == "glu":
        half = acc.shape[1] // 2
        o_refs[0][...] = acc[:, :half] * jax.nn.sigmoid(acc[:, half:])
    else:
        o_refs[0][...] = jax.nn.sigmoid(acc).astype(BF16)


def _proj(h, w, mode, tm, tnw, out_dtypes):
    n, d = h.shape
    ncol = w.shape[1]
    tno = tnw // 2 if mode == "glu" else tnw
    nout = ncol // 2 if mode == "glu" else ncol
    outs = pl.pallas_call(
        functools.partial(_proj_kernel, mode=mode),
        grid=(n // tm, ncol // tnw),
        in_specs=[pl.BlockSpec((tm, d), lambda i, j: (i, 0)),
                  pl.BlockSpec((d, tnw), lambda i, j: (0, j))],
        out_specs=[pl.BlockSpec((tm, tno), lambda i, j: (i, j)) for _ in out_dtypes],
        out_shape=[jax.ShapeDtypeStruct((n, nout), dt) for dt in out_dtypes],
        compiler_params=_cparams("arbitrary", "arbitrary"),
        name="proj_" + mode,
    )(h, w)
    return outs


def _kv_state_kernel(h_ref, w_ref, o_ref):
    o_ref[...] = jnp.dot(h_ref[...], w_ref[...], preferred_element_type=F32)


def _kv_state(h, w_kv, n_prompt, seq):
    n, d = h.shape
    batch = n_prompt // seq
    per_seq = seq // WINDOW
    steps = batch + (n - n_prompt) // WINDOW

    def row_block(i):
        return jnp.where(i < batch, (i + 1) * per_seq - 1, n_prompt // WINDOW + i - batch)

    return pl.pallas_call(
        _kv_state_kernel,
        grid=(steps,),
        in_specs=[pl.BlockSpec((WINDOW, d), lambda i: (row_block(i), 0)),
                  pl.BlockSpec((d, 2 * KV_DIM), lambda i: (0, 0))],
        out_specs=pl.BlockSpec((WINDOW, 2 * KV_DIM), lambda i: (i, 0)),
        out_shape=jax.ShapeDtypeStruct((steps * WINDOW, 2 * KV_DIM), F32),
        compiler_params=_cparams("arbitrary"),
        name="kv_state",
    )(h, w_kv)


def _t5_bucket(rel):
    nb = NUM_BUCKETS // 2
    n = -rel
    ret = jnp.where(n < 0, nb, 0)
    n = jnp.abs(n)
    max_exact = nb // 2
    nf = jnp.maximum(n, 1).astype(F32)
    large = max_exact + (jnp.log(nf / max_exact) / math.log(MAX_DISTANCE / max_exact)
                         * (nb - max_exact)).astype(jnp.int32)
    large = jnp.minimum(large, nb - 1)
    return ret + jnp.where(n < max_exact, n, large)


def _bias_kernel(tbl_ref, bkt_ref, o_ref):
    bkt = bkt_ref[...]
    nq = bkt.shape[0]
    for k in range(N_KV_HEADS):
        for g in range(Q_PER_KV):
            acc = jnp.zeros(bkt.shape, F32)
            for b in range(NUM_BUCKETS):
                acc = jnp.where(bkt == b, tbl_ref[b, k * Q_PER_KV + g], acc)
            o_ref[k, g * nq:(g + 1) * nq, :] = acc


def _rel_bias(rel_table, rel):
    nq, nk = rel.shape
    return pl.pallas_call(
        _bias_kernel,
        in_specs=[pl.BlockSpec(memory_space=pltpu.SMEM),
                  pl.BlockSpec((nq, nk), lambda: (0, 0))],
        out_specs=pl.BlockSpec((N_KV_HEADS, Q_PER_KV * nq, nk), lambda: (0, 0, 0)),
        out_shape=jax.ShapeDtypeStruct((N_KV_HEADS, Q_PER_KV * nq, nk), F32),
        name="rel_bias",
    )(rel_table.astype(F32), _t5_bucket(rel))


PAIR_BAND = 2 * CHUNK + WINDOW
HEADS_PER_COL = LANES // HEAD_DIM
COLS_PER_KV = Q_PER_KV // HEADS_PER_COL


def _pair_bias_kernel(tbl_ref, bkt_ref, o_ref):
    for e in range(2):
        bkt = bkt_ref[e]
        for k in range(N_KV_HEADS):
            for col in range(COLS_PER_KV):
                for half in range(HEADS_PER_COL):
                    head = k * Q_PER_KV + col * HEADS_PER_COL + half
                    acc = jnp.full(bkt.shape, NEG_INF, F32)
                    for b in range(NUM_BUCKETS):
                        acc = jnp.where(bkt == b, tbl_ref[b, head], acc)
                    r0 = (e * COLS_PER_KV + col) * CHUNK
                    o_ref[k, r0:r0 + CHUNK, half * PAIR_BAND:(half + 1) * PAIR_BAND] = acc


def _pair_bias(rel_table):
    qi = jnp.arange(CHUNK, dtype=jnp.int32)[None, :, None]
    kj = jnp.arange(PAIR_BAND, dtype=jnp.int32)[None, None, :]
    e = jnp.arange(2, dtype=jnp.int32)[:, None, None]
    rel = (kj - WINDOW) - (e * CHUNK + qi)
    key_chunk = kj // CHUNK - e
    seen = (key_chunk >= 0) & (key_chunk <= WIN_CHUNKS)
    bkt = jnp.where(seen, _t5_bucket(rel), -1)
    rows = 2 * COLS_PER_KV * CHUNK
    return pl.pallas_call(
        _pair_bias_kernel,
        in_specs=[pl.BlockSpec(memory_space=pltpu.SMEM),
                  pl.BlockSpec((2, CHUNK, PAIR_BAND), lambda: (0, 0, 0))],
        out_specs=pl.BlockSpec((N_KV_HEADS, rows, HEADS_PER_COL * PAIR_BAND), lambda: (0, 0, 0)),
        out_shape=jax.ShapeDtypeStruct((N_KV_HEADS, rows, HEADS_PER_COL * PAIR_BAND), F32),
        name="pair_bias",
    )(rel_table.astype(F32), bkt)


def _attend(qc, kk_all, vv_all, bias_ref, sink_ref, mask_thr):
    nq = qc.shape[0]
    nk = kk_all.shape[0]
    pieces = []
    for k in range(N_KV_HEADS):
        qs = jnp.concatenate(
            [qc[:, (k * Q_PER_KV + g) * HEAD_DIM:(k * Q_PER_KV + g + 1) * HEAD_DIM]
             for g in range(Q_PER_KV)], axis=0)
        kk = kk_all[:, k * HEAD_DIM:(k + 1) * HEAD_DIM]
        vv = vv_all[:, k * HEAD_DIM:(k + 1) * HEAD_DIM]
        lg = lax.dot_general(qs, kk, (((1,), (1,)), ((), ())),
                             preferred_element_type=F32) * (HEAD_DIM ** -0.5) + bias_ref[k]
        if mask_thr is not None:
            col = lax.broadcasted_iota(jnp.int32, (Q_PER_KV * nq, nk), 1)
            lg = jnp.where(col < mask_thr, NEG_INF, lg)
        s = sink_ref[k]
        m = jnp.maximum(jnp.max(lg, axis=-1, keepdims=True), s)
        p = jnp.exp(lg - m)
        den = jnp.sum(p, axis=-1, keepdims=True) + jnp.exp(s - m)
        o = jnp.dot(p.astype(BF16), vv, preferred_element_type=F32) / den
        pieces.extend(o[g * nq:(g + 1) * nq, :] for g in range(Q_PER_KV))
    return jnp.concatenate(pieces, axis=1)


def _attn_prompt_kernel(q_ref, kv_ref, halo_ref, bias_ref, sink_ref, o_ref, ka, kb, va, vb, *, pairs):
    t = pl.program_id(1)
    kvcat = jnp.concatenate([halo_ref[...], kv_ref[...]], axis=0)
    low = lax.broadcasted_iota(jnp.int32, (kvcat.shape[0], LANES), 1) < HEAD_DIM
    zero = jnp.zeros((kvcat.shape[0], LANES), BF16)
    ones_low = jnp.where(low, 1.0, 0.0).astype(BF16)
    ones_high = jnp.where(low, 0.0, 1.0).astype(BF16)
    kv_cols = KV_DIM // LANES
    for col in range(2 * kv_cols):
        x = kvcat[:, col * LANES:(col + 1) * LANES]
        xs = jnp.concatenate([x[:, HEAD_DIM:], x[:, :HEAD_DIM]], axis=1)
        dst_a, dst_b = (ka, kb) if col < kv_cols else (va, vb)
        k0 = (col % kv_cols) * HEADS_PER_COL
        dst_a[k0, :, 0:LANES] = jnp.where(low, x, zero)
        dst_b[k0, :, 0:LANES] = jnp.where(low, zero, xs)
        dst_a[k0 + 1, :, 0:LANES] = jnp.where(low, xs, zero)
        dst_b[k0 + 1, :, 0:LANES] = jnp.where(low, zero, x)
    for k in range(N_KV_HEADS):
        va[k, :, LANES:2 * LANES] = ones_low
        vb[k, :, LANES:2 * LANES] = ones_high

    out_low = lax.broadcasted_iota(jnp.int32, (2 * COLS_PER_KV * CHUNK, LANES), 1) < HEAD_DIM
    for p in range(pairs):
        r0 = p * 2 * CHUNK
        for k in range(N_KV_HEADS):
            lhs = jnp.concatenate(
                [q_ref[r0 + e * CHUNK:r0 + (e + 1) * CHUNK, (k * COLS_PER_KV + c) * LANES:(k * COLS_PER_KV + c + 1) * LANES]
                 for e in range(2) for c in range(COLS_PER_KV)], axis=0)
            keys = jnp.concatenate([ka[k, r0:r0 + PAIR_BAND, :], kb[k, r0:r0 + PAIR_BAND, :]], axis=0)
            lg = lax.dot_general(lhs, keys, (((1,), (1,)), ((), ())),
                                 preferred_element_type=F32) * (HEAD_DIM ** -0.5) + bias_ref[k]
            probs, sink_terms = [], []
            for half in range(HEADS_PER_COL):
                seg = lg[:, half * PAIR_BAND:(half + 1) * PAIR_BAND]
                if p == 0:
                    kcol = lax.broadcasted_iota(jnp.int32, seg.shape, 1)
                    seg = jnp.where(kcol < jnp.where(t == 0, WINDOW, 0), NEG_INF, seg)
                s = sink_ref[k, half]
                folded = seg[:, :LANES]
                for j in range(1, PAIR_BAND // LANES):
                    folded = jnp.maximum(folded, seg[:, j * LANES:(j + 1) * LANES])
                m = jnp.maximum(jnp.max(folded, axis=-1, keepdims=True), s)
                pr = jnp.exp(seg - jnp.concatenate([m] * (PAIR_BAND // LANES), axis=1))
                sink_terms.append(jnp.exp(s - m))
                probs.append(pr.astype(BF16))
            vals = jnp.concatenate([va[k, r0:r0 + PAIR_BAND, :], vb[k, r0:r0 + PAIR_BAND, :]], axis=0)
            oe = jnp.dot(jnp.concatenate(probs, axis=1), vals, preferred_element_type=F32)
            o = oe[:, :LANES] / (oe[:, LANES:] + jnp.where(out_low, sink_terms[0], sink_terms[1]))
            for e in range(2):
                for c in range(COLS_PER_KV):
                    rr = (e * COLS_PER_KV + c) * CHUNK
                    o_ref[r0 + e * CHUNK:r0 + (e + 1) * CHUNK,
                          (k * COLS_PER_KV + c) * LANES:(k * COLS_PER_KV + c + 1) * LANES] = (
                        o[rr:rr + CHUNK, :].astype(o_ref.dtype))


def _attn_prompt(q, kv, bias, sink_rows, n_prompt, seq, pairs):
    rows = pairs * 2 * CHUNK
    tiles = seq // rows
    halo_per_tile = rows // WINDOW
    keys = WINDOW + rows
    return pl.pallas_call(
        functools.partial(_attn_prompt_kernel, pairs=pairs),
        grid=(n_prompt // seq, tiles),
        scratch_shapes=[pltpu.VMEM((N_KV_HEADS, keys, LANES), BF16) for _ in range(2)]
        + [pltpu.VMEM((N_KV_HEADS, keys, 2 * LANES), BF16) for _ in range(2)],
        in_specs=[pl.BlockSpec((rows, ATTN_DIM), lambda b, t: (b * tiles + t, 0)),
                  pl.BlockSpec((rows, 2 * KV_DIM), lambda b, t: (b * tiles + t, 0)),
                  pl.BlockSpec((WINDOW, 2 * KV_DIM),
                               lambda b, t: (jnp.maximum((b * tiles + t) * halo_per_tile - 1, 0), 0)),
                  pl.BlockSpec(bias.shape, lambda b, t: (0, 0, 0)),
                  pl.BlockSpec(sink_rows.shape, lambda b, t: (0, 0, 0, 0))],
        out_specs=pl.BlockSpec((rows, ATTN_DIM), lambda b, t: (b * tiles + t, 0)),
        out_shape=jax.ShapeDtypeStruct((n_prompt, ATTN_DIM), BF16),
        compiler_params=_cparams("arbitrary", "arbitrary"),
        name="attn_prompt",
    )(q, kv, kv, bias, sink_rows)


def _attn_sample_kernel(q_ref, kv_ref, ck_ref, cv_ref, bias_ref, sink_ref, o_ref):
    kv = kv_ref[...]
    kk = jnp.concatenate([ck_ref[0].astype(BF16), kv[:, :KV_DIM]], axis=0)
    vv = jnp.concatenate([cv_ref[0].astype(BF16), kv[:, KV_DIM:]], axis=0)
    o = _attend(q_ref[...], kk, vv, bias_ref, sink_ref, None)
    o_ref[...] = o.astype(o_ref.dtype)


def _attn_sample(q, kv, cache_k, cache_v, bias, sink_rows, n_prompt, t_new):
    dec_batch, buf, _ = cache_k.shape
    first = n_prompt // t_new
    return pl.pallas_call(
        _attn_sample_kernel,
        grid=(dec_batch,),
        in_specs=[pl.BlockSpec((t_new, ATTN_DIM), lambda s: (first + s, 0)),
                  pl.BlockSpec((t_new, 2 * KV_DIM), lambda s: (first + s, 0)),
                  pl.BlockSpec((1, buf, KV_DIM), lambda s: (s, 0, 0)),
                  pl.BlockSpec((1, buf, KV_DIM), lambda s: (s, 0, 0)),
                  pl.BlockSpec(bias.shape, lambda s: (0, 0, 0)),
                  pl.BlockSpec(sink_rows.shape, lambda s: (0, 0, 0))],
        out_specs=pl.BlockSpec((t_new, ATTN_DIM), lambda s: (s, 0)),
        out_shape=jax.ShapeDtypeStruct((dec_batch * t_new, ATTN_DIM), BF16),
        compiler_params=_cparams("arbitrary"),
        name="attn_sample",
    )(q, kv, cache_k, cache_v, bias, sink_rows)


def _route(lt, t, counted, eidx_ref, wtok_ref, rank_ref, cnt_ref, cnt_scr):
    gl = [lt[r:r + 1, :] for r in range(N_GROUPS)]
    gmax = gl[0]
    gsel = jnp.zeros(gl[0].shape, jnp.int32)
    for r in range(1, N_GROUPS):
        better = gl[r] > gmax
        gsel = jnp.where(better, r, gsel)
        gmax = jnp.maximum(gmax, gl[r])
    gexp = [jnp.exp(v - gmax) for v in gl]
    gsum = gexp[0]
    for r in range(1, N_GROUPS):
        gsum = gsum + gexp[r]
    psel = jnp.zeros(gl[0].shape, F32)
    for r in range(N_GROUPS):
        psel = jnp.where(gsel == r, gexp[r] / gsum, psel)
    el = jnp.zeros((EXPERTS_PER_GROUP, lt.shape[1]), F32)
    for r in range(N_GROUPS):
        lo = EXPERT_ROW0 + r * EXPERTS_PER_GROUP
        el = jnp.where(gsel == r, lt[lo:lo + EXPERTS_PER_GROUP, :], el)
    emax = jnp.max(el, axis=0, keepdims=True)
    ee = jnp.exp(el - emax)
    pin = ee / jnp.sum(ee, axis=0, keepdims=True)
    idx = lax.broadcasted_iota(jnp.int32, pin.shape, 0)
    p1 = jnp.max(pin, axis=0, keepdims=True)
    i1 = jnp.min(jnp.where(pin == p1, idx, EXPERTS_PER_GROUP), axis=0, keepdims=True)
    rest = jnp.where(idx == i1, -1.0, pin)
    p2 = jnp.max(rest, axis=0, keepdims=True)
    i2 = jnp.min(jnp.where(rest == p2, idx, EXPERTS_PER_GROUP), axis=0, keepdims=True)
    tot = p1 + p2
    e1 = gsel * EXPERTS_PER_GROUP + i1
    e2 = gsel * EXPERTS_PER_GROUP + i2
    eidx_ref[t, 0:1, :] = e1
    eidx_ref[t, 1:2, :] = e2
    wtok_ref[t, 0:1, :] = psel * p1 / tot
    wtok_ref[t, 1:2, :] = psel * p2 / tot

    rows = lt.shape[1]
    eiota = lax.broadcasted_iota(jnp.int32, (N_EXPERTS, rows), 0)
    oh1 = (eiota == e1).astype(F32)
    oh2 = (eiota == e2).astype(F32)
    both = oh1 + oh2
    before = (lax.broadcasted_iota(jnp.int32, (rows, rows), 0)
              < lax.broadcasted_iota(jnp.int32, (rows, rows), 1)).astype(BF16)
    prior = jnp.dot(both.astype(BF16), before, preferred_element_type=F32) + cnt_scr[:, 0:1]
    rank_ref[t, 0:1, :] = jnp.sum(oh1 * prior, axis=0, keepdims=True).astype(jnp.int32)
    rank_ref[t, 1:2, :] = jnp.sum(oh2 * prior, axis=0, keepdims=True).astype(jnp.int32)
    cnt_scr[...] = cnt_scr[...] + jnp.where(counted, jnp.sum(both, axis=1, keepdims=True), 0.0)
    cnt_ref[...] = cnt_scr[...]


def _post_kernel(xp_ref, xs_ref, u_ref, uhalo_ref, sconv_ref, op_ref, os_ref, gates_ref,
                 gate1_ref, shift2_ref, scale2_ref,
                 cw_ref, cb_ref, lng_ref, lnb_ref, wpw2_ref, bpw2_ref, wao_ref, wout_ref,
                 n2_ref, wrt_ref, brt_ref,
                 x1_ref, h2_ref, eidx_ref, wtok_ref, rank_ref, cnt_ref,
                 uext, shift_scr, d_scr, s_scr, o_scr, mix_scr, h2b_scr, cnt_scr,
                 *, groups, prompt_tiles, tiles_per_seq, tiles):
    i = pl.program_id(0)
    d_model = xp_ref.shape[1]

    @pl.when(i == 0)
    def _():
        cnt_scr[...] = jnp.zeros(cnt_scr.shape, F32)
        h2b_scr[...] = jnp.zeros(h2b_scr.shape, BF16)
    nch = cw_ref.shape[0]
    cw = uext.shape[2]
    hist = uext.shape[1] - GROUP_ROWS
    lead = hist - (CONV_WIDTH - 1)

    def put_hist(g, rows):
        for c in range(nch):
            uext[g * nch + c, 0:hist, :] = rows[:, c * cw:(c + 1) * cw]

    @pl.when(i < prompt_tiles)
    def _():
        first = (i % tiles_per_seq) == 0
        put_hist(0, jnp.where(first, 0.0, uhalo_ref[...]))
        for g in range(1, groups):
            put_hist(g, u_ref[(g - 1) * GROUP_ROWS:g * GROUP_ROWS, :])
        o_scr[...] = op_ref[...]

    @pl.when(i >= prompt_tiles)
    def _():
        for g in range(groups):
            put_hist(g, sconv_ref[g])
        o_scr[...] = os_ref[...]

    for g in range(groups):
        for c in range(nch):
            uext[g * nch + c, hist:hist + GROUP_ROWS, :] = (
                u_ref[g * GROUP_ROWS:(g + 1) * GROUP_ROWS, c * cw:(c + 1) * cw])

    span = shift_scr.shape[1]

    def conv_chunk(k, carry):
        c = k % nch
        win_all = uext[k]
        for s in range(1, SUBLANES):
            shift_scr[s, :, :] = pltpu.roll(win_all, win_all.shape[0] - s, 0)[0:span, :]
        acc = None
        for j in range(CONV_WIDTH):
            base, s = divmod(lead + j, SUBLANES)
            rows = pl.ds(base * SUBLANES, GROUP_ROWS)
            win = uext[k, rows, :] if s == 0 else shift_scr[s, rows, :]
            term = win * cw_ref[c, j:j + 1, :]
            acc = term if acc is None else acc + term
        d_scr[k] = acc + cb_ref[c]
        return carry

    lax.fori_loop(0, groups * nch, conv_chunk, 0)

    for g in range(groups):
        dd = jnp.concatenate([d_scr[g * nch + c] for c in range(nch)], axis=1)
        mu = jnp.mean(dd, axis=-1, keepdims=True)
        var = jnp.mean(jnp.square(dd - mu), axis=-1, keepdims=True)
        y = (dd - mu) * lax.rsqrt(var + LN_EPS) * lng_ref[...] + lnb_ref[...]
        s_scr[g * GROUP_ROWS:(g + 1) * GROUP_ROWS, :] = (y * jax.nn.sigmoid(y)).astype(BF16)

    def route_tile(t, counted):
        lt = lax.dot_general(wrt_ref[...], h2b_scr[...], (((1,), (1,)), ((), ())),
                             preferred_element_type=F32) + brt_ref[...]
        _route(lt, t, counted, eidx_ref, wtok_ref, rank_ref, cnt_ref, cnt_scr)

    route_tile(jnp.maximum(i - 1, 0), i > 0)

    conv_out = jnp.dot(s_scr[...], wpw2_ref[...], preferred_element_type=F32) + bpw2_ref[...]
    attn_out = jnp.dot(o_scr[...], wao_ref[...], preferred_element_type=F32)
    merged = (gates_ref[:, :d_model].astype(F32) * attn_out
              + gates_ref[:, d_model:].astype(F32) * conv_out)
    mix_scr[...] = jnp.dot(merged.astype(BF16), wout_ref[...], preferred_element_type=F32)

    def residual(x_ref):
        def res_group(g, carry):
            r = pl.multiple_of(g * GROUP_ROWS, GROUP_ROWS)
            x1 = (x_ref[pl.ds(r, GROUP_ROWS), :]
                  + gate1_ref[pl.ds(g, 1), :] * mix_scr[pl.ds(r, GROUP_ROWS), :])
            x1_ref[pl.ds(r, GROUP_ROWS), :] = x1
            h = _rms_mod(x1, n2_ref[...], scale2_ref[pl.ds(g, 1), :], shift2_ref[pl.ds(g, 1), :])
            hb = h.astype(BF16)
            h2b_scr[pl.ds(r, GROUP_ROWS), :] = hb
            h2_ref[pl.ds(r, GROUP_ROWS), :] = _pack_bf16_pairs(hb)
            return carry

        lax.fori_loop(0, groups, res_group, 0, unroll=4)

    pl.when(i < prompt_tiles)(lambda: residual(xp_ref))
    pl.when(i >= prompt_tiles)(lambda: residual(xs_ref))

    @pl.when(i == tiles - 1)
    def _():
        route_tile(i, True)


def _post(x_p, x_s, u, sconv_pad, o_p, o_s, gates, gate1_g, shift2_g, scale2_g, p, n_prompt, seq, tm):
    d = x_p.shape[1]
    n = x_p.shape[0] + x_s.shape[0]
    cdim = u.shape[1]
    nch = cdim // CONV_CHUNK
    groups = tm // GROUP_ROWS
    tiles = n // tm
    prompt_tiles = n_prompt // tm
    const = lambda shape: pl.BlockSpec(shape, lambda i: (0,) * len(shape))
    row = lambda w: pl.BlockSpec((tm, w), lambda i: (i, 0))
    grp = pl.BlockSpec((groups, d), lambda i: (i, 0))
    outs = pl.pallas_call(
        functools.partial(_post_kernel, groups=groups, prompt_tiles=prompt_tiles,
                          tiles_per_seq=seq // tm, tiles=tiles),
        grid=(tiles,),
        in_specs=_two_stream_specs(tm, d, prompt_tiles) + [
                  row(cdim),
                  pl.BlockSpec((GROUP_ROWS, cdim), lambda i: (jnp.maximum(i * groups - 1, 0), 0)),
                  pl.BlockSpec((groups, GROUP_ROWS, cdim),
                               lambda i: (jnp.maximum(i - prompt_tiles, 0), 0, 0))]
                 + _two_stream_specs(tm, ATTN_DIM, prompt_tiles) + [
                  row(2 * d), grp, grp, grp,
                  const((nch, CONV_WIDTH, CONV_CHUNK)), const((nch, 1, CONV_CHUNK)),
                  const((1, cdim)), const((1, cdim)),
                  const((cdim, d)), const((1, d)), const((ATTN_DIM, d)), const((d, d)),
                  const((1, d)), const((ROUTER_ROWS, d)), const((ROUTER_ROWS, 1))],
        out_specs=[row(d),
                   row(d // 2),
                   const((tiles, 2, tm)), const((tiles, 2, tm)), const((tiles, 2, tm)),
                   pl.BlockSpec((N_EXPERTS, LANES), lambda i: (0, 0))],
        out_shape=[jax.ShapeDtypeStruct((n, d), F32),
                   jax.ShapeDtypeStruct((n, d // 2), jnp.uint32),
                   jax.ShapeDtypeStruct((tiles, 2, tm), jnp.int32),
                   jax.ShapeDtypeStruct((tiles, 2, tm), F32),
                   jax.ShapeDtypeStruct((tiles, 2, tm), jnp.int32),
                   jax.ShapeDtypeStruct((N_EXPERTS, LANES), F32)],
        scratch_shapes=[pltpu.VMEM((groups * nch, 2 * GROUP_ROWS, CONV_CHUNK), F32),
                        pltpu.VMEM((SUBLANES, 2 * GROUP_ROWS - SUBLANES, CONV_CHUNK), F32),
                        pltpu.VMEM((groups * nch, GROUP_ROWS, CONV_CHUNK), F32),
                        pltpu.VMEM((tm, cdim), BF16),
                        pltpu.VMEM((tm, ATTN_DIM), BF16),
                        pltpu.VMEM((tm, d), F32),
                        pltpu.VMEM((tm, d), BF16),
                        pltpu.VMEM((N_EXPERTS, LANES), F32)],
        compiler_params=_cparams("arbitrary"),
        name="post",
    )(x_p, x_s, u, u, sconv_pad, o_p, o_s, gates, gate1_g, shift2_g, scale2_g,
      p["conv_w"], p["conv_b"], p["ln_g"], p["ln_b"], p["w_pw2"], p["b_pw2"], p["w_ao"], p["w_out"],
      p["n2"], p["w_rt"], p["b_rt"])
    x1, h2, eidx, wtok, rank, cnt = outs
    per_slot = lambda a: a.transpose(1, 0, 2).reshape(2, n)
    return x1, h2, per_slot(eidx), per_slot(wtok), per_slot(rank), cnt


TABLE_GROUP = 8
GATHER_AHEAD = 8
FINAL_AHEAD = 1


def _grouped_tables(tab, group=TABLE_GROUP, ahead=1):
    steps, w = tab.shape
    assert steps % group == 0
    padded = jnp.concatenate([tab] + [tab[-1:]] * ahead, axis=0)
    cols = [tab.reshape(steps // group, group * w)] + [padded[group + a::group][:steps // group]
                                                         for a in range(ahead)]
    return jnp.concatenate(cols, axis=1).reshape(steps // group, 1, (group + ahead) * w)

def _moe_kernel(blk_e_ref, next_e_ref, nused_ref, src_ref, roww_ref,
                h2_hbm, wg_hbm, wu_hbm, wd_hbm, y_ref,
                xbuf, x16, wg_f, wu_f, wd_f, wg_b, wu_b, wd_b, gsem, wsem):
    i = pl.program_id(0)
    nused = nused_ref[0]
    nslots = GATHER_AHEAD + 1
    slot = i % nslots
    base = (i % TABLE_GROUP) * MOE_BLOCK
    weights = ((wg_hbm, wg_f, wg_b), (wu_hbm, wu_f, wu_b), (wd_hbm, wd_f, wd_b))

    def gather(first, sl):
        for r in range(MOE_BLOCK):
            pltpu.make_async_copy(h2_hbm.at[pl.ds(src_ref[0, 0, first + r], 1), :],
                                  xbuf.at[sl, pl.ds(r, 1), :], gsem.at[sl]).start(priority=0)

    def gather_wait(sl):
        pltpu.make_async_copy(h2_hbm.at[pl.ds(0, MOE_BLOCK), :], xbuf.at[sl], gsem.at[sl]).wait()

    def weights_start(e):
        for hbm, f32_buf, _ in weights:
            pltpu.make_async_copy(hbm.at[e], f32_buf, wsem.at[0]).start(priority=1)

    def weights_wait():
        for hbm, f32_buf, _ in weights:
            pltpu.make_async_copy(hbm.at[0], f32_buf, wsem.at[0]).wait()

    @pl.when(i == 0)
    def _():
        weights_start(blk_e_ref[0])
        for a in range(GATHER_AHEAD):
            gather(base + a * MOE_BLOCK, a)

    @pl.when(i >= nused)
    def _():
        y_ref[...] = jnp.zeros(y_ref.shape, y_ref.dtype)

    @pl.when(i < nused)
    def _():
        changed = jnp.logical_or(i == 0, blk_e_ref[i] != blk_e_ref[jnp.maximum(i - 1, 0)])

        @pl.when(changed)
        def _():
            weights_wait()
            for _, f32_buf, b16_buf in weights:
                b16_buf[...] = f32_buf[...].astype(BF16)
            weights_start(next_e_ref[i])

        gather_wait(slot)
        x_lo, x_hi = _unpack_bf16_pairs(xbuf[slot])
        x16[:, :x_lo.shape[1]] = x_lo.astype(BF16)
        x16[:, x_lo.shape[1]:] = x_hi.astype(BF16)

        gather(base + GATHER_AHEAD * MOE_BLOCK, (i + GATHER_AHEAD) % nslots)
        x = x16[...]
        hg = jnp.dot(x, wg_b[...], preferred_element_type=F32)
        hu = jnp.dot(x, wu_b[...], preferred_element_type=F32)
        hid = (hg * jax.nn.sigmoid(hg) * hu).astype(BF16)
        y = jnp.dot(hid, wd_b[...], preferred_element_type=F32) * roww_ref[...]
        y_ref[...] = _pack_bf16_pairs(y.astype(BF16))

        @pl.when(i == nused - 1)
        def _():
            weights_wait()
            for a in range(1, GATHER_AHEAD + 1):
                gather_wait((i + a) % nslots)


def _moe(h2, blk_e, next_e, nused, src, roww, w_gate, w_up, w_down):
    n_blocks = blk_e.shape[0]
    n_exp, d, de = w_gate.shape
    src_tables = _grouped_tables(src.reshape(n_blocks, MOE_BLOCK), ahead=GATHER_AHEAD)
    grid_spec = pltpu.PrefetchScalarGridSpec(
        num_scalar_prefetch=3,
        grid=(n_blocks,),
        in_specs=[pl.BlockSpec((1, 1, src_tables.shape[2]), lambda i, be, ne, nu: (i // TABLE_GROUP, 0, 0),
                               memory_space=pltpu.SMEM),
                  pl.BlockSpec((MOE_BLOCK, 1), lambda i, be, ne, nu: (i, 0)),
                  pl.BlockSpec(memory_space=pl.ANY),
                  pl.BlockSpec(memory_space=pl.ANY),
                  pl.BlockSpec(memory_space=pl.ANY),
                  pl.BlockSpec(memory_space=pl.ANY)],
        out_specs=pl.BlockSpec((MOE_BLOCK, d // 2), lambda i, be, ne, nu: (i, 0)),
        scratch_shapes=[pltpu.VMEM((GATHER_AHEAD + 1, MOE_BLOCK, d // 2), jnp.uint32),
                        pltpu.VMEM((MOE_BLOCK, d), BF16),
                        pltpu.VMEM((d, de), F32), pltpu.VMEM((d, de), F32), pltpu.VMEM((de, d), F32),
                        pltpu.VMEM((d, de), BF16), pltpu.VMEM((d, de), BF16), pltpu.VMEM((de, d), BF16),
                        pltpu.SemaphoreType.DMA((GATHER_AHEAD + 1,)), pltpu.SemaphoreType.DMA((1,))],
    )
    return pl.pallas_call(
        _moe_kernel,
        grid_spec=grid_spec,
        out_shape=jax.ShapeDtypeStruct((n_blocks * MOE_BLOCK, d // 2), jnp.uint32),
        compiler_params=_cparams("arbitrary"),
        name="moe",
    )(blk_e, next_e, nused, src_tables, roww, h2, w_gate, w_up, w_down)


def _dispatch(eidx, wtok, rank, counts, n):
    a_tot = 2 * n
    experts = jnp.arange(N_EXPERTS, dtype=jnp.int32)
    padded = (counts + MOE_BLOCK - 1) // MOE_BLOCK * MOE_BLOCK
    pad_end = jnp.sum(jnp.where(experts[None, :] <= experts[:, None], padded[None, :], 0), axis=1)
    pad_start = pad_end - padded
    dest = jnp.sum(jnp.where(eidx[:, :, None] == experts, pad_start, 0), axis=-1) + rank
    n_blocks = -(-(a_tot + N_EXPERTS * (MOE_BLOCK - 1)) // MOE_BLOCK)
    n_blocks = -(-n_blocks // TABLE_GROUP) * TABLE_GROUP
    n_rows = n_blocks * MOE_BLOCK
    tok = jnp.tile(jnp.arange(n, dtype=jnp.int32), 2)
    upd = jnp.stack([tok, lax.bitcast_convert_type(wtok.reshape(-1), jnp.int32)], axis=1)
    rows = jnp.zeros((n_rows, 2), jnp.int32).at[dest.reshape(-1)].set(
        upd, unique_indices=True, mode="promise_in_bounds")
    blk_start = jnp.arange(n_blocks, dtype=jnp.int32) * MOE_BLOCK
    blk_e = jnp.minimum(jnp.sum((pad_end[None, :] <= blk_start[:, None]).astype(jnp.int32), axis=1),
                        N_EXPERTS - 1)
    nused = (pad_end[-1] // MOE_BLOCK).reshape(1)
    roww = lax.bitcast_convert_type(rows[:, 1], F32)
    later = jnp.where((counts > 0)[None, :] & (experts[None, :] > experts[:, None]), experts[None, :], N_EXPERTS)
    next_of = jnp.min(later, axis=1)
    next_of = jnp.where(next_of < N_EXPERTS, next_of, experts)
    next_e = jnp.sum(jnp.where(blk_e[:, None] == experts[None, :], next_of[None, :], 0), axis=1)
    return blk_e, next_e, nused, rows[:, 0].reshape(n_blocks, 1, MOE_BLOCK), roww.reshape(n_rows, 1), dest


def _final_kernel(d0_ref, d1_ref, x1_ref, gate2_ref, fg_ref, yb_hbm,
                  op_ref, os_ref, ybuf, sem, *, groups, prompt_tiles, tiles, table_group):
    i = pl.program_id(0)
    nslots = FINAL_AHEAD + 1
    slot = i % nslots
    tm = x1_ref.shape[0]
    base = (i % table_group) * tm

    def gather(first, sl):
        for r in range(tm):
            pltpu.make_async_copy(yb_hbm.at[pl.ds(d0_ref[0, 0, first + r], 1), :],
                                  ybuf.at[sl, 0, pl.ds(r, 1), :], sem.at[sl]).start(priority=0)
            pltpu.make_async_copy(yb_hbm.at[pl.ds(d1_ref[0, 0, first + r], 1), :],
                                  ybuf.at[sl, 1, pl.ds(r, 1), :], sem.at[sl]).start(priority=1)

    def gather_wait(sl):
        for k in range(2):
            pltpu.make_async_copy(yb_hbm.at[pl.ds(0, tm), :], ybuf.at[sl, k], sem.at[sl]).wait()

    @pl.when(i == 0)
    def _():
        for a in range(FINAL_AHEAD):
            gather(base + a * tm, a)

    gather_wait(slot)
    gather(base + FINAL_AHEAD * tm, (i + FINAL_AHEAD) % nslots)

    def run(o_ref):
        def body(g, carry):
            r = pl.multiple_of(g * GROUP_ROWS, GROUP_ROWS)
            lo0, hi0 = _unpack_bf16_pairs(ybuf[slot, 0, pl.ds(r, GROUP_ROWS), :])
            lo1, hi1 = _unpack_bf16_pairs(ybuf[slot, 1, pl.ds(r, GROUP_ROWS), :])
            moe = jnp.concatenate([lo0 + lo1, hi0 + hi1], axis=1)
            x2 = x1_ref[pl.ds(r, GROUP_ROWS), :] + gate2_ref[pl.ds(g, 1), :] * moe
            ms = jnp.mean(x2 * x2, axis=-1, keepdims=True)
            o_ref[pl.ds(r, GROUP_ROWS), :] = x2 * lax.rsqrt(ms + RMS_EPS) * fg_ref[...]
            return carry

        lax.fori_loop(0, groups, body, 0, unroll=2)

    pl.when(i < prompt_tiles)(lambda: run(op_ref))
    pl.when(i >= prompt_tiles)(lambda: run(os_ref))

    @pl.when(i == tiles - 1)
    def _():
        for a in range(1, FINAL_AHEAD + 1):
            gather_wait((i + a) % nslots)


def _final(x1, yb, dest, gate2_g, final_g, n_prompt, tm):
    n, d = x1.shape
    groups = tm // GROUP_ROWS
    tiles = n // tm
    prompt_tiles = n_prompt // tm
    table_group = max(g for g in range(1, TABLE_GROUP + 1) if tiles % g == 0)
    tables = [_grouped_tables(dest[k].reshape(tiles, tm), table_group, FINAL_AHEAD) for k in range(2)]
    idx_spec = pl.BlockSpec((1, 1, tables[0].shape[2]), lambda i: (i // table_group, 0, 0),
                            memory_space=pltpu.SMEM)
    return pl.pallas_call(
        functools.partial(_final_kernel, groups=groups, prompt_tiles=prompt_tiles, tiles=tiles,
                          table_group=table_group),
        grid=(tiles,),
        in_specs=[idx_spec, idx_spec,
                  pl.BlockSpec((tm, d), lambda i: (i, 0)),
                  pl.BlockSpec((groups, d), lambda i: (i, 0)),
                  pl.BlockSpec((1, d), lambda i: (0, 0)),
                  pl.BlockSpec(memory_space=pl.ANY)],
        out_specs=_two_stream_specs(tm, d, prompt_tiles),
        out_shape=[jax.ShapeDtypeStruct((n_prompt, d), F32),
                   jax.ShapeDtypeStruct((n - n_prompt, d), F32)],
        scratch_shapes=[pltpu.VMEM((FINAL_AHEAD + 1, 2, tm, d // 2), jnp.uint32),
                        pltpu.SemaphoreType.DMA((FINAL_AHEAD + 1,))],
        compiler_params=_cparams("arbitrary"),
        name="final",
    )(tables[0], tables[1], x1, gate2_g, final_g.reshape(1, d), yb)


def _layer(x_p, x_s, c_all, cache_k, cache_v, sconv, rel_table, lp, seq, t_new):
    (n1, n2, w_ada, b_ada, w_in, sink, w_ao, conv_w, conv_b, ln_g, ln_b, w_pw2, b_pw2, w_out,
     w_grp, b_grp, w_rt, b_rt, w_gate, w_up, w_down) = lp
    n_prompt, d = x_p.shape
    n = n_prompt + x_s.shape[0]
    batch = n_prompt // seq
    dec_batch = (n - n_prompt) // t_new
    cdim = conv_w.shape[1]
    tm = TOKEN_TILE

    c_rows = -(-c_all.shape[0] // 8) * 8
    c_pad = jnp.pad(c_all, ((0, c_rows - c_all.shape[0]), (0, 0)))
    mods = _ada(c_pad, w_ada, b_ada)

    def per_group(k):
        m = mods[:, k * d:(k + 1) * d]
        mp = jnp.broadcast_to(m[:batch, None, :], (batch, seq // GROUP_ROWS, d))
        return jnp.concatenate([mp.reshape(batch * (seq // GROUP_ROWS), d), m[batch:batch + dec_batch]],
                               axis=0)

    shift1, scale1, gate1, shift2, scale2, gate2 = [per_group(k) for k in range(6)]

    h1 = _norm1(x_p, x_s, n1, scale1, shift1, NORM_ROWS)

    k_off = ATTN_DIM
    glu_off = k_off + 2 * KV_DIM
    gate_off = glu_off + 2 * cdim
    w_in_b = w_in.astype(BF16)
    tglu = PROJ_COLS // 2
    w_glu = jnp.stack([w_in_b[:, glu_off:glu_off + cdim].reshape(d, cdim // tglu, tglu),
                       w_in_b[:, glu_off + cdim:gate_off].reshape(d, cdim // tglu, tglu)],
                      axis=2).reshape(d, 2 * cdim)
    tmm = PROJ_ROWS
    (q,) = _proj(h1, w_in_b[:, :k_off], "qkv", tmm, PROJ_COLS, [BF16])
    (kv,) = _proj(h1, w_in_b[:, k_off:glu_off], "qkv", tmm, 2 * KV_DIM, [BF16])
    kv_state = _kv_state(h1, w_in_b[:, k_off:glu_off], n_prompt, seq)
    (u,) = _proj(h1, w_glu, "glu", tmm, 2 * tglu, [F32])
    (gates,) = _proj(h1, w_in_b[:, gate_off:], "gate", tmm, PROJ_COLS, [BF16])

    bias_p = _pair_bias(rel_table)
    buf = cache_k.shape[1]
    kpos = jnp.concatenate([jnp.arange(buf, dtype=jnp.int32) - buf, jnp.arange(t_new, dtype=jnp.int32)])
    bias_s = _rel_bias(rel_table, kpos[None, :] - jnp.arange(t_new, dtype=jnp.int32)[:, None])
    sink_f = sink.astype(F32).reshape(N_KV_HEADS, Q_PER_KV, 1, 1)
    sink_kch = sink.astype(F32).reshape(N_KV_HEADS, 1, COLS_PER_KV, HEADS_PER_COL).transpose(0, 3, 1, 2)
    sink_p = jnp.broadcast_to(sink_kch[..., None, None],
                              (N_KV_HEADS, HEADS_PER_COL, 2, COLS_PER_KV, CHUNK, LANES)).reshape(
        N_KV_HEADS, HEADS_PER_COL, 2 * COLS_PER_KV * CHUNK, LANES)
    sink_s = jnp.broadcast_to(sink_f, (N_KV_HEADS, Q_PER_KV, t_new, 1)).reshape(N_KV_HEADS, Q_PER_KV * t_new, 1)
    o_p = _attn_prompt(q, kv, bias_p, sink_p, n_prompt, seq, ATTN_PAIRS)
    o_s = _attn_sample(q, kv, cache_k.reshape(dec_batch, buf, KV_DIM), cache_v.reshape(dec_batch, buf, KV_DIM),
                       bias_s, sink_s, n_prompt, t_new)

    hist = GROUP_ROWS
    sconv_pad = jnp.pad(sconv, ((0, 0), (hist - sconv.shape[1], 0), (0, 0)))
    w_rt_t = jnp.zeros((ROUTER_ROWS, d), F32)
    w_rt_t = w_rt_t.at[:N_GROUPS].set(w_grp.T).at[EXPERT_ROW0:EXPERT_ROW0 + N_EXPERTS].set(w_rt.T)
    b_rt_t = jnp.zeros((ROUTER_ROWS, 1), F32)
    b_rt_t = b_rt_t.at[:N_GROUPS, 0].set(b_grp.astype(F32)).at[EXPERT_ROW0:EXPERT_ROW0 + N_EXPERTS, 0].set(
        b_rt.astype(F32))
    nch = cdim // CONV_CHUNK
    conv_w_c = conv_w.reshape(CONV_WIDTH, nch, CONV_CHUNK).transpose(1, 0, 2)
    params = dict(conv_w=conv_w_c, conv_b=conv_b.reshape(nch, 1, CONV_CHUNK), ln_g=ln_g.reshape(1, cdim),
                  ln_b=ln_b.reshape(1, cdim), w_pw2=w_pw2.astype(BF16), b_pw2=b_pw2.reshape(1, d),
                  w_ao=w_ao.astype(BF16), w_out=w_out.astype(BF16), n2=n2.reshape(1, d),
                  w_rt=w_rt_t.astype(BF16), b_rt=b_rt_t)
    x1, h2, eidx, wtok, rank, cnt = _post(x_p, x_s, u, sconv_pad, o_p, o_s, gates, gate1, shift2, scale2,
                                             params, n_prompt, seq, tm)

    blk_e, next_e, nused, src, roww, dest = _dispatch(eidx, wtok, rank, cnt[:, 0].astype(jnp.int32), n)
    yb = _moe(h2, blk_e, next_e, nused, src, roww, w_gate, w_up, w_down)
    return x1, yb, dest, gate2, kv_state, u


def kernel(x_prompt, x_sample, c_prompt, c_sample, cache_k, cache_v, state_conv, rel_bias_table, norm1_g, norm2_g, w_ada, b_ada, w_in, attn_sink, w_attn_o, conv_w, conv_b, conv_ln_g, conv_ln_b, w_pw2, b_pw2, w_out, w_group, b_group, w_router, b_router, w_gate, w_up, w_down, final_g):
    batch, seq, d = x_prompt.shape
    dec_batch, t_new, _ = x_sample.shape
    depth = norm1_g.shape[0]
    assert depth == 1, "single trunk layer"
    assert t_new == GROUP_ROWS and seq % GROUP_ROWS == 0
    n_prompt = batch * seq
    n = n_prompt + dec_batch * t_new
    x_p = x_prompt.reshape(n_prompt, d)
    x_s = x_sample.reshape(dec_batch * t_new, d)
    c_all = jnp.concatenate([c_prompt, c_sample], axis=0)
    l = 0
    lp = tuple(a.reshape(a.shape[1:]) for a in (
        norm1_g, norm2_g, w_ada, b_ada, w_in, attn_sink, w_attn_o, conv_w, conv_b, conv_ln_g, conv_ln_b,
        w_pw2, b_pw2, w_out, w_group, b_group, w_router, b_router, w_gate, w_up, w_down))
    x1, yb, dest, gate2, kv_state, u = _layer(x_p, x_s, c_all, cache_k[l], cache_v[l], state_conv[l],
                                              rel_bias_table, lp, seq, t_new)
    out_p, out_s = _final(x1, yb, dest, gate2, final_g, n_prompt, TOKEN_TILE)

    y_prompt = out_p.reshape(batch, seq, d)
    y_sample = out_s.reshape(dec_batch, t_new, d)
    cdim = u.shape[1]
    kvp = kv_state[:batch * WINDOW].reshape(batch, WINDOW, 2 * KV_DIM)
    new_k_prompt = kvp[..., :KV_DIM].reshape(1, batch, WINDOW, N_KV_HEADS, HEAD_DIM)
    new_v_prompt = kvp[..., KV_DIM:].reshape(1, batch, WINDOW, N_KV_HEADS, HEAD_DIM)
    new_conv_prompt = jnp.stack([u[(b + 1) * seq - (CONV_WIDTH - 1):(b + 1) * seq] for b in range(batch)])[None]
    kvs = kv_state[batch * WINDOW:].reshape(dec_batch, t_new, 2 * KV_DIM)
    buf = cache_k.shape[2]
    k_new = kvs[..., :KV_DIM].reshape(dec_batch, t_new, N_KV_HEADS, HEAD_DIM)
    v_new = kvs[..., KV_DIM:].reshape(dec_batch, t_new, N_KV_HEADS, HEAD_DIM)
    new_k_sample = jnp.concatenate([cache_k[l], k_new], axis=1)[:, -buf:][None]
    new_v_sample = jnp.concatenate([cache_v[l], v_new], axis=1)[:, -buf:][None]
    us = u[n_prompt:].reshape(dec_batch, t_new, cdim)
    new_conv_sample = jnp.concatenate([state_conv[l], us], axis=1)[:, -(CONV_WIDTH - 1):][None]
    return (y_prompt, y_sample, new_k_prompt, new_v_prompt, new_conv_prompt,
            new_k_sample, new_v_sample, new_conv_sample)
```

```python
import functools
import math

import jax
import jax.numpy as jnp
from jax import lax
from jax.experimental import pallas as pl
from jax.experimental.pallas import tpu as pltpu

F32 = jnp.float32
BF16 = jnp.bfloat16

CHUNK = 64
HEAD_DIM = 64
N_Q_HEADS = 16
N_KV_HEADS = 4
Q_PER_KV = N_Q_HEADS // N_KV_HEADS
ATTN_DIM = N_Q_HEADS * HEAD_DIM
KV_DIM = N_KV_HEADS * HEAD_DIM
WINDOW = 128
WIN_CHUNKS = WINDOW // CHUNK
CONV_WIDTH = 31
NUM_BUCKETS = 32
MAX_DISTANCE = 128
N_GROUPS = 4
EXPERTS_PER_GROUP = 8
N_EXPERTS = N_GROUPS * EXPERTS_PER_GROUP
MOE_BLOCK = 128
RMS_EPS = 1e-6
LN_EPS = 1e-5
NEG_INF = -1e30

GROUP_ROWS = 32
LANES = 128
SUBLANES = 8
CONV_CHUNK = 256
ROUTER_ROWS = 64
EXPERT_ROW0 = 8

NORM_ROWS = 512
PROJ_ROWS = 1536
PROJ_COLS = 1024
TOKEN_TILE = 256
ATTN_PAIRS = 8
V7X_VMEM_LIMIT = 56 * 1024 * 1024


def _cparams(*sem):
    return pltpu.CompilerParams(dimension_semantics=sem, vmem_limit_bytes=V7X_VMEM_LIMIT)


def _ada_kernel(c_ref, w_ref, b_ref, o_ref):
    c = c_ref[...]
    s = c * jax.nn.sigmoid(c)
    o_ref[...] = jnp.dot(s.astype(BF16), w_ref[...].astype(BF16),
                         preferred_element_type=F32) + b_ref[...]


def _ada(c_all, w_ada, b_ada):
    rows, d = c_all.shape
    ncol = w_ada.shape[1]
    tn = 1024
    return pl.pallas_call(
        _ada_kernel,
        grid=(ncol // tn,),
        in_specs=[pl.BlockSpec((rows, d), lambda j: (0, 0)),
                  pl.BlockSpec((d, tn), lambda j: (0, j)),
                  pl.BlockSpec((1, tn), lambda j: (0, j))],
        out_specs=pl.BlockSpec((rows, tn), lambda j: (0, j)),
        out_shape=jax.ShapeDtypeStruct((rows, ncol), F32),
        compiler_params=_cparams("arbitrary"),
        name="ada",
    )(c_all, w_ada, b_ada.reshape(1, ncol))


def _rms_mod(x, gain, scale, shift):
    ms = jnp.mean(x * x, axis=-1, keepdims=True)
    y = x * lax.rsqrt(ms + RMS_EPS) * gain
    return y * (1.0 + scale) + shift


def _pack_bf16_pairs(xb):
    w = xb.shape[1] // 2
    bits = lax.bitcast_convert_type(xb.astype(F32), jnp.uint32)
    return (bits[:, :w] >> 16) | (bits[:, w:] & jnp.uint32(0xFFFF0000))


def _unpack_bf16_pairs(p):
    lo = lax.bitcast_convert_type(p << 16, F32)
    hi = lax.bitcast_convert_type(p & jnp.uint32(0xFFFF0000), F32)
    return lo, hi


def _two_stream_specs(tm, width, prompt_tiles):
    return [pl.BlockSpec((tm, width), lambda i: (jnp.minimum(i, prompt_tiles - 1), 0)),
            pl.BlockSpec((tm, width), lambda i: (jnp.maximum(i - prompt_tiles, 0), 0))]


def _norm1_kernel(xp_ref, xs_ref, g_ref, sc_ref, sh_ref, h_ref, *, groups, prompt_tiles):
    def run(x_ref):
        def body(gi, carry):
            r = pl.multiple_of(gi * GROUP_ROWS, GROUP_ROWS)
            h = _rms_mod(x_ref[pl.ds(r, GROUP_ROWS), :], g_ref[...],
                         sc_ref[pl.ds(gi, 1), :], sh_ref[pl.ds(gi, 1), :])
            h_ref[pl.ds(r, GROUP_ROWS), :] = h.astype(h_ref.dtype)
            return carry

        lax.fori_loop(0, groups, body, 0, unroll=4)

    i = pl.program_id(0)
    pl.when(i < prompt_tiles)(lambda: run(xp_ref))
    pl.when(i >= prompt_tiles)(lambda: run(xs_ref))


def _norm1(x_p, x_s, gain, scale_g, shift_g, tm):
    d = x_p.shape[1]
    n = x_p.shape[0] + x_s.shape[0]
    groups = tm // GROUP_ROWS
    prompt_tiles = x_p.shape[0] // tm
    return pl.pallas_call(
        functools.partial(_norm1_kernel, groups=groups, prompt_tiles=prompt_tiles),
        grid=(n // tm,),
        in_specs=_two_stream_specs(tm, d, prompt_tiles) + [
            pl.BlockSpec((1, d), lambda i: (0, 0)),
            pl.BlockSpec((groups, d), lambda i: (i, 0)),
            pl.BlockSpec((groups, d), lambda i: (i, 0))],
        out_specs=pl.BlockSpec((tm, d), lambda i: (i, 0)),
        out_shape=jax.ShapeDtypeStruct((n, d), BF16),
        compiler_params=_cparams("arbitrary"),
        name="norm1",
    )(x_p, x_s, gain.reshape(1, d), scale_g, shift_g)


def _proj_kernel(h_ref, w_ref, *o_refs, mode):
    acc = jnp.dot(h_ref[...], w_ref[...], preferred_element_type=F32)
    if mode == "qkv":
        o_refs[0][...] = acc.astype(BF16)
    elif mode == "glu":
        half = acc.shape[1] // 2
        o_refs[0][...] = acc[:, :half] * jax.nn.sigmoid(acc[:, half:])
    else:
        o_refs[0][...] = jax.nn.sigmoid(acc).astype(BF16)


def _proj(h, w, mode, tm, tnw, out_dtypes):
    n, d = h.shape
    ncol = w.shape[1]
    tno = tnw // 2 if mode == "glu" else tnw
    nout = ncol // 2 if mode == "glu" else ncol
    outs = pl.pallas_call(
        functools.partial(_proj_kernel, mode=mode),
        grid=(n // tm, ncol // tnw),
        in_specs=[pl.BlockSpec((tm, d), lambda i, j: (i, 0)),
                  pl.BlockSpec((d, tnw), lambda i, j: (0, j))],
        out_specs=[pl.BlockSpec((tm, tno), lambda i, j: (i, j)) for _ in out_dtypes],
        out_shape=[jax.ShapeDtypeStruct((n, nout), dt) for dt in out_dtypes],
        compiler_params=_cparams("arbitrary", "arbitrary"),
        name="proj_" + mode,
    )(h, w)
    return outs


def _kv_state_kernel(h_ref, w_ref, o_ref):
    o_ref[...] = jnp.dot(h_ref[...], w_ref[...], preferred_element_type=F32)


def _kv_state(h, w_kv, n_prompt, seq):
    n, d = h.shape
    batch = n_prompt // seq
    per_seq = seq // WINDOW
    steps = batch + (n - n_prompt) // WINDOW

    def row_block(i):
        return jnp.where(i < batch, (i + 1) * per_seq - 1, n_prompt // WINDOW + i - batch)

    return pl.pallas_call(
        _kv_state_kernel,
        grid=(steps,),
        in_specs=[pl.BlockSpec((WINDOW, d), lambda i: (row_block(i), 0)),
                  pl.BlockSpec((d, 2 * KV_DIM), lambda i: (0, 0))],
        out_specs=pl.BlockSpec((WINDOW, 2 * KV_DIM), lambda i: (i, 0)),
        out_shape=jax.ShapeDtypeStruct((steps * WINDOW, 2 * KV_DIM), F32),
        compiler_params=_cparams("arbitrary"),
        name="kv_state",
    )(h, w_kv)


def _t5_bucket(rel):
    nb = NUM_BUCKETS // 2
    n = -rel
    ret = jnp.where(n < 0, nb, 0)
    n = jnp.abs(n)
    max_exact = nb // 2
    nf = jnp.maximum(n, 1).astype(F32)
    large = max_exact + (jnp.log(nf / max_exact) / math.log(MAX_DISTANCE / max_exact)
                         * (nb - max_exact)).astype(jnp.int32)
    large = jnp.minimum(large, nb - 1)
    return ret + jnp.where(n < max_exact, n, large)


def _bias_kernel(tbl_ref, bkt_ref, o_ref):
    bkt = bkt_ref[...]
    nq = bkt.shape[0]
    for k in range(N_KV_HEADS):
        for g in range(Q_PER_KV):
            acc = jnp.zeros(bkt.shape, F32)
            for b in range(NUM_BUCKETS):
                acc = jnp.where(bkt == b, tbl_ref[b, k * Q_PER_KV + g], acc)
            o_ref[k, g * nq:(g + 1) * nq, :] = acc


def _rel_bias(rel_table, rel):
    nq, nk = rel.shape
    return pl.pallas_call(
        _bias_kernel,
        in_specs=[pl.BlockSpec(memory_space=pltpu.SMEM),
                  pl.BlockSpec((nq, nk), lambda: (0, 0))],
        out_specs=pl.BlockSpec((N_KV_HEADS, Q_PER_KV * nq, nk), lambda: (0, 0, 0)),
        out_shape=jax.ShapeDtypeStruct((N_KV_HEADS, Q_PER_KV * nq, nk), F32),
        name="rel_bias",
    )(rel_table.astype(F32), _t5_bucket(rel))


PAIR_BAND = 2 * CHUNK + WINDOW
HEADS_PER_COL = LANES // HEAD_DIM
COLS_PER_KV = Q_PER_KV // HEADS_PER_COL


def _pair_bias_kernel(tbl_ref, bkt_ref, o_ref):
    for e in range(2):
        bkt = bkt_ref[e]
        for k in range(N_KV_HEADS):
            for col in range(COLS_PER_KV):
                for half in range(HEADS_PER_COL):
                    head = k * Q_PER_KV + col * HEADS_PER_COL + half
                    acc = jnp.full(bkt.shape, NEG_INF, F32)
                    for b in range(NUM_BUCKETS):
                        acc = jnp.where(bkt == b, tbl_ref[b, head], acc)
                    r0 = (e * COLS_PER_KV + col) * CHUNK
                    o_ref[k, r0:r0 + CHUNK, half * PAIR_BAND:(half + 1) * PAIR_BAND] = acc


def _pair_bias(rel_table):
    qi = jnp.arange(CHUNK, dtype=jnp.int32)[None, :, None]
    kj = jnp.arange(PAIR_BAND, dtype=jnp.int32)[None, None, :]
    e = jnp.arange(2, dtype=jnp.int32)[:, None, None]
    rel = (kj - WINDOW) - (e * CHUNK + qi)
    key_chunk = kj // CHUNK - e
    seen = (key_chunk >= 0) & (key_chunk <= WIN_CHUNKS)
    bkt = jnp.where(seen, _t5_bucket(rel), -1)
    rows = 2 * COLS_PER_KV * CHUNK
    return pl.pallas_call(
        _pair_bias_kernel,
        in_specs=[pl.BlockSpec(memory_space=pltpu.SMEM),
                  pl.BlockSpec((2, CHUNK, PAIR_BAND), lambda: (0, 0, 0))],
        out_specs=pl.BlockSpec((N_KV_HEADS, rows, HEADS_PER_COL * PAIR_BAND), lambda: (0, 0, 0)),
        out_shape=jax.ShapeDtypeStruct((N_KV_HEADS, rows, HEADS_PER_COL * PAIR_BAND), F32),
        name="pair_bias",
    )(rel_table.astype(F32), bkt)


def _attend(qc, kk_all, vv_all, bias_ref, sink_ref, mask_thr):
    nq = qc.shape[0]
    nk = kk_all.shape[0]
    pieces = []
    for k in range(N_KV_HEADS):
        qs = jnp.concatenate(
            [qc[:, (k * Q_PER_KV + g) * HEAD_DIM:(k * Q_PER_KV + g + 1) * HEAD_DIM]
             for g in range(Q_PER_KV)], axis=0)
        kk = kk_all[:, k * HEAD_DIM:(k + 1) * HEAD_DIM]
        vv = vv_all[:, k * HEAD_DIM:(k + 1) * HEAD_DIM]
        lg = lax.dot_general(qs, kk, (((1,), (1,)), ((), ())),
                             preferred_element_type=F32) * (HEAD_DIM ** -0.5) + bias_ref[k]
        if mask_thr is not None:
            col = lax.broadcasted_iota(jnp.int32, (Q_PER_KV * nq, nk), 1)
            lg = jnp.where(col < mask_thr, NEG_INF, lg)
        s = sink_ref[k]
        m = jnp.maximum(jnp.max(lg, axis=-1, keepdims=True), s)
        p = jnp.exp(lg - m)
        den = jnp.sum(p, axis=-1, keepdims=True) + jnp.exp(s - m)
        o = jnp.dot(p.astype(BF16), vv, preferred_element_type=F32) / den
        pieces.extend(o[g * nq:(g + 1) * nq, :] for g in range(Q_PER_KV))
    return jnp.concatenate(pieces, axis=1)


def _attn_prompt_kernel(q_ref, kv_ref, halo_ref, bias_ref, sink_ref, o_ref, ka, kb, va, vb, *, pairs):
    t = pl.program_id(1)
    kvcat = jnp.concatenate([halo_ref[...], kv_ref[...]], axis=0)
    low = lax.broadcasted_iota(jnp.int32, (kvcat.shape[0], LANES), 1) < HEAD_DIM
    zero = jnp.zeros((kvcat.shape[0], LANES), BF16)
    ones_low = jnp.where(low, 1.0, 0.0).astype(BF16)
    ones_high = jnp.where(low, 0.0, 1.0).astype(BF16)
    kv_cols = KV_DIM // LANES
    for col in range(2 * kv_cols):
        x = kvcat[:, col * LANES:(col + 1) * LANES]
        xs = jnp.concatenate([x[:, HEAD_DIM:], x[:, :HEAD_DIM]], axis=1)
        dst_a, dst_b = (ka, kb) if col < kv_cols else (va, vb)
        k0 = (col % kv_cols) * HEADS_PER_COL
        dst_a[k0, :, 0:LANES] = jnp.where(low, x, zero)
        dst_b[k0, :, 0:LANES] = jnp.where(low, zero, xs)
        dst_a[k0 + 1, :, 0:LANES] = jnp.where(low, xs, zero)
        dst_b[k0 + 1, :, 0:LANES] = jnp.where(low, zero, x)
    for k in range(N_KV_HEADS):
        va[k, :, LANES:2 * LANES] = ones_low
        vb[k, :, LANES:2 * LANES] = ones_high

    out_low = lax.broadcasted_iota(jnp.int32, (2 * COLS_PER_KV * CHUNK, LANES), 1) < HEAD_DIM
    for p in range(pairs):
        r0 = p * 2 * CHUNK
        for k in range(N_KV_HEADS):
            lhs = jnp.concatenate(
                [q_ref[r0 + e * CHUNK:r0 + (e + 1) * CHUNK, (k * COLS_PER_KV + c) * LANES:(k * COLS_PER_KV + c + 1) * LANES]
                 for e in range(2) for c in range(COLS_PER_KV)], axis=0)
            keys = jnp.concatenate([ka[k, r0:r0 + PAIR_BAND, :], kb[k, r0:r0 + PAIR_BAND, :]], axis=0)
            lg = lax.dot_general(lhs, keys, (((1,), (1,)), ((), ())),
                                 preferred_element_type=F32) * (HEAD_DIM ** -0.5) + bias_ref[k]
            probs, sink_terms = [], []
            for half in range(HEADS_PER_COL):
                seg = lg[:, half * PAIR_BAND:(half + 1) * PAIR_BAND]
                if p == 0:
                    kcol = lax.broadcasted_iota(jnp.int32, seg.shape, 1)
                    seg = jnp.where(kcol < jnp.where(t == 0, WINDOW, 0), NEG_INF, seg)
                s = sink_ref[k, half]
                folded = seg[:, :LANES]
                for j in range(1, PAIR_BAND // LANES):
                    folded = jnp.maximum(folded, seg[:, j * LANES:(j + 1) * LANES])
                m = jnp.maximum(jnp.max(folded, axis=-1, keepdims=True), s)
                pr = jnp.exp(seg - jnp.concatenate([m] * (PAIR_BAND // LANES), axis=1))
                sink_terms.append(jnp.exp(s - m))
                probs.append(pr.astype(BF16))
            vals = jnp.concatenate([va[k, r0:r0 + PAIR_BAND, :], vb[k, r0:r0 + PAIR_BAND, :]], axis=0)
            oe = jnp.dot(jnp.concatenate(probs, axis=1), vals, preferred_element_type=F32)
            o = oe[:, :LANES] / (oe[:, LANES:] + jnp.where(out_low, sink_terms[0], sink_terms[1]))
            for e in range(2):
                for c in range(COLS_PER_KV):
                    rr = (e * COLS_PER_KV + c) * CHUNK
                    o_ref[r0 + e * CHUNK:r0 + (e + 1) * CHUNK,
                          (k * COLS_PER_KV + c) * LANES:(k * COLS_PER_KV + c + 1) * LANES] = (
                        o[rr:rr + CHUNK, :].astype(o_ref.dtype))


def _attn_prompt(q, kv, bias, sink_rows, n_prompt, seq, pairs):
    rows = pairs * 2 * CHUNK
    tiles = seq // rows
    halo_per_tile = rows // WINDOW
    keys = WINDOW + rows
    return pl.pallas_call(
        functools.partial(_attn_prompt_kernel, pairs=pairs),
        grid=(n_prompt // seq, tiles),
        scratch_shapes=[pltpu.VMEM((N_KV_HEADS, keys, LANES), BF16) for _ in range(2)]
        + [pltpu.VMEM((N_KV_HEADS, keys, 2 * LANES), BF16) for _ in range(2)],
        in_specs=[pl.BlockSpec((rows, ATTN_DIM), lambda b, t: (b * tiles + t, 0)),
                  pl.BlockSpec((rows, 2 * KV_DIM), lambda b, t: (b * tiles + t, 0)),
                  pl.BlockSpec((WINDOW, 2 * KV_DIM),
                               lambda b, t: (jnp.maximum((b * tiles + t) * halo_per_tile - 1, 0), 0)),
                  pl.BlockSpec(bias.shape, lambda b, t: (0, 0, 0)),
                  pl.BlockSpec(sink_rows.shape, lambda b, t: (0, 0, 0, 0))],
        out_specs=pl.BlockSpec((rows, ATTN_DIM), lambda b, t: (b * tiles + t, 0)),
        out_shape=jax.ShapeDtypeStruct((n_prompt, ATTN_DIM), BF16),
        compiler_params=_cparams("arbitrary", "arbitrary"),
        name="attn_prompt",
    )(q, kv, kv, bias, sink_rows)


def _attn_sample_kernel(q_ref, kv_ref, ck_ref, cv_ref, bias_ref, sink_ref, o_ref):
    kv = kv_ref[...]
    kk = jnp.concatenate([ck_ref[0].astype(BF16), kv[:, :KV_DIM]], axis=0)
    vv = jnp.concatenate([cv_ref[0].astype(BF16), kv[:, KV_DIM:]], axis=0)
    o = _attend(q_ref[...], kk, vv, bias_ref, sink_ref, None)
    o_ref[...] = o.astype(o_ref.dtype)


def _attn_sample(q, kv, cache_k, cache_v, bias, sink_rows, n_prompt, t_new):
    dec_batch, buf, _ = cache_k.shape
    first = n_prompt // t_new
    return pl.pallas_call(
        _attn_sample_kernel,
        grid=(dec_batch,),
        in_specs=[pl.BlockSpec((t_new, ATTN_DIM), lambda s: (first + s, 0)),
                  pl.BlockSpec((t_new, 2 * KV_DIM), lambda s: (first + s, 0)),
                  pl.BlockSpec((1, buf, KV_DIM), lambda s: (s, 0, 0)),
                  pl.BlockSpec((1, buf, KV_DIM), lambda s: (s, 0, 0)),
                  pl.BlockSpec(bias.shape, lambda s: (0, 0, 0)),
                  pl.BlockSpec(sink_rows.shape, lambda s: (0, 0, 0))],
        out_specs=pl.BlockSpec((t_new, ATTN_DIM), lambda s: (s, 0)),
        out_shape=jax.ShapeDtypeStruct((dec_batch * t_new, ATTN_DIM), BF16),
        compiler_params=_cparams("arbitrary"),
        name="attn_sample",
    )(q, kv, cache_k, cache_v, bias, sink_rows)


def _route(lt, t, counted, eidx_ref, wtok_ref, rank_ref, cnt_ref, cnt_scr):
    gl = [lt[r:r + 1, :] for r in range(N_GROUPS)]
    gmax = gl[0]
    gsel = jnp.zeros(gl[0].shape, jnp.int32)
    for r in range(1, N_GROUPS):
        better = gl[r] > gmax
        gsel = jnp.where(better, r, gsel)
        gmax = jnp.maximum(gmax, gl[r])
    gexp = [jnp.exp(v - gmax) for v in gl]
    gsum = gexp[0]
    for r in range(1, N_GROUPS):
        gsum = gsum + gexp[r]
    psel = jnp.zeros(gl[0].shape, F32)
    for r in range(N_GROUPS):
        psel = jnp.where(gsel == r, gexp[r] / gsum, psel)
    el = jnp.zeros((EXPERTS_PER_GROUP, lt.shape[1]), F32)
    for r in range(N_GROUPS):
        lo = EXPERT_ROW0 + r * EXPERTS_PER_GROUP
        el = jnp.where(gsel == r, lt[lo:lo + EXPERTS_PER_GROUP, :], el)
    emax = jnp.max(el, axis=0, keepdims=True)
    ee = jnp.exp(el - emax)
    pin = ee / jnp.sum(ee, axis=0, keepdims=True)
    idx = lax.broadcasted_iota(jnp.int32, pin.shape, 0)
    p1 = jnp.max(pin, axis=0, keepdims=True)
    i1 = jnp.min(jnp.where(pin == p1, idx, EXPERTS_PER_GROUP), axis=0, keepdims=True)
    rest = jnp.where(idx == i1, -1.0, pin)
    p2 = jnp.max(rest, axis=0, keepdims=True)
    i2 = jnp.min(jnp.where(rest == p2, idx, EXPERTS_PER_GROUP), axis=0, keepdims=True)
    tot = p1 + p2
    e1 = gsel * EXPERTS_PER_GROUP + i1
    e2 = gsel * EXPERTS_PER_GROUP + i2
    eidx_ref[t, 0:1, :] = e1
    eidx_ref[t, 1:2, :] = e2
    wtok_ref[t, 0:1, :] = psel * p1 / tot
    wtok_ref[t, 1:2, :] = psel * p2 / tot

    rows = lt.shape[1]
    eiota = lax.broadcasted_iota(jnp.int32, (N_EXPERTS, rows), 0)
    oh1 = (eiota == e1).astype(F32)
    oh2 = (eiota == e2).astype(F32)
    both = oh1 + oh2
    before = (lax.broadcasted_iota(jnp.int32, (rows, rows), 0)
              < lax.broadcasted_iota(jnp.int32, (rows, rows), 1)).astype(BF16)
    prior = jnp.dot(both.astype(BF16), before, preferred_element_type=F32) + cnt_scr[:, 0:1]
    rank_ref[t, 0:1, :] = jnp.sum(oh1 * prior, axis=0, keepdims=True).astype(jnp.int32)
    rank_ref[t, 1:2, :] = jnp.sum(oh2 * prior, axis=0, keepdims=True).astype(jnp.int32)
    cnt_scr[...] = cnt_scr[...] + jnp.where(counted, jnp.sum(both, axis=1, keepdims=True), 0.0)
    cnt_ref[...] = cnt_scr[...]


def _post_kernel(xp_ref, xs_ref, u_ref, uhalo_ref, sconv_ref, op_ref, os_ref, gates_ref,
                 gate1_ref, shift2_ref, scale2_ref,
                 cw_ref, cb_ref, lng_ref, lnb_ref, wpw2_ref, bpw2_ref, wao_ref, wout_ref,
                 n2_ref, wrt_ref, brt_ref,
                 x1_ref, h2_ref, eidx_ref, wtok_ref, rank_ref, cnt_ref,
                 uext, shift_scr, d_scr, s_scr, o_scr, mix_scr, h2b_scr, cnt_scr,
                 *, groups, prompt_tiles, tiles_per_seq, tiles):
    i = pl.program_id(0)
    d_model = xp_ref.shape[1]

    @pl.when(i == 0)
    def _():
        cnt_scr[...] = jnp.zeros(cnt_scr.shape, F32)
        h2b_scr[...] = jnp.zeros(h2b_scr.shape, BF16)
    nch = cw_ref.shape[0]
    cw = uext.shape[2]
    hist = uext.shape[1] - GROUP_ROWS
    lead = hist - (CONV_WIDTH - 1)

    def put_hist(g, rows):
        for c in range(nch):
            uext[g * nch + c, 0:hist, :] = rows[:, c * cw:(c + 1) * cw]

    @pl.when(i < prompt_tiles)
    def _():
        first = (i % tiles_per_seq) == 0
        put_hist(0, jnp.where(first, 0.0, uhalo_ref[...]))
        for g in range(1, groups):
            put_hist(g, u_ref[(g - 1) * GROUP_ROWS:g * GROUP_ROWS, :])
        o_scr[...] = op_ref[...]

    @pl.when(i >= prompt_tiles)
    def _():
        for g in range(groups):
            put_hist(g, sconv_ref[g])
        o_scr[...] = os_ref[...]

    for g in range(groups):
        for c in range(nch):
            uext[g * nch + c, hist:hist + GROUP_ROWS, :] = (
                u_ref[g * GROUP_ROWS:(g + 1) * GROUP_ROWS, c * cw:(c + 1) * cw])

    span = shift_scr.shape[1]

    def conv_chunk(k, carry):
        c = k % nch
        win_all = uext[k]
        for s in range(1, SUBLANES):
            shift_scr[s, :, :] = pltpu.roll(win_all, win_all.shape[0] - s, 0)[0:span, :]
        acc = None
        for j in range(CONV_WIDTH):
            base, s = divmod(lead + j, SUBLANES)
            rows = pl.ds(base * SUBLANES, GROUP_ROWS)
            win = uext[k, rows, :] if s == 0 else shift_scr[s, rows, :]
            term = win * cw_ref[c, j:j + 1, :]
            acc = term if acc is None else acc + term
        d_scr[k] = acc + cb_ref[c]
        return carry

    lax.fori_loop(0, groups * nch, conv_chunk, 0)

    for g in range(groups):
        dd = jnp.concatenate([d_scr[g * nch + c] for c in range(nch)], axis=1)
        mu = jnp.mean(dd, axis=-1, keepdims=True)
        var = jnp.mean(jnp.square(dd - mu), axis=-1, keepdims=True)
        y = (dd - mu) * lax.rsqrt(var + LN_EPS) * lng_ref[...] + lnb_ref[...]
        s_scr[g * GROUP_ROWS:(g + 1) * GROUP_ROWS, :] = (y * jax.nn.sigmoid(y)).astype(BF16)

    def route_tile(t, counted):
        lt = lax.dot_general(wrt_ref[...], h2b_scr[...], (((1,), (1,)), ((), ())),
                             preferred_element_type=F32) + brt_ref[...]
        _route(lt, t, counted, eidx_ref, wtok_ref, rank_ref, cnt_ref, cnt_scr)

    route_tile(jnp.maximum(i - 1, 0), i > 0)

    conv_out = jnp.dot(s_scr[...], wpw2_ref[...], preferred_element_type=F32) + bpw2_ref[...]
    attn_out = jnp.dot(o_scr[...], wao_ref[...], preferred_element_type=F32)
    merged = (gates_ref[:, :d_model].astype(F32) * attn_out
              + gates_ref[:, d_model:].astype(F32) * conv_out)
    mix_scr[...] = jnp.dot(merged.astype(BF16), wout_ref[...], preferred_element_type=F32)

    def residual(x_ref):
        def res_group(g, carry):
            r = pl.multiple_of(g * GROUP_ROWS, GROUP_ROWS)
            x1 = (x_ref[pl.ds(r, GROUP_ROWS), :]
                  + gate1_ref[pl.ds(g, 1), :] * mix_scr[pl.ds(r, GROUP_ROWS), :])
            x1_ref[pl.ds(r, GROUP_ROWS), :] = x1
            h = _rms_mod(x1, n2_ref[...], scale2_ref[pl.ds(g, 1), :], shift2_ref[pl.ds(g, 1), :])
            hb = h.astype(BF16)
            h2b_scr[pl.ds(r, GROUP_ROWS), :] = hb
            h2_ref[pl.ds(r, GROUP_ROWS), :] = _pack_bf16_pairs(hb)
            return carry

        lax.fori_loop(0, groups, res_group, 0, unroll=4)

    pl.when(i < prompt_tiles)(lambda: residual(xp_ref))
    pl.when(i >= prompt_tiles)(lambda: residual(xs_ref))

    @pl.when(i == tiles - 1)
    def _():
        route_tile(i, True)


def _post(x_p, x_s, u, sconv_pad, o_p, o_s, gates, gate1_g, shift2_g, scale2_g, p, n_prompt, seq, tm):
    d = x_p.shape[1]
    n = x_p.shape[0] + x_s.shape[0]
    cdim = u.shape[1]
    nch = cdim // CONV_CHUNK
    groups = tm // GROUP_ROWS
    tiles = n // tm
    prompt_tiles = n_prompt // tm
    const = lambda shape: pl.BlockSpec(shape, lambda i: (0,) * len(shape))
    row = lambda w: pl.BlockSpec((tm, w), lambda i: (i, 0))
    grp = pl.BlockSpec((groups, d), lambda i: (i, 0))
    outs = pl.pallas_call(
        functools.partial(_post_kernel, groups=groups, prompt_tiles=prompt_tiles,
                          tiles_per_seq=seq // tm, tiles=tiles),
        grid=(tiles,),
        in_specs=_two_stream_specs(tm, d, prompt_tiles) + [
                  row(cdim),
                  pl.BlockSpec((GROUP_ROWS, cdim), lambda i: (jnp.maximum(i * groups - 1, 0), 0)),
                  pl.BlockSpec((groups, GROUP_ROWS, cdim),
                               lambda i: (jnp.maximum(i - prompt_tiles, 0), 0, 0))]
                 + _two_stream_specs(tm, ATTN_DIM, prompt_tiles) + [
                  row(2 * d), grp, grp, grp,
                  const((nch, CONV_WIDTH, CONV_CHUNK)), const((nch, 1, CONV_CHUNK)),
                  const((1, cdim)), const((1, cdim)),
                  const((cdim, d)), const((1, d)), const((ATTN_DIM, d)), const((d, d)),
                  const((1, d)), const((ROUTER_ROWS, d)), const((ROUTER_ROWS, 1))],
        out_specs=[row(d),
                   row(d // 2),
                   const((tiles, 2, tm)), const((tiles, 2, tm)), const((tiles, 2, tm)),
                   pl.BlockSpec((N_EXPERTS, LANES), lambda i: (0, 0))],
        out_shape=[jax.ShapeDtypeStruct((n, d), F32),
                   jax.ShapeDtypeStruct((n, d // 2), jnp.uint32),
                   jax.ShapeDtypeStruct((tiles, 2, tm), jnp.int32),
                   jax.ShapeDtypeStruct((tiles, 2, tm), F32),
                   jax.ShapeDtypeStruct((tiles, 2, tm), jnp.int32),
                   jax.ShapeDtypeStruct((N_EXPERTS, LANES), F32)],
        scratch_shapes=[pltpu.VMEM((groups * nch, 2 * GROUP_ROWS, CONV_CHUNK), F32),
                        pltpu.VMEM((SUBLANES, 2 * GROUP_ROWS - SUBLANES, CONV_CHUNK), F32),
                        pltpu.VMEM((groups * nch, GROUP_ROWS, CONV_CHUNK), F32),
                        pltpu.VMEM((tm, cdim), BF16),
                        pltpu.VMEM((tm, ATTN_DIM), BF16),
                        pltpu.VMEM((tm, d), F32),
                        pltpu.VMEM((tm, d), BF16),
                        pltpu.VMEM((N_EXPERTS, LANES), F32)],
        compiler_params=_cparams("arbitrary"),
        name="post",
    )(x_p, x_s, u, u, sconv_pad, o_p, o_s, gates, gate1_g, shift2_g, scale2_g,
      p["conv_w"], p["conv_b"], p["ln_g"], p["ln_b"], p["w_pw2"], p["b_pw2"], p["w_ao"], p["w_out"],
      p["n2"], p["w_rt"], p["b_rt"])
    x1, h2, eidx, wtok, rank, cnt = outs
    per_slot = lambda a: a.transpose(1, 0, 2).reshape(2, n)
    return x1, h2, per_slot(eidx), per_slot(wtok), per_slot(rank), cnt


TABLE_GROUP = 8
GATHER_AHEAD = 8
FINAL_AHEAD = 1


def _grouped_tables(tab, group=TABLE_GROUP, ahead=1):
    steps, w = tab.shape
    assert steps % group == 0
    padded = jnp.concatenate([tab] + [tab[-1:]] * ahead, axis=0)
    cols = [tab.reshape(steps // group, group * w)] + [padded[group + a::group][:steps // group]
                                                         for a in range(ahead)]
    return jnp.concatenate(cols, axis=1).reshape(steps // group, 1, (group + ahead) * w)

def _moe_kernel(blk_e_ref, next_e_ref, nused_ref, src_ref, roww_ref,
                h2_hbm, wg_hbm, wu_hbm, wd_hbm, y_ref,
                xbuf, x16, wg_f, wu_f, wd_f, wg_b, wu_b, wd_b, gsem, wsem):
    i = pl.program_id(0)
    nused = nused_ref[0]
    nslots = GATHER_AHEAD + 1
    slot = i % nslots
    base = (i % TABLE_GROUP) * MOE_BLOCK
    weights = ((wg_hbm, wg_f, wg_b), (wu_hbm, wu_f, wu_b), (wd_hbm, wd_f, wd_b))

    def gather(first, sl):
        for r in range(MOE_BLOCK):
            pltpu.make_async_copy(h2_hbm.at[pl.ds(src_ref[0, 0, first + r], 1), :],
                                  xbuf.at[sl, pl.ds(r, 1), :], gsem.at[sl]).start(priority=0)

    def gather_wait(sl):
        pltpu.make_async_copy(h2_hbm.at[pl.ds(0, MOE_BLOCK), :], xbuf.at[sl], gsem.at[sl]).wait()

    def weights_start(e):
        for hbm, f32_buf, _ in weights:
            pltpu.make_async_copy(hbm.at[e], f32_buf, wsem.at[0]).start(priority=1)

    def weights_wait():
        for hbm, f32_buf, _ in weights:
            pltpu.make_async_copy(hbm.at[0], f32_buf, wsem.at[0]).wait()

    @pl.when(i == 0)
    def _():
        weights_start(blk_e_ref[0])
        for a in range(GATHER_AHEAD):
            gather(base + a * MOE_BLOCK, a)

    @pl.when(i >= nused)
    def _():
        y_ref[...] = jnp.zeros(y_ref.shape, y_ref.dtype)

    @pl.when(i < nused)
    def _():
        changed = jnp.logical_or(i == 0, blk_e_ref[i] != blk_e_ref[jnp.maximum(i - 1, 0)])

        @pl.when(changed)
        def _():
            weights_wait()
            for _, f32_buf, b16_buf in weights:
                b16_buf[...] = f32_buf[...].astype(BF16)
            weights_start(next_e_ref[i])

        gather_wait(slot)
        x_lo, x_hi = _unpack_bf16_pairs(xbuf[slot])
        x16[:, :x_lo.shape[1]] = x_lo.astype(BF16)
        x16[:, x_lo.shape[1]:] = x_hi.astype(BF16)

        gather(base + GATHER_AHEAD * MOE_BLOCK, (i + GATHER_AHEAD) % nslots)
        x = x16[...]
        hg = jnp.dot(x, wg_b[...], preferred_element_type=F32)
        hu = jnp.dot(x, wu_b[...], preferred_element_type=F32)
        hid = (hg * jax.nn.sigmoid(hg) * hu).astype(BF16)
        y = jnp.dot(hid, wd_b[...], preferred_element_type=F32) * roww_ref[...]
        y_ref[...] = _pack_bf16_pairs(y.astype(BF16))

        @pl.when(i == nused - 1)
        def _():
            weights_wait()
            for a in range(1, GATHER_AHEAD + 1):
                gather_wait((i + a) % nslots)


def _moe(h2, blk_e, next_e, nused, src, roww, w_gate, w_up, w_down):
    n_blocks = blk_e.shape[0]
    n_exp, d, de = w_gate.shape
    src_tables = _grouped_tables(src.reshape(n_blocks, MOE_BLOCK), ahead=GATHER_AHEAD)
    grid_spec = pltpu.PrefetchScalarGridSpec(
        num_scalar_prefetch=3,
        grid=(n_blocks,),
        in_specs=[pl.BlockSpec((1, 1, src_tables.shape[2]), lambda i, be, ne, nu: (i // TABLE_GROUP, 0, 0),
                               memory_space=pltpu.SMEM),
                  pl.BlockSpec((MOE_BLOCK, 1), lambda i, be, ne, nu: (i, 0)),
                  pl.BlockSpec(memory_space=pl.ANY),
                  pl.BlockSpec(memory_space=pl.ANY),
                  pl.BlockSpec(memory_space=pl.ANY),
                  pl.BlockSpec(memory_space=pl.ANY)],
        out_specs=pl.BlockSpec((MOE_BLOCK, d // 2), lambda i, be, ne, nu: (i, 0)),
        scratch_shapes=[pltpu.VMEM((GATHER_AHEAD + 1, MOE_BLOCK, d // 2), jnp.uint32),
                        pltpu.VMEM((MOE_BLOCK, d), BF16),
                        pltpu.VMEM((d, de), F32), pltpu.VMEM((d, de), F32), pltpu.VMEM((de, d), F32),
                        pltpu.VMEM((d, de), BF16), pltpu.VMEM((d, de), BF16), pltpu.VMEM((de, d), BF16),
                        pltpu.SemaphoreType.DMA((GATHER_AHEAD + 1,)), pltpu.SemaphoreType.DMA((1,))],
    )
    return pl.pallas_call(
        _moe_kernel,
        grid_spec=grid_spec,
        out_shape=jax.ShapeDtypeStruct((n_blocks * MOE_BLOCK, d // 2), jnp.uint32),
        compiler_params=_cparams("arbitrary"),
        name="moe",
    )(blk_e, next_e, nused, src_tables, roww, h2, w_gate, w_up, w_down)


def _dispatch(eidx, wtok, rank, counts, n):
    a_tot = 2 * n
    experts = jnp.arange(N_EXPERTS, dtype=jnp.int32)
    padded = (counts + MOE_BLOCK - 1) // MOE_BLOCK * MOE_BLOCK
    pad_end = jnp.sum(jnp.where(experts[None, :] <= experts[:, None], padded[None, :], 0), axis=1)
    pad_start = pad_end - padded
    dest = jnp.sum(jnp.where(eidx[:, :, None] == experts, pad_start, 0), axis=-1) + rank
    n_blocks = -(-(a_tot + N_EXPERTS * (MOE_BLOCK - 1)) // MOE_BLOCK)
    n_blocks = -(-n_blocks // TABLE_GROUP) * TABLE_GROUP
    n_rows = n_blocks * MOE_BLOCK
    tok = jnp.tile(jnp.arange(n, dtype=jnp.int32), 2)
    upd = jnp.stack([tok, lax.bitcast_convert_type(wtok.reshape(-1), jnp.int32)], axis=1)
    rows = jnp.zeros((n_rows, 2), jnp.int32).at[dest.reshape(-1)].set(
        upd, unique_indices=True, mode="promise_in_bounds")
    blk_start = jnp.arange(n_blocks, dtype=jnp.int32) * MOE_BLOCK
    blk_e = jnp.minimum(jnp.sum((pad_end[None, :] <= blk_start[:, None]).astype(jnp.int32), axis=1),
                        N_EXPERTS - 1)
    nused = (pad_end[-1] // MOE_BLOCK).reshape(1)
    roww = lax.bitcast_convert_type(rows[:, 1], F32)
    later = jnp.where((counts > 0)[None, :] & (experts[None, :] > experts[:, None]), experts[None, :], N_EXPERTS)
    next_of = jnp.min(later, axis=1)
    next_of = jnp.where(next_of < N_EXPERTS, next_of, experts)
    next_e = jnp.sum(jnp.where(blk_e[:, None] == experts[None, :], next_of[None, :], 0), axis=1)
    return blk_e, next_e, nused, rows[:, 0].reshape(n_blocks, 1, MOE_BLOCK), roww.reshape(n_rows, 1), dest


def _final_kernel(d0_ref, d1_ref, x1_ref, gate2_ref, fg_ref, yb_hbm,
                  op_ref, os_ref, ybuf, sem, *, groups, prompt_tiles, tiles, table_group):
    i = pl.program_id(0)
    nslots = FINAL_AHEAD + 1
    slot = i % nslots
    tm = x1_ref.shape[0]
    base = (i % table_group) * tm

    def gather(first, sl):
        for r in range(tm):
            pltpu.make_async_copy(yb_hbm.at[pl.ds(d0_ref[0, 0, first + r], 1), :],
                                  ybuf.at[sl, 0, pl.ds(r, 1), :], sem.at[sl]).start(priority=0)
            pltpu.make_async_copy(yb_hbm.at[pl.ds(d1_ref[0, 0, first + r], 1), :],
                                  ybuf.at[sl, 1, pl.ds(r, 1), :], sem.at[sl]).start(priority=1)

    def gather_wait(sl):
        for k in range(2):
            pltpu.make_async_copy(yb_hbm.at[pl.ds(0, tm), :], ybuf.at[sl, k], sem.at[sl]).wait()

    @pl.when(i == 0)
    def _():
        for a in range(FINAL_AHEAD):
            gather(base + a * tm, a)

    gather_wait(slot)
    gather(base + FINAL_AHEAD * tm, (i + FINAL_AHEAD) % nslots)

    def run(o_ref):
        def body(g, carry):
            r = pl.multiple_of(g * GROUP_ROWS, GROUP_ROWS)
            lo0, hi0 = _unpack_bf16_pairs(ybuf[slot, 0, pl.ds(r, GROUP_ROWS), :])
            lo1, hi1 = _unpack_bf16_pairs(ybuf[slot, 1, pl.ds(r, GROUP_ROWS), :])
            moe = jnp.concatenate([lo0 + lo1, hi0 + hi1], axis=1)
            x2 = x1_ref[pl.ds(r, GROUP_ROWS), :] + gate2_ref[pl.ds(g, 1), :] * moe
            ms = jnp.mean(x2 * x2, axis=-1, keepdims=True)
            o_ref[pl.ds(r, GROUP_ROWS), :] = x2 * lax.rsqrt(ms + RMS_EPS) * fg_ref[...]
            return carry

        lax.fori_loop(0, groups, body, 0, unroll=4)

    pl.when(i < prompt_tiles)(lambda: run(op_ref))
    pl.when(i >= prompt_tiles)(lambda: run(os_ref))

    @pl.when(i == tiles - 1)
    def _():
        for a in range(1, FINAL_AHEAD + 1):
            gather_wait((i + a) % nslots)


def _final(x1, yb, dest, gate2_g, final_g, n_prompt, tm):
    n, d = x1.shape
    groups = tm // GROUP_ROWS
    tiles = n // tm
    prompt_tiles = n_prompt // tm
    table_group = max(g for g in range(1, TABLE_GROUP + 1) if tiles % g == 0)
    tables = [_grouped_tables(dest[k].reshape(tiles, tm), table_group, FINAL_AHEAD) for k in range(2)]
    idx_spec = pl.BlockSpec((1, 1, tables[0].shape[2]), lambda i: (i // table_group, 0, 0),
                            memory_space=pltpu.SMEM)
    return pl.pallas_call(
        functools.partial(_final_kernel, groups=groups, prompt_tiles=prompt_tiles, tiles=tiles,
                          table_group=table_group),
        grid=(tiles,),
        in_specs=[idx_spec, idx_spec,
                  pl.BlockSpec((tm, d), lambda i: (i, 0)),
                  pl.BlockSpec((groups, d), lambda i: (i, 0)),
                  pl.BlockSpec((1, d), lambda i: (0, 0)),
                  pl.BlockSpec(memory_space=pl.ANY)],
        out_specs=_two_stream_specs(tm, d, prompt_tiles),
        out_shape=[jax.ShapeDtypeStruct((n_prompt, d), F32),
                   jax.ShapeDtypeStruct((n - n_prompt, d), F32)],
        scratch_shapes=[pltpu.VMEM((FINAL_AHEAD + 1, 2, tm, d // 2), jnp.uint32),
                        pltpu.SemaphoreType.DMA((FINAL_AHEAD + 1,))],
        compiler_params=_cparams("arbitrary"),
        name="final",
    )(tables[0], tables[1], x1, gate2_g, final_g.reshape(1, d), yb)


def _layer(x_p, x_s, c_all, cache_k, cache_v, sconv, rel_table, lp, seq, t_new):
    (n1, n2, w_ada, b_ada, w_in, sink, w_ao, conv_w, conv_b, ln_g, ln_b, w_pw2, b_pw2, w_out,
     w_grp, b_grp, w_rt, b_rt, w_gate, w_up, w_down) = lp
    n_prompt, d = x_p.shape
    n = n_prompt + x_s.shape[0]
    batch = n_prompt // seq
    dec_batch = (n - n_prompt) // t_new
    cdim = conv_w.shape[1]
    tm = TOKEN_TILE

    c_rows = -(-c_all.shape[0] // 8) * 8
    c_pad = jnp.pad(c_all, ((0, c_rows - c_all.shape[0]), (0, 0)))
    mods = _ada(c_pad, w_ada, b_ada)

    def per_group(k):
        m = mods[:, k * d:(k + 1) * d]
        mp = jnp.broadcast_to(m[:batch, None, :], (batch, seq // GROUP_ROWS, d))
        return jnp.concatenate([mp.reshape(batch * (seq // GROUP_ROWS), d), m[batch:batch + dec_batch]],
                               axis=0)

    shift1, scale1, gate1, shift2, scale2, gate2 = [per_group(k) for k in range(6)]

    h1 = _norm1(x_p, x_s, n1, scale1, shift1, NORM_ROWS)

    k_off = ATTN_DIM
    glu_off = k_off + 2 * KV_DIM
    gate_off = glu_off + 2 * cdim
    w_in_b = w_in.astype(BF16)
    tglu = PROJ_COLS // 2
    w_glu = jnp.stack([w_in_b[:, glu_off:glu_off + cdim].reshape(d, cdim // tglu, tglu),
                       w_in_b[:, glu_off + cdim:gate_off].reshape(d, cdim // tglu, tglu)],
                      axis=2).reshape(d, 2 * cdim)
    tmm = PROJ_ROWS
    (q,) = _proj(h1, w_in_b[:, :k_off], "qkv", tmm, PROJ_COLS, [BF16])
    (kv,) = _proj(h1, w_in_b[:, k_off:glu_off], "qkv", tmm, 2 * KV_DIM, [BF16])
    kv_state = _kv_state(h1, w_in_b[:, k_off:glu_off], n_prompt, seq)
    (u,) = _proj(h1, w_glu, "glu", tmm, 2 * tglu, [F32])
    (gates,) = _proj(h1, w_in_b[:, gate_off:], "gate", tmm, PROJ_COLS, [BF16])

    bias_p = _pair_bias(rel_table)
    buf = cache_k.shape[1]
    kpos = jnp.concatenate([jnp.arange(buf, dtype=jnp.int32) - buf, jnp.arange(t_new, dtype=jnp.int32)])
    bias_s = _rel_bias(rel_table, kpos[None, :] - jnp.arange(t_new, dtype=jnp.int32)[:, None])
    sink_f = sink.astype(F32).reshape(N_KV_HEADS, Q_PER_KV, 1, 1)
    sink_kch = sink.astype(F32).reshape(N_KV_HEADS, 1, COLS_PER_KV, HEADS_PER_COL).transpose(0, 3, 1, 2)
    sink_p = jnp.broadcast_to(sink_kch[..., None, None],
                              (N_KV_HEADS, HEADS_PER_COL, 2, COLS_PER_KV, CHUNK, LANES)).reshape(
        N_KV_HEADS, HEADS_PER_COL, 2 * COLS_PER_KV * CHUNK, LANES)
    sink_s = jnp.broadcast_to(sink_f, (N_KV_HEADS, Q_PER_KV, t_new, 1)).reshape(N_KV_HEADS, Q_PER_KV * t_new, 1)
    o_p = _attn_prompt(q, kv, bias_p, sink_p, n_prompt, seq, ATTN_PAIRS)
    o_s = _attn_sample(q, kv, cache_k.reshape(dec_batch, buf, KV_DIM), cache_v.reshape(dec_batch, buf, KV_DIM),
                       bias_s, sink_s, n_prompt, t_new)

    hist = GROUP_ROWS
    sconv_pad = jnp.pad(sconv, ((0, 0), (hist - sconv.shape[1], 0), (0, 0)))
    w_rt_t = jnp.zeros((ROUTER_ROWS, d), F32)
    w_rt_t = w_rt_t.at[:N_GROUPS].set(w_grp.T).at[EXPERT_ROW0:EXPERT_ROW0 + N_EXPERTS].set(w_rt.T)
    b_rt_t = jnp.zeros((ROUTER_ROWS, 1), F32)
    b_rt_t = b_rt_t.at[:N_GROUPS, 0].set(b_grp.astype(F32)).at[EXPERT_ROW0:EXPERT_ROW0 + N_EXPERTS, 0].set(
        b_rt.astype(F32))
    nch = cdim // CONV_CHUNK
    conv_w_c = conv_w.reshape(CONV_WIDTH, nch, CONV_CHUNK).transpose(1, 0, 2)
    params = dict(conv_w=conv_w_c, conv_b=conv_b.reshape(nch, 1, CONV_CHUNK), ln_g=ln_g.reshape(1, cdim),
                  ln_b=ln_b.reshape(1, cdim), w_pw2=w_pw2.astype(BF16), b_pw2=b_pw2.reshape(1, d),
                  w_ao=w_ao.astype(BF16), w_out=w_out.astype(BF16), n2=n2.reshape(1, d),
                  w_rt=w_rt_t.astype(BF16), b_rt=b_rt_t)
    x1, h2, eidx, wtok, rank, cnt = _post(x_p, x_s, u, sconv_pad, o_p, o_s, gates, gate1, shift2, scale2,
                                             params, n_prompt, seq, tm)

    blk_e, next_e, nused, src, roww, dest = _dispatch(eidx, wtok, rank, cnt[:, 0].astype(jnp.int32), n)
    yb = _moe(h2, blk_e, next_e, nused, src, roww, w_gate, w_up, w_down)
    return x1, yb, dest, gate2, kv_state, u


def kernel(x_prompt, x_sample, c_prompt, c_sample, cache_k, cache_v, state_conv, rel_bias_table, norm1_g, norm2_g, w_ada, b_ada, w_in, attn_sink, w_attn_o, conv_w, conv_b, conv_ln_g, conv_ln_b, w_pw2, b_pw2, w_out, w_group, b_group, w_router, b_router, w_gate, w_up, w_down, final_g):
    batch, seq, d = x_prompt.shape
    dec_batch, t_new, _ = x_sample.shape
    depth = norm1_g.shape[0]
    assert depth == 1, "single trunk layer"
    assert t_new == GROUP_ROWS and seq % GROUP_ROWS == 0
    n_prompt = batch * seq
    n = n_prompt + dec_batch * t_new
    x_p = x_prompt.reshape(n_prompt, d)
    x_s = x_sample.reshape(dec_batch * t_new, d)
    c_all = jnp.concatenate([c_prompt, c_sample], axis=0)
    l = 0
    lp = tuple(a.reshape(a.shape[1:]) for a in (
        norm1_g, norm2_g, w_ada, b_ada, w_in, attn_sink, w_attn_o, conv_w, conv_b, conv_ln_g, conv_ln_b,
        w_pw2, b_pw2, w_out, w_group, b_group, w_router, b_router, w_gate, w_up, w_down))
    x1, yb, dest, gate2, kv_state, u = _layer(x_p, x_s, c_all, cache_k[l], cache_v[l], state_conv[l],
                                              rel_bias_table, lp, seq, t_new)
    out_p, out_s = _final(x1, yb, dest, gate2, final_g, n_prompt, TOKEN_TILE)

    y_prompt = out_p.reshape(batch, seq, d)
    y_sample = out_s.reshape(dec_batch, t_new, d)
    cdim = u.shape[1]
    kvp = kv_state[:batch * WINDOW].reshape(batch, WINDOW, 2 * KV_DIM)
    new_k_prompt = kvp[..., :KV_DIM].reshape(1, batch, WINDOW, N_KV_HEADS, HEAD_DIM)
    new_v_prompt = kvp[..., KV_DIM:].reshape(1, batch, WINDOW, N_KV_HEADS, HEAD_DIM)
    new_conv_prompt = jnp.stack([u[(b + 1) * seq - (CONV_WIDTH - 1):(b + 1) * seq] for b in range(batch)])[None]
    kvs = kv_state[batch * WINDOW:].reshape(dec_batch, t_new, 2 * KV_DIM)
    buf = cache_k.shape[2]
    k_new = kvs[..., :KV_DIM].reshape(dec_batch, t_new, N_KV_HEADS, HEAD_DIM)
    v_new = kvs[..., KV_DIM:].reshape(dec_batch, t_new, N_KV_HEADS, HEAD_DIM)
    new_k_sample = jnp.concatenate([cache_k[l], k_new], axis=1)[:, -buf:][None]
    new_v_sample = jnp.concatenate([cache_v[l], v_new], axis=1)[:, -buf:][None]
    us = u[n_prompt:].reshape(dec_batch, t_new, cdim)
    new_conv_sample = jnp.concatenate([state_conv[l], us], axis=1)[:, -(CONV_WIDTH - 1):][None]
    return (y_prompt, y_sample, new_k_prompt, new_v_prompt, new_conv_prompt,
            new_k_sample, new_v_sample, new_conv_sample)
```

```python
import functools
import math

import jax
import jax.numpy as jnp
from jax import lax
from jax.experimental import pallas as pl
from jax.experimental.pallas import tpu as pltpu

F32 = jnp.float32
BF16 = jnp.bfloat16

CHUNK = 64
HEAD_DIM = 64
N_Q_HEADS = 16
N_KV_HEADS = 4
Q_PER_KV = N_Q_HEADS // N_KV_HEADS
ATTN_DIM = N_Q_HEADS * HEAD_DIM
KV_DIM = N_KV_HEADS * HEAD_DIM
WINDOW = 128
WIN_CHUNKS = WINDOW // CHUNK
CONV_WIDTH = 31
NUM_BUCKETS = 32
MAX_DISTANCE = 128
N_GROUPS = 4
EXPERTS_PER_GROUP = 8
N_EXPERTS = N_GROUPS * EXPERTS_PER_GROUP
MOE_BLOCK = 128
RMS_EPS = 1e-6
LN_EPS = 1e-5
NEG_INF = -1e30

GROUP_ROWS = 32
LANES = 128
SUBLANES = 8
CONV_CHUNK = 256
ROUTER_ROWS = 64
EXPERT_ROW0 = 8

NORM_ROWS = 512
PROJ_ROWS = 1536
PROJ_COLS = 1024
TOKEN_TILE = 256
ATTN_PAIRS = 8
V7X_VMEM_LIMIT = 56 * 1024 * 1024


def _cparams(*sem):
    return pltpu.CompilerParams(dimension_semantics=sem, vmem_limit_bytes=V7X_VMEM_LIMIT)


def _ada_kernel(c_ref, w_ref, b_ref, o_ref):
    c = c_ref[...]
    s = c * jax.nn.sigmoid(c)
    o_ref[...] = jnp.dot(s.astype(BF16), w_ref[...].astype(BF16),
                         preferred_element_type=F32) + b_ref[...]


def _ada(c_all, w_ada, b_ada):
    rows, d = c_all.shape
    ncol = w_ada.shape[1]
    tn = 1024
    return pl.pallas_call(
        _ada_kernel,
        grid=(ncol // tn,),
        in_specs=[pl.BlockSpec((rows, d), lambda j: (0, 0)),
                  pl.BlockSpec((d, tn), lambda j: (0, j)),
                  pl.BlockSpec((1, tn), lambda j: (0, j))],
        out_specs=pl.BlockSpec((rows, tn), lambda j: (0, j)),
        out_shape=jax.ShapeDtypeStruct((rows, ncol), F32),
        compiler_params=_cparams("arbitrary"),
        name="ada",
    )(c_all, w_ada, b_ada.reshape(1, ncol))


def _rms_mod(x, gain, scale, shift):
    ms = jnp.mean(x * x, axis=-1, keepdims=True)
    y = x * lax.rsqrt(ms + RMS_EPS) * gain
    return y * (1.0 + scale) + shift


def _pack_bf16_pairs(xb):
    w = xb.shape[1] // 2
    bits = lax.bitcast_convert_type(xb.astype(F32), jnp.uint32)
    return (bits[:, :w] >> 16) | (bits[:, w:] & jnp.uint32(0xFFFF0000))


def _unpack_bf16_pairs(p):
    lo = lax.bitcast_convert_type(p << 16, F32)
    hi = lax.bitcast_convert_type(p & jnp.uint32(0xFFFF0000), F32)
    return lo, hi


def _two_stream_specs(tm, width, prompt_tiles):
    return [pl.BlockSpec((tm, width), lambda i: (jnp.minimum(i, prompt_tiles - 1), 0)),
            pl.BlockSpec((tm, width), lambda i: (jnp.maximum(i - prompt_tiles, 0), 0))]


def _norm1_kernel(xp_ref, xs_ref, g_ref, sc_ref, sh_ref, h_ref, *, groups, prompt_tiles):
    def run(x_ref):
        def body(gi, carry):
            r = pl.multiple_of(gi * GROUP_ROWS, GROUP_ROWS)
            h = _rms_mod(x_ref[pl.ds(r, GROUP_ROWS), :], g_ref[...],
                         sc_ref[pl.ds(gi, 1), :], sh_ref[pl.ds(gi, 1), :])
            h_ref[pl.ds(r, GROUP_ROWS), :] = h.astype(h_ref.dtype)
            return carry

        lax.fori_loop(0, groups, body, 0, unroll=4)

    i = pl.program_id(0)
    pl.when(i < prompt_tiles)(lambda: run(xp_ref))
    pl.when(i >= prompt_tiles)(lambda: run(xs_ref))


def _norm1(x_p, x_s, gain, scale_g, shift_g, tm):
    d = x_p.shape[1]
    n = x_p.shape[0] + x_s.shape[0]
    groups = tm // GROUP_ROWS
    prompt_tiles = x_p.shape[0] // tm
    return pl.pallas_call(
        functools.partial(_norm1_kernel, groups=groups, prompt_tiles=prompt_tiles),
        grid=(n // tm,),
        in_specs=_two_stream_specs(tm, d, prompt_tiles) + [
            pl.BlockSpec((1, d), lambda i: (0, 0)),
            pl.BlockSpec((groups, d), lambda i: (i, 0)),
            pl.BlockSpec((groups, d), lambda i: (i, 0))],
        out_specs=pl.BlockSpec((tm, d), lambda i: (i, 0)),
        out_shape=jax.ShapeDtypeStruct((n, d), BF16),
        compiler_params=_cparams("arbitrary"),
        name="norm1",
    )(x_p, x_s, gain.reshape(1, d), scale_g, shift_g)


def _proj_kernel(h_ref, w_ref, *o_refs, mode):
    acc = jnp.dot(h_ref[...], w_ref[...], preferred_element_type=F32)
    if mode == "qkv":
        o_refs[0][...] = acc.astype(BF16)
    elif mode == "glu":
        half = acc.shape[1] // 2
        o_refs[0][...] = acc[:, :half] * jax.nn.sigmoid(acc[:, half:])
    else:
        o_refs[0][...] = jax.nn.sigmoid(acc).astype(BF16)


def _proj(h, w, mode, tm, tnw, out_dtypes):
    n, d = h.shape
    ncol = w.shape[1]
    tno = tnw // 2 if mode == "glu" else tnw
    nout = ncol // 2 if mode == "glu" else ncol
    outs = pl.pallas_call(
        functools.partial(_proj_kernel, mode=mode),
        grid=(n // tm, ncol // tnw),
        in_specs=[pl.BlockSpec((tm, d), lambda i, j: (i, 0)),
                  pl.BlockSpec((d, tnw), lambda i, j: (0, j))],
        out_specs=[pl.BlockSpec((tm, tno), lambda i, j: (i, j)) for _ in out_dtypes],
        out_shape=[jax.ShapeDtypeStruct((n, nout), dt) for dt in out_dtypes],
        compiler_params=_cparams("arbitrary", "arbitrary"),
        name="proj_" + mode,
    )(h, w)
    return outs


def _kv_state_kernel(h_ref, w_ref, o_ref):
    o_ref[...] = jnp.dot(h_ref[...], w_ref[...], preferred_element_type=F32)


def _kv_state(h, w_kv, n_prompt, seq):
    n, d = h.shape
    batch = n_prompt // seq
    per_seq = seq // WINDOW
    steps = batch + (n - n_prompt) // WINDOW

    def row_block(i):
        return jnp.where(i < batch, (i + 1) * per_seq - 1, n_prompt // WINDOW + i - batch)

    return pl.pallas_call(
        _kv_state_kernel,
        grid=(steps,),
        in_specs=[pl.BlockSpec((WINDOW, d), lambda i: (row_block(i), 0)),
                  pl.BlockSpec((d, 2 * KV_DIM), lambda i: (0, 0))],
        out_specs=pl.BlockSpec((WINDOW, 2 * KV_DIM), lambda i: (i, 0)),
        out_shape=jax.ShapeDtypeStruct((steps * WINDOW, 2 * KV_DIM), F32),
        compiler_params=_cparams("arbitrary"),
        name="kv_state",
    )(h, w_kv)


def _t5_bucket(rel):
    nb = NUM_BUCKETS // 2
    n = -rel
    ret = jnp.where(n < 0, nb, 0)
    n = jnp.abs(n)
    max_exact = nb // 2
    nf = jnp.maximum(n, 1).astype(F32)
    large = max_exact + (jnp.log(nf / max_exact) / math.log(MAX_DISTANCE / max_exact)
                         * (nb - max_exact)).astype(jnp.int32)
    large = jnp.minimum(large, nb - 1)
    return ret + jnp.where(n < max_exact, n, large)


def _bias_kernel(tbl_ref, bkt_ref, o_ref):
    bkt = bkt_ref[...]
    nq = bkt.shape[0]
    for k in range(N_KV_HEADS):
        for g in range(Q_PER_KV):
            acc = jnp.zeros(bkt.shape, F32)
            for b in range(NUM_BUCKETS):
                acc = jnp.where(bkt == b, tbl_ref[b, k * Q_PER_KV + g], acc)
            o_ref[k, g * nq:(g + 1) * nq, :] = acc


def _rel_bias(rel_table, rel):
    nq, nk = rel.shape
    return pl.pallas_call(
        _bias_kernel,
        in_specs=[pl.BlockSpec(memory_space=pltpu.SMEM),
                  pl.BlockSpec((nq, nk), lambda: (0, 0))],
        out_specs=pl.BlockSpec((N_KV_HEADS, Q_PER_KV * nq, nk), lambda: (0, 0, 0)),
        out_shape=jax.ShapeDtypeStruct((N_KV_HEADS, Q_PER_KV * nq, nk), F32),
        name="rel_bias",
    )(rel_table.astype(F32), _t5_bucket(rel))


PAIR_BAND = 2 * CHUNK + WINDOW
HEADS_PER_COL = LANES // HEAD_DIM
COLS_PER_KV = Q_PER_KV // HEADS_PER_COL


def _pair_bias_kernel(tbl_ref, bkt_ref, o_ref):
    for e in range(2):
        bkt = bkt_ref[e]
        for k in range(N_KV_HEADS):
            for col in range(COLS_PER_KV):
                for half in range(HEADS_PER_COL):
                    head = k * Q_PER_KV + col * HEADS_PER_COL + half
                    acc = jnp.full(bkt.shape, NEG_INF, F32)
                    for b in range(NUM_BUCKETS):
                        acc = jnp.where(bkt == b, tbl_ref[b, head], acc)
                    r0 = (e * COLS_PER_KV + col) * CHUNK
                    o_ref[k, r0:r0 + CHUNK, half * PAIR_BAND:(half + 1) * PAIR_BAND] = acc


def _pair_bias(rel_table):
    qi = jnp.arange(CHUNK, dtype=jnp.int32)[None, :, None]
    kj = jnp.arange(PAIR_BAND, dtype=jnp.int32)[None, None, :]
    e = jnp.arange(2, dtype=jnp.int32)[:, None, None]
    rel = (kj - WINDOW) - (e * CHUNK + qi)
    key_chunk = kj // CHUNK - e
    seen = (key_chunk >= 0) & (key_chunk <= WIN_CHUNKS)
    bkt = jnp.where(seen, _t5_bucket(rel), -1)
    rows = 2 * COLS_PER_KV * CHUNK
    return pl.pallas_call(
        _pair_bias_kernel,
        in_specs=[pl.BlockSpec(memory_space=pltpu.SMEM),
                  pl.BlockSpec((2, CHUNK, PAIR_BAND), lambda: (0, 0, 0))],
        out_specs=pl.BlockSpec((N_KV_HEADS, rows, HEADS_PER_COL * PAIR_BAND), lambda: (0, 0, 0)),
        out_shape=jax.ShapeDtypeStruct((N_KV_HEADS, rows, HEADS_PER_COL * PAIR_BAND), F32),
        name="pair_bias",
    )(rel_table.astype(F32), bkt)


def _attend(qc, kk_all, vv_all, bias_ref, sink_ref, mask_thr):
    nq = qc.shape[0]
    nk = kk_all.shape[0]
    pieces = []
    for k in range(N_KV_HEADS):
        qs = jnp.concatenate(
            [qc[:, (k * Q_PER_KV + g) * HEAD_DIM:(k * Q_PER_KV + g + 1) * HEAD_DIM]
             for g in range(Q_PER_KV)], axis=0)
        kk = kk_all[:, k * HEAD_DIM:(k + 1) * HEAD_DIM]
        vv = vv_all[:, k * HEAD_DIM:(k + 1) * HEAD_DIM]
        lg = lax.dot_general(qs, kk, (((1,), (1,)), ((), ())),
                             preferred_element_type=F32) * (HEAD_DIM ** -0.5) + bias_ref[k]
        if mask_thr is not None:
            col = lax.broadcasted_iota(jnp.int32, (Q_PER_KV * nq, nk), 1)
            lg = jnp.where(col < mask_thr, NEG_INF, lg)
        s = sink_ref[k]
        m = jnp.maximum(jnp.max(lg, axis=-1, keepdims=True), s)
        p = jnp.exp(lg - m)
        den = jnp.sum(p, axis=-1, keepdims=True) + jnp.exp(s - m)
        o = jnp.dot(p.astype(BF16), vv, preferred_element_type=F32) / den
        pieces.extend(o[g * nq:(g + 1) * nq, :] for g in range(Q_PER_KV))
    return jnp.concatenate(pieces, axis=1)


def _attn_prompt_kernel(q_ref, kv_ref, halo_ref, bias_ref, sink_ref, o_ref, ka, kb, va, vb, *, pairs):
    t = pl.program_id(1)
    kvcat = jnp.concatenate([halo_ref[...], kv_ref[...]], axis=0)
    low = lax.broadcasted_iota(jnp.int32, (kvcat.shape[0], LANES), 1) < HEAD_DIM
    zero = jnp.zeros((kvcat.shape[0], LANES), BF16)
    ones_low = jnp.where(low, 1.0, 0.0).astype(BF16)
    ones_high = jnp.where(low, 0.0, 1.0).astype(BF16)
    kv_cols = KV_DIM // LANES
    for col in range(2 * kv_cols):
        x = kvcat[:, col * LANES:(col + 1) * LANES]
        xs = jnp.concatenate([x[:, HEAD_DIM:], x[:, :HEAD_DIM]], axis=1)
        dst_a, dst_b = (ka, kb) if col < kv_cols else (va, vb)
        k0 = (col % kv_cols) * HEADS_PER_COL
        dst_a[k0, :, 0:LANES] = jnp.where(low, x, zero)
        dst_b[k0, :, 0:LANES] = jnp.where(low, zero, xs)
        dst_a[k0 + 1, :, 0:LANES] = jnp.where(low, xs, zero)
        dst_b[k0 + 1, :, 0:LANES] = jnp.where(low, zero, x)
    for k in range(N_KV_HEADS):
        va[k, :, LANES:2 * LANES] = ones_low
        vb[k, :, LANES:2 * LANES] = ones_high

    out_low = lax.broadcasted_iota(jnp.int32, (2 * COLS_PER_KV * CHUNK, LANES), 1) < HEAD_DIM
    for p in range(pairs):
        r0 = p * 2 * CHUNK
        for k in range(N_KV_HEADS):
            lhs = jnp.concatenate(
                [q_ref[r0 + e * CHUNK:r0 + (e + 1) * CHUNK, (k * COLS_PER_KV + c) * LANES:(k * COLS_PER_KV + c + 1) * LANES]
                 for e in range(2) for c in range(COLS_PER_KV)], axis=0)
            keys = jnp.concatenate([ka[k, r0:r0 + PAIR_BAND, :], kb[k, r0:r0 + PAIR_BAND, :]], axis=0)
            lg = lax.dot_general(lhs, keys, (((1,), (1,)), ((), ())),
                                 preferred_element_type=F32) * (HEAD_DIM ** -0.5) + bias_ref[k]
            probs, sink_terms = [], []
            for half in range(HEADS_PER_COL):
                seg = lg[:, half * PAIR_BAND:(half + 1) * PAIR_BAND]
                if p == 0:
                    kcol = lax.broadcasted_iota(jnp.int32, seg.shape, 1)
                    seg = jnp.where(kcol < jnp.where(t == 0, WINDOW, 0), NEG_INF, seg)
                s = sink_ref[k, half]
                folded = seg[:, :LANES]
                for j in range(1, PAIR_BAND // LANES):
                    folded = jnp.maximum(folded, seg[:, j * LANES:(j + 1) * LANES])
                m = jnp.maximum(jnp.max(folded, axis=-1, keepdims=True), s)
                pr = jnp.exp(seg - jnp.concatenate([m] * (PAIR_BAND // LANES), axis=1))
                sink_terms.append(jnp.exp(s - m))
                probs.append(pr.astype(BF16))
            vals = jnp.concatenate([va[k, r0:r0 + PAIR_BAND, :], vb[k, r0:r0 + PAIR_BAND, :]], axis=0)
            oe = jnp.dot(jnp.concatenate(probs, axis=1), vals, preferred_element_type=F32)
            o = oe[:, :LANES] / (oe[:, LANES:] + jnp.where(out_low, sink_terms[0], sink_terms[1]))
            for e in range(2):
                for c in range(COLS_PER_KV):
                    rr = (e * COLS_PER_KV + c) * CHUNK
                    o_ref[r0 + e * CHUNK:r0 + (e + 1) * CHUNK,
                          (k * COLS_PER_KV + c) * LANES:(k * COLS_PER_KV + c + 1) * LANES] = (
                        o[rr:rr + CHUNK, :].astype(o_ref.dtype))


def _attn_prompt(q, kv, bias, sink_rows, n_prompt, seq, pairs):
    rows = pairs * 2 * CHUNK
    tiles = seq // rows
    halo_per_tile = rows // WINDOW
    keys = WINDOW + rows
    return pl.pallas_call(
        functools.partial(_attn_prompt_kernel, pairs=pairs),
        grid=(n_prompt // seq, tiles),
        scratch_shapes=[pltpu.VMEM((N_KV_HEADS, keys, LANES), BF16) for _ in range(2)]
        + [pltpu.VMEM((N_KV_HEADS, keys, 2 * LANES), BF16) for _ in range(2)],
        in_specs=[pl.BlockSpec((rows, ATTN_DIM), lambda b, t: (b * tiles + t, 0)),
                  pl.BlockSpec((rows, 2 * KV_DIM), lambda b, t: (b * tiles + t, 0)),
                  pl.BlockSpec((WINDOW, 2 * KV_DIM),
                               lambda b, t: (jnp.maximum((b * tiles + t) * halo_per_tile - 1, 0), 0)),
                  pl.BlockSpec(bias.shape, lambda b, t: (0, 0, 0)),
                  pl.BlockSpec(sink_rows.shape, lambda b, t: (0, 0, 0, 0))],
        out_specs=pl.BlockSpec((rows, ATTN_DIM), lambda b, t: (b * tiles + t, 0)),
        out_shape=jax.ShapeDtypeStruct((n_prompt, ATTN_DIM), BF16),
        compiler_params=_cparams("arbitrary", "arbitrary"),
        name="attn_prompt",
    )(q, kv, kv, bias, sink_rows)


def _attn_sample_kernel(q_ref, kv_ref, ck_ref, cv_ref, bias_ref, sink_ref, o_ref):
    kv = kv_ref[...]
    kk = jnp.concatenate([ck_ref[0].astype(BF16), kv[:, :KV_DIM]], axis=0)
    vv = jnp.concatenate([cv_ref[0].astype(BF16), kv[:, KV_DIM:]], axis=0)
    o = _attend(q_ref[...], kk, vv, bias_ref, sink_ref, None)
    o_ref[...] = o.astype(o_ref.dtype)


def _attn_sample(q, kv, cache_k, cache_v, bias, sink_rows, n_prompt, t_new):
    dec_batch, buf, _ = cache_k.shape
    first = n_prompt // t_new
    return pl.pallas_call(
        _attn_sample_kernel,
        grid=(dec_batch,),
        in_specs=[pl.BlockSpec((t_new, ATTN_DIM), lambda s: (first + s, 0)),
                  pl.BlockSpec((t_new, 2 * KV_DIM), lambda s: (first + s, 0)),
                  pl.BlockSpec((1, buf, KV_DIM), lambda s: (s, 0, 0)),
                  pl.BlockSpec((1, buf, KV_DIM), lambda s: (s, 0, 0)),
                  pl.BlockSpec(bias.shape, lambda s: (0, 0, 0)),
                  pl.BlockSpec(sink_rows.shape, lambda s: (0, 0, 0))],
        out_specs=pl.BlockSpec((t_new, ATTN_DIM), lambda s: (s, 0)),
        out_shape=jax.ShapeDtypeStruct((dec_batch * t_new, ATTN_DIM), BF16),
        compiler_params=_cparams("arbitrary"),
        name="attn_sample",
    )(q, kv, cache_k, cache_v, bias, sink_rows)


def _route(lt, t, counted, eidx_ref, wtok_ref, rank_ref, cnt_ref, cnt_scr):
    gl = [lt[r:r + 1, :] for r in range(N_GROUPS)]
    gmax = gl[0]
    gsel = jnp.zeros(gl[0].shape, jnp.int32)
    for r in range(1, N_GROUPS):
        better = gl[r] > gmax
        gsel = jnp.where(better, r, gsel)
        gmax = jnp.maximum(gmax, gl[r])
    gexp = [jnp.exp(v - gmax) for v in gl]
    gsum = gexp[0]
    for r in range(1, N_GROUPS):
        gsum = gsum + gexp[r]
    psel = jnp.zeros(gl[0].shape, F32)
    for r in range(N_GROUPS):
        psel = jnp.where(gsel == r, gexp[r] / gsum, psel)
    el = jnp.zeros((EXPERTS_PER_GROUP, lt.shape[1]), F32)
    for r in range(N_GROUPS):
        lo = EXPERT_ROW0 + r * EXPERTS_PER_GROUP
        el = jnp.where(gsel == r, lt[lo:lo + EXPERTS_PER_GROUP, :], el)
    emax = jnp.max(el, axis=0, keepdims=True)
    ee = jnp.exp(el - emax)
    pin = ee / jnp.sum(ee, axis=0, keepdims=True)
    idx = lax.broadcasted_iota(jnp.int32, pin.shape, 0)
    p1 = jnp.max(pin, axis=0, keepdims=True)
    i1 = jnp.min(jnp.where(pin == p1, idx, EXPERTS_PER_GROUP), axis=0, keepdims=True)
    rest = jnp.where(idx == i1, -1.0, pin)
    p2 = jnp.max(rest, axis=0, keepdims=True)
    i2 = jnp.min(jnp.where(rest == p2, idx, EXPERTS_PER_GROUP), axis=0, keepdims=True)
    tot = p1 + p2
    e1 = gsel * EXPERTS_PER_GROUP + i1
    e2 = gsel * EXPERTS_PER_GROUP + i2
    eidx_ref[t, 0:1, :] = e1
    eidx_ref[t, 1:2, :] = e2
    wtok_ref[t, 0:1, :] = psel * p1 / tot
    wtok_ref[t, 1:2, :] = psel * p2 / tot

    rows = lt.shape[1]
    eiota = lax.broadcasted_iota(jnp.int32, (N_EXPERTS, rows), 0)
    oh1 = (eiota == e1).astype(F32)
    oh2 = (eiota == e2).astype(F32)
    both = oh1 + oh2
    before = (lax.broadcasted_iota(jnp.int32, (rows, rows), 0)
              < lax.broadcasted_iota(jnp.int32, (rows, rows), 1)).astype(BF16)
    prior = jnp.dot(both.astype(BF16), before, preferred_element_type=F32) + cnt_scr[:, 0:1]
    rank_ref[t, 0:1, :] = jnp.sum(oh1 * prior, axis=0, keepdims=True).astype(jnp.int32)
    rank_ref[t, 1:2, :] = jnp.sum(oh2 * prior, axis=0, keepdims=True).astype(jnp.int32)
    cnt_scr[...] = cnt_scr[...] + jnp.where(counted, jnp.sum(both, axis=1, keepdims=True), 0.0)
    cnt_ref[...] = cnt_scr[...]


def _post_kernel(xp_ref, xs_ref, u_ref, uhalo_ref, sconv_ref, op_ref, os_ref, gates_ref,
                 gate1_ref, shift2_ref, scale2_ref,
                 cw_ref, cb_ref, lng_ref, lnb_ref, wpw2_ref, bpw2_ref, wao_ref, wout_ref,
                 n2_ref, wrt_ref, brt_ref,
                 x1_ref, h2_ref, eidx_ref, wtok_ref, rank_ref, cnt_ref,
                 uext, shift_scr, d_scr, s_scr, o_scr, mix_scr, h2b_scr, cnt_scr,
                 *, groups, prompt_tiles, tiles_per_seq, tiles):
    i = pl.program_id(0)
    d_model = xp_ref.shape[1]

    @pl.when(i == 0)
    def _():
        cnt_scr[...] = jnp.zeros(cnt_scr.shape, F32)
        h2b_scr[...] = jnp.zeros(h2b_scr.shape, BF16)
    nch = cw_ref.shape[0]
    cw = uext.shape[2]
    hist = uext.shape[1] - GROUP_ROWS
    lead = hist - (CONV_WIDTH - 1)

    def put_hist(g, rows):
        for c in range(nch):
            uext[g * nch + c, 0:hist, :] = rows[:, c * cw:(c + 1) * cw]

    @pl.when(i < prompt_tiles)
    def _():
        first = (i % tiles_per_seq) == 0
        put_hist(0, jnp.where(first, 0.0, uhalo_ref[...]))
        for g in range(1, groups):
            put_hist(g, u_ref[(g - 1) * GROUP_ROWS:g * GROUP_ROWS, :])
        o_scr[...] = op_ref[...]

    @pl.when(i >= prompt_tiles)
    def _():
        for g in range(groups):
            put_hist(g, sconv_ref[g])
        o_scr[...] = os_ref[...]

    for g in range(groups):
        for c in range(nch):
            uext[g * nch + c, hist:hist + GROUP_ROWS, :] = (
                u_ref[g * GROUP_ROWS:(g + 1) * GROUP_ROWS, c * cw:(c + 1) * cw])

    span = shift_scr.shape[1]

    def conv_chunk(k, carry):
        c = k % nch
        win_all = uext[k]
        for s in range(1, SUBLANES):
            shift_scr[s, :, :] = pltpu.roll(win_all, win_all.shape[0] - s, 0)[0:span, :]
        acc = None
        for j in range(CONV_WIDTH):
            base, s = divmod(lead + j, SUBLANES)
            rows = pl.ds(base * SUBLANES, GROUP_ROWS)
            win = uext[k, rows, :] if s == 0 else shift_scr[s, rows, :]
            term = win * cw_ref[c, j:j + 1, :]
            acc = term if acc is None else acc + term
        d_scr[k] = acc + cb_ref[c]
        return carry

    lax.fori_loop(0, groups * nch, conv_chunk, 0)

    for g in range(groups):
        dd = jnp.concatenate([d_scr[g * nch + c] for c in range(nch)], axis=1)
        mu = jnp.mean(dd, axis=-1, keepdims=True)
        var = jnp.mean(jnp.square(dd - mu), axis=-1, keepdims=True)
        y = (dd - mu) * lax.rsqrt(var + LN_EPS) * lng_ref[...] + lnb_ref[...]
        s_scr[g * GROUP_ROWS:(g + 1) * GROUP_ROWS, :] = (y * jax.nn.sigmoid(y)).astype(BF16)

    def route_tile(t, counted):
        lt = lax.dot_general(wrt_ref[...], h2b_scr[...], (((1,), (1,)), ((), ())),
                             preferred_element_type=F32) + brt_ref[...]
        _route(lt, t, counted, eidx_ref, wtok_ref, rank_ref, cnt_ref, cnt_scr)

    route_tile(jnp.maximum(i - 1, 0), i > 0)

    conv_out = jnp.dot(s_scr[...], wpw2_ref[...], preferred_element_type=F32) + bpw2_ref[...]
    attn_out = jnp.dot(o_scr[...], wao_ref[...], preferred_element_type=F32)
    merged = (gates_ref[:, :d_model].astype(F32) * attn_out
              + gates_ref[:, d_model:].astype(F32) * conv_out)
    mix_scr[...] = jnp.dot(merged.astype(BF16), wout_ref[...], preferred_element_type=F32)

    def residual(x_ref):
        def res_group(g, carry):
            r = pl.multiple_of(g * GROUP_ROWS, GROUP_ROWS)
            x1 = (x_ref[pl.ds(r, GROUP_ROWS), :]
                  + gate1_ref[pl.ds(g, 1), :] * mix_scr[pl.ds(r, GROUP_ROWS), :])
            x1_ref[pl.ds(r, GROUP_ROWS), :] = x1
            h = _rms_mod(x1, n2_ref[...], scale2_ref[pl.ds(g, 1), :], shift2_ref[pl.ds(g, 1), :])
            hb = h.astype(BF16)
            h2b_scr[pl.ds(r, GROUP_ROWS), :] = hb
            h2_ref[pl.ds(r, GROUP_ROWS), :] = _pack_bf16_pairs(hb)
            return carry

        lax.fori_loop(0, groups, res_group, 0, unroll=4)

    pl.when(i < prompt_tiles)(lambda: residual(xp_ref))
    pl.when(i >= prompt_tiles)(lambda: residual(xs_ref))

    @pl.when(i == tiles - 1)
    def _():
        route_tile(i, True)


def _post(x_p, x_s, u, sconv_pad, o_p, o_s, gates, gate1_g, shift2_g, scale2_g, p, n_prompt, seq, tm):
    d = x_p.shape[1]
    n = x_p.shape[0] + x_s.shape[0]
    cdim = u.shape[1]
    nch = cdim // CONV_CHUNK
    groups = tm // GROUP_ROWS
    tiles = n // tm
    prompt_tiles = n_prompt // tm
    const = lambda shape: pl.BlockSpec(shape, lambda i: (0,) * len(shape))
    row = lambda w: pl.BlockSpec((tm, w), lambda i: (i, 0))
    grp = pl.BlockSpec((groups, d), lambda i: (i, 0))
    outs = pl.pallas_call(
        functools.partial(_post_kernel, groups=groups, prompt_tiles=prompt_tiles,
                          tiles_per_seq=seq // tm, tiles=tiles),
        grid=(tiles,),
        in_specs=_two_stream_specs(tm, d, prompt_tiles) + [
                  row(cdim),
                  pl.BlockSpec((GROUP_ROWS, cdim), lambda i: (jnp.maximum(i * groups - 1, 0), 0)),
                  pl.BlockSpec((groups, GROUP_ROWS, cdim),
                               lambda i: (jnp.maximum(i - prompt_tiles, 0), 0, 0))]
                 + _two_stream_specs(tm, ATTN_DIM, prompt_tiles) + [
                  row(2 * d), grp, grp, grp,
                  const((nch, CONV_WIDTH, CONV_CHUNK)), const((nch, 1, CONV_CHUNK)),
                  const((1, cdim)), const((1, cdim)),
                  const((cdim, d)), const((1, d)), const((ATTN_DIM, d)), const((d, d)),
                  const((1, d)), const((ROUTER_ROWS, d)), const((ROUTER_ROWS, 1))],
        out_specs=[row(d),
                   row(d // 2),
                   const((tiles, 2, tm)), const((tiles, 2, tm)), const((tiles, 2, tm)),
                   pl.BlockSpec((N_EXPERTS, LANES), lambda i: (0, 0))],
        out_shape=[jax.ShapeDtypeStruct((n, d), F32),
                   jax.ShapeDtypeStruct((n, d // 2), jnp.uint32),
                   jax.ShapeDtypeStruct((tiles, 2, tm), jnp.int32),
                   jax.ShapeDtypeStruct((tiles, 2, tm), F32),
                   jax.ShapeDtypeStruct((tiles, 2, tm), jnp.int32),
                   jax.ShapeDtypeStruct((N_EXPERTS, LANES), F32)],
        scratch_shapes=[pltpu.VMEM((groups * nch, 2 * GROUP_ROWS, CONV_CHUNK), F32),
                        pltpu.VMEM((SUBLANES, 2 * GROUP_ROWS - SUBLANES, CONV_CHUNK), F32),
                        pltpu.VMEM((groups * nch, GROUP_ROWS, CONV_CHUNK), F32),
                        pltpu.VMEM((tm, cdim), BF16),
                        pltpu.VMEM((tm, ATTN_DIM), BF16),
                        pltpu.VMEM((tm, d), F32),
                        pltpu.VMEM((tm, d), BF16),
                        pltpu.VMEM((N_EXPERTS, LANES), F32)],
        compiler_params=_cparams("arbitrary"),
        name="post",
    )(x_p, x_s, u, u, sconv_pad, o_p, o_s, gates, gate1_g, shift2_g, scale2_g,
      p["conv_w"], p["conv_b"], p["ln_g"], p["ln_b"], p["w_pw2"], p["b_pw2"], p["w_ao"], p["w_out"],
      p["n2"], p["w_rt"], p["b_rt"])
    x1, h2, eidx, wtok, rank, cnt = outs
    per_slot = lambda a: a.transpose(1, 0, 2).reshape(2, n)
    return x1, h2, per_slot(eidx), per_slot(wtok), per_slot(rank), cnt


TABLE_GROUP = 8
GATHER_AHEAD = 8
FINAL_AHEAD = 1


def _grouped_tables(tab, group=TABLE_GROUP, ahead=1):
    steps, w = tab.shape
    assert steps % group == 0
    padded = jnp.concatenate([tab] + [tab[-1:]] * ahead, axis=0)
    cols = [tab.reshape(steps // group, group * w)] + [padded[group + a::group][:steps // group]
                                                         for a in range(ahead)]
    return jnp.concatenate(cols, axis=1).reshape(steps // group, 1, (group + ahead) * w)

def _moe_kernel(blk_e_ref, next_e_ref, nused_ref, src_ref, roww_ref,
                h2_hbm, wg_hbm, wu_hbm, wd_hbm, y_ref,
                xbuf, x16, wg_f, wu_f, wd_f, wg_b, wu_b, wd_b, gsem, wsem):
    i = pl.program_id(0)
    nused = nused_ref[0]
    nslots = GATHER_AHEAD + 1
    slot = i % nslots
    base = (i % TABLE_GROUP) * MOE_BLOCK
    weights = ((wg_hbm, wg_f, wg_b), (wu_hbm, wu_f, wu_b), (wd_hbm, wd_f, wd_b))

    def gather(first, sl):
        for r in range(MOE_BLOCK):
            pltpu.make_async_copy(h2_hbm.at[pl.ds(src_ref[0, 0, first + r], 1), :],
                                  xbuf.at[sl, pl.ds(r, 1), :], gsem.at[sl]).start(priority=0)

    def gather_wait(sl):
        pltpu.make_async_copy(h2_hbm.at[pl.ds(0, MOE_BLOCK), :], xbuf.at[sl], gsem.at[sl]).wait()

    def weights_start(e):
        for hbm, f32_buf, _ in weights:
            pltpu.make_async_copy(hbm.at[e], f32_buf, wsem.at[0]).start(priority=1)

    def weights_wait():
        for hbm, f32_buf, _ in weights:
            pltpu.make_async_copy(hbm.at[0], f32_buf, wsem.at[0]).wait()

    @pl.when(i == 0)
    def _():
        weights_start(blk_e_ref[0])
        for a in range(GATHER_AHEAD):
            gather(base + a * MOE_BLOCK, a)

    @pl.when(i >= nused)
    def _():
        y_ref[...] = jnp.zeros(y_ref.shape, y_ref.dtype)

    @pl.when(i < nused)
    def _():
        changed = jnp.logical_or(i == 0, blk_e_ref[i] != blk_e_ref[jnp.maximum(i - 1, 0)])

        @pl.when(changed)
        def _():
            weights_wait()
            for _, f32_buf, b16_buf in weights:
                b16_buf[...] = f32_buf[...].astype(BF16)
            weights_start(next_e_ref[i])

        gather_wait(slot)
        x_lo, x_hi = _unpack_bf16_pairs(xbuf[slot])
        x16[:, :x_lo.shape[1]] = x_lo.astype(BF16)
        x16[:, x_lo.shape[1]:] = x_hi.astype(BF16)

        x = x16[...]
        hg = jnp.dot(x, wg_b[...], preferred_element_type=F32)
        hu = jnp.dot(x, wu_b[...], preferred_element_type=F32)
        hid = (hg * jax.nn.sigmoid(hg) * hu).astype(BF16)
        y = jnp.dot(hid, wd_b[...], preferred_element_type=F32) * roww_ref[...]
        gather(base + GATHER_AHEAD * MOE_BLOCK, (i + GATHER_AHEAD) % nslots)
        y_ref[...] = _pack_bf16_pairs(y.astype(BF16))

        @pl.when(i == nused - 1)
        def _():
            weights_wait()
            for a in range(1, GATHER_AHEAD + 1):
                gather_wait((i + a) % nslots)


def _moe(h2, blk_e, next_e, nused, src, roww, w_gate, w_up, w_down):
    n_blocks = blk_e.shape[0]
    n_exp, d, de = w_gate.shape
    src_tables = _grouped_tables(src.reshape(n_blocks, MOE_BLOCK), ahead=GATHER_AHEAD)
    grid_spec = pltpu.PrefetchScalarGridSpec(
        num_scalar_prefetch=3,
        grid=(n_blocks,),
        in_specs=[pl.BlockSpec((1, 1, src_tables.shape[2]), lambda i, be, ne, nu: (i // TABLE_GROUP, 0, 0),
                               memory_space=pltpu.SMEM),
                  pl.BlockSpec((MOE_BLOCK, 1), lambda i, be, ne, nu: (i, 0)),
                  pl.BlockSpec(memory_space=pl.ANY),
                  pl.BlockSpec(memory_space=pl.ANY),
                  pl.BlockSpec(memory_space=pl.ANY),
                  pl.BlockSpec(memory_space=pl.ANY)],
        out_specs=pl.BlockSpec((MOE_BLOCK, d // 2), lambda i, be, ne, nu: (i, 0)),
        scratch_shapes=[pltpu.VMEM((GATHER_AHEAD + 1, MOE_BLOCK, d // 2), jnp.uint32),
                        pltpu.VMEM((MOE_BLOCK, d), BF16),
                        pltpu.VMEM((d, de), F32), pltpu.VMEM((d, de), F32), pltpu.VMEM((de, d), F32),
                        pltpu.VMEM((d, de), BF16), pltpu.VMEM((d, de), BF16), pltpu.VMEM((de, d), BF16),
                        pltpu.SemaphoreType.DMA((GATHER_AHEAD + 1,)), pltpu.SemaphoreType.DMA((1,))],
    )
    return pl.pallas_call(
        _moe_kernel,
        grid_spec=grid_spec,
        out_shape=jax.ShapeDtypeStruct((n_blocks * MOE_BLOCK, d // 2), jnp.uint32),
        compiler_params=_cparams("arbitrary"),
        name="moe",
    )(blk_e, next_e, nused, src_tables, roww, h2, w_gate, w_up, w_down)


def _dispatch(eidx, wtok, rank, counts, n):
    a_tot = 2 * n
    experts = jnp.arange(N_EXPERTS, dtype=jnp.int32)
    padded = (counts + MOE_BLOCK - 1) // MOE_BLOCK * MOE_BLOCK
    pad_end = jnp.sum(jnp.where(experts[None, :] <= experts[:, None], padded[None, :], 0), axis=1)
    pad_start = pad_end - padded
    dest = jnp.sum(jnp.where(eidx[:, :, None] == experts, pad_start, 0), axis=-1) + rank
    n_blocks = -(-(a_tot + N_EXPERTS * (MOE_BLOCK - 1)) // MOE_BLOCK)
    n_blocks = -(-n_blocks // TABLE_GROUP) * TABLE_GROUP
    n_rows = n_blocks * MOE_BLOCK
    tok = jnp.tile(jnp.arange(n, dtype=jnp.int32), 2)
    upd = jnp.stack([tok, lax.bitcast_convert_type(wtok.reshape(-1), jnp.int32)], axis=1)
    rows = jnp.zeros((n_rows, 2), jnp.int32).at[dest.reshape(-1)].set(
        upd, unique_indices=True, mode="promise_in_bounds")
    blk_start = jnp.arange(n_blocks, dtype=jnp.int32) * MOE_BLOCK
    blk_e = jnp.minimum(jnp.sum((pad_end[None, :] <= blk_start[:, None]).astype(jnp.int32), axis=1),
                        N_EXPERTS - 1)
    nused = (pad_end[-1] // MOE_BLOCK).reshape(1)
    roww = lax.bitcast_convert_type(rows[:, 1], F32)
    later = jnp.where((counts > 0)[None, :] & (experts[None, :] > experts[:, None]), experts[None, :], N_EXPERTS)
    next_of = jnp.min(later, axis=1)
    next_of = jnp.where(next_of < N_EXPERTS, next_of, experts)
    next_e = jnp.sum(jnp.where(blk_e[:, None] == experts[None, :], next_of[None, :], 0), axis=1)
    return blk_e, next_e, nused, rows[:, 0].reshape(n_blocks, 1, MOE_BLOCK), roww.reshape(n_rows, 1), dest


def _final_kernel(d0_ref, d1_ref, x1_ref, gate2_ref, fg_ref, yb_hbm,
                  op_ref, os_ref, ybuf, sem, *, groups, prompt_tiles, tiles, table_group):
    i = pl.program_id(0)
    nslots = FINAL_AHEAD + 1
    slot = i % nslots
    tm = x1_ref.shape[0]
    base = (i % table_group) * tm

    def gather(first, sl):
        for r in range(tm):
            pltpu.make_async_copy(yb_hbm.at[pl.ds(d0_ref[0, 0, first + r], 1), :],
                                  ybuf.at[sl, 0, pl.ds(r, 1), :], sem.at[sl]).start(priority=0)
            pltpu.make_async_copy(yb_hbm.at[pl.ds(d1_ref[0, 0, first + r], 1), :],
                                  ybuf.at[sl, 1, pl.ds(r, 1), :], sem.at[sl]).start(priority=1)

    def gather_wait(sl):
        for k in range(2):
            pltpu.make_async_copy(yb_hbm.at[pl.ds(0, tm), :], ybuf.at[sl, k], sem.at[sl]).wait()

    @pl.when(i == 0)
    def _():
        for a in range(FINAL_AHEAD):
            gather(base + a * tm, a)

    gather_wait(slot)
    gather(base + FINAL_AHEAD * tm, (i + FINAL_AHEAD) % nslots)

    def run(o_ref):
        def body(g, carry):
            r = pl.multiple_of(g * GROUP_ROWS, GROUP_ROWS)
            lo0, hi0 = _unpack_bf16_pairs(ybuf[slot, 0, pl.ds(r, GROUP_ROWS), :])
            lo1, hi1 = _unpack_bf16_pairs(ybuf[slot, 1, pl.ds(r, GROUP_ROWS), :])
            moe = jnp.concatenate([lo0 + lo1, hi0 + hi1], axis=1)
            x2 = x1_ref[pl.ds(r, GROUP_ROWS), :] + gate2_ref[pl.ds(g, 1), :] * moe
            ms = jnp.mean(x2 * x2, axis=-1, keepdims=True)
            o_ref[pl.ds(r, GROUP_ROWS), :] = x2 * lax.rsqrt(ms + RMS_EPS) * fg_ref[...]
            return carry

        lax.fori_loop(0, groups, body, 0, unroll=2)

    pl.when(i < prompt_tiles)(lambda: run(op_ref))
    pl.when(i >= prompt_tiles)(lambda: run(os_ref))

    @pl.when(i == tiles - 1)
    def _():
        for a in range(1, FINAL_AHEAD + 1):
            gather_wait((i + a) % nslots)


def _final(x1, yb, dest, gate2_g, final_g, n_prompt, tm):
    n, d = x1.shape
    groups = tm // GROUP_ROWS
    tiles = n // tm
    prompt_tiles = n_prompt // tm
    table_group = max(g for g in range(1, TABLE_GROUP + 1) if tiles % g == 0)
    tables = [_grouped_tables(dest[k].reshape(tiles, tm), table_group, FINAL_AHEAD) for k in range(2)]
    idx_spec = pl.BlockSpec((1, 1, tables[0].shape[2]), lambda i: (i // table_group, 0, 0),
                            memory_space=pltpu.SMEM)
    return pl.pallas_call(
        functools.partial(_final_kernel, groups=groups, prompt_tiles=prompt_tiles, tiles=tiles,
                          table_group=table_group),
        grid=(tiles,),
        in_specs=[idx_spec, idx_spec,
                  pl.BlockSpec((tm, d), lambda i: (i, 0)),
                  pl.BlockSpec((groups, d), lambda i: (i, 0)),
                  pl.BlockSpec((1, d), lambda i: (0, 0)),
                  pl.BlockSpec(memory_space=pl.ANY)],
        out_specs=_two_stream_specs(tm, d, prompt_tiles),
        out_shape=[jax.ShapeDtypeStruct((n_prompt, d), F32),
                   jax.ShapeDtypeStruct((n - n_prompt, d), F32)],
        scratch_shapes=[pltpu.VMEM((FINAL_AHEAD + 1, 2, tm, d // 2), jnp.uint32),
                        pltpu.SemaphoreType.DMA((FINAL_AHEAD + 1,))],
        compiler_params=_cparams("arbitrary"),
        name="final",
    )(tables[0], tables[1], x1, gate2_g, final_g.reshape(1, d), yb)


def _layer(x_p, x_s, c_all, cache_k, cache_v, sconv, rel_table, lp, seq, t_new):
    (n1, n2, w_ada, b_ada, w_in, sink, w_ao, conv_w, conv_b, ln_g, ln_b, w_pw2, b_pw2, w_out,
     w_grp, b_grp, w_rt, b_rt, w_gate, w_up, w_down) = lp
    n_prompt, d = x_p.shape
    n = n_prompt + x_s.shape[0]
    batch = n_prompt // seq
    dec_batch = (n - n_prompt) // t_new
    cdim = conv_w.shape[1]
    tm = TOKEN_TILE

    c_rows = -(-c_all.shape[0] // 8) * 8
    c_pad = jnp.pad(c_all, ((0, c_rows - c_all.shape[0]), (0, 0)))
    mods = _ada(c_pad, w_ada, b_ada)

    def per_group(k):
        m = mods[:, k * d:(k + 1) * d]
        mp = jnp.broadcast_to(m[:batch, None, :], (batch, seq // GROUP_ROWS, d))
        return jnp.concatenate([mp.reshape(batch * (seq // GROUP_ROWS), d), m[batch:batch + dec_batch]],
                               axis=0)

    shift1, scale1, gate1, shift2, scale2, gate2 = [per_group(k) for k in range(6)]

    h1 = _norm1(x_p, x_s, n1, scale1, shift1, NORM_ROWS)

    k_off = ATTN_DIM
    glu_off = k_off + 2 * KV_DIM
    gate_off = glu_off + 2 * cdim
    w_in_b = w_in.astype(BF16)
    tglu = PROJ_COLS // 2
    w_glu = jnp.stack([w_in_b[:, glu_off:glu_off + cdim].reshape(d, cdim // tglu, tglu),
                       w_in_b[:, glu_off + cdim:gate_off].reshape(d, cdim // tglu, tglu)],
                      axis=2).reshape(d, 2 * cdim)
    tmm = PROJ_ROWS
    (q,) = _proj(h1, w_in_b[:, :k_off], "qkv", tmm, PROJ_COLS, [BF16])
    (kv,) = _proj(h1, w_in_b[:, k_off:glu_off], "qkv", tmm, 2 * KV_DIM, [BF16])
    kv_state = _kv_state(h1, w_in_b[:, k_off:glu_off], n_prompt, seq)
    (u,) = _proj(h1, w_glu, "glu", tmm, 2 * tglu, [F32])
    (gates,) = _proj(h1, w_in_b[:, gate_off:], "gate", tmm, PROJ_COLS, [BF16])

    bias_p = _pair_bias(rel_table)
    buf = cache_k.shape[1]
    kpos = jnp.concatenate([jnp.arange(buf, dtype=jnp.int32) - buf, jnp.arange(t_new, dtype=jnp.int32)])
    bias_s = _rel_bias(rel_table, kpos[None, :] - jnp.arange(t_new, dtype=jnp.int32)[:, None])
    sink_f = sink.astype(F32).reshape(N_KV_HEADS, Q_PER_KV, 1, 1)
    sink_kch = sink.astype(F32).reshape(N_KV_HEADS, 1, COLS_PER_KV, HEADS_PER_COL).transpose(0, 3, 1, 2)
    sink_p = jnp.broadcast_to(sink_kch[..., None, None],
                              (N_KV_HEADS, HEADS_PER_COL, 2, COLS_PER_KV, CHUNK, LANES)).reshape(
        N_KV_HEADS, HEADS_PER_COL, 2 * COLS_PER_KV * CHUNK, LANES)
    sink_s = jnp.broadcast_to(sink_f, (N_KV_HEADS, Q_PER_KV, t_new, 1)).reshape(N_KV_HEADS, Q_PER_KV * t_new, 1)
    o_p = _attn_prompt(q, kv, bias_p, sink_p, n_prompt, seq, ATTN_PAIRS)
    o_s = _attn_sample(q, kv, cache_k.reshape(dec_batch, buf, KV_DIM), cache_v.reshape(dec_batch, buf, KV_DIM),
                       bias_s, sink_s, n_prompt, t_new)

    hist = GROUP_ROWS
    sconv_pad = jnp.pad(sconv, ((0, 0), (hist - sconv.shape[1], 0), (0, 0)))
    w_rt_t = jnp.zeros((ROUTER_ROWS, d), F32)
    w_rt_t = w_rt_t.at[:N_GROUPS].set(w_grp.T).at[EXPERT_ROW0:EXPERT_ROW0 + N_EXPERTS].set(w_rt.T)
    b_rt_t = jnp.zeros((ROUTER_ROWS, 1), F32)
    b_rt_t = b_rt_t.at[:N_GROUPS, 0].set(b_grp.astype(F32)).at[EXPERT_ROW0:EXPERT_ROW0 + N_EXPERTS, 0].set(
        b_rt.astype(F32))
    nch = cdim // CONV_CHUNK
    conv_w_c = conv_w.reshape(CONV_WIDTH, nch, CONV_CHUNK).transpose(1, 0, 2)
    params = dict(conv_w=conv_w_c, conv_b=conv_b.reshape(nch, 1, CONV_CHUNK), ln_g=ln_g.reshape(1, cdim),
                  ln_b=ln_b.reshape(1, cdim), w_pw2=w_pw2.astype(BF16), b_pw2=b_pw2.reshape(1, d),
                  w_ao=w_ao.astype(BF16), w_out=w_out.astype(BF16), n2=n2.reshape(1, d),
                  w_rt=w_rt_t.astype(BF16), b_rt=b_rt_t)
    x1, h2, eidx, wtok, rank, cnt = _post(x_p, x_s, u, sconv_pad, o_p, o_s, gates, gate1, shift2, scale2,
                                             params, n_prompt, seq, tm)

    blk_e, next_e, nused, src, roww, dest = _dispatch(eidx, wtok, rank, cnt[:, 0].astype(jnp.int32), n)
    yb = _moe(h2, blk_e, next_e, nused, src, roww, w_gate, w_up, w_down)
    return x1, yb, dest, gate2, kv_state, u


def kernel(x_prompt, x_sample, c_prompt, c_sample, cache_k, cache_v, state_conv, rel_bias_table, norm1_g, norm2_g, w_ada, b_ada, w_in, attn_sink, w_attn_o, conv_w, conv_b, conv_ln_g, conv_ln_b, w_pw2, b_pw2, w_out, w_group, b_group, w_router, b_router, w_gate, w_up, w_down, final_g):
    batch, seq, d = x_prompt.shape
    dec_batch, t_new, _ = x_sample.shape
    depth = norm1_g.shape[0]
    assert depth == 1, "single trunk layer"
    assert t_new == GROUP_ROWS and seq % GROUP_ROWS == 0
    n_prompt = batch * seq
    n = n_prompt + dec_batch * t_new
    x_p = x_prompt.reshape(n_prompt, d)
    x_s = x_sample.reshape(dec_batch * t_new, d)
    c_all = jnp.concatenate([c_prompt, c_sample], axis=0)
    l = 0
    lp = tuple(a.reshape(a.shape[1:]) for a in (
        norm1_g, norm2_g, w_ada, b_ada, w_in, attn_sink, w_attn_o, conv_w, conv_b, conv_ln_g, conv_ln_b,
        w_pw2, b_pw2, w_out, w_group, b_group, w_router, b_router, w_gate, w_up, w_down))
    x1, yb, dest, gate2, kv_state, u = _layer(x_p, x_s, c_all, cache_k[l], cache_v[l], state_conv[l],
                                              rel_bias_table, lp, seq, t_new)
    out_p, out_s = _final(x1, yb, dest, gate2, final_g, n_prompt, TOKEN_TILE)

    y_prompt = out_p.reshape(batch, seq, d)
    y_sample = out_s.reshape(dec_batch, t_new, d)
    cdim = u.shape[1]
    kvp = kv_state[:batch * WINDOW].reshape(batch, WINDOW, 2 * KV_DIM)
    new_k_prompt = kvp[..., :KV_DIM].reshape(1, batch, WINDOW, N_KV_HEADS, HEAD_DIM)
    new_v_prompt = kvp[..., KV_DIM:].reshape(1, batch, WINDOW, N_KV_HEADS, HEAD_DIM)
    new_conv_prompt = jnp.stack([u[(b + 1) * seq - (CONV_WIDTH - 1):(b + 1) * seq] for b in range(batch)])[None]
    kvs = kv_state[batch * WINDOW:].reshape(dec_batch, t_new, 2 * KV_DIM)
    buf = cache_k.shape[2]
    k_new = kvs[..., :KV_DIM].reshape(dec_batch, t_new, N_KV_HEADS, HEAD_DIM)
    v_new = kvs[..., KV_DIM:].reshape(dec_batch, t_new, N_KV_HEADS, HEAD_DIM)
    new_k_sample = jnp.concatenate([cache_k[l], k_new], axis=1)[:, -buf:][None]
    new_v_sample = jnp.concatenate([cache_v[l], v_new], axis=1)[:, -buf:][None]
    us = u[n_prompt:].reshape(dec_batch, t_new, cdim)
    new_conv_sample = jnp.concatenate([state_conv[l], us], axis=1)[:, -(CONV_WIDTH - 1):][None]
    return (y_prompt, y_sample, new_k_prompt, new_v_prompt, new_conv_prompt,
            new_k_sample, new_v_sample, new_conv_sample)
```
